```python
import math
import jax, jax.numpy as jnp
from jax import lax
import numpy as np

D_MODEL = 1024
BATCH = 16
SEQ = 2048
DEPTH = 2

HEAD_DIM = 64
SCALE = HEAD_DIM ** -0.5
NSA_HEADS = 8
NSA_GROUPS = 2
NSA_HPG = NSA_HEADS // NSA_GROUPS
CMP_LEN = 32
CMP_STRIDE = 16
CMP_HIDDEN = 128
SEL_BLOCK = 64
SEL_TOPN = 8
WINDOW = 256
WIN_QBLOCK = 128
MOBA_HEADS = 4
MOBA_BLOCK = 256
MOBA_TOPK = 3
MEM_LEN = 256
MEM_HEADS = 4
GATHER_QCHUNK = 32
N_BRANCHES = 3
D_FF = -(-8 * D_MODEL // (3 * 256)) * 256
REL_BUCKETS = 32
REL_MAX_DIST = 128
N_BIAS_HEADS = NSA_HEADS + MOBA_HEADS
RMS_EPS = 1e-6
NEG_INF = -1e30
FORCE_SCORE = 1e4

NSA_Q_W = NSA_HEADS * HEAD_DIM
NSA_KV_W = NSA_GROUPS * HEAD_DIM
NSA_GATE_W = NSA_HEADS * 3
MOBA_W = MOBA_HEADS * HEAD_DIM
MEM_W = MEM_HEADS * HEAD_DIM
SPLIT_SIZES = (NSA_Q_W,) + (NSA_KV_W,) * 6 + (NSA_GATE_W,) + (MOBA_W,) * 3 + (MEM_W,) + (D_MODEL,) * N_BRANCHES
IN_WIDTH = sum(SPLIT_SIZES)

kernel_name = 'hybrid_nsa_moba_memory_block'


def rms_norm(x, g):
    xf = x.astype(jnp.float32)
    y = xf * lax.rsqrt(jnp.mean(xf * xf, axis=-1, keepdims=True) + RMS_EPS)
    return (y * g.astype(jnp.float32)).astype(x.dtype)


def t5_bucket(dist):
    dist = jnp.maximum(dist, 0)
    max_exact = REL_BUCKETS // 2
    logd = jnp.log(jnp.maximum(dist, 1).astype(jnp.float32) / max_exact) / math.log(REL_MAX_DIST / max_exact)
    large = jnp.minimum(max_exact + (logd * (REL_BUCKETS - max_exact)).astype(jnp.int32), REL_BUCKETS - 1)
    return jnp.where(dist < max_exact, dist, large)


def masked_softmax(logits, mask):
    p = jax.nn.softmax(jnp.where(mask, logits, NEG_INF), axis=-1)
    return p * jnp.any(mask, axis=-1, keepdims=True)


def split_columns(t):
    points, acc = [], 0
    for size in SPLIT_SIZES[:-1]:
        acc += size
        points.append(acc)
    return jnp.split(t, points, axis=-1)


def to_heads(t, n_heads):
    b, s, _ = t.shape
    return t.reshape(b, s, n_heads, HEAD_DIM).transpose(0, 2, 1, 3)


def nsa_compressed(q, k, v, pos_k, w1_k, w2_k, pos_v, w1_v, w2_v, bias_tab):
    b, g, hpg, s, dh = q.shape
    n_cmp = (s - CMP_LEN) // CMP_STRIDE + 1
    idx = jnp.arange(n_cmp)[:, None] * CMP_STRIDE + jnp.arange(CMP_LEN)[None, :]

    def compress(t, pos, w1, w2):
        blocks = (t[:, :, idx] + pos).reshape(b, g, n_cmp, CMP_LEN * dh)
        return jax.nn.gelu(blocks @ w1) @ w2

    kc = compress(k, pos_k, w1_k, w2_k)
    vc = compress(v, pos_v, w1_v, w2_v)
    t = jnp.arange(s)
    dist = t[:, None] - (jnp.arange(n_cmp) * CMP_STRIDE + CMP_LEN - 1)[None, :]
    bias = bias_tab[t5_bucket(dist)].reshape(s, n_cmp, g, hpg).transpose(2, 3, 0, 1)
    logits = jnp.einsum('bghtd,bgcd->bghtc', q, kc).astype(jnp.float32) * SCALE + bias
    p = masked_softmax(logits, dist >= 0)
    o = jnp.einsum('bghtc,bgcd->bghtd', p.astype(vc.dtype), vc)
    return o, p


def nsa_selected(q, k, v, p_cmp, bias_tab):
    b, g, hpg, s, dh = q.shape
    n_blk = s // SEL_BLOCK
    n_cmp = p_cmp.shape[-1]
    cs = jnp.arange(n_cmp) * CMP_STRIDE
    ss = jnp.arange(n_blk) * SEL_BLOCK
    overlap = jnp.clip(jnp.minimum(cs[:, None] + CMP_LEN, ss[None, :] + SEL_BLOCK)
                       - jnp.maximum(cs[:, None], ss[None, :]), 0).astype(jnp.float32) / CMP_LEN
    imp = jnp.einsum('bghtc,cn->bgtn', p_cmp, overlap)
    t = jnp.arange(s)
    cur = (t // SEL_BLOCK)[:, None]
    blk = jnp.arange(n_blk)[None, :]
    forced = (blk == 0) | (blk == cur) | (blk == cur - 1)
    score = jnp.where(forced, FORCE_SCORE, jnp.where(blk <= cur, imp, NEG_INF))
    n_top = min(SEL_TOPN, n_blk)
    top_val, top_idx = lax.top_k(score, n_top)
    top_ok = top_val > NEG_INF / 2

    k_blk = k.reshape(b, g, n_blk, SEL_BLOCK, dh)
    v_blk = v.reshape(b, g, n_blk, SEL_BLOCK, dh)
    tab_g = bias_tab.reshape(REL_BUCKETS, g, hpg)
    qc_len = GATHER_QCHUNK
    n_chunks = s // qc_len
    bi = jnp.arange(b)[:, None, None, None]
    gi = jnp.arange(g)[None, :, None, None]
    n_keys = n_top * SEL_BLOCK

    def chunk(args):
        qc, ic, okc, t0 = args
        kg = k_blk[bi, gi, ic].reshape(b, g, qc_len, n_keys, dh)
        vg = v_blk[bi, gi, ic].reshape(b, g, qc_len, n_keys, dh)
        kpos = (ic[..., None] * SEL_BLOCK + jnp.arange(SEL_BLOCK)).reshape(b, g, qc_len, n_keys)
        dist = (t0 + jnp.arange(qc_len))[None, None, :, None] - kpos
        mask = (dist >= 0) & jnp.repeat(okc, SEL_BLOCK, axis=-1)
        bias = jnp.moveaxis(tab_g[t5_bucket(dist), gi], -1, 2)
        logits = jnp.einsum('bghqd,bgqkd->bghqk', qc, kg).astype(jnp.float32) * SCALE + bias
        p = masked_softmax(logits, mask[:, :, None])
        return jnp.einsum('bghqk,bgqkd->bghqd', p.astype(vg.dtype), vg)

    qs = jnp.moveaxis(q.reshape(b, g, hpg, n_chunks, qc_len, dh), 3, 0)
    idx_s = jnp.moveaxis(top_idx.reshape(b, g, n_chunks, qc_len, n_top), 2, 0)
    ok_s = jnp.moveaxis(top_ok.reshape(b, g, n_chunks, qc_len, n_top), 2, 0)
    t0s = jnp.arange(n_chunks, dtype=jnp.int32) * qc_len
    out = lax.map(chunk, (qs, idx_s, ok_s, t0s))
    return jnp.moveaxis(out, 0, 3).reshape(b, g, hpg, s, dh)


def nsa_window(q, k, v, bias_tab):
    b, g, hpg, s, dh = q.shape
    n_qb = s // WIN_QBLOCK
    span = WIN_QBLOCK + WINDOW
    kp = jnp.pad(k, ((0, 0), (0, 0), (WINDOW, 0), (0, 0)))
    vp = jnp.pad(v, ((0, 0), (0, 0), (WINDOW, 0), (0, 0)))
    idx = jnp.arange(n_qb)[:, None] * WIN_QBLOCK + jnp.arange(span)[None, :]
    kb = kp[:, :, idx]
    vb = vp[:, :, idx]
    qb = q.reshape(b, g, hpg, n_qb, WIN_QBLOCK, dh)
    dist = jnp.arange(WIN_QBLOCK)[:, None] - jnp.arange(span)[None, :] + WINDOW
    mask = ((dist >= 0) & (dist < WINDOW))[None] & ((idx - WINDOW) >= 0)[:, None, :]
    bias = bias_tab[t5_bucket(dist)].reshape(WIN_QBLOCK, span, g, hpg).transpose(2, 3, 0, 1)
    logits = jnp.einsum('bghnqd,bgnkd->bghnqk', qb, kb).astype(jnp.float32) * SCALE + bias[:, :, None]
    p = masked_softmax(logits, mask)
    o = jnp.einsum('bghnqk,bgnkd->bghnqd', p.astype(vb.dtype), vb)
    return o.reshape(b, g, hpg, s, dh)


def moba_attention(q, k, v, bias_tab):
    b, h, s, dh = q.shape
    n_blk = -(-s // MOBA_BLOCK)
    pad = n_blk * MOBA_BLOCK - s
    kp = jnp.pad(k, ((0, 0), (0, 0), (0, pad), (0, 0)))
    vp = jnp.pad(v, ((0, 0), (0, 0), (0, pad), (0, 0)))
    k_blk = kp.reshape(b, h, n_blk, MOBA_BLOCK, dh)
    v_blk = vp.reshape(b, h, n_blk, MOBA_BLOCK, dh)
    n_top = min(MOBA_TOPK, n_blk - 1)
    qc_len = GATHER_QCHUNK
    n_chunks = s // qc_len
    bi = jnp.arange(b)[:, None, None, None]
    hi = jnp.arange(h)[None, :, None, None]
    n_keys = n_top * MOBA_BLOCK

    def chunk(args):
        qc, t0 = args[0], args[1]
        tq = t0 + jnp.arange(qc_len)
        own = t0 // MOBA_BLOCK
        k_own = lax.dynamic_slice_in_dim(kp, own * MOBA_BLOCK, MOBA_BLOCK, axis=2)
        v_own = lax.dynamic_slice_in_dim(vp, own * MOBA_BLOCK, MOBA_BLOCK, axis=2)
        dist_own = tq[:, None] - (own * MOBA_BLOCK + jnp.arange(MOBA_BLOCK))[None, :]
        logits = (jnp.einsum('bhqd,bhkd->bhqk', qc, k_own).astype(jnp.float32) * SCALE
                  + bias_tab[t5_bucket(dist_own)].transpose(2, 0, 1))
        mask = jnp.broadcast_to(dist_own >= 0, logits.shape)
        if n_top > 0:
            ic, okc = args[2], args[3]
            kg = k_blk[bi, hi, ic].reshape(b, h, qc_len, n_keys, dh)
            vg = v_blk[bi, hi, ic].reshape(b, h, qc_len, n_keys, dh)
            kpos = (ic[..., None] * MOBA_BLOCK + jnp.arange(MOBA_BLOCK)).reshape(b, h, qc_len, n_keys)
            dist = tq[None, None, :, None] - kpos
            sel_logits = (jnp.einsum('bhqd,bhqkd->bhqk', qc, kg).astype(jnp.float32) * SCALE
                          + bias_tab[t5_bucket(dist), hi])
            logits = jnp.concatenate([sel_logits, logits], axis=-1)
            mask = jnp.concatenate([jnp.repeat(okc, MOBA_BLOCK, axis=-1), mask], axis=-1)
        p = masked_softmax(logits, mask).astype(v.dtype)
        o = jnp.einsum('bhqk,bhkd->bhqd', p[..., n_keys:], v_own)
        if n_top > 0:
            o = o + jnp.einsum('bhqk,bhqkd->bhqd', p[..., :n_keys], vg)
        return o

    qs = jnp.moveaxis(q.reshape(b, h, n_chunks, qc_len, dh), 2, 0)
    t0s = jnp.arange(n_chunks, dtype=jnp.int32) * qc_len
    if n_top > 0:
        k_mean = jnp.mean(k_blk, axis=3)
        gate = jnp.einsum('bhtd,bhnd->bhtn', q, k_mean).astype(jnp.float32)
        past = jnp.arange(n_blk)[None, :] < (jnp.arange(s) // MOBA_BLOCK)[:, None]
        top_val, top_idx = lax.top_k(jnp.where(past, gate, NEG_INF), n_top)
        top_ok = top_val > NEG_INF / 2
        idx_s = jnp.moveaxis(top_idx.reshape(b, h, n_chunks, qc_len, n_top), 2, 0)
        ok_s = jnp.moveaxis(top_ok.reshape(b, h, n_chunks, qc_len, n_top), 2, 0)
        out = lax.map(chunk, (qs, t0s, idx_s, ok_s))
    else:
        out = lax.map(chunk, (qs, t0s))
    return jnp.moveaxis(out, 0, 2).reshape(b, h, s, dh)


def memory_attention(q, mk, mv):
    logits = jnp.einsum('bhtd,bhmd->bhtm', q, mk).astype(jnp.float32) * SCALE
    p = jax.nn.softmax(logits, axis=-1).astype(mv.dtype)
    return jnp.einsum('bhtm,bhmd->bhtd', p, mv)


def merge_heads(o):
    b, h, s, dh = o.shape
    return o.transpose(0, 2, 1, 3).reshape(b, s, h * dh)


def hybrid_mixer(h, mem_n, rel_bias, w_in, cmp_pos_k, cmp_w1_k, cmp_w2_k, cmp_pos_v, cmp_w1_v, cmp_w2_v,
                 w_mem_kv, w_nsa_o, w_moba_o, w_mem_o, w_mix_out):
    b, s, _ = h.shape
    (q_n, k_c, v_c, k_s, v_s, k_w, v_w, g_n, q_m, k_m, v_m, q_x,
     gate_nsa, gate_moba, gate_mem) = split_columns(h @ w_in)

    tab_nsa = rel_bias[:, :NSA_HEADS]
    qn = to_heads(q_n, NSA_HEADS).reshape(b, NSA_GROUPS, NSA_HPG, s, HEAD_DIM)
    o_c, p_c = nsa_compressed(qn, to_heads(k_c, NSA_GROUPS), to_heads(v_c, NSA_GROUPS),
                              cmp_pos_k, cmp_w1_k, cmp_w2_k, cmp_pos_v, cmp_w1_v, cmp_w2_v, tab_nsa)
    o_s = nsa_selected(qn, to_heads(k_s, NSA_GROUPS), to_heads(v_s, NSA_GROUPS), p_c, tab_nsa)
    o_w = nsa_window(qn, to_heads(k_w, NSA_GROUPS), to_heads(v_w, NSA_GROUPS), tab_nsa)
    gb = jax.nn.sigmoid(g_n).reshape(b, s, NSA_GROUPS, NSA_HPG, 3).transpose(0, 2, 3, 1, 4)
    o_nsa = gb[..., 0:1] * o_c + gb[..., 1:2] * o_s + gb[..., 2:3] * o_w
    y_nsa = merge_heads(o_nsa.reshape(b, NSA_HEADS, s, HEAD_DIM)) @ w_nsa_o

    o_m = moba_attention(to_heads(q_m, MOBA_HEADS), to_heads(k_m, MOBA_HEADS), to_heads(v_m, MOBA_HEADS),
                         rel_bias[:, NSA_HEADS:])
    y_moba = merge_heads(o_m) @ w_moba_o

    mk, mv = jnp.split(mem_n @ w_mem_kv, 2, axis=-1)
    o_x = memory_attention(to_heads(q_x, MEM_HEADS), to_heads(mk, MEM_HEADS), to_heads(mv, MEM_HEADS))
    y_mem = merge_heads(o_x) @ w_mem_o

    merged = (jax.nn.sigmoid(gate_nsa) * y_nsa + jax.nn.sigmoid(gate_moba) * y_moba
              + jax.nn.sigmoid(gate_mem) * y_mem)
    return merged @ w_mix_out


def setup_inputs(seed: int = 0) -> dict:
    key = jax.random.key(seed)
    ks = jax.random.split(key, 24)

    def nrm(k, shape, scale):
        return jax.random.normal(k, shape, jnp.float32) * scale

    def gain(k):
        return 1.0 + 0.05 * jax.random.normal(k, (DEPTH, D_MODEL), jnp.float32)

    return {
        'x': nrm(ks[0], (BATCH, SEQ, D_MODEL), 1.0),
        'mem': nrm(ks[1], (BATCH, MEM_LEN, D_MODEL), 1.0),
        'rel_bias': nrm(ks[2], (REL_BUCKETS, N_BIAS_HEADS), 0.1),
        'pre_mix_g': gain(ks[3]),
        'mem_norm_g': gain(ks[4]),
        'post_mix_g': gain(ks[5]),
        'w_in': nrm(ks[6], (DEPTH, D_MODEL, IN_WIDTH), D_MODEL ** -0.5),
        'cmp_pos_k': nrm(ks[7], (DEPTH, CMP_LEN, HEAD_DIM), 0.02),
        'cmp_w1_k': nrm(ks[8], (DEPTH, CMP_LEN * HEAD_DIM, CMP_HIDDEN), (CMP_LEN * HEAD_DIM) ** -0.5),
        'cmp_w2_k': nrm(ks[9], (DEPTH, CMP_HIDDEN, HEAD_DIM), CMP_HIDDEN ** -0.5),
        'cmp_pos_v': nrm(ks[10], (DEPTH, CMP_LEN, HEAD_DIM), 0.02),
        'cmp_w1_v': nrm(ks[11], (DEPTH, CMP_LEN * HEAD_DIM, CMP_HIDDEN), (CMP_LEN * HEAD_DIM) ** -0.5),
        'cmp_w2_v': nrm(ks[12], (DEPTH, CMP_HIDDEN, HEAD_DIM), CMP_HIDDEN ** -0.5),
        'w_mem_kv': nrm(ks[13], (DEPTH, D_MODEL, 2 * MEM_W), D_MODEL ** -0.5),
        'w_nsa_o': nrm(ks[14], (DEPTH, NSA_Q_W, D_MODEL), NSA_Q_W ** -0.5),
        'w_moba_o': nrm(ks[15], (DEPTH, MOBA_W, D_MODEL), MOBA_W ** -0.5),
        'w_mem_o': nrm(ks[16], (DEPTH, MEM_W, D_MODEL), MEM_W ** -0.5),
        'w_mix_out': nrm(ks[17], (DEPTH, D_MODEL, D_MODEL), D_MODEL ** -0.5),
        'pre_ffn_g': gain(ks[18]),
        'post_ffn_g': gain(ks[19]),
        'w_ffn_gate': nrm(ks[20], (DEPTH, D_MODEL, D_FF), D_MODEL ** -0.5),
        'w_ffn_up': nrm(ks[21], (DEPTH, D_MODEL, D_FF), D_MODEL ** -0.5),
        'w_ffn_down': nrm(ks[22], (DEPTH, D_FF, D_MODEL), D_FF ** -0.5),
    }


def reference(x, mem, rel_bias, pre_mix_g, mem_norm_g, post_mix_g, w_in,
              cmp_pos_k, cmp_w1_k, cmp_w2_k, cmp_pos_v, cmp_w1_v, cmp_w2_v,
              w_mem_kv, w_nsa_o, w_moba_o, w_mem_o, w_mix_out,
              pre_ffn_g, post_ffn_g, w_ffn_gate, w_ffn_up, w_ffn_down):
    for l in range(DEPTH):
        h = rms_norm(x, pre_mix_g[l])
        mem_n = rms_norm(mem, mem_norm_g[l])
        y = hybrid_mixer(h, mem_n, rel_bias, w_in[l], cmp_pos_k[l], cmp_w1_k[l], cmp_w2_k[l],
                         cmp_pos_v[l], cmp_w1_v[l], cmp_w2_v[l], w_mem_kv[l],
                         w_nsa_o[l], w_moba_o[l], w_mem_o[l], w_mix_out[l])
        x = x + rms_norm(y, post_mix_g[l])
        h = rms_norm(x, pre_ffn_g[l])
        f = (jax.nn.silu(h @ w_ffn_gate[l]) * (h @ w_ffn_up[l])) @ w_ffn_down[l]
        x = x + rms_norm(f, post_ffn_g[l])
    return x
```

```python
import functools
import math

import numpy as np
import jax
import jax.numpy as jnp
from jax import lax
from jax.experimental import pallas as pl
from jax.experimental.pallas import tpu as pltpu

F32 = jnp.float32
BF16 = jnp.bfloat16

D_MODEL = 1024
HEAD_DIM = 64
SCALE = HEAD_DIM ** -0.5
NSA_HEADS = 8
NSA_GROUPS = 2
NSA_HPG = NSA_HEADS // NSA_GROUPS
CMP_LEN = 32
CMP_STRIDE = 16
CMP_HIDDEN = 128
SEL_BLOCK = 64
SEL_TOPN = 8
WINDOW = 256
MOBA_HEADS = 4
MOBA_BLOCK = 256
MOBA_TOPK = 3
MEM_HEADS = 4
REL_BUCKETS = 32
REL_MAX_DIST = 128
RMS_EPS = 1e-6
NEG_INF = -1e30
FORCE_SCORE = 1e4

NSA_Q_W = NSA_HEADS * HEAD_DIM
NSA_KV_W = NSA_GROUPS * HEAD_DIM
NSA_GATE_W = NSA_HEADS * 3
MOBA_W = MOBA_HEADS * HEAD_DIM
MEM_W = MEM_HEADS * HEAD_DIM
ATT_W = NSA_Q_W + 6 * NSA_KV_W + NSA_GATE_W + 3 * MOBA_W + MEM_W
LANES = 128
GATE_PAD = LANES
TILE = 256
N_CMP_PAD = 128
VMEM_LIMIT = 56 * 1024 * 1024


def _dot(a, b):
    return jnp.dot(a, b, preferred_element_type=F32)


def _dot_nt(a, b):
    return lax.dot_general(a, b, (((1,), (1,)), ((), ())), preferred_element_type=F32)


def _split_bf16(x):
    hi = x.astype(BF16)
    lo = (x - hi.astype(F32)).astype(BF16)
    return hi, lo


def _rms(x, g):
    return x * lax.rsqrt(jnp.mean(x * x, axis=-1, keepdims=True) + RMS_EPS) * g


def _t5_bucket(dist):
    dist = jnp.maximum(dist, 0)
    max_exact = REL_BUCKETS // 2
    logd = jnp.log(jnp.maximum(dist, 1).astype(jnp.float32) / max_exact) / math.log(REL_MAX_DIST / max_exact)
    large = jnp.minimum(max_exact + (logd * (REL_BUCKETS - max_exact)).astype(jnp.int32), REL_BUCKETS - 1)
    return jnp.where(dist < max_exact, dist, large)


def _params(sem):
    return pltpu.CompilerParams(dimension_semantics=sem, vmem_limit_bytes=VMEM_LIMIT)


def _const_spec(shape):
    nd = len(shape)
    return pl.BlockSpec(shape, lambda *_: (0,) * nd)


_INPROJ_OUTS = (
    ("qn", NSA_Q_W, BF16, True),
    ("kc", NSA_KV_W, F32, False), ("vc", NSA_KV_W, F32, False),
    ("ks", NSA_KV_W, BF16, False), ("vs", NSA_KV_W, BF16, False),
    ("kw", NSA_KV_W, BF16, False), ("vw", NSA_KV_W, BF16, False),
    ("gn", GATE_PAD, F32, False),
    ("qm", MOBA_W, BF16, True), ("km", MOBA_W, BF16, False), ("vm", MOBA_W, BF16, False),
    ("qx", MEM_W, BF16, True),
)
_INPROJ_W = sum(o[1] for o in _INPROJ_OUTS)


def _inproj_kernel(x_ref, g_ref, w_ref, *out_refs):
    h = _rms(x_ref[...], g_ref[...]).astype(BF16)
    lo = 0
    for (name, width, dtype, scaled), o_ref in zip(_INPROJ_OUTS, out_refs):
        y = _dot(h, w_ref[:, lo:lo + width])
        if scaled:
            y = y * SCALE
        if name == "gn":
            y = jax.nn.sigmoid(y)
        o_ref[...] = y.astype(dtype)
        lo += width


def _inproj(x2, g, w, tm=256):
    m = x2.shape[0]
    return pl.pallas_call(
        _inproj_kernel,
        grid=(m // tm,),
        in_specs=[pl.BlockSpec((tm, D_MODEL), lambda i: (i, 0)),
                  _const_spec((1, D_MODEL)),
                  _const_spec((D_MODEL, _INPROJ_W))],
        out_specs=[pl.BlockSpec((tm, o[1]), lambda i: (i, 0)) for o in _INPROJ_OUTS],
        out_shape=[jax.ShapeDtypeStruct((m, o[1]), o[2]) for o in _INPROJ_OUTS],
        compiler_params=_params(("parallel",)),
        name="inproj",
    )(x2, g, w)


def _compress_kernel(rk_ref, rv_ref, pk_ref, pv_ref, w1k_ref, w1v_ref, w2k_ref, w2v_ref, kc_ref, vc_ref):
    def one(r_ref, p_ref, w1_ref, w2_ref, o_ref):
        r = r_ref[0]
        top = _dot((r + p_ref[0:1, :]).astype(BF16), w1_ref[0])
        bot = _dot((r + p_ref[1:2, :]).astype(BF16), w1_ref[1])
        hid = top + pltpu.roll(bot, N_CMP_PAD - 1, 0)
        act = jax.nn.gelu(hid).astype(BF16)
        for g in range(NSA_GROUPS):
            o = _dot(act[:, g * CMP_HIDDEN:(g + 1) * CMP_HIDDEN], w2_ref[...])
            o_ref[0, :, g * HEAD_DIM:(g + 1) * HEAD_DIM] = o.astype(BF16)

    one(rk_ref, pk_ref, w1k_ref, w2k_ref, kc_ref)
    one(rv_ref, pv_ref, w1v_ref, w2v_ref, vc_ref)


def _compress(rk, rv, pk, pv, w1k, w1v, w2k, w2v):
    b = rk.shape[0]
    rw = rk.shape[2]
    r_spec = pl.BlockSpec((1, N_CMP_PAD, rw), lambda i: (i, 0, 0))
    o_spec = pl.BlockSpec((1, N_CMP_PAD, NSA_KV_W), lambda i: (i, 0, 0))
    return pl.pallas_call(
        _compress_kernel,
        grid=(b,),
        in_specs=[r_spec, r_spec, _const_spec(pk.shape), _const_spec(pv.shape),
                  _const_spec(w1k.shape), _const_spec(w1v.shape),
                  _const_spec(w2k.shape), _const_spec(w2v.shape)],
        out_specs=[o_spec, o_spec],
        out_shape=[jax.ShapeDtypeStruct((b, N_CMP_PAD, NSA_KV_W), BF16)] * 2,
        compiler_params=_params(("parallel",)),
        name="compress",
    )(rk, rv, pk, pv, w1k, w1v, w2k, w2v)


def _compress_weights(pos, w1):
    half = CMP_LEN // 2
    p = pos.reshape(2, half, 1, HEAD_DIM)
    p = jnp.broadcast_to(p, (2, half, NSA_GROUPS, HEAD_DIM)).reshape(2, half * NSA_KV_W)
    w = w1.reshape(2, half, HEAD_DIM, CMP_HIDDEN)
    eye = jnp.eye(NSA_GROUPS, dtype=w1.dtype)
    wbd = jnp.einsum("ajdm,gk->ajgdkm", w, eye).reshape(2, half * NSA_KV_W, NSA_GROUPS * CMP_HIDDEN)
    return p.astype(F32), wbd.astype(BF16)


def _memkv_kernel(m_ref, g_ref, w_ref, k_ref, v_ref):
    h = _rms(m_ref[...], g_ref[...]).astype(BF16)
    k_ref[...] = _dot(h, w_ref[:, :MEM_W]).astype(BF16)
    v_ref[...] = _dot(h, w_ref[:, MEM_W:]).astype(BF16)


def _memkv(mem2, g, w, tm=256):
    m = mem2.shape[0]
    o_spec = pl.BlockSpec((tm, MEM_W), lambda i: (i, 0))
    return pl.pallas_call(
        _memkv_kernel,
        grid=(m // tm,),
        in_specs=[pl.BlockSpec((tm, D_MODEL), lambda i: (i, 0)), _const_spec((1, D_MODEL)),
                  _const_spec((D_MODEL, 2 * MEM_W))],
        out_specs=[o_spec, o_spec],
        out_shape=[jax.ShapeDtypeStruct((m, MEM_W), BF16)] * 2,
        compiler_params=_params(("parallel",)),
        name="memkv",
    )(mem2, g, w)


def _flash_init(m_ref, l_ref, acc_ref):
    m_ref[...] = jnp.full(m_ref.shape, NEG_INF, F32)
    l_ref[...] = jnp.zeros(l_ref.shape, F32)
    acc_ref[...] = jnp.zeros(acc_ref.shape, F32)


def _flash_update(h, q, k, v, add, m_ref, l_ref, acc_ref):
    s = _dot_nt(q, k) + add
    m_old = m_ref[h]
    m_new = jnp.maximum(m_old, jnp.max(s, axis=-1, keepdims=True))
    alpha = jnp.exp(m_old - m_new)
    p = jnp.exp(s - m_new)
    l_ref[h] = alpha * l_ref[h] + jnp.sum(p, axis=-1, keepdims=True)
    acc_ref[h] = alpha * acc_ref[h] + _dot(p.astype(BF16), v)
    m_ref[h] = m_new


def _rank_before(score, n_cand):
    blk = lax.broadcasted_iota(jnp.int32, score.shape, 1)
    rank = jnp.zeros(score.shape, F32)
    for m in range(n_cand):
        col = jnp.broadcast_to(score[:, m:m + 1], score.shape)
        tie = jnp.where(blk > m, 1.0, 0.0)
        rank = rank + jnp.where(col > score, 1.0, 0.0) + jnp.where(col == score, tie, 0.0)
    return rank


def _nsa_kernel(q_ref, gn_ref, kc_ref, vc_ref, ks_ref, vs_ref, kw_ref, vw_ref,
                bc_ref, t_ref, tw_ref, ov_ref, e_ref, o_ref, m_sc, l_sc, acc_sc):
    qi = pl.program_id(1)
    row = lax.broadcasted_iota(jnp.int32, (TILE, 1), 0) + qi * TILE
    blk = lax.broadcasted_iota(jnp.int32, (TILE, LANES), 1)
    cur = row // SEL_BLOCK
    has_cmp = row >= CMP_LEN - 1
    prev = jnp.maximum(qi - 1, 0)
    no_prev = jnp.where(qi == 0, NEG_INF, 0.0).astype(F32)

    for g in range(NSA_GROUPS):
        gsl = slice(g * HEAD_DIM, (g + 1) * HEAD_DIM)
        heads = [g * NSA_HPG + j for j in range(NSA_HPG)]
        qs = [q_ref[:, h * HEAD_DIM:(h + 1) * HEAD_DIM] for h in heads]

        kc = kc_ref[0, :, gsl]
        vc = vc_ref[0, :, gsl]
        psum = jnp.zeros((TILE, N_CMP_PAD), F32)
        o_cmp = []
        for j, h in enumerate(heads):
            s = _dot_nt(qs[j], kc) + bc_ref[h]
            e = jnp.exp(s - jnp.max(s, axis=-1, keepdims=True))
            p = jnp.where(has_cmp, e / jnp.sum(e, axis=-1, keepdims=True), 0.0)
            psum = psum + p
            o_cmp.append(_dot(p.astype(BF16), vc))

        p_hi, p_lo = _split_bf16(psum)
        imp = _dot(p_hi, ov_ref[...]) + _dot(p_lo, ov_ref[...])
        forced = (blk == 0) | (blk == cur) | (blk == cur - 1)
        score = jnp.where(forced, FORCE_SCORE, jnp.where(blk <= cur, imp, NEG_INF))
        rank = _rank_before(score, ks_ref.shape[1] * (TILE // SEL_BLOCK))
        sel = jnp.where(rank < SEL_TOPN, jnp.where(score > NEG_INF / 2, 0.0, NEG_INF), NEG_INF)
        sel_neg = sel.astype(BF16)

        _flash_init(m_sc, l_sc, acc_sc)

        def sel_tile(kt, d):
            mask_add = _dot(sel_neg, e_ref[kt])
            k = ks_ref[0, kt, :, gsl]
            v = vs_ref[0, kt, :, gsl]
            for j, h in enumerate(heads):
                _flash_update(j, qs[j], k, v, t_ref[d, h] + mask_add, m_sc, l_sc, acc_sc)

        sel_tile(qi, 0)

        def sel_body(kt, carry):
            sel_tile(kt, jnp.minimum(qi - kt, 2))
            return carry

        lax.fori_loop(0, qi, sel_body, 0)

        k_prev = kw_ref[0, prev, :, gsl]
        v_prev = vw_ref[0, prev, :, gsl]
        k_own = kw_ref[0, qi, :, gsl]
        v_own = vw_ref[0, qi, :, gsl]
        for j, h in enumerate(heads):
            s0 = _dot_nt(qs[j], k_prev) + (tw_ref[h] + no_prev)
            s1 = _dot_nt(qs[j], k_own) + t_ref[0, h]
            m = jnp.maximum(jnp.max(s0, axis=-1, keepdims=True), jnp.max(s1, axis=-1, keepdims=True))
            e0 = jnp.exp(s0 - m)
            e1 = jnp.exp(s1 - m)
            l = jnp.sum(e0, axis=-1, keepdims=True) + jnp.sum(e1, axis=-1, keepdims=True)
            o_win = (_dot(e0.astype(BF16), v_prev) + _dot(e1.astype(BF16), v_own)) / l
            o_sel = acc_sc[j] / l_sc[j]
            gc = gn_ref[:, 3 * h:3 * h + 1]
            gs = gn_ref[:, 3 * h + 1:3 * h + 2]
            gw = gn_ref[:, 3 * h + 2:3 * h + 3]
            o = gc * o_cmp[j] + gs * o_sel + gw * o_win
            o_ref[:, h * HEAD_DIM:(h + 1) * HEAD_DIM] = o.astype(BF16)


def _nsa(b, s, qn, gn, kc, vc, ks, vs, kw, vw, bias_cmp, t_nsa, t_win, overlap, expand):
    nt = s // TILE
    kv_spec = pl.BlockSpec((1, nt, TILE, NSA_KV_W), lambda i, j: (i, 0, 0, 0))
    c_spec = pl.BlockSpec((1, N_CMP_PAD, NSA_KV_W), lambda i, j: (i, 0, 0))
    return pl.pallas_call(
        _nsa_kernel,
        grid=(b, nt),
        in_specs=[pl.BlockSpec((TILE, NSA_Q_W), lambda i, j: (i * nt + j, 0)),
                  pl.BlockSpec((TILE, GATE_PAD), lambda i, j: (i * nt + j, 0)),
                  c_spec, c_spec, kv_spec, kv_spec, kv_spec, kv_spec,
                  pl.BlockSpec((NSA_HEADS, TILE, N_CMP_PAD), lambda i, j: (0, j, 0)),
                  _const_spec(t_nsa.shape), _const_spec(t_win.shape),
                  _const_spec(overlap.shape), _const_spec(expand.shape)],
        out_specs=pl.BlockSpec((TILE, NSA_Q_W), lambda i, j: (i * nt + j, 0)),
        out_shape=jax.ShapeDtypeStruct((b * s, NSA_Q_W), BF16),
        scratch_shapes=[pltpu.VMEM((NSA_HPG, TILE, 1), F32), pltpu.VMEM((NSA_HPG, TILE, 1), F32),
                        pltpu.VMEM((NSA_HPG, TILE, HEAD_DIM), F32)],
        compiler_params=_params(("parallel", "arbitrary")),
        name="nsa",
    )(qn, gn, kc, vc, ks.reshape(b, nt, TILE, NSA_KV_W), vs.reshape(b, nt, TILE, NSA_KV_W),
      kw.reshape(b, nt, TILE, NSA_KV_W), vw.reshape(b, nt, TILE, NSA_KV_W),
      bias_cmp, t_nsa, t_win, overlap, expand)


def _moba_kernel(qm_ref, km_ref, vm_ref, qx_ref, mk_ref, mv_ref, t_ref, e_ref, om_ref, ox_ref,
                 m_sc, l_sc, acc_sc):
    c = pl.program_id(1)
    nt = km_ref.shape[1]
    blk = lax.broadcasted_iota(jnp.int32, (TILE, LANES), 1)

    sums = [jnp.sum(km_ref[0, n].astype(F32), axis=0, keepdims=True) for n in range(nt)]
    kmean = jnp.concatenate(sums + [jnp.zeros((LANES - nt, MOBA_W), F32)], axis=0) * (1.0 / MOBA_BLOCK)
    kmean_t = kmean.T
    km_hi, km_lo = _split_bf16(kmean_t)

    _flash_init(m_sc, l_sc, acc_sc)
    qs = [qm_ref[:, h * HEAD_DIM:(h + 1) * HEAD_DIM] for h in range(MOBA_HEADS)]
    sel_negs = []
    for h in range(MOBA_HEADS):
        hsl = slice(h * HEAD_DIM, (h + 1) * HEAD_DIM)
        gate = _dot(qs[h], km_hi[hsl, :]) + _dot(qs[h], km_lo[hsl, :])
        score = jnp.where(blk < c, gate, NEG_INF * SCALE)
        rank = _rank_before(score, nt)
        sel = jnp.where(rank < MOBA_TOPK, jnp.where(score > NEG_INF * SCALE / 2, 0.0, NEG_INF), NEG_INF)
        sel_negs.append(sel.astype(BF16))
        _flash_update(h, qs[h], km_ref[0, c, :, hsl], vm_ref[0, c, :, hsl], t_ref[0, h], m_sc, l_sc, acc_sc)

    def body(n, carry):
        d = jnp.minimum(c - n, 2)
        for h in range(MOBA_HEADS):
            hsl = slice(h * HEAD_DIM, (h + 1) * HEAD_DIM)
            add = t_ref[d, h] + _dot(sel_negs[h], e_ref[n])
            _flash_update(h, qs[h], km_ref[0, n, :, hsl], vm_ref[0, n, :, hsl], add, m_sc, l_sc, acc_sc)
        return carry

    lax.fori_loop(0, c, body, 0)

    for h in range(MOBA_HEADS):
        hsl = slice(h * HEAD_DIM, (h + 1) * HEAD_DIM)
        om_ref[:, hsl] = (acc_sc[h] / l_sc[h]).astype(BF16)
        s = _dot_nt(qx_ref[:, hsl], mk_ref[0, :, hsl])
        e = jnp.exp(s - jnp.max(s, axis=-1, keepdims=True))
        p = e / jnp.sum(e, axis=-1, keepdims=True)
        ox_ref[:, hsl] = _dot(p.astype(BF16), mv_ref[0, :, hsl]).astype(BF16)


def _moba(b, s, qm, km, vm, qx, mk, mv, t_moba, expand):
    nt = s // TILE
    mem_len = mk.shape[0] // b
    q_spec = pl.BlockSpec((TILE, MOBA_W), lambda i, j: (i * nt + j, 0))
    kv_spec = pl.BlockSpec((1, nt, TILE, MOBA_W), lambda i, j: (i, 0, 0, 0))
    mem_spec = pl.BlockSpec((1, mem_len, MEM_W), lambda i, j: (i, 0, 0))
    return pl.pallas_call(
        _moba_kernel,
        grid=(b, nt),
        in_specs=[q_spec, kv_spec, kv_spec, q_spec, mem_spec, mem_spec,
                  _const_spec(t_moba.shape), _const_spec(expand.shape)],
        out_specs=[q_spec, q_spec],
        out_shape=[jax.ShapeDtypeStruct((b * s, MOBA_W), BF16), jax.ShapeDtypeStruct((b * s, MEM_W), BF16)],
        scratch_shapes=[pltpu.VMEM((MOBA_HEADS, TILE, 1), F32), pltpu.VMEM((MOBA_HEADS, TILE, 1), F32),
                        pltpu.VMEM((MOBA_HEADS, TILE, HEAD_DIM), F32)],
        compiler_params=_params(("parallel", "arbitrary")),
        name="moba",
    )(qm, km.reshape(b, nt, TILE, MOBA_W), vm.reshape(b, nt, TILE, MOBA_W), qx,
      mk.reshape(b, mem_len, MEM_W), mv.reshape(b, mem_len, MEM_W), t_moba, expand)


def _mix_kernel(x_ref, on_ref, om_ref, ox_ref, g_pre_ref, g_post_ref, wg_ref, wn_ref, wm_ref, wx_ref,
                wo_ref, o_ref):
    x = x_ref[...]
    h = _rms(x, g_pre_ref[...]).astype(BF16)
    merged = jax.nn.sigmoid(_dot(h, wg_ref[:, :D_MODEL])) * _dot(on_ref[...], wn_ref[...])
    merged = merged + jax.nn.sigmoid(_dot(h, wg_ref[:, D_MODEL:2 * D_MODEL])) * _dot(om_ref[...], wm_ref[...])
    merged = merged + jax.nn.sigmoid(_dot(h, wg_ref[:, 2 * D_MODEL:])) * _dot(ox_ref[...], wx_ref[...])
    y = _dot(merged.astype(BF16), wo_ref[...])
    o_ref[...] = x + _rms(y, g_post_ref[...])


def _mix(x2, o_nsa, o_moba, o_mem, g_pre, g_post, w_gates, w_nsa_o, w_moba_o, w_mem_o, w_mix_out, tm=256):
    m = x2.shape[0]
    row = lambda w: pl.BlockSpec((tm, w), lambda i: (i, 0))
    return pl.pallas_call(
        _mix_kernel,
        grid=(m // tm,),
        in_specs=[row(D_MODEL), row(NSA_Q_W), row(MOBA_W), row(MEM_W),
                  _const_spec((1, D_MODEL)), _const_spec((1, D_MODEL)),
                  _const_spec(w_gates.shape), _const_spec(w_nsa_o.shape), _const_spec(w_moba_o.shape),
                  _const_spec(w_mem_o.shape), _const_spec(w_mix_out.shape)],
        out_specs=row(D_MODEL),
        out_shape=jax.ShapeDtypeStruct((m, D_MODEL), F32),
        compiler_params=_params(("parallel",)),
        name="mix",
    )(x2, o_nsa, o_moba, o_mem, g_pre, g_post, w_gates, w_nsa_o, w_moba_o, w_mem_o, w_mix_out)


FFN_CHUNK = 256


def _ffn_kernel(x_ref, g_pre_ref, g_post_ref, wg_ref, wu_ref, wd_ref, o_ref, a_sc):
    x = x_ref[...]
    h = _rms(x, g_pre_ref[...]).astype(BF16)
    d_ff = wg_ref.shape[1]
    for j in range(d_ff // FFN_CHUNK):
        sl = slice(j * FFN_CHUNK, (j + 1) * FFN_CHUNK)
        a_sc[:, sl] = (jax.nn.silu(_dot(h, wg_ref[:, sl])) * _dot(h, wu_ref[:, sl])).astype(BF16)
    f = _dot(a_sc[...], wd_ref[...])
    o_ref[...] = x + _rms(f, g_post_ref[...])


def _ffn(x2, g_pre, g_post, wg, wu, wd, tm=256):
    m = x2.shape[0]
    d_ff = wg.shape[1]
    return pl.pallas_call(
        _ffn_kernel,
        grid=(m // tm,),
        in_specs=[pl.BlockSpec((tm, D_MODEL), lambda i: (i, 0)),
                  _const_spec((1, D_MODEL)), _const_spec((1, D_MODEL)),
                  _const_spec(wg.shape), _const_spec(wu.shape), _const_spec(wd.shape)],
        out_specs=pl.BlockSpec((tm, D_MODEL), lambda i: (i, 0)),
        out_shape=jax.ShapeDtypeStruct((m, D_MODEL), F32),
        scratch_shapes=[pltpu.VMEM((tm, d_ff), BF16)],
        compiler_params=_params(("parallel",)),
        name="ffn",
    )(x2, g_pre, g_post, wg, wu, wd)


def _bias_tables(rel_bias, s):
    n_heads = rel_bias.shape[1]
    bvec = rel_bias[_t5_bucket(jnp.arange(s + TILE))].T.astype(F32)
    i = jnp.arange(TILE)[:, None]
    j = jnp.arange(TILE)[None, :]
    assert TILE + 1 >= REL_MAX_DIST
    tiles = []
    for d in range(3):
        dist = d * TILE + i - j
        t = bvec[:, jnp.maximum(dist, 0)]
        tiles.append(jnp.where(dist >= 0, t, NEG_INF))
    t_all = jnp.stack(tiles)
    dist1 = TILE + i - j
    t_win = jnp.where(dist1 < WINDOW, t_all[1, :NSA_HEADS], NEG_INF)
    n_cmp = (s - CMP_LEN) // CMP_STRIDE + 1
    cidx = jnp.arange(N_CMP_PAD)[None, :]
    dist_c = jnp.arange(s)[:, None] - (cidx * CMP_STRIDE + CMP_LEN - 1)
    b_cmp = jnp.where((dist_c >= 0) & (cidx < n_cmp), bvec[:NSA_HEADS][:, jnp.maximum(dist_c, 0)], NEG_INF)
    return t_all[:, :NSA_HEADS], t_win, t_all[:, NSA_HEADS:], b_cmp


def _geometry_tables(s):
    nt = s // TILE
    n_cmp = (s - CMP_LEN) // CMP_STRIDE + 1
    n_sel = s // SEL_BLOCK
    cs = np.arange(n_cmp) * CMP_STRIDE
    ss = np.arange(n_sel) * SEL_BLOCK
    ov = np.clip(np.minimum(cs[:, None] + CMP_LEN, ss[None, :] + SEL_BLOCK)
                 - np.maximum(cs[:, None], ss[None, :]), 0, None).astype(np.float32) / CMP_LEN
    overlap = np.zeros((N_CMP_PAD, LANES), np.float32)
    overlap[:n_cmp, :n_sel] = ov
    key = np.arange(s).reshape(nt, 1, TILE)
    n = np.arange(LANES).reshape(1, LANES, 1)
    expand_sel = (key // SEL_BLOCK == n).astype(np.float32)
    expand_moba = (key // MOBA_BLOCK == n).astype(np.float32)
    return (jnp.asarray(overlap, BF16), jnp.asarray(expand_sel, BF16), jnp.asarray(expand_moba, BF16))


def kernel(x, mem, rel_bias, pre_mix_g, mem_norm_g, post_mix_g, w_in, cmp_pos_k, cmp_w1_k, cmp_w2_k, cmp_pos_v, cmp_w1_v, cmp_w2_v, w_mem_kv, w_nsa_o, w_moba_o, w_mem_o, w_mix_out, pre_ffn_g, post_ffn_g, w_ffn_gate, w_ffn_up, w_ffn_down):
    b, s, d_model = x.shape
    depth = w_in.shape[0]
    assert d_model == D_MODEL and s % TILE == 0 and TILE == MOBA_BLOCK == WINDOW
    assert s // SEL_BLOCK <= LANES and (s - CMP_LEN) // CMP_STRIDE + 1 < N_CMP_PAD
    assert w_in.shape[2] == ATT_W + 3 * D_MODEL

    t_nsa, t_win, t_moba, b_cmp = _bias_tables(rel_bias, s)
    overlap, expand_sel, expand_moba = _geometry_tables(s)
    gate_lo = NSA_Q_W + 6 * NSA_KV_W
    rows_per_chunk = CMP_STRIDE * NSA_KV_W

    x2 = x.reshape(b * s, D_MODEL)
    mem2 = mem.reshape(-1, D_MODEL)
    for l in range(depth):
        w_att = jnp.concatenate(
            [w_in[l, :, :gate_lo + NSA_GATE_W],
             jnp.zeros((D_MODEL, GATE_PAD - NSA_GATE_W), w_in.dtype),
             w_in[l, :, gate_lo + NSA_GATE_W:ATT_W]], axis=1).astype(BF16)
        w_gates = w_in[l, :, ATT_W:].astype(BF16)
        row = lambda v: v[l].reshape(1, D_MODEL)

        qn, kc_raw, vc_raw, ks, vs, kw, vw, gn, qm, km, vm, qx = _inproj(x2, row(pre_mix_g), w_att)

        pk, w1k = _compress_weights(cmp_pos_k[l], cmp_w1_k[l])
        pv, w1v = _compress_weights(cmp_pos_v[l], cmp_w1_v[l])
        kc, vc = _compress(kc_raw.reshape(b, s // CMP_STRIDE, rows_per_chunk),
                           vc_raw.reshape(b, s // CMP_STRIDE, rows_per_chunk),
                           pk, pv, w1k, w1v, cmp_w2_k[l].astype(BF16), cmp_w2_v[l].astype(BF16))

        mk, mv = _memkv(mem2, row(mem_norm_g), w_mem_kv[l].astype(BF16))

        o_nsa = _nsa(b, s, qn, gn, kc, vc, ks, vs, kw, vw, b_cmp, t_nsa, t_win, overlap, expand_sel)
        o_moba, o_mem = _moba(b, s, qm, km, vm, qx, mk, mv, t_moba, expand_moba)

        x2 = _mix(x2, o_nsa, o_moba, o_mem, row(pre_mix_g), row(post_mix_g), w_gates,
                  w_nsa_o[l].astype(BF16), w_moba_o[l].astype(BF16), w_mem_o[l].astype(BF16),
                  w_mix_out[l].astype(BF16))
        x2 = _ffn(x2, row(pre_ffn_g), row(post_ffn_g), w_ffn_gate[l].astype(BF16),
                  w_ffn_up[l].astype(BF16), w_ffn_down[l].astype(BF16))
    return x2.reshape(b, s, D_MODEL)
```

```python
import functools
import math

import numpy as np
import jax
import jax.numpy as jnp
from jax import lax
from jax.experimental import pallas as pl
from jax.experimental.pallas import tpu as pltpu

F32 = jnp.float32
BF16 = jnp.bfloat16

D_MODEL = 1024
HEAD_DIM = 64
SCALE = HEAD_DIM ** -0.5
NSA_HEADS = 8
NSA_GROUPS = 2
NSA_HPG = NSA_HEADS // NSA_GROUPS
CMP_LEN = 32
CMP_STRIDE = 16
CMP_HIDDEN = 128
SEL_BLOCK = 64
SEL_TOPN = 8
WINDOW = 256
MOBA_HEADS = 4
MOBA_BLOCK = 256
MOBA_TOPK = 3
MEM_HEADS = 4
REL_BUCKETS = 32
REL_MAX_DIST = 128
N_BIAS_HEADS = NSA_HEADS + MOBA_HEADS
RMS_EPS = 1e-6
NEG_INF = -1e30
FORCE_SCORE = 1e4

NSA_Q_W = NSA_HEADS * HEAD_DIM
NSA_KV_W = NSA_GROUPS * HEAD_DIM
NSA_GATE_W = NSA_HEADS * 3
MOBA_W = MOBA_HEADS * HEAD_DIM
MEM_W = MEM_HEADS * HEAD_DIM
ATT_W = NSA_Q_W + 6 * NSA_KV_W + NSA_GATE_W + 3 * MOBA_W + MEM_W
LANES = 128
SUBLANES = 8
BF16_ROWS = 16
GATE_PAD = LANES
TILE = 256
N_CMP_PAD = 128
V_AUG = HEAD_DIM + BF16_ROWS
MASKED_BUCKET = REL_BUCKETS
VMEM_LIMIT = 56 * 1024 * 1024


def _dot(a, b):
    return jnp.dot(a, b, preferred_element_type=F32)


def _dot_nt(a, b):
    return lax.dot_general(a, b, (((1,), (1,)), ((), ())), preferred_element_type=F32)


def _split_bf16(x):
    hi = x.astype(BF16)
    lo = (x - hi.astype(F32)).astype(BF16)
    return hi, lo


def _rms(x, g):
    return x * lax.rsqrt(jnp.mean(x * x, axis=-1, keepdims=True) + RMS_EPS) * g


def _params(sem):
    return pltpu.CompilerParams(dimension_semantics=sem, vmem_limit_bytes=VMEM_LIMIT)


def _const_spec(shape):
    nd = len(shape)
    return pl.BlockSpec(shape, lambda *_: (0,) * nd)


_INPROJ_OUTS = (
    ("qn", NSA_Q_W, BF16, True),
    ("kc", NSA_KV_W, F32, False), ("vc", NSA_KV_W, F32, False),
    ("ks", NSA_KV_W, BF16, False), ("vs", NSA_KV_W, BF16, False),
    ("kw", NSA_KV_W, BF16, False), ("vw", NSA_KV_W, BF16, False),
    ("gn", GATE_PAD, F32, False),
    ("qm", MOBA_W, BF16, True), ("km", MOBA_W, BF16, False), ("vm", MOBA_W, BF16, False),
    ("qx", MEM_W, BF16, True),
)
_INPROJ_W = sum(o[1] for o in _INPROJ_OUTS)


def _inproj_kernel(x_ref, g_ref, w_ref, *out_refs):
    h = _rms(x_ref[...], g_ref[...]).astype(BF16)
    lo = 0
    for (name, width, dtype, scaled), o_ref in zip(_INPROJ_OUTS, out_refs):
        y = _dot(h, w_ref[:, lo:lo + width])
        if scaled:
            y = y * SCALE
        if name == "gn":
            y = jax.nn.sigmoid(y)
        o_ref[...] = y.astype(dtype)
        lo += width


def _inproj(x2, g, w, tm=256):
    m = x2.shape[0]
    return pl.pallas_call(
        _inproj_kernel,
        grid=(m // tm,),
        in_specs=[pl.BlockSpec((tm, D_MODEL), lambda i: (i, 0)),
                  _const_spec((1, D_MODEL)),
                  _const_spec((D_MODEL, _INPROJ_W))],
        out_specs=[pl.BlockSpec((tm, o[1]), lambda i: (i, 0)) for o in _INPROJ_OUTS],
        out_shape=[jax.ShapeDtypeStruct((m, o[1]), o[2]) for o in _INPROJ_OUTS],
        compiler_params=_params(("parallel",)),
        name="inproj",
    )(x2, g, w)


def _compress_kernel(rk_ref, rv_ref, pk_ref, pv_ref, w1k_ref, w1v_ref, w2k_ref, w2v_ref, kc_ref, vc_ref):
    def one(r_ref, p_ref, w1_ref, w2_ref):
        r = r_ref[0]
        top = _dot((r + p_ref[0:1, :]).astype(BF16), w1_ref[0])
        bot = _dot((r + p_ref[1:2, :]).astype(BF16), w1_ref[1])
        hid = top + pltpu.roll(bot, N_CMP_PAD - 1, 0)
        act = jax.nn.gelu(hid).astype(BF16)
        return jnp.concatenate(
            [_dot(act[:, g * CMP_HIDDEN:(g + 1) * CMP_HIDDEN], w2_ref[...]) for g in range(NSA_GROUPS)], axis=1)

    kc_ref[0] = one(rk_ref, pk_ref, w1k_ref, w2k_ref).astype(BF16)
    vc_ref[0] = one(rv_ref, pv_ref, w1v_ref, w2v_ref).T.astype(BF16)


def _compress(rk, rv, pk, pv, w1k, w1v, w2k, w2v):
    b = rk.shape[0]
    rw = rk.shape[2]
    r_spec = pl.BlockSpec((1, N_CMP_PAD, rw), lambda i: (i, 0, 0))
    o_spec = pl.BlockSpec((1, N_CMP_PAD, NSA_KV_W), lambda i: (i, 0, 0))
    return pl.pallas_call(
        _compress_kernel,
        grid=(b,),
        in_specs=[r_spec, r_spec, _const_spec(pk.shape), _const_spec(pv.shape),
                  _const_spec(w1k.shape), _const_spec(w1v.shape),
                  _const_spec(w2k.shape), _const_spec(w2v.shape)],
        out_specs=[o_spec, o_spec],
        out_shape=[jax.ShapeDtypeStruct((b, N_CMP_PAD, NSA_KV_W), BF16)] * 2,
        compiler_params=_params(("parallel",)),
        name="compress",
    )(rk, rv, pk, pv, w1k, w1v, w2k, w2v)


def _compress_weights(pos, w1):
    half = CMP_LEN // 2
    p = pos.reshape(2, half, 1, HEAD_DIM)
    p = jnp.broadcast_to(p, (2, half, NSA_GROUPS, HEAD_DIM)).reshape(2, half * NSA_KV_W)
    w = w1.reshape(2, half, HEAD_DIM, CMP_HIDDEN)
    eye = jnp.eye(NSA_GROUPS, dtype=w1.dtype)
    wbd = jnp.einsum("ajdm,gk->ajgdkm", w, eye).reshape(2, half * NSA_KV_W, NSA_GROUPS * CMP_HIDDEN)
    return p.astype(F32), wbd.astype(BF16)


def _memkv_kernel(m_ref, g_ref, w_ref, k_ref, v_ref):
    h = _rms(m_ref[...], g_ref[...]).astype(BF16)
    k_ref[...] = _dot(h, w_ref[:, :MEM_W]).astype(BF16)
    v_ref[...] = _dot(h, w_ref[:, MEM_W:]).astype(BF16)


def _memkv(mem2, g, w, tm=256):
    m = mem2.shape[0]
    o_spec = pl.BlockSpec((tm, MEM_W), lambda i: (i, 0))
    return pl.pallas_call(
        _memkv_kernel,
        grid=(m // tm,),
        in_specs=[pl.BlockSpec((tm, D_MODEL), lambda i: (i, 0)), _const_spec((1, D_MODEL)),
                  _const_spec((D_MODEL, 2 * MEM_W))],
        out_specs=[o_spec, o_spec],
        out_shape=[jax.ShapeDtypeStruct((m, MEM_W), BF16)] * 2,
        compiler_params=_params(("parallel",)),
        name="memkv",
    )(mem2, g, w)


def _expand_kernel(idx_ref, bias_ref, o_ref, *, head0, n_heads):
    rows, cols = idx_ref.shape

    def body(i, carry):
        r = pl.multiple_of(i * SUBLANES, SUBLANES)
        for c0 in range(0, cols, TILE):
            idx = idx_ref[pl.ds(r, SUBLANES), c0:c0 + TILE]
            out = [jnp.full(idx.shape, NEG_INF, F32)] * n_heads
            for bkt in range(REL_BUCKETS):
                hit = idx == bkt
                out = [jnp.where(hit, bias_ref[bkt, head0 + h], out[h]) for h in range(n_heads)]
            for h in range(n_heads):
                o_ref[h, pl.ds(r, SUBLANES), c0:c0 + TILE] = out[h]
        return carry

    lax.fori_loop(0, rows // SUBLANES, body, 0)


def _expand(idx, rel_bias, head0, n_heads):
    rows, cols = idx.shape
    return pl.pallas_call(
        functools.partial(_expand_kernel, head0=head0, n_heads=n_heads),
        in_specs=[pl.BlockSpec(memory_space=pltpu.VMEM), pl.BlockSpec(memory_space=pltpu.SMEM)],
        out_specs=pl.BlockSpec(memory_space=pltpu.VMEM),
        out_shape=jax.ShapeDtypeStruct((n_heads, rows, cols), F32),
        compiler_params=pltpu.CompilerParams(vmem_limit_bytes=VMEM_LIMIT),
        name="bias_expand",
    )(idx, rel_bias)


def _t5_bucket_np(dist):
    dist = np.maximum(dist, 0)
    max_exact = REL_BUCKETS // 2
    logd = np.log(np.maximum(dist, 1).astype(np.float32) / max_exact) / math.log(REL_MAX_DIST / max_exact)
    large = np.minimum(max_exact + (logd * (REL_BUCKETS - max_exact)).astype(np.int32), REL_BUCKETS - 1)
    return np.where(dist < max_exact, dist, large).astype(np.int32)


def _bucket_tables(s):
    j = np.arange(TILE)[:, None]
    i = np.arange(TILE)[None, :]
    assert TILE + 1 >= REL_MAX_DIST
    tiles = []
    for d in range(3):
        dist = d * TILE + i - j
        tiles.append(np.where(dist >= 0, _t5_bucket_np(dist), MASKED_BUCKET))
    dist1 = TILE + i - j
    win = np.where(dist1 < WINDOW, _t5_bucket_np(dist1), MASKED_BUCKET)
    n_cmp = (s - CMP_LEN) // CMP_STRIDE + 1
    c = np.arange(N_CMP_PAD)[:, None]
    dist_c = np.arange(s)[None, :] - (c * CMP_STRIDE + CMP_LEN - 1)
    cmp_idx = np.where((dist_c >= 0) & (c < n_cmp), _t5_bucket_np(dist_c), MASKED_BUCKET)
    as_i32 = lambda a: jnp.asarray(a.astype(np.int32))
    return as_i32(np.concatenate(tiles, axis=0)), as_i32(win), as_i32(cmp_idx)


def _overlap_table(s):
    n_cmp = (s - CMP_LEN) // CMP_STRIDE + 1
    n_sel = s // SEL_BLOCK
    cs = np.arange(n_cmp) * CMP_STRIDE
    ss = np.arange(n_sel) * SEL_BLOCK
    ov = np.clip(np.minimum(cs[:, None] + CMP_LEN, ss[None, :] + SEL_BLOCK)
                 - np.maximum(cs[:, None], ss[None, :]), 0, None).astype(np.float32) / CMP_LEN
    ovt = np.zeros((n_sel, N_CMP_PAD), np.float32)
    ovt[:, :n_cmp] = ov.T
    return jnp.asarray(ovt, BF16)


def _store_v_aug(vt_sc, idx, vt):
    ones = jnp.ones((BF16_ROWS, vt.shape[1]), BF16)
    vt_sc[idx] = jnp.concatenate([vt.astype(BF16), ones], axis=0)


def _flash_init(m_ref, acc_ref):
    m_ref[...] = jnp.full(m_ref.shape, NEG_INF, F32)
    acc_ref[...] = jnp.zeros(acc_ref.shape, F32)


def _flash_update(slot, q, k, vt, add, m_ref, acc_ref):
    s = _dot_nt(k, q) + add
    m_old = m_ref[slot]
    m_new = jnp.maximum(m_old, jnp.max(s, axis=0, keepdims=True))
    alpha = jnp.exp(m_old - m_new)
    p = jnp.exp(s - m_new).astype(BF16)
    acc_ref[slot] = alpha * acc_ref[slot] + _dot(vt, p)
    m_ref[slot] = m_new


def _softmax_av(s_list, vt_list):
    m = s_list[0].max(axis=0, keepdims=True)
    for s in s_list[1:]:
        m = jnp.maximum(m, s.max(axis=0, keepdims=True))
    acc = None
    for s, vt in zip(s_list, vt_list):
        part = _dot(vt, jnp.exp(s - m).astype(BF16))
        acc = part if acc is None else acc + part
    return acc


def _normalize(acc):
    return acc[:HEAD_DIM] / acc[HEAD_DIM:HEAD_DIM + 1]


def _rank_before(score, n_cand):
    blk = lax.broadcasted_iota(jnp.int32, score.shape, 0)
    rank = jnp.zeros(score.shape, F32)
    for m in range(n_cand):
        row = score[m:m + 1, :]
        tie = jnp.where(blk > m, 1.0, 0.0)
        rank = rank + jnp.where(row > score, 1.0, 0.0) + jnp.where(row == score, tie, 0.0)
    return rank


def _nsa_kernel(q_ref, gn_ref, kc_ref, vct_ref, ks_ref, vs_ref, kw_ref, vw_ref,
                bct_ref, tt_ref, twt_ref, ovt_ref, o_ref,
                vst_sc, vwt_sc, sel_sc, m_sc, acc_sc, ot_sc):
    qi = pl.program_id(1)
    nt = ks_ref.shape[1]
    n_sel = ovt_ref.shape[0]
    blocks_per_tile = TILE // SEL_BLOCK

    @pl.when(qi == 0)
    def _():
        for kt in range(nt):
            vs_t = vs_ref[0, kt].astype(F32).T
            vw_t = vw_ref[0, kt].astype(F32).T
            for g in range(NSA_GROUPS):
                _store_v_aug(vst_sc, (kt, g), vs_t[g * HEAD_DIM:(g + 1) * HEAD_DIM])
                _store_v_aug(vwt_sc, (kt, g), vw_t[g * HEAD_DIM:(g + 1) * HEAD_DIM])

    pos = lax.broadcasted_iota(jnp.int32, (1, TILE), 1) + qi * TILE
    cur = pos // SEL_BLOCK
    has_cmp = pos >= CMP_LEN - 1
    blk = lax.broadcasted_iota(jnp.int32, (n_sel, TILE), 0)
    prev = jnp.maximum(qi - 1, 0)
    no_prev = jnp.where(qi == 0, NEG_INF, 0.0).astype(F32)
    gates = gn_ref[...].T

    for g in range(NSA_GROUPS):
        gsl = slice(g * HEAD_DIM, (g + 1) * HEAD_DIM)
        heads = [g * NSA_HPG + j for j in range(NSA_HPG)]
        qs = [q_ref[:, h * HEAD_DIM:(h + 1) * HEAD_DIM] for h in heads]
        rows = [slice(h * HEAD_DIM, (h + 1) * HEAD_DIM) for h in heads]

        kc = kc_ref[0, :, gsl]
        vct = vct_ref[0, gsl, :]
        psum = jnp.zeros((N_CMP_PAD, TILE), F32)
        for j, h in enumerate(heads):
            s = _dot_nt(kc, qs[j]) + bct_ref[h]
            e = jnp.exp(s - jnp.max(s, axis=0, keepdims=True))
            p = jnp.where(has_cmp, e / jnp.sum(e, axis=0, keepdims=True), 0.0)
            psum = psum + p
            ot_sc[rows[j], :] = gates[3 * h:3 * h + 1, :] * _dot(vct, p.astype(BF16))

        p_hi, p_lo = _split_bf16(psum)
        imp = _dot(ovt_ref[...], p_hi) + _dot(ovt_ref[...], p_lo)
        forced = (blk == 0) | (blk == cur) | (blk == cur - 1)
        score = jnp.where(forced, FORCE_SCORE, jnp.where(blk <= cur, imp, NEG_INF))
        rank = _rank_before(score, n_sel)
        sel_sc[...] = jnp.where(rank < SEL_TOPN, jnp.where(score > NEG_INF / 2, 0.0, NEG_INF), NEG_INF)

        _flash_init(m_sc, acc_sc)

        def sel_tile(kt, d):
            mask = jnp.concatenate(
                [jnp.broadcast_to(sel_sc[pl.ds(kt * blocks_per_tile + r, 1), :], (SEL_BLOCK, TILE))
                 for r in range(blocks_per_tile)], axis=0)
            k = ks_ref[0, kt, :, gsl]
            vt = vst_sc[kt, g]
            for j, h in enumerate(heads):
                _flash_update(j, qs[j], k, vt, tt_ref[h, d] + mask, m_sc, acc_sc)

        sel_tile(qi, 0)

        def sel_body(kt, carry):
            sel_tile(kt, jnp.minimum(qi - kt, 2))
            return carry

        lax.fori_loop(0, qi, sel_body, 0)

        k_prev = kw_ref[0, prev, :, gsl]
        k_own = kw_ref[0, qi, :, gsl]
        for j, h in enumerate(heads):
            s0 = _dot_nt(k_prev, qs[j]) + (twt_ref[h] + no_prev)
            s1 = _dot_nt(k_own, qs[j]) + tt_ref[h, 0]
            o_win = _normalize(_softmax_av([s0, s1], [vwt_sc[prev, g], vwt_sc[qi, g]]))
            o_sel = _normalize(acc_sc[j])
            ot_sc[rows[j], :] = (ot_sc[rows[j], :] + gates[3 * h + 1:3 * h + 2, :] * o_sel
                                 + gates[3 * h + 2:3 * h + 3, :] * o_win)

    o_ref[...] = ot_sc[...].T.astype(BF16)


def _nsa(b, s, qn, gn, kc, vct, ks, vs, kw, vw, bias_cmp, t_nsa, t_win, ovt):
    nt = s // TILE
    n_sel = s // SEL_BLOCK
    kv_spec = pl.BlockSpec((1, nt, TILE, NSA_KV_W), lambda i, j: (i, 0, 0, 0))
    c_spec = pl.BlockSpec((1, N_CMP_PAD, NSA_KV_W), lambda i, j: (i, 0, 0))
    tile4 = lambda a: a.reshape(b, nt, TILE, NSA_KV_W)
    return pl.pallas_call(
        _nsa_kernel,
        grid=(b, nt),
        in_specs=[pl.BlockSpec((TILE, NSA_Q_W), lambda i, j: (i * nt + j, 0)),
                  pl.BlockSpec((TILE, GATE_PAD), lambda i, j: (i * nt + j, 0)),
                  c_spec, c_spec, kv_spec, kv_spec, kv_spec, kv_spec,
                  pl.BlockSpec((NSA_HEADS, N_CMP_PAD, TILE), lambda i, j: (0, 0, j)),
                  _const_spec(t_nsa.shape), _const_spec(t_win.shape), _const_spec(ovt.shape)],
        out_specs=pl.BlockSpec((TILE, NSA_Q_W), lambda i, j: (i * nt + j, 0)),
        out_shape=jax.ShapeDtypeStruct((b * s, NSA_Q_W), BF16),
        scratch_shapes=[pltpu.VMEM((nt, NSA_GROUPS, V_AUG, TILE), BF16),
                        pltpu.VMEM((nt, NSA_GROUPS, V_AUG, TILE), BF16),
                        pltpu.VMEM((n_sel, TILE), F32),
                        pltpu.VMEM((NSA_HPG, 1, TILE), F32),
                        pltpu.VMEM((NSA_HPG, V_AUG, TILE), F32),
                        pltpu.VMEM((NSA_Q_W, TILE), F32)],
        compiler_params=_params(("arbitrary", "arbitrary")),
        name="nsa",
    )(qn, gn, kc, vct, tile4(ks), tile4(vs), tile4(kw), tile4(vw), bias_cmp, t_nsa, t_win, ovt)


def _moba_kernel(qm_ref, km_ref, vm_ref, qx_ref, mk_ref, mv_ref, tt_ref, om_ref, ox_ref,
                 vmt_sc, mvt_sc, kmean_sc, sel_sc, m_sc, acc_sc, ot_sc):
    c = pl.program_id(1)
    nt = km_ref.shape[1]
    hsls = [slice(h * HEAD_DIM, (h + 1) * HEAD_DIM) for h in range(MOBA_HEADS)]

    @pl.when(c == 0)
    def _():
        kmean_sc[...] = jnp.zeros(kmean_sc.shape, F32)
        for n in range(nt):
            kmean_sc[n:n + 1, :] = jnp.sum(km_ref[0, n].astype(F32), axis=0, keepdims=True) * (1.0 / MOBA_BLOCK)
            vt = vm_ref[0, n].astype(F32).T
            for h in range(MOBA_HEADS):
                _store_v_aug(vmt_sc, (n, h), vt[hsls[h]])
        mvt = mv_ref[0].astype(F32).T
        for h in range(MOBA_HEADS):
            _store_v_aug(mvt_sc, h, mvt[hsls[h]])

    blk = lax.broadcasted_iota(jnp.int32, (kmean_sc.shape[0], TILE), 0)
    _flash_init(m_sc, acc_sc)
    qs = [qm_ref[:, hsl] for hsl in hsls]
    for h in range(MOBA_HEADS):
        km_hi, km_lo = _split_bf16(kmean_sc[:, hsls[h]])
        gate = _dot_nt(km_hi, qs[h]) + _dot_nt(km_lo, qs[h])
        score = jnp.where(blk < c, gate, NEG_INF * SCALE)
        rank = _rank_before(score, nt)
        sel_sc[h] = jnp.where(rank < MOBA_TOPK, jnp.where(score > NEG_INF * SCALE / 2, 0.0, NEG_INF), NEG_INF)
        _flash_update(h, qs[h], km_ref[0, c, :, hsls[h]], vmt_sc[c, h], tt_ref[h, 0], m_sc, acc_sc)

    def body(n, carry):
        d = jnp.minimum(c - n, 2)
        for h in range(MOBA_HEADS):
            add = tt_ref[h, d] + sel_sc[h, pl.ds(n, 1), :]
            _flash_update(h, qs[h], km_ref[0, n, :, hsls[h]], vmt_sc[n, h], add, m_sc, acc_sc)
        return carry

    lax.fori_loop(0, c, body, 0)

    for h in range(MOBA_HEADS):
        ot_sc[hsls[h], :] = _normalize(acc_sc[h])
    om_ref[...] = ot_sc[...].T.astype(BF16)

    for h in range(MOBA_HEADS):
        s = _dot_nt(mk_ref[0, :, hsls[h]], qx_ref[:, hsls[h]])
        ot_sc[hsls[h], :] = _normalize(_softmax_av([s], [mvt_sc[h]]))
    ox_ref[...] = ot_sc[...].T.astype(BF16)


def _moba(b, s, qm, km, vm, qx, mk, mv, t_moba):
    nt = s // TILE
    mem_len = mk.shape[0] // b
    assert MOBA_TOPK <= nt - 1 and nt <= BF16_ROWS
    q_spec = pl.BlockSpec((TILE, MOBA_W), lambda i, j: (i * nt + j, 0))
    kv_spec = pl.BlockSpec((1, nt, TILE, MOBA_W), lambda i, j: (i, 0, 0, 0))
    mem_spec = pl.BlockSpec((1, mem_len, MEM_W), lambda i, j: (i, 0, 0))
    return pl.pallas_call(
        _moba_kernel,
        grid=(b, nt),
        in_specs=[q_spec, kv_spec, kv_spec, q_spec, mem_spec, mem_spec, _const_spec(t_moba.shape)],
        out_specs=[q_spec, q_spec],
        out_shape=[jax.ShapeDtypeStruct((b * s, MOBA_W), BF16), jax.ShapeDtypeStruct((b * s, MEM_W), BF16)],
        scratch_shapes=[pltpu.VMEM((nt, MOBA_HEADS, V_AUG, TILE), BF16),
                        pltpu.VMEM((MEM_HEADS, V_AUG, mem_len), BF16),
                        pltpu.VMEM((BF16_ROWS, MOBA_W), F32),
                        pltpu.VMEM((MOBA_HEADS, BF16_ROWS, TILE), F32),
                        pltpu.VMEM((MOBA_HEADS, 1, TILE), F32),
                        pltpu.VMEM((MOBA_HEADS, V_AUG, TILE), F32),
                        pltpu.VMEM((MOBA_W, TILE), F32)],
        compiler_params=_params(("arbitrary", "arbitrary")),
        name="moba",
    )(qm, km.reshape(b, nt, TILE, MOBA_W), vm.reshape(b, nt, TILE, MOBA_W), qx,
      mk.reshape(b, mem_len, MEM_W), mv.reshape(b, mem_len, MEM_W), t_moba)


def _mix_kernel(x_ref, on_ref, om_ref, ox_ref, g_pre_ref, g_post_ref, wg_ref, wn_ref, wm_ref, wx_ref,
                wo_ref, o_ref):
    x = x_ref[...]
    h = _rms(x, g_pre_ref[...]).astype(BF16)
    merged = jax.nn.sigmoid(_dot(h, wg_ref[:, :D_MODEL])) * _dot(on_ref[...], wn_ref[...])
    merged = merged + jax.nn.sigmoid(_dot(h, wg_ref[:, D_MODEL:2 * D_MODEL])) * _dot(om_ref[...], wm_ref[...])
    merged = merged + jax.nn.sigmoid(_dot(h, wg_ref[:, 2 * D_MODEL:])) * _dot(ox_ref[...], wx_ref[...])
    y = _dot(merged.astype(BF16), wo_ref[...])
    o_ref[...] = x + _rms(y, g_post_ref[...])


def _mix(x2, o_nsa, o_moba, o_mem, g_pre, g_post, w_gates, w_nsa_o, w_moba_o, w_mem_o, w_mix_out, tm=256):
    m = x2.shape[0]
    row = lambda w: pl.BlockSpec((tm, w), lambda i: (i, 0))
    return pl.pallas_call(
        _mix_kernel,
        grid=(m // tm,),
        in_specs=[row(D_MODEL), row(NSA_Q_W), row(MOBA_W), row(MEM_W),
                  _const_spec((1, D_MODEL)), _const_spec((1, D_MODEL)),
                  _const_spec(w_gates.shape), _const_spec(w_nsa_o.shape), _const_spec(w_moba_o.shape),
                  _const_spec(w_mem_o.shape), _const_spec(w_mix_out.shape)],
        out_specs=row(D_MODEL),
        out_shape=jax.ShapeDtypeStruct((m, D_MODEL), F32),
        compiler_params=_params(("parallel",)),
        name="mix",
    )(x2, o_nsa, o_moba, o_mem, g_pre, g_post, w_gates, w_nsa_o, w_moba_o, w_mem_o, w_mix_out)


FFN_CHUNK = 256


def _ffn_kernel(x_ref, g_pre_ref, g_post_ref, wg_ref, wu_ref, wd_ref, o_ref, a_sc):
    x = x_ref[...]
    h = _rms(x, g_pre_ref[...]).astype(BF16)
    d_ff = wg_ref.shape[1]
    for j in range(d_ff // FFN_CHUNK):
        sl = slice(j * FFN_CHUNK, (j + 1) * FFN_CHUNK)
        a_sc[:, sl] = (jax.nn.silu(_dot(h, wg_ref[:, sl])) * _dot(h, wu_ref[:, sl])).astype(BF16)
    f = _dot(a_sc[...], wd_ref[...])
    o_ref[...] = x + _rms(f, g_post_ref[...])


def _ffn(x2, g_pre, g_post, wg, wu, wd, tm=256):
    m = x2.shape[0]
    d_ff = wg.shape[1]
    return pl.pallas_call(
        _ffn_kernel,
        grid=(m // tm,),
        in_specs=[pl.BlockSpec((tm, D_MODEL), lambda i: (i, 0)),
                  _const_spec((1, D_MODEL)), _const_spec((1, D_MODEL)),
                  _const_spec(wg.shape), _const_spec(wu.shape), _const_spec(wd.shape)],
        out_specs=pl.BlockSpec((tm, D_MODEL), lambda i: (i, 0)),
        out_shape=jax.ShapeDtypeStruct((m, D_MODEL), F32),
        scratch_shapes=[pltpu.VMEM((tm, d_ff), BF16)],
        compiler_params=_params(("parallel",)),
        name="ffn",
    )(x2, g_pre, g_post, wg, wu, wd)


def kernel(x, mem, rel_bias, pre_mix_g, mem_norm_g, post_mix_g, w_in, cmp_pos_k, cmp_w1_k, cmp_w2_k, cmp_pos_v, cmp_w1_v, cmp_w2_v, w_mem_kv, w_nsa_o, w_moba_o, w_mem_o, w_mix_out, pre_ffn_g, post_ffn_g, w_ffn_gate, w_ffn_up, w_ffn_down):
    b, s, d_model = x.shape
    depth = w_in.shape[0]
    assert d_model == D_MODEL and s % TILE == 0 and TILE == MOBA_BLOCK == WINDOW
    assert (s - CMP_LEN) // CMP_STRIDE + 1 < N_CMP_PAD and (s // SEL_BLOCK) % SUBLANES == 0
    assert w_in.shape[2] == ATT_W + 3 * D_MODEL and rel_bias.shape == (REL_BUCKETS, N_BIAS_HEADS)

    tile_idx, win_idx, cmp_idx = _bucket_tables(s)
    rel_bias = rel_bias.astype(F32)
    t_all = _expand(tile_idx, rel_bias, 0, N_BIAS_HEADS).reshape(N_BIAS_HEADS, 3, TILE, TILE)
    t_nsa, t_moba = t_all[:NSA_HEADS], t_all[NSA_HEADS:]
    t_win = _expand(win_idx, rel_bias, 0, NSA_HEADS)
    b_cmp = _expand(cmp_idx, rel_bias, 0, NSA_HEADS)
    ovt = _overlap_table(s)
    gate_lo = NSA_Q_W + 6 * NSA_KV_W
    rows_per_chunk = CMP_STRIDE * NSA_KV_W

    x2 = x.reshape(b * s, D_MODEL)
    mem2 = mem.reshape(-1, D_MODEL)
    for l in range(depth):
        w_att = jnp.concatenate(
            [w_in[l, :, :gate_lo + NSA_GATE_W],
             jnp.zeros((D_MODEL, GATE_PAD - NSA_GATE_W), w_in.dtype),
             w_in[l, :, gate_lo + NSA_GATE_W:ATT_W]], axis=1).astype(BF16)
        w_gates = w_in[l, :, ATT_W:].astype(BF16)
        row = lambda v: v[l].reshape(1, D_MODEL)

        qn, kc_raw, vc_raw, ks, vs, kw, vw, gn, qm, km, vm, qx = _inproj(x2, row(pre_mix_g), w_att)

        pk, w1k = _compress_weights(cmp_pos_k[l], cmp_w1_k[l])
        pv, w1v = _compress_weights(cmp_pos_v[l], cmp_w1_v[l])
        kc, vct = _compress(kc_raw.reshape(b, s // CMP_STRIDE, rows_per_chunk),
                            vc_raw.reshape(b, s // CMP_STRIDE, rows_per_chunk),
                            pk, pv, w1k, w1v, cmp_w2_k[l].astype(BF16), cmp_w2_v[l].astype(BF16))

        mk, mv = _memkv(mem2, row(mem_norm_g), w_mem_kv[l].astype(BF16))

        o_nsa = _nsa(b, s, qn, gn, kc, vct, ks, vs, kw, vw, b_cmp, t_nsa, t_win, ovt)
        o_moba, o_mem = _moba(b, s, qm, km, vm, qx, mk, mv, t_moba)

        x2 = _mix(x2, o_nsa, o_moba, o_mem, row(pre_mix_g), row(post_mix_g), w_gates,
                  w_nsa_o[l].astype(BF16), w_moba_o[l].astype(BF16), w_mem_o[l].astype(BF16),
                  w_mix_out[l].astype(BF16))
        x2 = _ffn(x2, row(pre_ffn_g), row(post_ffn_g), w_ffn_gate[l].astype(BF16),
                  w_ffn_up[l].astype(BF16), w_ffn_down[l].astype(BF16))
    return x2.reshape(b, s, D_MODEL)
```

```python
import functools
import math

import numpy as np
import jax
import jax.numpy as jnp
from jax import lax
from jax.experimental import pallas as pl
from jax.experimental.pallas import tpu as pltpu

F32 = jnp.float32
BF16 = jnp.bfloat16

D_MODEL = 1024
HEAD_DIM = 64
SCALE = HEAD_DIM ** -0.5
NSA_HEADS = 8
NSA_GROUPS = 2
NSA_HPG = NSA_HEADS // NSA_GROUPS
CMP_LEN = 32
CMP_STRIDE = 16
CMP_HIDDEN = 128
SEL_BLOCK = 64
SEL_TOPN = 8
WINDOW = 256
MOBA_HEADS = 4
MOBA_BLOCK = 256
MOBA_TOPK = 3
MEM_HEADS = 4
REL_BUCKETS = 32
REL_MAX_DIST = 128
N_BIAS_HEADS = NSA_HEADS + MOBA_HEADS
RMS_EPS = 1e-6
NEG_INF = -1e30
FORCE_SCORE = 1e4

NSA_Q_W = NSA_HEADS * HEAD_DIM
NSA_KV_W = NSA_GROUPS * HEAD_DIM
NSA_GATE_W = NSA_HEADS * 3
MOBA_W = MOBA_HEADS * HEAD_DIM
MEM_W = MEM_HEADS * HEAD_DIM
ATT_W = NSA_Q_W + 6 * NSA_KV_W + NSA_GATE_W + 3 * MOBA_W + MEM_W
LANES = 128
SUBLANES = 8
BF16_ROWS = 16
GATE_PAD = LANES
TILE = 256
N_CMP_PAD = 128
V_AUG = HEAD_DIM + BF16_ROWS
MASKED_BUCKET = REL_BUCKETS
VMEM_LIMIT = 56 * 1024 * 1024


def _dot(a, b):
    return jnp.dot(a, b, preferred_element_type=F32)


def _dot_nt(a, b):
    return lax.dot_general(a, b, (((1,), (1,)), ((), ())), preferred_element_type=F32)


def _split_bf16(x):
    hi = x.astype(BF16)
    lo = (x - hi.astype(F32)).astype(BF16)
    return hi, lo


def _rms(x, g):
    return x * lax.rsqrt(jnp.mean(x * x, axis=-1, keepdims=True) + RMS_EPS) * g


def _params(sem):
    return pltpu.CompilerParams(dimension_semantics=sem, vmem_limit_bytes=VMEM_LIMIT)


def _const_spec(shape):
    nd = len(shape)
    return pl.BlockSpec(shape, lambda *_: (0,) * nd)


_INPROJ_OUTS = (
    ("qn", NSA_Q_W, BF16, True),
    ("kc", NSA_KV_W, F32, False), ("vc", NSA_KV_W, F32, False),
    ("ks", NSA_KV_W, BF16, False), ("vs", NSA_KV_W, BF16, False),
    ("kw", NSA_KV_W, BF16, False), ("vw", NSA_KV_W, BF16, False),
    ("gn", GATE_PAD, F32, False),
    ("qm", MOBA_W, BF16, True), ("km", MOBA_W, BF16, False), ("vm", MOBA_W, BF16, False),
    ("qx", MEM_W, BF16, True),
)
_INPROJ_W = sum(o[1] for o in _INPROJ_OUTS)


def _inproj_kernel(x_ref, g_ref, w_ref, *out_refs):
    h = _rms(x_ref[...], g_ref[...]).astype(BF16)
    lo = 0
    for (name, width, dtype, scaled), o_ref in zip(_INPROJ_OUTS, out_refs):
        y = _dot(h, w_ref[:, lo:lo + width])
        if scaled:
            y = y * SCALE
        if name == "gn":
            y = jax.nn.sigmoid(y)
        o_ref[...] = y.astype(dtype)
        lo += width


def _inproj(x2, g, w, tm=256):
    m = x2.shape[0]
    return pl.pallas_call(
        _inproj_kernel,
        grid=(m // tm,),
        in_specs=[pl.BlockSpec((tm, D_MODEL), lambda i: (i, 0)),
                  _const_spec((1, D_MODEL)),
                  _const_spec((D_MODEL, _INPROJ_W))],
        out_specs=[pl.BlockSpec((tm, o[1]), lambda i: (i, 0)) for o in _INPROJ_OUTS],
        out_shape=[jax.ShapeDtypeStruct((m, o[1]), o[2]) for o in _INPROJ_OUTS],
        compiler_params=_params(("parallel",)),
        name="inproj",
    )(x2, g, w)


def _compress_kernel(rk_ref, rv_ref, pk_ref, pv_ref, w1k_ref, w1v_ref, w2k_ref, w2v_ref, kc_ref, vc_ref):
    def one(r_ref, p_ref, w1_ref, w2_ref):
        r = r_ref[0]
        top = _dot((r + p_ref[0:1, :]).astype(BF16), w1_ref[0])
        bot = _dot((r + p_ref[1:2, :]).astype(BF16), w1_ref[1])
        hid = top + pltpu.roll(bot, N_CMP_PAD - 1, 0)
        act = jax.nn.gelu(hid).astype(BF16)
        return jnp.concatenate(
            [_dot(act[:, g * CMP_HIDDEN:(g + 1) * CMP_HIDDEN], w2_ref[...]) for g in range(NSA_GROUPS)], axis=1)

    kc_ref[0] = one(rk_ref, pk_ref, w1k_ref, w2k_ref).astype(BF16)
    vc_ref[0] = one(rv_ref, pv_ref, w1v_ref, w2v_ref).T.astype(BF16)


def _compress(rk, rv, pk, pv, w1k, w1v, w2k, w2v):
    b = rk.shape[0]
    rw = rk.shape[2]
    r_spec = pl.BlockSpec((1, N_CMP_PAD, rw), lambda i: (i, 0, 0))
    o_spec = pl.BlockSpec((1, N_CMP_PAD, NSA_KV_W), lambda i: (i, 0, 0))
    return pl.pallas_call(
        _compress_kernel,
        grid=(b,),
        in_specs=[r_spec, r_spec, _const_spec(pk.shape), _const_spec(pv.shape),
                  _const_spec(w1k.shape), _const_spec(w1v.shape),
                  _const_spec(w2k.shape), _const_spec(w2v.shape)],
        out_specs=[o_spec, o_spec],
        out_shape=[jax.ShapeDtypeStruct((b, N_CMP_PAD, NSA_KV_W), BF16)] * 2,
        compiler_params=_params(("parallel",)),
        name="compress",
    )(rk, rv, pk, pv, w1k, w1v, w2k, w2v)


def _compress_weights(pos, w1):
    half = CMP_LEN // 2
    p = pos.reshape(2, half, 1, HEAD_DIM)
    p = jnp.broadcast_to(p, (2, half, NSA_GROUPS, HEAD_DIM)).reshape(2, half * NSA_KV_W)
    w = w1.reshape(2, half, HEAD_DIM, CMP_HIDDEN)
    eye = jnp.eye(NSA_GROUPS, dtype=w1.dtype)
    wbd = jnp.einsum("ajdm,gk->ajgdkm", w, eye).reshape(2, half * NSA_KV_W, NSA_GROUPS * CMP_HIDDEN)
    return p.astype(F32), wbd.astype(BF16)


def _memkv_kernel(m_ref, g_ref, w_ref, k_ref, v_ref):
    h = _rms(m_ref[...], g_ref[...]).astype(BF16)
    k_ref[...] = _dot(h, w_ref[:, :MEM_W]).astype(BF16)
    v_ref[...] = _dot(h, w_ref[:, MEM_W:]).astype(BF16)


def _memkv(mem2, g, w, tm=256):
    m = mem2.shape[0]
    o_spec = pl.BlockSpec((tm, MEM_W), lambda i: (i, 0))
    return pl.pallas_call(
        _memkv_kernel,
        grid=(m // tm,),
        in_specs=[pl.BlockSpec((tm, D_MODEL), lambda i: (i, 0)), _const_spec((1, D_MODEL)),
                  _const_spec((D_MODEL, 2 * MEM_W))],
        out_specs=[o_spec, o_spec],
        out_shape=[jax.ShapeDtypeStruct((m, MEM_W), BF16)] * 2,
        compiler_params=_params(("parallel",)),
        name="memkv",
    )(mem2, g, w)


def _expand_kernel(idx_ref, bias_ref, o_ref, *, head0, n_heads):
    rows, cols = idx_ref.shape

    def body(i, carry):
        r = pl.multiple_of(i * SUBLANES, SUBLANES)
        for c0 in range(0, cols, TILE):
            idx = idx_ref[pl.ds(r, SUBLANES), c0:c0 + TILE]
            out = [jnp.full(idx.shape, NEG_INF, F32)] * n_heads
            for bkt in range(REL_BUCKETS):
                hit = idx == bkt
                out = [jnp.where(hit, bias_ref[bkt, head0 + h], out[h]) for h in range(n_heads)]
            for h in range(n_heads):
                o_ref[h, pl.ds(r, SUBLANES), c0:c0 + TILE] = out[h]
        return carry

    lax.fori_loop(0, rows // SUBLANES, body, 0)


def _expand(idx, rel_bias, head0, n_heads):
    rows, cols = idx.shape
    return pl.pallas_call(
        functools.partial(_expand_kernel, head0=head0, n_heads=n_heads),
        in_specs=[pl.BlockSpec(memory_space=pltpu.VMEM), pl.BlockSpec(memory_space=pltpu.SMEM)],
        out_specs=pl.BlockSpec(memory_space=pltpu.VMEM),
        out_shape=jax.ShapeDtypeStruct((n_heads, rows, cols), F32),
        compiler_params=pltpu.CompilerParams(vmem_limit_bytes=VMEM_LIMIT),
        name="bias_expand",
    )(idx, rel_bias)


def _t5_bucket_np(dist):
    dist = np.maximum(dist, 0)
    max_exact = REL_BUCKETS // 2
    logd = np.log(np.maximum(dist, 1).astype(np.float32) / max_exact) / math.log(REL_MAX_DIST / max_exact)
    large = np.minimum(max_exact + (logd * (REL_BUCKETS - max_exact)).astype(np.int32), REL_BUCKETS - 1)
    return np.where(dist < max_exact, dist, large).astype(np.int32)


def _bucket_tables(s):
    j = np.arange(TILE)[:, None]
    i = np.arange(TILE)[None, :]
    assert TILE + 1 >= REL_MAX_DIST
    tiles = []
    for d in range(3):
        dist = d * TILE + i - j
        tiles.append(np.where(dist >= 0, _t5_bucket_np(dist), MASKED_BUCKET))
    dist1 = TILE + i - j
    win = np.where(dist1 < WINDOW, _t5_bucket_np(dist1), MASKED_BUCKET)
    n_cmp = (s - CMP_LEN) // CMP_STRIDE + 1
    c = np.arange(N_CMP_PAD)[:, None]
    dist_c = np.arange(s)[None, :] - (c * CMP_STRIDE + CMP_LEN - 1)
    cmp_idx = np.where((dist_c >= 0) & (c < n_cmp), _t5_bucket_np(dist_c), MASKED_BUCKET)
    as_i32 = lambda a: jnp.asarray(a.astype(np.int32))
    return as_i32(np.concatenate(tiles, axis=0)), as_i32(win), as_i32(cmp_idx)


def _overlap_table(s):
    n_cmp = (s - CMP_LEN) // CMP_STRIDE + 1
    n_sel = s // SEL_BLOCK
    cs = np.arange(n_cmp) * CMP_STRIDE
    ss = np.arange(n_sel) * SEL_BLOCK
    ov = np.clip(np.minimum(cs[:, None] + CMP_LEN, ss[None, :] + SEL_BLOCK)
                 - np.maximum(cs[:, None], ss[None, :]), 0, None).astype(np.float32) / CMP_LEN
    ovt = np.zeros((n_sel, N_CMP_PAD), np.float32)
    ovt[:, :n_cmp] = ov.T
    return jnp.asarray(ovt, BF16)


def _store_v_aug(vt_sc, idx, vt):
    ones = jnp.ones((BF16_ROWS, vt.shape[1]), BF16)
    vt_sc[idx] = jnp.concatenate([vt.astype(BF16), ones], axis=0)


def _flash_init(m_ref, acc_ref):
    m_ref[...] = jnp.full(m_ref.shape, NEG_INF, F32)
    acc_ref[...] = jnp.zeros(acc_ref.shape, F32)


def _lane_cat(xs):
    return jnp.concatenate(xs, axis=1)


def _flash_update(q, k, add, pv, m_ref, acc_ref):
    s = _dot_nt(k, q) + add
    m_old = m_ref[...]
    m_new = jnp.maximum(m_old, jnp.max(s, axis=0, keepdims=True))
    alpha = jnp.exp(m_old - m_new)
    p = jnp.exp(s - m_new).astype(BF16)
    acc_ref[...] = alpha * acc_ref[...] + pv(p)
    m_ref[...] = m_new


def _softmax_av(s_list, pv_list):
    m = s_list[0].max(axis=0, keepdims=True)
    for s in s_list[1:]:
        m = jnp.maximum(m, s.max(axis=0, keepdims=True))
    acc = None
    for s, pv in zip(s_list, pv_list):
        part = pv(jnp.exp(s - m).astype(BF16))
        acc = part if acc is None else acc + part
    return acc


def _normalize(acc):
    return acc[:HEAD_DIM] / acc[HEAD_DIM:HEAD_DIM + 1]


def _rank_before(score, n_cand):
    blk = lax.broadcasted_iota(jnp.int32, score.shape, 0)
    rank = jnp.zeros(score.shape, F32)
    for m in range(n_cand):
        row = score[m:m + 1, :]
        tie = jnp.where(blk > m, 1.0, 0.0)
        rank = rank + jnp.where(row > score, 1.0, 0.0) + jnp.where(row == score, tie, 0.0)
    return rank


def _nsa_kernel(q_ref, gn_ref, kc_ref, vct_ref, ks_ref, vs_ref, kw_ref, vw_ref,
                bct_ref, tt_ref, twt_ref, ovt_ref, o_ref,
                vst_sc, vwt_sc, sel_sc, m_sc, acc_sc, ot_sc):
    qi = pl.program_id(1)
    nt = ks_ref.shape[1]
    n_sel = ovt_ref.shape[0]
    blocks_per_tile = TILE // SEL_BLOCK

    @pl.when(qi == 0)
    def _():
        for kt in range(nt):
            vs_t = vs_ref[0, kt].astype(F32).T
            vw_t = vw_ref[0, kt].astype(F32).T
            for g in range(NSA_GROUPS):
                _store_v_aug(vst_sc, (kt, g), vs_t[g * HEAD_DIM:(g + 1) * HEAD_DIM])
                _store_v_aug(vwt_sc, (kt, g), vw_t[g * HEAD_DIM:(g + 1) * HEAD_DIM])

    pos = lax.broadcasted_iota(jnp.int32, (1, TILE), 1) + qi * TILE
    cur = pos // SEL_BLOCK
    has_cmp = pos >= CMP_LEN - 1
    blk = lax.broadcasted_iota(jnp.int32, (n_sel, TILE), 0)
    prev = jnp.maximum(qi - 1, 0)
    no_prev = jnp.where(qi == 0, NEG_INF, 0.0).astype(F32)
    gates = gn_ref[...].T

    for g in range(NSA_GROUPS):
        gsl = slice(g * HEAD_DIM, (g + 1) * HEAD_DIM)
        heads = [g * NSA_HPG + j for j in range(NSA_HPG)]
        q4 = jnp.concatenate([q_ref[:, h * HEAD_DIM:(h + 1) * HEAD_DIM] for h in heads], axis=0)
        gate = lambda branch: _lane_cat([gates[3 * h + branch:3 * h + branch + 1, :] for h in heads])

        kc = kc_ref[0, :, gsl]
        vct = vct_ref[0, gsl, :]
        s = _dot_nt(kc, q4) + _lane_cat([bct_ref[h] for h in heads])
        e = jnp.exp(s - jnp.max(s, axis=0, keepdims=True))
        p = jnp.where(_lane_cat([has_cmp] * NSA_HPG), e / jnp.sum(e, axis=0, keepdims=True), 0.0)
        psum = p[:, :TILE]
        for j in range(1, NSA_HPG):
            psum = psum + p[:, j * TILE:(j + 1) * TILE]
        o = gate(0) * _dot(vct, p.astype(BF16))

        p_hi, p_lo = _split_bf16(psum)
        imp = _dot(ovt_ref[...], p_hi) + _dot(ovt_ref[...], p_lo)
        forced = (blk == 0) | (blk == cur) | (blk == cur - 1)
        score = jnp.where(forced, FORCE_SCORE, jnp.where(blk <= cur, imp, NEG_INF))
        rank = _rank_before(score, n_sel)
        sel = jnp.where(rank < SEL_TOPN, jnp.where(score > NEG_INF / 2, 0.0, NEG_INF), NEG_INF)
        sel_sc[...] = _lane_cat([sel] * NSA_HPG)

        _flash_init(m_sc, acc_sc)

        def sel_tile(kt, d):
            mask = jnp.concatenate(
                [jnp.broadcast_to(sel_sc[pl.ds(kt * blocks_per_tile + r, 1), :], (SEL_BLOCK, NSA_HPG * TILE))
                 for r in range(blocks_per_tile)], axis=0)
            vt = vst_sc[kt, g]
            _flash_update(q4, ks_ref[0, kt, :, gsl], tt_ref[g, d] + mask, lambda pr: _dot(vt, pr), m_sc, acc_sc)

        sel_tile(qi, 0)

        def sel_body(kt, carry):
            sel_tile(kt, jnp.minimum(qi - kt, 2))
            return carry

        lax.fori_loop(0, qi, sel_body, 0)
        o = o + gate(1) * _normalize(acc_sc[...])

        s0 = _dot_nt(kw_ref[0, prev, :, gsl], q4) + (twt_ref[g] + no_prev)
        s1 = _dot_nt(kw_ref[0, qi, :, gsl], q4) + tt_ref[g, 0]
        acc_w = _softmax_av([s0, s1], [lambda pr: _dot(vwt_sc[prev, g], pr), lambda pr: _dot(vwt_sc[qi, g], pr)])
        o = o + gate(2) * _normalize(acc_w)
        for j, h in enumerate(heads):
            ot_sc[h * HEAD_DIM:(h + 1) * HEAD_DIM, :] = o[:, j * TILE:(j + 1) * TILE]

    o_ref[...] = ot_sc[...].T.astype(BF16)


def _nsa(b, s, qn, gn, kc, vct, ks, vs, kw, vw, bias_cmp, t_nsa, t_win, ovt):
    nt = s // TILE
    n_sel = s // SEL_BLOCK
    kv_spec = pl.BlockSpec((1, nt, TILE, NSA_KV_W), lambda i, j: (i, 0, 0, 0))
    c_spec = pl.BlockSpec((1, N_CMP_PAD, NSA_KV_W), lambda i, j: (i, 0, 0))
    tile4 = lambda a: a.reshape(b, nt, TILE, NSA_KV_W)
    return pl.pallas_call(
        _nsa_kernel,
        grid=(b, nt),
        in_specs=[pl.BlockSpec((TILE, NSA_Q_W), lambda i, j: (i * nt + j, 0)),
                  pl.BlockSpec((TILE, GATE_PAD), lambda i, j: (i * nt + j, 0)),
                  c_spec, c_spec, kv_spec, kv_spec, kv_spec, kv_spec,
                  pl.BlockSpec((NSA_HEADS, N_CMP_PAD, TILE), lambda i, j: (0, 0, j)),
                  _const_spec(t_nsa.shape), _const_spec(t_win.shape), _const_spec(ovt.shape)],
        out_specs=pl.BlockSpec((TILE, NSA_Q_W), lambda i, j: (i * nt + j, 0)),
        out_shape=jax.ShapeDtypeStruct((b * s, NSA_Q_W), BF16),
        scratch_shapes=[pltpu.VMEM((nt, NSA_GROUPS, V_AUG, TILE), BF16),
                        pltpu.VMEM((nt, NSA_GROUPS, V_AUG, TILE), BF16),
                        pltpu.VMEM((n_sel, NSA_HPG * TILE), F32),
                        pltpu.VMEM((1, NSA_HPG * TILE), F32),
                        pltpu.VMEM((V_AUG, NSA_HPG * TILE), F32),
                        pltpu.VMEM((NSA_Q_W, TILE), F32)],
        compiler_params=_params(("arbitrary", "arbitrary")),
        name="nsa",
    )(qn, gn, kc, vct, tile4(ks), tile4(vs), tile4(kw), tile4(vw), bias_cmp, t_nsa, t_win, ovt)


def _moba_kernel(qm_ref, km_ref, vm_ref, qx_ref, mk_ref, mv_ref, tt_ref, om_ref, ox_ref,
                 vmt_sc, mvt_sc, kmean_sc, sel_sc, m_sc, acc_sc, ot_sc):
    c = pl.program_id(1)
    nt = km_ref.shape[1]
    hsls = [slice(h * HEAD_DIM, (h + 1) * HEAD_DIM) for h in range(MOBA_HEADS)]

    @pl.when(c == 0)
    def _():
        kmean_sc[...] = jnp.zeros(kmean_sc.shape, F32)
        for n in range(nt):
            kmean_sc[n:n + 1, :] = jnp.sum(km_ref[0, n].astype(F32), axis=0, keepdims=True) * (1.0 / MOBA_BLOCK)
            vt = vm_ref[0, n].astype(F32).T
            for h in range(MOBA_HEADS):
                _store_v_aug(vmt_sc, (n, h), vt[hsls[h]])
        mvt = mv_ref[0].astype(F32).T
        for h in range(MOBA_HEADS):
            _store_v_aug(mvt_sc, h, mvt[hsls[h]])

    lane_head = lax.broadcasted_iota(jnp.int32, (TILE, MOBA_W), 1) // HEAD_DIM

    def block_diag(q_ref):
        q = q_ref[...].astype(F32)
        return jnp.concatenate([jnp.where(lane_head == h, q, 0.0) for h in range(MOBA_HEADS)],
                               axis=0).astype(BF16)

    def per_head_pv(vts):
        return lambda pr: _lane_cat([_dot(vts(h), pr[:, h * TILE:(h + 1) * TILE]) for h in range(MOBA_HEADS)])

    def store_heads(o_t, out_ref):
        for h in range(MOBA_HEADS):
            ot_sc[hsls[h], :] = o_t[:, h * TILE:(h + 1) * TILE]
        out_ref[...] = ot_sc[...].T.astype(BF16)

    qbd = block_diag(qm_ref)
    km_hi, km_lo = _split_bf16(kmean_sc[...])
    gate = _dot_nt(km_hi, qbd) + _dot_nt(km_lo, qbd)
    blk = lax.broadcasted_iota(jnp.int32, gate.shape, 0)
    score = jnp.where(blk < c, gate, NEG_INF * SCALE)
    rank = _rank_before(score, nt)
    sel_sc[...] = jnp.where(rank < MOBA_TOPK, jnp.where(score > NEG_INF * SCALE / 2, 0.0, NEG_INF), NEG_INF)

    _flash_init(m_sc, acc_sc)
    _flash_update(qbd, km_ref[0, c], tt_ref[0], per_head_pv(lambda h: vmt_sc[c, h]), m_sc, acc_sc)

    def body(n, carry):
        add = tt_ref[jnp.minimum(c - n, 2)] + sel_sc[pl.ds(n, 1), :]
        _flash_update(qbd, km_ref[0, n], add, per_head_pv(lambda h: vmt_sc[n, h]), m_sc, acc_sc)
        return carry

    lax.fori_loop(0, c, body, 0)
    store_heads(_normalize(acc_sc[...]), om_ref)

    s = _dot_nt(mk_ref[0], block_diag(qx_ref))
    store_heads(_normalize(_softmax_av([s], [per_head_pv(lambda h: mvt_sc[h])])), ox_ref)


def _moba(b, s, qm, km, vm, qx, mk, mv, t_moba):
    nt = s // TILE
    mem_len = mk.shape[0] // b
    assert MOBA_TOPK <= nt - 1 and nt <= BF16_ROWS
    q_spec = pl.BlockSpec((TILE, MOBA_W), lambda i, j: (i * nt + j, 0))
    kv_spec = pl.BlockSpec((1, nt, TILE, MOBA_W), lambda i, j: (i, 0, 0, 0))
    mem_spec = pl.BlockSpec((1, mem_len, MEM_W), lambda i, j: (i, 0, 0))
    return pl.pallas_call(
        _moba_kernel,
        grid=(b, nt),
        in_specs=[q_spec, kv_spec, kv_spec, q_spec, mem_spec, mem_spec, _const_spec(t_moba.shape)],
        out_specs=[q_spec, q_spec],
        out_shape=[jax.ShapeDtypeStruct((b * s, MOBA_W), BF16), jax.ShapeDtypeStruct((b * s, MEM_W), BF16)],
        scratch_shapes=[pltpu.VMEM((nt, MOBA_HEADS, V_AUG, TILE), BF16),
                        pltpu.VMEM((MEM_HEADS, V_AUG, mem_len), BF16),
                        pltpu.VMEM((BF16_ROWS, MOBA_W), F32),
                        pltpu.VMEM((BF16_ROWS, MOBA_HEADS * TILE), F32),
                        pltpu.VMEM((1, MOBA_HEADS * TILE), F32),
                        pltpu.VMEM((V_AUG, MOBA_HEADS * TILE), F32),
                        pltpu.VMEM((MOBA_W, TILE), F32)],
        compiler_params=_params(("arbitrary", "arbitrary")),
        name="moba",
    )(qm, km.reshape(b, nt, TILE, MOBA_W), vm.reshape(b, nt, TILE, MOBA_W), qx,
      mk.reshape(b, mem_len, MEM_W), mv.reshape(b, mem_len, MEM_W), t_moba)


def _mix_kernel(x_ref, on_ref, om_ref, ox_ref, g_pre_ref, g_post_ref, wg_ref, wn_ref, wm_ref, wx_ref,
                wo_ref, o_ref):
    x = x_ref[...]
    h = _rms(x, g_pre_ref[...]).astype(BF16)
    merged = jax.nn.sigmoid(_dot(h, wg_ref[:, :D_MODEL])) * _dot(on_ref[...], wn_ref[...])
    merged = merged + jax.nn.sigmoid(_dot(h, wg_ref[:, D_MODEL:2 * D_MODEL])) * _dot(om_ref[...], wm_ref[...])
    merged = merged + jax.nn.sigmoid(_dot(h, wg_ref[:, 2 * D_MODEL:])) * _dot(ox_ref[...], wx_ref[...])
    y = _dot(merged.astype(BF16), wo_ref[...])
    o_ref[...] = x + _rms(y, g_post_ref[...])


def _mix(x2, o_nsa, o_moba, o_mem, g_pre, g_post, w_gates, w_nsa_o, w_moba_o, w_mem_o, w_mix_out, tm=256):
    m = x2.shape[0]
    row = lambda w: pl.BlockSpec((tm, w), lambda i: (i, 0))
    return pl.pallas_call(
        _mix_kernel,
        grid=(m // tm,),
        in_specs=[row(D_MODEL), row(NSA_Q_W), row(MOBA_W), row(MEM_W),
                  _const_spec((1, D_MODEL)), _const_spec((1, D_MODEL)),
                  _const_spec(w_gates.shape), _const_spec(w_nsa_o.shape), _const_spec(w_moba_o.shape),
                  _const_spec(w_mem_o.shape), _const_spec(w_mix_out.shape)],
        out_specs=row(D_MODEL),
        out_shape=jax.ShapeDtypeStruct((m, D_MODEL), F32),
        compiler_params=_params(("parallel",)),
        name="mix",
    )(x2, o_nsa, o_moba, o_mem, g_pre, g_post, w_gates, w_nsa_o, w_moba_o, w_mem_o, w_mix_out)


FFN_CHUNK = 256


def _ffn_kernel(x_ref, g_pre_ref, g_post_ref, wg_ref, wu_ref, wd_ref, o_ref, a_sc):
    x = x_ref[...]
    h = _rms(x, g_pre_ref[...]).astype(BF16)
    d_ff = wg_ref.shape[1]
    for j in range(d_ff // FFN_CHUNK):
        sl = slice(j * FFN_CHUNK, (j + 1) * FFN_CHUNK)
        a_sc[:, sl] = (jax.nn.silu(_dot(h, wg_ref[:, sl])) * _dot(h, wu_ref[:, sl])).astype(BF16)
    f = _dot(a_sc[...], wd_ref[...])
    o_ref[...] = x + _rms(f, g_post_ref[...])


def _ffn(x2, g_pre, g_post, wg, wu, wd, tm=256):
    m = x2.shape[0]
    d_ff = wg.shape[1]
    return pl.pallas_call(
        _ffn_kernel,
        grid=(m // tm,),
        in_specs=[pl.BlockSpec((tm, D_MODEL), lambda i: (i, 0)),
                  _const_spec((1, D_MODEL)), _const_spec((1, D_MODEL)),
                  _const_spec(wg.shape), _const_spec(wu.shape), _const_spec(wd.shape)],
        out_specs=pl.BlockSpec((tm, D_MODEL), lambda i: (i, 0)),
        out_shape=jax.ShapeDtypeStruct((m, D_MODEL), F32),
        scratch_shapes=[pltpu.VMEM((tm, d_ff), BF16)],
        compiler_params=_params(("parallel",)),
        name="ffn",
    )(x2, g_pre, g_post, wg, wu, wd)


def kernel(x, mem, rel_bias, pre_mix_g, mem_norm_g, post_mix_g, w_in, cmp_pos_k, cmp_w1_k, cmp_w2_k, cmp_pos_v, cmp_w1_v, cmp_w2_v, w_mem_kv, w_nsa_o, w_moba_o, w_mem_o, w_mix_out, pre_ffn_g, post_ffn_g, w_ffn_gate, w_ffn_up, w_ffn_down):
    b, s, d_model = x.shape
    depth = w_in.shape[0]
    assert d_model == D_MODEL and s % TILE == 0 and TILE == MOBA_BLOCK == WINDOW
    assert (s - CMP_LEN) // CMP_STRIDE + 1 < N_CMP_PAD and (s // SEL_BLOCK) % SUBLANES == 0
    assert w_in.shape[2] == ATT_W + 3 * D_MODEL and rel_bias.shape == (REL_BUCKETS, N_BIAS_HEADS)

    tile_idx, win_idx, cmp_idx = _bucket_tables(s)
    rel_bias = rel_bias.astype(F32)
    t_all = _expand(tile_idx, rel_bias, 0, N_BIAS_HEADS).reshape(N_BIAS_HEADS, 3, TILE, TILE)
    t_nsa = t_all[:NSA_HEADS].reshape(NSA_GROUPS, NSA_HPG, 3, TILE, TILE).transpose(0, 2, 3, 1, 4)
    t_nsa = t_nsa.reshape(NSA_GROUPS, 3, TILE, NSA_HPG * TILE)
    t_moba = t_all[NSA_HEADS:].transpose(1, 2, 0, 3).reshape(3, TILE, MOBA_HEADS * TILE)
    t_win = _expand(win_idx, rel_bias, 0, NSA_HEADS).reshape(NSA_GROUPS, NSA_HPG, TILE, TILE)
    t_win = t_win.transpose(0, 2, 1, 3).reshape(NSA_GROUPS, TILE, NSA_HPG * TILE)
    b_cmp = _expand(cmp_idx, rel_bias, 0, NSA_HEADS)
    ovt = _overlap_table(s)
    gate_lo = NSA_Q_W + 6 * NSA_KV_W
    rows_per_chunk = CMP_STRIDE * NSA_KV_W

    x2 = x.reshape(b * s, D_MODEL)
    mem2 = mem.reshape(-1, D_MODEL)
    for l in range(depth):
        w_att = jnp.concatenate(
            [w_in[l, :, :gate_lo + NSA_GATE_W],
             jnp.zeros((D_MODEL, GATE_PAD - NSA_GATE_W), w_in.dtype),
             w_in[l, :, gate_lo + NSA_GATE_W:ATT_W]], axis=1).astype(BF16)
        w_gates = w_in[l, :, ATT_W:].astype(BF16)
        row = lambda v: v[l].reshape(1, D_MODEL)

        qn, kc_raw, vc_raw, ks, vs, kw, vw, gn, qm, km, vm, qx = _inproj(x2, row(pre_mix_g), w_att)

        pk, w1k = _compress_weights(cmp_pos_k[l], cmp_w1_k[l])
        pv, w1v = _compress_weights(cmp_pos_v[l], cmp_w1_v[l])
        kc, vct = _compress(kc_raw.reshape(b, s // CMP_STRIDE, rows_per_chunk),
                            vc_raw.reshape(b, s // CMP_STRIDE, rows_per_chunk),
                            pk, pv, w1k, w1v, cmp_w2_k[l].astype(BF16), cmp_w2_v[l].astype(BF16))

        mk, mv = _memkv(mem2, row(mem_norm_g), w_mem_kv[l].astype(BF16))

        o_nsa = _nsa(b, s, qn, gn, kc, vct, ks, vs, kw, vw, b_cmp, t_nsa, t_win, ovt)
        o_moba, o_mem = _moba(b, s, qm, km, vm, qx, mk, mv, t_moba)

        x2 = _mix(x2, o_nsa, o_moba, o_mem, row(pre_mix_g), row(post_mix_g), w_gates,
                  w_nsa_o[l].astype(BF16), w_moba_o[l].astype(BF16), w_mem_o[l].astype(BF16),
                  w_mix_out[l].astype(BF16))
        x2 = _ffn(x2, row(pre_ffn_g), row(post_ffn_g), w_ffn_gate[l].astype(BF16),
                  w_ffn_up[l].astype(BF16), w_ffn_down[l].astype(BF16))
    return x2.reshape(b, s, D_MODEL)
```

```python
import functools
import math

import numpy as np
import jax
import jax.numpy as jnp
from jax import lax
from jax.experimental import pallas as pl
from jax.experimental.pallas import tpu as pltpu

F32 = jnp.float32
BF16 = jnp.bfloat16

D_MODEL = 1024
HEAD_DIM = 64
SCALE = HEAD_DIM ** -0.5
NSA_HEADS = 8
NSA_GROUPS = 2
NSA_HPG = NSA_HEADS // NSA_GROUPS
CMP_LEN = 32
CMP_STRIDE = 16
CMP_HIDDEN = 128
SEL_BLOCK = 64
SEL_TOPN = 8
WINDOW = 256
MOBA_HEADS = 4
MOBA_BLOCK = 256
MOBA_TOPK = 3
MEM_HEADS = 4
REL_BUCKETS = 32
REL_MAX_DIST = 128
N_BIAS_HEADS = NSA_HEADS + MOBA_HEADS
RMS_EPS = 1e-6
NEG_INF = -1e30
FORCE_SCORE = 1e4

NSA_Q_W = NSA_HEADS * HEAD_DIM
NSA_KV_W = NSA_GROUPS * HEAD_DIM
NSA_GATE_W = NSA_HEADS * 3
MOBA_W = MOBA_HEADS * HEAD_DIM
MEM_W = MEM_HEADS * HEAD_DIM
ATT_W = NSA_Q_W + 6 * NSA_KV_W + NSA_GATE_W + 3 * MOBA_W + MEM_W
LANES = 128
SUBLANES = 8
BF16_ROWS = 16
GATE_PAD = LANES
TILE = 256
N_CMP_PAD = 128
V_AUG = HEAD_DIM + BF16_ROWS
MASKED_BUCKET = REL_BUCKETS
VMEM_LIMIT = 56 * 1024 * 1024


def _dot(a, b):
    return jnp.dot(a, b, preferred_element_type=F32)


def _dot_nt(a, b):
    return lax.dot_general(a, b, (((1,), (1,)), ((), ())), preferred_element_type=F32)


def _split_bf16(x):
    hi = x.astype(BF16)
    lo = (x - hi.astype(F32)).astype(BF16)
    return hi, lo


def _rms(x, g):
    return x * lax.rsqrt(jnp.mean(x * x, axis=-1, keepdims=True) + RMS_EPS) * g


def _params(sem):
    return pltpu.CompilerParams(dimension_semantics=sem, vmem_limit_bytes=VMEM_LIMIT)


def _const_spec(shape):
    nd = len(shape)
    return pl.BlockSpec(shape, lambda *_: (0,) * nd)


_INPROJ_OUTS = (
    ("qn", NSA_Q_W, BF16, True),
    ("kc", NSA_KV_W, F32, False), ("vc", NSA_KV_W, F32, False),
    ("ks", NSA_KV_W, BF16, False), ("vs", NSA_KV_W, BF16, False),
    ("kw", NSA_KV_W, BF16, False), ("vw", NSA_KV_W, BF16, False),
    ("gn", GATE_PAD, F32, False),
    ("qm", MOBA_W, BF16, True), ("km", MOBA_W, BF16, False), ("vm", MOBA_W, BF16, False),
    ("qx", MEM_W, BF16, True),
)
_INPROJ_W = sum(o[1] for o in _INPROJ_OUTS)


def _inproj_kernel(x_ref, g_ref, w_ref, *out_refs):
    h = _rms(x_ref[...], g_ref[...]).astype(BF16)
    lo = 0
    for (name, width, dtype, scaled), o_ref in zip(_INPROJ_OUTS, out_refs):
        y = _dot(h, w_ref[:, lo:lo + width])
        if scaled:
            y = y * SCALE
        if name == "gn":
            y = jax.nn.sigmoid(y)
        o_ref[...] = y.astype(dtype)
        lo += width


def _inproj(x2, g, w, tm=256):
    m = x2.shape[0]
    return pl.pallas_call(
        _inproj_kernel,
        grid=(m // tm,),
        in_specs=[pl.BlockSpec((tm, D_MODEL), lambda i: (i, 0)),
                  _const_spec((1, D_MODEL)),
                  _const_spec((D_MODEL, _INPROJ_W))],
        out_specs=[pl.BlockSpec((tm, o[1]), lambda i: (i, 0)) for o in _INPROJ_OUTS],
        out_shape=[jax.ShapeDtypeStruct((m, o[1]), o[2]) for o in _INPROJ_OUTS],
        compiler_params=_params(("parallel",)),
        name="inproj",
    )(x2, g, w)


def _compress_kernel(rk_ref, rv_ref, pk_ref, pv_ref, w1k_ref, w1v_ref, w2k_ref, w2v_ref, kc_ref, vc_ref):
    def one(r_ref, p_ref, w1_ref, w2_ref):
        r = r_ref[0]
        top = _dot((r + p_ref[0:1, :]).astype(BF16), w1_ref[0])
        bot = _dot((r + p_ref[1:2, :]).astype(BF16), w1_ref[1])
        hid = top + pltpu.roll(bot, N_CMP_PAD - 1, 0)
        act = jax.nn.gelu(hid).astype(BF16)
        return jnp.concatenate(
            [_dot(act[:, g * CMP_HIDDEN:(g + 1) * CMP_HIDDEN], w2_ref[...]) for g in range(NSA_GROUPS)], axis=1)

    kc_ref[0] = one(rk_ref, pk_ref, w1k_ref, w2k_ref).astype(BF16)
    vc_ref[0] = one(rv_ref, pv_ref, w1v_ref, w2v_ref).T.astype(BF16)


def _compress(rk, rv, pk, pv, w1k, w1v, w2k, w2v):
    b = rk.shape[0]
    rw = rk.shape[2]
    r_spec = pl.BlockSpec((1, N_CMP_PAD, rw), lambda i: (i, 0, 0))
    o_spec = pl.BlockSpec((1, N_CMP_PAD, NSA_KV_W), lambda i: (i, 0, 0))
    return pl.pallas_call(
        _compress_kernel,
        grid=(b,),
        in_specs=[r_spec, r_spec, _const_spec(pk.shape), _const_spec(pv.shape),
                  _const_spec(w1k.shape), _const_spec(w1v.shape),
                  _const_spec(w2k.shape), _const_spec(w2v.shape)],
        out_specs=[o_spec, o_spec],
        out_shape=[jax.ShapeDtypeStruct((b, N_CMP_PAD, NSA_KV_W), BF16)] * 2,
        compiler_params=_params(("parallel",)),
        name="compress",
    )(rk, rv, pk, pv, w1k, w1v, w2k, w2v)


def _compress_weights(pos, w1):
    half = CMP_LEN // 2
    p = pos.reshape(2, half, 1, HEAD_DIM)
    p = jnp.broadcast_to(p, (2, half, NSA_GROUPS, HEAD_DIM)).reshape(2, half * NSA_KV_W)
    w = w1.reshape(2, half, HEAD_DIM, CMP_HIDDEN)
    eye = jnp.eye(NSA_GROUPS, dtype=w1.dtype)
    wbd = jnp.einsum("ajdm,gk->ajgdkm", w, eye).reshape(2, half * NSA_KV_W, NSA_GROUPS * CMP_HIDDEN)
    return p.astype(F32), wbd.astype(BF16)


def _memkv_kernel(m_ref, g_ref, w_ref, k_ref, v_ref):
    h = _rms(m_ref[...], g_ref[...]).astype(BF16)
    k_ref[...] = _dot(h, w_ref[:, :MEM_W]).astype(BF16)
    v_ref[...] = _dot(h, w_ref[:, MEM_W:]).astype(BF16)


def _memkv(mem2, g, w, tm=256):
    m = mem2.shape[0]
    o_spec = pl.BlockSpec((tm, MEM_W), lambda i: (i, 0))
    return pl.pallas_call(
        _memkv_kernel,
        grid=(m // tm,),
        in_specs=[pl.BlockSpec((tm, D_MODEL), lambda i: (i, 0)), _const_spec((1, D_MODEL)),
                  _const_spec((D_MODEL, 2 * MEM_W))],
        out_specs=[o_spec, o_spec],
        out_shape=[jax.ShapeDtypeStruct((m, MEM_W), BF16)] * 2,
        compiler_params=_params(("parallel",)),
        name="memkv",
    )(mem2, g, w)


def _expand_kernel(idx_ref, bias_ref, o_ref, *, head0, n_heads):
    rows, cols = idx_ref.shape

    def body(i, carry):
        r = pl.multiple_of(i * SUBLANES, SUBLANES)
        for c0 in range(0, cols, TILE):
            idx = idx_ref[pl.ds(r, SUBLANES), c0:c0 + TILE]
            out = [jnp.full(idx.shape, NEG_INF, F32)] * n_heads
            for bkt in range(REL_BUCKETS):
                hit = idx == bkt
                out = [jnp.where(hit, bias_ref[bkt, head0 + h], out[h]) for h in range(n_heads)]
            for h in range(n_heads):
                o_ref[h, pl.ds(r, SUBLANES), c0:c0 + TILE] = out[h]
        return carry

    lax.fori_loop(0, rows // SUBLANES, body, 0)


def _expand(idx, rel_bias, head0, n_heads):
    rows, cols = idx.shape
    return pl.pallas_call(
        functools.partial(_expand_kernel, head0=head0, n_heads=n_heads),
        in_specs=[pl.BlockSpec(memory_space=pltpu.VMEM), pl.BlockSpec(memory_space=pltpu.SMEM)],
        out_specs=pl.BlockSpec(memory_space=pltpu.VMEM),
        out_shape=jax.ShapeDtypeStruct((n_heads, rows, cols), F32),
        compiler_params=pltpu.CompilerParams(vmem_limit_bytes=VMEM_LIMIT),
        name="bias_expand",
    )(idx, rel_bias)


def _t5_bucket_np(dist):
    dist = np.maximum(dist, 0)
    max_exact = REL_BUCKETS // 2
    logd = np.log(np.maximum(dist, 1).astype(np.float32) / max_exact) / math.log(REL_MAX_DIST / max_exact)
    large = np.minimum(max_exact + (logd * (REL_BUCKETS - max_exact)).astype(np.int32), REL_BUCKETS - 1)
    return np.where(dist < max_exact, dist, large).astype(np.int32)


def _bucket_tables(s):
    j = np.arange(TILE)[:, None]
    i = np.arange(TILE)[None, :]
    assert TILE + 1 >= REL_MAX_DIST
    tiles = []
    for d in range(3):
        dist = d * TILE + i - j
        tiles.append(np.where(dist >= 0, _t5_bucket_np(dist), MASKED_BUCKET))
    dist1 = TILE + i - j
    win = np.where(dist1 < WINDOW, _t5_bucket_np(dist1), MASKED_BUCKET)
    n_cmp = (s - CMP_LEN) // CMP_STRIDE + 1
    c = np.arange(N_CMP_PAD)[:, None]
    dist_c = np.arange(s)[None, :] - (c * CMP_STRIDE + CMP_LEN - 1)
    cmp_idx = np.where((dist_c >= 0) & (c < n_cmp), _t5_bucket_np(dist_c), MASKED_BUCKET)
    as_i32 = lambda a: jnp.asarray(a.astype(np.int32))
    return as_i32(np.concatenate(tiles, axis=0)), as_i32(win), as_i32(cmp_idx)


def _overlap_table(s):
    n_cmp = (s - CMP_LEN) // CMP_STRIDE + 1
    n_sel = s // SEL_BLOCK
    cs = np.arange(n_cmp) * CMP_STRIDE
    ss = np.arange(n_sel) * SEL_BLOCK
    ov = np.clip(np.minimum(cs[:, None] + CMP_LEN, ss[None, :] + SEL_BLOCK)
                 - np.maximum(cs[:, None], ss[None, :]), 0, None).astype(np.float32) / CMP_LEN
    ovt = np.zeros((n_sel, N_CMP_PAD), np.float32)
    ovt[:, :n_cmp] = ov.T
    return jnp.asarray(ovt, BF16)


def _store_v_aug(vt_sc, idx, vt):
    ones = jnp.ones((BF16_ROWS, vt.shape[1]), BF16)
    vt_sc[idx] = jnp.concatenate([vt.astype(BF16), ones], axis=0)


def _lane_cat(xs):
    return jnp.concatenate(xs, axis=1)


def _flash_pipelined(own, n_past, streams, s_sc, m_ref, acc_ref):
    def absorb(g, s, kt):
        m_old = m_ref[g]
        m_new = jnp.maximum(m_old, jnp.max(s, axis=0, keepdims=True))
        alpha = jnp.exp(m_old - m_new)
        p = jnp.exp(s - m_new).astype(BF16)
        acc_ref[g] = alpha * acc_ref[g] + streams[g][2](kt)(p)
        m_ref[g] = m_new

    m_ref[...] = jnp.full(m_ref.shape, NEG_INF, F32)
    acc_ref[...] = jnp.zeros(acc_ref.shape, F32)
    for g, stream in enumerate(streams):
        s_sc[0, g] = stream[0]()

    def body(i, carry):
        kt_cur = jnp.where(i == 0, own, i - 1)
        nxt = [stream[1](i) for stream in streams]
        for g in range(len(streams)):
            absorb(g, s_sc[0, g], kt_cur)
        for g in range(len(streams)):
            s_sc[0, g] = nxt[g]
        return carry

    lax.fori_loop(0, n_past, body, 0)
    kt_last = jnp.where(n_past == 0, own, n_past - 1)
    for g in range(len(streams)):
        absorb(g, s_sc[0, g], kt_last)


def _softmax_av(s_list, pv_list):
    m = s_list[0].max(axis=0, keepdims=True)
    for s in s_list[1:]:
        m = jnp.maximum(m, s.max(axis=0, keepdims=True))
    acc = None
    for s, pv in zip(s_list, pv_list):
        part = pv(jnp.exp(s - m).astype(BF16))
        acc = part if acc is None else acc + part
    return acc


def _normalize(acc):
    return acc[:HEAD_DIM] / acc[HEAD_DIM:HEAD_DIM + 1]


def _rank_before(score, n_cand):
    blk = lax.broadcasted_iota(jnp.int32, score.shape, 0)
    rank = jnp.zeros(score.shape, F32)
    for m in range(n_cand):
        row = score[m:m + 1, :]
        tie = jnp.where(blk > m, 1.0, 0.0)
        rank = rank + jnp.where(row > score, 1.0, 0.0) + jnp.where(row == score, tie, 0.0)
    return rank


def _nsa_kernel(q_ref, gn_ref, kc_ref, vct_ref, ks_ref, vs_ref, kw_ref, vw_ref,
                bct_ref, tt_ref, twt_ref, ovt_ref, o_ref,
                vst_sc, vwt_sc, q4_sc, og_sc, sel_sc, s_sc, m_sc, acc_sc, ot_sc):
    qi = pl.program_id(1)
    nt = ks_ref.shape[1]
    n_sel = ovt_ref.shape[0]
    blocks_per_tile = TILE // SEL_BLOCK

    @pl.when(qi == 0)
    def _():
        for kt in range(nt):
            vs_t = vs_ref[0, kt].astype(F32).T
            vw_t = vw_ref[0, kt].astype(F32).T
            for g in range(NSA_GROUPS):
                _store_v_aug(vst_sc, (kt, g), vs_t[g * HEAD_DIM:(g + 1) * HEAD_DIM])
                _store_v_aug(vwt_sc, (kt, g), vw_t[g * HEAD_DIM:(g + 1) * HEAD_DIM])

    pos = lax.broadcasted_iota(jnp.int32, (1, TILE), 1) + qi * TILE
    cur = pos // SEL_BLOCK
    has_cmp = pos >= CMP_LEN - 1
    blk = lax.broadcasted_iota(jnp.int32, (n_sel, TILE), 0)
    prev = jnp.maximum(qi - 1, 0)
    no_prev = jnp.where(qi == 0, NEG_INF, 0.0).astype(F32)
    gates = gn_ref[...].T

    gsls = [slice(g * HEAD_DIM, (g + 1) * HEAD_DIM) for g in range(NSA_GROUPS)]
    group_heads = [[g * NSA_HPG + j for j in range(NSA_HPG)] for g in range(NSA_GROUPS)]

    def gate(g, branch):
        return _lane_cat([gates[3 * h + branch:3 * h + branch + 1, :] for h in group_heads[g]])

    for g in range(NSA_GROUPS):
        heads = group_heads[g]
        q4 = jnp.concatenate([q_ref[:, h * HEAD_DIM:(h + 1) * HEAD_DIM] for h in heads], axis=0)
        q4_sc[g] = q4

        kc = kc_ref[0, :, gsls[g]]
        vct = vct_ref[0, gsls[g], :]
        s = _dot_nt(kc, q4) + _lane_cat([bct_ref[h] for h in heads])
        e = jnp.exp(s - jnp.max(s, axis=0, keepdims=True))
        p = jnp.where(_lane_cat([has_cmp] * NSA_HPG), e / jnp.sum(e, axis=0, keepdims=True), 0.0)
        psum = p[:, :TILE]
        for j in range(1, NSA_HPG):
            psum = psum + p[:, j * TILE:(j + 1) * TILE]
        og_sc[g] = gate(g, 0) * _dot(vct, p.astype(BF16))

        p_hi, p_lo = _split_bf16(psum)
        imp = _dot(ovt_ref[...], p_hi) + _dot(ovt_ref[...], p_lo)
        forced = (blk == 0) | (blk == cur) | (blk == cur - 1)
        score = jnp.where(forced, FORCE_SCORE, jnp.where(blk <= cur, imp, NEG_INF))
        rank = _rank_before(score, n_sel)
        sel = jnp.where(rank < SEL_TOPN, jnp.where(score > NEG_INF / 2, 0.0, NEG_INF), NEG_INF)
        sel_sc[g] = _lane_cat([sel] * NSA_HPG)

    def sel_stream(g):
        def scores(kt, d):
            mask = jnp.concatenate(
                [jnp.broadcast_to(sel_sc[g, pl.ds(kt * blocks_per_tile + r, 1), :], (SEL_BLOCK, NSA_HPG * TILE))
                 for r in range(blocks_per_tile)], axis=0)
            return _dot_nt(ks_ref[0, kt, :, gsls[g]], q4_sc[g]) + (tt_ref[g, d] + mask)

        return (lambda: scores(qi, 0),
                lambda kt: scores(kt, jnp.minimum(qi - kt, 2)),
                lambda kt: (lambda pr: _dot(vst_sc[kt, g], pr)))

    _flash_pipelined(qi, qi, [sel_stream(g) for g in range(NSA_GROUPS)], s_sc, m_sc, acc_sc)

    for g in range(NSA_GROUPS):
        q4 = q4_sc[g]
        s0 = _dot_nt(kw_ref[0, prev, :, gsls[g]], q4) + (twt_ref[g] + no_prev)
        s1 = _dot_nt(kw_ref[0, qi, :, gsls[g]], q4) + tt_ref[g, 0]
        acc_w = _softmax_av([s0, s1], [lambda pr: _dot(vwt_sc[prev, g], pr), lambda pr: _dot(vwt_sc[qi, g], pr)])
        o = og_sc[g] + gate(g, 1) * _normalize(acc_sc[g]) + gate(g, 2) * _normalize(acc_w)
        for j, h in enumerate(group_heads[g]):
            ot_sc[h * HEAD_DIM:(h + 1) * HEAD_DIM, :] = o[:, j * TILE:(j + 1) * TILE]

    o_ref[...] = ot_sc[...].T.astype(BF16)


def _nsa(b, s, qn, gn, kc, vct, ks, vs, kw, vw, bias_cmp, t_nsa, t_win, ovt):
    nt = s // TILE
    n_sel = s // SEL_BLOCK
    kv_spec = pl.BlockSpec((1, nt, TILE, NSA_KV_W), lambda i, j: (i, 0, 0, 0))
    c_spec = pl.BlockSpec((1, N_CMP_PAD, NSA_KV_W), lambda i, j: (i, 0, 0))
    tile4 = lambda a: a.reshape(b, nt, TILE, NSA_KV_W)
    return pl.pallas_call(
        _nsa_kernel,
        grid=(b, nt),
        in_specs=[pl.BlockSpec((TILE, NSA_Q_W), lambda i, j: (i * nt + j, 0)),
                  pl.BlockSpec((TILE, GATE_PAD), lambda i, j: (i * nt + j, 0)),
                  c_spec, c_spec, kv_spec, kv_spec, kv_spec, kv_spec,
                  pl.BlockSpec((NSA_HEADS, N_CMP_PAD, TILE), lambda i, j: (0, 0, j)),
                  _const_spec(t_nsa.shape), _const_spec(t_win.shape), _const_spec(ovt.shape)],
        out_specs=pl.BlockSpec((TILE, NSA_Q_W), lambda i, j: (i * nt + j, 0)),
        out_shape=jax.ShapeDtypeStruct((b * s, NSA_Q_W), BF16),
        scratch_shapes=[pltpu.VMEM((nt, NSA_GROUPS, V_AUG, TILE), BF16),
                        pltpu.VMEM((nt, NSA_GROUPS, V_AUG, TILE), BF16),
                        pltpu.VMEM((NSA_GROUPS, NSA_HPG * TILE, HEAD_DIM), BF16),
                        pltpu.VMEM((NSA_GROUPS, HEAD_DIM, NSA_HPG * TILE), F32),
                        pltpu.VMEM((NSA_GROUPS, n_sel, NSA_HPG * TILE), F32),
                        pltpu.VMEM((2, NSA_GROUPS, TILE, NSA_HPG * TILE), F32),
                        pltpu.VMEM((NSA_GROUPS, 1, NSA_HPG * TILE), F32),
                        pltpu.VMEM((NSA_GROUPS, V_AUG, NSA_HPG * TILE), F32),
                        pltpu.VMEM((NSA_Q_W, TILE), F32)],
        compiler_params=_params(("arbitrary", "arbitrary")),
        name="nsa",
    )(qn, gn, kc, vct, tile4(ks), tile4(vs), tile4(kw), tile4(vw), bias_cmp, t_nsa, t_win, ovt)


def _moba_kernel(qm_ref, km_ref, vm_ref, qx_ref, mk_ref, mv_ref, tt_ref, om_ref, ox_ref,
                 vmt_sc, mvt_sc, kmean_sc, qbd_sc, sel_sc, s_sc, m_sc, acc_sc, ot_sc):
    c = pl.program_id(1)
    nt = km_ref.shape[1]
    hsls = [slice(h * HEAD_DIM, (h + 1) * HEAD_DIM) for h in range(MOBA_HEADS)]

    @pl.when(c == 0)
    def _():
        kmean_sc[...] = jnp.zeros(kmean_sc.shape, F32)
        for n in range(nt):
            kmean_sc[n:n + 1, :] = jnp.sum(km_ref[0, n].astype(F32), axis=0, keepdims=True) * (1.0 / MOBA_BLOCK)
            vt = vm_ref[0, n].astype(F32).T
            for h in range(MOBA_HEADS):
                _store_v_aug(vmt_sc, (n, h), vt[hsls[h]])
        mvt = mv_ref[0].astype(F32).T
        for h in range(MOBA_HEADS):
            _store_v_aug(mvt_sc, h, mvt[hsls[h]])

    lane_head = lax.broadcasted_iota(jnp.int32, (TILE, MOBA_W), 1) // HEAD_DIM

    def block_diag(q_ref):
        q = q_ref[...].astype(F32)
        return jnp.concatenate([jnp.where(lane_head == h, q, 0.0) for h in range(MOBA_HEADS)],
                               axis=0).astype(BF16)

    def per_head_pv(vts):
        return lambda pr: _lane_cat([_dot(vts(h), pr[:, h * TILE:(h + 1) * TILE]) for h in range(MOBA_HEADS)])

    def store_heads(o_t, out_ref):
        for h in range(MOBA_HEADS):
            ot_sc[hsls[h], :] = o_t[:, h * TILE:(h + 1) * TILE]
        out_ref[...] = ot_sc[...].T.astype(BF16)

    qbd = block_diag(qm_ref)
    km_hi, km_lo = _split_bf16(kmean_sc[...])
    gate = _dot_nt(km_hi, qbd) + _dot_nt(km_lo, qbd)
    blk = lax.broadcasted_iota(jnp.int32, gate.shape, 0)
    score = jnp.where(blk < c, gate, NEG_INF * SCALE)
    rank = _rank_before(score, nt)
    sel_sc[...] = jnp.where(rank < MOBA_TOPK, jnp.where(score > NEG_INF * SCALE / 2, 0.0, NEG_INF), NEG_INF)

    qbd_sc[...] = qbd
    stream = (lambda: _dot_nt(km_ref[0, c], qbd_sc[...]) + tt_ref[0],
              lambda n: (_dot_nt(km_ref[0, n], qbd_sc[...])
                         + (tt_ref[jnp.minimum(c - n, 2)] + sel_sc[pl.ds(n, 1), :])),
              lambda n: per_head_pv(lambda h: vmt_sc[n, h]))
    _flash_pipelined(c, c, [stream], s_sc, m_sc, acc_sc)
    store_heads(_normalize(acc_sc[0]), om_ref)

    s = _dot_nt(mk_ref[0], block_diag(qx_ref))
    store_heads(_normalize(_softmax_av([s], [per_head_pv(lambda h: mvt_sc[h])])), ox_ref)


def _moba(b, s, qm, km, vm, qx, mk, mv, t_moba):
    nt = s // TILE
    mem_len = mk.shape[0] // b
    assert MOBA_TOPK <= nt - 1 and nt <= BF16_ROWS
    q_spec = pl.BlockSpec((TILE, MOBA_W), lambda i, j: (i * nt + j, 0))
    kv_spec = pl.BlockSpec((1, nt, TILE, MOBA_W), lambda i, j: (i, 0, 0, 0))
    mem_spec = pl.BlockSpec((1, mem_len, MEM_W), lambda i, j: (i, 0, 0))
    return pl.pallas_call(
        _moba_kernel,
        grid=(b, nt),
        in_specs=[q_spec, kv_spec, kv_spec, q_spec, mem_spec, mem_spec, _const_spec(t_moba.shape)],
        out_specs=[q_spec, q_spec],
        out_shape=[jax.ShapeDtypeStruct((b * s, MOBA_W), BF16), jax.ShapeDtypeStruct((b * s, MEM_W), BF16)],
        scratch_shapes=[pltpu.VMEM((nt, MOBA_HEADS, V_AUG, TILE), BF16),
                        pltpu.VMEM((MEM_HEADS, V_AUG, mem_len), BF16),
                        pltpu.VMEM((BF16_ROWS, MOBA_W), F32),
                        pltpu.VMEM((MOBA_HEADS * TILE, MOBA_W), BF16),
                        pltpu.VMEM((BF16_ROWS, MOBA_HEADS * TILE), F32),
                        pltpu.VMEM((2, 1, TILE, MOBA_HEADS * TILE), F32),
                        pltpu.VMEM((1, 1, MOBA_HEADS * TILE), F32),
                        pltpu.VMEM((1, V_AUG, MOBA_HEADS * TILE), F32),
                        pltpu.VMEM((MOBA_W, TILE), F32)],
        compiler_params=_params(("arbitrary", "arbitrary")),
        name="moba",
    )(qm, km.reshape(b, nt, TILE, MOBA_W), vm.reshape(b, nt, TILE, MOBA_W), qx,
      mk.reshape(b, mem_len, MEM_W), mv.reshape(b, mem_len, MEM_W), t_moba)


def _mix_kernel(x_ref, on_ref, om_ref, ox_ref, g_pre_ref, g_post_ref, wg_ref, wn_ref, wm_ref, wx_ref,
                wo_ref, o_ref):
    x = x_ref[...]
    h = _rms(x, g_pre_ref[...]).astype(BF16)
    merged = jax.nn.sigmoid(_dot(h, wg_ref[:, :D_MODEL])) * _dot(on_ref[...], wn_ref[...])
    merged = merged + jax.nn.sigmoid(_dot(h, wg_ref[:, D_MODEL:2 * D_MODEL])) * _dot(om_ref[...], wm_ref[...])
    merged = merged + jax.nn.sigmoid(_dot(h, wg_ref[:, 2 * D_MODEL:])) * _dot(ox_ref[...], wx_ref[...])
    y = _dot(merged.astype(BF16), wo_ref[...])
    o_ref[...] = x + _rms(y, g_post_ref[...])


def _mix(x2, o_nsa, o_moba, o_mem, g_pre, g_post, w_gates, w_nsa_o, w_moba_o, w_mem_o, w_mix_out, tm=256):
    m = x2.shape[0]
    row = lambda w: pl.BlockSpec((tm, w), lambda i: (i, 0))
    return pl.pallas_call(
        _mix_kernel,
        grid=(m // tm,),
        in_specs=[row(D_MODEL), row(NSA_Q_W), row(MOBA_W), row(MEM_W),
                  _const_spec((1, D_MODEL)), _const_spec((1, D_MODEL)),
                  _const_spec(w_gates.shape), _const_spec(w_nsa_o.shape), _const_spec(w_moba_o.shape),
                  _const_spec(w_mem_o.shape), _const_spec(w_mix_out.shape)],
        out_specs=row(D_MODEL),
        out_shape=jax.ShapeDtypeStruct((m, D_MODEL), F32),
        compiler_params=_params(("parallel",)),
        name="mix",
    )(x2, o_nsa, o_moba, o_mem, g_pre, g_post, w_gates, w_nsa_o, w_moba_o, w_mem_o, w_mix_out)


FFN_CHUNK = 256


def _ffn_kernel(x_ref, g_pre_ref, g_post_ref, wg_ref, wu_ref, wd_ref, o_ref, a_sc):
    x = x_ref[...]
    h = _rms(x, g_pre_ref[...]).astype(BF16)
    d_ff = wg_ref.shape[1]
    for j in range(d_ff // FFN_CHUNK):
        sl = slice(j * FFN_CHUNK, (j + 1) * FFN_CHUNK)
        a_sc[:, sl] = (jax.nn.silu(_dot(h, wg_ref[:, sl])) * _dot(h, wu_ref[:, sl])).astype(BF16)
    f = _dot(a_sc[...], wd_ref[...])
    o_ref[...] = x + _rms(f, g_post_ref[...])


def _ffn(x2, g_pre, g_post, wg, wu, wd, tm=256):
    m = x2.shape[0]
    d_ff = wg.shape[1]
    return pl.pallas_call(
        _ffn_kernel,
        grid=(m // tm,),
        in_specs=[pl.BlockSpec((tm, D_MODEL), lambda i: (i, 0)),
                  _const_spec((1, D_MODEL)), _const_spec((1, D_MODEL)),
                  _const_spec(wg.shape), _const_spec(wu.shape), _const_spec(wd.shape)],
        out_specs=pl.BlockSpec((tm, D_MODEL), lambda i: (i, 0)),
        out_shape=jax.ShapeDtypeStruct((m, D_MODEL), F32),
        scratch_shapes=[pltpu.VMEM((tm, d_ff), BF16)],
        compiler_params=_params(("parallel",)),
        name="ffn",
    )(x2, g_pre, g_post, wg, wu, wd)


def kernel(x, mem, rel_bias, pre_mix_g, mem_norm_g, post_mix_g, w_in, cmp_pos_k, cmp_w1_k, cmp_w2_k, cmp_pos_v, cmp_w1_v, cmp_w2_v, w_mem_kv, w_nsa_o, w_moba_o, w_mem_o, w_mix_out, pre_ffn_g, post_ffn_g, w_ffn_gate, w_ffn_up, w_ffn_down):
    b, s, d_model = x.shape
    depth = w_in.shape[0]
    assert d_model == D_MODEL and s % TILE == 0 and TILE == MOBA_BLOCK == WINDOW
    assert (s - CMP_LEN) // CMP_STRIDE + 1 < N_CMP_PAD and (s // SEL_BLOCK) % SUBLANES == 0
    assert w_in.shape[2] == ATT_W + 3 * D_MODEL and rel_bias.shape == (REL_BUCKETS, N_BIAS_HEADS)

    tile_idx, win_idx, cmp_idx = _bucket_tables(s)
    rel_bias = rel_bias.astype(F32)
    t_all = _expand(tile_idx, rel_bias, 0, N_BIAS_HEADS).reshape(N_BIAS_HEADS, 3, TILE, TILE)
    t_nsa = t_all[:NSA_HEADS].reshape(NSA_GROUPS, NSA_HPG, 3, TILE, TILE).transpose(0, 2, 3, 1, 4)
    t_nsa = t_nsa.reshape(NSA_GROUPS, 3, TILE, NSA_HPG * TILE)
    t_moba = t_all[NSA_HEADS:].transpose(1, 2, 0, 3).reshape(3, TILE, MOBA_HEADS * TILE)
    t_win = _expand(win_idx, rel_bias, 0, NSA_HEADS).reshape(NSA_GROUPS, NSA_HPG, TILE, TILE)
    t_win = t_win.transpose(0, 2, 1, 3).reshape(NSA_GROUPS, TILE, NSA_HPG * TILE)
    b_cmp = _expand(cmp_idx, rel_bias, 0, NSA_HEADS)
    ovt = _overlap_table(s)
    gate_lo = NSA_Q_W + 6 * NSA_KV_W
    rows_per_chunk = CMP_STRIDE * NSA_KV_W

    x2 = x.reshape(b * s, D_MODEL)
    mem2 = mem.reshape(-1, D_MODEL)
    for l in range(depth):
        w_att = jnp.concatenate(
            [w_in[l, :, :gate_lo + NSA_GATE_W],
             jnp.zeros((D_MODEL, GATE_PAD - NSA_GATE_W), w_in.dtype),
             w_in[l, :, gate_lo + NSA_GATE_W:ATT_W]], axis=1).astype(BF16)
        w_gates = w_in[l, :, ATT_W:].astype(BF16)
        row = lambda v: v[l].reshape(1, D_MODEL)

        qn, kc_raw, vc_raw, ks, vs, kw, vw, gn, qm, km, vm, qx = _inproj(x2, row(pre_mix_g), w_att)

        pk, w1k = _compress_weights(cmp_pos_k[l], cmp_w1_k[l])
        pv, w1v = _compress_weights(cmp_pos_v[l], cmp_w1_v[l])
        kc, vct = _compress(kc_raw.reshape(b, s // CMP_STRIDE, rows_per_chunk),
                            vc_raw.reshape(b, s // CMP_STRIDE, rows_per_chunk),
                            pk, pv, w1k, w1v, cmp_w2_k[l].astype(BF16), cmp_w2_v[l].astype(BF16))

        mk, mv = _memkv(mem2, row(mem_norm_g), w_mem_kv[l].astype(BF16))

        o_nsa = _nsa(b, s, qn, gn, kc, vct, ks, vs, kw, vw, b_cmp, t_nsa, t_win, ovt)
        o_moba, o_mem = _moba(b, s, qm, km, vm, qx, mk, mv, t_moba)

        x2 = _mix(x2, o_nsa, o_moba, o_mem, row(pre_mix_g), row(post_mix_g), w_gates,
                  w_nsa_o[l].astype(BF16), w_moba_o[l].astype(BF16), w_mem_o[l].astype(BF16),
                  w_mix_out[l].astype(BF16))
        x2 = _ffn(x2, row(pre_ffn_g), row(post_ffn_g), w_ffn_gate[l].astype(BF16),
                  w_ffn_up[l].astype(BF16), w_ffn_down[l].astype(BF16))
    return x2.reshape(b, s, D_MODEL)
```

```python
import functools
import math

import numpy as np
import jax
import jax.numpy as jnp
from jax import lax
from jax.experimental import pallas as pl
from jax.experimental.pallas import tpu as pltpu

F32 = jnp.float32
BF16 = jnp.bfloat16

D_MODEL = 1024
HEAD_DIM = 64
SCALE = HEAD_DIM ** -0.5
NSA_HEADS = 8
NSA_GROUPS = 2
NSA_HPG = NSA_HEADS // NSA_GROUPS
CMP_LEN = 32
CMP_STRIDE = 16
CMP_HIDDEN = 128
SEL_BLOCK = 64
SEL_TOPN = 8
WINDOW = 256
MOBA_HEADS = 4
MOBA_BLOCK = 256
MOBA_TOPK = 3
MEM_HEADS = 4
REL_BUCKETS = 32
REL_MAX_DIST = 128
N_BIAS_HEADS = NSA_HEADS + MOBA_HEADS
RMS_EPS = 1e-6
NEG_INF = -1e30
FORCE_SCORE = 1e4

NSA_Q_W = NSA_HEADS * HEAD_DIM
NSA_KV_W = NSA_GROUPS * HEAD_DIM
NSA_GATE_W = NSA_HEADS * 3
MOBA_W = MOBA_HEADS * HEAD_DIM
MEM_W = MEM_HEADS * HEAD_DIM
ATT_W = NSA_Q_W + 6 * NSA_KV_W + NSA_GATE_W + 3 * MOBA_W + MEM_W
LANES = 128
SUBLANES = 8
BF16_ROWS = 16
GATE_PAD = LANES
TILE = 256
N_CMP_PAD = 128
V_AUG = HEAD_DIM + BF16_ROWS
MASKED_BUCKET = REL_BUCKETS
VMEM_LIMIT = 56 * 1024 * 1024


def _dot(a, b):
    return jnp.dot(a, b, preferred_element_type=F32)


def _dot_nt(a, b):
    return lax.dot_general(a, b, (((1,), (1,)), ((), ())), preferred_element_type=F32)


def _split_bf16(x):
    hi = x.astype(BF16)
    lo = (x - hi.astype(F32)).astype(BF16)
    return hi, lo


def _rms(x, g):
    return x * lax.rsqrt(jnp.mean(x * x, axis=-1, keepdims=True) + RMS_EPS) * g


def _params(sem):
    return pltpu.CompilerParams(dimension_semantics=sem, vmem_limit_bytes=VMEM_LIMIT)


def _const_spec(shape):
    nd = len(shape)
    return pl.BlockSpec(shape, lambda *_: (0,) * nd, pipeline_mode=pl.Buffered(1))


_INPROJ_OUTS = (
    ("qn", NSA_Q_W, BF16, True),
    ("kc", NSA_KV_W, F32, False), ("vc", NSA_KV_W, F32, False),
    ("ks", NSA_KV_W, BF16, False), ("vs", NSA_KV_W, BF16, False),
    ("kw", NSA_KV_W, BF16, False), ("vw", NSA_KV_W, BF16, False),
    ("gn", GATE_PAD, F32, False),
    ("qm", MOBA_W, BF16, True), ("km", MOBA_W, BF16, False), ("vm", MOBA_W, BF16, False),
    ("qx", MEM_W, BF16, True),
)
_INPROJ_W = sum(o[1] for o in _INPROJ_OUTS)


def _inproj_kernel(x_ref, g_ref, w_ref, *out_refs):
    h = _rms(x_ref[...], g_ref[...]).astype(BF16)
    lo = 0
    for (name, width, dtype, scaled), o_ref in zip(_INPROJ_OUTS, out_refs):
        y = _dot(h, w_ref[:, lo:lo + width])
        if scaled:
            y = y * SCALE
        if name == "gn":
            y = jax.nn.sigmoid(y)
        o_ref[...] = y.astype(dtype)
        lo += width


def _inproj(x2, g, w, tm=512):
    m = x2.shape[0]
    return pl.pallas_call(
        _inproj_kernel,
        grid=(m // tm,),
        in_specs=[pl.BlockSpec((tm, D_MODEL), lambda i: (i, 0)),
                  _const_spec((1, D_MODEL)),
                  _const_spec((D_MODEL, _INPROJ_W))],
        out_specs=[pl.BlockSpec((tm, o[1]), lambda i: (i, 0)) for o in _INPROJ_OUTS],
        out_shape=[jax.ShapeDtypeStruct((m, o[1]), o[2]) for o in _INPROJ_OUTS],
        compiler_params=_params(("parallel",)),
        name="inproj",
    )(x2, g, w)


def _compress_kernel(rk_ref, rv_ref, pk_ref, pv_ref, w1k_ref, w1v_ref, w2k_ref, w2v_ref, kc_ref, vc_ref):
    def one(r_ref, p_ref, w1_ref, w2_ref):
        r = r_ref[0]
        top = _dot((r + p_ref[0:1, :]).astype(BF16), w1_ref[0])
        bot = _dot((r + p_ref[1:2, :]).astype(BF16), w1_ref[1])
        hid = top + pltpu.roll(bot, N_CMP_PAD - 1, 0)
        act = jax.nn.gelu(hid).astype(BF16)
        return jnp.concatenate(
            [_dot(act[:, g * CMP_HIDDEN:(g + 1) * CMP_HIDDEN], w2_ref[...]) for g in range(NSA_GROUPS)], axis=1)

    kc_ref[0] = one(rk_ref, pk_ref, w1k_ref, w2k_ref).astype(BF16)
    vc_ref[0] = one(rv_ref, pv_ref, w1v_ref, w2v_ref).T.astype(BF16)


def _compress(rk, rv, pk, pv, w1k, w1v, w2k, w2v):
    b = rk.shape[0]
    rw = rk.shape[2]
    r_spec = pl.BlockSpec((1, N_CMP_PAD, rw), lambda i: (i, 0, 0))
    o_spec = pl.BlockSpec((1, N_CMP_PAD, NSA_KV_W), lambda i: (i, 0, 0))
    return pl.pallas_call(
        _compress_kernel,
        grid=(b,),
        in_specs=[r_spec, r_spec, _const_spec(pk.shape), _const_spec(pv.shape),
                  _const_spec(w1k.shape), _const_spec(w1v.shape),
                  _const_spec(w2k.shape), _const_spec(w2v.shape)],
        out_specs=[o_spec, o_spec],
        out_shape=[jax.ShapeDtypeStruct((b, N_CMP_PAD, NSA_KV_W), BF16)] * 2,
        compiler_params=_params(("parallel",)),
        name="compress",
    )(rk, rv, pk, pv, w1k, w1v, w2k, w2v)


def _compress_weights(pos, w1):
    half = CMP_LEN // 2
    p = pos.reshape(2, half, 1, HEAD_DIM)
    p = jnp.broadcast_to(p, (2, half, NSA_GROUPS, HEAD_DIM)).reshape(2, half * NSA_KV_W)
    w = w1.reshape(2, half, HEAD_DIM, CMP_HIDDEN)
    eye = jnp.eye(NSA_GROUPS, dtype=w1.dtype)
    wbd = jnp.einsum("ajdm,gk->ajgdkm", w, eye).reshape(2, half * NSA_KV_W, NSA_GROUPS * CMP_HIDDEN)
    return p.astype(F32), wbd.astype(BF16)


def _memkv_kernel(m_ref, g_ref, w_ref, k_ref, v_ref):
    h = _rms(m_ref[...], g_ref[...]).astype(BF16)
    k_ref[...] = _dot(h, w_ref[:, :MEM_W]).astype(BF16)
    v_ref[...] = _dot(h, w_ref[:, MEM_W:]).astype(BF16)


def _memkv(mem2, g, w, tm=256):
    m = mem2.shape[0]
    o_spec = pl.BlockSpec((tm, MEM_W), lambda i: (i, 0))
    return pl.pallas_call(
        _memkv_kernel,
        grid=(m // tm,),
        in_specs=[pl.BlockSpec((tm, D_MODEL), lambda i: (i, 0)), _const_spec((1, D_MODEL)),
                  _const_spec((D_MODEL, 2 * MEM_W))],
        out_specs=[o_spec, o_spec],
        out_shape=[jax.ShapeDtypeStruct((m, MEM_W), BF16)] * 2,
        compiler_params=_params(("parallel",)),
        name="memkv",
    )(mem2, g, w)


def _expand_kernel(idx_ref, bias_ref, o_ref, *, head0, n_heads):
    rows, cols = idx_ref.shape

    def body(i, carry):
        r = pl.multiple_of(i * SUBLANES, SUBLANES)
        for c0 in range(0, cols, TILE):
            idx = idx_ref[pl.ds(r, SUBLANES), c0:c0 + TILE]
            out = [jnp.full(idx.shape, NEG_INF, F32)] * n_heads
            for bkt in range(REL_BUCKETS):
                hit = idx == bkt
                out = [jnp.where(hit, bias_ref[bkt, head0 + h], out[h]) for h in range(n_heads)]
            for h in range(n_heads):
                o_ref[h, pl.ds(r, SUBLANES), c0:c0 + TILE] = out[h]
        return carry

    lax.fori_loop(0, rows // SUBLANES, body, 0)


def _expand(idx, rel_bias, head0, n_heads):
    rows, cols = idx.shape
    return pl.pallas_call(
        functools.partial(_expand_kernel, head0=head0, n_heads=n_heads),
        in_specs=[pl.BlockSpec(memory_space=pltpu.VMEM), pl.BlockSpec(memory_space=pltpu.SMEM)],
        out_specs=pl.BlockSpec(memory_space=pltpu.VMEM),
        out_shape=jax.ShapeDtypeStruct((n_heads, rows, cols), F32),
        compiler_params=pltpu.CompilerParams(vmem_limit_bytes=VMEM_LIMIT),
        name="bias_expand",
    )(idx, rel_bias)


def _t5_bucket_np(dist):
    dist = np.maximum(dist, 0)
    max_exact = REL_BUCKETS // 2
    logd = np.log(np.maximum(dist, 1).astype(np.float32) / max_exact) / math.log(REL_MAX_DIST / max_exact)
    large = np.minimum(max_exact + (logd * (REL_BUCKETS - max_exact)).astype(np.int32), REL_BUCKETS - 1)
    return np.where(dist < max_exact, dist, large).astype(np.int32)


def _bucket_tables(s):
    j = np.arange(TILE)[:, None]
    i = np.arange(TILE)[None, :]
    assert TILE + 1 >= REL_MAX_DIST
    tiles = []
    for d in range(3):
        dist = d * TILE + i - j
        tiles.append(np.where(dist >= 0, _t5_bucket_np(dist), MASKED_BUCKET))
    dist1 = TILE + i - j
    win = np.where(dist1 < WINDOW, _t5_bucket_np(dist1), MASKED_BUCKET)
    n_cmp = (s - CMP_LEN) // CMP_STRIDE + 1
    c = np.arange(N_CMP_PAD)[:, None]
    dist_c = np.arange(s)[None, :] - (c * CMP_STRIDE + CMP_LEN - 1)
    cmp_idx = np.where((dist_c >= 0) & (c < n_cmp), _t5_bucket_np(dist_c), MASKED_BUCKET)
    as_i32 = lambda a: jnp.asarray(a.astype(np.int32))
    return as_i32(np.concatenate(tiles, axis=0)), as_i32(win), as_i32(cmp_idx)


def _overlap_table(s):
    n_cmp = (s - CMP_LEN) // CMP_STRIDE + 1
    n_sel = s // SEL_BLOCK
    cs = np.arange(n_cmp) * CMP_STRIDE
    ss = np.arange(n_sel) * SEL_BLOCK
    ov = np.clip(np.minimum(cs[:, None] + CMP_LEN, ss[None, :] + SEL_BLOCK)
                 - np.maximum(cs[:, None], ss[None, :]), 0, None).astype(np.float32) / CMP_LEN
    ovt = np.zeros((n_sel, N_CMP_PAD), np.float32)
    ovt[:, :n_cmp] = ov.T
    return jnp.asarray(ovt, BF16)


def _store_v_aug(vt_sc, idx, vt):
    ones = jnp.ones((BF16_ROWS, vt.shape[1]), BF16)
    vt_sc[idx] = jnp.concatenate([vt.astype(BF16), ones], axis=0)


def _lane_cat(xs):
    return jnp.concatenate(xs, axis=1)


def _flash_pipelined(own, n_past, streams, s_sc, m_ref, acc_ref):
    def absorb(g, s, kt):
        m_old = m_ref[g]
        m_new = jnp.maximum(m_old, jnp.max(s, axis=0, keepdims=True))
        alpha = jnp.exp(m_old - m_new)
        p = jnp.exp(s - m_new).astype(BF16)
        acc_ref[g] = alpha * acc_ref[g] + streams[g][2](kt)(p)
        m_ref[g] = m_new

    m_ref[...] = jnp.full(m_ref.shape, NEG_INF, F32)
    acc_ref[...] = jnp.zeros(acc_ref.shape, F32)
    for g, stream in enumerate(streams):
        s_sc[0, g] = stream[0]()

    def body(i, carry):
        kt_cur = jnp.where(i == 0, own, i - 1)
        nxt = [stream[1](i) for stream in streams]
        for g in range(len(streams)):
            absorb(g, s_sc[0, g], kt_cur)
        for g in range(len(streams)):
            s_sc[0, g] = nxt[g]
        return carry

    lax.fori_loop(0, n_past, body, 0)
    kt_last = jnp.where(n_past == 0, own, n_past - 1)
    for g in range(len(streams)):
        absorb(g, s_sc[0, g], kt_last)


def _softmax_av(s_list, pv_list):
    m = s_list[0].max(axis=0, keepdims=True)
    for s in s_list[1:]:
        m = jnp.maximum(m, s.max(axis=0, keepdims=True))
    acc = None
    for s, pv in zip(s_list, pv_list):
        part = pv(jnp.exp(s - m).astype(BF16))
        acc = part if acc is None else acc + part
    return acc


def _normalize(acc):
    return acc[:HEAD_DIM] / acc[HEAD_DIM:HEAD_DIM + 1]


def _rank_before(score, n_cand):
    blk = lax.broadcasted_iota(jnp.int32, score.shape, 0)
    rank = jnp.zeros(score.shape, F32)
    for m in range(n_cand):
        row = score[m:m + 1, :]
        tie = jnp.where(blk > m, 1.0, 0.0)
        rank = rank + jnp.where(row > score, 1.0, 0.0) + jnp.where(row == score, tie, 0.0)
    return rank


def _nsa_kernel(q_ref, gn_ref, kc_ref, vct_ref, ks_ref, vs_ref, kw_ref, vw_ref,
                bct_ref, tt_ref, twt_ref, ovt_ref, o_ref,
                vst_sc, vwt_sc, q4_sc, og_sc, sel_sc, s_sc, m_sc, acc_sc, ot_sc):
    qi = pl.program_id(1)
    nt = ks_ref.shape[1]
    n_sel = ovt_ref.shape[0]
    blocks_per_tile = TILE // SEL_BLOCK

    @pl.when(qi == 0)
    def _():
        for kt in range(nt):
            vs_t = vs_ref[0, kt].astype(F32).T
            vw_t = vw_ref[0, kt].astype(F32).T
            for g in range(NSA_GROUPS):
                _store_v_aug(vst_sc, (kt, g), vs_t[g * HEAD_DIM:(g + 1) * HEAD_DIM])
                _store_v_aug(vwt_sc, (kt, g), vw_t[g * HEAD_DIM:(g + 1) * HEAD_DIM])

    pos = lax.broadcasted_iota(jnp.int32, (1, TILE), 1) + qi * TILE
    cur = pos // SEL_BLOCK
    has_cmp = pos >= CMP_LEN - 1
    blk = lax.broadcasted_iota(jnp.int32, (n_sel, TILE), 0)
    prev = jnp.maximum(qi - 1, 0)
    no_prev = jnp.where(qi == 0, NEG_INF, 0.0).astype(F32)
    gates = gn_ref[...].T

    gsls = [slice(g * HEAD_DIM, (g + 1) * HEAD_DIM) for g in range(NSA_GROUPS)]
    group_heads = [[g * NSA_HPG + j for j in range(NSA_HPG)] for g in range(NSA_GROUPS)]

    def gate(g, branch):
        return _lane_cat([gates[3 * h + branch:3 * h + branch + 1, :] for h in group_heads[g]])

    for g in range(NSA_GROUPS):
        heads = group_heads[g]
        q4 = jnp.concatenate([q_ref[:, h * HEAD_DIM:(h + 1) * HEAD_DIM] for h in heads], axis=0)
        q4_sc[g] = q4

        kc = kc_ref[0, :, gsls[g]]
        vct = vct_ref[0, gsls[g], :]
        s = _dot_nt(kc, q4) + _lane_cat([bct_ref[h] for h in heads])
        e = jnp.exp(s - jnp.max(s, axis=0, keepdims=True))
        p = jnp.where(_lane_cat([has_cmp] * NSA_HPG), e / jnp.sum(e, axis=0, keepdims=True), 0.0)
        psum = p[:, :TILE]
        for j in range(1, NSA_HPG):
            psum = psum + p[:, j * TILE:(j + 1) * TILE]
        og_sc[g] = gate(g, 0) * _dot(vct, p.astype(BF16))

        p_hi, p_lo = _split_bf16(psum)
        imp = _dot(ovt_ref[...], p_hi) + _dot(ovt_ref[...], p_lo)
        forced = (blk == 0) | (blk == cur) | (blk == cur - 1)
        score = jnp.where(forced, FORCE_SCORE, jnp.where(blk <= cur, imp, NEG_INF))
        rank = _rank_before(score, n_sel)
        sel = jnp.where(rank < SEL_TOPN, jnp.where(score > NEG_INF / 2, 0.0, NEG_INF), NEG_INF)
        sel_sc[g] = _lane_cat([sel] * NSA_HPG)

    def sel_stream(g):
        def scores(kt, d):
            mask = jnp.concatenate(
                [jnp.broadcast_to(sel_sc[g, pl.ds(kt * blocks_per_tile + r, 1), :], (SEL_BLOCK, NSA_HPG * TILE))
                 for r in range(blocks_per_tile)], axis=0)
            return _dot_nt(ks_ref[0, kt, :, gsls[g]], q4_sc[g]) + (tt_ref[g, d] + mask)

        return (lambda: scores(qi, 0),
                lambda kt: scores(kt, jnp.minimum(qi - kt, 2)),
                lambda kt: (lambda pr: _dot(vst_sc[kt, g], pr)))

    _flash_pipelined(qi, qi, [sel_stream(g) for g in range(NSA_GROUPS)], s_sc, m_sc, acc_sc)

    for g in range(NSA_GROUPS):
        q4 = q4_sc[g]
        s0 = _dot_nt(kw_ref[0, prev, :, gsls[g]], q4) + (twt_ref[g] + no_prev)
        s1 = _dot_nt(kw_ref[0, qi, :, gsls[g]], q4) + tt_ref[g, 0]
        acc_w = _softmax_av([s0, s1], [lambda pr: _dot(vwt_sc[prev, g], pr), lambda pr: _dot(vwt_sc[qi, g], pr)])
        o = og_sc[g] + gate(g, 1) * _normalize(acc_sc[g]) + gate(g, 2) * _normalize(acc_w)
        for j, h in enumerate(group_heads[g]):
            ot_sc[h * HEAD_DIM:(h + 1) * HEAD_DIM, :] = o[:, j * TILE:(j + 1) * TILE]

    o_ref[...] = ot_sc[...].T.astype(BF16)


def _nsa(b, s, qn, gn, kc, vct, ks, vs, kw, vw, bias_cmp, t_nsa, t_win, ovt):
    nt = s // TILE
    n_sel = s // SEL_BLOCK
    kv_spec = pl.BlockSpec((1, nt, TILE, NSA_KV_W), lambda i, j: (i, 0, 0, 0))
    c_spec = pl.BlockSpec((1, N_CMP_PAD, NSA_KV_W), lambda i, j: (i, 0, 0))
    tile4 = lambda a: a.reshape(b, nt, TILE, NSA_KV_W)
    return pl.pallas_call(
        _nsa_kernel,
        grid=(b, nt),
        in_specs=[pl.BlockSpec((TILE, NSA_Q_W), lambda i, j: (i * nt + j, 0)),
                  pl.BlockSpec((TILE, GATE_PAD), lambda i, j: (i * nt + j, 0)),
                  c_spec, c_spec, kv_spec, kv_spec, kv_spec, kv_spec,
                  pl.BlockSpec((NSA_HEADS, N_CMP_PAD, TILE), lambda i, j: (0, 0, j)),
                  _const_spec(t_nsa.shape), _const_spec(t_win.shape), _const_spec(ovt.shape)],
        out_specs=pl.BlockSpec((TILE, NSA_Q_W), lambda i, j: (i * nt + j, 0)),
        out_shape=jax.ShapeDtypeStruct((b * s, NSA_Q_W), BF16),
        scratch_shapes=[pltpu.VMEM((nt, NSA_GROUPS, V_AUG, TILE), BF16),
                        pltpu.VMEM((nt, NSA_GROUPS, V_AUG, TILE), BF16),
                        pltpu.VMEM((NSA_GROUPS, NSA_HPG * TILE, HEAD_DIM), BF16),
                        pltpu.VMEM((NSA_GROUPS, HEAD_DIM, NSA_HPG * TILE), F32),
                        pltpu.VMEM((NSA_GROUPS, n_sel, NSA_HPG * TILE), F32),
                        pltpu.VMEM((2, NSA_GROUPS, TILE, NSA_HPG * TILE), F32),
                        pltpu.VMEM((NSA_GROUPS, 1, NSA_HPG * TILE), F32),
                        pltpu.VMEM((NSA_GROUPS, V_AUG, NSA_HPG * TILE), F32),
                        pltpu.VMEM((NSA_Q_W, TILE), F32)],
        compiler_params=_params(("arbitrary", "arbitrary")),
        name="nsa",
    )(qn, gn, kc, vct, tile4(ks), tile4(vs), tile4(kw), tile4(vw), bias_cmp, t_nsa, t_win, ovt)


def _moba_kernel(qm_ref, km_ref, vm_ref, qx_ref, mk_ref, mv_ref, tt_ref, om_ref, ox_ref,
                 vmt_sc, mvt_sc, kmean_sc, qbd_sc, sel_sc, s_sc, m_sc, acc_sc, ot_sc):
    c = pl.program_id(1)
    nt = km_ref.shape[1]
    hsls = [slice(h * HEAD_DIM, (h + 1) * HEAD_DIM) for h in range(MOBA_HEADS)]

    @pl.when(c == 0)
    def _():
        kmean_sc[...] = jnp.zeros(kmean_sc.shape, F32)
        for n in range(nt):
            kmean_sc[n:n + 1, :] = jnp.sum(km_ref[0, n].astype(F32), axis=0, keepdims=True) * (1.0 / MOBA_BLOCK)
            vt = vm_ref[0, n].astype(F32).T
            for h in range(MOBA_HEADS):
                _store_v_aug(vmt_sc, (n, h), vt[hsls[h]])
        mvt = mv_ref[0].astype(F32).T
        for h in range(MOBA_HEADS):
            _store_v_aug(mvt_sc, h, mvt[hsls[h]])

    lane_head = lax.broadcasted_iota(jnp.int32, (TILE, MOBA_W), 1) // HEAD_DIM

    def block_diag(q_ref):
        q = q_ref[...].astype(F32)
        return jnp.concatenate([jnp.where(lane_head == h, q, 0.0) for h in range(MOBA_HEADS)],
                               axis=0).astype(BF16)

    def per_head_pv(vts):
        return lambda pr: _lane_cat([_dot(vts(h), pr[:, h * TILE:(h + 1) * TILE]) for h in range(MOBA_HEADS)])

    def store_heads(o_t, out_ref):
        for h in range(MOBA_HEADS):
            ot_sc[hsls[h], :] = o_t[:, h * TILE:(h + 1) * TILE]
        out_ref[...] = ot_sc[...].T.astype(BF16)

    qbd = block_diag(qm_ref)
    km_hi, km_lo = _split_bf16(kmean_sc[...])
    gate = _dot_nt(km_hi, qbd) + _dot_nt(km_lo, qbd)
    blk = lax.broadcasted_iota(jnp.int32, gate.shape, 0)
    score = jnp.where(blk < c, gate, NEG_INF * SCALE)
    rank = _rank_before(score, nt)
    sel_sc[...] = jnp.where(rank < MOBA_TOPK, jnp.where(score > NEG_INF * SCALE / 2, 0.0, NEG_INF), NEG_INF)

    qbd_sc[...] = qbd
    stream = (lambda: _dot_nt(km_ref[0, c], qbd_sc[...]) + tt_ref[0],
              lambda n: (_dot_nt(km_ref[0, n], qbd_sc[...])
                         + (tt_ref[jnp.minimum(c - n, 2)] + sel_sc[pl.ds(n, 1), :])),
              lambda n: per_head_pv(lambda h: vmt_sc[n, h]))
    _flash_pipelined(c, c, [stream], s_sc, m_sc, acc_sc)
    store_heads(_normalize(acc_sc[0]), om_ref)

    s = _dot_nt(mk_ref[0], block_diag(qx_ref))
    store_heads(_normalize(_softmax_av([s], [per_head_pv(lambda h: mvt_sc[h])])), ox_ref)


def _moba(b, s, qm, km, vm, qx, mk, mv, t_moba):
    nt = s // TILE
    mem_len = mk.shape[0] // b
    assert MOBA_TOPK <= nt - 1 and nt <= BF16_ROWS
    q_spec = pl.BlockSpec((TILE, MOBA_W), lambda i, j: (i * nt + j, 0))
    kv_spec = pl.BlockSpec((1, nt, TILE, MOBA_W), lambda i, j: (i, 0, 0, 0))
    mem_spec = pl.BlockSpec((1, mem_len, MEM_W), lambda i, j: (i, 0, 0))
    return pl.pallas_call(
        _moba_kernel,
        grid=(b, nt),
        in_specs=[q_spec, kv_spec, kv_spec, q_spec, mem_spec, mem_spec, _const_spec(t_moba.shape)],
        out_specs=[q_spec, q_spec],
        out_shape=[jax.ShapeDtypeStruct((b * s, MOBA_W), BF16), jax.ShapeDtypeStruct((b * s, MEM_W), BF16)],
        scratch_shapes=[pltpu.VMEM((nt, MOBA_HEADS, V_AUG, TILE), BF16),
                        pltpu.VMEM((MEM_HEADS, V_AUG, mem_len), BF16),
                        pltpu.VMEM((BF16_ROWS, MOBA_W), F32),
                        pltpu.VMEM((MOBA_HEADS * TILE, MOBA_W), BF16),
                        pltpu.VMEM((BF16_ROWS, MOBA_HEADS * TILE), F32),
                        pltpu.VMEM((2, 1, TILE, MOBA_HEADS * TILE), F32),
                        pltpu.VMEM((1, 1, MOBA_HEADS * TILE), F32),
                        pltpu.VMEM((1, V_AUG, MOBA_HEADS * TILE), F32),
                        pltpu.VMEM((MOBA_W, TILE), F32)],
        compiler_params=_params(("arbitrary", "arbitrary")),
        name="moba",
    )(qm, km.reshape(b, nt, TILE, MOBA_W), vm.reshape(b, nt, TILE, MOBA_W), qx,
      mk.reshape(b, mem_len, MEM_W), mv.reshape(b, mem_len, MEM_W), t_moba)


def _mix_kernel(x_ref, on_ref, om_ref, ox_ref, g_pre_ref, g_post_ref, wg_ref, wn_ref, wm_ref, wx_ref,
                wo_ref, o_ref):
    x = x_ref[...]
    h = _rms(x, g_pre_ref[...]).astype(BF16)
    merged = jax.nn.sigmoid(_dot(h, wg_ref[:, :D_MODEL])) * _dot(on_ref[...], wn_ref[...])
    merged = merged + jax.nn.sigmoid(_dot(h, wg_ref[:, D_MODEL:2 * D_MODEL])) * _dot(om_ref[...], wm_ref[...])
    merged = merged + jax.nn.sigmoid(_dot(h, wg_ref[:, 2 * D_MODEL:])) * _dot(ox_ref[...], wx_ref[...])
    y = _dot(merged.astype(BF16), wo_ref[...])
    o_ref[...] = x + _rms(y, g_post_ref[...])


def _mix(x2, o_nsa, o_moba, o_mem, g_pre, g_post, w_gates, w_nsa_o, w_moba_o, w_mem_o, w_mix_out, tm=512):
    m = x2.shape[0]
    row = lambda w: pl.BlockSpec((tm, w), lambda i: (i, 0))
    return pl.pallas_call(
        _mix_kernel,
        grid=(m // tm,),
        in_specs=[row(D_MODEL), row(NSA_Q_W), row(MOBA_W), row(MEM_W),
                  _const_spec((1, D_MODEL)), _const_spec((1, D_MODEL)),
                  _const_spec(w_gates.shape), _const_spec(w_nsa_o.shape), _const_spec(w_moba_o.shape),
                  _const_spec(w_mem_o.shape), _const_spec(w_mix_out.shape)],
        out_specs=row(D_MODEL),
        out_shape=jax.ShapeDtypeStruct((m, D_MODEL), F32),
        compiler_params=_params(("parallel",)),
        name="mix",
    )(x2, o_nsa, o_moba, o_mem, g_pre, g_post, w_gates, w_nsa_o, w_moba_o, w_mem_o, w_mix_out)


FFN_CHUNK = 256


def _ffn_kernel(x_ref, g_pre_ref, g_post_ref, wg_ref, wu_ref, wd_ref, o_ref, a_sc):
    x = x_ref[...]
    h = _rms(x, g_pre_ref[...]).astype(BF16)
    d_ff = wg_ref.shape[1]
    for j in range(d_ff // FFN_CHUNK):
        sl = slice(j * FFN_CHUNK, (j + 1) * FFN_CHUNK)
        a_sc[:, sl] = (jax.nn.silu(_dot(h, wg_ref[:, sl])) * _dot(h, wu_ref[:, sl])).astype(BF16)
    f = _dot(a_sc[...], wd_ref[...])
    o_ref[...] = x + _rms(f, g_post_ref[...])


def _ffn(x2, g_pre, g_post, wg, wu, wd, tm=512):
    m = x2.shape[0]
    d_ff = wg.shape[1]
    return pl.pallas_call(
        _ffn_kernel,
        grid=(m // tm,),
        in_specs=[pl.BlockSpec((tm, D_MODEL), lambda i: (i, 0)),
                  _const_spec((1, D_MODEL)), _const_spec((1, D_MODEL)),
                  _const_spec(wg.shape), _const_spec(wu.shape), _const_spec(wd.shape)],
        out_specs=pl.BlockSpec((tm, D_MODEL), lambda i: (i, 0)),
        out_shape=jax.ShapeDtypeStruct((m, D_MODEL), F32),
        scratch_shapes=[pltpu.VMEM((tm, d_ff), BF16)],
        compiler_params=_params(("parallel",)),
        name="ffn",
    )(x2, g_pre, g_post, wg, wu, wd)


def kernel(x, mem, rel_bias, pre_mix_g, mem_norm_g, post_mix_g, w_in, cmp_pos_k, cmp_w1_k, cmp_w2_k, cmp_pos_v, cmp_w1_v, cmp_w2_v, w_mem_kv, w_nsa_o, w_moba_o, w_mem_o, w_mix_out, pre_ffn_g, post_ffn_g, w_ffn_gate, w_ffn_up, w_ffn_down):
    b, s, d_model = x.shape
    depth = w_in.shape[0]
    assert d_model == D_MODEL and s % TILE == 0 and TILE == MOBA_BLOCK == WINDOW
    assert (s - CMP_LEN) // CMP_STRIDE + 1 < N_CMP_PAD and (s // SEL_BLOCK) % SUBLANES == 0
    assert w_in.shape[2] == ATT_W + 3 * D_MODEL and rel_bias.shape == (REL_BUCKETS, N_BIAS_HEADS)

    tile_idx, win_idx, cmp_idx = _bucket_tables(s)
    rel_bias = rel_bias.astype(F32)
    t_all = _expand(tile_idx, rel_bias, 0, N_BIAS_HEADS).reshape(N_BIAS_HEADS, 3, TILE, TILE)
    t_nsa = t_all[:NSA_HEADS].reshape(NSA_GROUPS, NSA_HPG, 3, TILE, TILE).transpose(0, 2, 3, 1, 4)
    t_nsa = t_nsa.reshape(NSA_GROUPS, 3, TILE, NSA_HPG * TILE)
    t_moba = t_all[NSA_HEADS:].transpose(1, 2, 0, 3).reshape(3, TILE, MOBA_HEADS * TILE)
    t_win = _expand(win_idx, rel_bias, 0, NSA_HEADS).reshape(NSA_GROUPS, NSA_HPG, TILE, TILE)
    t_win = t_win.transpose(0, 2, 1, 3).reshape(NSA_GROUPS, TILE, NSA_HPG * TILE)
    b_cmp = _expand(cmp_idx, rel_bias, 0, NSA_HEADS)
    ovt = _overlap_table(s)
    gate_lo = NSA_Q_W + 6 * NSA_KV_W
    rows_per_chunk = CMP_STRIDE * NSA_KV_W

    x2 = x.reshape(b * s, D_MODEL)
    mem2 = mem.reshape(-1, D_MODEL)
    for l in range(depth):
        w_att = jnp.concatenate(
            [w_in[l, :, :gate_lo + NSA_GATE_W],
             jnp.zeros((D_MODEL, GATE_PAD - NSA_GATE_W), w_in.dtype),
             w_in[l, :, gate_lo + NSA_GATE_W:ATT_W]], axis=1).astype(BF16)
        w_gates = w_in[l, :, ATT_W:].astype(BF16)
        row = lambda v: v[l].reshape(1, D_MODEL)

        qn, kc_raw, vc_raw, ks, vs, kw, vw, gn, qm, km, vm, qx = _inproj(x2, row(pre_mix_g), w_att)

        pk, w1k = _compress_weights(cmp_pos_k[l], cmp_w1_k[l])
        pv, w1v = _compress_weights(cmp_pos_v[l], cmp_w1_v[l])
        kc, vct = _compress(kc_raw.reshape(b, s // CMP_STRIDE, rows_per_chunk),
                            vc_raw.reshape(b, s // CMP_STRIDE, rows_per_chunk),
                            pk, pv, w1k, w1v, cmp_w2_k[l].astype(BF16), cmp_w2_v[l].astype(BF16))

        mk, mv = _memkv(mem2, row(mem_norm_g), w_mem_kv[l].astype(BF16))

        o_nsa = _nsa(b, s, qn, gn, kc, vct, ks, vs, kw, vw, b_cmp, t_nsa, t_win, ovt)
        o_moba, o_mem = _moba(b, s, qm, km, vm, qx, mk, mv, t_moba)

        x2 = _mix(x2, o_nsa, o_moba, o_mem, row(pre_mix_g), row(post_mix_g), w_gates,
                  w_nsa_o[l].astype(BF16), w_moba_o[l].astype(BF16), w_mem_o[l].astype(BF16),
                  w_mix_out[l].astype(BF16))
        x2 = _ffn(x2, row(pre_ffn_g), row(post_ffn_g), w_ffn_gate[l].astype(BF16),
                  w_ffn_up[l].astype(BF16), w_ffn_down[l].astype(BF16))
    return x2.reshape(b, s, D_MODEL)
```

```python
import functools
import math

import numpy as np
import jax
import jax.numpy as jnp
from jax import lax
from jax.experimental import pallas as pl
from jax.experimental.pallas import tpu as pltpu

F32 = jnp.float32
BF16 = jnp.bfloat16

D_MODEL = 1024
HEAD_DIM = 64
SCALE = HEAD_DIM ** -0.5
LOG2E = math.log2(math.e)
Q_SCALE = SCALE * LOG2E
NSA_HEADS = 8
NSA_GROUPS = 2
NSA_HPG = NSA_HEADS // NSA_GROUPS
CMP_LEN = 32
CMP_STRIDE = 16
CMP_HIDDEN = 128
SEL_BLOCK = 64
SEL_TOPN = 8
WINDOW = 256
MOBA_HEADS = 4
MOBA_BLOCK = 256
MOBA_TOPK = 3
MEM_HEADS = 4
REL_BUCKETS = 32
REL_MAX_DIST = 128
N_BIAS_HEADS = NSA_HEADS + MOBA_HEADS
RMS_EPS = 1e-6
NEG_INF = -1e30
FORCE_SCORE = 1e4

NSA_Q_W = NSA_HEADS * HEAD_DIM
NSA_KV_W = NSA_GROUPS * HEAD_DIM
NSA_GATE_W = NSA_HEADS * 3
MOBA_W = MOBA_HEADS * HEAD_DIM
MEM_W = MEM_HEADS * HEAD_DIM
ATT_W = NSA_Q_W + 6 * NSA_KV_W + NSA_GATE_W + 3 * MOBA_W + MEM_W
LANES = 128
SUBLANES = 8
BF16_ROWS = 16
GATE_PAD = LANES
TILE = 256
N_CMP_PAD = 128
V_AUG = HEAD_DIM + BF16_ROWS
MASKED_BUCKET = REL_BUCKETS
VMEM_LIMIT = 56 * 1024 * 1024


def _dot(a, b):
    return jnp.dot(a, b, preferred_element_type=F32)


def _dot_nt(a, b):
    return lax.dot_general(a, b, (((1,), (1,)), ((), ())), preferred_element_type=F32)


def _split_bf16(x):
    hi = x.astype(BF16)
    lo = (x - hi.astype(F32)).astype(BF16)
    return hi, lo


def _rms(x, g):
    return x * lax.rsqrt(jnp.mean(x * x, axis=-1, keepdims=True) + RMS_EPS) * g


def _params(sem):
    return pltpu.CompilerParams(dimension_semantics=sem, vmem_limit_bytes=VMEM_LIMIT)


def _const_spec(shape):
    nd = len(shape)
    return pl.BlockSpec(shape, lambda *_: (0,) * nd, pipeline_mode=pl.Buffered(1))


_INPROJ_OUTS = (
    ("qn", NSA_Q_W, BF16, True),
    ("kc", NSA_KV_W, F32, False), ("vc", NSA_KV_W, F32, False),
    ("ks", NSA_KV_W, BF16, False), ("vs", NSA_KV_W, BF16, False),
    ("kw", NSA_KV_W, BF16, False), ("vw", NSA_KV_W, BF16, False),
    ("gn", GATE_PAD, F32, False),
    ("qm", MOBA_W, BF16, True), ("km", MOBA_W, BF16, False), ("vm", MOBA_W, BF16, False),
    ("qx", MEM_W, BF16, True),
)
_INPROJ_W = sum(o[1] for o in _INPROJ_OUTS)
KS_AUG_W = NSA_GROUPS * 2 * HEAD_DIM


def _inproj_out_width(name, width):
    return KS_AUG_W if name == "ks" else width


def _inproj_kernel(x_ref, g_ref, w_ref, e_ref, *out_refs):
    h = _rms(x_ref[...], g_ref[...]).astype(BF16)
    lo = 0
    for (name, width, dtype, scaled), o_ref in zip(_INPROJ_OUTS, out_refs):
        y = _dot(h, w_ref[:, lo:lo + width])
        if scaled:
            y = y * Q_SCALE
        if name == "gn":
            y = jax.nn.sigmoid(y)
        y = y.astype(dtype)
        if name == "ks":
            e = e_ref[...]
            y = _lane_cat([y[:, :HEAD_DIM], e, y[:, HEAD_DIM:], e])
        o_ref[...] = y
        lo += width


def _inproj(x2, g, w, e_cols, tm=512):
    m = x2.shape[0]
    tiles_per_seq = e_cols.shape[0] // tm
    outs = [(_inproj_out_width(o[0], o[1]), o[2]) for o in _INPROJ_OUTS]
    return pl.pallas_call(
        _inproj_kernel,
        grid=(m // tm,),
        in_specs=[pl.BlockSpec((tm, D_MODEL), lambda i: (i, 0)),
                  _const_spec((1, D_MODEL)),
                  _const_spec((D_MODEL, _INPROJ_W)),
                  pl.BlockSpec((tm, HEAD_DIM), lambda i: (i % tiles_per_seq, 0))],
        out_specs=[pl.BlockSpec((tm, w), lambda i: (i, 0)) for w, _ in outs],
        out_shape=[jax.ShapeDtypeStruct((m, w), d) for w, d in outs],
        compiler_params=_params(("parallel",)),
        name="inproj",
    )(x2, g, w, e_cols)


def _compress_kernel(rk_ref, rv_ref, pk_ref, pv_ref, w1k_ref, w1v_ref, w2k_ref, w2v_ref, kc_ref, vc_ref):
    def one(r_ref, p_ref, w1_ref, w2_ref):
        r = r_ref[0]
        top = _dot((r + p_ref[0:1, :]).astype(BF16), w1_ref[0])
        bot = _dot((r + p_ref[1:2, :]).astype(BF16), w1_ref[1])
        hid = top + pltpu.roll(bot, N_CMP_PAD - 1, 0)
        act = jax.nn.gelu(hid).astype(BF16)
        return jnp.concatenate(
            [_dot(act[:, g * CMP_HIDDEN:(g + 1) * CMP_HIDDEN], w2_ref[...]) for g in range(NSA_GROUPS)], axis=1)

    kc_ref[0] = one(rk_ref, pk_ref, w1k_ref, w2k_ref).astype(BF16)
    vc_ref[0] = one(rv_ref, pv_ref, w1v_ref, w2v_ref).T.astype(BF16)


def _compress(rk, rv, pk, pv, w1k, w1v, w2k, w2v):
    b = rk.shape[0]
    rw = rk.shape[2]
    r_spec = pl.BlockSpec((1, N_CMP_PAD, rw), lambda i: (i, 0, 0))
    o_spec = pl.BlockSpec((1, N_CMP_PAD, NSA_KV_W), lambda i: (i, 0, 0))
    return pl.pallas_call(
        _compress_kernel,
        grid=(b,),
        in_specs=[r_spec, r_spec, _const_spec(pk.shape), _const_spec(pv.shape),
                  _const_spec(w1k.shape), _const_spec(w1v.shape),
                  _const_spec(w2k.shape), _const_spec(w2v.shape)],
        out_specs=[o_spec, o_spec],
        out_shape=[jax.ShapeDtypeStruct((b, N_CMP_PAD, NSA_KV_W), BF16)] * 2,
        compiler_params=_params(("parallel",)),
        name="compress",
    )(rk, rv, pk, pv, w1k, w1v, w2k, w2v)


def _compress_weights(pos, w1):
    half = CMP_LEN // 2
    p = pos.reshape(2, half, 1, HEAD_DIM)
    p = jnp.broadcast_to(p, (2, half, NSA_GROUPS, HEAD_DIM)).reshape(2, half * NSA_KV_W)
    w = w1.reshape(2, half, HEAD_DIM, CMP_HIDDEN)
    eye = jnp.eye(NSA_GROUPS, dtype=w1.dtype)
    wbd = jnp.einsum("ajdm,gk->ajgdkm", w, eye).reshape(2, half * NSA_KV_W, NSA_GROUPS * CMP_HIDDEN)
    return p.astype(F32), wbd.astype(BF16)


def _memkv_kernel(m_ref, g_ref, w_ref, k_ref, v_ref):
    h = _rms(m_ref[...], g_ref[...]).astype(BF16)
    k_ref[...] = _dot(h, w_ref[:, :MEM_W]).astype(BF16)
    v_ref[...] = _dot(h, w_ref[:, MEM_W:]).astype(BF16)


def _memkv(mem2, g, w, tm=256):
    m = mem2.shape[0]
    o_spec = pl.BlockSpec((tm, MEM_W), lambda i: (i, 0))
    return pl.pallas_call(
        _memkv_kernel,
        grid=(m // tm,),
        in_specs=[pl.BlockSpec((tm, D_MODEL), lambda i: (i, 0)), _const_spec((1, D_MODEL)),
                  _const_spec((D_MODEL, 2 * MEM_W))],
        out_specs=[o_spec, o_spec],
        out_shape=[jax.ShapeDtypeStruct((m, MEM_W), BF16)] * 2,
        compiler_params=_params(("parallel",)),
        name="memkv",
    )(mem2, g, w)


def _expand_kernel(idx_ref, bias_ref, o_ref, *, head0, n_heads):
    rows, cols = idx_ref.shape

    def body(i, carry):
        r = pl.multiple_of(i * SUBLANES, SUBLANES)
        for c0 in range(0, cols, TILE):
            idx = idx_ref[pl.ds(r, SUBLANES), c0:c0 + TILE]
            out = [jnp.full(idx.shape, NEG_INF, F32)] * n_heads
            for bkt in range(REL_BUCKETS):
                hit = idx == bkt
                out = [jnp.where(hit, bias_ref[bkt, head0 + h], out[h]) for h in range(n_heads)]
            for h in range(n_heads):
                o_ref[h, pl.ds(r, SUBLANES), c0:c0 + TILE] = out[h] * LOG2E
        return carry

    lax.fori_loop(0, rows // SUBLANES, body, 0)


def _expand(idx, rel_bias, head0, n_heads):
    rows, cols = idx.shape
    return pl.pallas_call(
        functools.partial(_expand_kernel, head0=head0, n_heads=n_heads),
        in_specs=[pl.BlockSpec(memory_space=pltpu.VMEM), pl.BlockSpec(memory_space=pltpu.SMEM)],
        out_specs=pl.BlockSpec(memory_space=pltpu.VMEM),
        out_shape=jax.ShapeDtypeStruct((n_heads, rows, cols), F32),
        compiler_params=pltpu.CompilerParams(vmem_limit_bytes=VMEM_LIMIT),
        name="bias_expand",
    )(idx, rel_bias)


def _t5_bucket_np(dist):
    dist = np.maximum(dist, 0)
    max_exact = REL_BUCKETS // 2
    logd = np.log(np.maximum(dist, 1).astype(np.float32) / max_exact) / math.log(REL_MAX_DIST / max_exact)
    large = np.minimum(max_exact + (logd * (REL_BUCKETS - max_exact)).astype(np.int32), REL_BUCKETS - 1)
    return np.where(dist < max_exact, dist, large).astype(np.int32)


def _bucket_tables(s):
    j = np.arange(TILE)[:, None]
    i = np.arange(TILE)[None, :]
    assert TILE + 1 >= REL_MAX_DIST
    tiles = []
    for d in range(2):
        dist = d * TILE + i - j
        tiles.append(np.where(dist >= 0, _t5_bucket_np(dist), MASKED_BUCKET))
    dist1 = TILE + i - j
    win = np.where(dist1 < WINDOW, _t5_bucket_np(dist1), MASKED_BUCKET)
    n_cmp = (s - CMP_LEN) // CMP_STRIDE + 1
    c = np.arange(N_CMP_PAD)[:, None]
    dist_c = np.arange(s)[None, :] - (c * CMP_STRIDE + CMP_LEN - 1)
    cmp_idx = np.where((dist_c >= 0) & (c < n_cmp), _t5_bucket_np(dist_c), MASKED_BUCKET)
    as_i32 = lambda a: jnp.asarray(a.astype(np.int32))
    return as_i32(np.concatenate(tiles, axis=0)), as_i32(win), as_i32(cmp_idx)


def _overlap_table(s):
    n_cmp = (s - CMP_LEN) // CMP_STRIDE + 1
    n_sel = s // SEL_BLOCK
    cs = np.arange(n_cmp) * CMP_STRIDE
    ss = np.arange(n_sel) * SEL_BLOCK
    ov = np.clip(np.minimum(cs[:, None] + CMP_LEN, ss[None, :] + SEL_BLOCK)
                 - np.maximum(cs[:, None], ss[None, :]), 0, None).astype(np.float32) / CMP_LEN
    ovt = np.zeros((n_sel, N_CMP_PAD), np.float32)
    ovt[:, :n_cmp] = ov.T
    return jnp.asarray(ovt, BF16)


def _store_v_aug(vt_sc, idx, vt):
    ones = jnp.ones((BF16_ROWS, vt.shape[1]), BF16)
    vt_sc[idx] = jnp.concatenate([vt.astype(BF16), ones], axis=0)


def _lane_cat(xs):
    return jnp.concatenate(xs, axis=1)


def _flash_pipelined(own, streams, s_sc, m_ref, acc_ref):
    has_prev = jnp.where(own > 0, 1.0, 0.0).astype(F32)
    prev = jnp.maximum(own - 1, 0)
    n_far = jnp.maximum(own - 1, 0)

    def absorb(g, s, kt, c_row, w_row):
        u = jnp.max(s, axis=0, keepdims=True) + c_row
        m_old = m_ref[g]
        m_new = jnp.maximum(m_old, jnp.where(w_row > 0.0, u, NEG_INF))
        alpha = jnp.exp2(m_old - m_new)
        shift = jnp.maximum(m_new, u) - c_row
        p = jnp.exp2(s - shift).astype(BF16)
        acc_ref[g] = alpha * acc_ref[g] + w_row * streams[g]["pv"](kt)(p)
        m_ref[g] = m_new

    def absorb_slot(g, i):
        is_prev = i == 0
        kt = jnp.where(is_prev, prev, i - 1)
        c_row = jnp.where(is_prev, 0.0, streams[g]["c_far"])
        w_row = streams[g]["w"](kt) * jnp.where(is_prev, has_prev, 1.0)
        absorb(g, s_sc[g], kt, c_row, w_row)

    m_ref[...] = jnp.full(m_ref.shape, NEG_INF, F32)
    acc_ref[...] = jnp.zeros(acc_ref.shape, F32)
    s_own = [stream["own"]() for stream in streams]
    for g, stream in enumerate(streams):
        s_sc[g] = stream["prev"]()
    zero_row = jnp.zeros((1, s_own[0].shape[1]), F32)
    for g in range(len(streams)):
        absorb(g, s_own[g], own, zero_row, zero_row + 1.0)

    def body(i, carry):
        nxt = [stream["far"](i) for stream in streams]
        for g in range(len(streams)):
            absorb_slot(g, i)
        for g in range(len(streams)):
            s_sc[g] = nxt[g]
        return carry

    lax.fori_loop(0, n_far, body, 0)
    for g in range(len(streams)):
        absorb_slot(g, n_far)


def _softmax_av(s_list, pv_list):
    m = s_list[0].max(axis=0, keepdims=True)
    for s in s_list[1:]:
        m = jnp.maximum(m, s.max(axis=0, keepdims=True))
    acc = None
    for s, pv in zip(s_list, pv_list):
        part = pv(jnp.exp2(s - m).astype(BF16))
        acc = part if acc is None else acc + part
    return acc


def _normalize(acc):
    return acc[:HEAD_DIM] / acc[HEAD_DIM:HEAD_DIM + 1]


def _rank_before(score, n_cand):
    blk = lax.broadcasted_iota(jnp.int32, score.shape, 0)
    rank = jnp.zeros(score.shape, F32)
    for m in range(n_cand):
        row = score[m:m + 1, :]
        tie = jnp.where(blk > m, 1.0, 0.0)
        rank = rank + jnp.where(row > score, 1.0, 0.0) + jnp.where(row == score, tie, 0.0)
    return rank


def _nsa_kernel(q_ref, gn_ref, kc_ref, vct_ref, ks_ref, vs_ref, kw_ref, vw_ref,
                bct_ref, tt_ref, twt_ref, cfar_ref, ovt_ref, o_ref,
                vst_sc, vwt_sc, qa_sc, og_sc, s_sc, m_sc, acc_sc, ot_sc):
    qi = pl.program_id(1)
    nt = ks_ref.shape[1]
    n_sel = ovt_ref.shape[0]

    @pl.when(qi == 0)
    def _():
        for kt in range(nt):
            vs_t = vs_ref[0, kt].astype(F32).T
            vw_t = vw_ref[0, kt].astype(F32).T
            for g in range(NSA_GROUPS):
                _store_v_aug(vst_sc, (kt, g), vs_t[g * HEAD_DIM:(g + 1) * HEAD_DIM])
                _store_v_aug(vwt_sc, (kt, g), vw_t[g * HEAD_DIM:(g + 1) * HEAD_DIM])

    pos = lax.broadcasted_iota(jnp.int32, (1, TILE), 1) + qi * TILE
    cur = pos // SEL_BLOCK
    has_cmp = pos >= CMP_LEN - 1
    blk = lax.broadcasted_iota(jnp.int32, (n_sel, TILE), 0)
    prev = jnp.maximum(qi - 1, 0)
    no_prev = jnp.where(qi == 0, NEG_INF, 0.0).astype(F32)
    gates = gn_ref[...].T

    gsls = [slice(g * HEAD_DIM, (g + 1) * HEAD_DIM) for g in range(NSA_GROUPS)]
    group_heads = [[g * NSA_HPG + j for j in range(NSA_HPG)] for g in range(NSA_GROUPS)]

    def gate(g, branch):
        return _lane_cat([gates[3 * h + branch:3 * h + branch + 1, :] for h in group_heads[g]])

    for g in range(NSA_GROUPS):
        heads = group_heads[g]
        q4 = jnp.concatenate([q_ref[:, h * HEAD_DIM:(h + 1) * HEAD_DIM] for h in heads], axis=0)
        qa_sc[g, :, 0:HEAD_DIM] = q4

        kc = kc_ref[0, :, gsls[g]]
        vct = vct_ref[0, gsls[g], :]
        s = _dot_nt(kc, q4) + _lane_cat([bct_ref[h] for h in heads])
        e = jnp.exp2(s - jnp.max(s, axis=0, keepdims=True))
        p = jnp.where(_lane_cat([has_cmp] * NSA_HPG), e / jnp.sum(e, axis=0, keepdims=True), 0.0)
        psum = p[:, :TILE]
        for j in range(1, NSA_HPG):
            psum = psum + p[:, j * TILE:(j + 1) * TILE]
        og_sc[g] = gate(g, 0) * _dot(vct, p.astype(BF16))

        p_hi, p_lo = _split_bf16(psum)
        imp = _dot(ovt_ref[...], p_hi) + _dot(ovt_ref[...], p_lo)
        forced = (blk == 0) | (blk == cur) | (blk == cur - 1)
        score = jnp.where(forced, FORCE_SCORE, jnp.where(blk <= cur, imp, NEG_INF))
        rank = _rank_before(score, n_sel)
        sel = jnp.where(rank < SEL_TOPN, jnp.where(score > NEG_INF / 2, 0.0, NEG_INF), NEG_INF)
        sel_t = jnp.concatenate([sel, jnp.zeros((LANES - n_sel, TILE), F32)], axis=0).T[:, :HEAD_DIM]
        for j in range(NSA_HPG):
            qa_sc[g, j * TILE:(j + 1) * TILE, HEAD_DIM:2 * HEAD_DIM] = sel_t.astype(BF16)

    ones_row = jnp.ones((1, NSA_HPG * TILE), F32)

    def sel_stream(g):
        def qk(kt):
            return _dot_nt(ks_ref[0, kt, :, g * 2 * HEAD_DIM:(g + 1) * 2 * HEAD_DIM], qa_sc[g])

        return dict(own=lambda: qk(qi) + tt_ref[g, 0], prev=lambda: qk(prev) + tt_ref[g, 1], far=qk,
                    c_far=cfar_ref[g], w=lambda kt: ones_row,
                    pv=lambda kt: (lambda pr: _dot(vst_sc[kt, g], pr)))

    _flash_pipelined(qi, [sel_stream(g) for g in range(NSA_GROUPS)], s_sc, m_sc, acc_sc)

    for g in range(NSA_GROUPS):
        q4 = qa_sc[g, :, 0:HEAD_DIM]
        s0 = _dot_nt(kw_ref[0, prev, :, gsls[g]], q4) + (twt_ref[g] + no_prev)
        s1 = _dot_nt(kw_ref[0, qi, :, gsls[g]], q4) + tt_ref[g, 0]
        acc_w = _softmax_av([s0, s1], [lambda pr: _dot(vwt_sc[prev, g], pr), lambda pr: _dot(vwt_sc[qi, g], pr)])
        o = og_sc[g] + gate(g, 1) * _normalize(acc_sc[g]) + gate(g, 2) * _normalize(acc_w)
        for j, h in enumerate(group_heads[g]):
            ot_sc[h * HEAD_DIM:(h + 1) * HEAD_DIM, :] = o[:, j * TILE:(j + 1) * TILE]

    o_ref[...] = ot_sc[...].T.astype(BF16)


def _nsa(b, s, qn, gn, kc, vct, ks, vs, kw, vw, bias_cmp, t_nsa, t_win, c_far, ovt):
    nt = s // TILE
    kv_spec = pl.BlockSpec((1, nt, TILE, NSA_KV_W), lambda i, j: (i, 0, 0, 0))
    ks_spec = pl.BlockSpec((1, nt, TILE, KS_AUG_W), lambda i, j: (i, 0, 0, 0))
    c_spec = pl.BlockSpec((1, N_CMP_PAD, NSA_KV_W), lambda i, j: (i, 0, 0))
    tile4 = lambda a: a.reshape(b, nt, TILE, a.shape[-1])
    return pl.pallas_call(
        _nsa_kernel,
        grid=(b, nt),
        in_specs=[pl.BlockSpec((TILE, NSA_Q_W), lambda i, j: (i * nt + j, 0)),
                  pl.BlockSpec((TILE, GATE_PAD), lambda i, j: (i * nt + j, 0)),
                  c_spec, c_spec, ks_spec, kv_spec, kv_spec, kv_spec,
                  pl.BlockSpec((NSA_HEADS, N_CMP_PAD, TILE), lambda i, j: (0, 0, j)),
                  _const_spec(t_nsa.shape), _const_spec(t_win.shape), _const_spec(c_far.shape),
                  _const_spec(ovt.shape)],
        out_specs=pl.BlockSpec((TILE, NSA_Q_W), lambda i, j: (i * nt + j, 0)),
        out_shape=jax.ShapeDtypeStruct((b * s, NSA_Q_W), BF16),
        scratch_shapes=[pltpu.VMEM((nt, NSA_GROUPS, V_AUG, TILE), BF16),
                        pltpu.VMEM((nt, NSA_GROUPS, V_AUG, TILE), BF16),
                        pltpu.VMEM((NSA_GROUPS, NSA_HPG * TILE, 2 * HEAD_DIM), BF16),
                        pltpu.VMEM((NSA_GROUPS, HEAD_DIM, NSA_HPG * TILE), F32),
                        pltpu.VMEM((NSA_GROUPS, TILE, NSA_HPG * TILE), F32),
                        pltpu.VMEM((NSA_GROUPS, 1, NSA_HPG * TILE), F32),
                        pltpu.VMEM((NSA_GROUPS, V_AUG, NSA_HPG * TILE), F32),
                        pltpu.VMEM((NSA_Q_W, TILE), F32)],
        compiler_params=_params(("arbitrary", "arbitrary")),
        name="nsa",
    )(qn, gn, kc, vct, tile4(ks), tile4(vs), tile4(kw), tile4(vw), bias_cmp, t_nsa, t_win, c_far, ovt)


def _moba_kernel(qm_ref, km_ref, vm_ref, qx_ref, mk_ref, mv_ref, tt_ref, cfar_ref, om_ref, ox_ref,
                 vmt_sc, mvt_sc, kmean_sc, qbd_sc, sel_sc, s_sc, m_sc, acc_sc, ot_sc):
    c = pl.program_id(1)
    nt = km_ref.shape[1]
    hsls = [slice(h * HEAD_DIM, (h + 1) * HEAD_DIM) for h in range(MOBA_HEADS)]

    @pl.when(c == 0)
    def _():
        kmean_sc[...] = jnp.zeros(kmean_sc.shape, F32)
        for n in range(nt):
            kmean_sc[n:n + 1, :] = jnp.sum(km_ref[0, n].astype(F32), axis=0, keepdims=True) * (1.0 / MOBA_BLOCK)
            vt = vm_ref[0, n].astype(F32).T
            for h in range(MOBA_HEADS):
                _store_v_aug(vmt_sc, (n, h), vt[hsls[h]])
        mvt = mv_ref[0].astype(F32).T
        for h in range(MOBA_HEADS):
            _store_v_aug(mvt_sc, h, mvt[hsls[h]])

    lane_head = lax.broadcasted_iota(jnp.int32, (TILE, MOBA_W), 1) // HEAD_DIM

    def block_diag(q_ref):
        q = q_ref[...].astype(F32)
        return jnp.concatenate([jnp.where(lane_head == h, q, 0.0) for h in range(MOBA_HEADS)],
                               axis=0).astype(BF16)

    def per_head_pv(vts):
        return lambda pr: _lane_cat([_dot(vts(h), pr[:, h * TILE:(h + 1) * TILE]) for h in range(MOBA_HEADS)])

    def store_heads(o_t, out_ref):
        for h in range(MOBA_HEADS):
            ot_sc[hsls[h], :] = o_t[:, h * TILE:(h + 1) * TILE]
        out_ref[...] = ot_sc[...].T.astype(BF16)

    qbd = block_diag(qm_ref)
    km_hi, km_lo = _split_bf16(kmean_sc[...])
    gate = _dot_nt(km_hi, qbd) + _dot_nt(km_lo, qbd)
    blk = lax.broadcasted_iota(jnp.int32, gate.shape, 0)
    score = jnp.where(blk < c, gate, NEG_INF * Q_SCALE)
    rank = _rank_before(score, nt)
    sel_sc[...] = jnp.where(rank < MOBA_TOPK, jnp.where(score > NEG_INF * Q_SCALE / 2, 1.0, 0.0), 0.0)

    qbd_sc[...] = qbd
    qk = lambda n: _dot_nt(km_ref[0, n], qbd_sc[...])
    stream = dict(own=lambda: qk(c) + tt_ref[0], prev=lambda: qk(jnp.maximum(c - 1, 0)) + tt_ref[1], far=qk,
                  c_far=cfar_ref[...], w=lambda n: sel_sc[pl.ds(n, 1), :],
                  pv=lambda n: per_head_pv(lambda h: vmt_sc[n, h]))
    _flash_pipelined(c, [stream], s_sc, m_sc, acc_sc)
    store_heads(_normalize(acc_sc[0]), om_ref)

    s = _dot_nt(mk_ref[0], block_diag(qx_ref))
    store_heads(_normalize(_softmax_av([s], [per_head_pv(lambda h: mvt_sc[h])])), ox_ref)


def _moba(b, s, qm, km, vm, qx, mk, mv, t_moba, c_far):
    nt = s // TILE
    mem_len = mk.shape[0] // b
    assert MOBA_TOPK <= nt - 1 and nt <= BF16_ROWS
    q_spec = pl.BlockSpec((TILE, MOBA_W), lambda i, j: (i * nt + j, 0))
    kv_spec = pl.BlockSpec((1, nt, TILE, MOBA_W), lambda i, j: (i, 0, 0, 0))
    mem_spec = pl.BlockSpec((1, mem_len, MEM_W), lambda i, j: (i, 0, 0))
    return pl.pallas_call(
        _moba_kernel,
        grid=(b, nt),
        in_specs=[q_spec, kv_spec, kv_spec, q_spec, mem_spec, mem_spec, _const_spec(t_moba.shape),
                  _const_spec(c_far.shape)],
        out_specs=[q_spec, q_spec],
        out_shape=[jax.ShapeDtypeStruct((b * s, MOBA_W), BF16), jax.ShapeDtypeStruct((b * s, MEM_W), BF16)],
        scratch_shapes=[pltpu.VMEM((nt, MOBA_HEADS, V_AUG, TILE), BF16),
                        pltpu.VMEM((MEM_HEADS, V_AUG, mem_len), BF16),
                        pltpu.VMEM((BF16_ROWS, MOBA_W), F32),
                        pltpu.VMEM((MOBA_HEADS * TILE, MOBA_W), BF16),
                        pltpu.VMEM((BF16_ROWS, MOBA_HEADS * TILE), F32),
                        pltpu.VMEM((1, TILE, MOBA_HEADS * TILE), F32),
                        pltpu.VMEM((1, 1, MOBA_HEADS * TILE), F32),
                        pltpu.VMEM((1, V_AUG, MOBA_HEADS * TILE), F32),
                        pltpu.VMEM((MOBA_W, TILE), F32)],
        compiler_params=_params(("arbitrary", "arbitrary")),
        name="moba",
    )(qm, km.reshape(b, nt, TILE, MOBA_W), vm.reshape(b, nt, TILE, MOBA_W), qx,
      mk.reshape(b, mem_len, MEM_W), mv.reshape(b, mem_len, MEM_W), t_moba, c_far)


def _mix_kernel(x_ref, on_ref, om_ref, ox_ref, g_pre_ref, g_post_ref, wg_ref, wn_ref, wm_ref, wx_ref,
                wo_ref, o_ref):
    x = x_ref[...]
    h = _rms(x, g_pre_ref[...]).astype(BF16)
    merged = jax.nn.sigmoid(_dot(h, wg_ref[:, :D_MODEL])) * _dot(on_ref[...], wn_ref[...])
    merged = merged + jax.nn.sigmoid(_dot(h, wg_ref[:, D_MODEL:2 * D_MODEL])) * _dot(om_ref[...], wm_ref[...])
    merged = merged + jax.nn.sigmoid(_dot(h, wg_ref[:, 2 * D_MODEL:])) * _dot(ox_ref[...], wx_ref[...])
    y = _dot(merged.astype(BF16), wo_ref[...])
    o_ref[...] = x + _rms(y, g_post_ref[...])


def _mix(x2, o_nsa, o_moba, o_mem, g_pre, g_post, w_gates, w_nsa_o, w_moba_o, w_mem_o, w_mix_out, tm=512):
    m = x2.shape[0]
    row = lambda w: pl.BlockSpec((tm, w), lambda i: (i, 0))
    return pl.pallas_call(
        _mix_kernel,
        grid=(m // tm,),
        in_specs=[row(D_MODEL), row(NSA_Q_W), row(MOBA_W), row(MEM_W),
                  _const_spec((1, D_MODEL)), _const_spec((1, D_MODEL)),
                  _const_spec(w_gates.shape), _const_spec(w_nsa_o.shape), _const_spec(w_moba_o.shape),
                  _const_spec(w_mem_o.shape), _const_spec(w_mix_out.shape)],
        out_specs=row(D_MODEL),
        out_shape=jax.ShapeDtypeStruct((m, D_MODEL), F32),
        compiler_params=_params(("parallel",)),
        name="mix",
    )(x2, o_nsa, o_moba, o_mem, g_pre, g_post, w_gates, w_nsa_o, w_moba_o, w_mem_o, w_mix_out)


FFN_CHUNK = 256


def _ffn_kernel(x_ref, g_pre_ref, g_post_ref, wg_ref, wu_ref, wd_ref, o_ref, a_sc):
    x = x_ref[...]
    h = _rms(x, g_pre_ref[...]).astype(BF16)
    d_ff = wg_ref.shape[1]
    for j in range(d_ff // FFN_CHUNK):
        sl = slice(j * FFN_CHUNK, (j + 1) * FFN_CHUNK)
        a_sc[:, sl] = (jax.nn.silu(_dot(h, wg_ref[:, sl])) * _dot(h, wu_ref[:, sl])).astype(BF16)
    f = _dot(a_sc[...], wd_ref[...])
    o_ref[...] = x + _rms(f, g_post_ref[...])


def _ffn(x2, g_pre, g_post, wg, wu, wd, tm=512):
    m = x2.shape[0]
    d_ff = wg.shape[1]
    return pl.pallas_call(
        _ffn_kernel,
        grid=(m // tm,),
        in_specs=[pl.BlockSpec((tm, D_MODEL), lambda i: (i, 0)),
                  _const_spec((1, D_MODEL)), _const_spec((1, D_MODEL)),
                  _const_spec(wg.shape), _const_spec(wu.shape), _const_spec(wd.shape)],
        out_specs=pl.BlockSpec((tm, D_MODEL), lambda i: (i, 0)),
        out_shape=jax.ShapeDtypeStruct((m, D_MODEL), F32),
        scratch_shapes=[pltpu.VMEM((tm, d_ff), BF16)],
        compiler_params=_params(("parallel",)),
        name="ffn",
    )(x2, g_pre, g_post, wg, wu, wd)


def kernel(x, mem, rel_bias, pre_mix_g, mem_norm_g, post_mix_g, w_in, cmp_pos_k, cmp_w1_k, cmp_w2_k, cmp_pos_v, cmp_w1_v, cmp_w2_v, w_mem_kv, w_nsa_o, w_moba_o, w_mem_o, w_mix_out, pre_ffn_g, post_ffn_g, w_ffn_gate, w_ffn_up, w_ffn_down):
    b, s, d_model = x.shape
    depth = w_in.shape[0]
    assert d_model == D_MODEL and s % TILE == 0 and TILE == MOBA_BLOCK == WINDOW
    assert (s - CMP_LEN) // CMP_STRIDE + 1 < N_CMP_PAD and (s // SEL_BLOCK) % SUBLANES == 0 and s // SEL_BLOCK <= HEAD_DIM
    assert w_in.shape[2] == ATT_W + 3 * D_MODEL and rel_bias.shape == (REL_BUCKETS, N_BIAS_HEADS)

    tile_idx, win_idx, cmp_idx = _bucket_tables(s)
    rel_bias = rel_bias.astype(F32)
    t_all = _expand(tile_idx, rel_bias, 0, N_BIAS_HEADS).reshape(N_BIAS_HEADS, 2, TILE, TILE)
    t_nsa = t_all[:NSA_HEADS].reshape(NSA_GROUPS, NSA_HPG, 2, TILE, TILE).transpose(0, 2, 3, 1, 4)
    t_nsa = t_nsa.reshape(NSA_GROUPS, 2, TILE, NSA_HPG * TILE)
    t_moba = t_all[NSA_HEADS:].transpose(1, 2, 0, 3).reshape(2, TILE, MOBA_HEADS * TILE)
    t_win = _expand(win_idx, rel_bias, 0, NSA_HEADS).reshape(NSA_GROUPS, NSA_HPG, TILE, TILE)
    t_win = t_win.transpose(0, 2, 1, 3).reshape(NSA_GROUPS, TILE, NSA_HPG * TILE)
    b_cmp = _expand(cmp_idx, rel_bias, 0, NSA_HEADS)
    c_far = jnp.repeat(rel_bias[REL_BUCKETS - 1] * LOG2E, TILE)
    c_far_nsa = c_far[:NSA_HEADS * TILE].reshape(NSA_GROUPS, 1, NSA_HPG * TILE)
    c_far_moba = c_far[NSA_HEADS * TILE:].reshape(1, MOBA_HEADS * TILE)
    ovt = _overlap_table(s)
    sel_cols = np.zeros((s, HEAD_DIM), np.float32)
    sel_cols[np.arange(s), np.arange(s) // SEL_BLOCK] = 1.0
    sel_cols = jnp.asarray(sel_cols, BF16)
    gate_lo = NSA_Q_W + 6 * NSA_KV_W
    rows_per_chunk = CMP_STRIDE * NSA_KV_W

    x2 = x.reshape(b * s, D_MODEL)
    mem2 = mem.reshape(-1, D_MODEL)
    for l in range(depth):
        w_att = jnp.concatenate(
            [w_in[l, :, :gate_lo + NSA_GATE_W],
             jnp.zeros((D_MODEL, GATE_PAD - NSA_GATE_W), w_in.dtype),
             w_in[l, :, gate_lo + NSA_GATE_W:ATT_W]], axis=1).astype(BF16)
        w_gates = w_in[l, :, ATT_W:].astype(BF16)
        row = lambda v: v[l].reshape(1, D_MODEL)

        qn, kc_raw, vc_raw, ks, vs, kw, vw, gn, qm, km, vm, qx = _inproj(x2, row(pre_mix_g), w_att, sel_cols)

        pk, w1k = _compress_weights(cmp_pos_k[l], cmp_w1_k[l])
        pv, w1v = _compress_weights(cmp_pos_v[l], cmp_w1_v[l])
        kc, vct = _compress(kc_raw.reshape(b, s // CMP_STRIDE, rows_per_chunk),
                            vc_raw.reshape(b, s // CMP_STRIDE, rows_per_chunk),
                            pk, pv, w1k, w1v, cmp_w2_k[l].astype(BF16), cmp_w2_v[l].astype(BF16))

        mk, mv = _memkv(mem2, row(mem_norm_g), w_mem_kv[l].astype(BF16))

        o_nsa = _nsa(b, s, qn, gn, kc, vct, ks, vs, kw, vw, b_cmp, t_nsa, t_win, c_far_nsa, ovt)
        o_moba, o_mem = _moba(b, s, qm, km, vm, qx, mk, mv, t_moba, c_far_moba)

        x2 = _mix(x2, o_nsa, o_moba, o_mem, row(pre_mix_g), row(post_mix_g), w_gates,
                  w_nsa_o[l].astype(BF16), w_moba_o[l].astype(BF16), w_mem_o[l].astype(BF16),
                  w_mix_out[l].astype(BF16))
        x2 = _ffn(x2, row(pre_ffn_g), row(post_ffn_g), w_ffn_gate[l].astype(BF16),
                  w_ffn_up[l].astype(BF16), w_ffn_down[l].astype(BF16))
    return x2.reshape(b, s, D_MODEL)
```

```python
import functools
import math

import numpy as np
import jax
import jax.numpy as jnp
from jax import lax
from jax.experimental import pallas as pl
from jax.experimental.pallas import tpu as pltpu

F32 = jnp.float32
BF16 = jnp.bfloat16

D_MODEL = 1024
HEAD_DIM = 64
SCALE = HEAD_DIM ** -0.5
LOG2E = math.log2(math.e)
Q_SCALE = SCALE * LOG2E
NSA_HEADS = 8
NSA_GROUPS = 2
NSA_HPG = NSA_HEADS // NSA_GROUPS
CMP_LEN = 32
CMP_STRIDE = 16
CMP_HIDDEN = 128
SEL_BLOCK = 64
SEL_TOPN = 8
WINDOW = 256
MOBA_HEADS = 4
MOBA_BLOCK = 256
MOBA_TOPK = 3
MEM_HEADS = 4
REL_BUCKETS = 32
REL_MAX_DIST = 128
N_BIAS_HEADS = NSA_HEADS + MOBA_HEADS
RMS_EPS = 1e-6
NEG_INF = -1e30
FORCE_SCORE = 1e4

NSA_Q_W = NSA_HEADS * HEAD_DIM
NSA_KV_W = NSA_GROUPS * HEAD_DIM
NSA_GATE_W = NSA_HEADS * 3
MOBA_W = MOBA_HEADS * HEAD_DIM
MEM_W = MEM_HEADS * HEAD_DIM
ATT_W = NSA_Q_W + 6 * NSA_KV_W + NSA_GATE_W + 3 * MOBA_W + MEM_W
LANES = 128
SUBLANES = 8
BF16_ROWS = 16
MXU_COLS = 256
GATE_PAD = LANES
TILE = 256
N_CMP_PAD = 128
V_AUG = HEAD_DIM + BF16_ROWS
MASKED_BUCKET = REL_BUCKETS
VMEM_LIMIT = 56 * 1024 * 1024


def _dot(a, b):
    return jnp.dot(a, b, preferred_element_type=F32)


def _dot_nt(a, b):
    return lax.dot_general(a, b, (((1,), (1,)), ((), ())), preferred_element_type=F32)


def _split_bf16(x):
    hi = x.astype(BF16)
    lo = (x - hi.astype(F32)).astype(BF16)
    return hi, lo


def _rms(x, g):
    return x * lax.rsqrt(jnp.mean(x * x, axis=-1, keepdims=True) + RMS_EPS) * g


def _params(sem):
    return pltpu.CompilerParams(dimension_semantics=sem, vmem_limit_bytes=VMEM_LIMIT)


def _const_spec(shape):
    nd = len(shape)
    return pl.BlockSpec(shape, lambda *_: (0,) * nd, pipeline_mode=pl.Buffered(1))


_INPROJ_OUTS = (
    ("qn", NSA_Q_W, BF16, True),
    ("kc", NSA_KV_W, F32, False), ("vc", NSA_KV_W, F32, False),
    ("ks", NSA_KV_W, BF16, False), ("vs", NSA_KV_W, BF16, False),
    ("kw", NSA_KV_W, BF16, False), ("vw", NSA_KV_W, BF16, False),
    ("gn", GATE_PAD, F32, False),
    ("qm", MOBA_W, BF16, True), ("km", MOBA_W, BF16, False), ("vm", MOBA_W, BF16, False),
    ("qx", MEM_W, BF16, True),
)
_INPROJ_W = sum(o[1] for o in _INPROJ_OUTS)
KS_AUG_W = NSA_GROUPS * 2 * HEAD_DIM


def _inproj_out_width(name, width):
    return KS_AUG_W if name == "ks" else width


def _inproj_kernel(x_ref, g_ref, w_ref, e_ref, *out_refs):
    h = _rms(x_ref[...], g_ref[...]).astype(BF16)
    runs, lo = [], 0
    for out in zip(_INPROJ_OUTS, out_refs):
        if runs and runs[-1][1] < MXU_COLS:
            runs[-1][0].append(out)
            runs[-1][1] += out[0][1]
        else:
            runs.append([[out], out[0][1], lo])
        lo += out[0][1]
    for outs, run_width, run_lo in runs:
        y_run = _dot(h, w_ref[:, run_lo:run_lo + run_width])
        lo = 0
        for (name, width, dtype, scaled), o_ref in outs:
            y = y_run[:, lo:lo + width]
            if scaled:
                y = y * Q_SCALE
            if name == "gn":
                y = jax.nn.sigmoid(y)
            y = y.astype(dtype)
            if name == "ks":
                e = e_ref[...]
                y = _lane_cat([y[:, :HEAD_DIM], e, y[:, HEAD_DIM:], e])
            o_ref[...] = y
            lo += width


def _inproj(x2, g, w, e_cols, tm=512):
    m = x2.shape[0]
    tiles_per_seq = e_cols.shape[0] // tm
    outs = [(_inproj_out_width(o[0], o[1]), o[2]) for o in _INPROJ_OUTS]
    return pl.pallas_call(
        _inproj_kernel,
        grid=(m // tm,),
        in_specs=[pl.BlockSpec((tm, D_MODEL), lambda i: (i, 0)),
                  _const_spec((1, D_MODEL)),
                  _const_spec((D_MODEL, _INPROJ_W)),
                  pl.BlockSpec((tm, HEAD_DIM), lambda i: (i % tiles_per_seq, 0))],
        out_specs=[pl.BlockSpec((tm, w), lambda i: (i, 0)) for w, _ in outs],
        out_shape=[jax.ShapeDtypeStruct((m, w), d) for w, d in outs],
        compiler_params=_params(("parallel",)),
        name="inproj",
    )(x2, g, w, e_cols)


def _compress_kernel(rk_ref, rv_ref, pk_ref, pv_ref, w1k_ref, w1v_ref, w2k_ref, w2v_ref, kc_ref, vc_ref):
    def one(r_ref, p_ref, w1_ref, w2_ref):
        r = r_ref[0]
        top = _dot((r + p_ref[0:1, :]).astype(BF16), w1_ref[0])
        bot = _dot((r + p_ref[1:2, :]).astype(BF16), w1_ref[1])
        hid = top + pltpu.roll(bot, N_CMP_PAD - 1, 0)
        act = jax.nn.gelu(hid).astype(BF16)
        return jnp.concatenate(
            [_dot(act[:, g * CMP_HIDDEN:(g + 1) * CMP_HIDDEN], w2_ref[...]) for g in range(NSA_GROUPS)], axis=1)

    kc_ref[0] = one(rk_ref, pk_ref, w1k_ref, w2k_ref).astype(BF16)
    vc_ref[0] = one(rv_ref, pv_ref, w1v_ref, w2v_ref).T.astype(BF16)


def _compress(rk, rv, pk, pv, w1k, w1v, w2k, w2v):
    b = rk.shape[0]
    rw = rk.shape[2]
    r_spec = pl.BlockSpec((1, N_CMP_PAD, rw), lambda i: (i, 0, 0))
    o_spec = pl.BlockSpec((1, N_CMP_PAD, NSA_KV_W), lambda i: (i, 0, 0))
    return pl.pallas_call(
        _compress_kernel,
        grid=(b,),
        in_specs=[r_spec, r_spec, _const_spec(pk.shape), _const_spec(pv.shape),
                  _const_spec(w1k.shape), _const_spec(w1v.shape),
                  _const_spec(w2k.shape), _const_spec(w2v.shape)],
        out_specs=[o_spec, o_spec],
        out_shape=[jax.ShapeDtypeStruct((b, N_CMP_PAD, NSA_KV_W), BF16)] * 2,
        compiler_params=_params(("parallel",)),
        name="compress",
    )(rk, rv, pk, pv, w1k, w1v, w2k, w2v)


def _compress_weights(pos, w1):
    half = CMP_LEN // 2
    p = pos.reshape(2, half, 1, HEAD_DIM)
    p = jnp.broadcast_to(p, (2, half, NSA_GROUPS, HEAD_DIM)).reshape(2, half * NSA_KV_W)
    w = w1.reshape(2, half, HEAD_DIM, CMP_HIDDEN)
    eye = jnp.eye(NSA_GROUPS, dtype=w1.dtype)
    wbd = jnp.einsum("ajdm,gk->ajgdkm", w, eye).reshape(2, half * NSA_KV_W, NSA_GROUPS * CMP_HIDDEN)
    return p.astype(F32), wbd.astype(BF16)


def _memkv_kernel(m_ref, g_ref, w_ref, k_ref, v_ref):
    h = _rms(m_ref[...], g_ref[...]).astype(BF16)
    k_ref[...] = _dot(h, w_ref[:, :MEM_W]).astype(BF16)
    v_ref[...] = _dot(h, w_ref[:, MEM_W:]).astype(BF16)


def _memkv(mem2, g, w, tm=256):
    m = mem2.shape[0]
    o_spec = pl.BlockSpec((tm, MEM_W), lambda i: (i, 0))
    return pl.pallas_call(
        _memkv_kernel,
        grid=(m // tm,),
        in_specs=[pl.BlockSpec((tm, D_MODEL), lambda i: (i, 0)), _const_spec((1, D_MODEL)),
                  _const_spec((D_MODEL, 2 * MEM_W))],
        out_specs=[o_spec, o_spec],
        out_shape=[jax.ShapeDtypeStruct((m, MEM_W), BF16)] * 2,
        compiler_params=_params(("parallel",)),
        name="memkv",
    )(mem2, g, w)


def _expand_kernel(idx_ref, bias_ref, o_ref, *, head0, n_heads):
    rows, cols = idx_ref.shape

    def body(i, carry):
        r = pl.multiple_of(i * SUBLANES, SUBLANES)
        for c0 in range(0, cols, TILE):
            idx = idx_ref[pl.ds(r, SUBLANES), c0:c0 + TILE]
            out = [jnp.full(idx.shape, NEG_INF, F32)] * n_heads
            for bkt in range(REL_BUCKETS):
                hit = idx == bkt
                out = [jnp.where(hit, bias_ref[bkt, head0 + h], out[h]) for h in range(n_heads)]
            for h in range(n_heads):
                o_ref[h, pl.ds(r, SUBLANES), c0:c0 + TILE] = out[h] * LOG2E
        return carry

    lax.fori_loop(0, rows // SUBLANES, body, 0)


def _expand(idx, rel_bias, head0, n_heads):
    rows, cols = idx.shape
    return pl.pallas_call(
        functools.partial(_expand_kernel, head0=head0, n_heads=n_heads),
        in_specs=[pl.BlockSpec(memory_space=pltpu.VMEM), pl.BlockSpec(memory_space=pltpu.SMEM)],
        out_specs=pl.BlockSpec(memory_space=pltpu.VMEM),
        out_shape=jax.ShapeDtypeStruct((n_heads, rows, cols), F32),
        compiler_params=pltpu.CompilerParams(vmem_limit_bytes=VMEM_LIMIT),
        name="bias_expand",
    )(idx, rel_bias)


def _t5_bucket_np(dist):
    dist = np.maximum(dist, 0)
    max_exact = REL_BUCKETS // 2
    logd = np.log(np.maximum(dist, 1).astype(np.float32) / max_exact) / math.log(REL_MAX_DIST / max_exact)
    large = np.minimum(max_exact + (logd * (REL_BUCKETS - max_exact)).astype(np.int32), REL_BUCKETS - 1)
    return np.where(dist < max_exact, dist, large).astype(np.int32)


def _bucket_tables(s):
    j = np.arange(TILE)[:, None]
    i = np.arange(TILE)[None, :]
    assert TILE + 1 >= REL_MAX_DIST
    tiles = []
    for d in range(2):
        dist = d * TILE + i - j
        tiles.append(np.where(dist >= 0, _t5_bucket_np(dist), MASKED_BUCKET))
    dist1 = TILE + i - j
    win = np.where(dist1 < WINDOW, _t5_bucket_np(dist1), MASKED_BUCKET)
    n_cmp = (s - CMP_LEN) // CMP_STRIDE + 1
    c = np.arange(N_CMP_PAD)[:, None]
    dist_c = np.arange(s)[None, :] - (c * CMP_STRIDE + CMP_LEN - 1)
    cmp_idx = np.where((dist_c >= 0) & (c < n_cmp), _t5_bucket_np(dist_c), MASKED_BUCKET)
    as_i32 = lambda a: jnp.asarray(a.astype(np.int32))
    return as_i32(np.concatenate(tiles, axis=0)), as_i32(win), as_i32(cmp_idx)


def _overlap_table(s):
    n_cmp = (s - CMP_LEN) // CMP_STRIDE + 1
    n_sel = s // SEL_BLOCK
    cs = np.arange(n_cmp) * CMP_STRIDE
    ss = np.arange(n_sel) * SEL_BLOCK
    ov = np.clip(np.minimum(cs[:, None] + CMP_LEN, ss[None, :] + SEL_BLOCK)
                 - np.maximum(cs[:, None], ss[None, :]), 0, None).astype(np.float32) / CMP_LEN
    ovt = np.zeros((n_sel, N_CMP_PAD), np.float32)
    ovt[:, :n_cmp] = ov.T
    return jnp.asarray(ovt, BF16)


def _store_v_aug(vt_sc, idx, vt):
    ones = jnp.ones((BF16_ROWS, vt.shape[1]), BF16)
    vt_sc[idx] = jnp.concatenate([vt.astype(BF16), ones], axis=0)


def _lane_cat(xs):
    return jnp.concatenate(xs, axis=1)


def _flash_pipelined(own, streams, s_sc, m_ref, acc_ref):
    has_prev = jnp.where(own > 0, 1.0, 0.0).astype(F32)
    prev = jnp.maximum(own - 1, 0)
    n_far = jnp.maximum(own - 1, 0)

    def absorb(g, s, kt, c_row, w_row):
        u = jnp.max(s, axis=0, keepdims=True) + c_row
        m_old = m_ref[g]
        m_new = jnp.maximum(m_old, jnp.where(w_row > 0.0, u, NEG_INF))
        alpha = jnp.exp2(m_old - m_new)
        shift = jnp.maximum(m_new, u) - c_row
        p = jnp.exp2(s - shift).astype(BF16)
        acc_ref[g] = alpha * acc_ref[g] + w_row * streams[g]["pv"](kt)(p)
        m_ref[g] = m_new

    def absorb_slot(g, i):
        is_prev = i == 0
        kt = jnp.where(is_prev, prev, i - 1)
        c_row = jnp.where(is_prev, 0.0, streams[g]["c_far"])
        w_row = streams[g]["w"](kt) * jnp.where(is_prev, has_prev, 1.0)
        absorb(g, s_sc[g], kt, c_row, w_row)

    m_ref[...] = jnp.full(m_ref.shape, NEG_INF, F32)
    acc_ref[...] = jnp.zeros(acc_ref.shape, F32)
    s_own = [stream["own"]() for stream in streams]
    for g, stream in enumerate(streams):
        s_sc[g] = stream["prev"]()
    zero_row = jnp.zeros((1, s_own[0].shape[1]), F32)
    for g in range(len(streams)):
        absorb(g, s_own[g], own, zero_row, zero_row + 1.0)

    def body(i, carry):
        nxt = [stream["far"](i) for stream in streams]
        for g in range(len(streams)):
            absorb_slot(g, i)
        for g in range(len(streams)):
            s_sc[g] = nxt[g]
        return carry

    lax.fori_loop(0, n_far, body, 0)
    for g in range(len(streams)):
        absorb_slot(g, n_far)


def _softmax_av(s_list, pv_list):
    m = s_list[0].max(axis=0, keepdims=True)
    for s in s_list[1:]:
        m = jnp.maximum(m, s.max(axis=0, keepdims=True))
    acc = None
    for s, pv in zip(s_list, pv_list):
        part = pv(jnp.exp2(s - m).astype(BF16))
        acc = part if acc is None else acc + part
    return acc


def _normalize(acc):
    return acc[:HEAD_DIM] / acc[HEAD_DIM:HEAD_DIM + 1]


def _rank_before(score, n_cand):
    blk = lax.broadcasted_iota(jnp.int32, score.shape, 0)
    rank = jnp.zeros(score.shape, F32)
    for m in range(n_cand):
        row = score[m:m + 1, :]
        tie = jnp.where(blk > m, 1.0, 0.0)
        rank = rank + jnp.where(row > score, 1.0, 0.0) + jnp.where(row == score, tie, 0.0)
    return rank


def _nsa_kernel(q_ref, gn_ref, kc_ref, vct_ref, ks_ref, vs_ref, kw_ref, vw_ref,
                bct_ref, tt_ref, twt_ref, cfar_ref, ovt_ref, o_ref,
                vst_sc, vwt_sc, qa_sc, og_sc, s_sc, m_sc, acc_sc, ot_sc):
    qi = pl.program_id(1)
    nt = ks_ref.shape[1]
    n_sel = ovt_ref.shape[0]

    @pl.when(qi == 0)
    def _():
        for kt in range(nt):
            vs_t = vs_ref[0, kt].astype(F32).T
            vw_t = vw_ref[0, kt].astype(F32).T
            for g in range(NSA_GROUPS):
                _store_v_aug(vst_sc, (kt, g), vs_t[g * HEAD_DIM:(g + 1) * HEAD_DIM])
                _store_v_aug(vwt_sc, (kt, g), vw_t[g * HEAD_DIM:(g + 1) * HEAD_DIM])

    pos = lax.broadcasted_iota(jnp.int32, (1, TILE), 1) + qi * TILE
    cur = pos // SEL_BLOCK
    has_cmp = pos >= CMP_LEN - 1
    blk = lax.broadcasted_iota(jnp.int32, (n_sel, TILE), 0)
    prev = jnp.maximum(qi - 1, 0)
    no_prev = jnp.where(qi == 0, NEG_INF, 0.0).astype(F32)
    gates = gn_ref[...].T

    gsls = [slice(g * HEAD_DIM, (g + 1) * HEAD_DIM) for g in range(NSA_GROUPS)]
    group_heads = [[g * NSA_HPG + j for j in range(NSA_HPG)] for g in range(NSA_GROUPS)]

    def gate(g, branch):
        return _lane_cat([gates[3 * h + branch:3 * h + branch + 1, :] for h in group_heads[g]])

    for g in range(NSA_GROUPS):
        heads = group_heads[g]
        q4 = jnp.concatenate([q_ref[:, h * HEAD_DIM:(h + 1) * HEAD_DIM] for h in heads], axis=0)
        qa_sc[g, :, 0:HEAD_DIM] = q4

        s0 = _dot_nt(kw_ref[0, prev, :, gsls[g]], q4) + (twt_ref[g] + no_prev)
        s1 = _dot_nt(kw_ref[0, qi, :, gsls[g]], q4) + tt_ref[g, 0]
        acc_w = _softmax_av([s0, s1], [lambda pr: _dot(vwt_sc[prev, g], pr), lambda pr: _dot(vwt_sc[qi, g], pr)])
        o_win = gate(g, 2) * _normalize(acc_w)

        kc = kc_ref[0, :, gsls[g]]
        vct = vct_ref[0, gsls[g], :]
        s = _dot_nt(kc, q4) + _lane_cat([bct_ref[h] for h in heads])
        e = jnp.exp2(s - jnp.max(s, axis=0, keepdims=True))
        p = jnp.where(_lane_cat([has_cmp] * NSA_HPG), e / jnp.sum(e, axis=0, keepdims=True), 0.0)
        psum = p[:, :TILE]
        for j in range(1, NSA_HPG):
            psum = psum + p[:, j * TILE:(j + 1) * TILE]
        og_sc[g] = gate(g, 0) * _dot(vct, p.astype(BF16)) + o_win

        p_hi, p_lo = _split_bf16(psum)
        imp = _dot(ovt_ref[...], p_hi) + _dot(ovt_ref[...], p_lo)
        forced = (blk == 0) | (blk == cur) | (blk == cur - 1)
        score = jnp.where(forced, FORCE_SCORE, jnp.where(blk <= cur, imp, NEG_INF))
        rank = _rank_before(score, n_sel)
        sel = jnp.where(rank < SEL_TOPN, jnp.where(score > NEG_INF / 2, 0.0, NEG_INF), NEG_INF)
        sel_t = jnp.concatenate([sel, jnp.zeros((LANES - n_sel, TILE), F32)], axis=0).T[:, :HEAD_DIM]
        for j in range(NSA_HPG):
            qa_sc[g, j * TILE:(j + 1) * TILE, HEAD_DIM:2 * HEAD_DIM] = sel_t.astype(BF16)

    ones_row = jnp.ones((1, NSA_HPG * TILE), F32)

    def sel_stream(g):
        def qk(kt):
            return _dot_nt(ks_ref[0, kt, :, g * 2 * HEAD_DIM:(g + 1) * 2 * HEAD_DIM], qa_sc[g])

        return dict(own=lambda: qk(qi) + tt_ref[g, 0], prev=lambda: qk(prev) + tt_ref[g, 1], far=qk,
                    c_far=cfar_ref[g], w=lambda kt: ones_row,
                    pv=lambda kt: (lambda pr: _dot(vst_sc[kt, g], pr)))

    _flash_pipelined(qi, [sel_stream(g) for g in range(NSA_GROUPS)], s_sc, m_sc, acc_sc)

    for g in range(NSA_GROUPS):
        o = og_sc[g] + gate(g, 1) * _normalize(acc_sc[g])
        for j, h in enumerate(group_heads[g]):
            ot_sc[h * HEAD_DIM:(h + 1) * HEAD_DIM, :] = o[:, j * TILE:(j + 1) * TILE]

    o_ref[...] = ot_sc[...].T.astype(BF16)


def _nsa(b, s, qn, gn, kc, vct, ks, vs, kw, vw, bias_cmp, t_nsa, t_win, c_far, ovt):
    nt = s // TILE
    kv_spec = pl.BlockSpec((1, nt, TILE, NSA_KV_W), lambda i, j: (i, 0, 0, 0))
    ks_spec = pl.BlockSpec((1, nt, TILE, KS_AUG_W), lambda i, j: (i, 0, 0, 0))
    c_spec = pl.BlockSpec((1, N_CMP_PAD, NSA_KV_W), lambda i, j: (i, 0, 0))
    tile4 = lambda a: a.reshape(b, nt, TILE, a.shape[-1])
    return pl.pallas_call(
        _nsa_kernel,
        grid=(b, nt),
        in_specs=[pl.BlockSpec((TILE, NSA_Q_W), lambda i, j: (i * nt + j, 0)),
                  pl.BlockSpec((TILE, GATE_PAD), lambda i, j: (i * nt + j, 0)),
                  c_spec, c_spec, ks_spec, kv_spec, kv_spec, kv_spec,
                  pl.BlockSpec((NSA_HEADS, N_CMP_PAD, TILE), lambda i, j: (0, 0, j)),
                  _const_spec(t_nsa.shape), _const_spec(t_win.shape), _const_spec(c_far.shape),
                  _const_spec(ovt.shape)],
        out_specs=pl.BlockSpec((TILE, NSA_Q_W), lambda i, j: (i * nt + j, 0)),
        out_shape=jax.ShapeDtypeStruct((b * s, NSA_Q_W), BF16),
        scratch_shapes=[pltpu.VMEM((nt, NSA_GROUPS, V_AUG, TILE), BF16),
                        pltpu.VMEM((nt, NSA_GROUPS, V_AUG, TILE), BF16),
                        pltpu.VMEM((NSA_GROUPS, NSA_HPG * TILE, 2 * HEAD_DIM), BF16),
                        pltpu.VMEM((NSA_GROUPS, HEAD_DIM, NSA_HPG * TILE), F32),
                        pltpu.VMEM((NSA_GROUPS, TILE, NSA_HPG * TILE), F32),
                        pltpu.VMEM((NSA_GROUPS, 1, NSA_HPG * TILE), F32),
                        pltpu.VMEM((NSA_GROUPS, V_AUG, NSA_HPG * TILE), F32),
                        pltpu.VMEM((NSA_Q_W, TILE), F32)],
        compiler_params=_params(("arbitrary", "arbitrary")),
        name="nsa",
    )(qn, gn, kc, vct, tile4(ks), tile4(vs), tile4(kw), tile4(vw), bias_cmp, t_nsa, t_win, c_far, ovt)


def _moba_kernel(qm_ref, km_ref, vm_ref, qx_ref, mk_ref, mv_ref, tt_ref, cfar_ref, om_ref, ox_ref,
                 vmt_sc, mvt_sc, kmean_sc, qbd_sc, sel_sc, s_sc, m_sc, acc_sc, ot_sc):
    c = pl.program_id(1)
    nt = km_ref.shape[1]
    hsls = [slice(h * HEAD_DIM, (h + 1) * HEAD_DIM) for h in range(MOBA_HEADS)]

    @pl.when(c == 0)
    def _():
        kmean_sc[...] = jnp.zeros(kmean_sc.shape, F32)
        for n in range(nt):
            kmean_sc[n:n + 1, :] = jnp.sum(km_ref[0, n].astype(F32), axis=0, keepdims=True) * (1.0 / MOBA_BLOCK)
            vt = vm_ref[0, n].astype(F32).T
            for h in range(MOBA_HEADS):
                _store_v_aug(vmt_sc, (n, h), vt[hsls[h]])
        mvt = mv_ref[0].astype(F32).T
        for h in range(MOBA_HEADS):
            _store_v_aug(mvt_sc, h, mvt[hsls[h]])

    lane_head = lax.broadcasted_iota(jnp.int32, (TILE, MOBA_W), 1) // HEAD_DIM

    def block_diag(q_ref):
        q = q_ref[...].astype(F32)
        return jnp.concatenate([jnp.where(lane_head == h, q, 0.0) for h in range(MOBA_HEADS)],
                               axis=0).astype(BF16)

    def per_head_pv(vts):
        return lambda pr: _lane_cat([_dot(vts(h), pr[:, h * TILE:(h + 1) * TILE]) for h in range(MOBA_HEADS)])

    def store_heads(o_t, out_ref):
        for h in range(MOBA_HEADS):
            ot_sc[hsls[h], :] = o_t[:, h * TILE:(h + 1) * TILE]
        out_ref[...] = ot_sc[...].T.astype(BF16)

    qbd = block_diag(qm_ref)
    km_hi, km_lo = _split_bf16(kmean_sc[...])
    gate = _dot_nt(km_hi, qbd) + _dot_nt(km_lo, qbd)
    blk = lax.broadcasted_iota(jnp.int32, gate.shape, 0)
    score = jnp.where(blk < c, gate, NEG_INF * Q_SCALE)
    rank = _rank_before(score, nt)
    sel_sc[...] = jnp.where(rank < MOBA_TOPK, jnp.where(score > NEG_INF * Q_SCALE / 2, 1.0, 0.0), 0.0)

    qbd_sc[...] = qbd
    qk = lambda n: _dot_nt(km_ref[0, n], qbd_sc[...])
    stream = dict(own=lambda: qk(c) + tt_ref[0], prev=lambda: qk(jnp.maximum(c - 1, 0)) + tt_ref[1], far=qk,
                  c_far=cfar_ref[...], w=lambda n: sel_sc[pl.ds(n, 1), :],
                  pv=lambda n: per_head_pv(lambda h: vmt_sc[n, h]))
    s = _dot_nt(mk_ref[0], block_diag(qx_ref))
    store_heads(_normalize(_softmax_av([s], [per_head_pv(lambda h: mvt_sc[h])])), ox_ref)

    _flash_pipelined(c, [stream], s_sc, m_sc, acc_sc)
    store_heads(_normalize(acc_sc[0]), om_ref)


def _moba(b, s, qm, km, vm, qx, mk, mv, t_moba, c_far):
    nt = s // TILE
    mem_len = mk.shape[0] // b
    assert MOBA_TOPK <= nt - 1 and nt <= BF16_ROWS
    q_spec = pl.BlockSpec((TILE, MOBA_W), lambda i, j: (i * nt + j, 0))
    kv_spec = pl.BlockSpec((1, nt, TILE, MOBA_W), lambda i, j: (i, 0, 0, 0))
    mem_spec = pl.BlockSpec((1, mem_len, MEM_W), lambda i, j: (i, 0, 0))
    return pl.pallas_call(
        _moba_kernel,
        grid=(b, nt),
        in_specs=[q_spec, kv_spec, kv_spec, q_spec, mem_spec, mem_spec, _const_spec(t_moba.shape),
                  _const_spec(c_far.shape)],
        out_specs=[q_spec, q_spec],
        out_shape=[jax.ShapeDtypeStruct((b * s, MOBA_W), BF16), jax.ShapeDtypeStruct((b * s, MEM_W), BF16)],
        scratch_shapes=[pltpu.VMEM((nt, MOBA_HEADS, V_AUG, TILE), BF16),
                        pltpu.VMEM((MEM_HEADS, V_AUG, mem_len), BF16),
                        pltpu.VMEM((BF16_ROWS, MOBA_W), F32),
                        pltpu.VMEM((MOBA_HEADS * TILE, MOBA_W), BF16),
                        pltpu.VMEM((BF16_ROWS, MOBA_HEADS * TILE), F32),
                        pltpu.VMEM((1, TILE, MOBA_HEADS * TILE), F32),
                        pltpu.VMEM((1, 1, MOBA_HEADS * TILE), F32),
                        pltpu.VMEM((1, V_AUG, MOBA_HEADS * TILE), F32),
                        pltpu.VMEM((MOBA_W, TILE), F32)],
        compiler_params=_params(("arbitrary", "arbitrary")),
        name="moba",
    )(qm, km.reshape(b, nt, TILE, MOBA_W), vm.reshape(b, nt, TILE, MOBA_W), qx,
      mk.reshape(b, mem_len, MEM_W), mv.reshape(b, mem_len, MEM_W), t_moba, c_far)


def _mix_kernel(x_ref, on_ref, om_ref, ox_ref, g_pre_ref, g_post_ref, wg_ref, wn_ref, wm_ref, wx_ref,
                wo_ref, o_ref):
    x = x_ref[...]
    h = _rms(x, g_pre_ref[...]).astype(BF16)
    merged = jax.nn.sigmoid(_dot(h, wg_ref[:, :D_MODEL])) * _dot(on_ref[...], wn_ref[...])
    merged = merged + jax.nn.sigmoid(_dot(h, wg_ref[:, D_MODEL:2 * D_MODEL])) * _dot(om_ref[...], wm_ref[...])
    merged = merged + jax.nn.sigmoid(_dot(h, wg_ref[:, 2 * D_MODEL:])) * _dot(ox_ref[...], wx_ref[...])
    y = _dot(merged.astype(BF16), wo_ref[...])
    o_ref[...] = x + _rms(y, g_post_ref[...])


def _mix(x2, o_nsa, o_moba, o_mem, g_pre, g_post, w_gates, w_nsa_o, w_moba_o, w_mem_o, w_mix_out, tm=512):
    m = x2.shape[0]
    row = lambda w: pl.BlockSpec((tm, w), lambda i: (i, 0))
    return pl.pallas_call(
        _mix_kernel,
        grid=(m // tm,),
        in_specs=[row(D_MODEL), row(NSA_Q_W), row(MOBA_W), row(MEM_W),
                  _const_spec((1, D_MODEL)), _const_spec((1, D_MODEL)),
                  _const_spec(w_gates.shape), _const_spec(w_nsa_o.shape), _const_spec(w_moba_o.shape),
                  _const_spec(w_mem_o.shape), _const_spec(w_mix_out.shape)],
        out_specs=row(D_MODEL),
        out_shape=jax.ShapeDtypeStruct((m, D_MODEL), F32),
        compiler_params=_params(("parallel",)),
        name="mix",
    )(x2, o_nsa, o_moba, o_mem, g_pre, g_post, w_gates, w_nsa_o, w_moba_o, w_mem_o, w_mix_out)


FFN_CHUNK = 256


def _ffn_kernel(x_ref, g_pre_ref, g_post_ref, wg_ref, wu_ref, wd_ref, o_ref, a_sc):
    x = x_ref[...]
    h = _rms(x, g_pre_ref[...]).astype(BF16)
    d_ff = wg_ref.shape[1]
    for j in range(d_ff // FFN_CHUNK):
        sl = slice(j * FFN_CHUNK, (j + 1) * FFN_CHUNK)
        a_sc[:, sl] = (jax.nn.silu(_dot(h, wg_ref[:, sl])) * _dot(h, wu_ref[:, sl])).astype(BF16)
    f = _dot(a_sc[...], wd_ref[...])
    o_ref[...] = x + _rms(f, g_post_ref[...])


def _ffn(x2, g_pre, g_post, wg, wu, wd, tm=512):
    m = x2.shape[0]
    d_ff = wg.shape[1]
    return pl.pallas_call(
        _ffn_kernel,
        grid=(m // tm,),
        in_specs=[pl.BlockSpec((tm, D_MODEL), lambda i: (i, 0)),
                  _const_spec((1, D_MODEL)), _const_spec((1, D_MODEL)),
                  _const_spec(wg.shape), _const_spec(wu.shape), _const_spec(wd.shape)],
        out_specs=pl.BlockSpec((tm, D_MODEL), lambda i: (i, 0)),
        out_shape=jax.ShapeDtypeStruct((m, D_MODEL), F32),
        scratch_shapes=[pltpu.VMEM((tm, d_ff), BF16)],
        compiler_params=_params(("parallel",)),
        name="ffn",
    )(x2, g_pre, g_post, wg, wu, wd)


def kernel(x, mem, rel_bias, pre_mix_g, mem_norm_g, post_mix_g, w_in, cmp_pos_k, cmp_w1_k, cmp_w2_k, cmp_pos_v, cmp_w1_v, cmp_w2_v, w_mem_kv, w_nsa_o, w_moba_o, w_mem_o, w_mix_out, pre_ffn_g, post_ffn_g, w_ffn_gate, w_ffn_up, w_ffn_down):
    b, s, d_model = x.shape
    depth = w_in.shape[0]
    assert d_model == D_MODEL and s % TILE == 0 and TILE == MOBA_BLOCK == WINDOW
    assert (s - CMP_LEN) // CMP_STRIDE + 1 < N_CMP_PAD and (s // SEL_BLOCK) % SUBLANES == 0 and s // SEL_BLOCK <= HEAD_DIM
    assert w_in.shape[2] == ATT_W + 3 * D_MODEL and rel_bias.shape == (REL_BUCKETS, N_BIAS_HEADS)

    tile_idx, win_idx, cmp_idx = _bucket_tables(s)
    rel_bias = rel_bias.astype(F32)
    t_all = _expand(tile_idx, rel_bias, 0, N_BIAS_HEADS).reshape(N_BIAS_HEADS, 2, TILE, TILE)
    t_nsa = t_all[:NSA_HEADS].reshape(NSA_GROUPS, NSA_HPG, 2, TILE, TILE).transpose(0, 2, 3, 1, 4)
    t_nsa = t_nsa.reshape(NSA_GROUPS, 2, TILE, NSA_HPG * TILE)
    t_moba = t_all[NSA_HEADS:].transpose(1, 2, 0, 3).reshape(2, TILE, MOBA_HEADS * TILE)
    t_win = _expand(win_idx, rel_bias, 0, NSA_HEADS).reshape(NSA_GROUPS, NSA_HPG, TILE, TILE)
    t_win = t_win.transpose(0, 2, 1, 3).reshape(NSA_GROUPS, TILE, NSA_HPG * TILE)
    b_cmp = _expand(cmp_idx, rel_bias, 0, NSA_HEADS)
    c_far = jnp.repeat(rel_bias[REL_BUCKETS - 1] * LOG2E, TILE)
    c_far_nsa = c_far[:NSA_HEADS * TILE].reshape(NSA_GROUPS, 1, NSA_HPG * TILE)
    c_far_moba = c_far[NSA_HEADS * TILE:].reshape(1, MOBA_HEADS * TILE)
    ovt = _overlap_table(s)
    sel_cols = np.zeros((s, HEAD_DIM), np.float32)
    sel_cols[np.arange(s), np.arange(s) // SEL_BLOCK] = 1.0
    sel_cols = jnp.asarray(sel_cols, BF16)
    gate_lo = NSA_Q_W + 6 * NSA_KV_W
    rows_per_chunk = CMP_STRIDE * NSA_KV_W

    x2 = x.reshape(b * s, D_MODEL)
    mem2 = mem.reshape(-1, D_MODEL)
    for l in range(depth):
        w_att = jnp.concatenate(
            [w_in[l, :, :gate_lo + NSA_GATE_W],
             jnp.zeros((D_MODEL, GATE_PAD - NSA_GATE_W), w_in.dtype),
             w_in[l, :, gate_lo + NSA_GATE_W:ATT_W]], axis=1).astype(BF16)
        w_gates = w_in[l, :, ATT_W:].astype(BF16)
        row = lambda v: v[l].reshape(1, D_MODEL)

        qn, kc_raw, vc_raw, ks, vs, kw, vw, gn, qm, km, vm, qx = _inproj(x2, row(pre_mix_g), w_att, sel_cols)

        pk, w1k = _compress_weights(cmp_pos_k[l], cmp_w1_k[l])
        pv, w1v = _compress_weights(cmp_pos_v[l], cmp_w1_v[l])
        kc, vct = _compress(kc_raw.reshape(b, s // CMP_STRIDE, rows_per_chunk),
                            vc_raw.reshape(b, s // CMP_STRIDE, rows_per_chunk),
                            pk, pv, w1k, w1v, cmp_w2_k[l].astype(BF16), cmp_w2_v[l].astype(BF16))

        mk, mv = _memkv(mem2, row(mem_norm_g), w_mem_kv[l].astype(BF16))

        o_nsa = _nsa(b, s, qn, gn, kc, vct, ks, vs, kw, vw, b_cmp, t_nsa, t_win, c_far_nsa, ovt)
        o_moba, o_mem = _moba(b, s, qm, km, vm, qx, mk, mv, t_moba, c_far_moba)

        x2 = _mix(x2, o_nsa, o_moba, o_mem, row(pre_mix_g), row(post_mix_g), w_gates,
                  w_nsa_o[l].astype(BF16), w_moba_o[l].astype(BF16), w_mem_o[l].astype(BF16),
                  w_mix_out[l].astype(BF16))
        x2 = _ffn(x2, row(pre_ffn_g), row(post_ffn_g), w_ffn_gate[l].astype(BF16),
                  w_ffn_up[l].astype(BF16), w_ffn_down[l].astype(BF16))
    return x2.reshape(b, s, D_MODEL)
```

```python
import functools
import math

import numpy as np
import jax
import jax.numpy as jnp
from jax import lax
from jax.experimental import pallas as pl
from jax.experimental.pallas import tpu as pltpu

F32 = jnp.float32
BF16 = jnp.bfloat16

D_MODEL = 1024
HEAD_DIM = 64
SCALE = HEAD_DIM ** -0.5
LOG2E = math.log2(math.e)
Q_SCALE = SCALE * LOG2E
NSA_HEADS = 8
NSA_GROUPS = 2
NSA_HPG = NSA_HEADS // NSA_GROUPS
CMP_LEN = 32
CMP_STRIDE = 16
CMP_HIDDEN = 128
SEL_BLOCK = 64
SEL_TOPN = 8
WINDOW = 256
MOBA_HEADS = 4
MOBA_BLOCK = 256
MOBA_TOPK = 3
MEM_HEADS = 4
REL_BUCKETS = 32
REL_MAX_DIST = 128
N_BIAS_HEADS = NSA_HEADS + MOBA_HEADS
RMS_EPS = 1e-6
NEG_INF = -1e30
FORCE_SCORE = 1e4

NSA_Q_W = NSA_HEADS * HEAD_DIM
NSA_KV_W = NSA_GROUPS * HEAD_DIM
NSA_GATE_W = NSA_HEADS * 3
MOBA_W = MOBA_HEADS * HEAD_DIM
MEM_W = MEM_HEADS * HEAD_DIM
ATT_W = NSA_Q_W + 6 * NSA_KV_W + NSA_GATE_W + 3 * MOBA_W + MEM_W
LANES = 128
SUBLANES = 8
BF16_ROWS = 16
MXU_COLS = 256
GATE_PAD = LANES
TILE = 256
N_CMP_PAD = 128
V_AUG = HEAD_DIM + BF16_ROWS
MASKED_BUCKET = REL_BUCKETS
VMEM_LIMIT = 56 * 1024 * 1024


def _dot(a, b):
    return jnp.dot(a, b, preferred_element_type=F32)


def _dot_nt(a, b):
    return lax.dot_general(a, b, (((1,), (1,)), ((), ())), preferred_element_type=F32)


def _split_bf16(x):
    hi = x.astype(BF16)
    lo = (x - hi.astype(F32)).astype(BF16)
    return hi, lo


def _rms(x, g):
    return x * lax.rsqrt(jnp.mean(x * x, axis=-1, keepdims=True) + RMS_EPS) * g


def _params(sem):
    return pltpu.CompilerParams(dimension_semantics=sem, vmem_limit_bytes=VMEM_LIMIT)


def _const_spec(shape):
    nd = len(shape)
    return pl.BlockSpec(shape, lambda *_: (0,) * nd, pipeline_mode=pl.Buffered(1))


_INPROJ_OUTS = (
    ("qn", NSA_Q_W, BF16, True),
    ("kc", NSA_KV_W, F32, False), ("vc", NSA_KV_W, F32, False),
    ("ks", NSA_KV_W, BF16, False), ("vs", NSA_KV_W, BF16, False),
    ("kw", NSA_KV_W, BF16, False), ("vw", NSA_KV_W, BF16, False),
    ("gn", GATE_PAD, F32, False),
    ("qm", MOBA_W, BF16, True), ("km", MOBA_W, BF16, False), ("vm", MOBA_W, BF16, False),
    ("qx", MEM_W, BF16, True),
)
_INPROJ_W = sum(o[1] for o in _INPROJ_OUTS)
KS_AUG_W = NSA_GROUPS * 2 * HEAD_DIM


def _inproj_out_width(name, width):
    return KS_AUG_W if name == "ks" else width


def _inproj_kernel(x_ref, g_ref, w_ref, e_ref, *out_refs):
    h = _rms(x_ref[...], g_ref[...]).astype(BF16)
    runs, lo = [], 0
    for out in zip(_INPROJ_OUTS, out_refs):
        if runs and runs[-1][1] < MXU_COLS:
            runs[-1][0].append(out)
            runs[-1][1] += out[0][1]
        else:
            runs.append([[out], out[0][1], lo])
        lo += out[0][1]
    for outs, run_width, run_lo in runs:
        y_run = _dot(h, w_ref[:, run_lo:run_lo + run_width])
        lo = 0
        for (name, width, dtype, scaled), o_ref in outs:
            y = y_run[:, lo:lo + width]
            if scaled:
                y = y * Q_SCALE
            if name == "gn":
                y = jax.nn.sigmoid(y)
            y = y.astype(dtype)
            if name == "ks":
                e = e_ref[...]
                y = _lane_cat([y[:, :HEAD_DIM], e, y[:, HEAD_DIM:], e])
            o_ref[...] = y
            lo += width


def _inproj(x2, g, w, e_cols, tm=512):
    m = x2.shape[0]
    tiles_per_seq = e_cols.shape[0] // tm
    outs = [(_inproj_out_width(o[0], o[1]), o[2]) for o in _INPROJ_OUTS]
    return pl.pallas_call(
        _inproj_kernel,
        grid=(m // tm,),
        in_specs=[pl.BlockSpec((tm, D_MODEL), lambda i: (i, 0)),
                  _const_spec((1, D_MODEL)),
                  _const_spec((D_MODEL, _INPROJ_W)),
                  pl.BlockSpec((tm, HEAD_DIM), lambda i: (i % tiles_per_seq, 0))],
        out_specs=[pl.BlockSpec((tm, w), lambda i: (i, 0)) for w, _ in outs],
        out_shape=[jax.ShapeDtypeStruct((m, w), d) for w, d in outs],
        compiler_params=_params(("parallel",)),
        name="inproj",
    )(x2, g, w, e_cols)


def _compress_kernel(rk_ref, rv_ref, pk_ref, pv_ref, w1k_ref, w1v_ref, w2k_ref, w2v_ref, kc_ref, vc_ref):
    def one(r_ref, p_ref, w1_ref, w2_ref):
        r = r_ref[0]
        top = _dot((r + p_ref[0:1, :]).astype(BF16), w1_ref[0])
        bot = _dot((r + p_ref[1:2, :]).astype(BF16), w1_ref[1])
        hid = top + pltpu.roll(bot, N_CMP_PAD - 1, 0)
        act = jax.nn.gelu(hid).astype(BF16)
        return jnp.concatenate(
            [_dot(act[:, g * CMP_HIDDEN:(g + 1) * CMP_HIDDEN], w2_ref[...]) for g in range(NSA_GROUPS)], axis=1)

    kc_ref[0] = one(rk_ref, pk_ref, w1k_ref, w2k_ref).astype(BF16)
    vc_ref[0] = one(rv_ref, pv_ref, w1v_ref, w2v_ref).T.astype(BF16)


def _compress(rk, rv, pk, pv, w1k, w1v, w2k, w2v):
    b = rk.shape[0]
    rw = rk.shape[2]
    r_spec = pl.BlockSpec((1, N_CMP_PAD, rw), lambda i: (i, 0, 0))
    o_spec = pl.BlockSpec((1, N_CMP_PAD, NSA_KV_W), lambda i: (i, 0, 0))
    return pl.pallas_call(
        _compress_kernel,
        grid=(b,),
        in_specs=[r_spec, r_spec, _const_spec(pk.shape), _const_spec(pv.shape),
                  _const_spec(w1k.shape), _const_spec(w1v.shape),
                  _const_spec(w2k.shape), _const_spec(w2v.shape)],
        out_specs=[o_spec, o_spec],
        out_shape=[jax.ShapeDtypeStruct((b, N_CMP_PAD, NSA_KV_W), BF16)] * 2,
        compiler_params=_params(("parallel",)),
        name="compress",
    )(rk, rv, pk, pv, w1k, w1v, w2k, w2v)


def _compress_weights(pos, w1):
    half = CMP_LEN // 2
    p = pos.reshape(2, half, 1, HEAD_DIM)
    p = jnp.broadcast_to(p, (2, half, NSA_GROUPS, HEAD_DIM)).reshape(2, half * NSA_KV_W)
    w = w1.reshape(2, half, HEAD_DIM, CMP_HIDDEN)
    eye = jnp.eye(NSA_GROUPS, dtype=w1.dtype)
    wbd = jnp.einsum("ajdm,gk->ajgdkm", w, eye).reshape(2, half * NSA_KV_W, NSA_GROUPS * CMP_HIDDEN)
    return p.astype(F32), wbd.astype(BF16)


def _memkv_kernel(m_ref, g_ref, w_ref, k_ref, v_ref):
    h = _rms(m_ref[...], g_ref[...]).astype(BF16)
    k_ref[...] = _dot(h, w_ref[:, :MEM_W]).astype(BF16)
    v_ref[...] = _dot(h, w_ref[:, MEM_W:]).astype(BF16)


def _memkv(mem2, g, w, tm=256):
    m = mem2.shape[0]
    o_spec = pl.BlockSpec((tm, MEM_W), lambda i: (i, 0))
    return pl.pallas_call(
        _memkv_kernel,
        grid=(m // tm,),
        in_specs=[pl.BlockSpec((tm, D_MODEL), lambda i: (i, 0)), _const_spec((1, D_MODEL)),
                  _const_spec((D_MODEL, 2 * MEM_W))],
        out_specs=[o_spec, o_spec],
        out_shape=[jax.ShapeDtypeStruct((m, MEM_W), BF16)] * 2,
        compiler_params=_params(("parallel",)),
        name="memkv",
    )(mem2, g, w)


def _expand_kernel(idx_ref, bias_ref, o_ref, *, head0, n_heads):
    rows, cols = idx_ref.shape

    def body(i, carry):
        r = pl.multiple_of(i * SUBLANES, SUBLANES)
        for c0 in range(0, cols, TILE):
            idx = idx_ref[pl.ds(r, SUBLANES), c0:c0 + TILE]
            out = [jnp.full(idx.shape, NEG_INF, F32)] * n_heads
            for bkt in range(REL_BUCKETS):
                hit = idx == bkt
                out = [jnp.where(hit, bias_ref[bkt, head0 + h], out[h]) for h in range(n_heads)]
            for h in range(n_heads):
                o_ref[h, pl.ds(r, SUBLANES), c0:c0 + TILE] = out[h] * LOG2E
        return carry

    lax.fori_loop(0, rows // SUBLANES, body, 0)


def _expand(idx, rel_bias, head0, n_heads):
    rows, cols = idx.shape
    return pl.pallas_call(
        functools.partial(_expand_kernel, head0=head0, n_heads=n_heads),
        in_specs=[pl.BlockSpec(memory_space=pltpu.VMEM), pl.BlockSpec(memory_space=pltpu.SMEM)],
        out_specs=pl.BlockSpec(memory_space=pltpu.VMEM),
        out_shape=jax.ShapeDtypeStruct((n_heads, rows, cols), F32),
        compiler_params=pltpu.CompilerParams(vmem_limit_bytes=VMEM_LIMIT),
        name="bias_expand",
    )(idx, rel_bias)


def _t5_bucket_np(dist):
    dist = np.maximum(dist, 0)
    max_exact = REL_BUCKETS // 2
    logd = np.log(np.maximum(dist, 1).astype(np.float32) / max_exact) / math.log(REL_MAX_DIST / max_exact)
    large = np.minimum(max_exact + (logd * (REL_BUCKETS - max_exact)).astype(np.int32), REL_BUCKETS - 1)
    return np.where(dist < max_exact, dist, large).astype(np.int32)


def _bucket_tables(s):
    j = np.arange(TILE)[:, None]
    i = np.arange(TILE)[None, :]
    assert TILE + 1 >= REL_MAX_DIST
    tiles = []
    for d in range(2):
        dist = d * TILE + i - j
        tiles.append(np.where(dist >= 0, _t5_bucket_np(dist), MASKED_BUCKET))
    dist1 = TILE + i - j
    win = np.where(dist1 < WINDOW, _t5_bucket_np(dist1), MASKED_BUCKET)
    n_cmp = (s - CMP_LEN) // CMP_STRIDE + 1
    c = np.arange(N_CMP_PAD)[:, None]
    dist_c = np.arange(s)[None, :] - (c * CMP_STRIDE + CMP_LEN - 1)
    cmp_idx = np.where((dist_c >= 0) & (c < n_cmp), _t5_bucket_np(dist_c), MASKED_BUCKET)
    as_i32 = lambda a: jnp.asarray(a.astype(np.int32))
    return as_i32(np.concatenate(tiles, axis=0)), as_i32(win), as_i32(cmp_idx)


def _overlap_table(s):
    n_cmp = (s - CMP_LEN) // CMP_STRIDE + 1
    n_sel = s // SEL_BLOCK
    cs = np.arange(n_cmp) * CMP_STRIDE
    ss = np.arange(n_sel) * SEL_BLOCK
    ov = np.clip(np.minimum(cs[:, None] + CMP_LEN, ss[None, :] + SEL_BLOCK)
                 - np.maximum(cs[:, None], ss[None, :]), 0, None).astype(np.float32) / CMP_LEN
    ovt = np.zeros((n_sel, N_CMP_PAD), np.float32)
    ovt[:, :n_cmp] = ov.T
    return jnp.asarray(ovt, BF16)


def _store_v_aug(vt_sc, idx, vt):
    ones = jnp.ones((BF16_ROWS, vt.shape[1]), BF16)
    vt_sc[idx] = jnp.concatenate([vt.astype(BF16), ones], axis=0)


def _lane_cat(xs):
    return jnp.concatenate(xs, axis=1)


def _flash_pipelined(own, streams, s_sc, m_ref, acc_ref):
    has_prev = jnp.where(own > 0, 1.0, 0.0).astype(F32)
    prev = jnp.maximum(own - 1, 0)
    n_far = jnp.maximum(own - 1, 0)

    def absorb(g, s, kt, c_row, w_row):
        u = jnp.max(s, axis=0, keepdims=True) + c_row
        m_old = m_ref[g]
        m_new = jnp.maximum(m_old, jnp.where(w_row > 0.0, u, NEG_INF))
        alpha = jnp.exp2(m_old - m_new)
        shift = jnp.maximum(m_new, u) - c_row
        p = jnp.exp2((s - shift).astype(BF16))
        acc_ref[g] = alpha * acc_ref[g] + w_row * streams[g]["pv"](kt)(p)
        m_ref[g] = m_new

    def absorb_slot(g, i):
        is_prev = i == 0
        kt = jnp.where(is_prev, prev, i - 1)
        c_row = jnp.where(is_prev, 0.0, streams[g]["c_far"])
        w_row = streams[g]["w"](kt) * jnp.where(is_prev, has_prev, 1.0)
        absorb(g, s_sc[g], kt, c_row, w_row)

    m_ref[...] = jnp.full(m_ref.shape, NEG_INF, F32)
    acc_ref[...] = jnp.zeros(acc_ref.shape, F32)
    s_own = [stream["own"]() for stream in streams]
    for g, stream in enumerate(streams):
        s_sc[g] = stream["prev"]()
    zero_row = jnp.zeros((1, s_own[0].shape[1]), F32)
    for g in range(len(streams)):
        absorb(g, s_own[g], own, zero_row, zero_row + 1.0)

    def body(i, carry):
        nxt = [stream["far"](i) for stream in streams]
        for g in range(len(streams)):
            absorb_slot(g, i)
        for g in range(len(streams)):
            s_sc[g] = nxt[g]
        return carry

    lax.fori_loop(0, n_far, body, 0)
    for g in range(len(streams)):
        absorb_slot(g, n_far)


def _softmax_av(s_list, pv_list):
    m = s_list[0].max(axis=0, keepdims=True)
    for s in s_list[1:]:
        m = jnp.maximum(m, s.max(axis=0, keepdims=True))
    acc = None
    for s, pv in zip(s_list, pv_list):
        part = pv(jnp.exp2((s - m).astype(BF16)))
        acc = part if acc is None else acc + part
    return acc


def _normalize(acc):
    return acc[:HEAD_DIM] / acc[HEAD_DIM:HEAD_DIM + 1]


def _rank_before(score, n_cand):
    blk = lax.broadcasted_iota(jnp.int32, score.shape, 0)
    rank = jnp.zeros(score.shape, F32)
    for m in range(n_cand):
        row = score[m:m + 1, :]
        tie = jnp.where(blk > m, 1.0, 0.0)
        rank = rank + jnp.where(row > score, 1.0, 0.0) + jnp.where(row == score, tie, 0.0)
    return rank


def _nsa_kernel(q_ref, gn_ref, kc_ref, vct_ref, ks_ref, vs_ref, kw_ref, vw_ref,
                bct_ref, tt_ref, twt_ref, cfar_ref, ovt_ref, o_ref,
                vst_sc, vwt_sc, qa_sc, og_sc, s_sc, m_sc, acc_sc, ot_sc):
    qi = pl.program_id(1)
    nt = ks_ref.shape[1]
    n_sel = ovt_ref.shape[0]

    @pl.when(qi == 0)
    def _():
        for kt in range(nt):
            vs_t = vs_ref[0, kt].astype(F32).T
            vw_t = vw_ref[0, kt].astype(F32).T
            for g in range(NSA_GROUPS):
                _store_v_aug(vst_sc, (kt, g), vs_t[g * HEAD_DIM:(g + 1) * HEAD_DIM])
                _store_v_aug(vwt_sc, (kt, g), vw_t[g * HEAD_DIM:(g + 1) * HEAD_DIM])

    pos = lax.broadcasted_iota(jnp.int32, (1, TILE), 1) + qi * TILE
    cur = pos // SEL_BLOCK
    has_cmp = pos >= CMP_LEN - 1
    blk = lax.broadcasted_iota(jnp.int32, (n_sel, TILE), 0)
    prev = jnp.maximum(qi - 1, 0)
    no_prev = jnp.where(qi == 0, NEG_INF, 0.0).astype(F32)
    gates = gn_ref[...].T

    gsls = [slice(g * HEAD_DIM, (g + 1) * HEAD_DIM) for g in range(NSA_GROUPS)]
    group_heads = [[g * NSA_HPG + j for j in range(NSA_HPG)] for g in range(NSA_GROUPS)]

    def gate(g, branch):
        return _lane_cat([gates[3 * h + branch:3 * h + branch + 1, :] for h in group_heads[g]])

    for g in range(NSA_GROUPS):
        heads = group_heads[g]
        q4 = jnp.concatenate([q_ref[:, h * HEAD_DIM:(h + 1) * HEAD_DIM] for h in heads], axis=0)
        qa_sc[g, :, 0:HEAD_DIM] = q4

        s0 = _dot_nt(kw_ref[0, prev, :, gsls[g]], q4) + (twt_ref[g] + no_prev)
        s1 = _dot_nt(kw_ref[0, qi, :, gsls[g]], q4) + tt_ref[g, 0]
        acc_w = _softmax_av([s0, s1], [lambda pr: _dot(vwt_sc[prev, g], pr), lambda pr: _dot(vwt_sc[qi, g], pr)])
        o_win = gate(g, 2) * _normalize(acc_w)

        kc = kc_ref[0, :, gsls[g]]
        vct = vct_ref[0, gsls[g], :]
        s = _dot_nt(kc, q4) + _lane_cat([bct_ref[h] for h in heads])
        e = jnp.exp2(s - jnp.max(s, axis=0, keepdims=True))
        p = jnp.where(_lane_cat([has_cmp] * NSA_HPG), e / jnp.sum(e, axis=0, keepdims=True), 0.0)
        psum = p[:, :TILE]
        for j in range(1, NSA_HPG):
            psum = psum + p[:, j * TILE:(j + 1) * TILE]
        og_sc[g] = gate(g, 0) * _dot(vct, p.astype(BF16)) + o_win

        p_hi, p_lo = _split_bf16(psum)
        imp = _dot(ovt_ref[...], p_hi) + _dot(ovt_ref[...], p_lo)
        forced = (blk == 0) | (blk == cur) | (blk == cur - 1)
        score = jnp.where(forced, FORCE_SCORE, jnp.where(blk <= cur, imp, NEG_INF))
        rank = _rank_before(score, n_sel)
        sel = jnp.where(rank < SEL_TOPN, jnp.where(score > NEG_INF / 2, 0.0, NEG_INF), NEG_INF)
        sel_t = jnp.concatenate([sel, jnp.zeros((LANES - n_sel, TILE), F32)], axis=0).T[:, :HEAD_DIM]
        for j in range(NSA_HPG):
            qa_sc[g, j * TILE:(j + 1) * TILE, HEAD_DIM:2 * HEAD_DIM] = sel_t.astype(BF16)

    ones_row = jnp.ones((1, NSA_HPG * TILE), F32)

    def sel_stream(g):
        def qk(kt):
            return _dot_nt(ks_ref[0, kt, :, g * 2 * HEAD_DIM:(g + 1) * 2 * HEAD_DIM], qa_sc[g])

        return dict(own=lambda: qk(qi) + tt_ref[g, 0], prev=lambda: qk(prev) + tt_ref[g, 1], far=qk,
                    c_far=cfar_ref[g], w=lambda kt: ones_row,
                    pv=lambda kt: (lambda pr: _dot(vst_sc[kt, g], pr)))

    _flash_pipelined(qi, [sel_stream(g) for g in range(NSA_GROUPS)], s_sc, m_sc, acc_sc)

    for g in range(NSA_GROUPS):
        o = og_sc[g] + gate(g, 1) * _normalize(acc_sc[g])
        for j, h in enumerate(group_heads[g]):
            ot_sc[h * HEAD_DIM:(h + 1) * HEAD_DIM, :] = o[:, j * TILE:(j + 1) * TILE]

    o_ref[...] = ot_sc[...].T.astype(BF16)


def _nsa(b, s, qn, gn, kc, vct, ks, vs, kw, vw, bias_cmp, t_nsa, t_win, c_far, ovt):
    nt = s // TILE
    kv_spec = pl.BlockSpec((1, nt, TILE, NSA_KV_W), lambda i, j: (i, 0, 0, 0))
    ks_spec = pl.BlockSpec((1, nt, TILE, KS_AUG_W), lambda i, j: (i, 0, 0, 0))
    c_spec = pl.BlockSpec((1, N_CMP_PAD, NSA_KV_W), lambda i, j: (i, 0, 0))
    tile4 = lambda a: a.reshape(b, nt, TILE, a.shape[-1])
    return pl.pallas_call(
        _nsa_kernel,
        grid=(b, nt),
        in_specs=[pl.BlockSpec((TILE, NSA_Q_W), lambda i, j: (i * nt + j, 0)),
                  pl.BlockSpec((TILE, GATE_PAD), lambda i, j: (i * nt + j, 0)),
                  c_spec, c_spec, ks_spec, kv_spec, kv_spec, kv_spec,
                  pl.BlockSpec((NSA_HEADS, N_CMP_PAD, TILE), lambda i, j: (0, 0, j)),
                  _const_spec(t_nsa.shape), _const_spec(t_win.shape), _const_spec(c_far.shape),
                  _const_spec(ovt.shape)],
        out_specs=pl.BlockSpec((TILE, NSA_Q_W), lambda i, j: (i * nt + j, 0)),
        out_shape=jax.ShapeDtypeStruct((b * s, NSA_Q_W), BF16),
        scratch_shapes=[pltpu.VMEM((nt, NSA_GROUPS, V_AUG, TILE), BF16),
                        pltpu.VMEM((nt, NSA_GROUPS, V_AUG, TILE), BF16),
                        pltpu.VMEM((NSA_GROUPS, NSA_HPG * TILE, 2 * HEAD_DIM), BF16),
                        pltpu.VMEM((NSA_GROUPS, HEAD_DIM, NSA_HPG * TILE), F32),
                        pltpu.VMEM((NSA_GROUPS, TILE, NSA_HPG * TILE), F32),
                        pltpu.VMEM((NSA_GROUPS, 1, NSA_HPG * TILE), F32),
                        pltpu.VMEM((NSA_GROUPS, V_AUG, NSA_HPG * TILE), F32),
                        pltpu.VMEM((NSA_Q_W, TILE), F32)],
        compiler_params=_params(("arbitrary", "arbitrary")),
        name="nsa",
    )(qn, gn, kc, vct, tile4(ks), tile4(vs), tile4(kw), tile4(vw), bias_cmp, t_nsa, t_win, c_far, ovt)


def _moba_kernel(qm_ref, km_ref, vm_ref, qx_ref, mk_ref, mv_ref, tt_ref, cfar_ref, om_ref, ox_ref,
                 vmt_sc, mvt_sc, kmean_sc, qbd_sc, sel_sc, s_sc, m_sc, acc_sc, ot_sc):
    c = pl.program_id(1)
    nt = km_ref.shape[1]
    hsls = [slice(h * HEAD_DIM, (h + 1) * HEAD_DIM) for h in range(MOBA_HEADS)]

    @pl.when(c == 0)
    def _():
        kmean_sc[...] = jnp.zeros(kmean_sc.shape, F32)
        for n in range(nt):
            kmean_sc[n:n + 1, :] = jnp.sum(km_ref[0, n].astype(F32), axis=0, keepdims=True) * (1.0 / MOBA_BLOCK)
            vt = vm_ref[0, n].astype(F32).T
            for h in range(MOBA_HEADS):
                _store_v_aug(vmt_sc, (n, h), vt[hsls[h]])
        mvt = mv_ref[0].astype(F32).T
        for h in range(MOBA_HEADS):
            _store_v_aug(mvt_sc, h, mvt[hsls[h]])

    lane_head = lax.broadcasted_iota(jnp.int32, (TILE, MOBA_W), 1) // HEAD_DIM

    def block_diag(q_ref):
        q = q_ref[...].astype(F32)
        return jnp.concatenate([jnp.where(lane_head == h, q, 0.0) for h in range(MOBA_HEADS)],
                               axis=0).astype(BF16)

    def per_head_pv(vts):
        return lambda pr: _lane_cat([_dot(vts(h), pr[:, h * TILE:(h + 1) * TILE]) for h in range(MOBA_HEADS)])

    def store_heads(o_t, out_ref):
        for h in range(MOBA_HEADS):
            ot_sc[hsls[h], :] = o_t[:, h * TILE:(h + 1) * TILE]
        out_ref[...] = ot_sc[...].T.astype(BF16)

    qbd = block_diag(qm_ref)
    km_hi, km_lo = _split_bf16(kmean_sc[...])
    gate = _dot_nt(km_hi, qbd) + _dot_nt(km_lo, qbd)
    blk = lax.broadcasted_iota(jnp.int32, gate.shape, 0)
    score = jnp.where(blk < c, gate, NEG_INF * Q_SCALE)
    rank = _rank_before(score, nt)
    sel_sc[...] = jnp.where(rank < MOBA_TOPK, jnp.where(score > NEG_INF * Q_SCALE / 2, 1.0, 0.0), 0.0)

    qbd_sc[...] = qbd
    qk = lambda n: _dot_nt(km_ref[0, n], qbd_sc[...])
    stream = dict(own=lambda: qk(c) + tt_ref[0], prev=lambda: qk(jnp.maximum(c - 1, 0)) + tt_ref[1], far=qk,
                  c_far=cfar_ref[...], w=lambda n: sel_sc[pl.ds(n, 1), :],
                  pv=lambda n: per_head_pv(lambda h: vmt_sc[n, h]))
    s = _dot_nt(mk_ref[0], block_diag(qx_ref))
    store_heads(_normalize(_softmax_av([s], [per_head_pv(lambda h: mvt_sc[h])])), ox_ref)

    _flash_pipelined(c, [stream], s_sc, m_sc, acc_sc)
    store_heads(_normalize(acc_sc[0]), om_ref)


def _moba(b, s, qm, km, vm, qx, mk, mv, t_moba, c_far):
    nt = s // TILE
    mem_len = mk.shape[0] // b
    assert MOBA_TOPK <= nt - 1 and nt <= BF16_ROWS
    q_spec = pl.BlockSpec((TILE, MOBA_W), lambda i, j: (i * nt + j, 0))
    kv_spec = pl.BlockSpec((1, nt, TILE, MOBA_W), lambda i, j: (i, 0, 0, 0))
    mem_spec = pl.BlockSpec((1, mem_len, MEM_W), lambda i, j: (i, 0, 0))
    return pl.pallas_call(
        _moba_kernel,
        grid=(b, nt),
        in_specs=[q_spec, kv_spec, kv_spec, q_spec, mem_spec, mem_spec, _const_spec(t_moba.shape),
                  _const_spec(c_far.shape)],
        out_specs=[q_spec, q_spec],
        out_shape=[jax.ShapeDtypeStruct((b * s, MOBA_W), BF16), jax.ShapeDtypeStruct((b * s, MEM_W), BF16)],
        scratch_shapes=[pltpu.VMEM((nt, MOBA_HEADS, V_AUG, TILE), BF16),
                        pltpu.VMEM((MEM_HEADS, V_AUG, mem_len), BF16),
                        pltpu.VMEM((BF16_ROWS, MOBA_W), F32),
                        pltpu.VMEM((MOBA_HEADS * TILE, MOBA_W), BF16),
                        pltpu.VMEM((BF16_ROWS, MOBA_HEADS * TILE), F32),
                        pltpu.VMEM((1, TILE, MOBA_HEADS * TILE), F32),
                        pltpu.VMEM((1, 1, MOBA_HEADS * TILE), F32),
                        pltpu.VMEM((1, V_AUG, MOBA_HEADS * TILE), F32),
                        pltpu.VMEM((MOBA_W, TILE), F32)],
        compiler_params=_params(("arbitrary", "arbitrary")),
        name="moba",
    )(qm, km.reshape(b, nt, TILE, MOBA_W), vm.reshape(b, nt, TILE, MOBA_W), qx,
      mk.reshape(b, mem_len, MEM_W), mv.reshape(b, mem_len, MEM_W), t_moba, c_far)


def _mix_kernel(x_ref, on_ref, om_ref, ox_ref, g_pre_ref, g_post_ref, wg_ref, wn_ref, wm_ref, wx_ref,
                wo_ref, o_ref):
    x = x_ref[...]
    h = _rms(x, g_pre_ref[...]).astype(BF16)
    merged = jax.nn.sigmoid(_dot(h, wg_ref[:, :D_MODEL])) * _dot(on_ref[...], wn_ref[...])
    merged = merged + jax.nn.sigmoid(_dot(h, wg_ref[:, D_MODEL:2 * D_MODEL])) * _dot(om_ref[...], wm_ref[...])
    merged = merged + jax.nn.sigmoid(_dot(h, wg_ref[:, 2 * D_MODEL:])) * _dot(ox_ref[...], wx_ref[...])
    y = _dot(merged.astype(BF16), wo_ref[...])
    o_ref[...] = x + _rms(y, g_post_ref[...])


def _mix(x2, o_nsa, o_moba, o_mem, g_pre, g_post, w_gates, w_nsa_o, w_moba_o, w_mem_o, w_mix_out, tm=512):
    m = x2.shape[0]
    row = lambda w: pl.BlockSpec((tm, w), lambda i: (i, 0))
    return pl.pallas_call(
        _mix_kernel,
        grid=(m // tm,),
        in_specs=[row(D_MODEL), row(NSA_Q_W), row(MOBA_W), row(MEM_W),
                  _const_spec((1, D_MODEL)), _const_spec((1, D_MODEL)),
                  _const_spec(w_gates.shape), _const_spec(w_nsa_o.shape), _const_spec(w_moba_o.shape),
                  _const_spec(w_mem_o.shape), _const_spec(w_mix_out.shape)],
        out_specs=row(D_MODEL),
        out_shape=jax.ShapeDtypeStruct((m, D_MODEL), F32),
        compiler_params=_params(("parallel",)),
        name="mix",
    )(x2, o_nsa, o_moba, o_mem, g_pre, g_post, w_gates, w_nsa_o, w_moba_o, w_mem_o, w_mix_out)


FFN_CHUNK = 256


def _ffn_kernel(x_ref, g_pre_ref, g_post_ref, wg_ref, wu_ref, wd_ref, o_ref, a_sc):
    x = x_ref[...]
    h = _rms(x, g_pre_ref[...]).astype(BF16)
    d_ff = wg_ref.shape[1]
    for j in range(d_ff // FFN_CHUNK):
        sl = slice(j * FFN_CHUNK, (j + 1) * FFN_CHUNK)
        a_sc[:, sl] = (jax.nn.silu(_dot(h, wg_ref[:, sl])) * _dot(h, wu_ref[:, sl])).astype(BF16)
    f = _dot(a_sc[...], wd_ref[...])
    o_ref[...] = x + _rms(f, g_post_ref[...])


def _ffn(x2, g_pre, g_post, wg, wu, wd, tm=512):
    m = x2.shape[0]
    d_ff = wg.shape[1]
    return pl.pallas_call(
        _ffn_kernel,
        grid=(m // tm,),
        in_specs=[pl.BlockSpec((tm, D_MODEL), lambda i: (i, 0)),
                  _const_spec((1, D_MODEL)), _const_spec((1, D_MODEL)),
                  _const_spec(wg.shape), _const_spec(wu.shape), _const_spec(wd.shape)],
        out_specs=pl.BlockSpec((tm, D_MODEL), lambda i: (i, 0)),
        out_shape=jax.ShapeDtypeStruct((m, D_MODEL), F32),
        scratch_shapes=[pltpu.VMEM((tm, d_ff), BF16)],
        compiler_params=_params(("parallel",)),
        name="ffn",
    )(x2, g_pre, g_post, wg, wu, wd)


def kernel(x, mem, rel_bias, pre_mix_g, mem_norm_g, post_mix_g, w_in, cmp_pos_k, cmp_w1_k, cmp_w2_k, cmp_pos_v, cmp_w1_v, cmp_w2_v, w_mem_kv, w_nsa_o, w_moba_o, w_mem_o, w_mix_out, pre_ffn_g, post_ffn_g, w_ffn_gate, w_ffn_up, w_ffn_down):
    b, s, d_model = x.shape
    depth = w_in.shape[0]
    assert d_model == D_MODEL and s % TILE == 0 and TILE == MOBA_BLOCK == WINDOW
    assert (s - CMP_LEN) // CMP_STRIDE + 1 < N_CMP_PAD and (s // SEL_BLOCK) % SUBLANES == 0 and s // SEL_BLOCK <= HEAD_DIM
    assert w_in.shape[2] == ATT_W + 3 * D_MODEL and rel_bias.shape == (REL_BUCKETS, N_BIAS_HEADS)

    tile_idx, win_idx, cmp_idx = _bucket_tables(s)
    rel_bias = rel_bias.astype(F32)
    t_all = _expand(tile_idx, rel_bias, 0, N_BIAS_HEADS).reshape(N_BIAS_HEADS, 2, TILE, TILE)
    t_nsa = t_all[:NSA_HEADS].reshape(NSA_GROUPS, NSA_HPG, 2, TILE, TILE).transpose(0, 2, 3, 1, 4)
    t_nsa = t_nsa.reshape(NSA_GROUPS, 2, TILE, NSA_HPG * TILE)
    t_moba = t_all[NSA_HEADS:].transpose(1, 2, 0, 3).reshape(2, TILE, MOBA_HEADS * TILE)
    t_win = _expand(win_idx, rel_bias, 0, NSA_HEADS).reshape(NSA_GROUPS, NSA_HPG, TILE, TILE)
    t_win = t_win.transpose(0, 2, 1, 3).reshape(NSA_GROUPS, TILE, NSA_HPG * TILE)
    b_cmp = _expand(cmp_idx, rel_bias, 0, NSA_HEADS)
    c_far = jnp.repeat(rel_bias[REL_BUCKETS - 1] * LOG2E, TILE)
    c_far_nsa = c_far[:NSA_HEADS * TILE].reshape(NSA_GROUPS, 1, NSA_HPG * TILE)
    c_far_moba = c_far[NSA_HEADS * TILE:].reshape(1, MOBA_HEADS * TILE)
    ovt = _overlap_table(s)
    sel_cols = np.zeros((s, HEAD_DIM), np.float32)
    sel_cols[np.arange(s), np.arange(s) // SEL_BLOCK] = 1.0
    sel_cols = jnp.asarray(sel_cols, BF16)
    gate_lo = NSA_Q_W + 6 * NSA_KV_W
    rows_per_chunk = CMP_STRIDE * NSA_KV_W

    x2 = x.reshape(b * s, D_MODEL)
    mem2 = mem.reshape(-1, D_MODEL)
    for l in range(depth):
        w_att = jnp.concatenate(
            [w_in[l, :, :gate_lo + NSA_GATE_W],
             jnp.zeros((D_MODEL, GATE_PAD - NSA_GATE_W), w_in.dtype),
             w_in[l, :, gate_lo + NSA_GATE_W:ATT_W]], axis=1).astype(BF16)
        w_gates = w_in[l, :, ATT_W:].astype(BF16)
        row = lambda v: v[l].reshape(1, D_MODEL)

        qn, kc_raw, vc_raw, ks, vs, kw, vw, gn, qm, km, vm, qx = _inproj(x2, row(pre_mix_g), w_att, sel_cols)

        pk, w1k = _compress_weights(cmp_pos_k[l], cmp_w1_k[l])
        pv, w1v = _compress_weights(cmp_pos_v[l], cmp_w1_v[l])
        kc, vct = _compress(kc_raw.reshape(b, s // CMP_STRIDE, rows_per_chunk),
                            vc_raw.reshape(b, s // CMP_STRIDE, rows_per_chunk),
                            pk, pv, w1k, w1v, cmp_w2_k[l].astype(BF16), cmp_w2_v[l].astype(BF16))

        mk, mv = _memkv(mem2, row(mem_norm_g), w_mem_kv[l].astype(BF16))

        o_nsa = _nsa(b, s, qn, gn, kc, vct, ks, vs, kw, vw, b_cmp, t_nsa, t_win, c_far_nsa, ovt)
        o_moba, o_mem = _moba(b, s, qm, km, vm, qx, mk, mv, t_moba, c_far_moba)

        x2 = _mix(x2, o_nsa, o_moba, o_mem, row(pre_mix_g), row(post_mix_g), w_gates,
                  w_nsa_o[l].astype(BF16), w_moba_o[l].astype(BF16), w_mem_o[l].astype(BF16),
                  w_mix_out[l].astype(BF16))
        x2 = _ffn(x2, row(pre_ffn_g), row(post_ffn_g), w_ffn_gate[l].astype(BF16),
                  w_ffn_up[l].astype(BF16), w_ffn_down[l].astype(BF16))
    return x2.reshape(b, s, D_MODEL)
```

```python
import functools
import math

import numpy as np
import jax
import jax.numpy as jnp
from jax import lax
from jax.experimental import pallas as pl
from jax.experimental.pallas import tpu as pltpu

F32 = jnp.float32
BF16 = jnp.bfloat16

D_MODEL = 1024
HEAD_DIM = 64
SCALE = HEAD_DIM ** -0.5
LOG2E = math.log2(math.e)
Q_SCALE = SCALE * LOG2E
NSA_HEADS = 8
NSA_GROUPS = 2
NSA_HPG = NSA_HEADS // NSA_GROUPS
CMP_LEN = 32
CMP_STRIDE = 16
CMP_HIDDEN = 128
SEL_BLOCK = 64
SEL_TOPN = 8
WINDOW = 256
MOBA_HEADS = 4
MOBA_BLOCK = 256
MOBA_TOPK = 3
MEM_HEADS = 4
REL_BUCKETS = 32
REL_MAX_DIST = 128
N_BIAS_HEADS = NSA_HEADS + MOBA_HEADS
RMS_EPS = 1e-6
NEG_INF = -1e30
FORCE_SCORE = 1e4

NSA_Q_W = NSA_HEADS * HEAD_DIM
NSA_KV_W = NSA_GROUPS * HEAD_DIM
NSA_GATE_W = NSA_HEADS * 3
MOBA_W = MOBA_HEADS * HEAD_DIM
MEM_W = MEM_HEADS * HEAD_DIM
ATT_W = NSA_Q_W + 6 * NSA_KV_W + NSA_GATE_W + 3 * MOBA_W + MEM_W
LANES = 128
SUBLANES = 8
BF16_ROWS = 16
MXU_COLS = 256
GATE_PAD = LANES
TILE = 256
N_CMP_PAD = 128
V_AUG = HEAD_DIM + BF16_ROWS
MASKED_BUCKET = REL_BUCKETS
VMEM_LIMIT = 56 * 1024 * 1024


def _dot(a, b):
    return jnp.dot(a, b, preferred_element_type=F32)


def _dot_nt(a, b):
    return lax.dot_general(a, b, (((1,), (1,)), ((), ())), preferred_element_type=F32)


def _split_bf16(x):
    hi = x.astype(BF16)
    lo = (x - hi.astype(F32)).astype(BF16)
    return hi, lo


def _rms(x, g):
    return x * lax.rsqrt(jnp.mean(x * x, axis=-1, keepdims=True) + RMS_EPS) * g


def _params(sem):
    return pltpu.CompilerParams(dimension_semantics=sem, vmem_limit_bytes=VMEM_LIMIT)


def _const_spec(shape):
    nd = len(shape)
    return pl.BlockSpec(shape, lambda *_: (0,) * nd, pipeline_mode=pl.Buffered(1))


_INPROJ_OUTS = (
    ("qn", NSA_Q_W, BF16, True),
    ("kc", NSA_KV_W, F32, False), ("vc", NSA_KV_W, F32, False),
    ("ks", NSA_KV_W, BF16, False), ("vs", NSA_KV_W, BF16, False),
    ("kw", NSA_KV_W, BF16, False), ("vw", NSA_KV_W, BF16, False),
    ("gn", GATE_PAD, F32, False),
    ("qm", MOBA_W, BF16, True), ("km", MOBA_W, BF16, False), ("vm", MOBA_W, BF16, False),
    ("qx", MEM_W, BF16, True),
)
_INPROJ_W = sum(o[1] for o in _INPROJ_OUTS)
_INPROJ_CHUNKED = ("kc", "vc")
KS_AUG_W = NSA_GROUPS * 2 * HEAD_DIM


def _inproj_out_width(name, width):
    return KS_AUG_W if name == "ks" else width


def _inproj_kernel(x_ref, g_ref, w_ref, e_ref, *refs):
    out_refs, rows_sc = refs[:-1], refs[-1]
    h = _rms(x_ref[...], g_ref[...]).astype(BF16)
    runs, lo = [], 0
    for out in zip(_INPROJ_OUTS, out_refs):
        if runs and runs[-1][1] < MXU_COLS:
            runs[-1][0].append(out)
            runs[-1][1] += out[0][1]
        else:
            runs.append([[out], out[0][1], lo])
        lo += out[0][1]
    for outs, run_width, run_lo in runs:
        y_run = _dot(h, w_ref[:, run_lo:run_lo + run_width])
        lo = 0
        for (name, width, dtype, scaled), o_ref in outs:
            y = y_run[:, lo:lo + width]
            if scaled:
                y = y * Q_SCALE
            if name == "gn":
                y = jax.nn.sigmoid(y)
            y = y.astype(dtype)
            if name == "ks":
                e = e_ref[...]
                y = _lane_cat([y[:, :HEAD_DIM], e, y[:, HEAD_DIM:], e])
            if name in _INPROJ_CHUNKED:
                rows_sc[...] = y
                for j in range(CMP_STRIDE):
                    o_ref[:, j * width:(j + 1) * width] = rows_sc[pl.ds(j, o_ref.shape[0], stride=CMP_STRIDE), :]
            else:
                o_ref[...] = y
            lo += width


def _inproj(x2, g, w, e_cols, tm=512):
    m = x2.shape[0]
    tiles_per_seq = e_cols.shape[0] // tm
    outs = []
    for name, width, dtype, _ in _INPROJ_OUTS:
        if name in _INPROJ_CHUNKED:
            outs.append((CMP_STRIDE, CMP_STRIDE * width, dtype))
        else:
            outs.append((1, _inproj_out_width(name, width), dtype))
    return pl.pallas_call(
        _inproj_kernel,
        grid=(m // tm,),
        in_specs=[pl.BlockSpec((tm, D_MODEL), lambda i: (i, 0)),
                  _const_spec((1, D_MODEL)),
                  _const_spec((D_MODEL, _INPROJ_W)),
                  pl.BlockSpec((tm, HEAD_DIM), lambda i: (i % tiles_per_seq, 0))],
        out_specs=[pl.BlockSpec((tm // r, w), lambda i: (i, 0)) for r, w, _ in outs],
        out_shape=[jax.ShapeDtypeStruct((m // r, w), d) for r, w, d in outs],
        scratch_shapes=[pltpu.VMEM((tm, NSA_KV_W), F32)],
        compiler_params=_params(("parallel",)),
        name="inproj",
    )(x2, g, w, e_cols)


def _compress_kernel(rk_ref, rv_ref, pk_ref, pv_ref, w1k_ref, w1v_ref, w2k_ref, w2v_ref, kc_ref, vc_ref):
    def one(r_ref, p_ref, w1_ref, w2_ref):
        r = r_ref[0]
        top = _dot((r + p_ref[0:1, :]).astype(BF16), w1_ref[0])
        bot = _dot((r + p_ref[1:2, :]).astype(BF16), w1_ref[1])
        hid = top + pltpu.roll(bot, N_CMP_PAD - 1, 0)
        act = jax.nn.gelu(hid).astype(BF16)
        return jnp.concatenate(
            [_dot(act[:, g * CMP_HIDDEN:(g + 1) * CMP_HIDDEN], w2_ref[...]) for g in range(NSA_GROUPS)], axis=1)

    kc_ref[0] = one(rk_ref, pk_ref, w1k_ref, w2k_ref).astype(BF16)
    vc_ref[0] = one(rv_ref, pv_ref, w1v_ref, w2v_ref).T.astype(BF16)


def _compress(rk, rv, pk, pv, w1k, w1v, w2k, w2v):
    b = rk.shape[0]
    rw = rk.shape[2]
    r_spec = pl.BlockSpec((1, N_CMP_PAD, rw), lambda i: (i, 0, 0))
    o_spec = pl.BlockSpec((1, N_CMP_PAD, NSA_KV_W), lambda i: (i, 0, 0))
    return pl.pallas_call(
        _compress_kernel,
        grid=(b,),
        in_specs=[r_spec, r_spec, _const_spec(pk.shape), _const_spec(pv.shape),
                  _const_spec(w1k.shape), _const_spec(w1v.shape),
                  _const_spec(w2k.shape), _const_spec(w2v.shape)],
        out_specs=[o_spec, o_spec],
        out_shape=[jax.ShapeDtypeStruct((b, N_CMP_PAD, NSA_KV_W), BF16)] * 2,
        compiler_params=_params(("parallel",)),
        name="compress",
    )(rk, rv, pk, pv, w1k, w1v, w2k, w2v)


def _compress_weights(pos, w1):
    half = CMP_LEN // 2
    p = pos.reshape(2, half, 1, HEAD_DIM)
    p = jnp.broadcast_to(p, (2, half, NSA_GROUPS, HEAD_DIM)).reshape(2, half * NSA_KV_W)
    w = w1.reshape(2, half, HEAD_DIM, CMP_HIDDEN)
    eye = jnp.eye(NSA_GROUPS, dtype=w1.dtype)
    wbd = jnp.einsum("ajdm,gk->ajgdkm", w, eye).reshape(2, half * NSA_KV_W, NSA_GROUPS * CMP_HIDDEN)
    return p.astype(F32), wbd.astype(BF16)


def _memkv_kernel(m_ref, g_ref, w_ref, k_ref, v_ref):
    h = _rms(m_ref[...], g_ref[...]).astype(BF16)
    k_ref[...] = _dot(h, w_ref[:, :MEM_W]).astype(BF16)
    v_ref[...] = _dot(h, w_ref[:, MEM_W:]).astype(BF16)


def _memkv(mem2, g, w, tm=256):
    m = mem2.shape[0]
    o_spec = pl.BlockSpec((tm, MEM_W), lambda i: (i, 0))
    return pl.pallas_call(
        _memkv_kernel,
        grid=(m // tm,),
        in_specs=[pl.BlockSpec((tm, D_MODEL), lambda i: (i, 0)), _const_spec((1, D_MODEL)),
                  _const_spec((D_MODEL, 2 * MEM_W))],
        out_specs=[o_spec, o_spec],
        out_shape=[jax.ShapeDtypeStruct((m, MEM_W), BF16)] * 2,
        compiler_params=_params(("parallel",)),
        name="memkv",
    )(mem2, g, w)


def _expand_kernel(idx_ref, bias_ref, o_ref, *, head0, n_heads, heads_per_group):
    rows, cols = idx_ref.shape

    def body(i, carry):
        r = pl.multiple_of(i * SUBLANES, SUBLANES)
        for c0 in range(0, cols, TILE):
            idx = idx_ref[pl.ds(r, SUBLANES), c0:c0 + TILE]
            out = [jnp.full(idx.shape, NEG_INF, F32)] * n_heads
            for bkt in range(REL_BUCKETS):
                hit = idx == bkt
                out = [jnp.where(hit, bias_ref[bkt, head0 + h], out[h]) for h in range(n_heads)]
            for h in range(n_heads):
                col = (h % heads_per_group) * cols + c0
                o_ref[h // heads_per_group, pl.ds(r, SUBLANES), col:col + TILE] = out[h] * LOG2E
        return carry

    lax.fori_loop(0, rows // SUBLANES, body, 0)


def _expand(idx, rel_bias, head0, n_heads, heads_per_group=1):
    rows, cols = idx.shape
    return pl.pallas_call(
        functools.partial(_expand_kernel, head0=head0, n_heads=n_heads, heads_per_group=heads_per_group),
        in_specs=[pl.BlockSpec(memory_space=pltpu.VMEM), pl.BlockSpec(memory_space=pltpu.SMEM)],
        out_specs=pl.BlockSpec(memory_space=pltpu.VMEM),
        out_shape=jax.ShapeDtypeStruct((n_heads // heads_per_group, rows, heads_per_group * cols), F32),
        compiler_params=pltpu.CompilerParams(vmem_limit_bytes=VMEM_LIMIT),
        name="bias_expand",
    )(idx, rel_bias)


def _t5_bucket_np(dist):
    dist = np.maximum(dist, 0)
    max_exact = REL_BUCKETS // 2
    logd = np.log(np.maximum(dist, 1).astype(np.float32) / max_exact) / math.log(REL_MAX_DIST / max_exact)
    large = np.minimum(max_exact + (logd * (REL_BUCKETS - max_exact)).astype(np.int32), REL_BUCKETS - 1)
    return np.where(dist < max_exact, dist, large).astype(np.int32)


def _bucket_tables(s):
    j = np.arange(TILE)[:, None]
    i = np.arange(TILE)[None, :]
    assert TILE + 1 >= REL_MAX_DIST
    tiles = []
    for d in range(2):
        dist = d * TILE + i - j
        tiles.append(np.where(dist >= 0, _t5_bucket_np(dist), MASKED_BUCKET))
    dist1 = TILE + i - j
    win = np.where(dist1 < WINDOW, _t5_bucket_np(dist1), MASKED_BUCKET)
    n_cmp = (s - CMP_LEN) // CMP_STRIDE + 1
    c = np.arange(N_CMP_PAD)[:, None]
    dist_c = np.arange(s)[None, :] - (c * CMP_STRIDE + CMP_LEN - 1)
    cmp_idx = np.where((dist_c >= 0) & (c < n_cmp), _t5_bucket_np(dist_c), MASKED_BUCKET)
    as_i32 = lambda a: jnp.asarray(a.astype(np.int32))
    return as_i32(np.concatenate(tiles, axis=0)), as_i32(win), as_i32(cmp_idx)


def _overlap_table(s):
    n_cmp = (s - CMP_LEN) // CMP_STRIDE + 1
    n_sel = s // SEL_BLOCK
    cs = np.arange(n_cmp) * CMP_STRIDE
    ss = np.arange(n_sel) * SEL_BLOCK
    ov = np.clip(np.minimum(cs[:, None] + CMP_LEN, ss[None, :] + SEL_BLOCK)
                 - np.maximum(cs[:, None], ss[None, :]), 0, None).astype(np.float32) / CMP_LEN
    ovt = np.zeros((n_sel, N_CMP_PAD), np.float32)
    ovt[:, :n_cmp] = ov.T
    return jnp.asarray(ovt, BF16)


def _store_v_aug(vt_sc, idx, vt):
    ones = jnp.ones((BF16_ROWS, vt.shape[1]), BF16)
    vt_sc[idx] = jnp.concatenate([vt.astype(BF16), ones], axis=0)


def _lane_cat(xs):
    return jnp.concatenate(xs, axis=1)


def _flash_pipelined(own, streams, s_sc, m_ref, acc_ref):
    has_prev = jnp.where(own > 0, 1.0, 0.0).astype(F32)
    prev = jnp.maximum(own - 1, 0)
    n_far = jnp.maximum(own - 1, 0)

    def absorb(g, s, kt, c_row, w_row):
        u = jnp.max(s, axis=0, keepdims=True) + c_row
        m_old = m_ref[g]
        m_new = jnp.maximum(m_old, jnp.where(w_row > 0.0, u, NEG_INF))
        alpha = jnp.exp2(m_old - m_new)
        shift = jnp.maximum(m_new, u) - c_row
        p = jnp.exp2(s - shift).astype(BF16)
        acc_ref[g] = alpha * acc_ref[g] + w_row * streams[g]["pv"](kt)(p)
        m_ref[g] = m_new

    def absorb_slot(g, i):
        is_prev = i == 0
        kt = jnp.where(is_prev, prev, i - 1)
        c_row = jnp.where(is_prev, 0.0, streams[g]["c_far"])
        w_row = streams[g]["w"](kt) * jnp.where(is_prev, has_prev, 1.0)
        absorb(g, s_sc[g], kt, c_row, w_row)

    m_ref[...] = jnp.full(m_ref.shape, NEG_INF, F32)
    acc_ref[...] = jnp.zeros(acc_ref.shape, F32)
    s_own = [stream["own"]() for stream in streams]
    for g, stream in enumerate(streams):
        s_sc[g] = stream["prev"]()
    zero_row = jnp.zeros((1, s_own[0].shape[1]), F32)
    for g in range(len(streams)):
        absorb(g, s_own[g], own, zero_row, zero_row + 1.0)

    def body(i, carry):
        nxt = [stream["far"](i) for stream in streams]
        for g in range(len(streams)):
            absorb_slot(g, i)
        for g in range(len(streams)):
            s_sc[g] = nxt[g]
        return carry

    lax.fori_loop(0, n_far, body, 0)
    for g in range(len(streams)):
        absorb_slot(g, n_far)


def _softmax_av(s_list, pv_list):
    m = s_list[0].max(axis=0, keepdims=True)
    for s in s_list[1:]:
        m = jnp.maximum(m, s.max(axis=0, keepdims=True))
    acc = None
    for s, pv in zip(s_list, pv_list):
        part = pv(jnp.exp2(s - m).astype(BF16))
        acc = part if acc is None else acc + part
    return acc


def _normalize(acc):
    return acc[:HEAD_DIM] / acc[HEAD_DIM:HEAD_DIM + 1]


def _rank_before(score, n_cand):
    blk = lax.broadcasted_iota(jnp.int32, score.shape, 0)
    rank = jnp.zeros(score.shape, F32)
    for m in range(n_cand):
        row = score[m:m + 1, :]
        tie = jnp.where(blk > m, 1.0, 0.0)
        rank = rank + jnp.where(row > score, 1.0, 0.0) + jnp.where(row == score, tie, 0.0)
    return rank


def _nsa_kernel(q_ref, gn_ref, kc_ref, vct_ref, ks_ref, vs_ref, kw_ref, vw_ref,
                bct_ref, tt_ref, twt_ref, cfar_ref, ovt_ref, o_ref,
                vst_sc, vwt_sc, qa_sc, og_sc, s_sc, m_sc, acc_sc, ot_sc):
    qi = pl.program_id(1)
    nt = ks_ref.shape[1]
    n_sel = ovt_ref.shape[0]

    @pl.when(qi == 0)
    def _():
        for kt in range(nt):
            vs_t = vs_ref[0, kt].astype(F32).T
            vw_t = vw_ref[0, kt].astype(F32).T
            for g in range(NSA_GROUPS):
                _store_v_aug(vst_sc, (kt, g), vs_t[g * HEAD_DIM:(g + 1) * HEAD_DIM])
                _store_v_aug(vwt_sc, (kt, g), vw_t[g * HEAD_DIM:(g + 1) * HEAD_DIM])

    pos = lax.broadcasted_iota(jnp.int32, (1, TILE), 1) + qi * TILE
    cur = pos // SEL_BLOCK
    has_cmp = pos >= CMP_LEN - 1
    blk = lax.broadcasted_iota(jnp.int32, (n_sel, TILE), 0)
    prev = jnp.maximum(qi - 1, 0)
    no_prev = jnp.where(qi == 0, NEG_INF, 0.0).astype(F32)
    gates = gn_ref[...].T

    gsls = [slice(g * HEAD_DIM, (g + 1) * HEAD_DIM) for g in range(NSA_GROUPS)]
    group_heads = [[g * NSA_HPG + j for j in range(NSA_HPG)] for g in range(NSA_GROUPS)]

    def gate(g, branch):
        return _lane_cat([gates[3 * h + branch:3 * h + branch + 1, :] for h in group_heads[g]])

    for g in range(NSA_GROUPS):
        heads = group_heads[g]
        q4 = jnp.concatenate([q_ref[:, h * HEAD_DIM:(h + 1) * HEAD_DIM] for h in heads], axis=0)
        qa_sc[g, :, 0:HEAD_DIM] = q4

        s0 = _dot_nt(kw_ref[0, prev, :, gsls[g]], q4) + (twt_ref[g] + no_prev)
        s1 = _dot_nt(kw_ref[0, qi, :, gsls[g]], q4) + tt_ref[g, 0]
        acc_w = _softmax_av([s0, s1], [lambda pr: _dot(vwt_sc[prev, g], pr), lambda pr: _dot(vwt_sc[qi, g], pr)])
        o_win = gate(g, 2) * _normalize(acc_w)

        kc = kc_ref[0, :, gsls[g]]
        vct = vct_ref[0, gsls[g], :]
        s = _dot_nt(kc, q4) + _lane_cat([bct_ref[h] for h in heads])
        e = jnp.exp2(s - jnp.max(s, axis=0, keepdims=True))
        p = jnp.where(_lane_cat([has_cmp] * NSA_HPG), e / jnp.sum(e, axis=0, keepdims=True), 0.0)
        psum = p[:, :TILE]
        for j in range(1, NSA_HPG):
            psum = psum + p[:, j * TILE:(j + 1) * TILE]
        og_sc[g] = gate(g, 0) * _dot(vct, p.astype(BF16)) + o_win

        p_hi, p_lo = _split_bf16(psum)
        imp = _dot(ovt_ref[...], p_hi) + _dot(ovt_ref[...], p_lo)
        forced = (blk == 0) | (blk == cur) | (blk == cur - 1)
        score = jnp.where(forced, FORCE_SCORE, jnp.where(blk <= cur, imp, NEG_INF))
        rank = _rank_before(score, n_sel)
        sel = jnp.where(rank < SEL_TOPN, jnp.where(score > NEG_INF / 2, 0.0, NEG_INF), NEG_INF)
        sel_t = jnp.concatenate([sel, jnp.zeros((LANES - n_sel, TILE), F32)], axis=0).T[:, :HEAD_DIM]
        for j in range(NSA_HPG):
            qa_sc[g, j * TILE:(j + 1) * TILE, HEAD_DIM:2 * HEAD_DIM] = sel_t.astype(BF16)

    ones_row = jnp.ones((1, NSA_HPG * TILE), F32)

    def sel_stream(g):
        def qk(kt):
            return _dot_nt(ks_ref[0, kt, :, g * 2 * HEAD_DIM:(g + 1) * 2 * HEAD_DIM], qa_sc[g])

        return dict(own=lambda: qk(qi) + tt_ref[g, 0], prev=lambda: qk(prev) + tt_ref[g, 1], far=qk,
                    c_far=cfar_ref[g], w=lambda kt: ones_row,
                    pv=lambda kt: (lambda pr: _dot(vst_sc[kt, g], pr)))

    _flash_pipelined(qi, [sel_stream(g) for g in range(NSA_GROUPS)], s_sc, m_sc, acc_sc)

    for g in range(NSA_GROUPS):
        o = og_sc[g] + gate(g, 1) * _normalize(acc_sc[g])
        for j, h in enumerate(group_heads[g]):
            ot_sc[h * HEAD_DIM:(h + 1) * HEAD_DIM, :] = o[:, j * TILE:(j + 1) * TILE]

    o_ref[...] = ot_sc[...].T.astype(BF16)


def _nsa(b, s, qn, gn, kc, vct, ks, vs, kw, vw, bias_cmp, t_nsa, t_win, c_far, ovt):
    nt = s // TILE
    kv_spec = pl.BlockSpec((1, nt, TILE, NSA_KV_W), lambda i, j: (i, 0, 0, 0))
    ks_spec = pl.BlockSpec((1, nt, TILE, KS_AUG_W), lambda i, j: (i, 0, 0, 0))
    c_spec = pl.BlockSpec((1, N_CMP_PAD, NSA_KV_W), lambda i, j: (i, 0, 0))
    tile4 = lambda a: a.reshape(b, nt, TILE, a.shape[-1])
    return pl.pallas_call(
        _nsa_kernel,
        grid=(b, nt),
        in_specs=[pl.BlockSpec((TILE, NSA_Q_W), lambda i, j: (i * nt + j, 0)),
                  pl.BlockSpec((TILE, GATE_PAD), lambda i, j: (i * nt + j, 0)),
                  c_spec, c_spec, ks_spec, kv_spec, kv_spec, kv_spec,
                  pl.BlockSpec((NSA_HEADS, N_CMP_PAD, TILE), lambda i, j: (0, 0, j)),
                  _const_spec(t_nsa.shape), _const_spec(t_win.shape), _const_spec(c_far.shape),
                  _const_spec(ovt.shape)],
        out_specs=pl.BlockSpec((TILE, NSA_Q_W), lambda i, j: (i * nt + j, 0)),
        out_shape=jax.ShapeDtypeStruct((b * s, NSA_Q_W), BF16),
        scratch_shapes=[pltpu.VMEM((nt, NSA_GROUPS, V_AUG, TILE), BF16),
                        pltpu.VMEM((nt, NSA_GROUPS, V_AUG, TILE), BF16),
                        pltpu.VMEM((NSA_GROUPS, NSA_HPG * TILE, 2 * HEAD_DIM), BF16),
                        pltpu.VMEM((NSA_GROUPS, HEAD_DIM, NSA_HPG * TILE), F32),
                        pltpu.VMEM((NSA_GROUPS, TILE, NSA_HPG * TILE), F32),
                        pltpu.VMEM((NSA_GROUPS, 1, NSA_HPG * TILE), F32),
                        pltpu.VMEM((NSA_GROUPS, V_AUG, NSA_HPG * TILE), F32),
                        pltpu.VMEM((NSA_Q_W, TILE), F32)],
        compiler_params=_params(("arbitrary", "arbitrary")),
        name="nsa",
    )(qn, gn, kc, vct, tile4(ks), tile4(vs), tile4(kw), tile4(vw), bias_cmp, t_nsa, t_win, c_far, ovt)


def _moba_kernel(qm_ref, km_ref, vm_ref, qx_ref, mk_ref, mv_ref, tt_ref, cfar_ref, om_ref, ox_ref,
                 vmt_sc, mvt_sc, kmean_sc, qbd_sc, sel_sc, s_sc, m_sc, acc_sc, ot_sc):
    c = pl.program_id(1)
    nt = km_ref.shape[1]
    hsls = [slice(h * HEAD_DIM, (h + 1) * HEAD_DIM) for h in range(MOBA_HEADS)]

    @pl.when(c == 0)
    def _():
        kmean_sc[...] = jnp.zeros(kmean_sc.shape, F32)
        for n in range(nt):
            kmean_sc[n:n + 1, :] = jnp.sum(km_ref[0, n].astype(F32), axis=0, keepdims=True) * (1.0 / MOBA_BLOCK)
            vt = vm_ref[0, n].astype(F32).T
            for h in range(MOBA_HEADS):
                _store_v_aug(vmt_sc, (n, h), vt[hsls[h]])
        mvt = mv_ref[0].astype(F32).T
        for h in range(MOBA_HEADS):
            _store_v_aug(mvt_sc, h, mvt[hsls[h]])

    lane_head = lax.broadcasted_iota(jnp.int32, (TILE, MOBA_W), 1) // HEAD_DIM

    def block_diag(q_ref):
        q = q_ref[...].astype(F32)
        return jnp.concatenate([jnp.where(lane_head == h, q, 0.0) for h in range(MOBA_HEADS)],
                               axis=0).astype(BF16)

    def per_head_pv(vts):
        return lambda pr: _lane_cat([_dot(vts(h), pr[:, h * TILE:(h + 1) * TILE]) for h in range(MOBA_HEADS)])

    def store_heads(o_t, out_ref):
        for h in range(MOBA_HEADS):
            ot_sc[hsls[h], :] = o_t[:, h * TILE:(h + 1) * TILE]
        out_ref[...] = ot_sc[...].T.astype(BF16)

    qbd = block_diag(qm_ref)
    km_hi, km_lo = _split_bf16(kmean_sc[...])
    gate = _dot_nt(km_hi, qbd) + _dot_nt(km_lo, qbd)
    blk = lax.broadcasted_iota(jnp.int32, gate.shape, 0)
    score = jnp.where(blk < c, gate, NEG_INF * Q_SCALE)
    rank = _rank_before(score, nt)
    sel_sc[...] = jnp.where(rank < MOBA_TOPK, jnp.where(score > NEG_INF * Q_SCALE / 2, 1.0, 0.0), 0.0)

    qbd_sc[...] = qbd
    qk = lambda n: _dot_nt(km_ref[0, n], qbd_sc[...])
    stream = dict(own=lambda: qk(c) + tt_ref[0], prev=lambda: qk(jnp.maximum(c - 1, 0)) + tt_ref[1], far=qk,
                  c_far=cfar_ref[...], w=lambda n: sel_sc[pl.ds(n, 1), :],
                  pv=lambda n: per_head_pv(lambda h: vmt_sc[n, h]))
    s = _dot_nt(mk_ref[0], block_diag(qx_ref))
    store_heads(_normalize(_softmax_av([s], [per_head_pv(lambda h: mvt_sc[h])])), ox_ref)

    _flash_pipelined(c, [stream], s_sc, m_sc, acc_sc)
    store_heads(_normalize(acc_sc[0]), om_ref)


def _moba(b, s, qm, km, vm, qx, mk, mv, t_moba, c_far):
    nt = s // TILE
    mem_len = mk.shape[0] // b
    assert MOBA_TOPK <= nt - 1 and nt <= BF16_ROWS
    q_spec = pl.BlockSpec((TILE, MOBA_W), lambda i, j: (i * nt + j, 0))
    kv_spec = pl.BlockSpec((1, nt, TILE, MOBA_W), lambda i, j: (i, 0, 0, 0))
    mem_spec = pl.BlockSpec((1, mem_len, MEM_W), lambda i, j: (i, 0, 0))
    return pl.pallas_call(
        _moba_kernel,
        grid=(b, nt),
        in_specs=[q_spec, kv_spec, kv_spec, q_spec, mem_spec, mem_spec, _const_spec(t_moba.shape),
                  _const_spec(c_far.shape)],
        out_specs=[q_spec, q_spec],
        out_shape=[jax.ShapeDtypeStruct((b * s, MOBA_W), BF16), jax.ShapeDtypeStruct((b * s, MEM_W), BF16)],
        scratch_shapes=[pltpu.VMEM((nt, MOBA_HEADS, V_AUG, TILE), BF16),
                        pltpu.VMEM((MEM_HEADS, V_AUG, mem_len), BF16),
                        pltpu.VMEM((BF16_ROWS, MOBA_W), F32),
                        pltpu.VMEM((MOBA_HEADS * TILE, MOBA_W), BF16),
                        pltpu.VMEM((BF16_ROWS, MOBA_HEADS * TILE), F32),
                        pltpu.VMEM((1, TILE, MOBA_HEADS * TILE), F32),
                        pltpu.VMEM((1, 1, MOBA_HEADS * TILE), F32),
                        pltpu.VMEM((1, V_AUG, MOBA_HEADS * TILE), F32),
                        pltpu.VMEM((MOBA_W, TILE), F32)],
        compiler_params=_params(("arbitrary", "arbitrary")),
        name="moba",
    )(qm, km.reshape(b, nt, TILE, MOBA_W), vm.reshape(b, nt, TILE, MOBA_W), qx,
      mk.reshape(b, mem_len, MEM_W), mv.reshape(b, mem_len, MEM_W), t_moba, c_far)


def _mix_kernel(x_ref, on_ref, om_ref, ox_ref, g_pre_ref, g_post_ref, wg_ref, wn_ref, wm_ref, wx_ref,
                wo_ref, o_ref):
    x = x_ref[...]
    h = _rms(x, g_pre_ref[...]).astype(BF16)
    merged = jax.nn.sigmoid(_dot(h, wg_ref[:, :D_MODEL])) * _dot(on_ref[...], wn_ref[...])
    merged = merged + jax.nn.sigmoid(_dot(h, wg_ref[:, D_MODEL:2 * D_MODEL])) * _dot(om_ref[...], wm_ref[...])
    merged = merged + jax.nn.sigmoid(_dot(h, wg_ref[:, 2 * D_MODEL:])) * _dot(ox_ref[...], wx_ref[...])
    y = _dot(merged.astype(BF16), wo_ref[...])
    o_ref[...] = x + _rms(y, g_post_ref[...])


def _mix(x2, o_nsa, o_moba, o_mem, g_pre, g_post, w_gates, w_nsa_o, w_moba_o, w_mem_o, w_mix_out, tm=512):
    m = x2.shape[0]
    row = lambda w: pl.BlockSpec((tm, w), lambda i: (i, 0))
    return pl.pallas_call(
        _mix_kernel,
        grid=(m // tm,),
        in_specs=[row(D_MODEL), row(NSA_Q_W), row(MOBA_W), row(MEM_W),
                  _const_spec((1, D_MODEL)), _const_spec((1, D_MODEL)),
                  _const_spec(w_gates.shape), _const_spec(w_nsa_o.shape), _const_spec(w_moba_o.shape),
                  _const_spec(w_mem_o.shape), _const_spec(w_mix_out.shape)],
        out_specs=row(D_MODEL),
        out_shape=jax.ShapeDtypeStruct((m, D_MODEL), F32),
        compiler_params=_params(("parallel",)),
        name="mix",
    )(x2, o_nsa, o_moba, o_mem, g_pre, g_post, w_gates, w_nsa_o, w_moba_o, w_mem_o, w_mix_out)


FFN_CHUNK = 256


def _ffn_kernel(x_ref, g_pre_ref, g_post_ref, wg_ref, wu_ref, wd_ref, o_ref, a_sc):
    x = x_ref[...]
    h = _rms(x, g_pre_ref[...]).astype(BF16)
    d_ff = wg_ref.shape[1]
    for j in range(d_ff // FFN_CHUNK):
        sl = slice(j * FFN_CHUNK, (j + 1) * FFN_CHUNK)
        a_sc[:, sl] = (jax.nn.silu(_dot(h, wg_ref[:, sl])) * _dot(h, wu_ref[:, sl])).astype(BF16)
    f = _dot(a_sc[...], wd_ref[...])
    o_ref[...] = x + _rms(f, g_post_ref[...])


def _ffn(x2, g_pre, g_post, wg, wu, wd, tm=512):
    m = x2.shape[0]
    d_ff = wg.shape[1]
    return pl.pallas_call(
        _ffn_kernel,
        grid=(m // tm,),
        in_specs=[pl.BlockSpec((tm, D_MODEL), lambda i: (i, 0)),
                  _const_spec((1, D_MODEL)), _const_spec((1, D_MODEL)),
                  _const_spec(wg.shape), _const_spec(wu.shape), _const_spec(wd.shape)],
        out_specs=pl.BlockSpec((tm, D_MODEL), lambda i: (i, 0)),
        out_shape=jax.ShapeDtypeStruct((m, D_MODEL), F32),
        scratch_shapes=[pltpu.VMEM((tm, d_ff), BF16)],
        compiler_params=_params(("parallel",)),
        name="ffn",
    )(x2, g_pre, g_post, wg, wu, wd)


def kernel(x, mem, rel_bias, pre_mix_g, mem_norm_g, post_mix_g, w_in, cmp_pos_k, cmp_w1_k, cmp_w2_k, cmp_pos_v, cmp_w1_v, cmp_w2_v, w_mem_kv, w_nsa_o, w_moba_o, w_mem_o, w_mix_out, pre_ffn_g, post_ffn_g, w_ffn_gate, w_ffn_up, w_ffn_down):
    b, s, d_model = x.shape
    depth = w_in.shape[0]
    assert d_model == D_MODEL and s % TILE == 0 and TILE == MOBA_BLOCK == WINDOW
    assert (s - CMP_LEN) // CMP_STRIDE + 1 < N_CMP_PAD and (s // SEL_BLOCK) % SUBLANES == 0 and s // SEL_BLOCK <= HEAD_DIM
    assert w_in.shape[2] == ATT_W + 3 * D_MODEL and rel_bias.shape == (REL_BUCKETS, N_BIAS_HEADS)

    tile_idx, win_idx, cmp_idx = _bucket_tables(s)
    rel_bias = rel_bias.astype(F32)
    t_nsa = _expand(tile_idx, rel_bias, 0, NSA_HEADS, NSA_HPG).reshape(NSA_GROUPS, 2, TILE, NSA_HPG * TILE)
    t_moba = _expand(tile_idx, rel_bias, NSA_HEADS, MOBA_HEADS, MOBA_HEADS).reshape(2, TILE, MOBA_HEADS * TILE)
    t_win = _expand(win_idx, rel_bias, 0, NSA_HEADS, NSA_HPG)
    b_cmp = _expand(cmp_idx, rel_bias, 0, NSA_HEADS)
    c_far = jnp.repeat(rel_bias[REL_BUCKETS - 1] * LOG2E, TILE)
    c_far_nsa = c_far[:NSA_HEADS * TILE].reshape(NSA_GROUPS, 1, NSA_HPG * TILE)
    c_far_moba = c_far[NSA_HEADS * TILE:].reshape(1, MOBA_HEADS * TILE)
    ovt = _overlap_table(s)
    sel_cols = np.zeros((s, HEAD_DIM), np.float32)
    sel_cols[np.arange(s), np.arange(s) // SEL_BLOCK] = 1.0
    sel_cols = jnp.asarray(sel_cols, BF16)
    gate_lo = NSA_Q_W + 6 * NSA_KV_W
    rows_per_chunk = CMP_STRIDE * NSA_KV_W

    x2 = x.reshape(b * s, D_MODEL)
    mem2 = mem.reshape(-1, D_MODEL)
    for l in range(depth):
        w_att = jnp.concatenate(
            [w_in[l, :, :gate_lo + NSA_GATE_W],
             jnp.zeros((D_MODEL, GATE_PAD - NSA_GATE_W), w_in.dtype),
             w_in[l, :, gate_lo + NSA_GATE_W:ATT_W]], axis=1).astype(BF16)
        w_gates = w_in[l, :, ATT_W:].astype(BF16)
        row = lambda v: v[l].reshape(1, D_MODEL)

        qn, kc_raw, vc_raw, ks, vs, kw, vw, gn, qm, km, vm, qx = _inproj(x2, row(pre_mix_g), w_att, sel_cols)

        pk, w1k = _compress_weights(cmp_pos_k[l], cmp_w1_k[l])
        pv, w1v = _compress_weights(cmp_pos_v[l], cmp_w1_v[l])
        kc, vct = _compress(kc_raw.reshape(b, s // CMP_STRIDE, rows_per_chunk),
                            vc_raw.reshape(b, s // CMP_STRIDE, rows_per_chunk),
                            pk, pv, w1k, w1v, cmp_w2_k[l].astype(BF16), cmp_w2_v[l].astype(BF16))

        mk, mv = _memkv(mem2, row(mem_norm_g), w_mem_kv[l].astype(BF16))

        o_nsa = _nsa(b, s, qn, gn, kc, vct, ks, vs, kw, vw, b_cmp, t_nsa, t_win, c_far_nsa, ovt)
        o_moba, o_mem = _moba(b, s, qm, km, vm, qx, mk, mv, t_moba, c_far_moba)

        x2 = _mix(x2, o_nsa, o_moba, o_mem, row(pre_mix_g), row(post_mix_g), w_gates,
                  w_nsa_o[l].astype(BF16), w_moba_o[l].astype(BF16), w_mem_o[l].astype(BF16),
                  w_mix_out[l].astype(BF16))
        x2 = _ffn(x2, row(pre_ffn_g), row(post_ffn_g), w_ffn_gate[l].astype(BF16),
                  w_ffn_up[l].astype(BF16), w_ffn_down[l].astype(BF16))
    return x2.reshape(b, s, D_MODEL)
```

```python
import functools
import math

import numpy as np
import jax
import jax.numpy as jnp
from jax import lax
from jax.experimental import pallas as pl
from jax.experimental.pallas import tpu as pltpu

F32 = jnp.float32
BF16 = jnp.bfloat16

D_MODEL = 1024
HEAD_DIM = 64
SCALE = HEAD_DIM ** -0.5
LOG2E = math.log2(math.e)
Q_SCALE = SCALE * LOG2E
NSA_HEADS = 8
NSA_GROUPS = 2
NSA_HPG = NSA_HEADS // NSA_GROUPS
CMP_LEN = 32
CMP_STRIDE = 16
CMP_HIDDEN = 128
SEL_BLOCK = 64
SEL_TOPN = 8
WINDOW = 256
MOBA_HEADS = 4
MOBA_BLOCK = 256
MOBA_TOPK = 3
MEM_HEADS = 4
REL_BUCKETS = 32
REL_MAX_DIST = 128
N_BIAS_HEADS = NSA_HEADS + MOBA_HEADS
RMS_EPS = 1e-6
NEG_INF = -1e30
FORCE_SCORE = 1e4

NSA_Q_W = NSA_HEADS * HEAD_DIM
NSA_KV_W = NSA_GROUPS * HEAD_DIM
NSA_GATE_W = NSA_HEADS * 3
MOBA_W = MOBA_HEADS * HEAD_DIM
MEM_W = MEM_HEADS * HEAD_DIM
ATT_W = NSA_Q_W + 6 * NSA_KV_W + NSA_GATE_W + 3 * MOBA_W + MEM_W
LANES = 128
SUBLANES = 8
BF16_ROWS = 16
MXU_COLS = 256
GATE_PAD = LANES
TILE = 256
N_CMP_PAD = 128
V_AUG = HEAD_DIM + BF16_ROWS
MASKED_BUCKET = REL_BUCKETS
VMEM_LIMIT = 56 * 1024 * 1024


def _dot(a, b):
    return jnp.dot(a, b, preferred_element_type=F32)


def _dot_nt(a, b):
    return lax.dot_general(a, b, (((1,), (1,)), ((), ())), preferred_element_type=F32)


def _split_bf16(x):
    hi = x.astype(BF16)
    lo = (x - hi.astype(F32)).astype(BF16)
    return hi, lo


def _rms(x, g):
    return x * lax.rsqrt(jnp.mean(x * x, axis=-1, keepdims=True) + RMS_EPS) * g


def _params(sem):
    return pltpu.CompilerParams(dimension_semantics=sem, vmem_limit_bytes=VMEM_LIMIT)


def _const_spec(shape):
    nd = len(shape)
    return pl.BlockSpec(shape, lambda *_: (0,) * nd, pipeline_mode=pl.Buffered(1))


_INPROJ_OUTS = (
    ("qn", NSA_Q_W, BF16, True),
    ("kc", NSA_KV_W, F32, False), ("vc", NSA_KV_W, F32, False),
    ("ks", NSA_KV_W, BF16, False), ("vs", NSA_KV_W, BF16, False),
    ("kw", NSA_KV_W, BF16, False), ("vw", NSA_KV_W, BF16, False),
    ("gn", GATE_PAD, F32, False),
    ("qm", MOBA_W, BF16, True), ("km", MOBA_W, BF16, False), ("vm", MOBA_W, BF16, False),
    ("qx", MEM_W, BF16, True),
)
_INPROJ_W = sum(o[1] for o in _INPROJ_OUTS)
_INPROJ_CHUNKED = ("kc", "vc")
KS_AUG_W = NSA_GROUPS * 2 * HEAD_DIM


def _inproj_out_width(name, width):
    return KS_AUG_W if name == "ks" else width


def _inproj_kernel(x_ref, g_ref, w_ref, e_ref, *refs):
    out_refs, rows_sc = refs[:-1], refs[-1]
    h = _rms(x_ref[...], g_ref[...]).astype(BF16)
    runs, lo = [], 0
    for out in zip(_INPROJ_OUTS, out_refs):
        if runs and runs[-1][1] < MXU_COLS:
            runs[-1][0].append(out)
            runs[-1][1] += out[0][1]
        else:
            runs.append([[out], out[0][1], lo])
        lo += out[0][1]
    for outs, run_width, run_lo in runs:
        y_run = _dot(h, w_ref[:, run_lo:run_lo + run_width])
        lo = 0
        for (name, width, dtype, scaled), o_ref in outs:
            y = y_run[:, lo:lo + width]
            if scaled:
                y = y * Q_SCALE
            if name == "gn":
                y = jax.nn.sigmoid(y)
            y = y.astype(dtype)
            if name == "ks":
                e = e_ref[...]
                y = _lane_cat([y[:, :HEAD_DIM], e, y[:, HEAD_DIM:], e])
            if name in _INPROJ_CHUNKED:
                rows_sc[...] = y
                for j in range(CMP_STRIDE):
                    o_ref[:, j * width:(j + 1) * width] = rows_sc[pl.ds(j, o_ref.shape[0], stride=CMP_STRIDE), :]
            else:
                o_ref[...] = y
            lo += width


def _inproj(x2, g, w, e_cols, tm=512):
    m = x2.shape[0]
    tiles_per_seq = e_cols.shape[0] // tm
    outs = []
    for name, width, dtype, _ in _INPROJ_OUTS:
        if name in _INPROJ_CHUNKED:
            outs.append((CMP_STRIDE, CMP_STRIDE * width, dtype))
        else:
            outs.append((1, _inproj_out_width(name, width), dtype))
    return pl.pallas_call(
        _inproj_kernel,
        grid=(m // tm,),
        in_specs=[pl.BlockSpec((tm, D_MODEL), lambda i: (i, 0)),
                  _const_spec((1, D_MODEL)),
                  _const_spec((D_MODEL, _INPROJ_W)),
                  pl.BlockSpec((tm, HEAD_DIM), lambda i: (i % tiles_per_seq, 0))],
        out_specs=[pl.BlockSpec((tm // r, w), lambda i: (i, 0)) for r, w, _ in outs],
        out_shape=[jax.ShapeDtypeStruct((m // r, w), d) for r, w, d in outs],
        scratch_shapes=[pltpu.VMEM((tm, NSA_KV_W), F32)],
        compiler_params=_params(("parallel",)),
        name="inproj",
    )(x2, g, w, e_cols)


def _compress_kernel(rk_ref, rv_ref, pk_ref, pv_ref, w1k_ref, w1v_ref, w2k_ref, w2v_ref, kc_ref, vc_ref):
    def one(r_ref, p_ref, w1_ref, w2_ref):
        r = r_ref[0]
        top = _dot((r + p_ref[0:1, :]).astype(BF16), w1_ref[0])
        bot = _dot((r + p_ref[1:2, :]).astype(BF16), w1_ref[1])
        hid = top + pltpu.roll(bot, N_CMP_PAD - 1, 0)
        act = jax.nn.gelu(hid).astype(BF16)
        return jnp.concatenate(
            [_dot(act[:, g * CMP_HIDDEN:(g + 1) * CMP_HIDDEN], w2_ref[...]) for g in range(NSA_GROUPS)], axis=1)

    kc_ref[0] = one(rk_ref, pk_ref, w1k_ref, w2k_ref).astype(BF16)
    vc_ref[0] = one(rv_ref, pv_ref, w1v_ref, w2v_ref).T.astype(BF16)


def _compress(rk, rv, pk, pv, w1k, w1v, w2k, w2v):
    b = rk.shape[0]
    rw = rk.shape[2]
    r_spec = pl.BlockSpec((1, N_CMP_PAD, rw), lambda i: (i, 0, 0))
    o_spec = pl.BlockSpec((1, N_CMP_PAD, NSA_KV_W), lambda i: (i, 0, 0))
    return pl.pallas_call(
        _compress_kernel,
        grid=(b,),
        in_specs=[r_spec, r_spec, _const_spec(pk.shape), _const_spec(pv.shape),
                  _const_spec(w1k.shape), _const_spec(w1v.shape),
                  _const_spec(w2k.shape), _const_spec(w2v.shape)],
        out_specs=[o_spec, o_spec],
        out_shape=[jax.ShapeDtypeStruct((b, N_CMP_PAD, NSA_KV_W), BF16)] * 2,
        compiler_params=_params(("parallel",)),
        name="compress",
    )(rk, rv, pk, pv, w1k, w1v, w2k, w2v)


def _compress_weights(pos, w1):
    half = CMP_LEN // 2
    p = pos.reshape(2, half, 1, HEAD_DIM)
    p = jnp.broadcast_to(p, (2, half, NSA_GROUPS, HEAD_DIM)).reshape(2, half * NSA_KV_W)
    w = w1.reshape(2, half, HEAD_DIM, CMP_HIDDEN)
    eye = jnp.eye(NSA_GROUPS, dtype=w1.dtype)
    wbd = jnp.einsum("ajdm,gk->ajgdkm", w, eye).reshape(2, half * NSA_KV_W, NSA_GROUPS * CMP_HIDDEN)
    return p.astype(F32), wbd.astype(BF16)


def _memkv_kernel(m_ref, g_ref, w_ref, k_ref, v_ref):
    h = _rms(m_ref[...], g_ref[...]).astype(BF16)
    k_ref[...] = _dot(h, w_ref[:, :MEM_W]).astype(BF16)
    v_ref[...] = _dot(h, w_ref[:, MEM_W:]).astype(BF16)


def _memkv(mem2, g, w, tm=256):
    m = mem2.shape[0]
    o_spec = pl.BlockSpec((tm, MEM_W), lambda i: (i, 0))
    return pl.pallas_call(
        _memkv_kernel,
        grid=(m // tm,),
        in_specs=[pl.BlockSpec((tm, D_MODEL), lambda i: (i, 0)), _const_spec((1, D_MODEL)),
                  _const_spec((D_MODEL, 2 * MEM_W))],
        out_specs=[o_spec, o_spec],
        out_shape=[jax.ShapeDtypeStruct((m, MEM_W), BF16)] * 2,
        compiler_params=_params(("parallel",)),
        name="memkv",
    )(mem2, g, w)


def _expand_kernel(idx_ref, bias_ref, o_ref, *, head0, n_heads, heads_per_group):
    rows, cols = idx_ref.shape

    def body(i, carry):
        r = pl.multiple_of(i * SUBLANES, SUBLANES)
        for c0 in range(0, cols, TILE):
            idx = idx_ref[pl.ds(r, SUBLANES), c0:c0 + TILE]
            out = [jnp.full(idx.shape, NEG_INF, F32)] * n_heads
            for bkt in range(REL_BUCKETS):
                hit = idx == bkt
                out = [jnp.where(hit, bias_ref[bkt, head0 + h], out[h]) for h in range(n_heads)]
            for h in range(n_heads):
                col = (h % heads_per_group) * cols + c0
                o_ref[h // heads_per_group, pl.ds(r, SUBLANES), col:col + TILE] = out[h] * LOG2E
        return carry

    lax.fori_loop(0, rows // SUBLANES, body, 0)


def _expand(idx, rel_bias, head0, n_heads, heads_per_group=1):
    rows, cols = idx.shape
    return pl.pallas_call(
        functools.partial(_expand_kernel, head0=head0, n_heads=n_heads, heads_per_group=heads_per_group),
        in_specs=[pl.BlockSpec(memory_space=pltpu.VMEM), pl.BlockSpec(memory_space=pltpu.SMEM)],
        out_specs=pl.BlockSpec(memory_space=pltpu.VMEM),
        out_shape=jax.ShapeDtypeStruct((n_heads // heads_per_group, rows, heads_per_group * cols), F32),
        compiler_params=pltpu.CompilerParams(vmem_limit_bytes=VMEM_LIMIT),
        name="bias_expand",
    )(idx, rel_bias)


def _t5_bucket_np(dist):
    dist = np.maximum(dist, 0)
    max_exact = REL_BUCKETS // 2
    logd = np.log(np.maximum(dist, 1).astype(np.float32) / max_exact) / math.log(REL_MAX_DIST / max_exact)
    large = np.minimum(max_exact + (logd * (REL_BUCKETS - max_exact)).astype(np.int32), REL_BUCKETS - 1)
    return np.where(dist < max_exact, dist, large).astype(np.int32)


def _bucket_tables(s):
    j = np.arange(TILE)[:, None]
    i = np.arange(TILE)[None, :]
    assert TILE + 1 >= REL_MAX_DIST
    tiles = []
    for d in range(2):
        dist = d * TILE + i - j
        tiles.append(np.where(dist >= 0, _t5_bucket_np(dist), MASKED_BUCKET))
    dist1 = TILE + i - j
    win = np.where(dist1 < WINDOW, _t5_bucket_np(dist1), MASKED_BUCKET)
    n_cmp = (s - CMP_LEN) // CMP_STRIDE + 1
    c = np.arange(N_CMP_PAD)[:, None]
    dist_c = np.arange(s)[None, :] - (c * CMP_STRIDE + CMP_LEN - 1)
    cmp_idx = np.where((dist_c >= 0) & (c < n_cmp), _t5_bucket_np(dist_c), MASKED_BUCKET)
    as_i32 = lambda a: jnp.asarray(a.astype(np.int32))
    return as_i32(np.concatenate(tiles, axis=0)), as_i32(win), as_i32(cmp_idx)


def _overlap_table(s):
    n_cmp = (s - CMP_LEN) // CMP_STRIDE + 1
    n_sel = s // SEL_BLOCK
    cs = np.arange(n_cmp) * CMP_STRIDE
    ss = np.arange(n_sel) * SEL_BLOCK
    ov = np.clip(np.minimum(cs[:, None] + CMP_LEN, ss[None, :] + SEL_BLOCK)
                 - np.maximum(cs[:, None], ss[None, :]), 0, None).astype(np.float32) / CMP_LEN
    ovt = np.zeros((n_sel, N_CMP_PAD), np.float32)
    ovt[:, :n_cmp] = ov.T
    return jnp.asarray(ovt, BF16)


def _store_v_aug(vt_sc, idx, vt):
    ones = jnp.ones((BF16_ROWS, vt.shape[1]), BF16)
    vt_sc[idx] = jnp.concatenate([vt.astype(BF16), ones], axis=0)


def _lane_cat(xs):
    return jnp.concatenate(xs, axis=1)


def _flash_pipelined(own, streams, s_sc, m_ref, acc_ref):
    has_prev = jnp.where(own > 0, 1.0, 0.0).astype(F32)
    prev = jnp.maximum(own - 1, 0)
    n_far = jnp.maximum(own - 1, 0)

    def absorb(g, s, kt, c_row, w_row):
        u = jnp.max(s, axis=0, keepdims=True) + c_row
        m_old = m_ref[g]
        m_new = jnp.maximum(m_old, jnp.where(w_row > 0.0, u, NEG_INF))
        alpha = jnp.exp2(m_old - m_new)
        shift = jnp.maximum(m_new, u) - c_row
        p = jnp.exp2(s - shift).astype(BF16)
        acc_ref[g] = alpha * acc_ref[g] + w_row * streams[g]["pv"](kt)(p)
        m_ref[g] = m_new

    def absorb_slot(g, i):
        is_prev = i == 0
        kt = jnp.where(is_prev, prev, i - 1)
        c_row = jnp.where(is_prev, 0.0, streams[g]["c_far"])
        w_row = streams[g]["w"](kt) * jnp.where(is_prev, has_prev, 1.0)
        absorb(g, s_sc[g], kt, c_row, w_row)

    m_ref[...] = jnp.full(m_ref.shape, NEG_INF, F32)
    acc_ref[...] = jnp.zeros(acc_ref.shape, F32)
    s_own = [stream["own"]() for stream in streams]
    for g, stream in enumerate(streams):
        s_sc[g] = stream["prev"]()
    zero_row = jnp.zeros((1, s_own[0].shape[1]), F32)
    for g in range(len(streams)):
        absorb(g, s_own[g], own, zero_row, zero_row + 1.0)

    def body(i, carry):
        nxt = [stream["far"](i) for stream in streams]
        for g in range(len(streams)):
            absorb_slot(g, i)
        for g in range(len(streams)):
            s_sc[g] = nxt[g]
        return carry

    lax.fori_loop(0, n_far, body, 0)
    for g in range(len(streams)):
        absorb_slot(g, n_far)


def _softmax_av(s_list, pv_list):
    m = s_list[0].max(axis=0, keepdims=True)
    for s in s_list[1:]:
        m = jnp.maximum(m, s.max(axis=0, keepdims=True))
    acc = None
    for s, pv in zip(s_list, pv_list):
        part = pv(jnp.exp2(s - m).astype(BF16))
        acc = part if acc is None else acc + part
    return acc


def _normalize(acc):
    return acc[:HEAD_DIM] / acc[HEAD_DIM:HEAD_DIM + 1]


def _rank_before(score, n_cand):
    blk = lax.broadcasted_iota(jnp.int32, score.shape, 0)
    rank = jnp.zeros(score.shape, F32)
    for m in range(n_cand):
        row = score[m:m + 1, :]
        tie = jnp.where(blk > m, 1.0, 0.0)
        rank = rank + jnp.where(row > score, 1.0, 0.0) + jnp.where(row == score, tie, 0.0)
    return rank


def _nsa_kernel(q_ref, gn_ref, kc_ref, vct_ref, ks_ref, vs_ref, kw_ref, vw_ref,
                bct_ref, tt_ref, twt_ref, cfar_ref, ovt_ref, o_ref,
                vst_sc, vwt_sc, qa_sc, og_sc, s_sc, m_sc, acc_sc, ot_sc):
    qi = pl.program_id(1)
    nt = ks_ref.shape[1]
    n_sel = ovt_ref.shape[0]

    @pl.when(qi == 0)
    def _():
        for kt in range(nt):
            vs_t = vs_ref[0, kt].astype(F32).T
            vw_t = vw_ref[0, kt].astype(F32).T
            for g in range(NSA_GROUPS):
                _store_v_aug(vst_sc, (kt, g), vs_t[g * HEAD_DIM:(g + 1) * HEAD_DIM])
                _store_v_aug(vwt_sc, (kt, g), vw_t[g * HEAD_DIM:(g + 1) * HEAD_DIM])

    pos = lax.broadcasted_iota(jnp.int32, (1, TILE), 1) + qi * TILE
    cur = pos // SEL_BLOCK
    has_cmp = pos >= CMP_LEN - 1
    blk = lax.broadcasted_iota(jnp.int32, (n_sel, TILE), 0)
    prev = jnp.maximum(qi - 1, 0)
    no_prev = jnp.where(qi == 0, NEG_INF, 0.0).astype(F32)
    gates = gn_ref[...].T

    gsls = [slice(g * HEAD_DIM, (g + 1) * HEAD_DIM) for g in range(NSA_GROUPS)]
    group_heads = [[g * NSA_HPG + j for j in range(NSA_HPG)] for g in range(NSA_GROUPS)]

    def gate(g, branch):
        return _lane_cat([gates[3 * h + branch:3 * h + branch + 1, :] for h in group_heads[g]])

    for g in range(NSA_GROUPS):
        heads = group_heads[g]
        q4 = jnp.concatenate([q_ref[:, h * HEAD_DIM:(h + 1) * HEAD_DIM] for h in heads], axis=0)
        qa_sc[g, :, 0:HEAD_DIM] = q4

        s0 = _dot_nt(kw_ref[0, prev, :, gsls[g]], q4) + (twt_ref[g] + no_prev)
        s1 = _dot_nt(kw_ref[0, qi, :, gsls[g]], q4) + tt_ref[g, 0]
        acc_w = _softmax_av([s0, s1], [lambda pr: _dot(vwt_sc[prev, g], pr), lambda pr: _dot(vwt_sc[qi, g], pr)])
        o_win = gate(g, 2) * _normalize(acc_w)

        kc = kc_ref[0, :, gsls[g]]
        vct = vct_ref[0, gsls[g], :]
        q_cols = pl.ds(pl.multiple_of(qi * TILE, TILE), TILE)
        s = _dot_nt(kc, q4) + _lane_cat([bct_ref[h, :, q_cols] for h in heads])
        e = jnp.exp2(s - jnp.max(s, axis=0, keepdims=True))
        p = jnp.where(_lane_cat([has_cmp] * NSA_HPG), e / jnp.sum(e, axis=0, keepdims=True), 0.0)
        psum = p[:, :TILE]
        for j in range(1, NSA_HPG):
            psum = psum + p[:, j * TILE:(j + 1) * TILE]
        og_sc[g] = gate(g, 0) * _dot(vct, p.astype(BF16)) + o_win

        p_hi, p_lo = _split_bf16(psum)
        imp = _dot(ovt_ref[...], p_hi) + _dot(ovt_ref[...], p_lo)
        forced = (blk == 0) | (blk == cur) | (blk == cur - 1)
        score = jnp.where(forced, FORCE_SCORE, jnp.where(blk <= cur, imp, NEG_INF))
        rank = _rank_before(score, n_sel)
        sel = jnp.where(rank < SEL_TOPN, jnp.where(score > NEG_INF / 2, 0.0, NEG_INF), NEG_INF)
        sel_t = jnp.concatenate([sel, jnp.zeros((LANES - n_sel, TILE), F32)], axis=0).T[:, :HEAD_DIM]
        for j in range(NSA_HPG):
            qa_sc[g, j * TILE:(j + 1) * TILE, HEAD_DIM:2 * HEAD_DIM] = sel_t.astype(BF16)

    ones_row = jnp.ones((1, NSA_HPG * TILE), F32)

    def sel_stream(g):
        def qk(kt):
            return _dot_nt(ks_ref[0, kt, :, g * 2 * HEAD_DIM:(g + 1) * 2 * HEAD_DIM], qa_sc[g])

        return dict(own=lambda: qk(qi) + tt_ref[g, 0], prev=lambda: qk(prev) + tt_ref[g, 1], far=qk,
                    c_far=cfar_ref[g], w=lambda kt: ones_row,
                    pv=lambda kt: (lambda pr: _dot(vst_sc[kt, g], pr)))

    _flash_pipelined(qi, [sel_stream(g) for g in range(NSA_GROUPS)], s_sc, m_sc, acc_sc)

    for g in range(NSA_GROUPS):
        o = og_sc[g] + gate(g, 1) * _normalize(acc_sc[g])
        for j, h in enumerate(group_heads[g]):
            ot_sc[h * HEAD_DIM:(h + 1) * HEAD_DIM, :] = o[:, j * TILE:(j + 1) * TILE]

    o_ref[...] = ot_sc[...].T.astype(BF16)


def _nsa(b, s, qn, gn, kc, vct, ks, vs, kw, vw, bias_cmp, t_nsa, t_win, c_far, ovt):
    nt = s // TILE
    kv_spec = pl.BlockSpec((1, nt, TILE, NSA_KV_W), lambda i, j: (i, 0, 0, 0))
    ks_spec = pl.BlockSpec((1, nt, TILE, KS_AUG_W), lambda i, j: (i, 0, 0, 0))
    c_spec = pl.BlockSpec((1, N_CMP_PAD, NSA_KV_W), lambda i, j: (i, 0, 0))
    tile4 = lambda a: a.reshape(b, nt, TILE, a.shape[-1])
    return pl.pallas_call(
        _nsa_kernel,
        grid=(b, nt),
        in_specs=[pl.BlockSpec((TILE, NSA_Q_W), lambda i, j: (i * nt + j, 0)),
                  pl.BlockSpec((TILE, GATE_PAD), lambda i, j: (i * nt + j, 0)),
                  c_spec, c_spec, ks_spec, kv_spec, kv_spec, kv_spec,
                  _const_spec(bias_cmp.shape),
                  _const_spec(t_nsa.shape), _const_spec(t_win.shape), _const_spec(c_far.shape),
                  _const_spec(ovt.shape)],
        out_specs=pl.BlockSpec((TILE, NSA_Q_W), lambda i, j: (i * nt + j, 0)),
        out_shape=jax.ShapeDtypeStruct((b * s, NSA_Q_W), BF16),
        scratch_shapes=[pltpu.VMEM((nt, NSA_GROUPS, V_AUG, TILE), BF16),
                        pltpu.VMEM((nt, NSA_GROUPS, V_AUG, TILE), BF16),
                        pltpu.VMEM((NSA_GROUPS, NSA_HPG * TILE, 2 * HEAD_DIM), BF16),
                        pltpu.VMEM((NSA_GROUPS, HEAD_DIM, NSA_HPG * TILE), F32),
                        pltpu.VMEM((NSA_GROUPS, TILE, NSA_HPG * TILE), F32),
                        pltpu.VMEM((NSA_GROUPS, 1, NSA_HPG * TILE), F32),
                        pltpu.VMEM((NSA_GROUPS, V_AUG, NSA_HPG * TILE), F32),
                        pltpu.VMEM((NSA_Q_W, TILE), F32)],
        compiler_params=_params(("arbitrary", "arbitrary")),
        name="nsa",
    )(qn, gn, kc, vct, tile4(ks), tile4(vs), tile4(kw), tile4(vw), bias_cmp, t_nsa, t_win, c_far, ovt)


def _moba_kernel(qm_ref, km_ref, vm_ref, qx_ref, mk_ref, mv_ref, tt_ref, cfar_ref, om_ref, ox_ref,
                 vmt_sc, mvt_sc, kmean_sc, qbd_sc, sel_sc, s_sc, m_sc, acc_sc, ot_sc):
    c = pl.program_id(1)
    nt = km_ref.shape[1]
    hsls = [slice(h * HEAD_DIM, (h + 1) * HEAD_DIM) for h in range(MOBA_HEADS)]

    @pl.when(c == 0)
    def _():
        kmean_sc[...] = jnp.zeros(kmean_sc.shape, F32)
        for n in range(nt):
            kmean_sc[n:n + 1, :] = jnp.sum(km_ref[0, n].astype(F32), axis=0, keepdims=True) * (1.0 / MOBA_BLOCK)
            vt = vm_ref[0, n].astype(F32).T
            for h in range(MOBA_HEADS):
                _store_v_aug(vmt_sc, (n, h), vt[hsls[h]])
        mvt = mv_ref[0].astype(F32).T
        for h in range(MOBA_HEADS):
            _store_v_aug(mvt_sc, h, mvt[hsls[h]])

    lane_head = lax.broadcasted_iota(jnp.int32, (TILE, MOBA_W), 1) // HEAD_DIM

    def block_diag(q_ref):
        q = q_ref[...].astype(F32)
        return jnp.concatenate([jnp.where(lane_head == h, q, 0.0) for h in range(MOBA_HEADS)],
                               axis=0).astype(BF16)

    def per_head_pv(vts):
        return lambda pr: _lane_cat([_dot(vts(h), pr[:, h * TILE:(h + 1) * TILE]) for h in range(MOBA_HEADS)])

    def store_heads(o_t, out_ref):
        for h in range(MOBA_HEADS):
            ot_sc[hsls[h], :] = o_t[:, h * TILE:(h + 1) * TILE]
        out_ref[...] = ot_sc[...].T.astype(BF16)

    qbd = block_diag(qm_ref)
    km_hi, km_lo = _split_bf16(kmean_sc[...])
    gate = _dot_nt(km_hi, qbd) + _dot_nt(km_lo, qbd)
    blk = lax.broadcasted_iota(jnp.int32, gate.shape, 0)
    score = jnp.where(blk < c, gate, NEG_INF * Q_SCALE)
    rank = _rank_before(score, nt)
    sel_sc[...] = jnp.where(rank < MOBA_TOPK, jnp.where(score > NEG_INF * Q_SCALE / 2, 1.0, 0.0), 0.0)

    qbd_sc[...] = qbd
    qk = lambda n: _dot_nt(km_ref[0, n], qbd_sc[...])
    stream = dict(own=lambda: qk(c) + tt_ref[0], prev=lambda: qk(jnp.maximum(c - 1, 0)) + tt_ref[1], far=qk,
                  c_far=cfar_ref[...], w=lambda n: sel_sc[pl.ds(n, 1), :],
                  pv=lambda n: per_head_pv(lambda h: vmt_sc[n, h]))
    s = _dot_nt(mk_ref[0], block_diag(qx_ref))
    store_heads(_normalize(_softmax_av([s], [per_head_pv(lambda h: mvt_sc[h])])), ox_ref)

    _flash_pipelined(c, [stream], s_sc, m_sc, acc_sc)
    store_heads(_normalize(acc_sc[0]), om_ref)


def _moba(b, s, qm, km, vm, qx, mk, mv, t_moba, c_far):
    nt = s // TILE
    mem_len = mk.shape[0] // b
    assert MOBA_TOPK <= nt - 1 and nt <= BF16_ROWS
    q_spec = pl.BlockSpec((TILE, MOBA_W), lambda i, j: (i * nt + j, 0))
    kv_spec = pl.BlockSpec((1, nt, TILE, MOBA_W), lambda i, j: (i, 0, 0, 0))
    mem_spec = pl.BlockSpec((1, mem_len, MEM_W), lambda i, j: (i, 0, 0))
    return pl.pallas_call(
        _moba_kernel,
        grid=(b, nt),
        in_specs=[q_spec, kv_spec, kv_spec, q_spec, mem_spec, mem_spec, _const_spec(t_moba.shape),
                  _const_spec(c_far.shape)],
        out_specs=[q_spec, q_spec],
        out_shape=[jax.ShapeDtypeStruct((b * s, MOBA_W), BF16), jax.ShapeDtypeStruct((b * s, MEM_W), BF16)],
        scratch_shapes=[pltpu.VMEM((nt, MOBA_HEADS, V_AUG, TILE), BF16),
                        pltpu.VMEM((MEM_HEADS, V_AUG, mem_len), BF16),
                        pltpu.VMEM((BF16_ROWS, MOBA_W), F32),
                        pltpu.VMEM((MOBA_HEADS * TILE, MOBA_W), BF16),
                        pltpu.VMEM((BF16_ROWS, MOBA_HEADS * TILE), F32),
                        pltpu.VMEM((1, TILE, MOBA_HEADS * TILE), F32),
                        pltpu.VMEM((1, 1, MOBA_HEADS * TILE), F32),
                        pltpu.VMEM((1, V_AUG, MOBA_HEADS * TILE), F32),
                        pltpu.VMEM((MOBA_W, TILE), F32)],
        compiler_params=_params(("arbitrary", "arbitrary")),
        name="moba",
    )(qm, km.reshape(b, nt, TILE, MOBA_W), vm.reshape(b, nt, TILE, MOBA_W), qx,
      mk.reshape(b, mem_len, MEM_W), mv.reshape(b, mem_len, MEM_W), t_moba, c_far)


def _mix_kernel(x_ref, on_ref, om_ref, ox_ref, g_pre_ref, g_post_ref, wg_ref, wn_ref, wm_ref, wx_ref,
                wo_ref, o_ref):
    x = x_ref[...]
    h = _rms(x, g_pre_ref[...]).astype(BF16)
    merged = jax.nn.sigmoid(_dot(h, wg_ref[:, :D_MODEL])) * _dot(on_ref[...], wn_ref[...])
    merged = merged + jax.nn.sigmoid(_dot(h, wg_ref[:, D_MODEL:2 * D_MODEL])) * _dot(om_ref[...], wm_ref[...])
    merged = merged + jax.nn.sigmoid(_dot(h, wg_ref[:, 2 * D_MODEL:])) * _dot(ox_ref[...], wx_ref[...])
    y = _dot(merged.astype(BF16), wo_ref[...])
    o_ref[...] = x + _rms(y, g_post_ref[...])


def _mix(x2, o_nsa, o_moba, o_mem, g_pre, g_post, w_gates, w_nsa_o, w_moba_o, w_mem_o, w_mix_out, tm=512):
    m = x2.shape[0]
    row = lambda w: pl.BlockSpec((tm, w), lambda i: (i, 0))
    return pl.pallas_call(
        _mix_kernel,
        grid=(m // tm,),
        in_specs=[row(D_MODEL), row(NSA_Q_W), row(MOBA_W), row(MEM_W),
                  _const_spec((1, D_MODEL)), _const_spec((1, D_MODEL)),
                  _const_spec(w_gates.shape), _const_spec(w_nsa_o.shape), _const_spec(w_moba_o.shape),
                  _const_spec(w_mem_o.shape), _const_spec(w_mix_out.shape)],
        out_specs=row(D_MODEL),
        out_shape=jax.ShapeDtypeStruct((m, D_MODEL), F32),
        compiler_params=_params(("parallel",)),
        name="mix",
    )(x2, o_nsa, o_moba, o_mem, g_pre, g_post, w_gates, w_nsa_o, w_moba_o, w_mem_o, w_mix_out)


FFN_CHUNK = 256


def _ffn_kernel(x_ref, g_pre_ref, g_post_ref, wg_ref, wu_ref, wd_ref, o_ref, a_sc):
    x = x_ref[...]
    h = _rms(x, g_pre_ref[...]).astype(BF16)
    d_ff = wg_ref.shape[1]
    for j in range(d_ff // FFN_CHUNK):
        sl = slice(j * FFN_CHUNK, (j + 1) * FFN_CHUNK)
        a_sc[:, sl] = (jax.nn.silu(_dot(h, wg_ref[:, sl])) * _dot(h, wu_ref[:, sl])).astype(BF16)
    f = _dot(a_sc[...], wd_ref[...])
    o_ref[...] = x + _rms(f, g_post_ref[...])


def _ffn(x2, g_pre, g_post, wg, wu, wd, tm=512):
    m = x2.shape[0]
    d_ff = wg.shape[1]
    return pl.pallas_call(
        _ffn_kernel,
        grid=(m // tm,),
        in_specs=[pl.BlockSpec((tm, D_MODEL), lambda i: (i, 0)),
                  _const_spec((1, D_MODEL)), _const_spec((1, D_MODEL)),
                  _const_spec(wg.shape), _const_spec(wu.shape), _const_spec(wd.shape)],
        out_specs=pl.BlockSpec((tm, D_MODEL), lambda i: (i, 0)),
        out_shape=jax.ShapeDtypeStruct((m, D_MODEL), F32),
        scratch_shapes=[pltpu.VMEM((tm, d_ff), BF16)],
        compiler_params=_params(("parallel",)),
        name="ffn",
    )(x2, g_pre, g_post, wg, wu, wd)


def kernel(x, mem, rel_bias, pre_mix_g, mem_norm_g, post_mix_g, w_in, cmp_pos_k, cmp_w1_k, cmp_w2_k, cmp_pos_v, cmp_w1_v, cmp_w2_v, w_mem_kv, w_nsa_o, w_moba_o, w_mem_o, w_mix_out, pre_ffn_g, post_ffn_g, w_ffn_gate, w_ffn_up, w_ffn_down):
    b, s, d_model = x.shape
    depth = w_in.shape[0]
    assert d_model == D_MODEL and s % TILE == 0 and TILE == MOBA_BLOCK == WINDOW
    assert (s - CMP_LEN) // CMP_STRIDE + 1 < N_CMP_PAD and (s // SEL_BLOCK) % SUBLANES == 0 and s // SEL_BLOCK <= HEAD_DIM
    assert w_in.shape[2] == ATT_W + 3 * D_MODEL and rel_bias.shape == (REL_BUCKETS, N_BIAS_HEADS)

    tile_idx, win_idx, cmp_idx = _bucket_tables(s)
    rel_bias = rel_bias.astype(F32)
    t_nsa = _expand(tile_idx, rel_bias, 0, NSA_HEADS, NSA_HPG).reshape(NSA_GROUPS, 2, TILE, NSA_HPG * TILE)
    t_moba = _expand(tile_idx, rel_bias, NSA_HEADS, MOBA_HEADS, MOBA_HEADS).reshape(2, TILE, MOBA_HEADS * TILE)
    t_win = _expand(win_idx, rel_bias, 0, NSA_HEADS, NSA_HPG)
    b_cmp = _expand(cmp_idx, rel_bias, 0, NSA_HEADS)
    c_far = jnp.repeat(rel_bias[REL_BUCKETS - 1] * LOG2E, TILE)
    c_far_nsa = c_far[:NSA_HEADS * TILE].reshape(NSA_GROUPS, 1, NSA_HPG * TILE)
    c_far_moba = c_far[NSA_HEADS * TILE:].reshape(1, MOBA_HEADS * TILE)
    ovt = _overlap_table(s)
    sel_cols = np.zeros((s, HEAD_DIM), np.float32)
    sel_cols[np.arange(s), np.arange(s) // SEL_BLOCK] = 1.0
    sel_cols = jnp.asarray(sel_cols, BF16)
    gate_lo = NSA_Q_W + 6 * NSA_KV_W
    rows_per_chunk = CMP_STRIDE * NSA_KV_W

    x2 = x.reshape(b * s, D_MODEL)
    mem2 = mem.reshape(-1, D_MODEL)
    for l in range(depth):
        w_att = jnp.concatenate(
            [w_in[l, :, :gate_lo + NSA_GATE_W],
             jnp.zeros((D_MODEL, GATE_PAD - NSA_GATE_W), w_in.dtype),
             w_in[l, :, gate_lo + NSA_GATE_W:ATT_W]], axis=1).astype(BF16)
        w_gates = w_in[l, :, ATT_W:].astype(BF16)
        row = lambda v: v[l].reshape(1, D_MODEL)

        qn, kc_raw, vc_raw, ks, vs, kw, vw, gn, qm, km, vm, qx = _inproj(x2, row(pre_mix_g), w_att, sel_cols)

        pk, w1k = _compress_weights(cmp_pos_k[l], cmp_w1_k[l])
        pv, w1v = _compress_weights(cmp_pos_v[l], cmp_w1_v[l])
        kc, vct = _compress(kc_raw.reshape(b, s // CMP_STRIDE, rows_per_chunk),
                            vc_raw.reshape(b, s // CMP_STRIDE, rows_per_chunk),
                            pk, pv, w1k, w1v, cmp_w2_k[l].astype(BF16), cmp_w2_v[l].astype(BF16))

        mk, mv = _memkv(mem2, row(mem_norm_g), w_mem_kv[l].astype(BF16))

        o_nsa = _nsa(b, s, qn, gn, kc, vct, ks, vs, kw, vw, b_cmp, t_nsa, t_win, c_far_nsa, ovt)
        o_moba, o_mem = _moba(b, s, qm, km, vm, qx, mk, mv, t_moba, c_far_moba)

        x2 = _mix(x2, o_nsa, o_moba, o_mem, row(pre_mix_g), row(post_mix_g), w_gates,
                  w_nsa_o[l].astype(BF16), w_moba_o[l].astype(BF16), w_mem_o[l].astype(BF16),
                  w_mix_out[l].astype(BF16))
        x2 = _ffn(x2, row(pre_ffn_g), row(post_ffn_g), w_ffn_gate[l].astype(BF16),
                  w_ffn_up[l].astype(BF16), w_ffn_down[l].astype(BF16))
    return x2.reshape(b, s, D_MODEL)
```

```python
import functools
import math

import numpy as np
import jax
import jax.numpy as jnp
from jax import lax
from jax.experimental import pallas as pl
from jax.experimental.pallas import tpu as pltpu

F32 = jnp.float32
BF16 = jnp.bfloat16

D_MODEL = 1024
HEAD_DIM = 64
SCALE = HEAD_DIM ** -0.5
LOG2E = math.log2(math.e)
Q_SCALE = SCALE * LOG2E
NSA_HEADS = 8
NSA_GROUPS = 2
NSA_HPG = NSA_HEADS // NSA_GROUPS
CMP_LEN = 32
CMP_STRIDE = 16
CMP_HIDDEN = 128
SEL_BLOCK = 64
SEL_TOPN = 8
WINDOW = 256
MOBA_HEADS = 4
MOBA_BLOCK = 256
MOBA_TOPK = 3
MEM_HEADS = 4
REL_BUCKETS = 32
REL_MAX_DIST = 128
N_BIAS_HEADS = NSA_HEADS + MOBA_HEADS
RMS_EPS = 1e-6
NEG_INF = -1e30
FORCE_SCORE = 1e4

NSA_Q_W = NSA_HEADS * HEAD_DIM
NSA_KV_W = NSA_GROUPS * HEAD_DIM
NSA_GATE_W = NSA_HEADS * 3
MOBA_W = MOBA_HEADS * HEAD_DIM
MEM_W = MEM_HEADS * HEAD_DIM
ATT_W = NSA_Q_W + 6 * NSA_KV_W + NSA_GATE_W + 3 * MOBA_W + MEM_W
LANES = 128
SUBLANES = 8
BF16_ROWS = 16
MXU_COLS = 256
GATE_PAD = LANES
TILE = 256
N_CMP_PAD = 128
V_AUG = HEAD_DIM + BF16_ROWS
MASKED_BUCKET = REL_BUCKETS
VMEM_LIMIT = 56 * 1024 * 1024


def _dot(a, b):
    return jnp.dot(a, b, preferred_element_type=F32)


def _split_bf16(x):
    hi = x.astype(BF16)
    lo = (x - hi.astype(F32)).astype(BF16)
    return hi, lo


def _rms(x, g):
    return x * lax.rsqrt(jnp.mean(x * x, axis=-1, keepdims=True) + RMS_EPS) * g


def _params(sem):
    return pltpu.CompilerParams(dimension_semantics=sem, vmem_limit_bytes=VMEM_LIMIT)


def _const_spec(shape):
    nd = len(shape)
    return pl.BlockSpec(shape, lambda *_: (0,) * nd, pipeline_mode=pl.Buffered(1))


_INPROJ_OUTS = (
    ("qn", NSA_Q_W, BF16, True),
    ("kc", NSA_KV_W, F32, False), ("vc", NSA_KV_W, F32, False),
    ("ks", NSA_KV_W, BF16, False), ("vs", NSA_KV_W, BF16, False),
    ("kw", NSA_KV_W, BF16, False), ("vw", NSA_KV_W, BF16, False),
    ("gn", GATE_PAD, F32, False),
    ("qm", MOBA_W, BF16, True), ("km", MOBA_W, BF16, False), ("vm", MOBA_W, BF16, False),
    ("qx", MEM_W, BF16, True),
)
_INPROJ_W = sum(o[1] for o in _INPROJ_OUTS)
_INPROJ_CHUNKED = ("kc", "vc")
KS_AUG_W = NSA_GROUPS * 2 * HEAD_DIM


def _inproj_out_width(name, width):
    return KS_AUG_W if name == "ks" else width


def _inproj_kernel(x_ref, g_ref, w_ref, e_ref, *refs):
    out_refs, rows_sc = refs[:-1], refs[-1]
    h = _rms(x_ref[...], g_ref[...]).astype(BF16)
    runs, lo = [], 0
    for out in zip(_INPROJ_OUTS, out_refs):
        if runs and runs[-1][1] < MXU_COLS:
            runs[-1][0].append(out)
            runs[-1][1] += out[0][1]
        else:
            runs.append([[out], out[0][1], lo])
        lo += out[0][1]
    for outs, run_width, run_lo in runs:
        y_run = _dot(h, w_ref[:, run_lo:run_lo + run_width])
        lo = 0
        for (name, width, dtype, scaled), o_ref in outs:
            y = y_run[:, lo:lo + width]
            if scaled:
                y = y * Q_SCALE
            if name == "gn":
                y = jax.nn.sigmoid(y)
            y = y.astype(dtype)
            if name == "ks":
                e = e_ref[...]
                y = _lane_cat([y[:, :HEAD_DIM], e, y[:, HEAD_DIM:], e])
            if name in _INPROJ_CHUNKED:
                rows_sc[...] = y
                for j in range(CMP_STRIDE):
                    o_ref[:, j * width:(j + 1) * width] = rows_sc[pl.ds(j, o_ref.shape[0], stride=CMP_STRIDE), :]
            else:
                o_ref[...] = y
            lo += width


def _inproj(x2, g, w, e_cols, tm=512):
    m = x2.shape[0]
    tiles_per_seq = e_cols.shape[0] // tm
    outs = []
    for name, width, dtype, _ in _INPROJ_OUTS:
        if name in _INPROJ_CHUNKED:
            outs.append((CMP_STRIDE, CMP_STRIDE * width, dtype))
        else:
            outs.append((1, _inproj_out_width(name, width), dtype))
    return pl.pallas_call(
        _inproj_kernel,
        grid=(m // tm,),
        in_specs=[pl.BlockSpec((tm, D_MODEL), lambda i: (i, 0)),
                  _const_spec((1, D_MODEL)),
                  _const_spec((D_MODEL, _INPROJ_W)),
                  pl.BlockSpec((tm, HEAD_DIM), lambda i: (i % tiles_per_seq, 0))],
        out_specs=[pl.BlockSpec((tm // r, w), lambda i: (i, 0)) for r, w, _ in outs],
        out_shape=[jax.ShapeDtypeStruct((m // r, w), d) for r, w, d in outs],
        scratch_shapes=[pltpu.VMEM((tm, NSA_KV_W), F32)],
        compiler_params=_params(("parallel",)),
        name="inproj",
    )(x2, g, w, e_cols)


def _compress_kernel(rk_ref, rv_ref, pk_ref, pv_ref, w1k_ref, w1v_ref, w2k_ref, w2v_ref, kc_ref, vc_ref):
    def one(r_ref, p_ref, w1_ref, w2_ref):
        r = r_ref[0]
        top = _dot((r + p_ref[0:1, :]).astype(BF16), w1_ref[0])
        bot = _dot((r + p_ref[1:2, :]).astype(BF16), w1_ref[1])
        hid = top + pltpu.roll(bot, N_CMP_PAD - 1, 0)
        act = jax.nn.gelu(hid).astype(BF16)
        return jnp.concatenate(
            [_dot(act[:, g * CMP_HIDDEN:(g + 1) * CMP_HIDDEN], w2_ref[...]) for g in range(NSA_GROUPS)], axis=1)

    kc_ref[0] = one(rk_ref, pk_ref, w1k_ref, w2k_ref).astype(BF16)
    vc_ref[0] = one(rv_ref, pv_ref, w1v_ref, w2v_ref).T.astype(BF16)


def _compress(rk, rv, pk, pv, w1k, w1v, w2k, w2v):
    b = rk.shape[0]
    rw = rk.shape[2]
    r_spec = pl.BlockSpec((1, N_CMP_PAD, rw), lambda i: (i, 0, 0))
    o_spec = pl.BlockSpec((1, N_CMP_PAD, NSA_KV_W), lambda i: (i, 0, 0))
    return pl.pallas_call(
        _compress_kernel,
        grid=(b,),
        in_specs=[r_spec, r_spec, _const_spec(pk.shape), _const_spec(pv.shape),
                  _const_spec(w1k.shape), _const_spec(w1v.shape),
                  _const_spec(w2k.shape), _const_spec(w2v.shape)],
        out_specs=[o_spec, o_spec],
        out_shape=[jax.ShapeDtypeStruct((b, N_CMP_PAD, NSA_KV_W), BF16)] * 2,
        compiler_params=_params(("parallel",)),
        name="compress",
    )(rk, rv, pk, pv, w1k, w1v, w2k, w2v)


def _compress_weights(pos, w1):
    half = CMP_LEN // 2
    p = pos.reshape(2, half, 1, HEAD_DIM)
    p = jnp.broadcast_to(p, (2, half, NSA_GROUPS, HEAD_DIM)).reshape(2, half * NSA_KV_W)
    w = w1.reshape(2, half, HEAD_DIM, CMP_HIDDEN)
    eye = jnp.eye(NSA_GROUPS, dtype=w1.dtype)
    wbd = jnp.einsum("ajdm,gk->ajgdkm", w, eye).reshape(2, half * NSA_KV_W, NSA_GROUPS * CMP_HIDDEN)
    return p.astype(F32), wbd.astype(BF16)


def _memkv_kernel(m_ref, g_ref, w_ref, k_ref, v_ref):
    h = _rms(m_ref[...], g_ref[...]).astype(BF16)
    k_ref[...] = _dot(h, w_ref[:, :MEM_W]).astype(BF16)
    v_ref[...] = _dot(h, w_ref[:, MEM_W:]).astype(BF16)


def _memkv(mem2, g, w, tm=256):
    m = mem2.shape[0]
    o_spec = pl.BlockSpec((tm, MEM_W), lambda i: (i, 0))
    return pl.pallas_call(
        _memkv_kernel,
        grid=(m // tm,),
        in_specs=[pl.BlockSpec((tm, D_MODEL), lambda i: (i, 0)), _const_spec((1, D_MODEL)),
                  _const_spec((D_MODEL, 2 * MEM_W))],
        out_specs=[o_spec, o_spec],
        out_shape=[jax.ShapeDtypeStruct((m, MEM_W), BF16)] * 2,
        compiler_params=_params(("parallel",)),
        name="memkv",
    )(mem2, g, w)


def _expand_kernel(idx_ref, bias_ref, o_ref, *, head0, n_heads, heads_per_group):
    rows, cols = idx_ref.shape

    def body(i, carry):
        r = pl.multiple_of(i * SUBLANES, SUBLANES)
        for c0 in range(0, cols, TILE):
            idx = idx_ref[pl.ds(r, SUBLANES), c0:c0 + TILE]
            out = [jnp.full(idx.shape, NEG_INF, F32)] * n_heads
            for bkt in range(REL_BUCKETS):
                hit = idx == bkt
                out = [jnp.where(hit, bias_ref[bkt, head0 + h], out[h]) for h in range(n_heads)]
            for h in range(n_heads):
                col = (h % heads_per_group) * cols + c0
                o_ref[h // heads_per_group, pl.ds(r, SUBLANES), col:col + TILE] = out[h] * LOG2E
        return carry

    lax.fori_loop(0, rows // SUBLANES, body, 0)


def _expand(idx, rel_bias, head0, n_heads, heads_per_group=1):
    rows, cols = idx.shape
    return pl.pallas_call(
        functools.partial(_expand_kernel, head0=head0, n_heads=n_heads, heads_per_group=heads_per_group),
        in_specs=[pl.BlockSpec(memory_space=pltpu.VMEM), pl.BlockSpec(memory_space=pltpu.SMEM)],
        out_specs=pl.BlockSpec(memory_space=pltpu.VMEM),
        out_shape=jax.ShapeDtypeStruct((n_heads // heads_per_group, rows, heads_per_group * cols), F32),
        compiler_params=pltpu.CompilerParams(vmem_limit_bytes=VMEM_LIMIT),
        name="bias_expand",
    )(idx, rel_bias)


def _t5_bucket_np(dist):
    dist = np.maximum(dist, 0)
    max_exact = REL_BUCKETS // 2
    logd = np.log(np.maximum(dist, 1).astype(np.float32) / max_exact) / math.log(REL_MAX_DIST / max_exact)
    large = np.minimum(max_exact + (logd * (REL_BUCKETS - max_exact)).astype(np.int32), REL_BUCKETS - 1)
    return np.where(dist < max_exact, dist, large).astype(np.int32)


def _bucket_tables(s):
    j = np.arange(TILE)[:, None]
    i = np.arange(TILE)[None, :]
    assert TILE + 1 >= REL_MAX_DIST
    tiles = []
    for d in range(2):
        dist = d * TILE + i - j
        tiles.append(np.where(dist >= 0, _t5_bucket_np(dist), MASKED_BUCKET))
    dist1 = TILE + i - j
    win = np.where(dist1 < WINDOW, _t5_bucket_np(dist1), MASKED_BUCKET)
    n_cmp = (s - CMP_LEN) // CMP_STRIDE + 1
    c = np.arange(N_CMP_PAD)[:, None]
    dist_c = np.arange(s)[None, :] - (c * CMP_STRIDE + CMP_LEN - 1)
    cmp_idx = np.where((dist_c >= 0) & (c < n_cmp), _t5_bucket_np(dist_c), MASKED_BUCKET)
    as_i32 = lambda a: jnp.asarray(a.astype(np.int32))
    return as_i32(np.concatenate(tiles, axis=0)), as_i32(win), as_i32(cmp_idx)


def _overlap_table(s):
    n_cmp = (s - CMP_LEN) // CMP_STRIDE + 1
    n_sel = s // SEL_BLOCK
    cs = np.arange(n_cmp) * CMP_STRIDE
    ss = np.arange(n_sel) * SEL_BLOCK
    ov = np.clip(np.minimum(cs[:, None] + CMP_LEN, ss[None, :] + SEL_BLOCK)
                 - np.maximum(cs[:, None], ss[None, :]), 0, None).astype(np.float32) / CMP_LEN
    ovt = np.zeros((n_sel, N_CMP_PAD), np.float32)
    ovt[:, :n_cmp] = ov.T
    return jnp.asarray(ovt, BF16)


def _store_v_aug(vt_sc, idx, vt):
    ones = jnp.ones((BF16_ROWS, vt.shape[1]), BF16)
    vt_sc[idx] = jnp.concatenate([vt.astype(BF16), ones], axis=0)


def _lane_cat(xs):
    return jnp.concatenate(xs, axis=1)


def _flash_pipelined(own, streams, s_sc, m_ref, acc_ref):
    has_prev = jnp.where(own > 0, 1.0, 0.0).astype(F32)
    prev = jnp.maximum(own - 1, 0)
    n_far = jnp.maximum(own - 1, 0)

    def absorb(g, s, kt, c_row, w_row):
        u = jnp.max(s, axis=0, keepdims=True) + c_row
        m_old = m_ref[g]
        m_new = jnp.maximum(m_old, jnp.where(w_row > 0.0, u, NEG_INF))
        alpha = jnp.exp2(m_old - m_new)
        shift = jnp.maximum(m_new, u) - c_row
        p = jnp.exp2(s - shift).astype(BF16)
        acc_ref[g] = alpha * acc_ref[g] + w_row * streams[g]["pv"](kt)(p)
        m_ref[g] = m_new

    def absorb_slot(g, i):
        is_prev = i == 0
        kt = jnp.where(is_prev, prev, i - 1)
        c_row = jnp.where(is_prev, 0.0, streams[g]["c_far"])
        w_row = streams[g]["w"](kt) * jnp.where(is_prev, has_prev, 1.0)
        absorb(g, s_sc[g], kt, c_row, w_row)

    m_ref[...] = jnp.full(m_ref.shape, NEG_INF, F32)
    acc_ref[...] = jnp.zeros(acc_ref.shape, F32)
    s_own = [stream["own"]() for stream in streams]
    for g, stream in enumerate(streams):
        s_sc[g] = stream["prev"]()
    zero_row = jnp.zeros((1, s_own[0].shape[1]), F32)
    for g in range(len(streams)):
        absorb(g, s_own[g], own, zero_row, zero_row + 1.0)

    def body(i, carry):
        nxt = [stream["far"](i) for stream in streams]
        for g in range(len(streams)):
            absorb_slot(g, i)
        for g in range(len(streams)):
            s_sc[g] = nxt[g]
        return carry

    lax.fori_loop(0, n_far, body, 0)
    for g in range(len(streams)):
        absorb_slot(g, n_far)


def _softmax_av(s_list, pv_list):
    m = s_list[0].max(axis=0, keepdims=True)
    for s in s_list[1:]:
        m = jnp.maximum(m, s.max(axis=0, keepdims=True))
    acc = None
    for s, pv in zip(s_list, pv_list):
        part = pv(jnp.exp2(s - m).astype(BF16))
        acc = part if acc is None else acc + part
    return acc


def _normalize(acc):
    return acc[:HEAD_DIM] / acc[HEAD_DIM:HEAD_DIM + 1]


def _rank_before(score, n_cand):
    blk = lax.broadcasted_iota(jnp.int32, score.shape, 0)
    rank = jnp.zeros(score.shape, F32)
    for m in range(n_cand):
        row = score[m:m + 1, :]
        tie = jnp.where(blk > m, 1.0, 0.0)
        rank = rank + jnp.where(row > score, 1.0, 0.0) + jnp.where(row == score, tie, 0.0)
    return rank


def _nsa_kernel(q_ref, gn_ref, kc_ref, vct_ref, ks_ref, vs_ref, kw_ref, vw_ref,
                bct_ref, tt_ref, twt_ref, cfar_ref, ovt_ref, o_ref,
                vst_sc, vwt_sc, qa_sc, og_sc, s_sc, m_sc, acc_sc, ot_sc):
    qi = pl.program_id(1)
    nt = ks_ref.shape[1]
    n_sel = ovt_ref.shape[0]

    @pl.when(qi == 0)
    def _():
        for kt in range(nt):
            vs_t = vs_ref[0, kt].astype(F32).T
            vw_t = vw_ref[0, kt].astype(F32).T
            for g in range(NSA_GROUPS):
                _store_v_aug(vst_sc, (kt, g), vs_t[g * HEAD_DIM:(g + 1) * HEAD_DIM])
                _store_v_aug(vwt_sc, (kt, g), vw_t[g * HEAD_DIM:(g + 1) * HEAD_DIM])

    pos = lax.broadcasted_iota(jnp.int32, (1, TILE), 1) + qi * TILE
    cur = pos // SEL_BLOCK
    has_cmp = pos >= CMP_LEN - 1
    blk = lax.broadcasted_iota(jnp.int32, (n_sel, TILE), 0)
    prev = jnp.maximum(qi - 1, 0)
    no_prev = jnp.where(qi == 0, NEG_INF, 0.0).astype(F32)
    gates = gn_ref[...].T

    gsls = [slice(g * HEAD_DIM, (g + 1) * HEAD_DIM) for g in range(NSA_GROUPS)]
    group_heads = [[g * NSA_HPG + j for j in range(NSA_HPG)] for g in range(NSA_GROUPS)]

    def gate(g, branch):
        return _lane_cat([gates[3 * h + branch:3 * h + branch + 1, :] for h in group_heads[g]])

    q_t = q_ref[...].astype(F32).T
    for g in range(NSA_GROUPS):
        heads = group_heads[g]
        q4 = _lane_cat([q_t[h * HEAD_DIM:(h + 1) * HEAD_DIM, :] for h in heads]).astype(BF16)
        qa_sc[g, 0:HEAD_DIM, :] = q4
        qa_sc[g, HEAD_DIM + n_sel:, :] = jnp.zeros((HEAD_DIM - n_sel, NSA_HPG * TILE), BF16)

        s0 = _dot(kw_ref[0, prev, :, gsls[g]], q4) + (twt_ref[g] + no_prev)
        s1 = _dot(kw_ref[0, qi, :, gsls[g]], q4) + tt_ref[g, 0]
        acc_w = _softmax_av([s0, s1], [lambda pr: _dot(vwt_sc[prev, g], pr), lambda pr: _dot(vwt_sc[qi, g], pr)])
        o_win = gate(g, 2) * _normalize(acc_w)

        kc = kc_ref[0, :, gsls[g]]
        vct = vct_ref[0, gsls[g], :]
        q_cols = pl.ds(pl.multiple_of(qi * TILE, TILE), TILE)
        s = _dot(kc, q4) + _lane_cat([bct_ref[h, :, q_cols] for h in heads])
        e = jnp.exp2(s - jnp.max(s, axis=0, keepdims=True))
        p = jnp.where(_lane_cat([has_cmp] * NSA_HPG), e / jnp.sum(e, axis=0, keepdims=True), 0.0)
        psum = p[:, :TILE]
        for j in range(1, NSA_HPG):
            psum = psum + p[:, j * TILE:(j + 1) * TILE]
        og_sc[g] = gate(g, 0) * _dot(vct, p.astype(BF16)) + o_win

        p_hi, p_lo = _split_bf16(psum)
        imp = _dot(ovt_ref[...], p_hi) + _dot(ovt_ref[...], p_lo)
        forced = (blk == 0) | (blk == cur) | (blk == cur - 1)
        score = jnp.where(forced, FORCE_SCORE, jnp.where(blk <= cur, imp, NEG_INF))
        rank = _rank_before(score, n_sel)
        sel = jnp.where(rank < SEL_TOPN, jnp.where(score > NEG_INF / 2, 0.0, NEG_INF), NEG_INF)
        qa_sc[g, HEAD_DIM:HEAD_DIM + n_sel, :] = _lane_cat([sel.astype(BF16)] * NSA_HPG)

    ones_row = jnp.ones((1, NSA_HPG * TILE), F32)

    def sel_stream(g):
        def qk(kt):
            return _dot(ks_ref[0, kt, :, g * 2 * HEAD_DIM:(g + 1) * 2 * HEAD_DIM], qa_sc[g])

        return dict(own=lambda: qk(qi) + tt_ref[g, 0], prev=lambda: qk(prev) + tt_ref[g, 1], far=qk,
                    c_far=cfar_ref[g], w=lambda kt: ones_row,
                    pv=lambda kt: (lambda pr: _dot(vst_sc[kt, g], pr)))

    _flash_pipelined(qi, [sel_stream(g) for g in range(NSA_GROUPS)], s_sc, m_sc, acc_sc)

    for g in range(NSA_GROUPS):
        o = og_sc[g] + gate(g, 1) * _normalize(acc_sc[g])
        for j, h in enumerate(group_heads[g]):
            ot_sc[h * HEAD_DIM:(h + 1) * HEAD_DIM, :] = o[:, j * TILE:(j + 1) * TILE]

    o_ref[...] = ot_sc[...].T.astype(BF16)


def _nsa(b, s, qn, gn, kc, vct, ks, vs, kw, vw, bias_cmp, t_nsa, t_win, c_far, ovt):
    nt = s // TILE
    kv_spec = pl.BlockSpec((1, nt, TILE, NSA_KV_W), lambda i, j: (i, 0, 0, 0))
    ks_spec = pl.BlockSpec((1, nt, TILE, KS_AUG_W), lambda i, j: (i, 0, 0, 0))
    c_spec = pl.BlockSpec((1, N_CMP_PAD, NSA_KV_W), lambda i, j: (i, 0, 0))
    tile4 = lambda a: a.reshape(b, nt, TILE, a.shape[-1])
    return pl.pallas_call(
        _nsa_kernel,
        grid=(b, nt),
        in_specs=[pl.BlockSpec((TILE, NSA_Q_W), lambda i, j: (i * nt + j, 0)),
                  pl.BlockSpec((TILE, GATE_PAD), lambda i, j: (i * nt + j, 0)),
                  c_spec, c_spec, ks_spec, kv_spec, kv_spec, kv_spec,
                  _const_spec(bias_cmp.shape),
                  _const_spec(t_nsa.shape), _const_spec(t_win.shape), _const_spec(c_far.shape),
                  _const_spec(ovt.shape)],
        out_specs=pl.BlockSpec((TILE, NSA_Q_W), lambda i, j: (i * nt + j, 0)),
        out_shape=jax.ShapeDtypeStruct((b * s, NSA_Q_W), BF16),
        scratch_shapes=[pltpu.VMEM((nt, NSA_GROUPS, V_AUG, TILE), BF16),
                        pltpu.VMEM((nt, NSA_GROUPS, V_AUG, TILE), BF16),
                        pltpu.VMEM((NSA_GROUPS, 2 * HEAD_DIM, NSA_HPG * TILE), BF16),
                        pltpu.VMEM((NSA_GROUPS, HEAD_DIM, NSA_HPG * TILE), F32),
                        pltpu.VMEM((NSA_GROUPS, TILE, NSA_HPG * TILE), F32),
                        pltpu.VMEM((NSA_GROUPS, 1, NSA_HPG * TILE), F32),
                        pltpu.VMEM((NSA_GROUPS, V_AUG, NSA_HPG * TILE), F32),
                        pltpu.VMEM((NSA_Q_W, TILE), F32)],
        compiler_params=_params(("arbitrary", "arbitrary")),
        name="nsa",
    )(qn, gn, kc, vct, tile4(ks), tile4(vs), tile4(kw), tile4(vw), bias_cmp, t_nsa, t_win, c_far, ovt)


def _moba_kernel(qm_ref, km_ref, vm_ref, qx_ref, mk_ref, mv_ref, tt_ref, cfar_ref, om_ref, ox_ref,
                 vmt_sc, mvt_sc, kmean_sc, qbd_sc, sel_sc, s_sc, m_sc, acc_sc, ot_sc):
    c = pl.program_id(1)
    nt = km_ref.shape[1]
    hsls = [slice(h * HEAD_DIM, (h + 1) * HEAD_DIM) for h in range(MOBA_HEADS)]

    @pl.when(c == 0)
    def _():
        kmean_sc[...] = jnp.zeros(kmean_sc.shape, F32)
        for n in range(nt):
            kmean_sc[n:n + 1, :] = jnp.sum(km_ref[0, n].astype(F32), axis=0, keepdims=True) * (1.0 / MOBA_BLOCK)
            vt = vm_ref[0, n].astype(F32).T
            for h in range(MOBA_HEADS):
                _store_v_aug(vmt_sc, (n, h), vt[hsls[h]])
        mvt = mv_ref[0].astype(F32).T
        for h in range(MOBA_HEADS):
            _store_v_aug(mvt_sc, h, mvt[hsls[h]])

    row_head = lax.broadcasted_iota(jnp.int32, (MOBA_W, TILE), 0) // HEAD_DIM

    def block_diag(q_ref):
        q_t = q_ref[...].astype(F32).T
        return _lane_cat([jnp.where(row_head == h, q_t, 0.0) for h in range(MOBA_HEADS)]).astype(BF16)

    def per_head_pv(vts):
        return lambda pr: _lane_cat([_dot(vts(h), pr[:, h * TILE:(h + 1) * TILE]) for h in range(MOBA_HEADS)])

    def store_heads(o_t, out_ref):
        for h in range(MOBA_HEADS):
            ot_sc[hsls[h], :] = o_t[:, h * TILE:(h + 1) * TILE]
        out_ref[...] = ot_sc[...].T.astype(BF16)

    qbd = block_diag(qm_ref)
    km_hi, km_lo = _split_bf16(kmean_sc[...])
    gate = _dot(km_hi, qbd) + _dot(km_lo, qbd)
    blk = lax.broadcasted_iota(jnp.int32, gate.shape, 0)
    score = jnp.where(blk < c, gate, NEG_INF * Q_SCALE)
    rank = _rank_before(score, nt)
    sel_sc[...] = jnp.where(rank < MOBA_TOPK, jnp.where(score > NEG_INF * Q_SCALE / 2, 1.0, 0.0), 0.0)

    qbd_sc[...] = qbd
    qk = lambda n: _dot(km_ref[0, n], qbd_sc[...])
    stream = dict(own=lambda: qk(c) + tt_ref[0], prev=lambda: qk(jnp.maximum(c - 1, 0)) + tt_ref[1], far=qk,
                  c_far=cfar_ref[...], w=lambda n: sel_sc[pl.ds(n, 1), :],
                  pv=lambda n: per_head_pv(lambda h: vmt_sc[n, h]))
    s = _dot(mk_ref[0], block_diag(qx_ref))
    store_heads(_normalize(_softmax_av([s], [per_head_pv(lambda h: mvt_sc[h])])), ox_ref)

    _flash_pipelined(c, [stream], s_sc, m_sc, acc_sc)
    store_heads(_normalize(acc_sc[0]), om_ref)


def _moba(b, s, qm, km, vm, qx, mk, mv, t_moba, c_far):
    nt = s // TILE
    mem_len = mk.shape[0] // b
    assert MOBA_TOPK <= nt - 1 and nt <= BF16_ROWS
    q_spec = pl.BlockSpec((TILE, MOBA_W), lambda i, j: (i * nt + j, 0))
    kv_spec = pl.BlockSpec((1, nt, TILE, MOBA_W), lambda i, j: (i, 0, 0, 0))
    mem_spec = pl.BlockSpec((1, mem_len, MEM_W), lambda i, j: (i, 0, 0))
    return pl.pallas_call(
        _moba_kernel,
        grid=(b, nt),
        in_specs=[q_spec, kv_spec, kv_spec, q_spec, mem_spec, mem_spec, _const_spec(t_moba.shape),
                  _const_spec(c_far.shape)],
        out_specs=[q_spec, q_spec],
        out_shape=[jax.ShapeDtypeStruct((b * s, MOBA_W), BF16), jax.ShapeDtypeStruct((b * s, MEM_W), BF16)],
        scratch_shapes=[pltpu.VMEM((nt, MOBA_HEADS, V_AUG, TILE), BF16),
                        pltpu.VMEM((MEM_HEADS, V_AUG, mem_len), BF16),
                        pltpu.VMEM((BF16_ROWS, MOBA_W), F32),
                        pltpu.VMEM((MOBA_W, MOBA_HEADS * TILE), BF16),
                        pltpu.VMEM((BF16_ROWS, MOBA_HEADS * TILE), F32),
                        pltpu.VMEM((1, TILE, MOBA_HEADS * TILE), F32),
                        pltpu.VMEM((1, 1, MOBA_HEADS * TILE), F32),
                        pltpu.VMEM((1, V_AUG, MOBA_HEADS * TILE), F32),
                        pltpu.VMEM((MOBA_W, TILE), F32)],
        compiler_params=_params(("arbitrary", "arbitrary")),
        name="moba",
    )(qm, km.reshape(b, nt, TILE, MOBA_W), vm.reshape(b, nt, TILE, MOBA_W), qx,
      mk.reshape(b, mem_len, MEM_W), mv.reshape(b, mem_len, MEM_W), t_moba, c_far)


def _mix_kernel(x_ref, on_ref, om_ref, ox_ref, g_pre_ref, g_post_ref, wg_ref, wn_ref, wm_ref, wx_ref,
                wo_ref, o_ref):
    x = x_ref[...]
    h = _rms(x, g_pre_ref[...]).astype(BF16)
    merged = jax.nn.sigmoid(_dot(h, wg_ref[:, :D_MODEL])) * _dot(on_ref[...], wn_ref[...])
    merged = merged + jax.nn.sigmoid(_dot(h, wg_ref[:, D_MODEL:2 * D_MODEL])) * _dot(om_ref[...], wm_ref[...])
    merged = merged + jax.nn.sigmoid(_dot(h, wg_ref[:, 2 * D_MODEL:])) * _dot(ox_ref[...], wx_ref[...])
    y = _dot(merged.astype(BF16), wo_ref[...])
    o_ref[...] = x + _rms(y, g_post_ref[...])


def _mix(x2, o_nsa, o_moba, o_mem, g_pre, g_post, w_gates, w_nsa_o, w_moba_o, w_mem_o, w_mix_out, tm=512):
    m = x2.shape[0]
    row = lambda w: pl.BlockSpec((tm, w), lambda i: (i, 0))
    return pl.pallas_call(
        _mix_kernel,
        grid=(m // tm,),
        in_specs=[row(D_MODEL), row(NSA_Q_W), row(MOBA_W), row(MEM_W),
                  _const_spec((1, D_MODEL)), _const_spec((1, D_MODEL)),
                  _const_spec(w_gates.shape), _const_spec(w_nsa_o.shape), _const_spec(w_moba_o.shape),
                  _const_spec(w_mem_o.shape), _const_spec(w_mix_out.shape)],
        out_specs=row(D_MODEL),
        out_shape=jax.ShapeDtypeStruct((m, D_MODEL), F32),
        compiler_params=_params(("parallel",)),
        name="mix",
    )(x2, o_nsa, o_moba, o_mem, g_pre, g_post, w_gates, w_nsa_o, w_moba_o, w_mem_o, w_mix_out)


FFN_CHUNK = 256


def _ffn_kernel(x_ref, g_pre_ref, g_post_ref, wg_ref, wu_ref, wd_ref, o_ref, a_sc):
    x = x_ref[...]
    h = _rms(x, g_pre_ref[...]).astype(BF16)
    d_ff = wg_ref.shape[1]
    for j in range(d_ff // FFN_CHUNK):
        sl = slice(j * FFN_CHUNK, (j + 1) * FFN_CHUNK)
        a_sc[:, sl] = (jax.nn.silu(_dot(h, wg_ref[:, sl])) * _dot(h, wu_ref[:, sl])).astype(BF16)
    f = _dot(a_sc[...], wd_ref[...])
    o_ref[...] = x + _rms(f, g_post_ref[...])


def _ffn(x2, g_pre, g_post, wg, wu, wd, tm=512):
    m = x2.shape[0]
    d_ff = wg.shape[1]
    return pl.pallas_call(
        _ffn_kernel,
        grid=(m // tm,),
        in_specs=[pl.BlockSpec((tm, D_MODEL), lambda i: (i, 0)),
                  _const_spec((1, D_MODEL)), _const_spec((1, D_MODEL)),
                  _const_spec(wg.shape), _const_spec(wu.shape), _const_spec(wd.shape)],
        out_specs=pl.BlockSpec((tm, D_MODEL), lambda i: (i, 0)),
        out_shape=jax.ShapeDtypeStruct((m, D_MODEL), F32),
        scratch_shapes=[pltpu.VMEM((tm, d_ff), BF16)],
        compiler_params=_params(("parallel",)),
        name="ffn",
    )(x2, g_pre, g_post, wg, wu, wd)


def kernel(x, mem, rel_bias, pre_mix_g, mem_norm_g, post_mix_g, w_in, cmp_pos_k, cmp_w1_k, cmp_w2_k, cmp_pos_v, cmp_w1_v, cmp_w2_v, w_mem_kv, w_nsa_o, w_moba_o, w_mem_o, w_mix_out, pre_ffn_g, post_ffn_g, w_ffn_gate, w_ffn_up, w_ffn_down):
    b, s, d_model = x.shape
    depth = w_in.shape[0]
    assert d_model == D_MODEL and s % TILE == 0 and TILE == MOBA_BLOCK == WINDOW
    assert (s - CMP_LEN) // CMP_STRIDE + 1 < N_CMP_PAD and (s // SEL_BLOCK) % SUBLANES == 0 and s // SEL_BLOCK <= HEAD_DIM
    assert w_in.shape[2] == ATT_W + 3 * D_MODEL and rel_bias.shape == (REL_BUCKETS, N_BIAS_HEADS)

    tile_idx, win_idx, cmp_idx = _bucket_tables(s)
    rel_bias = rel_bias.astype(F32)
    t_nsa = _expand(tile_idx, rel_bias, 0, NSA_HEADS, NSA_HPG).reshape(NSA_GROUPS, 2, TILE, NSA_HPG * TILE)
    t_moba = _expand(tile_idx, rel_bias, NSA_HEADS, MOBA_HEADS, MOBA_HEADS).reshape(2, TILE, MOBA_HEADS * TILE)
    t_win = _expand(win_idx, rel_bias, 0, NSA_HEADS, NSA_HPG)
    b_cmp = _expand(cmp_idx, rel_bias, 0, NSA_HEADS)
    c_far = jnp.repeat(rel_bias[REL_BUCKETS - 1] * LOG2E, TILE)
    c_far_nsa = c_far[:NSA_HEADS * TILE].reshape(NSA_GROUPS, 1, NSA_HPG * TILE)
    c_far_moba = c_far[NSA_HEADS * TILE:].reshape(1, MOBA_HEADS * TILE)
    ovt = _overlap_table(s)
    sel_cols = np.zeros((s, HEAD_DIM), np.float32)
    sel_cols[np.arange(s), np.arange(s) // SEL_BLOCK] = 1.0
    sel_cols = jnp.asarray(sel_cols, BF16)
    gate_lo = NSA_Q_W + 6 * NSA_KV_W
    rows_per_chunk = CMP_STRIDE * NSA_KV_W

    x2 = x.reshape(b * s, D_MODEL)
    mem2 = mem.reshape(-1, D_MODEL)
    for l in range(depth):
        w_att = jnp.concatenate(
            [w_in[l, :, :gate_lo + NSA_GATE_W],
             jnp.zeros((D_MODEL, GATE_PAD - NSA_GATE_W), w_in.dtype),
             w_in[l, :, gate_lo + NSA_GATE_W:ATT_W]], axis=1).astype(BF16)
        w_gates = w_in[l, :, ATT_W:].astype(BF16)
        row = lambda v: v[l].reshape(1, D_MODEL)

        qn, kc_raw, vc_raw, ks, vs, kw, vw, gn, qm, km, vm, qx = _inproj(x2, row(pre_mix_g), w_att, sel_cols)

        pk, w1k = _compress_weights(cmp_pos_k[l], cmp_w1_k[l])
        pv, w1v = _compress_weights(cmp_pos_v[l], cmp_w1_v[l])
        kc, vct = _compress(kc_raw.reshape(b, s // CMP_STRIDE, rows_per_chunk),
                            vc_raw.reshape(b, s // CMP_STRIDE, rows_per_chunk),
                            pk, pv, w1k, w1v, cmp_w2_k[l].astype(BF16), cmp_w2_v[l].astype(BF16))

        mk, mv = _memkv(mem2, row(mem_norm_g), w_mem_kv[l].astype(BF16))

        o_nsa = _nsa(b, s, qn, gn, kc, vct, ks, vs, kw, vw, b_cmp, t_nsa, t_win, c_far_nsa, ovt)
        o_moba, o_mem = _moba(b, s, qm, km, vm, qx, mk, mv, t_moba, c_far_moba)

        x2 = _mix(x2, o_nsa, o_moba, o_mem, row(pre_mix_g), row(post_mix_g), w_gates,
                  w_nsa_o[l].astype(BF16), w_moba_o[l].astype(BF16), w_mem_o[l].astype(BF16),
                  w_mix_out[l].astype(BF16))
        x2 = _ffn(x2, row(pre_ffn_g), row(post_ffn_g), w_ffn_gate[l].astype(BF16),
                  w_ffn_up[l].astype(BF16), w_ffn_down[l].astype(BF16))
    return x2.reshape(b, s, D_MODEL)
```

```python
import functools
import math

import numpy as np
import jax
import jax.numpy as jnp
from jax import lax
from jax.experimental import pallas as pl
from jax.experimental.pallas import tpu as pltpu

F32 = jnp.float32
BF16 = jnp.bfloat16

D_MODEL = 1024
HEAD_DIM = 64
SCALE = HEAD_DIM ** -0.5
LOG2E = math.log2(math.e)
Q_SCALE = SCALE * LOG2E
NSA_HEADS = 8
NSA_GROUPS = 2
NSA_HPG = NSA_HEADS // NSA_GROUPS
CMP_LEN = 32
CMP_STRIDE = 16
CMP_HIDDEN = 128
SEL_BLOCK = 64
SEL_TOPN = 8
WINDOW = 256
MOBA_HEADS = 4
MOBA_BLOCK = 256
MOBA_TOPK = 3
MEM_HEADS = 4
REL_BUCKETS = 32
REL_MAX_DIST = 128
N_BIAS_HEADS = NSA_HEADS + MOBA_HEADS
RMS_EPS = 1e-6
NEG_INF = -1e30
FORCE_SCORE = 1e4

NSA_Q_W = NSA_HEADS * HEAD_DIM
NSA_KV_W = NSA_GROUPS * HEAD_DIM
NSA_GATE_W = NSA_HEADS * 3
MOBA_W = MOBA_HEADS * HEAD_DIM
MEM_W = MEM_HEADS * HEAD_DIM
ATT_W = NSA_Q_W + 6 * NSA_KV_W + NSA_GATE_W + 3 * MOBA_W + MEM_W
LANES = 128
SUBLANES = 8
BF16_ROWS = 16
MXU_COLS = 256
GATE_PAD = LANES
TILE = 256
HALF = TILE // 2
N_CMP_PAD = 128
V_AUG = HEAD_DIM + BF16_ROWS
MASKED_BUCKET = REL_BUCKETS
VMEM_LIMIT = 56 * 1024 * 1024


def _dot(a, b):
    return jnp.dot(a, b, preferred_element_type=F32)


def _split_bf16(x):
    hi = x.astype(BF16)
    lo = (x - hi.astype(F32)).astype(BF16)
    return hi, lo


def _rms(x, g):
    return x * lax.rsqrt(jnp.mean(x * x, axis=-1, keepdims=True) + RMS_EPS) * g


def _params(sem):
    return pltpu.CompilerParams(dimension_semantics=sem, vmem_limit_bytes=VMEM_LIMIT)


def _const_spec(shape):
    nd = len(shape)
    return pl.BlockSpec(shape, lambda *_: (0,) * nd, pipeline_mode=pl.Buffered(1))


_INPROJ_OUTS = (
    ("qn", NSA_Q_W, BF16, True),
    ("kc", NSA_KV_W, F32, False), ("vc", NSA_KV_W, F32, False),
    ("ks", NSA_KV_W, BF16, False), ("vs", NSA_KV_W, BF16, False),
    ("kw", NSA_KV_W, BF16, False), ("vw", NSA_KV_W, BF16, False),
    ("gn", GATE_PAD, F32, False),
    ("qm", MOBA_W, BF16, True), ("km", MOBA_W, BF16, False), ("vm", MOBA_W, BF16, False),
    ("qx", MEM_W, BF16, True),
)
_INPROJ_W = sum(o[1] for o in _INPROJ_OUTS)
_INPROJ_CHUNKED = ("kc", "vc")
KS_AUG_W = NSA_GROUPS * 2 * HEAD_DIM


def _inproj_out_width(name, width):
    return KS_AUG_W if name == "ks" else width


def _inproj_kernel(x_ref, g_ref, w_ref, e_ref, *refs):
    out_refs, rows_sc = refs[:-1], refs[-1]
    h = _rms(x_ref[...], g_ref[...]).astype(BF16)
    runs, lo = [], 0
    for out in zip(_INPROJ_OUTS, out_refs):
        if runs and runs[-1][1] < MXU_COLS:
            runs[-1][0].append(out)
            runs[-1][1] += out[0][1]
        else:
            runs.append([[out], out[0][1], lo])
        lo += out[0][1]
    for outs, run_width, run_lo in runs:
        y_run = _dot(h, w_ref[:, run_lo:run_lo + run_width])
        lo = 0
        for (name, width, dtype, scaled), o_ref in outs:
            y = y_run[:, lo:lo + width]
            if scaled:
                y = y * Q_SCALE
            if name == "gn":
                y = jax.nn.sigmoid(y)
            y = y.astype(dtype)
            if name == "ks":
                e = e_ref[...]
                y = _lane_cat([y[:, :HEAD_DIM], e, y[:, HEAD_DIM:], e])
            if name in _INPROJ_CHUNKED:
                rows_sc[...] = y
                for j in range(CMP_STRIDE):
                    o_ref[:, j * width:(j + 1) * width] = rows_sc[pl.ds(j, o_ref.shape[0], stride=CMP_STRIDE), :]
            else:
                o_ref[...] = y
            lo += width


def _inproj(x2, g, w, e_cols, tm=512):
    m = x2.shape[0]
    tiles_per_seq = e_cols.shape[0] // tm
    outs = []
    for name, width, dtype, _ in _INPROJ_OUTS:
        if name in _INPROJ_CHUNKED:
            outs.append((CMP_STRIDE, CMP_STRIDE * width, dtype))
        else:
            outs.append((1, _inproj_out_width(name, width), dtype))
    return pl.pallas_call(
        _inproj_kernel,
        grid=(m // tm,),
        in_specs=[pl.BlockSpec((tm, D_MODEL), lambda i: (i, 0)),
                  _const_spec((1, D_MODEL)),
                  _const_spec((D_MODEL, _INPROJ_W)),
                  pl.BlockSpec((tm, HEAD_DIM), lambda i: (i % tiles_per_seq, 0))],
        out_specs=[pl.BlockSpec((tm // r, w), lambda i: (i, 0)) for r, w, _ in outs],
        out_shape=[jax.ShapeDtypeStruct((m // r, w), d) for r, w, d in outs],
        scratch_shapes=[pltpu.VMEM((tm, NSA_KV_W), F32)],
        compiler_params=_params(("parallel",)),
        name="inproj",
    )(x2, g, w, e_cols)


def _compress_kernel(rk_ref, rv_ref, pk_ref, pv_ref, w1k_ref, w1v_ref, w2k_ref, w2v_ref, kc_ref, vc_ref):
    def one(r_ref, p_ref, w1_ref, w2_ref):
        r = r_ref[0]
        top = _dot((r + p_ref[0:1, :]).astype(BF16), w1_ref[0])
        bot = _dot((r + p_ref[1:2, :]).astype(BF16), w1_ref[1])
        hid = top + pltpu.roll(bot, N_CMP_PAD - 1, 0)
        act = jax.nn.gelu(hid).astype(BF16)
        return jnp.concatenate(
            [_dot(act[:, g * CMP_HIDDEN:(g + 1) * CMP_HIDDEN], w2_ref[...]) for g in range(NSA_GROUPS)], axis=1)

    kc_ref[0] = one(rk_ref, pk_ref, w1k_ref, w2k_ref).astype(BF16)
    vc_ref[0] = one(rv_ref, pv_ref, w1v_ref, w2v_ref).T.astype(BF16)


def _compress(rk, rv, pk, pv, w1k, w1v, w2k, w2v):
    b = rk.shape[0]
    rw = rk.shape[2]
    r_spec = pl.BlockSpec((1, N_CMP_PAD, rw), lambda i: (i, 0, 0))
    o_spec = pl.BlockSpec((1, N_CMP_PAD, NSA_KV_W), lambda i: (i, 0, 0))
    return pl.pallas_call(
        _compress_kernel,
        grid=(b,),
        in_specs=[r_spec, r_spec, _const_spec(pk.shape), _const_spec(pv.shape),
                  _const_spec(w1k.shape), _const_spec(w1v.shape),
                  _const_spec(w2k.shape), _const_spec(w2v.shape)],
        out_specs=[o_spec, o_spec],
        out_shape=[jax.ShapeDtypeStruct((b, N_CMP_PAD, NSA_KV_W), BF16)] * 2,
        compiler_params=_params(("parallel",)),
        name="compress",
    )(rk, rv, pk, pv, w1k, w1v, w2k, w2v)


def _compress_weights(pos, w1):
    half = CMP_LEN // 2
    p = pos.reshape(2, half, 1, HEAD_DIM)
    p = jnp.broadcast_to(p, (2, half, NSA_GROUPS, HEAD_DIM)).reshape(2, half * NSA_KV_W)
    w = w1.reshape(2, half, HEAD_DIM, CMP_HIDDEN)
    eye = jnp.eye(NSA_GROUPS, dtype=w1.dtype)
    wbd = jnp.einsum("ajdm,gk->ajgdkm", w, eye).reshape(2, half * NSA_KV_W, NSA_GROUPS * CMP_HIDDEN)
    return p.astype(F32), wbd.astype(BF16)


def _memkv_kernel(m_ref, g_ref, w_ref, k_ref, v_ref):
    h = _rms(m_ref[...], g_ref[...]).astype(BF16)
    k_ref[...] = _dot(h, w_ref[:, :MEM_W]).astype(BF16)
    v_ref[...] = _dot(h, w_ref[:, MEM_W:]).astype(BF16)


def _memkv(mem2, g, w, tm=256):
    m = mem2.shape[0]
    o_spec = pl.BlockSpec((tm, MEM_W), lambda i: (i, 0))
    return pl.pallas_call(
        _memkv_kernel,
        grid=(m // tm,),
        in_specs=[pl.BlockSpec((tm, D_MODEL), lambda i: (i, 0)), _const_spec((1, D_MODEL)),
                  _const_spec((D_MODEL, 2 * MEM_W))],
        out_specs=[o_spec, o_spec],
        out_shape=[jax.ShapeDtypeStruct((m, MEM_W), BF16)] * 2,
        compiler_params=_params(("parallel",)),
        name="memkv",
    )(mem2, g, w)


def _expand_kernel(idx_ref, bias_ref, o_ref, *, head0, n_heads, heads_per_group):
    rows, cols = idx_ref.shape

    def body(i, carry):
        r = pl.multiple_of(i * SUBLANES, SUBLANES)
        for c0 in range(0, cols, TILE):
            idx = idx_ref[pl.ds(r, SUBLANES), c0:c0 + TILE]
            out = [jnp.full(idx.shape, NEG_INF, F32)] * n_heads
            for bkt in range(REL_BUCKETS):
                hit = idx == bkt
                out = [jnp.where(hit, bias_ref[bkt, head0 + h], out[h]) for h in range(n_heads)]
            for h in range(n_heads):
                col = (h % heads_per_group) * cols + c0
                o_ref[h // heads_per_group, pl.ds(r, SUBLANES), col:col + TILE] = out[h] * LOG2E
        return carry

    lax.fori_loop(0, rows // SUBLANES, body, 0)


def _expand(idx, rel_bias, head0, n_heads, heads_per_group=1):
    rows, cols = idx.shape
    return pl.pallas_call(
        functools.partial(_expand_kernel, head0=head0, n_heads=n_heads, heads_per_group=heads_per_group),
        in_specs=[pl.BlockSpec(memory_space=pltpu.VMEM), pl.BlockSpec(memory_space=pltpu.SMEM)],
        out_specs=pl.BlockSpec(memory_space=pltpu.VMEM),
        out_shape=jax.ShapeDtypeStruct((n_heads // heads_per_group, rows, heads_per_group * cols), F32),
        compiler_params=pltpu.CompilerParams(vmem_limit_bytes=VMEM_LIMIT),
        name="bias_expand",
    )(idx, rel_bias)


def _t5_bucket_np(dist):
    dist = np.maximum(dist, 0)
    max_exact = REL_BUCKETS // 2
    logd = np.log(np.maximum(dist, 1).astype(np.float32) / max_exact) / math.log(REL_MAX_DIST / max_exact)
    large = np.minimum(max_exact + (logd * (REL_BUCKETS - max_exact)).astype(np.int32), REL_BUCKETS - 1)
    return np.where(dist < max_exact, dist, large).astype(np.int32)


def _bucket_tables(s):
    j = np.arange(TILE)[:, None]
    i = np.arange(TILE)[None, :]
    assert TILE + 1 >= REL_MAX_DIST
    tiles = []
    for d in range(2):
        dist = d * TILE + i - j
        tiles.append(np.where(dist >= 0, _t5_bucket_np(dist), MASKED_BUCKET))
    dist1 = TILE + i - j
    win = np.where(dist1 < WINDOW, _t5_bucket_np(dist1), MASKED_BUCKET)
    n_cmp = (s - CMP_LEN) // CMP_STRIDE + 1
    c = np.arange(N_CMP_PAD)[:, None]
    dist_c = np.arange(s)[None, :] - (c * CMP_STRIDE + CMP_LEN - 1)
    cmp_idx = np.where((dist_c >= 0) & (c < n_cmp), _t5_bucket_np(dist_c), MASKED_BUCKET)
    as_i32 = lambda a: jnp.asarray(a.astype(np.int32))
    return as_i32(np.concatenate(tiles, axis=0)), as_i32(win), as_i32(cmp_idx)


def _overlap_table(s):
    n_cmp = (s - CMP_LEN) // CMP_STRIDE + 1
    n_sel = s // SEL_BLOCK
    cs = np.arange(n_cmp) * CMP_STRIDE
    ss = np.arange(n_sel) * SEL_BLOCK
    ov = np.clip(np.minimum(cs[:, None] + CMP_LEN, ss[None, :] + SEL_BLOCK)
                 - np.maximum(cs[:, None], ss[None, :]), 0, None).astype(np.float32) / CMP_LEN
    ovt = np.zeros((n_sel, N_CMP_PAD), np.float32)
    ovt[:, :n_cmp] = ov.T
    return jnp.asarray(ovt, BF16)


def _store_v_aug(vt_sc, idx, vt):
    ones = jnp.ones((BF16_ROWS, vt.shape[1]), BF16)
    vt_sc[idx] = jnp.concatenate([vt.astype(BF16), ones], axis=0)


def _lane_cat(xs):
    return jnp.concatenate(xs, axis=1)


def _query_halves(x):
    n = x.shape[-1] // TILE
    first = _lane_cat([x[:, k * TILE:k * TILE + HALF] for k in range(n)])
    second = _lane_cat([x[:, k * TILE + HALF:(k + 1) * TILE] for k in range(n)])
    return first, second


def _join_query_halves(first, second):
    n = first.shape[-1] // HALF
    return _lane_cat([part for k in range(n)
                      for part in (first[:, k * HALF:(k + 1) * HALF], second[:, k * HALF:(k + 1) * HALF])])


def _triangle_tile(k, q, table, pv_lo, pv_hi, causal):
    q_first, q_second = _query_halves(q)
    lo, hi = (0, HALF), (HALF, TILE)
    if causal:
        s_wide = _dot(k(*lo), q) + table(*lo)
        s_narrow = _dot(k(*hi), q_second) + _query_halves(table(*hi))[1]
        pv_wide, pv_narrow = pv_lo, pv_hi
    else:
        s_wide = _dot(k(*hi), q) + table(*hi)
        s_narrow = _dot(k(*lo), q_first) + _query_halves(table(*lo))[0]
        pv_wide, pv_narrow = pv_hi, pv_lo
    mw_first, mw_second = _query_halves(jnp.max(s_wide, axis=0, keepdims=True))
    m_narrow = jnp.max(s_narrow, axis=0, keepdims=True)
    if causal:
        m_narrow = jnp.maximum(m_narrow, mw_second)
        m = _join_query_halves(mw_first, m_narrow)
    else:
        m_narrow = jnp.maximum(m_narrow, mw_first)
        m = _join_query_halves(m_narrow, mw_second)
    aw_first, aw_second = _query_halves(pv_wide(jnp.exp2(s_wide - m).astype(BF16)))
    a_narrow = pv_narrow(jnp.exp2(s_narrow - m_narrow).astype(BF16))
    if causal:
        return m, _join_query_halves(aw_first, aw_second + a_narrow)
    return m, _join_query_halves(aw_first + a_narrow, aw_second)


def _flash_pipelined(own, streams, s_sc, m_ref, acc_ref):
    has_prev = jnp.where(own > 0, 1.0, 0.0).astype(F32)
    prev = jnp.maximum(own - 1, 0)
    n_far = jnp.maximum(own - 1, 0)

    def absorb(g, s, kt, c_row, w_row):
        u = jnp.max(s, axis=0, keepdims=True) + c_row
        m_old = m_ref[g]
        m_new = jnp.maximum(m_old, jnp.where(w_row > 0.0, u, NEG_INF))
        alpha = jnp.exp2(m_old - m_new)
        shift = jnp.maximum(m_new, u) - c_row
        p = jnp.exp2(s - shift).astype(BF16)
        acc_ref[g] = alpha * acc_ref[g] + w_row * streams[g]["pv"](kt)(p)
        m_ref[g] = m_new

    def absorb_slot(g, i):
        is_prev = i == 0
        kt = jnp.where(is_prev, prev, i - 1)
        c_row = jnp.where(is_prev, 0.0, streams[g]["c_far"])
        w_row = streams[g]["w"](kt) * jnp.where(is_prev, has_prev, 1.0)
        absorb(g, s_sc[g], kt, c_row, w_row)

    for g, stream in enumerate(streams):
        m_ref[g], acc_ref[g] = stream["own"]()
    for g, stream in enumerate(streams):
        s_sc[g] = stream["prev"]()

    def body(i, carry):
        nxt = [stream["far"](i) for stream in streams]
        for g in range(len(streams)):
            absorb_slot(g, i)
        for g in range(len(streams)):
            s_sc[g] = nxt[g]
        return carry

    lax.fori_loop(0, n_far, body, 0)
    for g in range(len(streams)):
        absorb_slot(g, n_far)


def _softmax_av(s_list, pv_list):
    m = s_list[0].max(axis=0, keepdims=True)
    for s in s_list[1:]:
        m = jnp.maximum(m, s.max(axis=0, keepdims=True))
    acc = None
    for s, pv in zip(s_list, pv_list):
        part = pv(jnp.exp2(s - m).astype(BF16))
        acc = part if acc is None else acc + part
    return acc


def _normalize(acc):
    return acc[:HEAD_DIM] / acc[HEAD_DIM:HEAD_DIM + 1]


def _rank_before(score, n_cand):
    blk = lax.broadcasted_iota(jnp.int32, score.shape, 0)
    rank = jnp.zeros(score.shape, F32)
    for m in range(n_cand):
        row = score[m:m + 1, :]
        tie = jnp.where(blk > m, 1.0, 0.0)
        rank = rank + jnp.where(row > score, 1.0, 0.0) + jnp.where(row == score, tie, 0.0)
    return rank


def _nsa_kernel(q_ref, gn_ref, kc_ref, vct_ref, ks_ref, vs_ref, kw_ref, vw_ref,
                bct_ref, tt_ref, twt_ref, cfar_ref, ovt_ref, o_ref,
                vst_sc, vwt_sc, qa_sc, og_sc, s_sc, m_sc, acc_sc, ot_sc):
    qi = pl.program_id(1)
    nt = ks_ref.shape[1]
    n_sel = ovt_ref.shape[0]

    @pl.when(qi == 0)
    def _():
        for kt in range(nt):
            vs_t = vs_ref[0, kt].astype(F32).T
            vw_t = vw_ref[0, kt].astype(F32).T
            for g in range(NSA_GROUPS):
                _store_v_aug(vst_sc, (kt, g), vs_t[g * HEAD_DIM:(g + 1) * HEAD_DIM])
                _store_v_aug(vwt_sc, (kt, g), vw_t[g * HEAD_DIM:(g + 1) * HEAD_DIM])

    pos = lax.broadcasted_iota(jnp.int32, (1, TILE), 1) + qi * TILE
    cur = pos // SEL_BLOCK
    has_cmp = pos >= CMP_LEN - 1
    blk = lax.broadcasted_iota(jnp.int32, (n_sel, TILE), 0)
    prev = jnp.maximum(qi - 1, 0)
    no_prev = jnp.where(qi == 0, NEG_INF, 0.0).astype(F32)
    gates = gn_ref[...].T

    gsls = [slice(g * HEAD_DIM, (g + 1) * HEAD_DIM) for g in range(NSA_GROUPS)]
    group_heads = [[g * NSA_HPG + j for j in range(NSA_HPG)] for g in range(NSA_GROUPS)]

    def gate(g, branch):
        return _lane_cat([gates[3 * h + branch:3 * h + branch + 1, :] for h in group_heads[g]])

    q_t = q_ref[...].astype(F32).T
    for g in range(NSA_GROUPS):
        heads = group_heads[g]
        q4 = _lane_cat([q_t[h * HEAD_DIM:(h + 1) * HEAD_DIM, :] for h in heads]).astype(BF16)
        qa_sc[g, 0:HEAD_DIM, :] = q4
        qa_sc[g, HEAD_DIM + n_sel:, :] = jnp.zeros((HEAD_DIM - n_sel, NSA_HPG * TILE), BF16)

        def v_half(vt_sc, kt, lo, hi):
            return lambda pr: _dot(vt_sc[kt, g, :, lo:hi], pr)

        m_own, acc_own = _triangle_tile(
            lambda lo, hi: kw_ref[0, qi, lo:hi, gsls[g]], q4, lambda lo, hi: tt_ref[g, 0, lo:hi, :],
            v_half(vwt_sc, qi, 0, HALF), v_half(vwt_sc, qi, HALF, TILE), causal=True)
        m_prev, acc_prev = _triangle_tile(
            lambda lo, hi: kw_ref[0, prev, lo:hi, gsls[g]], q4, lambda lo, hi: twt_ref[g, lo:hi, :] + no_prev,
            v_half(vwt_sc, prev, 0, HALF), v_half(vwt_sc, prev, HALF, TILE), causal=False)
        m_win = jnp.maximum(m_own, m_prev)
        acc_w = acc_own * jnp.exp2(m_own - m_win) + acc_prev * jnp.exp2(m_prev - m_win)
        o_win = gate(g, 2) * _normalize(acc_w)

        kc = kc_ref[0, :, gsls[g]]
        vct = vct_ref[0, gsls[g], :]
        q_cols = pl.ds(pl.multiple_of(qi * TILE, TILE), TILE)
        s = _dot(kc, q4) + _lane_cat([bct_ref[h, :, q_cols] for h in heads])
        e = jnp.exp2(s - jnp.max(s, axis=0, keepdims=True))
        p = jnp.where(_lane_cat([has_cmp] * NSA_HPG), e / jnp.sum(e, axis=0, keepdims=True), 0.0)
        psum = p[:, :TILE]
        for j in range(1, NSA_HPG):
            psum = psum + p[:, j * TILE:(j + 1) * TILE]
        og_sc[g] = gate(g, 0) * _dot(vct, p.astype(BF16)) + o_win

        p_hi, p_lo = _split_bf16(psum)
        imp = _dot(ovt_ref[...], p_hi) + _dot(ovt_ref[...], p_lo)
        forced = (blk == 0) | (blk == cur) | (blk == cur - 1)
        score = jnp.where(forced, FORCE_SCORE, jnp.where(blk <= cur, imp, NEG_INF))
        rank = _rank_before(score, n_sel)
        sel = jnp.where(rank < SEL_TOPN, jnp.where(score > NEG_INF / 2, 0.0, NEG_INF), NEG_INF)
        qa_sc[g, HEAD_DIM:HEAD_DIM + n_sel, :] = _lane_cat([sel.astype(BF16)] * NSA_HPG)

    ones_row = jnp.ones((1, NSA_HPG * TILE), F32)

    def sel_stream(g):
        def qk(kt):
            return _dot(ks_ref[0, kt, :, g * 2 * HEAD_DIM:(g + 1) * 2 * HEAD_DIM], qa_sc[g])

        def own():
            return _triangle_tile(
                lambda lo, hi: ks_ref[0, qi, lo:hi, g * 2 * HEAD_DIM:(g + 1) * 2 * HEAD_DIM], qa_sc[g],
                lambda lo, hi: tt_ref[g, 0, lo:hi, :],
                lambda pr: _dot(vst_sc[qi, g, :, 0:HALF], pr), lambda pr: _dot(vst_sc[qi, g, :, HALF:TILE], pr),
                causal=True)

        return dict(own=own, prev=lambda: qk(prev) + tt_ref[g, 1], far=qk,
                    c_far=cfar_ref[g], w=lambda kt: ones_row,
                    pv=lambda kt: (lambda pr: _dot(vst_sc[kt, g], pr)))

    _flash_pipelined(qi, [sel_stream(g) for g in range(NSA_GROUPS)], s_sc, m_sc, acc_sc)

    for g in range(NSA_GROUPS):
        o = og_sc[g] + gate(g, 1) * _normalize(acc_sc[g])
        for j, h in enumerate(group_heads[g]):
            ot_sc[h * HEAD_DIM:(h + 1) * HEAD_DIM, :] = o[:, j * TILE:(j + 1) * TILE]

    o_ref[...] = ot_sc[...].T.astype(BF16)


def _nsa(b, s, qn, gn, kc, vct, ks, vs, kw, vw, bias_cmp, t_nsa, t_win, c_far, ovt):
    nt = s // TILE
    kv_spec = pl.BlockSpec((1, nt, TILE, NSA_KV_W), lambda i, j: (i, 0, 0, 0))
    ks_spec = pl.BlockSpec((1, nt, TILE, KS_AUG_W), lambda i, j: (i, 0, 0, 0))
    c_spec = pl.BlockSpec((1, N_CMP_PAD, NSA_KV_W), lambda i, j: (i, 0, 0))
    tile4 = lambda a: a.reshape(b, nt, TILE, a.shape[-1])
    return pl.pallas_call(
        _nsa_kernel,
        grid=(b, nt),
        in_specs=[pl.BlockSpec((TILE, NSA_Q_W), lambda i, j: (i * nt + j, 0)),
                  pl.BlockSpec((TILE, GATE_PAD), lambda i, j: (i * nt + j, 0)),
                  c_spec, c_spec, ks_spec, kv_spec, kv_spec, kv_spec,
                  _const_spec(bias_cmp.shape),
                  _const_spec(t_nsa.shape), _const_spec(t_win.shape), _const_spec(c_far.shape),
                  _const_spec(ovt.shape)],
        out_specs=pl.BlockSpec((TILE, NSA_Q_W), lambda i, j: (i * nt + j, 0)),
        out_shape=jax.ShapeDtypeStruct((b * s, NSA_Q_W), BF16),
        scratch_shapes=[pltpu.VMEM((nt, NSA_GROUPS, V_AUG, TILE), BF16),
                        pltpu.VMEM((nt, NSA_GROUPS, V_AUG, TILE), BF16),
                        pltpu.VMEM((NSA_GROUPS, 2 * HEAD_DIM, NSA_HPG * TILE), BF16),
                        pltpu.VMEM((NSA_GROUPS, HEAD_DIM, NSA_HPG * TILE), F32),
                        pltpu.VMEM((NSA_GROUPS, TILE, NSA_HPG * TILE), F32),
                        pltpu.VMEM((NSA_GROUPS, 1, NSA_HPG * TILE), F32),
                        pltpu.VMEM((NSA_GROUPS, V_AUG, NSA_HPG * TILE), F32),
                        pltpu.VMEM((NSA_Q_W, TILE), F32)],
        compiler_params=_params(("arbitrary", "arbitrary")),
        name="nsa",
    )(qn, gn, kc, vct, tile4(ks), tile4(vs), tile4(kw), tile4(vw), bias_cmp, t_nsa, t_win, c_far, ovt)


def _moba_kernel(qm_ref, km_ref, vm_ref, qx_ref, mk_ref, mv_ref, tt_ref, cfar_ref, om_ref, ox_ref,
                 vmt_sc, mvt_sc, kmean_sc, qbd_sc, sel_sc, s_sc, m_sc, acc_sc, ot_sc):
    c = pl.program_id(1)
    nt = km_ref.shape[1]
    hsls = [slice(h * HEAD_DIM, (h + 1) * HEAD_DIM) for h in range(MOBA_HEADS)]

    @pl.when(c == 0)
    def _():
        kmean_sc[...] = jnp.zeros(kmean_sc.shape, F32)
        for n in range(nt):
            kmean_sc[n:n + 1, :] = jnp.sum(km_ref[0, n].astype(F32), axis=0, keepdims=True) * (1.0 / MOBA_BLOCK)
            vt = vm_ref[0, n].astype(F32).T
            for h in range(MOBA_HEADS):
                _store_v_aug(vmt_sc, (n, h), vt[hsls[h]])
        mvt = mv_ref[0].astype(F32).T
        for h in range(MOBA_HEADS):
            _store_v_aug(mvt_sc, h, mvt[hsls[h]])

    row_head = lax.broadcasted_iota(jnp.int32, (MOBA_W, TILE), 0) // HEAD_DIM

    def block_diag(q_ref):
        q_t = q_ref[...].astype(F32).T
        return _lane_cat([jnp.where(row_head == h, q_t, 0.0) for h in range(MOBA_HEADS)]).astype(BF16)

    def per_head_pv(vts):
        return lambda pr: _lane_cat([_dot(vts(h), pr[:, h * TILE:(h + 1) * TILE]) for h in range(MOBA_HEADS)])

    def store_heads(o_t, out_ref):
        for h in range(MOBA_HEADS):
            ot_sc[hsls[h], :] = o_t[:, h * TILE:(h + 1) * TILE]
        out_ref[...] = ot_sc[...].T.astype(BF16)

    qbd = block_diag(qm_ref)
    km_hi, km_lo = _split_bf16(kmean_sc[...])
    gate = _dot(km_hi, qbd) + _dot(km_lo, qbd)
    blk = lax.broadcasted_iota(jnp.int32, gate.shape, 0)
    score = jnp.where(blk < c, gate, NEG_INF * Q_SCALE)
    rank = _rank_before(score, nt)
    sel_sc[...] = jnp.where(rank < MOBA_TOPK, jnp.where(score > NEG_INF * Q_SCALE / 2, 1.0, 0.0), 0.0)

    qbd_sc[...] = qbd
    qk = lambda n: _dot(km_ref[0, n], qbd_sc[...])
    def own_pv(lo, hi):
        def pv(pr):
            width = pr.shape[1] // MOBA_HEADS
            return _lane_cat([_dot(vmt_sc[c, h, :, lo:hi], pr[:, h * width:(h + 1) * width])
                              for h in range(MOBA_HEADS)])
        return pv

    def own():
        return _triangle_tile(lambda lo, hi: km_ref[0, c, lo:hi, :], qbd_sc[...], lambda lo, hi: tt_ref[0, lo:hi, :],
                              own_pv(0, HALF), own_pv(HALF, TILE), causal=True)

    stream = dict(own=own, prev=lambda: qk(jnp.maximum(c - 1, 0)) + tt_ref[1], far=qk,
                  c_far=cfar_ref[...], w=lambda n: sel_sc[pl.ds(n, 1), :],
                  pv=lambda n: per_head_pv(lambda h: vmt_sc[n, h]))
    s = _dot(mk_ref[0], block_diag(qx_ref))
    store_heads(_normalize(_softmax_av([s], [per_head_pv(lambda h: mvt_sc[h])])), ox_ref)

    _flash_pipelined(c, [stream], s_sc, m_sc, acc_sc)
    store_heads(_normalize(acc_sc[0]), om_ref)


def _moba(b, s, qm, km, vm, qx, mk, mv, t_moba, c_far):
    nt = s // TILE
    mem_len = mk.shape[0] // b
    assert MOBA_TOPK <= nt - 1 and nt <= BF16_ROWS
    q_spec = pl.BlockSpec((TILE, MOBA_W), lambda i, j: (i * nt + j, 0))
    kv_spec = pl.BlockSpec((1, nt, TILE, MOBA_W), lambda i, j: (i, 0, 0, 0))
    mem_spec = pl.BlockSpec((1, mem_len, MEM_W), lambda i, j: (i, 0, 0))
    return pl.pallas_call(
        _moba_kernel,
        grid=(b, nt),
        in_specs=[q_spec, kv_spec, kv_spec, q_spec, mem_spec, mem_spec, _const_spec(t_moba.shape),
                  _const_spec(c_far.shape)],
        out_specs=[q_spec, q_spec],
        out_shape=[jax.ShapeDtypeStruct((b * s, MOBA_W), BF16), jax.ShapeDtypeStruct((b * s, MEM_W), BF16)],
        scratch_shapes=[pltpu.VMEM((nt, MOBA_HEADS, V_AUG, TILE), BF16),
                        pltpu.VMEM((MEM_HEADS, V_AUG, mem_len), BF16),
                        pltpu.VMEM((BF16_ROWS, MOBA_W), F32),
                        pltpu.VMEM((MOBA_W, MOBA_HEADS * TILE), BF16),
                        pltpu.VMEM((BF16_ROWS, MOBA_HEADS * TILE), F32),
                        pltpu.VMEM((1, TILE, MOBA_HEADS * TILE), F32),
                        pltpu.VMEM((1, 1, MOBA_HEADS * TILE), F32),
                        pltpu.VMEM((1, V_AUG, MOBA_HEADS * TILE), F32),
                        pltpu.VMEM((MOBA_W, TILE), F32)],
        compiler_params=_params(("arbitrary", "arbitrary")),
        name="moba",
    )(qm, km.reshape(b, nt, TILE, MOBA_W), vm.reshape(b, nt, TILE, MOBA_W), qx,
      mk.reshape(b, mem_len, MEM_W), mv.reshape(b, mem_len, MEM_W), t_moba, c_far)


def _mix_kernel(x_ref, on_ref, om_ref, ox_ref, g_pre_ref, g_post_ref, wg_ref, wn_ref, wm_ref, wx_ref,
                wo_ref, o_ref):
    x = x_ref[...]
    h = _rms(x, g_pre_ref[...]).astype(BF16)
    merged = jax.nn.sigmoid(_dot(h, wg_ref[:, :D_MODEL])) * _dot(on_ref[...], wn_ref[...])
    merged = merged + jax.nn.sigmoid(_dot(h, wg_ref[:, D_MODEL:2 * D_MODEL])) * _dot(om_ref[...], wm_ref[...])
    merged = merged + jax.nn.sigmoid(_dot(h, wg_ref[:, 2 * D_MODEL:])) * _dot(ox_ref[...], wx_ref[...])
    y = _dot(merged.astype(BF16), wo_ref[...])
    o_ref[...] = x + _rms(y, g_post_ref[...])


def _mix(x2, o_nsa, o_moba, o_mem, g_pre, g_post, w_gates, w_nsa_o, w_moba_o, w_mem_o, w_mix_out, tm=512):
    m = x2.shape[0]
    row = lambda w: pl.BlockSpec((tm, w), lambda i: (i, 0))
    return pl.pallas_call(
        _mix_kernel,
        grid=(m // tm,),
        in_specs=[row(D_MODEL), row(NSA_Q_W), row(MOBA_W), row(MEM_W),
                  _const_spec((1, D_MODEL)), _const_spec((1, D_MODEL)),
                  _const_spec(w_gates.shape), _const_spec(w_nsa_o.shape), _const_spec(w_moba_o.shape),
                  _const_spec(w_mem_o.shape), _const_spec(w_mix_out.shape)],
        out_specs=row(D_MODEL),
        out_shape=jax.ShapeDtypeStruct((m, D_MODEL), F32),
        compiler_params=_params(("parallel",)),
        name="mix",
    )(x2, o_nsa, o_moba, o_mem, g_pre, g_post, w_gates, w_nsa_o, w_moba_o, w_mem_o, w_mix_out)


FFN_CHUNK = 256


def _ffn_kernel(x_ref, g_pre_ref, g_post_ref, wg_ref, wu_ref, wd_ref, o_ref, a_sc):
    x = x_ref[...]
    h = _rms(x, g_pre_ref[...]).astype(BF16)
    d_ff = wg_ref.shape[1]
    for j in range(d_ff // FFN_CHUNK):
        sl = slice(j * FFN_CHUNK, (j + 1) * FFN_CHUNK)
        a_sc[:, sl] = (jax.nn.silu(_dot(h, wg_ref[:, sl])) * _dot(h, wu_ref[:, sl])).astype(BF16)
    f = _dot(a_sc[...], wd_ref[...])
    o_ref[...] = x + _rms(f, g_post_ref[...])


def _ffn(x2, g_pre, g_post, wg, wu, wd, tm=512):
    m = x2.shape[0]
    d_ff = wg.shape[1]
    return pl.pallas_call(
        _ffn_kernel,
        grid=(m // tm,),
        in_specs=[pl.BlockSpec((tm, D_MODEL), lambda i: (i, 0)),
                  _const_spec((1, D_MODEL)), _const_spec((1, D_MODEL)),
                  _const_spec(wg.shape), _const_spec(wu.shape), _const_spec(wd.shape)],
        out_specs=pl.BlockSpec((tm, D_MODEL), lambda i: (i, 0)),
        out_shape=jax.ShapeDtypeStruct((m, D_MODEL), F32),
        scratch_shapes=[pltpu.VMEM((tm, d_ff), BF16)],
        compiler_params=_params(("parallel",)),
        name="ffn",
    )(x2, g_pre, g_post, wg, wu, wd)


def kernel(x, mem, rel_bias, pre_mix_g, mem_norm_g, post_mix_g, w_in, cmp_pos_k, cmp_w1_k, cmp_w2_k, cmp_pos_v, cmp_w1_v, cmp_w2_v, w_mem_kv, w_nsa_o, w_moba_o, w_mem_o, w_mix_out, pre_ffn_g, post_ffn_g, w_ffn_gate, w_ffn_up, w_ffn_down):
    b, s, d_model = x.shape
    depth = w_in.shape[0]
    assert d_model == D_MODEL and s % TILE == 0 and TILE == MOBA_BLOCK == WINDOW
    assert (s - CMP_LEN) // CMP_STRIDE + 1 < N_CMP_PAD and (s // SEL_BLOCK) % SUBLANES == 0 and s // SEL_BLOCK <= HEAD_DIM
    assert w_in.shape[2] == ATT_W + 3 * D_MODEL and rel_bias.shape == (REL_BUCKETS, N_BIAS_HEADS)

    tile_idx, win_idx, cmp_idx = _bucket_tables(s)
    rel_bias = rel_bias.astype(F32)
    t_nsa = _expand(tile_idx, rel_bias, 0, NSA_HEADS, NSA_HPG).reshape(NSA_GROUPS, 2, TILE, NSA_HPG * TILE)
    t_moba = _expand(tile_idx, rel_bias, NSA_HEADS, MOBA_HEADS, MOBA_HEADS).reshape(2, TILE, MOBA_HEADS * TILE)
    t_win = _expand(win_idx, rel_bias, 0, NSA_HEADS, NSA_HPG)
    b_cmp = _expand(cmp_idx, rel_bias, 0, NSA_HEADS)
    c_far = jnp.repeat(rel_bias[REL_BUCKETS - 1] * LOG2E, TILE)
    c_far_nsa = c_far[:NSA_HEADS * TILE].reshape(NSA_GROUPS, 1, NSA_HPG * TILE)
    c_far_moba = c_far[NSA_HEADS * TILE:].reshape(1, MOBA_HEADS * TILE)
    ovt = _overlap_table(s)
    sel_cols = np.zeros((s, HEAD_DIM), np.float32)
    sel_cols[np.arange(s), np.arange(s) // SEL_BLOCK] = 1.0
    sel_cols = jnp.asarray(sel_cols, BF16)
    gate_lo = NSA_Q_W + 6 * NSA_KV_W
    rows_per_chunk = CMP_STRIDE * NSA_KV_W

    x2 = x.reshape(b * s, D_MODEL)
    mem2 = mem.reshape(-1, D_MODEL)
    for l in range(depth):
        w_att = jnp.concatenate(
            [w_in[l, :, :gate_lo + NSA_GATE_W],
             jnp.zeros((D_MODEL, GATE_PAD - NSA_GATE_W), w_in.dtype),
             w_in[l, :, gate_lo + NSA_GATE_W:ATT_W]], axis=1).astype(BF16)
        w_gates = w_in[l, :, ATT_W:].astype(BF16)
        row = lambda v: v[l].reshape(1, D_MODEL)

        qn, kc_raw, vc_raw, ks, vs, kw, vw, gn, qm, km, vm, qx = _inproj(x2, row(pre_mix_g), w_att, sel_cols)

        pk, w1k = _compress_weights(cmp_pos_k[l], cmp_w1_k[l])
        pv, w1v = _compress_weights(cmp_pos_v[l], cmp_w1_v[l])
        kc, vct = _compress(kc_raw.reshape(b, s // CMP_STRIDE, rows_per_chunk),
                            vc_raw.reshape(b, s // CMP_STRIDE, rows_per_chunk),
                            pk, pv, w1k, w1v, cmp_w2_k[l].astype(BF16), cmp_w2_v[l].astype(BF16))

        mk, mv = _memkv(mem2, row(mem_norm_g), w_mem_kv[l].astype(BF16))

        o_nsa = _nsa(b, s, qn, gn, kc, vct, ks, vs, kw, vw, b_cmp, t_nsa, t_win, c_far_nsa, ovt)
        o_moba, o_mem = _moba(b, s, qm, km, vm, qx, mk, mv, t_moba, c_far_moba)

        x2 = _mix(x2, o_nsa, o_moba, o_mem, row(pre_mix_g), row(post_mix_g), w_gates,
                  w_nsa_o[l].astype(BF16), w_moba_o[l].astype(BF16), w_mem_o[l].astype(BF16),
                  w_mix_out[l].astype(BF16))
        x2 = _ffn(x2, row(pre_ffn_g), row(post_ffn_g), w_ffn_gate[l].astype(BF16),
                  w_ffn_up[l].astype(BF16), w_ffn_down[l].astype(BF16))
    return x2.reshape(b, s, D_MODEL)
```

```python
import functools
import math

import numpy as np
import jax
import jax.numpy as jnp
from jax import lax
from jax.experimental import pallas as pl
from jax.experimental.pallas import tpu as pltpu

F32 = jnp.float32
BF16 = jnp.bfloat16

D_MODEL = 1024
HEAD_DIM = 64
SCALE = HEAD_DIM ** -0.5
LOG2E = math.log2(math.e)
Q_SCALE = SCALE * LOG2E
NSA_HEADS = 8
NSA_GROUPS = 2
NSA_HPG = NSA_HEADS // NSA_GROUPS
CMP_LEN = 32
CMP_STRIDE = 16
CMP_HIDDEN = 128
SEL_BLOCK = 64
SEL_TOPN = 8
WINDOW = 256
MOBA_HEADS = 4
MOBA_BLOCK = 256
MOBA_TOPK = 3
MEM_HEADS = 4
REL_BUCKETS = 32
REL_MAX_DIST = 128
N_BIAS_HEADS = NSA_HEADS + MOBA_HEADS
RMS_EPS = 1e-6
NEG_INF = -1e30
FORCE_SCORE = 1e4

NSA_Q_W = NSA_HEADS * HEAD_DIM
NSA_KV_W = NSA_GROUPS * HEAD_DIM
NSA_GATE_W = NSA_HEADS * 3
MOBA_W = MOBA_HEADS * HEAD_DIM
MEM_W = MEM_HEADS * HEAD_DIM
ATT_W = NSA_Q_W + 6 * NSA_KV_W + NSA_GATE_W + 3 * MOBA_W + MEM_W
LANES = 128
SUBLANES = 8
BF16_ROWS = 16
MXU_COLS = 256
GATE_PAD = LANES
TILE = 256
HALF = TILE // 2
N_CMP_PAD = 128
V_AUG = HEAD_DIM + BF16_ROWS
MASKED_BUCKET = REL_BUCKETS
VMEM_LIMIT = 56 * 1024 * 1024


def _dot(a, b):
    return jnp.dot(a, b, preferred_element_type=F32)


def _split_bf16(x):
    hi = x.astype(BF16)
    lo = (x - hi.astype(F32)).astype(BF16)
    return hi, lo


def _rms(x, g):
    return x * lax.rsqrt(jnp.mean(x * x, axis=-1, keepdims=True) + RMS_EPS) * g


def _params(sem):
    return pltpu.CompilerParams(dimension_semantics=sem, vmem_limit_bytes=VMEM_LIMIT)


def _const_spec(shape):
    nd = len(shape)
    return pl.BlockSpec(shape, lambda *_: (0,) * nd, pipeline_mode=pl.Buffered(1))


_INPROJ_OUTS = (
    ("qn", NSA_Q_W, BF16, True),
    ("kc", NSA_KV_W, F32, False), ("vc", NSA_KV_W, F32, False),
    ("ks", NSA_KV_W, BF16, False), ("vs", NSA_KV_W, BF16, False),
    ("kw", NSA_KV_W, BF16, False), ("vw", NSA_KV_W, BF16, False),
    ("gn", GATE_PAD, F32, False),
    ("qm", MOBA_W, BF16, True), ("km", MOBA_W, BF16, False), ("vm", MOBA_W, BF16, False),
    ("qx", MEM_W, BF16, True),
)
_INPROJ_W = sum(o[1] for o in _INPROJ_OUTS)
_INPROJ_CHUNKED = ("kc", "vc")
KS_AUG_W = NSA_GROUPS * 2 * HEAD_DIM


def _inproj_out_width(name, width):
    return KS_AUG_W if name == "ks" else width


def _inproj_kernel(x_ref, g_ref, w_ref, e_ref, *refs):
    out_refs, rows_sc = refs[:-1], refs[-1]
    h = _rms(x_ref[...], g_ref[...]).astype(BF16)
    runs, lo = [], 0
    for out in zip(_INPROJ_OUTS, out_refs):
        if runs and runs[-1][1] < MXU_COLS:
            runs[-1][0].append(out)
            runs[-1][1] += out[0][1]
        else:
            runs.append([[out], out[0][1], lo])
        lo += out[0][1]
    for outs, run_width, run_lo in runs:
        y_run = _dot(h, w_ref[:, run_lo:run_lo + run_width])
        lo = 0
        for (name, width, dtype, scaled), o_ref in outs:
            y = y_run[:, lo:lo + width]
            if scaled:
                y = y * Q_SCALE
            if name == "gn":
                y = jax.nn.sigmoid(y)
            y = y.astype(dtype)
            if name == "ks":
                e = e_ref[...]
                y = _lane_cat([y[:, :HEAD_DIM], e, y[:, HEAD_DIM:], e])
            if name in _INPROJ_CHUNKED:
                rows_sc[...] = y
                for j in range(CMP_STRIDE):
                    o_ref[:, j * width:(j + 1) * width] = rows_sc[pl.ds(j, o_ref.shape[0], stride=CMP_STRIDE), :]
            else:
                o_ref[...] = y
            lo += width


def _inproj(x2, g, w, e_cols, tm=512):
    m = x2.shape[0]
    tiles_per_seq = e_cols.shape[0] // tm
    outs = []
    for name, width, dtype, _ in _INPROJ_OUTS:
        if name in _INPROJ_CHUNKED:
            outs.append((CMP_STRIDE, CMP_STRIDE * width, dtype))
        else:
            outs.append((1, _inproj_out_width(name, width), dtype))
    return pl.pallas_call(
        _inproj_kernel,
        grid=(m // tm,),
        in_specs=[pl.BlockSpec((tm, D_MODEL), lambda i: (i, 0)),
                  _const_spec((1, D_MODEL)),
                  _const_spec((D_MODEL, _INPROJ_W)),
                  pl.BlockSpec((tm, HEAD_DIM), lambda i: (i % tiles_per_seq, 0))],
        out_specs=[pl.BlockSpec((tm // r, w), lambda i: (i, 0)) for r, w, _ in outs],
        out_shape=[jax.ShapeDtypeStruct((m // r, w), d) for r, w, d in outs],
        scratch_shapes=[pltpu.VMEM((tm, NSA_KV_W), F32)],
        compiler_params=_params(("parallel",)),
        name="inproj",
    )(x2, g, w, e_cols)


def _compress_kernel(rk_ref, rv_ref, pk_ref, pv_ref, w1k_ref, w1v_ref, w2k_ref, w2v_ref, kc_ref, vc_ref):
    def one(r_ref, p_ref, w1_ref, w2_ref):
        r = r_ref[0]
        top = _dot((r + p_ref[0:1, :]).astype(BF16), w1_ref[0])
        bot = _dot((r + p_ref[1:2, :]).astype(BF16), w1_ref[1])
        hid = top + pltpu.roll(bot, N_CMP_PAD - 1, 0)
        act = jax.nn.gelu(hid).astype(BF16)
        return jnp.concatenate(
            [_dot(act[:, g * CMP_HIDDEN:(g + 1) * CMP_HIDDEN], w2_ref[...]) for g in range(NSA_GROUPS)], axis=1)

    kc_ref[0] = one(rk_ref, pk_ref, w1k_ref, w2k_ref).astype(BF16)
    vc_ref[0] = one(rv_ref, pv_ref, w1v_ref, w2v_ref).T.astype(BF16)


def _compress(rk, rv, pk, pv, w1k, w1v, w2k, w2v):
    b = rk.shape[0]
    rw = rk.shape[2]
    r_spec = pl.BlockSpec((1, N_CMP_PAD, rw), lambda i: (i, 0, 0))
    o_spec = pl.BlockSpec((1, N_CMP_PAD, NSA_KV_W), lambda i: (i, 0, 0))
    return pl.pallas_call(
        _compress_kernel,
        grid=(b,),
        in_specs=[r_spec, r_spec, _const_spec(pk.shape), _const_spec(pv.shape),
                  _const_spec(w1k.shape), _const_spec(w1v.shape),
                  _const_spec(w2k.shape), _const_spec(w2v.shape)],
        out_specs=[o_spec, o_spec],
        out_shape=[jax.ShapeDtypeStruct((b, N_CMP_PAD, NSA_KV_W), BF16)] * 2,
        compiler_params=_params(("parallel",)),
        name="compress",
    )(rk, rv, pk, pv, w1k, w1v, w2k, w2v)


def _compress_weights(pos, w1):
    half = CMP_LEN // 2
    p = pos.reshape(2, half, 1, HEAD_DIM)
    p = jnp.broadcast_to(p, (2, half, NSA_GROUPS, HEAD_DIM)).reshape(2, half * NSA_KV_W)
    w = w1.reshape(2, half, HEAD_DIM, CMP_HIDDEN)
    eye = jnp.eye(NSA_GROUPS, dtype=w1.dtype)
    wbd = jnp.einsum("ajdm,gk->ajgdkm", w, eye).reshape(2, half * NSA_KV_W, NSA_GROUPS * CMP_HIDDEN)
    return p.astype(F32), wbd.astype(BF16)


def _memkv_kernel(m_ref, g_ref, w_ref, k_ref, v_ref):
    h = _rms(m_ref[...], g_ref[...]).astype(BF16)
    k_ref[...] = _dot(h, w_ref[:, :MEM_W]).astype(BF16)
    v_ref[...] = _dot(h, w_ref[:, MEM_W:]).astype(BF16)


def _memkv(mem2, g, w, tm=256):
    m = mem2.shape[0]
    o_spec = pl.BlockSpec((tm, MEM_W), lambda i: (i, 0))
    return pl.pallas_call(
        _memkv_kernel,
        grid=(m // tm,),
        in_specs=[pl.BlockSpec((tm, D_MODEL), lambda i: (i, 0)), _const_spec((1, D_MODEL)),
                  _const_spec((D_MODEL, 2 * MEM_W))],
        out_specs=[o_spec, o_spec],
        out_shape=[jax.ShapeDtypeStruct((m, MEM_W), BF16)] * 2,
        compiler_params=_params(("parallel",)),
        name="memkv",
    )(mem2, g, w)


def _expand_kernel(idx_ref, bias_ref, o_ref, *, head0, n_heads, heads_per_group):
    rows, cols = idx_ref.shape

    def body(i, carry):
        r = pl.multiple_of(i * SUBLANES, SUBLANES)
        for c0 in range(0, cols, TILE):
            idx = idx_ref[pl.ds(r, SUBLANES), c0:c0 + TILE]
            out = [jnp.full(idx.shape, NEG_INF, F32)] * n_heads
            for bkt in range(REL_BUCKETS):
                hit = idx == bkt
                out = [jnp.where(hit, bias_ref[bkt, head0 + h], out[h]) for h in range(n_heads)]
            for h in range(n_heads):
                col = (h % heads_per_group) * cols + c0
                o_ref[h // heads_per_group, pl.ds(r, SUBLANES), col:col + TILE] = out[h] * LOG2E
        return carry

    lax.fori_loop(0, rows // SUBLANES, body, 0)


def _expand(idx, rel_bias, head0, n_heads, heads_per_group=1):
    rows, cols = idx.shape
    return pl.pallas_call(
        functools.partial(_expand_kernel, head0=head0, n_heads=n_heads, heads_per_group=heads_per_group),
        in_specs=[pl.BlockSpec(memory_space=pltpu.VMEM), pl.BlockSpec(memory_space=pltpu.SMEM)],
        out_specs=pl.BlockSpec(memory_space=pltpu.VMEM),
        out_shape=jax.ShapeDtypeStruct((n_heads // heads_per_group, rows, heads_per_group * cols), F32),
        compiler_params=pltpu.CompilerParams(vmem_limit_bytes=VMEM_LIMIT),
        name="bias_expand",
    )(idx, rel_bias)


def _t5_bucket_np(dist):
    dist = np.maximum(dist, 0)
    max_exact = REL_BUCKETS // 2
    logd = np.log(np.maximum(dist, 1).astype(np.float32) / max_exact) / math.log(REL_MAX_DIST / max_exact)
    large = np.minimum(max_exact + (logd * (REL_BUCKETS - max_exact)).astype(np.int32), REL_BUCKETS - 1)
    return np.where(dist < max_exact, dist, large).astype(np.int32)


def _bucket_tables(s):
    j = np.arange(TILE)[:, None]
    i = np.arange(TILE)[None, :]
    assert TILE + 1 >= REL_MAX_DIST
    tiles = []
    for d in range(2):
        dist = d * TILE + i - j
        tiles.append(np.where(dist >= 0, _t5_bucket_np(dist), MASKED_BUCKET))
    dist1 = TILE + i - j
    win = np.where(dist1 < WINDOW, _t5_bucket_np(dist1), MASKED_BUCKET)
    n_cmp = (s - CMP_LEN) // CMP_STRIDE + 1
    c = np.arange(N_CMP_PAD)[:, None]
    dist_c = np.arange(s)[None, :] - (c * CMP_STRIDE + CMP_LEN - 1)
    cmp_idx = np.where((dist_c >= 0) & (c < n_cmp), _t5_bucket_np(dist_c), MASKED_BUCKET)
    as_i32 = lambda a: jnp.asarray(a.astype(np.int32))
    return as_i32(np.concatenate(tiles, axis=0)), as_i32(win), as_i32(cmp_idx)


def _overlap_table(s):
    n_cmp = (s - CMP_LEN) // CMP_STRIDE + 1
    n_sel = s // SEL_BLOCK
    cs = np.arange(n_cmp) * CMP_STRIDE
    ss = np.arange(n_sel) * SEL_BLOCK
    ov = np.clip(np.minimum(cs[:, None] + CMP_LEN, ss[None, :] + SEL_BLOCK)
                 - np.maximum(cs[:, None], ss[None, :]), 0, None).astype(np.float32) / CMP_LEN
    ovt = np.zeros((n_sel, N_CMP_PAD), np.float32)
    ovt[:, :n_cmp] = ov.T
    return jnp.asarray(ovt, BF16)


def _store_v_aug(vt_sc, idx, vt):
    ones = jnp.ones((BF16_ROWS, vt.shape[1]), BF16)
    vt_sc[idx] = jnp.concatenate([vt.astype(BF16), ones], axis=0)


def _lane_cat(xs):
    return jnp.concatenate(xs, axis=1)


def _query_halves(x):
    n = x.shape[-1] // TILE
    first = _lane_cat([x[:, k * TILE:k * TILE + HALF] for k in range(n)])
    second = _lane_cat([x[:, k * TILE + HALF:(k + 1) * TILE] for k in range(n)])
    return first, second


def _join_query_halves(first, second):
    n = first.shape[-1] // HALF
    return _lane_cat([part for k in range(n)
                      for part in (first[:, k * HALF:(k + 1) * HALF], second[:, k * HALF:(k + 1) * HALF])])


def _triangle_tile(k, q, table, pv_lo, pv_hi, causal):
    q_first, q_second = _query_halves(q)
    lo, hi = (0, HALF), (HALF, TILE)
    if causal:
        s_wide = _dot(k(*lo), q) + table(*lo)
        s_narrow = _dot(k(*hi), q_second) + _query_halves(table(*hi))[1]
        pv_wide, pv_narrow = pv_lo, pv_hi
    else:
        s_wide = _dot(k(*hi), q) + table(*hi)
        s_narrow = _dot(k(*lo), q_first) + _query_halves(table(*lo))[0]
        pv_wide, pv_narrow = pv_hi, pv_lo
    mw_first, mw_second = _query_halves(jnp.max(s_wide, axis=0, keepdims=True))
    m_narrow = jnp.max(s_narrow, axis=0, keepdims=True)
    if causal:
        m_narrow = jnp.maximum(m_narrow, mw_second)
        m = _join_query_halves(mw_first, m_narrow)
    else:
        m_narrow = jnp.maximum(m_narrow, mw_first)
        m = _join_query_halves(m_narrow, mw_second)
    aw_first, aw_second = _query_halves(pv_wide(jnp.exp2(s_wide - m).astype(BF16)))
    a_narrow = pv_narrow(jnp.exp2(s_narrow - m_narrow).astype(BF16))
    if causal:
        return m, _join_query_halves(aw_first, aw_second + a_narrow)
    return m, _join_query_halves(aw_first + a_narrow, aw_second)


def _flash_pipelined(own, streams, s_sc, m_ref, acc_ref):
    has_prev = jnp.where(own > 0, 1.0, 0.0).astype(F32)
    prev = jnp.maximum(own - 1, 0)
    n_far = jnp.maximum(own - 1, 0)

    def absorb(g, s, kt, c_row, w_row):
        u = jnp.max(s, axis=0, keepdims=True) + c_row
        m_old = m_ref[g]
        m_new = jnp.maximum(m_old, jnp.where(w_row > 0.0, u, NEG_INF))
        alpha = jnp.exp2(m_old - m_new)
        shift = jnp.maximum(m_new, u) - c_row
        p = jnp.exp2(s - shift).astype(BF16)
        acc_ref[g] = alpha * acc_ref[g] + w_row * streams[g]["pv"](kt)(p)
        m_ref[g] = m_new

    def absorb_slot(g, i):
        is_prev = i == 0
        kt = jnp.where(is_prev, prev, i - 1)
        c_row = jnp.where(is_prev, 0.0, streams[g]["c_far"])
        w_row = streams[g]["w"](kt) * jnp.where(is_prev, has_prev, 1.0)
        absorb(g, s_sc[g], kt, c_row, w_row)

    for g, stream in enumerate(streams):
        m_ref[g], acc_ref[g] = stream["own"]()
    for g, stream in enumerate(streams):
        s_sc[g] = stream["prev"]()

    def body(i, carry):
        nxt = [stream["far"](i) for stream in streams]
        for g in range(len(streams)):
            absorb_slot(g, i)
        for g in range(len(streams)):
            s_sc[g] = nxt[g]
        return carry

    lax.fori_loop(0, n_far, body, 0)
    for g in range(len(streams)):
        absorb_slot(g, n_far)


def _softmax_av(s_list, pv_list):
    m = s_list[0].max(axis=0, keepdims=True)
    for s in s_list[1:]:
        m = jnp.maximum(m, s.max(axis=0, keepdims=True))
    acc = None
    for s, pv in zip(s_list, pv_list):
        part = pv(jnp.exp2(s - m).astype(BF16))
        acc = part if acc is None else acc + part
    return acc


def _normalize(acc):
    return acc[:HEAD_DIM] / acc[HEAD_DIM:HEAD_DIM + 1]


def _rank_before(score, n_cand):
    blk = lax.broadcasted_iota(jnp.int32, score.shape, 0)
    rank = jnp.zeros(score.shape, F32)
    for m in range(n_cand):
        row = score[m:m + 1, :]
        tie = jnp.where(blk > m, 1.0, 0.0)
        rank = rank + jnp.where(row > score, 1.0, 0.0) + jnp.where(row == score, tie, 0.0)
    return rank


def _nsa_steps(qi, q_ref, gn_ref, kc_ref, vct_ref, ks_ref, vs_ref, kw_ref, vw_ref,
               bct_ref, tt_ref, twt_ref, cfar_ref, ovt_ref, o_ref,
               vst_sc, vwt_sc, qa_sc, og_sc, acc_sc, ot_sc):
    nt = ks_ref.shape[1]
    n_sel = ovt_ref.shape[0]

    @pl.when(qi == 0)
    def _():
        for kt in range(nt):
            vs_t = vs_ref[0, kt].astype(F32).T
            vw_t = vw_ref[0, kt].astype(F32).T
            for g in range(NSA_GROUPS):
                _store_v_aug(vst_sc, (kt, g), vs_t[g * HEAD_DIM:(g + 1) * HEAD_DIM])
                _store_v_aug(vwt_sc, (kt, g), vw_t[g * HEAD_DIM:(g + 1) * HEAD_DIM])

    pos = lax.broadcasted_iota(jnp.int32, (1, TILE), 1) + qi * TILE
    cur = pos // SEL_BLOCK
    has_cmp = pos >= CMP_LEN - 1
    blk = lax.broadcasted_iota(jnp.int32, (n_sel, TILE), 0)
    prev = jnp.maximum(qi - 1, 0)
    no_prev = jnp.where(qi == 0, NEG_INF, 0.0).astype(F32)
    gates = gn_ref[...].T

    gsls = [slice(g * HEAD_DIM, (g + 1) * HEAD_DIM) for g in range(NSA_GROUPS)]
    group_heads = [[g * NSA_HPG + j for j in range(NSA_HPG)] for g in range(NSA_GROUPS)]

    def gate(g, branch):
        return _lane_cat([gates[3 * h + branch:3 * h + branch + 1, :] for h in group_heads[g]])

    q_t = q_ref[...].astype(F32).T
    for g in range(NSA_GROUPS):
        heads = group_heads[g]
        q4 = _lane_cat([q_t[h * HEAD_DIM:(h + 1) * HEAD_DIM, :] for h in heads]).astype(BF16)
        qa_sc[g, 0:HEAD_DIM, :] = q4
        qa_sc[g, HEAD_DIM + n_sel:, :] = jnp.zeros((HEAD_DIM - n_sel, NSA_HPG * TILE), BF16)

        def v_half(vt_sc, kt, lo, hi):
            return lambda pr: _dot(vt_sc[kt, g, :, lo:hi], pr)

        m_own, acc_own = _triangle_tile(
            lambda lo, hi: kw_ref[0, qi, lo:hi, gsls[g]], q4, lambda lo, hi: tt_ref[g, 0, lo:hi, :],
            v_half(vwt_sc, qi, 0, HALF), v_half(vwt_sc, qi, HALF, TILE), causal=True)
        m_prev, acc_prev = _triangle_tile(
            lambda lo, hi: kw_ref[0, prev, lo:hi, gsls[g]], q4, lambda lo, hi: twt_ref[g, lo:hi, :] + no_prev,
            v_half(vwt_sc, prev, 0, HALF), v_half(vwt_sc, prev, HALF, TILE), causal=False)
        m_win = jnp.maximum(m_own, m_prev)
        acc_w = acc_own * jnp.exp2(m_own - m_win) + acc_prev * jnp.exp2(m_prev - m_win)
        o_win = gate(g, 2) * _normalize(acc_w)

        kc = kc_ref[0, :, gsls[g]]
        vct = vct_ref[0, gsls[g], :]
        q_cols = pl.ds(pl.multiple_of(qi * TILE, TILE), TILE)
        s = _dot(kc, q4) + _lane_cat([bct_ref[h, :, q_cols] for h in heads])
        e = jnp.exp2(s - jnp.max(s, axis=0, keepdims=True))
        p = jnp.where(_lane_cat([has_cmp] * NSA_HPG), e / jnp.sum(e, axis=0, keepdims=True), 0.0)
        psum = p[:, :TILE]
        for j in range(1, NSA_HPG):
            psum = psum + p[:, j * TILE:(j + 1) * TILE]
        og_sc[g] = gate(g, 0) * _dot(vct, p.astype(BF16)) + o_win

        p_hi, p_lo = _split_bf16(psum)
        imp = _dot(ovt_ref[...], p_hi) + _dot(ovt_ref[...], p_lo)
        forced = (blk == 0) | (blk == cur) | (blk == cur - 1)
        score = jnp.where(forced, FORCE_SCORE, jnp.where(blk <= cur, imp, NEG_INF))
        rank = _rank_before(score, n_sel)
        sel = jnp.where(rank < SEL_TOPN, jnp.where(score > NEG_INF / 2, 0.0, NEG_INF), NEG_INF)
        qa_sc[g, HEAD_DIM:HEAD_DIM + n_sel, :] = _lane_cat([sel.astype(BF16)] * NSA_HPG)

    ones_row = jnp.ones((1, NSA_HPG * TILE), F32)

    def sel_stream(g):
        def qk(kt):
            return _dot(ks_ref[0, kt, :, g * 2 * HEAD_DIM:(g + 1) * 2 * HEAD_DIM], qa_sc[g])

        def own():
            return _triangle_tile(
                lambda lo, hi: ks_ref[0, qi, lo:hi, g * 2 * HEAD_DIM:(g + 1) * 2 * HEAD_DIM], qa_sc[g],
                lambda lo, hi: tt_ref[g, 0, lo:hi, :],
                lambda pr: _dot(vst_sc[qi, g, :, 0:HALF], pr), lambda pr: _dot(vst_sc[qi, g, :, HALF:TILE], pr),
                causal=True)

        return dict(own=own, prev=lambda: qk(prev) + tt_ref[g, 1], far=qk,
                    c_far=cfar_ref[g], w=lambda kt: ones_row,
                    pv=lambda kt: (lambda pr: _dot(vst_sc[kt, g], pr)))

    yield [sel_stream(g) for g in range(NSA_GROUPS)]

    for g in range(NSA_GROUPS):
        o = og_sc[g] + gate(g, 1) * _normalize(acc_sc[g])
        for j, h in enumerate(group_heads[g]):
            ot_sc[h * HEAD_DIM:(h + 1) * HEAD_DIM, :] = o[:, j * TILE:(j + 1) * TILE]

    o_ref[...] = ot_sc[...].T.astype(BF16)


def _moba_steps(c, qm_ref, km_ref, vm_ref, qx_ref, mk_ref, mv_ref, tt_ref, cfar_ref, om_ref, ox_ref,
                vmt_sc, mvt_sc, kmean_sc, qbd_sc, sel_sc, acc_sc, ot_sc, slot):
    nt = km_ref.shape[1]
    hsls = [slice(h * HEAD_DIM, (h + 1) * HEAD_DIM) for h in range(MOBA_HEADS)]

    @pl.when(c == 0)
    def _():
        kmean_sc[...] = jnp.zeros(kmean_sc.shape, F32)
        for n in range(nt):
            kmean_sc[n:n + 1, :] = jnp.sum(km_ref[0, n].astype(F32), axis=0, keepdims=True) * (1.0 / MOBA_BLOCK)
            vt = vm_ref[0, n].astype(F32).T
            for h in range(MOBA_HEADS):
                _store_v_aug(vmt_sc, (n, h), vt[hsls[h]])
        mvt = mv_ref[0].astype(F32).T
        for h in range(MOBA_HEADS):
            _store_v_aug(mvt_sc, h, mvt[hsls[h]])

    row_head = lax.broadcasted_iota(jnp.int32, (MOBA_W, TILE), 0) // HEAD_DIM

    def block_diag(q_ref):
        q_t = q_ref[...].astype(F32).T
        return _lane_cat([jnp.where(row_head == h, q_t, 0.0) for h in range(MOBA_HEADS)]).astype(BF16)

    def per_head_pv(vts):
        return lambda pr: _lane_cat([_dot(vts(h), pr[:, h * TILE:(h + 1) * TILE]) for h in range(MOBA_HEADS)])

    def store_heads(o_t, out_ref):
        for h in range(MOBA_HEADS):
            ot_sc[hsls[h], :] = o_t[:, h * TILE:(h + 1) * TILE]
        out_ref[...] = ot_sc[...].T.astype(BF16)

    qbd = block_diag(qm_ref)
    km_hi, km_lo = _split_bf16(kmean_sc[...])
    gate = _dot(km_hi, qbd) + _dot(km_lo, qbd)
    blk = lax.broadcasted_iota(jnp.int32, gate.shape, 0)
    score = jnp.where(blk < c, gate, NEG_INF * Q_SCALE)
    rank = _rank_before(score, nt)
    sel_sc[...] = jnp.where(rank < MOBA_TOPK, jnp.where(score > NEG_INF * Q_SCALE / 2, 1.0, 0.0), 0.0)

    qbd_sc[...] = qbd
    qk = lambda n: _dot(km_ref[0, n], qbd_sc[...])
    def own_pv(lo, hi):
        def pv(pr):
            width = pr.shape[1] // MOBA_HEADS
            return _lane_cat([_dot(vmt_sc[c, h, :, lo:hi], pr[:, h * width:(h + 1) * width])
                              for h in range(MOBA_HEADS)])
        return pv

    def own():
        return _triangle_tile(lambda lo, hi: km_ref[0, c, lo:hi, :], qbd_sc[...], lambda lo, hi: tt_ref[0, lo:hi, :],
                              own_pv(0, HALF), own_pv(HALF, TILE), causal=True)

    stream = dict(own=own, prev=lambda: qk(jnp.maximum(c - 1, 0)) + tt_ref[1], far=qk,
                  c_far=cfar_ref[...], w=lambda n: sel_sc[pl.ds(n, 1), :],
                  pv=lambda n: per_head_pv(lambda h: vmt_sc[n, h]))
    s = _dot(mk_ref[0], block_diag(qx_ref))
    store_heads(_normalize(_softmax_av([s], [per_head_pv(lambda h: mvt_sc[h])])), ox_ref)

    yield [stream]
    store_heads(_normalize(acc_sc[slot]), om_ref)


N_NSA_IN, N_MOBA_IN = 13, 8
N_NSA_SCRATCH, N_MOBA_SCRATCH = 4, 5
N_STREAMS = NSA_GROUPS + 1


def _attention_kernel(*refs):
    qi = pl.program_id(1)
    nsa_in, refs = refs[:N_NSA_IN], refs[N_NSA_IN:]
    moba_in, refs = refs[:N_MOBA_IN], refs[N_MOBA_IN:]
    (o_nsa, o_moba, o_mem), refs = refs[:3], refs[3:]
    nsa_sc, refs = refs[:N_NSA_SCRATCH], refs[N_NSA_SCRATCH:]
    moba_sc, refs = refs[:N_MOBA_SCRATCH], refs[N_MOBA_SCRATCH:]
    s_sc, m_sc, acc_sc, otn_sc, otm_sc = refs
    nsa = _nsa_steps(qi, *nsa_in, o_nsa, *nsa_sc, acc_sc, otn_sc)
    moba = _moba_steps(qi, *moba_in, o_moba, o_mem, *moba_sc, acc_sc, otm_sc, NSA_GROUPS)
    streams = next(nsa) + next(moba)
    _flash_pipelined(qi, streams, s_sc, m_sc, acc_sc)
    for steps in (nsa, moba):
        for _ in steps:
            pass


def _attention(b, s, qn, gn, kc, vct, ks, vs, kw, vw, bias_cmp, t_nsa, t_win, c_far_nsa, ovt,
               qm, km, vm, qx, mk, mv, t_moba, c_far_moba):
    nt = s // TILE
    mem_len = mk.shape[0] // b
    assert MOBA_TOPK <= nt - 1 and nt <= BF16_ROWS
    n_lanes = NSA_HPG * TILE
    assert MOBA_HEADS * TILE == n_lanes
    row_spec = lambda w: pl.BlockSpec((TILE, w), lambda i, j: (i * nt + j, 0))
    seq_spec = lambda w: pl.BlockSpec((1, nt, TILE, w), lambda i, j: (i, 0, 0, 0))
    per_batch = lambda rows, w: pl.BlockSpec((1, rows, w), lambda i, j: (i, 0, 0))
    tiles = lambda a: a.reshape(b, nt, TILE, a.shape[-1])
    nsa_in = [(qn, row_spec(NSA_Q_W)), (gn, row_spec(GATE_PAD)),
              (kc, per_batch(N_CMP_PAD, NSA_KV_W)), (vct, per_batch(N_CMP_PAD, NSA_KV_W)),
              (tiles(ks), seq_spec(KS_AUG_W)), (tiles(vs), seq_spec(NSA_KV_W)),
              (tiles(kw), seq_spec(NSA_KV_W)), (tiles(vw), seq_spec(NSA_KV_W)),
              (bias_cmp, _const_spec(bias_cmp.shape)), (t_nsa, _const_spec(t_nsa.shape)),
              (t_win, _const_spec(t_win.shape)), (c_far_nsa, _const_spec(c_far_nsa.shape)),
              (ovt, _const_spec(ovt.shape))]
    moba_in = [(qm, row_spec(MOBA_W)), (tiles(km), seq_spec(MOBA_W)), (tiles(vm), seq_spec(MOBA_W)),
               (qx, row_spec(MEM_W)),
               (mk.reshape(b, mem_len, MEM_W), per_batch(mem_len, MEM_W)),
               (mv.reshape(b, mem_len, MEM_W), per_batch(mem_len, MEM_W)),
               (t_moba, _const_spec(t_moba.shape)), (c_far_moba, _const_spec(c_far_moba.shape))]
    assert len(nsa_in) == N_NSA_IN and len(moba_in) == N_MOBA_IN
    nsa_scratch = [pltpu.VMEM((nt, NSA_GROUPS, V_AUG, TILE), BF16),
                   pltpu.VMEM((nt, NSA_GROUPS, V_AUG, TILE), BF16),
                   pltpu.VMEM((NSA_GROUPS, 2 * HEAD_DIM, n_lanes), BF16),
                   pltpu.VMEM((NSA_GROUPS, HEAD_DIM, n_lanes), F32)]
    moba_scratch = [pltpu.VMEM((nt, MOBA_HEADS, V_AUG, TILE), BF16),
                    pltpu.VMEM((MEM_HEADS, V_AUG, mem_len), BF16),
                    pltpu.VMEM((BF16_ROWS, MOBA_W), F32),
                    pltpu.VMEM((MOBA_W, n_lanes), BF16),
                    pltpu.VMEM((BF16_ROWS, n_lanes), F32)]
    assert len(nsa_scratch) == N_NSA_SCRATCH and len(moba_scratch) == N_MOBA_SCRATCH
    shared_scratch = [pltpu.VMEM((N_STREAMS, TILE, n_lanes), F32),
                      pltpu.VMEM((N_STREAMS, 1, n_lanes), F32),
                      pltpu.VMEM((N_STREAMS, V_AUG, n_lanes), F32),
                      pltpu.VMEM((NSA_Q_W, TILE), F32), pltpu.VMEM((MOBA_W, TILE), F32)]
    inputs = nsa_in + moba_in
    return pl.pallas_call(
        _attention_kernel,
        grid=(b, nt),
        in_specs=[spec for _, spec in inputs],
        out_specs=[row_spec(NSA_Q_W), row_spec(MOBA_W), row_spec(MEM_W)],
        out_shape=[jax.ShapeDtypeStruct((b * s, w), BF16) for w in (NSA_Q_W, MOBA_W, MEM_W)],
        scratch_shapes=nsa_scratch + moba_scratch + shared_scratch,
        compiler_params=_params(("arbitrary", "arbitrary")),
        name="attention",
    )(*[a for a, _ in inputs])


def _mix_kernel(x_ref, on_ref, om_ref, ox_ref, g_pre_ref, g_post_ref, wg_ref, wn_ref, wm_ref, wx_ref,
                wo_ref, o_ref):
    x = x_ref[...]
    h = _rms(x, g_pre_ref[...]).astype(BF16)
    merged = jax.nn.sigmoid(_dot(h, wg_ref[:, :D_MODEL])) * _dot(on_ref[...], wn_ref[...])
    merged = merged + jax.nn.sigmoid(_dot(h, wg_ref[:, D_MODEL:2 * D_MODEL])) * _dot(om_ref[...], wm_ref[...])
    merged = merged + jax.nn.sigmoid(_dot(h, wg_ref[:, 2 * D_MODEL:])) * _dot(ox_ref[...], wx_ref[...])
    y = _dot(merged.astype(BF16), wo_ref[...])
    o_ref[...] = x + _rms(y, g_post_ref[...])


def _mix(x2, o_nsa, o_moba, o_mem, g_pre, g_post, w_gates, w_nsa_o, w_moba_o, w_mem_o, w_mix_out, tm=512):
    m = x2.shape[0]
    row = lambda w: pl.BlockSpec((tm, w), lambda i: (i, 0))
    return pl.pallas_call(
        _mix_kernel,
        grid=(m // tm,),
        in_specs=[row(D_MODEL), row(NSA_Q_W), row(MOBA_W), row(MEM_W),
                  _const_spec((1, D_MODEL)), _const_spec((1, D_MODEL)),
                  _const_spec(w_gates.shape), _const_spec(w_nsa_o.shape), _const_spec(w_moba_o.shape),
                  _const_spec(w_mem_o.shape), _const_spec(w_mix_out.shape)],
        out_specs=row(D_MODEL),
        out_shape=jax.ShapeDtypeStruct((m, D_MODEL), F32),
        compiler_params=_params(("parallel",)),
        name="mix",
    )(x2, o_nsa, o_moba, o_mem, g_pre, g_post, w_gates, w_nsa_o, w_moba_o, w_mem_o, w_mix_out)


FFN_CHUNK = 256


def _ffn_kernel(x_ref, g_pre_ref, g_post_ref, wg_ref, wu_ref, wd_ref, o_ref, a_sc):
    x = x_ref[...]
    h = _rms(x, g_pre_ref[...]).astype(BF16)
    d_ff = wg_ref.shape[1]
    for j in range(d_ff // FFN_CHUNK):
        sl = slice(j * FFN_CHUNK, (j + 1) * FFN_CHUNK)
        a_sc[:, sl] = (jax.nn.silu(_dot(h, wg_ref[:, sl])) * _dot(h, wu_ref[:, sl])).astype(BF16)
    f = _dot(a_sc[...], wd_ref[...])
    o_ref[...] = x + _rms(f, g_post_ref[...])


def _ffn(x2, g_pre, g_post, wg, wu, wd, tm=512):
    m = x2.shape[0]
    d_ff = wg.shape[1]
    return pl.pallas_call(
        _ffn_kernel,
        grid=(m // tm,),
        in_specs=[pl.BlockSpec((tm, D_MODEL), lambda i: (i, 0)),
                  _const_spec((1, D_MODEL)), _const_spec((1, D_MODEL)),
                  _const_spec(wg.shape), _const_spec(wu.shape), _const_spec(wd.shape)],
        out_specs=pl.BlockSpec((tm, D_MODEL), lambda i: (i, 0)),
        out_shape=jax.ShapeDtypeStruct((m, D_MODEL), F32),
        scratch_shapes=[pltpu.VMEM((tm, d_ff), BF16)],
        compiler_params=_params(("parallel",)),
        name="ffn",
    )(x2, g_pre, g_post, wg, wu, wd)


def kernel(x, mem, rel_bias, pre_mix_g, mem_norm_g, post_mix_g, w_in, cmp_pos_k, cmp_w1_k, cmp_w2_k, cmp_pos_v, cmp_w1_v, cmp_w2_v, w_mem_kv, w_nsa_o, w_moba_o, w_mem_o, w_mix_out, pre_ffn_g, post_ffn_g, w_ffn_gate, w_ffn_up, w_ffn_down):
    b, s, d_model = x.shape
    depth = w_in.shape[0]
    assert d_model == D_MODEL and s % TILE == 0 and TILE == MOBA_BLOCK == WINDOW
    assert (s - CMP_LEN) // CMP_STRIDE + 1 < N_CMP_PAD and (s // SEL_BLOCK) % SUBLANES == 0 and s // SEL_BLOCK <= HEAD_DIM
    assert w_in.shape[2] == ATT_W + 3 * D_MODEL and rel_bias.shape == (REL_BUCKETS, N_BIAS_HEADS)

    tile_idx, win_idx, cmp_idx = _bucket_tables(s)
    rel_bias = rel_bias.astype(F32)
    t_nsa = _expand(tile_idx, rel_bias, 0, NSA_HEADS, NSA_HPG).reshape(NSA_GROUPS, 2, TILE, NSA_HPG * TILE)
    t_moba = _expand(tile_idx, rel_bias, NSA_HEADS, MOBA_HEADS, MOBA_HEADS).reshape(2, TILE, MOBA_HEADS * TILE)
    t_win = _expand(win_idx, rel_bias, 0, NSA_HEADS, NSA_HPG)
    b_cmp = _expand(cmp_idx, rel_bias, 0, NSA_HEADS)
    c_far = jnp.repeat(rel_bias[REL_BUCKETS - 1] * LOG2E, TILE)
    c_far_nsa = c_far[:NSA_HEADS * TILE].reshape(NSA_GROUPS, 1, NSA_HPG * TILE)
    c_far_moba = c_far[NSA_HEADS * TILE:].reshape(1, MOBA_HEADS * TILE)
    ovt = _overlap_table(s)
    sel_cols = np.zeros((s, HEAD_DIM), np.float32)
    sel_cols[np.arange(s), np.arange(s) // SEL_BLOCK] = 1.0
    sel_cols = jnp.asarray(sel_cols, BF16)
    gate_lo = NSA_Q_W + 6 * NSA_KV_W
    rows_per_chunk = CMP_STRIDE * NSA_KV_W

    x2 = x.reshape(b * s, D_MODEL)
    mem2 = mem.reshape(-1, D_MODEL)
    for l in range(depth):
        w_att = jnp.concatenate(
            [w_in[l, :, :gate_lo + NSA_GATE_W],
             jnp.zeros((D_MODEL, GATE_PAD - NSA_GATE_W), w_in.dtype),
             w_in[l, :, gate_lo + NSA_GATE_W:ATT_W]], axis=1).astype(BF16)
        w_gates = w_in[l, :, ATT_W:].astype(BF16)
        row = lambda v: v[l].reshape(1, D_MODEL)

        qn, kc_raw, vc_raw, ks, vs, kw, vw, gn, qm, km, vm, qx = _inproj(x2, row(pre_mix_g), w_att, sel_cols)

        pk, w1k = _compress_weights(cmp_pos_k[l], cmp_w1_k[l])
        pv, w1v = _compress_weights(cmp_pos_v[l], cmp_w1_v[l])
        kc, vct = _compress(kc_raw.reshape(b, s // CMP_STRIDE, rows_per_chunk),
                            vc_raw.reshape(b, s // CMP_STRIDE, rows_per_chunk),
                            pk, pv, w1k, w1v, cmp_w2_k[l].astype(BF16), cmp_w2_v[l].astype(BF16))

        mk, mv = _memkv(mem2, row(mem_norm_g), w_mem_kv[l].astype(BF16))

        o_nsa, o_moba, o_mem = _attention(b, s, qn, gn, kc, vct, ks, vs, kw, vw, b_cmp, t_nsa, t_win, c_far_nsa, ovt,
                                          qm, km, vm, qx, mk, mv, t_moba, c_far_moba)

        x2 = _mix(x2, o_nsa, o_moba, o_mem, row(pre_mix_g), row(post_mix_g), w_gates,
                  w_nsa_o[l].astype(BF16), w_moba_o[l].astype(BF16), w_mem_o[l].astype(BF16),
                  w_mix_out[l].astype(BF16))
        x2 = _ffn(x2, row(pre_ffn_g), row(post_ffn_g), w_ffn_gate[l].astype(BF16),
                  w_ffn_up[l].astype(BF16), w_ffn_down[l].astype(BF16))
    return x2.reshape(b, s, D_MODEL)
```

```python
import functools
import math

import numpy as np
import jax
import jax.numpy as jnp
from jax import lax
from jax.experimental import pallas as pl
from jax.experimental.pallas import tpu as pltpu

F32 = jnp.float32
BF16 = jnp.bfloat16

D_MODEL = 1024
HEAD_DIM = 64
SCALE = HEAD_DIM ** -0.5
LOG2E = math.log2(math.e)
Q_SCALE = SCALE * LOG2E
NSA_HEADS = 8
NSA_GROUPS = 2
NSA_HPG = NSA_HEADS // NSA_GROUPS
CMP_LEN = 32
CMP_STRIDE = 16
CMP_HIDDEN = 128
SEL_BLOCK = 64
SEL_TOPN = 8
WINDOW = 256
MOBA_HEADS = 4
MOBA_BLOCK = 256
MOBA_TOPK = 3
MEM_HEADS = 4
REL_BUCKETS = 32
REL_MAX_DIST = 128
N_BIAS_HEADS = NSA_HEADS + MOBA_HEADS
RMS_EPS = 1e-6
NEG_INF = -1e30
FORCE_SCORE = 1e4

NSA_Q_W = NSA_HEADS * HEAD_DIM
NSA_KV_W = NSA_GROUPS * HEAD_DIM
NSA_GATE_W = NSA_HEADS * 3
MOBA_W = MOBA_HEADS * HEAD_DIM
MEM_W = MEM_HEADS * HEAD_DIM
ATT_W = NSA_Q_W + 6 * NSA_KV_W + NSA_GATE_W + 3 * MOBA_W + MEM_W
LANES = 128
SUBLANES = 8
BF16_ROWS = 16
MXU_COLS = 256
GATE_PAD = LANES
TILE = 256
HALF = TILE // 2
N_CMP_PAD = 128
V_AUG = HEAD_DIM + BF16_ROWS
MASKED_BUCKET = REL_BUCKETS
VMEM_LIMIT = 56 * 1024 * 1024


def _dot(a, b):
    return jnp.dot(a, b, preferred_element_type=F32)


def _split_bf16(x):
    hi = x.astype(BF16)
    lo = (x - hi.astype(F32)).astype(BF16)
    return hi, lo


def _rms(x, g):
    return x * lax.rsqrt(jnp.mean(x * x, axis=-1, keepdims=True) + RMS_EPS) * g


def _params(sem):
    return pltpu.CompilerParams(dimension_semantics=sem, vmem_limit_bytes=VMEM_LIMIT)


def _const_spec(shape):
    nd = len(shape)
    return pl.BlockSpec(shape, lambda *_: (0,) * nd, pipeline_mode=pl.Buffered(1))


_INPROJ_OUTS = (
    ("qn", NSA_Q_W, BF16, True),
    ("kc", NSA_KV_W, F32, False), ("vc", NSA_KV_W, F32, False),
    ("ks", NSA_KV_W, BF16, False), ("vs", NSA_KV_W, BF16, False),
    ("kw", NSA_KV_W, BF16, False), ("vw", NSA_KV_W, BF16, False),
    ("gn", GATE_PAD, F32, False),
    ("qm", MOBA_W, BF16, True), ("km", MOBA_W, BF16, False), ("vm", MOBA_W, BF16, False),
    ("qx", MEM_W, BF16, True),
)
_INPROJ_W = sum(o[1] for o in _INPROJ_OUTS)
_INPROJ_CHUNKED = ("kc", "vc")
KS_AUG_W = NSA_GROUPS * 2 * HEAD_DIM


def _inproj_out_width(name, width):
    return KS_AUG_W if name == "ks" else width


def _inproj_kernel(x_ref, g_ref, w_ref, e_ref, *refs):
    out_refs, rows_sc = refs[:-1], refs[-1]
    h = _rms(x_ref[...], g_ref[...]).astype(BF16)
    runs, lo = [], 0
    for out in zip(_INPROJ_OUTS, out_refs):
        if runs and runs[-1][1] < MXU_COLS:
            runs[-1][0].append(out)
            runs[-1][1] += out[0][1]
        else:
            runs.append([[out], out[0][1], lo])
        lo += out[0][1]
    for outs, run_width, run_lo in runs:
        y_run = _dot(h, w_ref[:, run_lo:run_lo + run_width])
        lo = 0
        for (name, width, dtype, scaled), o_ref in outs:
            y = y_run[:, lo:lo + width]
            if scaled:
                y = y * Q_SCALE
            if name == "gn":
                y = jax.nn.sigmoid(y)
            y = y.astype(dtype)
            if name == "ks":
                e = e_ref[...]
                y = _lane_cat([y[:, :HEAD_DIM], e, y[:, HEAD_DIM:], e])
            if name in _INPROJ_CHUNKED:
                rows_sc[...] = y
                for j in range(CMP_STRIDE):
                    o_ref[:, j * width:(j + 1) * width] = rows_sc[pl.ds(j, o_ref.shape[0], stride=CMP_STRIDE), :]
            else:
                o_ref[...] = y
            lo += width


def _inproj(x2, g, w, e_cols, tm=512):
    m = x2.shape[0]
    tiles_per_seq = e_cols.shape[0] // tm
    outs = []
    for name, width, dtype, _ in _INPROJ_OUTS:
        if name in _INPROJ_CHUNKED:
            outs.append((CMP_STRIDE, CMP_STRIDE * width, dtype))
        else:
            outs.append((1, _inproj_out_width(name, width), dtype))
    return pl.pallas_call(
        _inproj_kernel,
        grid=(m // tm,),
        in_specs=[pl.BlockSpec((tm, D_MODEL), lambda i: (i, 0)),
                  _const_spec((1, D_MODEL)),
                  _const_spec((D_MODEL, _INPROJ_W)),
                  pl.BlockSpec((tm, HEAD_DIM), lambda i: (i % tiles_per_seq, 0))],
        out_specs=[pl.BlockSpec((tm // r, w), lambda i: (i, 0)) for r, w, _ in outs],
        out_shape=[jax.ShapeDtypeStruct((m // r, w), d) for r, w, d in outs],
        scratch_shapes=[pltpu.VMEM((tm, NSA_KV_W), F32)],
        compiler_params=_params(("parallel",)),
        name="inproj",
    )(x2, g, w, e_cols)


def _compress_kernel(rk_ref, rv_ref, pk_ref, pv_ref, w1k_ref, w1v_ref, w2k_ref, w2v_ref, kc_ref, vc_ref):
    def one(r_ref, p_ref, w1_ref, w2_ref):
        r = r_ref[0]
        top = _dot((r + p_ref[0:1, :]).astype(BF16), w1_ref[0])
        bot = _dot((r + p_ref[1:2, :]).astype(BF16), w1_ref[1])
        hid = top + pltpu.roll(bot, N_CMP_PAD - 1, 0)
        act = jax.nn.gelu(hid).astype(BF16)
        return jnp.concatenate(
            [_dot(act[:, g * CMP_HIDDEN:(g + 1) * CMP_HIDDEN], w2_ref[...]) for g in range(NSA_GROUPS)], axis=1)

    kc_ref[0] = one(rk_ref, pk_ref, w1k_ref, w2k_ref).astype(BF16)
    vc_ref[0] = one(rv_ref, pv_ref, w1v_ref, w2v_ref).T.astype(BF16)


def _compress(rk, rv, pk, pv, w1k, w1v, w2k, w2v):
    b = rk.shape[0]
    rw = rk.shape[2]
    r_spec = pl.BlockSpec((1, N_CMP_PAD, rw), lambda i: (i, 0, 0))
    o_spec = pl.BlockSpec((1, N_CMP_PAD, NSA_KV_W), lambda i: (i, 0, 0))
    return pl.pallas_call(
        _compress_kernel,
        grid=(b,),
        in_specs=[r_spec, r_spec, _const_spec(pk.shape), _const_spec(pv.shape),
                  _const_spec(w1k.shape), _const_spec(w1v.shape),
                  _const_spec(w2k.shape), _const_spec(w2v.shape)],
        out_specs=[o_spec, o_spec],
        out_shape=[jax.ShapeDtypeStruct((b, N_CMP_PAD, NSA_KV_W), BF16)] * 2,
        compiler_params=_params(("parallel",)),
        name="compress",
    )(rk, rv, pk, pv, w1k, w1v, w2k, w2v)


def _compress_weights(pos, w1):
    half = CMP_LEN // 2
    p = pos.reshape(2, half, 1, HEAD_DIM)
    p = jnp.broadcast_to(p, (2, half, NSA_GROUPS, HEAD_DIM)).reshape(2, half * NSA_KV_W)
    w = w1.reshape(2, half, HEAD_DIM, CMP_HIDDEN)
    eye = jnp.eye(NSA_GROUPS, dtype=w1.dtype)
    wbd = jnp.einsum("ajdm,gk->ajgdkm", w, eye).reshape(2, half * NSA_KV_W, NSA_GROUPS * CMP_HIDDEN)
    return p.astype(F32), wbd.astype(BF16)


def _memkv_kernel(m_ref, g_ref, w_ref, k_ref, v_ref):
    h = _rms(m_ref[...], g_ref[...]).astype(BF16)
    k_ref[...] = _dot(h, w_ref[:, :MEM_W]).astype(BF16)
    v_ref[...] = _dot(h, w_ref[:, MEM_W:]).astype(BF16)


def _memkv(mem2, g, w, tm=256):
    m = mem2.shape[0]
    o_spec = pl.BlockSpec((tm, MEM_W), lambda i: (i, 0))
    return pl.pallas_call(
        _memkv_kernel,
        grid=(m // tm,),
        in_specs=[pl.BlockSpec((tm, D_MODEL), lambda i: (i, 0)), _const_spec((1, D_MODEL)),
                  _const_spec((D_MODEL, 2 * MEM_W))],
        out_specs=[o_spec, o_spec],
        out_shape=[jax.ShapeDtypeStruct((m, MEM_W), BF16)] * 2,
        compiler_params=_params(("parallel",)),
        name="memkv",
    )(mem2, g, w)


def _expand_kernel(idx_ref, bias_ref, o_ref, *, head0, n_heads, heads_per_group):
    rows, cols = idx_ref.shape

    def body(i, carry):
        r = pl.multiple_of(i * SUBLANES, SUBLANES)
        for c0 in range(0, cols, TILE):
            idx = idx_ref[pl.ds(r, SUBLANES), c0:c0 + TILE]
            out = [jnp.full(idx.shape, NEG_INF, F32)] * n_heads
            for bkt in range(REL_BUCKETS):
                hit = idx == bkt
                out = [jnp.where(hit, bias_ref[bkt, head0 + h], out[h]) for h in range(n_heads)]
            for h in range(n_heads):
                col = (h % heads_per_group) * cols + c0
                o_ref[h // heads_per_group, pl.ds(r, SUBLANES), col:col + TILE] = out[h] * LOG2E
        return carry

    lax.fori_loop(0, rows // SUBLANES, body, 0)


def _expand(idx, rel_bias, head0, n_heads, heads_per_group=1):
    rows, cols = idx.shape
    return pl.pallas_call(
        functools.partial(_expand_kernel, head0=head0, n_heads=n_heads, heads_per_group=heads_per_group),
        in_specs=[pl.BlockSpec(memory_space=pltpu.VMEM), pl.BlockSpec(memory_space=pltpu.SMEM)],
        out_specs=pl.BlockSpec(memory_space=pltpu.VMEM),
        out_shape=jax.ShapeDtypeStruct((n_heads // heads_per_group, rows, heads_per_group * cols), F32),
        compiler_params=pltpu.CompilerParams(vmem_limit_bytes=VMEM_LIMIT),
        name="bias_expand",
    )(idx, rel_bias)


def _t5_bucket_np(dist):
    dist = np.maximum(dist, 0)
    max_exact = REL_BUCKETS // 2
    logd = np.log(np.maximum(dist, 1).astype(np.float32) / max_exact) / math.log(REL_MAX_DIST / max_exact)
    large = np.minimum(max_exact + (logd * (REL_BUCKETS - max_exact)).astype(np.int32), REL_BUCKETS - 1)
    return np.where(dist < max_exact, dist, large).astype(np.int32)


def _bucket_tables(s):
    j = np.arange(TILE)[:, None]
    i = np.arange(TILE)[None, :]
    assert TILE + 1 >= REL_MAX_DIST
    tiles = []
    for d in range(2):
        dist = d * TILE + i - j
        tiles.append(np.where(dist >= 0, _t5_bucket_np(dist), MASKED_BUCKET))
    dist1 = TILE + i - j
    win = np.where(dist1 < WINDOW, _t5_bucket_np(dist1), MASKED_BUCKET)
    n_cmp = (s - CMP_LEN) // CMP_STRIDE + 1
    c = np.arange(N_CMP_PAD)[:, None]
    dist_c = np.arange(s)[None, :] - (c * CMP_STRIDE + CMP_LEN - 1)
    cmp_idx = np.where((dist_c >= 0) & (c < n_cmp), _t5_bucket_np(dist_c), MASKED_BUCKET)
    as_i32 = lambda a: jnp.asarray(a.astype(np.int32))
    return as_i32(np.concatenate(tiles, axis=0)), as_i32(win), as_i32(cmp_idx)


def _overlap_table(s):
    n_cmp = (s - CMP_LEN) // CMP_STRIDE + 1
    n_sel = s // SEL_BLOCK
    cs = np.arange(n_cmp) * CMP_STRIDE
    ss = np.arange(n_sel) * SEL_BLOCK
    ov = np.clip(np.minimum(cs[:, None] + CMP_LEN, ss[None, :] + SEL_BLOCK)
                 - np.maximum(cs[:, None], ss[None, :]), 0, None).astype(np.float32) / CMP_LEN
    ovt = np.zeros((n_sel, N_CMP_PAD), np.float32)
    ovt[:, :n_cmp] = ov.T
    return jnp.asarray(ovt, BF16)


def _store_v_aug(vt_sc, idx, vt):
    ones = jnp.ones((BF16_ROWS, vt.shape[1]), BF16)
    vt_sc[idx] = jnp.concatenate([vt.astype(BF16), ones], axis=0)


def _lane_cat(xs):
    return jnp.concatenate(xs, axis=1)


def _query_halves(x):
    n = x.shape[-1] // TILE
    first = _lane_cat([x[:, k * TILE:k * TILE + HALF] for k in range(n)])
    second = _lane_cat([x[:, k * TILE + HALF:(k + 1) * TILE] for k in range(n)])
    return first, second


def _join_query_halves(first, second):
    n = first.shape[-1] // HALF
    return _lane_cat([part for k in range(n)
                      for part in (first[:, k * HALF:(k + 1) * HALF], second[:, k * HALF:(k + 1) * HALF])])


def _triangle_tile(k, q, table, pv_lo, pv_hi, causal):
    q_first, q_second = _query_halves(q)
    lo, hi = (0, HALF), (HALF, TILE)
    if causal:
        s_wide = _dot(k(*lo), q) + table(*lo)
        s_narrow = _dot(k(*hi), q_second) + _query_halves(table(*hi))[1]
        pv_wide, pv_narrow = pv_lo, pv_hi
    else:
        s_wide = _dot(k(*hi), q) + table(*hi)
        s_narrow = _dot(k(*lo), q_first) + _query_halves(table(*lo))[0]
        pv_wide, pv_narrow = pv_hi, pv_lo
    mw_first, mw_second = _query_halves(jnp.max(s_wide, axis=0, keepdims=True))
    m_narrow = jnp.max(s_narrow, axis=0, keepdims=True)
    if causal:
        m_narrow = jnp.maximum(m_narrow, mw_second)
        m = _join_query_halves(mw_first, m_narrow)
    else:
        m_narrow = jnp.maximum(m_narrow, mw_first)
        m = _join_query_halves(m_narrow, mw_second)
    aw_first, aw_second = _query_halves(pv_wide(jnp.exp2(s_wide - m).astype(BF16)))
    a_narrow = pv_narrow(jnp.exp2(s_narrow - m_narrow).astype(BF16))
    if causal:
        return m, _join_query_halves(aw_first, aw_second + a_narrow)
    return m, _join_query_halves(aw_first + a_narrow, aw_second)


def _flash_pipelined(own, streams, s_sc, m_ref, acc_ref):
    has_prev = jnp.where(own > 0, 1.0, 0.0).astype(F32)
    prev = jnp.maximum(own - 1, 0)
    n_far = jnp.maximum(own - 1, 0)

    def absorb(g, s, kt, c_row, w_row):
        u = jnp.max(s, axis=0, keepdims=True) + c_row
        m_old = m_ref[g]
        m_new = jnp.maximum(m_old, jnp.where(w_row > 0.0, u, NEG_INF))
        alpha = jnp.exp2(m_old - m_new)
        shift = jnp.maximum(m_new, u) - c_row
        p = jnp.exp2(s - shift).astype(BF16)
        acc_ref[g] = alpha * acc_ref[g] + w_row * streams[g]["pv"](kt)(p)
        m_ref[g] = m_new

    def absorb_slot(g, i):
        is_prev = i == 0
        kt = jnp.where(is_prev, prev, i - 1)
        c_row = jnp.where(is_prev, 0.0, streams[g]["c_far"])
        w_row = streams[g]["w"](kt) * jnp.where(is_prev, has_prev, 1.0)
        absorb(g, s_sc[g], kt, c_row, w_row)

    for g, stream in enumerate(streams):
        m_ref[g], acc_ref[g] = stream["own"]()
    for g, stream in enumerate(streams):
        s_sc[g] = stream["prev"]()

    def body(i, carry):
        nxt = [stream["far"](i) for stream in streams]
        for g in range(len(streams)):
            absorb_slot(g, i)
        for g in range(len(streams)):
            s_sc[g] = nxt[g]
        return carry

    lax.fori_loop(0, n_far, body, 0)
    for g in range(len(streams)):
        absorb_slot(g, n_far)


def _softmax_av(s_list, pv_list):
    m = s_list[0].max(axis=0, keepdims=True)
    for s in s_list[1:]:
        m = jnp.maximum(m, s.max(axis=0, keepdims=True))
    acc = None
    for s, pv in zip(s_list, pv_list):
        part = pv(jnp.exp2(s - m).astype(BF16))
        acc = part if acc is None else acc + part
    return acc


def _normalize(acc):
    return acc[:HEAD_DIM] / acc[HEAD_DIM:HEAD_DIM + 1]


def _rank_before(score, n_cand):
    ranks = []
    for r0 in range(0, score.shape[0], SUBLANES):
        tile = score[r0:r0 + SUBLANES]
        blk = lax.broadcasted_iota(jnp.int32, tile.shape, 0) + r0
        rank = jnp.zeros(tile.shape, F32)
        for m in range(n_cand):
            row = score[m:m + 1, :]
            if m < r0:
                before = jnp.where(row >= tile, 1.0, 0.0)
            elif m >= r0 + SUBLANES:
                before = jnp.where(row > tile, 1.0, 0.0)
            else:
                before = jnp.where(blk > m, jnp.where(row >= tile, 1.0, 0.0), jnp.where(row > tile, 1.0, 0.0))
            rank = rank + before
        ranks.append(rank)
    return jnp.concatenate(ranks, axis=0)


def _nsa_steps(qi, q_ref, gn_ref, kc_ref, vct_ref, ks_ref, vs_ref, kw_ref, vw_ref,
               bct_ref, tt_ref, twt_ref, cfar_ref, ovt_ref, o_ref,
               vst_sc, vwt_sc, qa_sc, og_sc, acc_sc, ot_sc):
    nt = ks_ref.shape[1]
    n_sel = ovt_ref.shape[0]

    @pl.when(qi == 0)
    def _():
        for kt in range(nt):
            vs_t = vs_ref[0, kt].astype(F32).T
            vw_t = vw_ref[0, kt].astype(F32).T
            for g in range(NSA_GROUPS):
                _store_v_aug(vst_sc, (kt, g), vs_t[g * HEAD_DIM:(g + 1) * HEAD_DIM])
                _store_v_aug(vwt_sc, (kt, g), vw_t[g * HEAD_DIM:(g + 1) * HEAD_DIM])

    pos = lax.broadcasted_iota(jnp.int32, (1, TILE), 1) + qi * TILE
    cur = pos // SEL_BLOCK
    has_cmp = pos >= CMP_LEN - 1
    blk = lax.broadcasted_iota(jnp.int32, (n_sel, TILE), 0)
    prev = jnp.maximum(qi - 1, 0)
    no_prev = jnp.where(qi == 0, NEG_INF, 0.0).astype(F32)
    gates = gn_ref[...].T

    gsls = [slice(g * HEAD_DIM, (g + 1) * HEAD_DIM) for g in range(NSA_GROUPS)]
    group_heads = [[g * NSA_HPG + j for j in range(NSA_HPG)] for g in range(NSA_GROUPS)]

    def gate(g, branch):
        return _lane_cat([gates[3 * h + branch:3 * h + branch + 1, :] for h in group_heads[g]])

    q_t = q_ref[...].astype(F32).T
    for g in range(NSA_GROUPS):
        heads = group_heads[g]
        q4 = _lane_cat([q_t[h * HEAD_DIM:(h + 1) * HEAD_DIM, :] for h in heads]).astype(BF16)
        qa_sc[g, 0:HEAD_DIM, :] = q4
        qa_sc[g, HEAD_DIM + n_sel:, :] = jnp.zeros((HEAD_DIM - n_sel, NSA_HPG * TILE), BF16)

        def v_half(vt_sc, kt, lo, hi):
            return lambda pr: _dot(vt_sc[kt, g, :, lo:hi], pr)

        m_own, acc_own = _triangle_tile(
            lambda lo, hi: kw_ref[0, qi, lo:hi, gsls[g]], q4, lambda lo, hi: tt_ref[g, 0, lo:hi, :],
            v_half(vwt_sc, qi, 0, HALF), v_half(vwt_sc, qi, HALF, TILE), causal=True)
        m_prev, acc_prev = _triangle_tile(
            lambda lo, hi: kw_ref[0, prev, lo:hi, gsls[g]], q4, lambda lo, hi: twt_ref[g, lo:hi, :] + no_prev,
            v_half(vwt_sc, prev, 0, HALF), v_half(vwt_sc, prev, HALF, TILE), causal=False)
        m_win = jnp.maximum(m_own, m_prev)
        acc_w = acc_own * jnp.exp2(m_own - m_win) + acc_prev * jnp.exp2(m_prev - m_win)
        o_win = gate(g, 2) * _normalize(acc_w)

        kc = kc_ref[0, :, gsls[g]]
        vct = vct_ref[0, gsls[g], :]
        q_cols = pl.ds(pl.multiple_of(qi * TILE, TILE), TILE)
        s = _dot(kc, q4) + _lane_cat([bct_ref[h, :, q_cols] for h in heads])
        e = jnp.exp2(s - jnp.max(s, axis=0, keepdims=True))
        p = jnp.where(_lane_cat([has_cmp] * NSA_HPG), e / jnp.sum(e, axis=0, keepdims=True), 0.0)
        psum = p[:, :TILE]
        for j in range(1, NSA_HPG):
            psum = psum + p[:, j * TILE:(j + 1) * TILE]
        og_sc[g] = gate(g, 0) * _dot(vct, p.astype(BF16)) + o_win

        p_hi, p_lo = _split_bf16(psum)
        imp = _dot(ovt_ref[...], p_hi) + _dot(ovt_ref[...], p_lo)
        forced = (blk == 0) | (blk == cur) | (blk == cur - 1)
        score = jnp.where(forced, FORCE_SCORE, jnp.where(blk <= cur, imp, NEG_INF))
        rank = _rank_before(score, n_sel)
        sel = jnp.where(rank < SEL_TOPN, jnp.where(score > NEG_INF / 2, 0.0, NEG_INF), NEG_INF)
        qa_sc[g, HEAD_DIM:HEAD_DIM + n_sel, :] = _lane_cat([sel.astype(BF16)] * NSA_HPG)

    ones_row = jnp.ones((1, NSA_HPG * TILE), F32)

    def sel_stream(g):
        def qk(kt):
            return _dot(ks_ref[0, kt, :, g * 2 * HEAD_DIM:(g + 1) * 2 * HEAD_DIM], qa_sc[g])

        def own():
            return _triangle_tile(
                lambda lo, hi: ks_ref[0, qi, lo:hi, g * 2 * HEAD_DIM:(g + 1) * 2 * HEAD_DIM], qa_sc[g],
                lambda lo, hi: tt_ref[g, 0, lo:hi, :],
                lambda pr: _dot(vst_sc[qi, g, :, 0:HALF], pr), lambda pr: _dot(vst_sc[qi, g, :, HALF:TILE], pr),
                causal=True)

        return dict(own=own, prev=lambda: qk(prev) + tt_ref[g, 1], far=qk,
                    c_far=cfar_ref[g], w=lambda kt: ones_row,
                    pv=lambda kt: (lambda pr: _dot(vst_sc[kt, g], pr)))

    yield [sel_stream(g) for g in range(NSA_GROUPS)]

    for g in range(NSA_GROUPS):
        o = og_sc[g] + gate(g, 1) * _normalize(acc_sc[g])
        for j, h in enumerate(group_heads[g]):
            ot_sc[h * HEAD_DIM:(h + 1) * HEAD_DIM, :] = o[:, j * TILE:(j + 1) * TILE]

    o_ref[...] = ot_sc[...].T.astype(BF16)


def _moba_steps(c, qm_ref, km_ref, vm_ref, qx_ref, mk_ref, mv_ref, tt_ref, cfar_ref, om_ref, ox_ref,
                vmt_sc, mvt_sc, kmean_sc, qbd_sc, sel_sc, acc_sc, ot_sc, slot):
    nt = km_ref.shape[1]
    hsls = [slice(h * HEAD_DIM, (h + 1) * HEAD_DIM) for h in range(MOBA_HEADS)]

    @pl.when(c == 0)
    def _():
        kmean_sc[...] = jnp.zeros(kmean_sc.shape, F32)
        for n in range(nt):
            kmean_sc[n:n + 1, :] = jnp.sum(km_ref[0, n].astype(F32), axis=0, keepdims=True) * (1.0 / MOBA_BLOCK)
            vt = vm_ref[0, n].astype(F32).T
            for h in range(MOBA_HEADS):
                _store_v_aug(vmt_sc, (n, h), vt[hsls[h]])
        mvt = mv_ref[0].astype(F32).T
        for h in range(MOBA_HEADS):
            _store_v_aug(mvt_sc, h, mvt[hsls[h]])

    row_head = lax.broadcasted_iota(jnp.int32, (MOBA_W, TILE), 0) // HEAD_DIM

    def block_diag(q_ref):
        q_t = q_ref[...].astype(F32).T
        return _lane_cat([jnp.where(row_head == h, q_t, 0.0) for h in range(MOBA_HEADS)]).astype(BF16)

    def per_head_pv(vts):
        return lambda pr: _lane_cat([_dot(vts(h), pr[:, h * TILE:(h + 1) * TILE]) for h in range(MOBA_HEADS)])

    def store_heads(o_t, out_ref):
        for h in range(MOBA_HEADS):
            ot_sc[hsls[h], :] = o_t[:, h * TILE:(h + 1) * TILE]
        out_ref[...] = ot_sc[...].T.astype(BF16)

    qbd = block_diag(qm_ref)
    km_hi, km_lo = _split_bf16(kmean_sc[...])
    n_rows = -(-nt // SUBLANES) * SUBLANES
    gate = (_dot(km_hi, qbd) + _dot(km_lo, qbd))[:n_rows]
    blk = lax.broadcasted_iota(jnp.int32, gate.shape, 0)
    score = jnp.where(blk < c, gate, NEG_INF * Q_SCALE)
    rank = _rank_before(score, nt)
    sel_sc[0:n_rows, :] = jnp.where(rank < MOBA_TOPK, jnp.where(score > NEG_INF * Q_SCALE / 2, 1.0, 0.0), 0.0)

    qbd_sc[...] = qbd
    qk = lambda n: _dot(km_ref[0, n], qbd_sc[...])
    def own_pv(lo, hi):
        def pv(pr):
            width = pr.shape[1] // MOBA_HEADS
            return _lane_cat([_dot(vmt_sc[c, h, :, lo:hi], pr[:, h * width:(h + 1) * width])
                              for h in range(MOBA_HEADS)])
        return pv

    def own():
        return _triangle_tile(lambda lo, hi: km_ref[0, c, lo:hi, :], qbd_sc[...], lambda lo, hi: tt_ref[0, lo:hi, :],
                              own_pv(0, HALF), own_pv(HALF, TILE), causal=True)

    stream = dict(own=own, prev=lambda: qk(jnp.maximum(c - 1, 0)) + tt_ref[1], far=qk,
                  c_far=cfar_ref[...], w=lambda n: sel_sc[pl.ds(n, 1), :],
                  pv=lambda n: per_head_pv(lambda h: vmt_sc[n, h]))
    s = _dot(mk_ref[0], block_diag(qx_ref))
    store_heads(_normalize(_softmax_av([s], [per_head_pv(lambda h: mvt_sc[h])])), ox_ref)

    yield [stream]
    store_heads(_normalize(acc_sc[slot]), om_ref)


N_NSA_IN, N_MOBA_IN = 13, 8
N_NSA_SCRATCH, N_MOBA_SCRATCH = 4, 5
N_STREAMS = NSA_GROUPS + 1


def _attention_kernel(*refs):
    qi = pl.program_id(1)
    nsa_in, refs = refs[:N_NSA_IN], refs[N_NSA_IN:]
    moba_in, refs = refs[:N_MOBA_IN], refs[N_MOBA_IN:]
    (o_nsa, o_moba, o_mem), refs = refs[:3], refs[3:]
    nsa_sc, refs = refs[:N_NSA_SCRATCH], refs[N_NSA_SCRATCH:]
    moba_sc, refs = refs[:N_MOBA_SCRATCH], refs[N_MOBA_SCRATCH:]
    s_sc, m_sc, acc_sc, otn_sc, otm_sc = refs
    nsa = _nsa_steps(qi, *nsa_in, o_nsa, *nsa_sc, acc_sc, otn_sc)
    moba = _moba_steps(qi, *moba_in, o_moba, o_mem, *moba_sc, acc_sc, otm_sc, NSA_GROUPS)
    streams = next(nsa) + next(moba)
    _flash_pipelined(qi, streams, s_sc, m_sc, acc_sc)
    for steps in (nsa, moba):
        for _ in steps:
            pass


def _attention(b, s, qn, gn, kc, vct, ks, vs, kw, vw, bias_cmp, t_nsa, t_win, c_far_nsa, ovt,
               qm, km, vm, qx, mk, mv, t_moba, c_far_moba):
    nt = s // TILE
    mem_len = mk.shape[0] // b
    assert MOBA_TOPK <= nt - 1 and nt <= BF16_ROWS
    n_lanes = NSA_HPG * TILE
    assert MOBA_HEADS * TILE == n_lanes
    row_spec = lambda w: pl.BlockSpec((TILE, w), lambda i, j: (i * nt + j, 0))
    seq_spec = lambda w: pl.BlockSpec((1, nt, TILE, w), lambda i, j: (i, 0, 0, 0))
    per_batch = lambda rows, w: pl.BlockSpec((1, rows, w), lambda i, j: (i, 0, 0))
    tiles = lambda a: a.reshape(b, nt, TILE, a.shape[-1])
    nsa_in = [(qn, row_spec(NSA_Q_W)), (gn, row_spec(GATE_PAD)),
              (kc, per_batch(N_CMP_PAD, NSA_KV_W)), (vct, per_batch(N_CMP_PAD, NSA_KV_W)),
              (tiles(ks), seq_spec(KS_AUG_W)), (tiles(vs), seq_spec(NSA_KV_W)),
              (tiles(kw), seq_spec(NSA_KV_W)), (tiles(vw), seq_spec(NSA_KV_W)),
              (bias_cmp, _const_spec(bias_cmp.shape)), (t_nsa, _const_spec(t_nsa.shape)),
              (t_win, _const_spec(t_win.shape)), (c_far_nsa, _const_spec(c_far_nsa.shape)),
              (ovt, _const_spec(ovt.shape))]
    moba_in = [(qm, row_spec(MOBA_W)), (tiles(km), seq_spec(MOBA_W)), (tiles(vm), seq_spec(MOBA_W)),
               (qx, row_spec(MEM_W)),
               (mk.reshape(b, mem_len, MEM_W), per_batch(mem_len, MEM_W)),
               (mv.reshape(b, mem_len, MEM_W), per_batch(mem_len, MEM_W)),
               (t_moba, _const_spec(t_moba.shape)), (c_far_moba, _const_spec(c_far_moba.shape))]
    assert len(nsa_in) == N_NSA_IN and len(moba_in) == N_MOBA_IN
    nsa_scratch = [pltpu.VMEM((nt, NSA_GROUPS, V_AUG, TILE), BF16),
                   pltpu.VMEM((nt, NSA_GROUPS, V_AUG, TILE), BF16),
                   pltpu.VMEM((NSA_GROUPS, 2 * HEAD_DIM, n_lanes), BF16),
                   pltpu.VMEM((NSA_GROUPS, HEAD_DIM, n_lanes), F32)]
    moba_scratch = [pltpu.VMEM((nt, MOBA_HEADS, V_AUG, TILE), BF16),
                    pltpu.VMEM((MEM_HEADS, V_AUG, mem_len), BF16),
                    pltpu.VMEM((BF16_ROWS, MOBA_W), F32),
                    pltpu.VMEM((MOBA_W, n_lanes), BF16),
                    pltpu.VMEM((BF16_ROWS, n_lanes), F32)]
    assert len(nsa_scratch) == N_NSA_SCRATCH and len(moba_scratch) == N_MOBA_SCRATCH
    shared_scratch = [pltpu.VMEM((N_STREAMS, TILE, n_lanes), F32),
                      pltpu.VMEM((N_STREAMS, 1, n_lanes), F32),
                      pltpu.VMEM((N_STREAMS, V_AUG, n_lanes), F32),
                      pltpu.VMEM((NSA_Q_W, TILE), F32), pltpu.VMEM((MOBA_W, TILE), F32)]
    inputs = nsa_in + moba_in
    return pl.pallas_call(
        _attention_kernel,
        grid=(b, nt),
        in_specs=[spec for _, spec in inputs],
        out_specs=[row_spec(NSA_Q_W), row_spec(MOBA_W), row_spec(MEM_W)],
        out_shape=[jax.ShapeDtypeStruct((b * s, w), BF16) for w in (NSA_Q_W, MOBA_W, MEM_W)],
        scratch_shapes=nsa_scratch + moba_scratch + shared_scratch,
        compiler_params=_params(("arbitrary", "arbitrary")),
        name="attention",
    )(*[a for a, _ in inputs])


def _mix_kernel(x_ref, on_ref, om_ref, ox_ref, g_pre_ref, g_post_ref, wg_ref, wn_ref, wm_ref, wx_ref,
                wo_ref, o_ref):
    x = x_ref[...]
    h = _rms(x, g_pre_ref[...]).astype(BF16)
    merged = jax.nn.sigmoid(_dot(h, wg_ref[:, :D_MODEL])) * _dot(on_ref[...], wn_ref[...])
    merged = merged + jax.nn.sigmoid(_dot(h, wg_ref[:, D_MODEL:2 * D_MODEL])) * _dot(om_ref[...], wm_ref[...])
    merged = merged + jax.nn.sigmoid(_dot(h, wg_ref[:, 2 * D_MODEL:])) * _dot(ox_ref[...], wx_ref[...])
    y = _dot(merged.astype(BF16), wo_ref[...])
    o_ref[...] = x + _rms(y, g_post_ref[...])


def _mix(x2, o_nsa, o_moba, o_mem, g_pre, g_post, w_gates, w_nsa_o, w_moba_o, w_mem_o, w_mix_out, tm=512):
    m = x2.shape[0]
    row = lambda w: pl.BlockSpec((tm, w), lambda i: (i, 0))
    return pl.pallas_call(
        _mix_kernel,
        grid=(m // tm,),
        in_specs=[row(D_MODEL), row(NSA_Q_W), row(MOBA_W), row(MEM_W),
                  _const_spec((1, D_MODEL)), _const_spec((1, D_MODEL)),
                  _const_spec(w_gates.shape), _const_spec(w_nsa_o.shape), _const_spec(w_moba_o.shape),
                  _const_spec(w_mem_o.shape), _const_spec(w_mix_out.shape)],
        out_specs=row(D_MODEL),
        out_shape=jax.ShapeDtypeStruct((m, D_MODEL), F32),
        compiler_params=_params(("parallel",)),
        name="mix",
    )(x2, o_nsa, o_moba, o_mem, g_pre, g_post, w_gates, w_nsa_o, w_moba_o, w_mem_o, w_mix_out)


FFN_CHUNK = 256


def _ffn_kernel(x_ref, g_pre_ref, g_post_ref, wg_ref, wu_ref, wd_ref, o_ref, a_sc):
    x = x_ref[...]
    h = _rms(x, g_pre_ref[...]).astype(BF16)
    d_ff = wg_ref.shape[1]
    for j in range(d_ff // FFN_CHUNK):
        sl = slice(j * FFN_CHUNK, (j + 1) * FFN_CHUNK)
        a_sc[:, sl] = (jax.nn.silu(_dot(h, wg_ref[:, sl])) * _dot(h, wu_ref[:, sl])).astype(BF16)
    f = _dot(a_sc[...], wd_ref[...])
    o_ref[...] = x + _rms(f, g_post_ref[...])


def _ffn(x2, g_pre, g_post, wg, wu, wd, tm=512):
    m = x2.shape[0]
    d_ff = wg.shape[1]
    return pl.pallas_call(
        _ffn_kernel,
        grid=(m // tm,),
        in_specs=[pl.BlockSpec((tm, D_MODEL), lambda i: (i, 0)),
                  _const_spec((1, D_MODEL)), _const_spec((1, D_MODEL)),
                  _const_spec(wg.shape), _const_spec(wu.shape), _const_spec(wd.shape)],
        out_specs=pl.BlockSpec((tm, D_MODEL), lambda i: (i, 0)),
        out_shape=jax.ShapeDtypeStruct((m, D_MODEL), F32),
        scratch_shapes=[pltpu.VMEM((tm, d_ff), BF16)],
        compiler_params=_params(("parallel",)),
        name="ffn",
    )(x2, g_pre, g_post, wg, wu, wd)


def kernel(x, mem, rel_bias, pre_mix_g, mem_norm_g, post_mix_g, w_in, cmp_pos_k, cmp_w1_k, cmp_w2_k, cmp_pos_v, cmp_w1_v, cmp_w2_v, w_mem_kv, w_nsa_o, w_moba_o, w_mem_o, w_mix_out, pre_ffn_g, post_ffn_g, w_ffn_gate, w_ffn_up, w_ffn_down):
    b, s, d_model = x.shape
    depth = w_in.shape[0]
    assert d_model == D_MODEL and s % TILE == 0 and TILE == MOBA_BLOCK == WINDOW
    assert (s - CMP_LEN) // CMP_STRIDE + 1 < N_CMP_PAD and (s // SEL_BLOCK) % SUBLANES == 0 and s // SEL_BLOCK <= HEAD_DIM
    assert w_in.shape[2] == ATT_W + 3 * D_MODEL and rel_bias.shape == (REL_BUCKETS, N_BIAS_HEADS)

    tile_idx, win_idx, cmp_idx = _bucket_tables(s)
    rel_bias = rel_bias.astype(F32)
    t_nsa = _expand(tile_idx, rel_bias, 0, NSA_HEADS, NSA_HPG).reshape(NSA_GROUPS, 2, TILE, NSA_HPG * TILE)
    t_moba = _expand(tile_idx, rel_bias, NSA_HEADS, MOBA_HEADS, MOBA_HEADS).reshape(2, TILE, MOBA_HEADS * TILE)
    t_win = _expand(win_idx, rel_bias, 0, NSA_HEADS, NSA_HPG)
    b_cmp = _expand(cmp_idx, rel_bias, 0, NSA_HEADS)
    c_far = jnp.repeat(rel_bias[REL_BUCKETS - 1] * LOG2E, TILE)
    c_far_nsa = c_far[:NSA_HEADS * TILE].reshape(NSA_GROUPS, 1, NSA_HPG * TILE)
    c_far_moba = c_far[NSA_HEADS * TILE:].reshape(1, MOBA_HEADS * TILE)
    ovt = _overlap_table(s)
    sel_cols = np.zeros((s, HEAD_DIM), np.float32)
    sel_cols[np.arange(s), np.arange(s) // SEL_BLOCK] = 1.0
    sel_cols = jnp.asarray(sel_cols, BF16)
    gate_lo = NSA_Q_W + 6 * NSA_KV_W
    rows_per_chunk = CMP_STRIDE * NSA_KV_W

    x2 = x.reshape(b * s, D_MODEL)
    mem2 = mem.reshape(-1, D_MODEL)
    for l in range(depth):
        w_att = jnp.concatenate(
            [w_in[l, :, :gate_lo + NSA_GATE_W],
             jnp.zeros((D_MODEL, GATE_PAD - NSA_GATE_W), w_in.dtype),
             w_in[l, :, gate_lo + NSA_GATE_W:ATT_W]], axis=1).astype(BF16)
        w_gates = w_in[l, :, ATT_W:].astype(BF16)
        row = lambda v: v[l].reshape(1, D_MODEL)

        qn, kc_raw, vc_raw, ks, vs, kw, vw, gn, qm, km, vm, qx = _inproj(x2, row(pre_mix_g), w_att, sel_cols)

        pk, w1k = _compress_weights(cmp_pos_k[l], cmp_w1_k[l])
        pv, w1v = _compress_weights(cmp_pos_v[l], cmp_w1_v[l])
        kc, vct = _compress(kc_raw.reshape(b, s // CMP_STRIDE, rows_per_chunk),
                            vc_raw.reshape(b, s // CMP_STRIDE, rows_per_chunk),
                            pk, pv, w1k, w1v, cmp_w2_k[l].astype(BF16), cmp_w2_v[l].astype(BF16))

        mk, mv = _memkv(mem2, row(mem_norm_g), w_mem_kv[l].astype(BF16))

        o_nsa, o_moba, o_mem = _attention(b, s, qn, gn, kc, vct, ks, vs, kw, vw, b_cmp, t_nsa, t_win, c_far_nsa, ovt,
                                          qm, km, vm, qx, mk, mv, t_moba, c_far_moba)

        x2 = _mix(x2, o_nsa, o_moba, o_mem, row(pre_mix_g), row(post_mix_g), w_gates,
                  w_nsa_o[l].astype(BF16), w_moba_o[l].astype(BF16), w_mem_o[l].astype(BF16),
                  w_mix_out[l].astype(BF16))
        x2 = _ffn(x2, row(pre_ffn_g), row(post_ffn_g), w_ffn_gate[l].astype(BF16),
                  w_ffn_up[l].astype(BF16), w_ffn_down[l].astype(BF16))
    return x2.reshape(b, s, D_MODEL)
```

```python
import functools
import math

import numpy as np
import jax
import jax.numpy as jnp
from jax import lax
from jax.experimental import pallas as pl
from jax.experimental.pallas import tpu as pltpu

F32 = jnp.float32
BF16 = jnp.bfloat16

D_MODEL = 1024
HEAD_DIM = 64
SCALE = HEAD_DIM ** -0.5
LOG2E = math.log2(math.e)
Q_SCALE = SCALE * LOG2E
NSA_HEADS = 8
NSA_GROUPS = 2
NSA_HPG = NSA_HEADS // NSA_GROUPS
CMP_LEN = 32
CMP_STRIDE = 16
CMP_HIDDEN = 128
SEL_BLOCK = 64
SEL_TOPN = 8
WINDOW = 256
MOBA_HEADS = 4
MOBA_BLOCK = 256
MOBA_TOPK = 3
MEM_HEADS = 4
REL_BUCKETS = 32
REL_MAX_DIST = 128
N_BIAS_HEADS = NSA_HEADS + MOBA_HEADS
RMS_EPS = 1e-6
NEG_INF = -1e30
FORCE_SCORE = 1e4

NSA_Q_W = NSA_HEADS * HEAD_DIM
NSA_KV_W = NSA_GROUPS * HEAD_DIM
NSA_GATE_W = NSA_HEADS * 3
MOBA_W = MOBA_HEADS * HEAD_DIM
MEM_W = MEM_HEADS * HEAD_DIM
ATT_W = NSA_Q_W + 6 * NSA_KV_W + NSA_GATE_W + 3 * MOBA_W + MEM_W
LANES = 128
SUBLANES = 8
BF16_ROWS = 16
MXU_COLS = 256
GATE_PAD = LANES
TILE = 256
HALF = TILE // 2
N_CMP_PAD = 128
V_AUG = HEAD_DIM + BF16_ROWS
MASKED_BUCKET = REL_BUCKETS
VMEM_LIMIT = 56 * 1024 * 1024


def _dot(a, b):
    return jnp.dot(a, b, preferred_element_type=F32)


def _split_bf16(x):
    hi = x.astype(BF16)
    lo = (x - hi.astype(F32)).astype(BF16)
    return hi, lo


def _rms(x, g):
    return x * lax.rsqrt(jnp.mean(x * x, axis=-1, keepdims=True) + RMS_EPS) * g


def _params(sem):
    return pltpu.CompilerParams(dimension_semantics=sem, vmem_limit_bytes=VMEM_LIMIT)


def _const_spec(shape):
    nd = len(shape)
    return pl.BlockSpec(shape, lambda *_: (0,) * nd, pipeline_mode=pl.Buffered(1))


_INPROJ_OUTS = (
    ("qn", NSA_Q_W, BF16, True),
    ("kc", NSA_KV_W, F32, False), ("vc", NSA_KV_W, F32, False),
    ("ks", NSA_KV_W, BF16, False), ("vs", NSA_KV_W, BF16, False),
    ("kw", NSA_KV_W, BF16, False), ("vw", NSA_KV_W, BF16, False),
    ("gn", GATE_PAD, F32, False),
    ("qm", MOBA_W, BF16, True), ("km", MOBA_W, BF16, False), ("vm", MOBA_W, BF16, False),
    ("qx", MEM_W, BF16, True),
)
_INPROJ_W = sum(o[1] for o in _INPROJ_OUTS)
_INPROJ_CHUNKED = ("kc", "vc")
_INPROJ_TRANSPOSED = ("qn", "gn", "qm", "qx")
KS_AUG_W = NSA_GROUPS * 2 * HEAD_DIM


def _inproj_out_width(name, width):
    return KS_AUG_W if name == "ks" else width


def _inproj_kernel(x_ref, g_ref, w_ref, e_ref, *refs):
    out_refs, rows_sc = refs[:-1], refs[-1]
    h = _rms(x_ref[...], g_ref[...]).astype(BF16)
    runs, lo = [], 0
    for out in zip(_INPROJ_OUTS, out_refs):
        if runs and runs[-1][1] < MXU_COLS:
            runs[-1][0].append(out)
            runs[-1][1] += out[0][1]
        else:
            runs.append([[out], out[0][1], lo])
        lo += out[0][1]
    for outs, run_width, run_lo in runs:
        y_run = _dot(h, w_ref[:, run_lo:run_lo + run_width])
        lo = 0
        for (name, width, dtype, scaled), o_ref in outs:
            y = y_run[:, lo:lo + width]
            if scaled:
                y = y * Q_SCALE
            if name == "gn":
                y = jax.nn.sigmoid(y)
            if name in _INPROJ_TRANSPOSED:
                y = y.T
            y = y.astype(dtype)
            if name == "ks":
                e = e_ref[...]
                y = _lane_cat([y[:, :HEAD_DIM], e, y[:, HEAD_DIM:], e])
            if name in _INPROJ_CHUNKED:
                rows_sc[...] = y
                for j in range(CMP_STRIDE):
                    o_ref[:, j * width:(j + 1) * width] = rows_sc[pl.ds(j, o_ref.shape[0], stride=CMP_STRIDE), :]
            else:
                o_ref[...] = y
            lo += width


def _inproj(x2, g, w, e_cols, tm=512):
    m = x2.shape[0]
    tiles_per_seq = e_cols.shape[0] // tm
    out_specs, out_shape = [], []
    for name, width, dtype, _ in _INPROJ_OUTS:
        if name in _INPROJ_TRANSPOSED:
            out_specs.append(pl.BlockSpec((width, tm), lambda i: (0, i)))
            out_shape.append(jax.ShapeDtypeStruct((width, m), dtype))
            continue
        rows, width = (CMP_STRIDE, CMP_STRIDE * width) if name in _INPROJ_CHUNKED else (1, _inproj_out_width(name, width))
        out_specs.append(pl.BlockSpec((tm // rows, width), lambda i: (i, 0)))
        out_shape.append(jax.ShapeDtypeStruct((m // rows, width), dtype))
    return pl.pallas_call(
        _inproj_kernel,
        grid=(m // tm,),
        in_specs=[pl.BlockSpec((tm, D_MODEL), lambda i: (i, 0)),
                  _const_spec((1, D_MODEL)),
                  _const_spec((D_MODEL, _INPROJ_W)),
                  pl.BlockSpec((tm, HEAD_DIM), lambda i: (i % tiles_per_seq, 0))],
        out_specs=out_specs,
        out_shape=out_shape,
        scratch_shapes=[pltpu.VMEM((tm, NSA_KV_W), F32)],
        compiler_params=_params(("parallel",)),
        name="inproj",
    )(x2, g, w, e_cols)


def _compress_kernel(rk_ref, rv_ref, pk_ref, pv_ref, w1k_ref, w1v_ref, w2k_ref, w2v_ref, kc_ref, vc_ref):
    def one(r_ref, p_ref, w1_ref, w2_ref):
        r = r_ref[0]
        top = _dot((r + p_ref[0:1, :]).astype(BF16), w1_ref[0])
        bot = _dot((r + p_ref[1:2, :]).astype(BF16), w1_ref[1])
        hid = top + pltpu.roll(bot, N_CMP_PAD - 1, 0)
        act = jax.nn.gelu(hid).astype(BF16)
        return jnp.concatenate(
            [_dot(act[:, g * CMP_HIDDEN:(g + 1) * CMP_HIDDEN], w2_ref[...]) for g in range(NSA_GROUPS)], axis=1)

    kc_ref[0] = one(rk_ref, pk_ref, w1k_ref, w2k_ref).astype(BF16)
    vc_ref[0] = one(rv_ref, pv_ref, w1v_ref, w2v_ref).T.astype(BF16)


def _compress(rk, rv, pk, pv, w1k, w1v, w2k, w2v):
    b = rk.shape[0]
    rw = rk.shape[2]
    r_spec = pl.BlockSpec((1, N_CMP_PAD, rw), lambda i: (i, 0, 0))
    o_spec = pl.BlockSpec((1, N_CMP_PAD, NSA_KV_W), lambda i: (i, 0, 0))
    return pl.pallas_call(
        _compress_kernel,
        grid=(b,),
        in_specs=[r_spec, r_spec, _const_spec(pk.shape), _const_spec(pv.shape),
                  _const_spec(w1k.shape), _const_spec(w1v.shape),
                  _const_spec(w2k.shape), _const_spec(w2v.shape)],
        out_specs=[o_spec, o_spec],
        out_shape=[jax.ShapeDtypeStruct((b, N_CMP_PAD, NSA_KV_W), BF16)] * 2,
        compiler_params=_params(("parallel",)),
        name="compress",
    )(rk, rv, pk, pv, w1k, w1v, w2k, w2v)


def _compress_weights(pos, w1):
    half = CMP_LEN // 2
    p = pos.reshape(2, half, 1, HEAD_DIM)
    p = jnp.broadcast_to(p, (2, half, NSA_GROUPS, HEAD_DIM)).reshape(2, half * NSA_KV_W)
    w = w1.reshape(2, half, HEAD_DIM, CMP_HIDDEN)
    eye = jnp.eye(NSA_GROUPS, dtype=w1.dtype)
    wbd = jnp.einsum("ajdm,gk->ajgdkm", w, eye).reshape(2, half * NSA_KV_W, NSA_GROUPS * CMP_HIDDEN)
    return p.astype(F32), wbd.astype(BF16)


def _memkv_kernel(m_ref, g_ref, w_ref, k_ref, v_ref):
    h = _rms(m_ref[...], g_ref[...]).astype(BF16)
    k_ref[...] = _dot(h, w_ref[:, :MEM_W]).astype(BF16)
    v_ref[...] = _dot(h, w_ref[:, MEM_W:]).astype(BF16)


def _memkv(mem2, g, w, tm=256):
    m = mem2.shape[0]
    o_spec = pl.BlockSpec((tm, MEM_W), lambda i: (i, 0))
    return pl.pallas_call(
        _memkv_kernel,
        grid=(m // tm,),
        in_specs=[pl.BlockSpec((tm, D_MODEL), lambda i: (i, 0)), _const_spec((1, D_MODEL)),
                  _const_spec((D_MODEL, 2 * MEM_W))],
        out_specs=[o_spec, o_spec],
        out_shape=[jax.ShapeDtypeStruct((m, MEM_W), BF16)] * 2,
        compiler_params=_params(("parallel",)),
        name="memkv",
    )(mem2, g, w)


def _expand_kernel(idx_ref, bias_ref, o_ref, *, head0, n_heads, heads_per_group):
    rows, cols = idx_ref.shape

    def body(i, carry):
        r = pl.multiple_of(i * SUBLANES, SUBLANES)
        for c0 in range(0, cols, TILE):
            idx = idx_ref[pl.ds(r, SUBLANES), c0:c0 + TILE]
            out = [jnp.full(idx.shape, NEG_INF, F32)] * n_heads
            for bkt in range(REL_BUCKETS):
                hit = idx == bkt
                out = [jnp.where(hit, bias_ref[bkt, head0 + h], out[h]) for h in range(n_heads)]
            for h in range(n_heads):
                col = (h % heads_per_group) * cols + c0
                o_ref[h // heads_per_group, pl.ds(r, SUBLANES), col:col + TILE] = out[h] * LOG2E
        return carry

    lax.fori_loop(0, rows // SUBLANES, body, 0)


def _expand(idx, rel_bias, head0, n_heads, heads_per_group=1):
    rows, cols = idx.shape
    return pl.pallas_call(
        functools.partial(_expand_kernel, head0=head0, n_heads=n_heads, heads_per_group=heads_per_group),
        in_specs=[pl.BlockSpec(memory_space=pltpu.VMEM), pl.BlockSpec(memory_space=pltpu.SMEM)],
        out_specs=pl.BlockSpec(memory_space=pltpu.VMEM),
        out_shape=jax.ShapeDtypeStruct((n_heads // heads_per_group, rows, heads_per_group * cols), F32),
        compiler_params=pltpu.CompilerParams(vmem_limit_bytes=VMEM_LIMIT),
        name="bias_expand",
    )(idx, rel_bias)


def _t5_bucket_np(dist):
    dist = np.maximum(dist, 0)
    max_exact = REL_BUCKETS // 2
    logd = np.log(np.maximum(dist, 1).astype(np.float32) / max_exact) / math.log(REL_MAX_DIST / max_exact)
    large = np.minimum(max_exact + (logd * (REL_BUCKETS - max_exact)).astype(np.int32), REL_BUCKETS - 1)
    return np.where(dist < max_exact, dist, large).astype(np.int32)


def _bucket_tables(s):
    j = np.arange(TILE)[:, None]
    i = np.arange(TILE)[None, :]
    assert TILE + 1 >= REL_MAX_DIST
    tiles = []
    for d in range(2):
        dist = d * TILE + i - j
        tiles.append(np.where(dist >= 0, _t5_bucket_np(dist), MASKED_BUCKET))
    dist1 = TILE + i - j
    win = np.where(dist1 < WINDOW, _t5_bucket_np(dist1), MASKED_BUCKET)
    n_cmp = (s - CMP_LEN) // CMP_STRIDE + 1
    c = np.arange(N_CMP_PAD)[:, None]
    dist_c = np.arange(s)[None, :] - (c * CMP_STRIDE + CMP_LEN - 1)
    cmp_idx = np.where((dist_c >= 0) & (c < n_cmp), _t5_bucket_np(dist_c), MASKED_BUCKET)
    as_i32 = lambda a: jnp.asarray(a.astype(np.int32))
    return as_i32(np.concatenate(tiles, axis=0)), as_i32(win), as_i32(cmp_idx)


def _overlap_table(s):
    n_cmp = (s - CMP_LEN) // CMP_STRIDE + 1
    n_sel = s // SEL_BLOCK
    cs = np.arange(n_cmp) * CMP_STRIDE
    ss = np.arange(n_sel) * SEL_BLOCK
    ov = np.clip(np.minimum(cs[:, None] + CMP_LEN, ss[None, :] + SEL_BLOCK)
                 - np.maximum(cs[:, None], ss[None, :]), 0, None).astype(np.float32) / CMP_LEN
    ovt = np.zeros((n_sel, N_CMP_PAD), np.float32)
    ovt[:, :n_cmp] = ov.T
    return jnp.asarray(ovt, BF16)


def _store_v_aug(vt_sc, idx, vt):
    ones = jnp.ones((BF16_ROWS, vt.shape[1]), BF16)
    vt_sc[idx] = jnp.concatenate([vt.astype(BF16), ones], axis=0)


def _lane_cat(xs):
    return jnp.concatenate(xs, axis=1)


def _query_halves(x):
    n = x.shape[-1] // TILE
    first = _lane_cat([x[:, k * TILE:k * TILE + HALF] for k in range(n)])
    second = _lane_cat([x[:, k * TILE + HALF:(k + 1) * TILE] for k in range(n)])
    return first, second


def _join_query_halves(first, second):
    n = first.shape[-1] // HALF
    return _lane_cat([part for k in range(n)
                      for part in (first[:, k * HALF:(k + 1) * HALF], second[:, k * HALF:(k + 1) * HALF])])


def _triangle_tile(k, q, table, pv_lo, pv_hi, causal):
    q_first, q_second = _query_halves(q)
    lo, hi = (0, HALF), (HALF, TILE)
    if causal:
        s_wide = _dot(k(*lo), q) + table(*lo)
        s_narrow = _dot(k(*hi), q_second) + _query_halves(table(*hi))[1]
        pv_wide, pv_narrow = pv_lo, pv_hi
    else:
        s_wide = _dot(k(*hi), q) + table(*hi)
        s_narrow = _dot(k(*lo), q_first) + _query_halves(table(*lo))[0]
        pv_wide, pv_narrow = pv_hi, pv_lo
    mw_first, mw_second = _query_halves(jnp.max(s_wide, axis=0, keepdims=True))
    m_narrow = jnp.max(s_narrow, axis=0, keepdims=True)
    if causal:
        m_narrow = jnp.maximum(m_narrow, mw_second)
        m = _join_query_halves(mw_first, m_narrow)
    else:
        m_narrow = jnp.maximum(m_narrow, mw_first)
        m = _join_query_halves(m_narrow, mw_second)
    aw_first, aw_second = _query_halves(pv_wide(jnp.exp2(s_wide - m).astype(BF16)))
    a_narrow = pv_narrow(jnp.exp2(s_narrow - m_narrow).astype(BF16))
    if causal:
        return m, _join_query_halves(aw_first, aw_second + a_narrow)
    return m, _join_query_halves(aw_first + a_narrow, aw_second)


def _flash_pipelined(own, streams, s_sc, m_ref, acc_ref):
    has_prev = jnp.where(own > 0, 1.0, 0.0).astype(F32)
    prev = jnp.maximum(own - 1, 0)
    n_far = jnp.maximum(own - 1, 0)

    def absorb(g, s, kt, c_row, w_row):
        u = jnp.max(s, axis=0, keepdims=True) + c_row
        m_old = m_ref[g]
        m_new = jnp.maximum(m_old, jnp.where(w_row > 0.0, u, NEG_INF))
        alpha = jnp.exp2(m_old - m_new)
        shift = jnp.maximum(m_new, u) - c_row
        p = jnp.exp2(s - shift).astype(BF16)
        acc_ref[g] = alpha * acc_ref[g] + w_row * streams[g]["pv"](kt)(p)
        m_ref[g] = m_new

    def absorb_slot(g, i):
        is_prev = i == 0
        kt = jnp.where(is_prev, prev, i - 1)
        c_row = jnp.where(is_prev, 0.0, streams[g]["c_far"])
        w_row = streams[g]["w"](kt) * jnp.where(is_prev, has_prev, 1.0)
        absorb(g, s_sc[g], kt, c_row, w_row)

    for g, stream in enumerate(streams):
        m_ref[g], acc_ref[g] = stream["own"]()
    for g, stream in enumerate(streams):
        s_sc[g] = stream["prev"]()

    def body(i, carry):
        nxt = [stream["far"](i) for stream in streams]
        for g in range(len(streams)):
            absorb_slot(g, i)
        for g in range(len(streams)):
            s_sc[g] = nxt[g]
        return carry

    lax.fori_loop(0, n_far, body, 0)
    for g in range(len(streams)):
        absorb_slot(g, n_far)


def _softmax_av(s_list, pv_list):
    m = s_list[0].max(axis=0, keepdims=True)
    for s in s_list[1:]:
        m = jnp.maximum(m, s.max(axis=0, keepdims=True))
    acc = None
    for s, pv in zip(s_list, pv_list):
        part = pv(jnp.exp2(s - m).astype(BF16))
        acc = part if acc is None else acc + part
    return acc


def _normalize(acc):
    return acc[:HEAD_DIM] / acc[HEAD_DIM:HEAD_DIM + 1]


def _rank_before(score, n_cand):
    ranks = []
    for r0 in range(0, score.shape[0], SUBLANES):
        tile = score[r0:r0 + SUBLANES]
        blk = lax.broadcasted_iota(jnp.int32, tile.shape, 0) + r0
        rank = jnp.zeros(tile.shape, F32)
        for m in range(n_cand):
            row = score[m:m + 1, :]
            if m < r0:
                before = jnp.where(row >= tile, 1.0, 0.0)
            elif m >= r0 + SUBLANES:
                before = jnp.where(row > tile, 1.0, 0.0)
            else:
                before = jnp.where(blk > m, jnp.where(row >= tile, 1.0, 0.0), jnp.where(row > tile, 1.0, 0.0))
            rank = rank + before
        ranks.append(rank)
    return jnp.concatenate(ranks, axis=0)


def _nsa_steps(qi, q_ref, gn_ref, kc_ref, vct_ref, ks_ref, vs_ref, kw_ref, vw_ref,
               bct_ref, tt_ref, twt_ref, cfar_ref, ovt_ref, o_ref,
               vst_sc, vwt_sc, qa_sc, og_sc, acc_sc):
    nt = ks_ref.shape[1]
    n_sel = ovt_ref.shape[0]

    @pl.when(qi == 0)
    def _():
        for kt in range(nt):
            vs_t = vs_ref[0, kt].astype(F32).T
            vw_t = vw_ref[0, kt].astype(F32).T
            for g in range(NSA_GROUPS):
                _store_v_aug(vst_sc, (kt, g), vs_t[g * HEAD_DIM:(g + 1) * HEAD_DIM])
                _store_v_aug(vwt_sc, (kt, g), vw_t[g * HEAD_DIM:(g + 1) * HEAD_DIM])

    pos = lax.broadcasted_iota(jnp.int32, (1, TILE), 1) + qi * TILE
    cur = pos // SEL_BLOCK
    has_cmp = pos >= CMP_LEN - 1
    blk = lax.broadcasted_iota(jnp.int32, (n_sel, TILE), 0)
    prev = jnp.maximum(qi - 1, 0)
    no_prev = jnp.where(qi == 0, NEG_INF, 0.0).astype(F32)
    gates = gn_ref[...]

    gsls = [slice(g * HEAD_DIM, (g + 1) * HEAD_DIM) for g in range(NSA_GROUPS)]
    group_heads = [[g * NSA_HPG + j for j in range(NSA_HPG)] for g in range(NSA_GROUPS)]

    def gate(g, branch):
        return _lane_cat([gates[3 * h + branch:3 * h + branch + 1, :] for h in group_heads[g]])

    for g in range(NSA_GROUPS):
        heads = group_heads[g]
        q4 = _lane_cat([q_ref[h * HEAD_DIM:(h + 1) * HEAD_DIM, :] for h in heads])
        qa_sc[g, 0:HEAD_DIM, :] = q4
        qa_sc[g, HEAD_DIM + n_sel:, :] = jnp.zeros((HEAD_DIM - n_sel, NSA_HPG * TILE), BF16)

        def v_half(vt_sc, kt, lo, hi):
            return lambda pr: _dot(vt_sc[kt, g, :, lo:hi], pr)

        m_own, acc_own = _triangle_tile(
            lambda lo, hi: kw_ref[0, qi, lo:hi, gsls[g]], q4, lambda lo, hi: tt_ref[g, 0, lo:hi, :],
            v_half(vwt_sc, qi, 0, HALF), v_half(vwt_sc, qi, HALF, TILE), causal=True)
        m_prev, acc_prev = _triangle_tile(
            lambda lo, hi: kw_ref[0, prev, lo:hi, gsls[g]], q4, lambda lo, hi: twt_ref[g, lo:hi, :] + no_prev,
            v_half(vwt_sc, prev, 0, HALF), v_half(vwt_sc, prev, HALF, TILE), causal=False)
        m_win = jnp.maximum(m_own, m_prev)
        acc_w = acc_own * jnp.exp2(m_own - m_win) + acc_prev * jnp.exp2(m_prev - m_win)
        o_win = gate(g, 2) * _normalize(acc_w)

        kc = kc_ref[0, :, gsls[g]]
        vct = vct_ref[0, gsls[g], :]
        q_cols = pl.ds(pl.multiple_of(qi * TILE, TILE), TILE)
        s = _dot(kc, q4) + _lane_cat([bct_ref[h, :, q_cols] for h in heads])
        e = jnp.exp2(s - jnp.max(s, axis=0, keepdims=True))
        p = jnp.where(_lane_cat([has_cmp] * NSA_HPG), e / jnp.sum(e, axis=0, keepdims=True), 0.0)
        psum = p[:, :TILE]
        for j in range(1, NSA_HPG):
            psum = psum + p[:, j * TILE:(j + 1) * TILE]
        og_sc[g] = gate(g, 0) * _dot(vct, p.astype(BF16)) + o_win

        p_hi, p_lo = _split_bf16(psum)
        imp = _dot(ovt_ref[...], p_hi) + _dot(ovt_ref[...], p_lo)
        forced = (blk == 0) | (blk == cur) | (blk == cur - 1)
        score = jnp.where(forced, FORCE_SCORE, jnp.where(blk <= cur, imp, NEG_INF))
        rank = _rank_before(score, n_sel)
        sel = jnp.where(rank < SEL_TOPN, jnp.where(score > NEG_INF / 2, 0.0, NEG_INF), NEG_INF)
        qa_sc[g, HEAD_DIM:HEAD_DIM + n_sel, :] = _lane_cat([sel.astype(BF16)] * NSA_HPG)

    ones_row = jnp.ones((1, NSA_HPG * TILE), F32)

    def sel_stream(g):
        def qk(kt):
            return _dot(ks_ref[0, kt, :, g * 2 * HEAD_DIM:(g + 1) * 2 * HEAD_DIM], qa_sc[g])

        def own():
            return _triangle_tile(
                lambda lo, hi: ks_ref[0, qi, lo:hi, g * 2 * HEAD_DIM:(g + 1) * 2 * HEAD_DIM], qa_sc[g],
                lambda lo, hi: tt_ref[g, 0, lo:hi, :],
                lambda pr: _dot(vst_sc[qi, g, :, 0:HALF], pr), lambda pr: _dot(vst_sc[qi, g, :, HALF:TILE], pr),
                causal=True)

        return dict(own=own, prev=lambda: qk(prev) + tt_ref[g, 1], far=qk,
                    c_far=cfar_ref[g], w=lambda kt: ones_row,
                    pv=lambda kt: (lambda pr: _dot(vst_sc[kt, g], pr)))

    yield [sel_stream(g) for g in range(NSA_GROUPS)]

    for g in range(NSA_GROUPS):
        o = og_sc[g] + gate(g, 1) * _normalize(acc_sc[g])
        for j, h in enumerate(group_heads[g]):
            o_ref[h * HEAD_DIM:(h + 1) * HEAD_DIM, :] = o[:, j * TILE:(j + 1) * TILE].astype(BF16)


def _moba_steps(c, qm_ref, km_ref, vm_ref, qx_ref, mk_ref, mv_ref, tt_ref, cfar_ref, om_ref, ox_ref,
                vmt_sc, mvt_sc, kmean_sc, qbd_sc, sel_sc, acc_sc, slot):
    nt = km_ref.shape[1]
    hsls = [slice(h * HEAD_DIM, (h + 1) * HEAD_DIM) for h in range(MOBA_HEADS)]

    @pl.when(c == 0)
    def _():
        kmean_sc[...] = jnp.zeros(kmean_sc.shape, F32)
        for n in range(nt):
            kmean_sc[n:n + 1, :] = jnp.sum(km_ref[0, n].astype(F32), axis=0, keepdims=True) * (1.0 / MOBA_BLOCK)
            vt = vm_ref[0, n].astype(F32).T
            for h in range(MOBA_HEADS):
                _store_v_aug(vmt_sc, (n, h), vt[hsls[h]])
        mvt = mv_ref[0].astype(F32).T
        for h in range(MOBA_HEADS):
            _store_v_aug(mvt_sc, h, mvt[hsls[h]])

    row_head = lax.broadcasted_iota(jnp.int32, (MOBA_W, TILE), 0) // HEAD_DIM

    def block_diag(q_ref):
        q_t = q_ref[...].astype(F32)
        return _lane_cat([jnp.where(row_head == h, q_t, 0.0) for h in range(MOBA_HEADS)]).astype(BF16)

    def per_head_pv(vts):
        return lambda pr: _lane_cat([_dot(vts(h), pr[:, h * TILE:(h + 1) * TILE]) for h in range(MOBA_HEADS)])

    def store_heads(o_t, out_ref):
        for h in range(MOBA_HEADS):
            out_ref[hsls[h], :] = o_t[:, h * TILE:(h + 1) * TILE].astype(BF16)

    qbd = block_diag(qm_ref)
    km_hi, km_lo = _split_bf16(kmean_sc[...])
    n_rows = -(-nt // SUBLANES) * SUBLANES
    gate = (_dot(km_hi, qbd) + _dot(km_lo, qbd))[:n_rows]
    blk = lax.broadcasted_iota(jnp.int32, gate.shape, 0)
    score = jnp.where(blk < c, gate, NEG_INF * Q_SCALE)
    rank = _rank_before(score, nt)
    sel_sc[0:n_rows, :] = jnp.where(rank < MOBA_TOPK, jnp.where(score > NEG_INF * Q_SCALE / 2, 1.0, 0.0), 0.0)

    qbd_sc[...] = qbd
    qk = lambda n: _dot(km_ref[0, n], qbd_sc[...])
    def own_pv(lo, hi):
        def pv(pr):
            width = pr.shape[1] // MOBA_HEADS
            return _lane_cat([_dot(vmt_sc[c, h, :, lo:hi], pr[:, h * width:(h + 1) * width])
                              for h in range(MOBA_HEADS)])
        return pv

    def own():
        return _triangle_tile(lambda lo, hi: km_ref[0, c, lo:hi, :], qbd_sc[...], lambda lo, hi: tt_ref[0, lo:hi, :],
                              own_pv(0, HALF), own_pv(HALF, TILE), causal=True)

    stream = dict(own=own, prev=lambda: qk(jnp.maximum(c - 1, 0)) + tt_ref[1], far=qk,
                  c_far=cfar_ref[...], w=lambda n: sel_sc[pl.ds(n, 1), :],
                  pv=lambda n: per_head_pv(lambda h: vmt_sc[n, h]))
    s = _dot(mk_ref[0], block_diag(qx_ref))
    store_heads(_normalize(_softmax_av([s], [per_head_pv(lambda h: mvt_sc[h])])), ox_ref)

    yield [stream]
    store_heads(_normalize(acc_sc[slot]), om_ref)


N_NSA_IN, N_MOBA_IN = 13, 8
N_NSA_SCRATCH, N_MOBA_SCRATCH = 4, 5
N_STREAMS = NSA_GROUPS + 1


def _attention_kernel(*refs):
    qi = pl.program_id(1)
    nsa_in, refs = refs[:N_NSA_IN], refs[N_NSA_IN:]
    moba_in, refs = refs[:N_MOBA_IN], refs[N_MOBA_IN:]
    (o_nsa, o_moba, o_mem), refs = refs[:3], refs[3:]
    nsa_sc, refs = refs[:N_NSA_SCRATCH], refs[N_NSA_SCRATCH:]
    moba_sc, refs = refs[:N_MOBA_SCRATCH], refs[N_MOBA_SCRATCH:]
    s_sc, m_sc, acc_sc = refs
    nsa = _nsa_steps(qi, *nsa_in, o_nsa, *nsa_sc, acc_sc)
    moba = _moba_steps(qi, *moba_in, o_moba, o_mem, *moba_sc, acc_sc, NSA_GROUPS)
    streams = next(nsa) + next(moba)
    _flash_pipelined(qi, streams, s_sc, m_sc, acc_sc)
    for steps in (nsa, moba):
        for _ in steps:
            pass


def _attention(b, s, qn, gn, kc, vct, ks, vs, kw, vw, bias_cmp, t_nsa, t_win, c_far_nsa, ovt,
               qm, km, vm, qx, mk, mv, t_moba, c_far_moba):
    nt = s // TILE
    mem_len = mk.shape[0] // b
    assert MOBA_TOPK <= nt - 1 and nt <= BF16_ROWS
    n_lanes = NSA_HPG * TILE
    assert MOBA_HEADS * TILE == n_lanes
    row_spec = lambda w: pl.BlockSpec((w, TILE), lambda i, j: (0, i * nt + j))
    seq_spec = lambda w: pl.BlockSpec((1, nt, TILE, w), lambda i, j: (i, 0, 0, 0))
    per_batch = lambda rows, w: pl.BlockSpec((1, rows, w), lambda i, j: (i, 0, 0))
    tiles = lambda a: a.reshape(b, nt, TILE, a.shape[-1])
    nsa_in = [(qn, row_spec(NSA_Q_W)), (gn, row_spec(GATE_PAD)),
              (kc, per_batch(N_CMP_PAD, NSA_KV_W)), (vct, per_batch(N_CMP_PAD, NSA_KV_W)),
              (tiles(ks), seq_spec(KS_AUG_W)), (tiles(vs), seq_spec(NSA_KV_W)),
              (tiles(kw), seq_spec(NSA_KV_W)), (tiles(vw), seq_spec(NSA_KV_W)),
              (bias_cmp, _const_spec(bias_cmp.shape)), (t_nsa, _const_spec(t_nsa.shape)),
              (t_win, _const_spec(t_win.shape)), (c_far_nsa, _const_spec(c_far_nsa.shape)),
              (ovt, _const_spec(ovt.shape))]
    moba_in = [(qm, row_spec(MOBA_W)), (tiles(km), seq_spec(MOBA_W)), (tiles(vm), seq_spec(MOBA_W)),
               (qx, row_spec(MEM_W)),
               (mk.reshape(b, mem_len, MEM_W), per_batch(mem_len, MEM_W)),
               (mv.reshape(b, mem_len, MEM_W), per_batch(mem_len, MEM_W)),
               (t_moba, _const_spec(t_moba.shape)), (c_far_moba, _const_spec(c_far_moba.shape))]
    assert len(nsa_in) == N_NSA_IN and len(moba_in) == N_MOBA_IN
    nsa_scratch = [pltpu.VMEM((nt, NSA_GROUPS, V_AUG, TILE), BF16),
                   pltpu.VMEM((nt, NSA_GROUPS, V_AUG, TILE), BF16),
                   pltpu.VMEM((NSA_GROUPS, 2 * HEAD_DIM, n_lanes), BF16),
                   pltpu.VMEM((NSA_GROUPS, HEAD_DIM, n_lanes), F32)]
    moba_scratch = [pltpu.VMEM((nt, MOBA_HEADS, V_AUG, TILE), BF16),
                    pltpu.VMEM((MEM_HEADS, V_AUG, mem_len), BF16),
                    pltpu.VMEM((BF16_ROWS, MOBA_W), F32),
                    pltpu.VMEM((MOBA_W, n_lanes), BF16),
                    pltpu.VMEM((BF16_ROWS, n_lanes), F32)]
    assert len(nsa_scratch) == N_NSA_SCRATCH and len(moba_scratch) == N_MOBA_SCRATCH
    shared_scratch = [pltpu.VMEM((N_STREAMS, TILE, n_lanes), F32),
                      pltpu.VMEM((N_STREAMS, 1, n_lanes), F32),
                      pltpu.VMEM((N_STREAMS, V_AUG, n_lanes), F32)]
    inputs = nsa_in + moba_in
    return pl.pallas_call(
        _attention_kernel,
        grid=(b, nt),
        in_specs=[spec for _, spec in inputs],
        out_specs=[row_spec(NSA_Q_W), row_spec(MOBA_W), row_spec(MEM_W)],
        out_shape=[jax.ShapeDtypeStruct((w, b * s), BF16) for w in (NSA_Q_W, MOBA_W, MEM_W)],
        scratch_shapes=nsa_scratch + moba_scratch + shared_scratch,
        compiler_params=_params(("arbitrary", "arbitrary")),
        name="attention",
    )(*[a for a, _ in inputs])


def _mix_kernel(x_ref, on_ref, om_ref, ox_ref, g_pre_ref, g_post_ref, wg_ref, wn_ref, wm_ref, wx_ref,
                wo_ref, o_ref):
    x = x_ref[...]
    h = _rms(x, g_pre_ref[...]).astype(BF16)
    rows = lambda o_ref: o_ref[...].astype(F32).T.astype(BF16)
    merged = jax.nn.sigmoid(_dot(h, wg_ref[:, :D_MODEL])) * _dot(rows(on_ref), wn_ref[...])
    merged = merged + jax.nn.sigmoid(_dot(h, wg_ref[:, D_MODEL:2 * D_MODEL])) * _dot(rows(om_ref), wm_ref[...])
    merged = merged + jax.nn.sigmoid(_dot(h, wg_ref[:, 2 * D_MODEL:])) * _dot(rows(ox_ref), wx_ref[...])
    y = _dot(merged.astype(BF16), wo_ref[...])
    o_ref[...] = x + _rms(y, g_post_ref[...])


def _mix(x2, o_nsa, o_moba, o_mem, g_pre, g_post, w_gates, w_nsa_o, w_moba_o, w_mem_o, w_mix_out, tm=512):
    m = x2.shape[0]
    row = lambda w: pl.BlockSpec((tm, w), lambda i: (i, 0))
    col = lambda w: pl.BlockSpec((w, tm), lambda i: (0, i))
    return pl.pallas_call(
        _mix_kernel,
        grid=(m // tm,),
        in_specs=[row(D_MODEL), col(NSA_Q_W), col(MOBA_W), col(MEM_W),
                  _const_spec((1, D_MODEL)), _const_spec((1, D_MODEL)),
                  _const_spec(w_gates.shape), _const_spec(w_nsa_o.shape), _const_spec(w_moba_o.shape),
                  _const_spec(w_mem_o.shape), _const_spec(w_mix_out.shape)],
        out_specs=row(D_MODEL),
        out_shape=jax.ShapeDtypeStruct((m, D_MODEL), F32),
        compiler_params=_params(("parallel",)),
        name="mix",
    )(x2, o_nsa, o_moba, o_mem, g_pre, g_post, w_gates, w_nsa_o, w_moba_o, w_mem_o, w_mix_out)


FFN_CHUNK = 256


def _ffn_kernel(x_ref, g_pre_ref, g_post_ref, wg_ref, wu_ref, wd_ref, o_ref, a_sc):
    x = x_ref[...]
    h = _rms(x, g_pre_ref[...]).astype(BF16)
    d_ff = wg_ref.shape[1]
    for j in range(d_ff // FFN_CHUNK):
        sl = slice(j * FFN_CHUNK, (j + 1) * FFN_CHUNK)
        a_sc[:, sl] = (jax.nn.silu(_dot(h, wg_ref[:, sl])) * _dot(h, wu_ref[:, sl])).astype(BF16)
    f = _dot(a_sc[...], wd_ref[...])
    o_ref[...] = x + _rms(f, g_post_ref[...])


def _ffn(x2, g_pre, g_post, wg, wu, wd, tm=512):
    m = x2.shape[0]
    d_ff = wg.shape[1]
    return pl.pallas_call(
        _ffn_kernel,
        grid=(m // tm,),
        in_specs=[pl.BlockSpec((tm, D_MODEL), lambda i: (i, 0)),
                  _const_spec((1, D_MODEL)), _const_spec((1, D_MODEL)),
                  _const_spec(wg.shape), _const_spec(wu.shape), _const_spec(wd.shape)],
        out_specs=pl.BlockSpec((tm, D_MODEL), lambda i: (i, 0)),
        out_shape=jax.ShapeDtypeStruct((m, D_MODEL), F32),
        scratch_shapes=[pltpu.VMEM((tm, d_ff), BF16)],
        compiler_params=_params(("parallel",)),
        name="ffn",
    )(x2, g_pre, g_post, wg, wu, wd)


def kernel(x, mem, rel_bias, pre_mix_g, mem_norm_g, post_mix_g, w_in, cmp_pos_k, cmp_w1_k, cmp_w2_k, cmp_pos_v, cmp_w1_v, cmp_w2_v, w_mem_kv, w_nsa_o, w_moba_o, w_mem_o, w_mix_out, pre_ffn_g, post_ffn_g, w_ffn_gate, w_ffn_up, w_ffn_down):
    b, s, d_model = x.shape
    depth = w_in.shape[0]
    assert d_model == D_MODEL and s % TILE == 0 and TILE == MOBA_BLOCK == WINDOW
    assert (s - CMP_LEN) // CMP_STRIDE + 1 < N_CMP_PAD and (s // SEL_BLOCK) % SUBLANES == 0 and s // SEL_BLOCK <= HEAD_DIM
    assert w_in.shape[2] == ATT_W + 3 * D_MODEL and rel_bias.shape == (REL_BUCKETS, N_BIAS_HEADS)

    tile_idx, win_idx, cmp_idx = _bucket_tables(s)
    rel_bias = rel_bias.astype(F32)
    t_nsa = _expand(tile_idx, rel_bias, 0, NSA_HEADS, NSA_HPG).reshape(NSA_GROUPS, 2, TILE, NSA_HPG * TILE)
    t_moba = _expand(tile_idx, rel_bias, NSA_HEADS, MOBA_HEADS, MOBA_HEADS).reshape(2, TILE, MOBA_HEADS * TILE)
    t_win = _expand(win_idx, rel_bias, 0, NSA_HEADS, NSA_HPG)
    b_cmp = _expand(cmp_idx, rel_bias, 0, NSA_HEADS)
    c_far = jnp.repeat(rel_bias[REL_BUCKETS - 1] * LOG2E, TILE)
    c_far_nsa = c_far[:NSA_HEADS * TILE].reshape(NSA_GROUPS, 1, NSA_HPG * TILE)
    c_far_moba = c_far[NSA_HEADS * TILE:].reshape(1, MOBA_HEADS * TILE)
    ovt = _overlap_table(s)
    sel_cols = np.zeros((s, HEAD_DIM), np.float32)
    sel_cols[np.arange(s), np.arange(s) // SEL_BLOCK] = 1.0
    sel_cols = jnp.asarray(sel_cols, BF16)
    gate_lo = NSA_Q_W + 6 * NSA_KV_W
    rows_per_chunk = CMP_STRIDE * NSA_KV_W

    x2 = x.reshape(b * s, D_MODEL)
    mem2 = mem.reshape(-1, D_MODEL)
    for l in range(depth):
        w_att = jnp.concatenate(
            [w_in[l, :, :gate_lo + NSA_GATE_W],
             jnp.zeros((D_MODEL, GATE_PAD - NSA_GATE_W), w_in.dtype),
             w_in[l, :, gate_lo + NSA_GATE_W:ATT_W]], axis=1).astype(BF16)
        w_gates = w_in[l, :, ATT_W:].astype(BF16)
        row = lambda v: v[l].reshape(1, D_MODEL)

        qn, kc_raw, vc_raw, ks, vs, kw, vw, gn, qm, km, vm, qx = _inproj(x2, row(pre_mix_g), w_att, sel_cols)

        pk, w1k = _compress_weights(cmp_pos_k[l], cmp_w1_k[l])
        pv, w1v = _compress_weights(cmp_pos_v[l], cmp_w1_v[l])
        kc, vct = _compress(kc_raw.reshape(b, s // CMP_STRIDE, rows_per_chunk),
                            vc_raw.reshape(b, s // CMP_STRIDE, rows_per_chunk),
                            pk, pv, w1k, w1v, cmp_w2_k[l].astype(BF16), cmp_w2_v[l].astype(BF16))

        mk, mv = _memkv(mem2, row(mem_norm_g), w_mem_kv[l].astype(BF16))

        o_nsa, o_moba, o_mem = _attention(b, s, qn, gn, kc, vct, ks, vs, kw, vw, b_cmp, t_nsa, t_win, c_far_nsa, ovt,
                                          qm, km, vm, qx, mk, mv, t_moba, c_far_moba)

        x2 = _mix(x2, o_nsa, o_moba, o_mem, row(pre_mix_g), row(post_mix_g), w_gates,
                  w_nsa_o[l].astype(BF16), w_moba_o[l].astype(BF16), w_mem_o[l].astype(BF16),
                  w_mix_out[l].astype(BF16))
        x2 = _ffn(x2, row(pre_ffn_g), row(post_ffn_g), w_ffn_gate[l].astype(BF16),
                  w_ffn_up[l].astype(BF16), w_ffn_down[l].astype(BF16))
    return x2.reshape(b, s, D_MODEL)
```

```python
import functools
import math

import numpy as np
import jax
import jax.numpy as jnp
from jax import lax
from jax.experimental import pallas as pl
from jax.experimental.pallas import tpu as pltpu

F32 = jnp.float32
BF16 = jnp.bfloat16

D_MODEL = 1024
HEAD_DIM = 64
SCALE = HEAD_DIM ** -0.5
LOG2E = math.log2(math.e)
Q_SCALE = SCALE * LOG2E
NSA_HEADS = 8
NSA_GROUPS = 2
NSA_HPG = NSA_HEADS // NSA_GROUPS
CMP_LEN = 32
CMP_STRIDE = 16
CMP_HIDDEN = 128
SEL_BLOCK = 64
SEL_TOPN = 8
WINDOW = 256
MOBA_HEADS = 4
MOBA_BLOCK = 256
MOBA_TOPK = 3
MEM_HEADS = 4
REL_BUCKETS = 32
REL_MAX_DIST = 128
N_BIAS_HEADS = NSA_HEADS + MOBA_HEADS
RMS_EPS = 1e-6
NEG_INF = -1e30
FORCE_SCORE = 1e4

NSA_Q_W = NSA_HEADS * HEAD_DIM
NSA_KV_W = NSA_GROUPS * HEAD_DIM
NSA_GATE_W = NSA_HEADS * 3
MOBA_W = MOBA_HEADS * HEAD_DIM
MEM_W = MEM_HEADS * HEAD_DIM
ATT_W = NSA_Q_W + 6 * NSA_KV_W + NSA_GATE_W + 3 * MOBA_W + MEM_W
LANES = 128
SUBLANES = 8
BF16_ROWS = 16
MXU_COLS = 256
GATE_PAD = LANES
TILE = 256
HALF = TILE // 2
N_CMP_PAD = 128
V_AUG = HEAD_DIM + BF16_ROWS
MASKED_BUCKET = REL_BUCKETS
VMEM_LIMIT = 56 * 1024 * 1024


def _dot(a, b):
    return jnp.dot(a, b, preferred_element_type=F32)


def _split_bf16(x):
    hi = x.astype(BF16)
    lo = (x - hi.astype(F32)).astype(BF16)
    return hi, lo


def _rms(x, g):
    return x * lax.rsqrt(jnp.mean(x * x, axis=-1, keepdims=True) + RMS_EPS) * g


def _params(sem):
    return pltpu.CompilerParams(dimension_semantics=sem, vmem_limit_bytes=VMEM_LIMIT)


def _const_spec(shape):
    nd = len(shape)
    return pl.BlockSpec(shape, lambda *_: (0,) * nd, pipeline_mode=pl.Buffered(1))


_INPROJ_OUTS = (
    ("qn", NSA_Q_W, BF16, True),
    ("kc", NSA_KV_W, F32, False), ("vc", NSA_KV_W, F32, False),
    ("ks", NSA_KV_W, BF16, False), ("vs", NSA_KV_W, BF16, False),
    ("kw", NSA_KV_W, BF16, False), ("vw", NSA_KV_W, BF16, False),
    ("gn", GATE_PAD, F32, False),
    ("qm", MOBA_W, BF16, True), ("km", MOBA_W, BF16, False), ("vm", MOBA_W, BF16, False),
    ("qx", MEM_W, BF16, True),
)
_INPROJ_W = sum(o[1] for o in _INPROJ_OUTS)
_INPROJ_CHUNKED = ("kc", "vc")
_INPROJ_TRANSPOSED = ("qn", "gn", "qm", "qx")
KS_AUG_W = NSA_GROUPS * 2 * HEAD_DIM


def _inproj_out_width(name, width):
    return KS_AUG_W if name == "ks" else width


def _inproj_kernel(x_ref, g_ref, w_ref, e_ref, *refs):
    out_refs, rows_sc = refs[:-1], refs[-1]
    h = _rms(x_ref[...], g_ref[...]).astype(BF16)
    runs, lo = [], 0
    for out in zip(_INPROJ_OUTS, out_refs):
        if runs and runs[-1][1] < MXU_COLS:
            runs[-1][0].append(out)
            runs[-1][1] += out[0][1]
        else:
            runs.append([[out], out[0][1], lo])
        lo += out[0][1]
    for outs, run_width, run_lo in runs:
        y_run = _dot(h, w_ref[:, run_lo:run_lo + run_width])
        lo = 0
        for (name, width, dtype, scaled), o_ref in outs:
            y = y_run[:, lo:lo + width]
            if scaled:
                y = y * Q_SCALE
            if name == "gn":
                y = jax.nn.sigmoid(y)
            if name in _INPROJ_TRANSPOSED:
                y = y.T
            y = y.astype(dtype)
            if name == "ks":
                e = e_ref[...]
                y = _lane_cat([y[:, :HEAD_DIM], e, y[:, HEAD_DIM:], e])
            if name in _INPROJ_CHUNKED:
                rows_sc[...] = y
                for j in range(CMP_STRIDE):
                    o_ref[:, j * width:(j + 1) * width] = rows_sc[pl.ds(j, o_ref.shape[0], stride=CMP_STRIDE), :]
            else:
                o_ref[...] = y
            lo += width


def _inproj(x2, g, w, e_cols, tm=512):
    m = x2.shape[0]
    tiles_per_seq = e_cols.shape[0] // tm
    out_specs, out_shape = [], []
    for name, width, dtype, _ in _INPROJ_OUTS:
        if name in _INPROJ_TRANSPOSED:
            out_specs.append(pl.BlockSpec((width, tm), lambda i: (0, i)))
            out_shape.append(jax.ShapeDtypeStruct((width, m), dtype))
            continue
        rows, width = (CMP_STRIDE, CMP_STRIDE * width) if name in _INPROJ_CHUNKED else (1, _inproj_out_width(name, width))
        out_specs.append(pl.BlockSpec((tm // rows, width), lambda i: (i, 0)))
        out_shape.append(jax.ShapeDtypeStruct((m // rows, width), dtype))
    return pl.pallas_call(
        _inproj_kernel,
        grid=(m // tm,),
        in_specs=[pl.BlockSpec((tm, D_MODEL), lambda i: (i, 0)),
                  _const_spec((1, D_MODEL)),
                  _const_spec((D_MODEL, _INPROJ_W)),
                  pl.BlockSpec((tm, HEAD_DIM), lambda i: (i % tiles_per_seq, 0))],
        out_specs=out_specs,
        out_shape=out_shape,
        scratch_shapes=[pltpu.VMEM((tm, NSA_KV_W), F32)],
        compiler_params=_params(("parallel",)),
        name="inproj",
    )(x2, g, w, e_cols)


def _compress_kernel(rk_ref, rv_ref, pk_ref, pv_ref, w1k_ref, w1v_ref, w2k_ref, w2v_ref, kc_ref, vc_ref):
    nb = rk_ref.shape[0]

    def one(r_ref, p_ref, w1_ref, w2_ref):
        r = r_ref[...].reshape(nb * N_CMP_PAD, r_ref.shape[2])
        top = _dot((r + p_ref[0:1, :]).astype(BF16), w1_ref[0])
        bot = _dot((r + p_ref[1:2, :]).astype(BF16), w1_ref[1])
        hid = top + pltpu.roll(bot, nb * N_CMP_PAD - 1, 0)
        act = jax.nn.gelu(hid).astype(BF16)
        return jnp.concatenate(
            [_dot(act[:, g * CMP_HIDDEN:(g + 1) * CMP_HIDDEN], w2_ref[...]) for g in range(NSA_GROUPS)], axis=1)

    k_out = one(rk_ref, pk_ref, w1k_ref, w2k_ref)
    v_out = one(rv_ref, pv_ref, w1v_ref, w2v_ref)
    for n in range(nb):
        rows = slice(n * N_CMP_PAD, (n + 1) * N_CMP_PAD)
        kc_ref[n] = k_out[rows].astype(BF16)
        vc_ref[n] = v_out[rows].T.astype(BF16)


def _compress(rk, rv, pk, pv, w1k, w1v, w2k, w2v):
    b = rk.shape[0]
    rw = rk.shape[2]
    nb = 2 if b % 2 == 0 else 1
    r_spec = pl.BlockSpec((nb, N_CMP_PAD, rw), lambda i: (i, 0, 0))
    o_spec = pl.BlockSpec((nb, N_CMP_PAD, NSA_KV_W), lambda i: (i, 0, 0))
    return pl.pallas_call(
        _compress_kernel,
        grid=(b // nb,),
        in_specs=[r_spec, r_spec, _const_spec(pk.shape), _const_spec(pv.shape),
                  _const_spec(w1k.shape), _const_spec(w1v.shape),
                  _const_spec(w2k.shape), _const_spec(w2v.shape)],
        out_specs=[o_spec, o_spec],
        out_shape=[jax.ShapeDtypeStruct((b, N_CMP_PAD, NSA_KV_W), BF16)] * 2,
        compiler_params=_params(("parallel",)),
        name="compress",
    )(rk, rv, pk, pv, w1k, w1v, w2k, w2v)


def _compress_weights(pos, w1):
    half = CMP_LEN // 2
    p = pos.reshape(2, half, 1, HEAD_DIM)
    p = jnp.broadcast_to(p, (2, half, NSA_GROUPS, HEAD_DIM)).reshape(2, half * NSA_KV_W)
    w = w1.reshape(2, half, HEAD_DIM, CMP_HIDDEN)
    eye = jnp.eye(NSA_GROUPS, dtype=w1.dtype)
    wbd = jnp.einsum("ajdm,gk->ajgdkm", w, eye).reshape(2, half * NSA_KV_W, NSA_GROUPS * CMP_HIDDEN)
    return p.astype(F32), wbd.astype(BF16)


def _memkv_kernel(m_ref, g_ref, w_ref, k_ref, v_ref):
    h = _rms(m_ref[...], g_ref[...]).astype(BF16)
    k_ref[...] = _dot(h, w_ref[:, :MEM_W]).astype(BF16)
    v_ref[...] = _dot(h, w_ref[:, MEM_W:]).astype(BF16)


def _memkv(mem2, g, w, tm=512):
    m = mem2.shape[0]
    o_spec = pl.BlockSpec((tm, MEM_W), lambda i: (i, 0))
    return pl.pallas_call(
        _memkv_kernel,
        grid=(m // tm,),
        in_specs=[pl.BlockSpec((tm, D_MODEL), lambda i: (i, 0)), _const_spec((1, D_MODEL)),
                  _const_spec((D_MODEL, 2 * MEM_W))],
        out_specs=[o_spec, o_spec],
        out_shape=[jax.ShapeDtypeStruct((m, MEM_W), BF16)] * 2,
        compiler_params=_params(("parallel",)),
        name="memkv",
    )(mem2, g, w)


def _expand_kernel(idx_ref, bias_ref, o_ref, *, head0, n_heads, heads_per_group):
    rows, cols = idx_ref.shape

    def body(i, carry):
        r = pl.multiple_of(i * SUBLANES, SUBLANES)
        for c0 in range(0, cols, TILE):
            idx = idx_ref[pl.ds(r, SUBLANES), c0:c0 + TILE]
            out = [jnp.full(idx.shape, NEG_INF, F32)] * n_heads
            for bkt in range(REL_BUCKETS):
                hit = idx == bkt
                out = [jnp.where(hit, bias_ref[bkt, head0 + h], out[h]) for h in range(n_heads)]
            for h in range(n_heads):
                col = (h % heads_per_group) * cols + c0
                o_ref[h // heads_per_group, pl.ds(r, SUBLANES), col:col + TILE] = out[h] * LOG2E
        return carry

    lax.fori_loop(0, rows // SUBLANES, body, 0)


def _expand(idx, rel_bias, head0, n_heads, heads_per_group=1):
    rows, cols = idx.shape
    return pl.pallas_call(
        functools.partial(_expand_kernel, head0=head0, n_heads=n_heads, heads_per_group=heads_per_group),
        in_specs=[pl.BlockSpec(memory_space=pltpu.VMEM), pl.BlockSpec(memory_space=pltpu.SMEM)],
        out_specs=pl.BlockSpec(memory_space=pltpu.VMEM),
        out_shape=jax.ShapeDtypeStruct((n_heads // heads_per_group, rows, heads_per_group * cols), F32),
        compiler_params=pltpu.CompilerParams(vmem_limit_bytes=VMEM_LIMIT),
        name="bias_expand",
    )(idx, rel_bias)


def _t5_bucket_np(dist):
    dist = np.maximum(dist, 0)
    max_exact = REL_BUCKETS // 2
    logd = np.log(np.maximum(dist, 1).astype(np.float32) / max_exact) / math.log(REL_MAX_DIST / max_exact)
    large = np.minimum(max_exact + (logd * (REL_BUCKETS - max_exact)).astype(np.int32), REL_BUCKETS - 1)
    return np.where(dist < max_exact, dist, large).astype(np.int32)


def _bucket_tables(s):
    j = np.arange(TILE)[:, None]
    i = np.arange(TILE)[None, :]
    assert TILE + 1 >= REL_MAX_DIST
    tiles = []
    for d in range(2):
        dist = d * TILE + i - j
        tiles.append(np.where(dist >= 0, _t5_bucket_np(dist), MASKED_BUCKET))
    dist1 = TILE + i - j
    win = np.where(dist1 < WINDOW, _t5_bucket_np(dist1), MASKED_BUCKET)
    n_cmp = (s - CMP_LEN) // CMP_STRIDE + 1
    assert n_cmp * CMP_STRIDE + CMP_LEN - 1 >= s or n_cmp == N_CMP_PAD
    rel = np.arange(N_CMP_PAD + (s - TILE) // CMP_STRIDE)[:, None] - (s - TILE) // CMP_STRIDE
    dist_c = i - (rel * CMP_STRIDE + CMP_LEN - 1)
    cmp_idx = np.where(dist_c >= 0, _t5_bucket_np(dist_c), MASKED_BUCKET)
    as_i32 = lambda a: jnp.asarray(a.astype(np.int32))
    return as_i32(np.concatenate(tiles, axis=0)), as_i32(win), as_i32(cmp_idx)


def _overlap_table(s):
    n_cmp = (s - CMP_LEN) // CMP_STRIDE + 1
    n_sel = s // SEL_BLOCK
    cs = np.arange(n_cmp) * CMP_STRIDE
    ss = np.arange(n_sel) * SEL_BLOCK
    ov = np.clip(np.minimum(cs[:, None] + CMP_LEN, ss[None, :] + SEL_BLOCK)
                 - np.maximum(cs[:, None], ss[None, :]), 0, None).astype(np.float32) / CMP_LEN
    ovt = np.zeros((n_sel, N_CMP_PAD), np.float32)
    ovt[:, :n_cmp] = ov.T
    return jnp.asarray(ovt, BF16)


def _store_v_aug(vt_sc, idx, vt):
    ones = jnp.ones((BF16_ROWS, vt.shape[1]), BF16)
    vt_sc[idx] = jnp.concatenate([vt.astype(BF16), ones], axis=0)


def _lane_cat(xs):
    return jnp.concatenate(xs, axis=1)


def _query_halves(x):
    n = x.shape[-1] // TILE
    first = _lane_cat([x[:, k * TILE:k * TILE + HALF] for k in range(n)])
    second = _lane_cat([x[:, k * TILE + HALF:(k + 1) * TILE] for k in range(n)])
    return first, second


def _join_query_halves(first, second):
    n = first.shape[-1] // HALF
    return _lane_cat([part for k in range(n)
                      for part in (first[:, k * HALF:(k + 1) * HALF], second[:, k * HALF:(k + 1) * HALF])])


def _triangle_tile(k, q, table, pv_lo, pv_hi, causal):
    q_first, q_second = _query_halves(q)
    lo, hi = (0, HALF), (HALF, TILE)
    if causal:
        s_wide = _dot(k(*lo), q) + table(*lo)
        s_narrow = _dot(k(*hi), q_second) + _query_halves(table(*hi))[1]
        pv_wide, pv_narrow = pv_lo, pv_hi
    else:
        s_wide = _dot(k(*hi), q) + table(*hi)
        s_narrow = _dot(k(*lo), q_first) + _query_halves(table(*lo))[0]
        pv_wide, pv_narrow = pv_hi, pv_lo
    mw_first, mw_second = _query_halves(jnp.max(s_wide, axis=0, keepdims=True))
    m_narrow = jnp.max(s_narrow, axis=0, keepdims=True)
    if causal:
        m_narrow = jnp.maximum(m_narrow, mw_second)
        m = _join_query_halves(mw_first, m_narrow)
    else:
        m_narrow = jnp.maximum(m_narrow, mw_first)
        m = _join_query_halves(m_narrow, mw_second)
    aw_first, aw_second = _query_halves(pv_wide(jnp.exp2(s_wide - m).astype(BF16)))
    a_narrow = pv_narrow(jnp.exp2(s_narrow - m_narrow).astype(BF16))
    if causal:
        return m, _join_query_halves(aw_first, aw_second + a_narrow)
    return m, _join_query_halves(aw_first + a_narrow, aw_second)


def _flash_pipelined(own, streams, s_sc, m_ref, acc_ref):
    has_prev = jnp.where(own > 0, 1.0, 0.0).astype(F32)
    prev = jnp.maximum(own - 1, 0)
    n_far = jnp.maximum(own - 1, 0)

    def absorb(g, s, kt, c_row, w_row):
        u = jnp.max(s, axis=0, keepdims=True) + c_row
        m_old = m_ref[g]
        m_new = jnp.maximum(m_old, jnp.where(w_row > 0.0, u, NEG_INF))
        alpha = jnp.exp2(m_old - m_new)
        shift = jnp.maximum(m_new, u) - c_row
        p = jnp.exp2(s - shift).astype(BF16)
        acc_ref[g] = alpha * acc_ref[g] + w_row * streams[g]["pv"](kt)(p)
        m_ref[g] = m_new

    def absorb_slot(g, i):
        is_prev = i == 0
        kt = jnp.where(is_prev, prev, i - 1)
        c_row = jnp.where(is_prev, 0.0, streams[g]["c_far"])
        w_row = streams[g]["w"](kt) * jnp.where(is_prev, has_prev, 1.0)
        absorb(g, s_sc[g], kt, c_row, w_row)

    for g, stream in enumerate(streams):
        m_ref[g], acc_ref[g] = stream["own"]()
    for g, stream in enumerate(streams):
        s_sc[g] = stream["prev"]()

    def body(i, carry):
        nxt = [stream["far"](i) for stream in streams]
        for g in range(len(streams)):
            absorb_slot(g, i)
        for g in range(len(streams)):
            s_sc[g] = nxt[g]
        return carry

    lax.fori_loop(0, n_far, body, 0)
    for g in range(len(streams)):
        absorb_slot(g, n_far)


def _softmax_av(s_list, pv_list):
    m = s_list[0].max(axis=0, keepdims=True)
    for s in s_list[1:]:
        m = jnp.maximum(m, s.max(axis=0, keepdims=True))
    acc = None
    for s, pv in zip(s_list, pv_list):
        part = pv(jnp.exp2(s - m).astype(BF16))
        acc = part if acc is None else acc + part
    return acc


def _normalize(acc):
    return acc[:HEAD_DIM] / acc[HEAD_DIM:HEAD_DIM + 1]


def _rank_before(score, n_cand):
    ranks = []
    for r0 in range(0, score.shape[0], SUBLANES):
        tile = score[r0:r0 + SUBLANES]
        blk = lax.broadcasted_iota(jnp.int32, tile.shape, 0) + r0
        rank = jnp.zeros(tile.shape, F32)
        for m in range(n_cand):
            row = score[m:m + 1, :]
            if m < r0:
                before = jnp.where(row >= tile, 1.0, 0.0)
            elif m >= r0 + SUBLANES:
                before = jnp.where(row > tile, 1.0, 0.0)
            else:
                before = jnp.where(blk > m, jnp.where(row >= tile, 1.0, 0.0), jnp.where(row > tile, 1.0, 0.0))
            rank = rank + before
        ranks.append(rank)
    return jnp.concatenate(ranks, axis=0)


def _nsa_steps(qi, q_ref, gn_ref, kc_ref, vct_ref, ks_ref, vs_ref, kw_ref, vw_ref,
               bct_ref, tt_ref, twt_ref, cfar_ref, ovt_ref, o_ref,
               vst_sc, vwt_sc, qa_sc, og_sc, acc_sc):
    nt = ks_ref.shape[1]
    n_sel = ovt_ref.shape[0]

    @pl.when(qi == 0)
    def _():
        for kt in range(nt):
            vs_t = vs_ref[0, kt].astype(F32).T
            vw_t = vw_ref[0, kt].astype(F32).T
            for g in range(NSA_GROUPS):
                _store_v_aug(vst_sc, (kt, g), vs_t[g * HEAD_DIM:(g + 1) * HEAD_DIM])
                _store_v_aug(vwt_sc, (kt, g), vw_t[g * HEAD_DIM:(g + 1) * HEAD_DIM])

    pos = lax.broadcasted_iota(jnp.int32, (1, TILE), 1) + qi * TILE
    cur = pos // SEL_BLOCK
    has_cmp = pos >= CMP_LEN - 1
    blk = lax.broadcasted_iota(jnp.int32, (n_sel, TILE), 0)
    prev = jnp.maximum(qi - 1, 0)
    no_prev = jnp.where(qi == 0, NEG_INF, 0.0).astype(F32)
    gates = gn_ref[...]

    gsls = [slice(g * HEAD_DIM, (g + 1) * HEAD_DIM) for g in range(NSA_GROUPS)]
    group_heads = [[g * NSA_HPG + j for j in range(NSA_HPG)] for g in range(NSA_GROUPS)]

    def gate(g, branch):
        return _lane_cat([gates[3 * h + branch:3 * h + branch + 1, :] for h in group_heads[g]])

    for g in range(NSA_GROUPS):
        heads = group_heads[g]
        q4 = _lane_cat([q_ref[h * HEAD_DIM:(h + 1) * HEAD_DIM, :] for h in heads])
        qa_sc[g, 0:HEAD_DIM, :] = q4
        qa_sc[g, HEAD_DIM + n_sel:, :] = jnp.zeros((HEAD_DIM - n_sel, NSA_HPG * TILE), BF16)

        def v_half(vt_sc, kt, lo, hi):
            return lambda pr: _dot(vt_sc[kt, g, :, lo:hi], pr)

        m_own, acc_own = _triangle_tile(
            lambda lo, hi: kw_ref[0, qi, lo:hi, gsls[g]], q4, lambda lo, hi: tt_ref[g, 0, lo:hi, :],
            v_half(vwt_sc, qi, 0, HALF), v_half(vwt_sc, qi, HALF, TILE), causal=True)
        m_prev, acc_prev = _triangle_tile(
            lambda lo, hi: kw_ref[0, prev, lo:hi, gsls[g]], q4, lambda lo, hi: twt_ref[g, lo:hi, :] + no_prev,
            v_half(vwt_sc, prev, 0, HALF), v_half(vwt_sc, prev, HALF, TILE), causal=False)
        m_win = jnp.maximum(m_own, m_prev)
        acc_w = acc_own * jnp.exp2(m_own - m_win) + acc_prev * jnp.exp2(m_prev - m_win)
        o_win = gate(g, 2) * _normalize(acc_w)

        kc = kc_ref[0, :, gsls[g]]
        vct = vct_ref[0, gsls[g], :]
        rows_per_tile = TILE // CMP_STRIDE
        c_rows = pl.ds(pl.multiple_of((nt - 1 - qi) * rows_per_tile, rows_per_tile), N_CMP_PAD)
        s = _dot(kc, q4) + _lane_cat([bct_ref[h, c_rows, :] for h in heads])
        e = jnp.exp2(s - jnp.max(s, axis=0, keepdims=True))
        p = jnp.where(_lane_cat([has_cmp] * NSA_HPG), e / jnp.sum(e, axis=0, keepdims=True), 0.0)
        psum = p[:, :TILE]
        for j in range(1, NSA_HPG):
            psum = psum + p[:, j * TILE:(j + 1) * TILE]
        og_sc[g] = gate(g, 0) * _dot(vct, p.astype(BF16)) + o_win

        p_hi, p_lo = _split_bf16(psum)
        imp = _dot(ovt_ref[...], p_hi) + _dot(ovt_ref[...], p_lo)
        forced = (blk == 0) | (blk == cur) | (blk == cur - 1)
        score = jnp.where(forced, FORCE_SCORE, jnp.where(blk <= cur, imp, NEG_INF))
        rank = _rank_before(score, n_sel)
        sel = jnp.where(rank < SEL_TOPN, jnp.where(score > NEG_INF / 2, 0.0, NEG_INF), NEG_INF)
        qa_sc[g, HEAD_DIM:HEAD_DIM + n_sel, :] = _lane_cat([sel.astype(BF16)] * NSA_HPG)

    ones_row = jnp.ones((1, NSA_HPG * TILE), F32)

    def sel_stream(g):
        def qk(kt):
            return _dot(ks_ref[0, kt, :, g * 2 * HEAD_DIM:(g + 1) * 2 * HEAD_DIM], qa_sc[g])

        def own():
            return _triangle_tile(
                lambda lo, hi: ks_ref[0, qi, lo:hi, g * 2 * HEAD_DIM:(g + 1) * 2 * HEAD_DIM], qa_sc[g],
                lambda lo, hi: tt_ref[g, 0, lo:hi, :],
                lambda pr: _dot(vst_sc[qi, g, :, 0:HALF], pr), lambda pr: _dot(vst_sc[qi, g, :, HALF:TILE], pr),
                causal=True)

        return dict(own=own, prev=lambda: qk(prev) + tt_ref[g, 1], far=qk,
                    c_far=cfar_ref[g], w=lambda kt: ones_row,
                    pv=lambda kt: (lambda pr: _dot(vst_sc[kt, g], pr)))

    yield [sel_stream(g) for g in range(NSA_GROUPS)]

    for g in range(NSA_GROUPS):
        o = og_sc[g] + gate(g, 1) * _normalize(acc_sc[g])
        for j, h in enumerate(group_heads[g]):
            o_ref[h * HEAD_DIM:(h + 1) * HEAD_DIM, :] = o[:, j * TILE:(j + 1) * TILE].astype(BF16)


def _moba_steps(c, qm_ref, km_ref, vm_ref, qx_ref, mk_ref, mv_ref, tt_ref, cfar_ref, om_ref, ox_ref,
                vmt_sc, mvt_sc, kmean_sc, qbd_sc, sel_sc, acc_sc, slot):
    nt = km_ref.shape[1]
    hsls = [slice(h * HEAD_DIM, (h + 1) * HEAD_DIM) for h in range(MOBA_HEADS)]

    @pl.when(c == 0)
    def _():
        kmean_sc[...] = jnp.zeros(kmean_sc.shape, F32)
        for n in range(nt):
            kmean_sc[n:n + 1, :] = jnp.sum(km_ref[0, n].astype(F32), axis=0, keepdims=True) * (1.0 / MOBA_BLOCK)
            vt = vm_ref[0, n].astype(F32).T
            for h in range(MOBA_HEADS):
                _store_v_aug(vmt_sc, (n, h), vt[hsls[h]])
        mvt = mv_ref[0].astype(F32).T
        for h in range(MOBA_HEADS):
            _store_v_aug(mvt_sc, h, mvt[hsls[h]])

    row_head = lax.broadcasted_iota(jnp.int32, (MOBA_W, TILE), 0) // HEAD_DIM

    def block_diag(q_ref):
        q_t = q_ref[...].astype(F32)
        return _lane_cat([jnp.where(row_head == h, q_t, 0.0) for h in range(MOBA_HEADS)]).astype(BF16)

    def per_head_pv(vts):
        return lambda pr: _lane_cat([_dot(vts(h), pr[:, h * TILE:(h + 1) * TILE]) for h in range(MOBA_HEADS)])

    def store_heads(o_t, out_ref):
        for h in range(MOBA_HEADS):
            out_ref[hsls[h], :] = o_t[:, h * TILE:(h + 1) * TILE].astype(BF16)

    qbd = block_diag(qm_ref)
    km_hi, km_lo = _split_bf16(kmean_sc[...])
    n_rows = -(-nt // SUBLANES) * SUBLANES
    gate = (_dot(km_hi, qbd) + _dot(km_lo, qbd))[:n_rows]
    blk = lax.broadcasted_iota(jnp.int32, gate.shape, 0)
    score = jnp.where(blk < c, gate, NEG_INF * Q_SCALE)
    rank = _rank_before(score, nt)
    sel_sc[0:n_rows, :] = jnp.where(rank < MOBA_TOPK, jnp.where(score > NEG_INF * Q_SCALE / 2, 1.0, 0.0), 0.0)

    qbd_sc[...] = qbd
    qk = lambda n: _dot(km_ref[0, n], qbd_sc[...])
    def own_pv(lo, hi):
        def pv(pr):
            width = pr.shape[1] // MOBA_HEADS
            return _lane_cat([_dot(vmt_sc[c, h, :, lo:hi], pr[:, h * width:(h + 1) * width])
                              for h in range(MOBA_HEADS)])
        return pv

    def own():
        return _triangle_tile(lambda lo, hi: km_ref[0, c, lo:hi, :], qbd_sc[...], lambda lo, hi: tt_ref[0, lo:hi, :],
                              own_pv(0, HALF), own_pv(HALF, TILE), causal=True)

    stream = dict(own=own, prev=lambda: qk(jnp.maximum(c - 1, 0)) + tt_ref[1], far=qk,
                  c_far=cfar_ref[...], w=lambda n: sel_sc[pl.ds(n, 1), :],
                  pv=lambda n: per_head_pv(lambda h: vmt_sc[n, h]))
    s = _dot(mk_ref[0], block_diag(qx_ref))
    store_heads(_normalize(_softmax_av([s], [per_head_pv(lambda h: mvt_sc[h])])), ox_ref)

    yield [stream]
    store_heads(_normalize(acc_sc[slot]), om_ref)


N_NSA_IN, N_MOBA_IN = 13, 8
N_NSA_SCRATCH, N_MOBA_SCRATCH = 4, 5
N_STREAMS = NSA_GROUPS + 1


def _attention_kernel(*refs):
    qi = pl.program_id(1)
    nsa_in, refs = refs[:N_NSA_IN], refs[N_NSA_IN:]
    moba_in, refs = refs[:N_MOBA_IN], refs[N_MOBA_IN:]
    (o_nsa, o_moba, o_mem), refs = refs[:3], refs[3:]
    nsa_sc, refs = refs[:N_NSA_SCRATCH], refs[N_NSA_SCRATCH:]
    moba_sc, refs = refs[:N_MOBA_SCRATCH], refs[N_MOBA_SCRATCH:]
    s_sc, m_sc, acc_sc = refs
    nsa = _nsa_steps(qi, *nsa_in, o_nsa, *nsa_sc, acc_sc)
    moba = _moba_steps(qi, *moba_in, o_moba, o_mem, *moba_sc, acc_sc, NSA_GROUPS)
    streams = next(nsa) + next(moba)
    _flash_pipelined(qi, streams, s_sc, m_sc, acc_sc)
    for steps in (nsa, moba):
        for _ in steps:
            pass


def _attention(b, s, qn, gn, kc, vct, ks, vs, kw, vw, bias_cmp, t_nsa, t_win, c_far_nsa, ovt,
               qm, km, vm, qx, mk, mv, t_moba, c_far_moba):
    nt = s // TILE
    mem_len = mk.shape[0] // b
    assert MOBA_TOPK <= nt - 1 and nt <= BF16_ROWS
    n_lanes = NSA_HPG * TILE
    assert MOBA_HEADS * TILE == n_lanes
    row_spec = lambda w: pl.BlockSpec((w, TILE), lambda i, j: (0, i * nt + j))
    seq_spec = lambda w: pl.BlockSpec((1, nt, TILE, w), lambda i, j: (i, 0, 0, 0))
    per_batch = lambda rows, w: pl.BlockSpec((1, rows, w), lambda i, j: (i, 0, 0))
    tiles = lambda a: a.reshape(b, nt, TILE, a.shape[-1])
    nsa_in = [(qn, row_spec(NSA_Q_W)), (gn, row_spec(GATE_PAD)),
              (kc, per_batch(N_CMP_PAD, NSA_KV_W)), (vct, per_batch(N_CMP_PAD, NSA_KV_W)),
              (tiles(ks), seq_spec(KS_AUG_W)), (tiles(vs), seq_spec(NSA_KV_W)),
              (tiles(kw), seq_spec(NSA_KV_W)), (tiles(vw), seq_spec(NSA_KV_W)),
              (bias_cmp, _const_spec(bias_cmp.shape)), (t_nsa, _const_spec(t_nsa.shape)),
              (t_win, _const_spec(t_win.shape)), (c_far_nsa, _const_spec(c_far_nsa.shape)),
              (ovt, _const_spec(ovt.shape))]
    moba_in = [(qm, row_spec(MOBA_W)), (tiles(km), seq_spec(MOBA_W)), (tiles(vm), seq_spec(MOBA_W)),
               (qx, row_spec(MEM_W)),
               (mk.reshape(b, mem_len, MEM_W), per_batch(mem_len, MEM_W)),
               (mv.reshape(b, mem_len, MEM_W), per_batch(mem_len, MEM_W)),
               (t_moba, _const_spec(t_moba.shape)), (c_far_moba, _const_spec(c_far_moba.shape))]
    assert len(nsa_in) == N_NSA_IN and len(moba_in) == N_MOBA_IN
    nsa_scratch = [pltpu.VMEM((nt, NSA_GROUPS, V_AUG, TILE), BF16),
                   pltpu.VMEM((nt, NSA_GROUPS, V_AUG, TILE), BF16),
                   pltpu.VMEM((NSA_GROUPS, 2 * HEAD_DIM, n_lanes), BF16),
                   pltpu.VMEM((NSA_GROUPS, HEAD_DIM, n_lanes), F32)]
    moba_scratch = [pltpu.VMEM((nt, MOBA_HEADS, V_AUG, TILE), BF16),
                    pltpu.VMEM((MEM_HEADS, V_AUG, mem_len), BF16),
                    pltpu.VMEM((BF16_ROWS, MOBA_W), F32),
                    pltpu.VMEM((MOBA_W, n_lanes), BF16),
                    pltpu.VMEM((BF16_ROWS, n_lanes), F32)]
    assert len(nsa_scratch) == N_NSA_SCRATCH and len(moba_scratch) == N_MOBA_SCRATCH
    shared_scratch = [pltpu.VMEM((N_STREAMS, TILE, n_lanes), F32),
                      pltpu.VMEM((N_STREAMS, 1, n_lanes), F32),
                      pltpu.VMEM((N_STREAMS, V_AUG, n_lanes), F32)]
    inputs = nsa_in + moba_in
    return pl.pallas_call(
        _attention_kernel,
        grid=(b, nt),
        in_specs=[spec for _, spec in inputs],
        out_specs=[row_spec(NSA_Q_W), row_spec(MOBA_W), row_spec(MEM_W)],
        out_shape=[jax.ShapeDtypeStruct((w, b * s), BF16) for w in (NSA_Q_W, MOBA_W, MEM_W)],
        scratch_shapes=nsa_scratch + moba_scratch + shared_scratch,
        compiler_params=_params(("arbitrary", "arbitrary")),
        name="attention",
    )(*[a for a, _ in inputs])


def _mix_kernel(x_ref, on_ref, om_ref, ox_ref, g_pre_ref, g_post_ref, wg_ref, wn_ref, wm_ref, wx_ref,
                wo_ref, o_ref):
    x = x_ref[...]
    h = _rms(x, g_pre_ref[...]).astype(BF16)
    rows = lambda o_ref: o_ref[...].astype(F32).T.astype(BF16)
    merged = jax.nn.sigmoid(_dot(h, wg_ref[:, :D_MODEL])) * _dot(rows(on_ref), wn_ref[...])
    merged = merged + jax.nn.sigmoid(_dot(h, wg_ref[:, D_MODEL:2 * D_MODEL])) * _dot(rows(om_ref), wm_ref[...])
    merged = merged + jax.nn.sigmoid(_dot(h, wg_ref[:, 2 * D_MODEL:])) * _dot(rows(ox_ref), wx_ref[...])
    y = _dot(merged.astype(BF16), wo_ref[...])
    o_ref[...] = x + _rms(y, g_post_ref[...])


def _mix(x2, o_nsa, o_moba, o_mem, g_pre, g_post, w_gates, w_nsa_o, w_moba_o, w_mem_o, w_mix_out, tm=512):
    m = x2.shape[0]
    row = lambda w: pl.BlockSpec((tm, w), lambda i: (i, 0))
    col = lambda w: pl.BlockSpec((w, tm), lambda i: (0, i))
    return pl.pallas_call(
        _mix_kernel,
        grid=(m // tm,),
        in_specs=[row(D_MODEL), col(NSA_Q_W), col(MOBA_W), col(MEM_W),
                  _const_spec((1, D_MODEL)), _const_spec((1, D_MODEL)),
                  _const_spec(w_gates.shape), _const_spec(w_nsa_o.shape), _const_spec(w_moba_o.shape),
                  _const_spec(w_mem_o.shape), _const_spec(w_mix_out.shape)],
        out_specs=row(D_MODEL),
        out_shape=jax.ShapeDtypeStruct((m, D_MODEL), F32),
        compiler_params=_params(("parallel",)),
        name="mix",
    )(x2, o_nsa, o_moba, o_mem, g_pre, g_post, w_gates, w_nsa_o, w_moba_o, w_mem_o, w_mix_out)


FFN_CHUNK = 256


def _ffn_kernel(x_ref, g_pre_ref, g_post_ref, wg_ref, wu_ref, wd_ref, o_ref, a_sc):
    x = x_ref[...]
    h = _rms(x, g_pre_ref[...]).astype(BF16)
    d_ff = wg_ref.shape[1]
    for j in range(d_ff // FFN_CHUNK):
        sl = slice(j * FFN_CHUNK, (j + 1) * FFN_CHUNK)
        a_sc[:, sl] = (jax.nn.silu(_dot(h, wg_ref[:, sl])) * _dot(h, wu_ref[:, sl])).astype(BF16)
    f = _dot(a_sc[...], wd_ref[...])
    o_ref[...] = x + _rms(f, g_post_ref[...])


def _ffn(x2, g_pre, g_post, wg, wu, wd, tm=512):
    m = x2.shape[0]
    d_ff = wg.shape[1]
    return pl.pallas_call(
        _ffn_kernel,
        grid=(m // tm,),
        in_specs=[pl.BlockSpec((tm, D_MODEL), lambda i: (i, 0)),
                  _const_spec((1, D_MODEL)), _const_spec((1, D_MODEL)),
                  _const_spec(wg.shape), _const_spec(wu.shape), _const_spec(wd.shape)],
        out_specs=pl.BlockSpec((tm, D_MODEL), lambda i: (i, 0)),
        out_shape=jax.ShapeDtypeStruct((m, D_MODEL), F32),
        scratch_shapes=[pltpu.VMEM((tm, d_ff), BF16)],
        compiler_params=_params(("parallel",)),
        name="ffn",
    )(x2, g_pre, g_post, wg, wu, wd)


def kernel(x, mem, rel_bias, pre_mix_g, mem_norm_g, post_mix_g, w_in, cmp_pos_k, cmp_w1_k, cmp_w2_k, cmp_pos_v, cmp_w1_v, cmp_w2_v, w_mem_kv, w_nsa_o, w_moba_o, w_mem_o, w_mix_out, pre_ffn_g, post_ffn_g, w_ffn_gate, w_ffn_up, w_ffn_down):
    b, s, d_model = x.shape
    depth = w_in.shape[0]
    assert d_model == D_MODEL and s % TILE == 0 and TILE == MOBA_BLOCK == WINDOW
    assert (s - CMP_LEN) // CMP_STRIDE + 1 < N_CMP_PAD and (s // SEL_BLOCK) % SUBLANES == 0 and s // SEL_BLOCK <= HEAD_DIM
    assert w_in.shape[2] == ATT_W + 3 * D_MODEL and rel_bias.shape == (REL_BUCKETS, N_BIAS_HEADS)

    tile_idx, win_idx, cmp_idx = _bucket_tables(s)
    rel_bias = rel_bias.astype(F32)
    t_nsa = _expand(tile_idx, rel_bias, 0, NSA_HEADS, NSA_HPG).reshape(NSA_GROUPS, 2, TILE, NSA_HPG * TILE)
    t_moba = _expand(tile_idx, rel_bias, NSA_HEADS, MOBA_HEADS, MOBA_HEADS).reshape(2, TILE, MOBA_HEADS * TILE)
    t_win = _expand(win_idx, rel_bias, 0, NSA_HEADS, NSA_HPG)
    b_cmp = _expand(cmp_idx, rel_bias, 0, NSA_HEADS)
    c_far = jnp.repeat(rel_bias[REL_BUCKETS - 1] * LOG2E, TILE)
    c_far_nsa = c_far[:NSA_HEADS * TILE].reshape(NSA_GROUPS, 1, NSA_HPG * TILE)
    c_far_moba = c_far[NSA_HEADS * TILE:].reshape(1, MOBA_HEADS * TILE)
    ovt = _overlap_table(s)
    sel_cols = np.zeros((s, HEAD_DIM), np.float32)
    sel_cols[np.arange(s), np.arange(s) // SEL_BLOCK] = 1.0
    sel_cols = jnp.asarray(sel_cols, BF16)
    gate_lo = NSA_Q_W + 6 * NSA_KV_W
    rows_per_chunk = CMP_STRIDE * NSA_KV_W

    x2 = x.reshape(b * s, D_MODEL)
    mem2 = mem.reshape(-1, D_MODEL)
    for l in range(depth):
        w_att = jnp.concatenate(
            [w_in[l, :, :gate_lo + NSA_GATE_W],
             jnp.zeros((D_MODEL, GATE_PAD - NSA_GATE_W), w_in.dtype),
             w_in[l, :, gate_lo + NSA_GATE_W:ATT_W]], axis=1).astype(BF16)
        w_gates = w_in[l, :, ATT_W:].astype(BF16)
        row = lambda v: v[l].reshape(1, D_MODEL)

        qn, kc_raw, vc_raw, ks, vs, kw, vw, gn, qm, km, vm, qx = _inproj(x2, row(pre_mix_g), w_att, sel_cols)

        pk, w1k = _compress_weights(cmp_pos_k[l], cmp_w1_k[l])
        pv, w1v = _compress_weights(cmp_pos_v[l], cmp_w1_v[l])
        kc, vct = _compress(kc_raw.reshape(b, s // CMP_STRIDE, rows_per_chunk),
                            vc_raw.reshape(b, s // CMP_STRIDE, rows_per_chunk),
                            pk, pv, w1k, w1v, cmp_w2_k[l].astype(BF16), cmp_w2_v[l].astype(BF16))

        mk, mv = _memkv(mem2, row(mem_norm_g), w_mem_kv[l].astype(BF16))

        o_nsa, o_moba, o_mem = _attention(b, s, qn, gn, kc, vct, ks, vs, kw, vw, b_cmp, t_nsa, t_win, c_far_nsa, ovt,
                                          qm, km, vm, qx, mk, mv, t_moba, c_far_moba)

        x2 = _mix(x2, o_nsa, o_moba, o_mem, row(pre_mix_g), row(post_mix_g), w_gates,
                  w_nsa_o[l].astype(BF16), w_moba_o[l].astype(BF16), w_mem_o[l].astype(BF16),
                  w_mix_out[l].astype(BF16))
        x2 = _ffn(x2, row(pre_ffn_g), row(post_ffn_g), w_ffn_gate[l].astype(BF16),
                  w_ffn_up[l].astype(BF16), w_ffn_down[l].astype(BF16))
    return x2.reshape(b, s, D_MODEL)
```

```python
import functools
import math

import numpy as np
import jax
import jax.numpy as jnp
from jax import lax
from jax.experimental import pallas as pl
from jax.experimental.pallas import tpu as pltpu

F32 = jnp.float32
BF16 = jnp.bfloat16

D_MODEL = 1024
HEAD_DIM = 64
SCALE = HEAD_DIM ** -0.5
LOG2E = math.log2(math.e)
Q_SCALE = SCALE * LOG2E
NSA_HEADS = 8
NSA_GROUPS = 2
NSA_HPG = NSA_HEADS // NSA_GROUPS
CMP_LEN = 32
CMP_STRIDE = 16
CMP_HIDDEN = 128
SEL_BLOCK = 64
SEL_TOPN = 8
WINDOW = 256
MOBA_HEADS = 4
MOBA_BLOCK = 256
MOBA_TOPK = 3
MEM_HEADS = 4
REL_BUCKETS = 32
REL_MAX_DIST = 128
N_BIAS_HEADS = NSA_HEADS + MOBA_HEADS
RMS_EPS = 1e-6
NEG_INF = -1e30
FORCE_SCORE = 1e4

NSA_Q_W = NSA_HEADS * HEAD_DIM
NSA_KV_W = NSA_GROUPS * HEAD_DIM
NSA_GATE_W = NSA_HEADS * 3
MOBA_W = MOBA_HEADS * HEAD_DIM
MEM_W = MEM_HEADS * HEAD_DIM
ATT_W = NSA_Q_W + 6 * NSA_KV_W + NSA_GATE_W + 3 * MOBA_W + MEM_W
LANES = 128
SUBLANES = 8
BF16_ROWS = 16
MXU_COLS = 256
GATE_PAD = LANES
TILE = 256
HALF = TILE // 2
N_CMP_PAD = 128
V_AUG = HEAD_DIM + BF16_ROWS
MASKED_BUCKET = REL_BUCKETS
VMEM_LIMIT = 56 * 1024 * 1024


def _dot(a, b):
    return jnp.dot(a, b, preferred_element_type=F32)


def _split_bf16(x):
    hi = x.astype(BF16)
    lo = (x - hi.astype(F32)).astype(BF16)
    return hi, lo


def _rms(x, g):
    return x * lax.rsqrt(jnp.mean(x * x, axis=-1, keepdims=True) + RMS_EPS) * g


def _params(sem):
    return pltpu.CompilerParams(dimension_semantics=sem, vmem_limit_bytes=VMEM_LIMIT)


def _const_spec(shape):
    nd = len(shape)
    return pl.BlockSpec(shape, lambda *_: (0,) * nd, pipeline_mode=pl.Buffered(1))


_INPROJ_OUTS = (
    ("qn", NSA_Q_W, BF16, True),
    ("kc", NSA_KV_W, F32, False), ("vc", NSA_KV_W, F32, False),
    ("ks", NSA_KV_W, BF16, False), ("vs", NSA_KV_W, BF16, False),
    ("kw", NSA_KV_W, BF16, False), ("vw", NSA_KV_W, BF16, False),
    ("gn", GATE_PAD, F32, False),
    ("qm", MOBA_W, BF16, True), ("km", MOBA_W, BF16, False), ("vm", MOBA_W, BF16, False),
    ("qx", MEM_W, BF16, True),
)
_INPROJ_W = sum(o[1] for o in _INPROJ_OUTS)
_INPROJ_CHUNKED = ("kc", "vc")
_INPROJ_TRANSPOSED = ("qn", "gn", "qm", "qx")
KS_AUG_W = NSA_GROUPS * 2 * HEAD_DIM


def _inproj_out_width(name, width):
    return KS_AUG_W if name == "ks" else width


def _inproj_kernel(x_ref, g_ref, w_ref, e_ref, *refs):
    out_refs, rows_sc = refs[:-1], refs[-1]
    h = _rms(x_ref[...], g_ref[...]).astype(BF16)
    runs, lo = [], 0
    for out in zip(_INPROJ_OUTS, out_refs):
        if runs and runs[-1][1] < MXU_COLS:
            runs[-1][0].append(out)
            runs[-1][1] += out[0][1]
        else:
            runs.append([[out], out[0][1], lo])
        lo += out[0][1]
    for outs, run_width, run_lo in runs:
        y_run = _dot(h, w_ref[:, run_lo:run_lo + run_width])
        lo = 0
        for (name, width, dtype, scaled), o_ref in outs:
            y = y_run[:, lo:lo + width]
            if scaled:
                y = y * Q_SCALE
            if name == "gn":
                y = jax.nn.sigmoid(y)
            if name in _INPROJ_TRANSPOSED:
                y = y.T
            y = y.astype(dtype)
            if name == "ks":
                e = e_ref[...]
                y = _lane_cat([y[:, :HEAD_DIM], e, y[:, HEAD_DIM:], e])
            if name in _INPROJ_CHUNKED:
                rows_sc[...] = y
                for j in range(CMP_STRIDE):
                    o_ref[:, j * width:(j + 1) * width] = rows_sc[pl.ds(j, o_ref.shape[0], stride=CMP_STRIDE), :]
            else:
                o_ref[...] = y
            lo += width


def _inproj(x2, g, w, e_cols, tm=512):
    m = x2.shape[0]
    tiles_per_seq = e_cols.shape[0] // tm
    out_specs, out_shape = [], []
    for name, width, dtype, _ in _INPROJ_OUTS:
        if name in _INPROJ_TRANSPOSED:
            out_specs.append(pl.BlockSpec((width, tm), lambda i: (0, i)))
            out_shape.append(jax.ShapeDtypeStruct((width, m), dtype))
            continue
        rows, width = (CMP_STRIDE, CMP_STRIDE * width) if name in _INPROJ_CHUNKED else (1, _inproj_out_width(name, width))
        out_specs.append(pl.BlockSpec((tm // rows, width), lambda i: (i, 0)))
        out_shape.append(jax.ShapeDtypeStruct((m // rows, width), dtype))
    return pl.pallas_call(
        _inproj_kernel,
        grid=(m // tm,),
        in_specs=[pl.BlockSpec((tm, D_MODEL), lambda i: (i, 0)),
                  _const_spec((1, D_MODEL)),
                  _const_spec((D_MODEL, _INPROJ_W)),
                  pl.BlockSpec((tm, HEAD_DIM), lambda i: (i % tiles_per_seq, 0))],
        out_specs=out_specs,
        out_shape=out_shape,
        scratch_shapes=[pltpu.VMEM((tm, NSA_KV_W), F32)],
        compiler_params=_params(("parallel",)),
        name="inproj",
    )(x2, g, w, e_cols)


def _compress_kernel(rk_ref, rv_ref, pk_ref, pv_ref, w1k_ref, w1v_ref, w2k_ref, w2v_ref, kc_ref, vc_ref):
    nb = rk_ref.shape[0]

    def one(r_ref, p_ref, w1_ref, w2_ref):
        r = r_ref[...].reshape(nb * N_CMP_PAD, r_ref.shape[2])
        top = _dot((r + p_ref[0:1, :]).astype(BF16), w1_ref[0])
        bot = _dot((r + p_ref[1:2, :]).astype(BF16), w1_ref[1])
        hid = top + pltpu.roll(bot, nb * N_CMP_PAD - 1, 0)
        act = jax.nn.gelu(hid).astype(BF16)
        return jnp.concatenate(
            [_dot(act[:, g * CMP_HIDDEN:(g + 1) * CMP_HIDDEN], w2_ref[...]) for g in range(NSA_GROUPS)], axis=1)

    k_out = one(rk_ref, pk_ref, w1k_ref, w2k_ref)
    v_out = one(rv_ref, pv_ref, w1v_ref, w2v_ref)
    for n in range(nb):
        rows = slice(n * N_CMP_PAD, (n + 1) * N_CMP_PAD)
        kc_ref[n] = k_out[rows].astype(BF16)
        vc_ref[n] = v_out[rows].T.astype(BF16)


def _compress(rk, rv, pk, pv, w1k, w1v, w2k, w2v):
    b = rk.shape[0]
    rw = rk.shape[2]
    nb = 2 if b % 2 == 0 else 1
    r_spec = pl.BlockSpec((nb, N_CMP_PAD, rw), lambda i: (i, 0, 0))
    o_spec = pl.BlockSpec((nb, N_CMP_PAD, NSA_KV_W), lambda i: (i, 0, 0))
    return pl.pallas_call(
        _compress_kernel,
        grid=(b // nb,),
        in_specs=[r_spec, r_spec, _const_spec(pk.shape), _const_spec(pv.shape),
                  _const_spec(w1k.shape), _const_spec(w1v.shape),
                  _const_spec(w2k.shape), _const_spec(w2v.shape)],
        out_specs=[o_spec, o_spec],
        out_shape=[jax.ShapeDtypeStruct((b, N_CMP_PAD, NSA_KV_W), BF16)] * 2,
        compiler_params=_params(("parallel",)),
        name="compress",
    )(rk, rv, pk, pv, w1k, w1v, w2k, w2v)


def _compress_weights(pos, w1):
    half = CMP_LEN // 2
    p = pos.reshape(2, half, 1, HEAD_DIM)
    p = jnp.broadcast_to(p, (2, half, NSA_GROUPS, HEAD_DIM)).reshape(2, half * NSA_KV_W)
    w = w1.reshape(2, half, HEAD_DIM, CMP_HIDDEN)
    eye = jnp.eye(NSA_GROUPS, dtype=w1.dtype)
    wbd = jnp.einsum("ajdm,gk->ajgdkm", w, eye).reshape(2, half * NSA_KV_W, NSA_GROUPS * CMP_HIDDEN)
    return p.astype(F32), wbd.astype(BF16)


def _memkv_kernel(m_ref, g_ref, w_ref, k_ref, v_ref):
    h = _rms(m_ref[...], g_ref[...]).astype(BF16)
    k_ref[...] = _dot(h, w_ref[:, :MEM_W]).astype(BF16)
    v_ref[...] = _dot(h, w_ref[:, MEM_W:]).astype(BF16)


def _memkv(mem2, g, w, tm=512):
    m = mem2.shape[0]
    o_spec = pl.BlockSpec((tm, MEM_W), lambda i: (i, 0))
    return pl.pallas_call(
        _memkv_kernel,
        grid=(m // tm,),
        in_specs=[pl.BlockSpec((tm, D_MODEL), lambda i: (i, 0)), _const_spec((1, D_MODEL)),
                  _const_spec((D_MODEL, 2 * MEM_W))],
        out_specs=[o_spec, o_spec],
        out_shape=[jax.ShapeDtypeStruct((m, MEM_W), BF16)] * 2,
        compiler_params=_params(("parallel",)),
        name="memkv",
    )(mem2, g, w)


def _expand_kernel(idx_ref, bias_ref, o_ref, *, head0, n_heads, heads_per_group):
    rows, cols = idx_ref.shape

    def body(i, carry):
        r = pl.multiple_of(i * SUBLANES, SUBLANES)
        for c0 in range(0, cols, TILE):
            idx = idx_ref[pl.ds(r, SUBLANES), c0:c0 + TILE]
            out = [jnp.full(idx.shape, NEG_INF, F32)] * n_heads
            for bkt in range(REL_BUCKETS):
                hit = idx == bkt
                out = [jnp.where(hit, bias_ref[bkt, head0 + h], out[h]) for h in range(n_heads)]
            for h in range(n_heads):
                col = (h % heads_per_group) * cols + c0
                o_ref[h // heads_per_group, pl.ds(r, SUBLANES), col:col + TILE] = out[h] * LOG2E
        return carry

    lax.fori_loop(0, rows // SUBLANES, body, 0)


def _expand(idx, rel_bias, head0, n_heads, heads_per_group=1):
    rows, cols = idx.shape
    return pl.pallas_call(
        functools.partial(_expand_kernel, head0=head0, n_heads=n_heads, heads_per_group=heads_per_group),
        in_specs=[pl.BlockSpec(memory_space=pltpu.VMEM), pl.BlockSpec(memory_space=pltpu.SMEM)],
        out_specs=pl.BlockSpec(memory_space=pltpu.VMEM),
        out_shape=jax.ShapeDtypeStruct((n_heads // heads_per_group, rows, heads_per_group * cols), F32),
        compiler_params=pltpu.CompilerParams(vmem_limit_bytes=VMEM_LIMIT),
        name="bias_expand",
    )(idx, rel_bias)


def _t5_bucket_np(dist):
    dist = np.maximum(dist, 0)
    max_exact = REL_BUCKETS // 2
    logd = np.log(np.maximum(dist, 1).astype(np.float32) / max_exact) / math.log(REL_MAX_DIST / max_exact)
    large = np.minimum(max_exact + (logd * (REL_BUCKETS - max_exact)).astype(np.int32), REL_BUCKETS - 1)
    return np.where(dist < max_exact, dist, large).astype(np.int32)


def _bucket_tables(s):
    j = np.arange(TILE)[:, None]
    i = np.arange(TILE)[None, :]
    assert TILE + 1 >= REL_MAX_DIST
    tiles = []
    for d in range(2):
        dist = d * TILE + i - j
        tiles.append(np.where(dist >= 0, _t5_bucket_np(dist), MASKED_BUCKET))
    dist1 = TILE + i - j
    win = np.where(dist1 < WINDOW, _t5_bucket_np(dist1), MASKED_BUCKET)
    n_cmp = (s - CMP_LEN) // CMP_STRIDE + 1
    assert n_cmp * CMP_STRIDE + CMP_LEN - 1 >= s or n_cmp == N_CMP_PAD
    rel = np.arange(N_CMP_PAD + (s - TILE) // CMP_STRIDE)[:, None] - (s - TILE) // CMP_STRIDE
    dist_c = i - (rel * CMP_STRIDE + CMP_LEN - 1)
    cmp_idx = np.where(dist_c >= 0, _t5_bucket_np(dist_c), MASKED_BUCKET)
    as_i32 = lambda a: jnp.asarray(a.astype(np.int32))
    return as_i32(np.concatenate(tiles, axis=0)), as_i32(win), as_i32(cmp_idx)


def _overlap_table(s):
    n_cmp = (s - CMP_LEN) // CMP_STRIDE + 1
    n_sel = s // SEL_BLOCK
    cs = np.arange(n_cmp) * CMP_STRIDE
    ss = np.arange(n_sel) * SEL_BLOCK
    ov = np.clip(np.minimum(cs[:, None] + CMP_LEN, ss[None, :] + SEL_BLOCK)
                 - np.maximum(cs[:, None], ss[None, :]), 0, None).astype(np.float32) / CMP_LEN
    ovt = np.zeros((n_sel, N_CMP_PAD), np.float32)
    ovt[:, :n_cmp] = ov.T
    return jnp.asarray(ovt, BF16)


def _store_v_aug(vt_sc, idx, vt):
    ones = jnp.ones((BF16_ROWS, vt.shape[1]), BF16)
    vt_sc[idx] = jnp.concatenate([vt.astype(BF16), ones], axis=0)


def _lane_cat(xs):
    return jnp.concatenate(xs, axis=1)


def _query_halves(x):
    n = x.shape[-1] // TILE
    first = _lane_cat([x[:, k * TILE:k * TILE + HALF] for k in range(n)])
    second = _lane_cat([x[:, k * TILE + HALF:(k + 1) * TILE] for k in range(n)])
    return first, second


def _join_query_halves(first, second):
    n = first.shape[-1] // HALF
    return _lane_cat([part for k in range(n)
                      for part in (first[:, k * HALF:(k + 1) * HALF], second[:, k * HALF:(k + 1) * HALF])])


def _triangle_tile(k, q, table, pv_lo, pv_hi, causal):
    q_first, q_second = _query_halves(q)
    lo, hi = (0, HALF), (HALF, TILE)
    if causal:
        s_wide = _dot(k(*lo), q) + table(*lo)
        s_narrow = _dot(k(*hi), q_second) + _query_halves(table(*hi))[1]
        pv_wide, pv_narrow = pv_lo, pv_hi
    else:
        s_wide = _dot(k(*hi), q) + table(*hi)
        s_narrow = _dot(k(*lo), q_first) + _query_halves(table(*lo))[0]
        pv_wide, pv_narrow = pv_hi, pv_lo
    mw_first, mw_second = _query_halves(jnp.max(s_wide, axis=0, keepdims=True))
    m_narrow = jnp.max(s_narrow, axis=0, keepdims=True)
    if causal:
        m_narrow = jnp.maximum(m_narrow, mw_second)
        m = _join_query_halves(mw_first, m_narrow)
    else:
        m_narrow = jnp.maximum(m_narrow, mw_first)
        m = _join_query_halves(m_narrow, mw_second)
    aw_first, aw_second = _query_halves(pv_wide(jnp.exp2(s_wide - m).astype(BF16)))
    a_narrow = pv_narrow(jnp.exp2(s_narrow - m_narrow).astype(BF16))
    if causal:
        return m, _join_query_halves(aw_first, aw_second + a_narrow)
    return m, _join_query_halves(aw_first + a_narrow, aw_second)


def _flash_pipelined(own, streams, s_sc, m_ref, acc_ref):
    has_prev = jnp.where(own > 0, 1.0, 0.0).astype(F32)
    prev = jnp.maximum(own - 1, 0)
    n_far = jnp.maximum(own - 1, 0)

    def absorb(g, s, kt, c_row, w_row):
        u = jnp.max(s, axis=0, keepdims=True) + c_row
        m_old = m_ref[g]
        m_new = jnp.maximum(m_old, jnp.where(w_row > 0.0, u, NEG_INF))
        alpha = jnp.exp2(m_old - m_new)
        shift = jnp.maximum(m_new, u) - c_row
        p = jnp.exp2(s - shift).astype(BF16)
        acc_ref[g] = alpha * acc_ref[g] + w_row * streams[g]["pv"](kt)(p)
        m_ref[g] = m_new

    def absorb_slot(g, i):
        is_prev = i == 0
        kt = jnp.where(is_prev, prev, i - 1)
        c_row = jnp.where(is_prev, 0.0, streams[g]["c_far"])
        w_row = streams[g]["w"](kt) * jnp.where(is_prev, has_prev, 1.0)
        absorb(g, s_sc[g], kt, c_row, w_row)

    def body(i, carry):
        for g in range(len(streams)):
            nxt = streams[g]["far"](i)
            absorb_slot(g, i)
            s_sc[g] = nxt
        return carry

    lax.fori_loop(0, n_far, body, 0)
    for g in range(len(streams)):
        absorb_slot(g, n_far)


def _flash_start(g, stream, s_sc, m_ref, acc_ref):
    m_ref[g], acc_ref[g] = stream["own"]()
    s_sc[g] = stream["prev"]()


def _softmax_av(s_list, pv_list):
    m = s_list[0].max(axis=0, keepdims=True)
    for s in s_list[1:]:
        m = jnp.maximum(m, s.max(axis=0, keepdims=True))
    acc = None
    for s, pv in zip(s_list, pv_list):
        part = pv(jnp.exp2(s - m).astype(BF16))
        acc = part if acc is None else acc + part
    return acc


def _normalize(acc):
    return acc[:HEAD_DIM] / acc[HEAD_DIM:HEAD_DIM + 1]


def _rank_before(score, n_cand):
    ranks = []
    for r0 in range(0, score.shape[0], SUBLANES):
        tile = score[r0:r0 + SUBLANES]
        blk = lax.broadcasted_iota(jnp.int32, tile.shape, 0) + r0
        rank = jnp.zeros(tile.shape, F32)
        for m in range(n_cand):
            row = score[m:m + 1, :]
            if m < r0:
                before = jnp.where(row >= tile, 1.0, 0.0)
            elif m >= r0 + SUBLANES:
                before = jnp.where(row > tile, 1.0, 0.0)
            else:
                before = jnp.where(blk > m, jnp.where(row >= tile, 1.0, 0.0), jnp.where(row > tile, 1.0, 0.0))
            rank = rank + before
        ranks.append(rank)
    return jnp.concatenate(ranks, axis=0)


def _nsa_steps(qi, q_ref, gn_ref, kc_ref, vct_ref, ks_ref, vs_ref, kw_ref, vw_ref,
               bct_ref, tt_ref, twt_ref, cfar_ref, ovt_ref, o_ref,
               vst_sc, vwt_sc, qa_sc, og_sc, acc_sc):
    nt = ks_ref.shape[1]
    n_sel = ovt_ref.shape[0]

    @pl.when(qi == 0)
    def _():
        for kt in range(nt):
            vs_t = vs_ref[0, kt].astype(F32).T
            vw_t = vw_ref[0, kt].astype(F32).T
            for g in range(NSA_GROUPS):
                _store_v_aug(vst_sc, (kt, g), vs_t[g * HEAD_DIM:(g + 1) * HEAD_DIM])
                _store_v_aug(vwt_sc, (kt, g), vw_t[g * HEAD_DIM:(g + 1) * HEAD_DIM])

    pos = lax.broadcasted_iota(jnp.int32, (1, TILE), 1) + qi * TILE
    cur = pos // SEL_BLOCK
    has_cmp = pos >= CMP_LEN - 1
    blk = lax.broadcasted_iota(jnp.int32, (n_sel, TILE), 0)
    prev = jnp.maximum(qi - 1, 0)
    no_prev = jnp.where(qi == 0, NEG_INF, 0.0).astype(F32)
    gates = gn_ref[...]

    gsls = [slice(g * HEAD_DIM, (g + 1) * HEAD_DIM) for g in range(NSA_GROUPS)]
    group_heads = [[g * NSA_HPG + j for j in range(NSA_HPG)] for g in range(NSA_GROUPS)]

    def gate(g, branch):
        return _lane_cat([gates[3 * h + branch:3 * h + branch + 1, :] for h in group_heads[g]])

    ones_row = jnp.ones((1, NSA_HPG * TILE), F32)

    def sel_stream(g):
        def qk(kt):
            return _dot(ks_ref[0, kt, :, g * 2 * HEAD_DIM:(g + 1) * 2 * HEAD_DIM], qa_sc[g])

        def own():
            return _triangle_tile(
                lambda lo, hi: ks_ref[0, qi, lo:hi, g * 2 * HEAD_DIM:(g + 1) * 2 * HEAD_DIM], qa_sc[g],
                lambda lo, hi: tt_ref[g, 0, lo:hi, :],
                lambda pr: _dot(vst_sc[qi, g, :, 0:HALF], pr), lambda pr: _dot(vst_sc[qi, g, :, HALF:TILE], pr),
                causal=True)

        return dict(own=own, prev=lambda: qk(prev) + tt_ref[g, 1], far=qk,
                    c_far=cfar_ref[g], w=lambda kt: ones_row,
                    pv=lambda kt: (lambda pr: _dot(vst_sc[kt, g], pr)))

    for g in range(NSA_GROUPS):
        heads = group_heads[g]
        q4 = _lane_cat([q_ref[h * HEAD_DIM:(h + 1) * HEAD_DIM, :] for h in heads])
        qa_sc[g, 0:HEAD_DIM, :] = q4
        qa_sc[g, HEAD_DIM + n_sel:, :] = jnp.zeros((HEAD_DIM - n_sel, NSA_HPG * TILE), BF16)

        def v_half(vt_sc, kt, lo, hi):
            return lambda pr: _dot(vt_sc[kt, g, :, lo:hi], pr)

        m_own, acc_own = _triangle_tile(
            lambda lo, hi: kw_ref[0, qi, lo:hi, gsls[g]], q4, lambda lo, hi: tt_ref[g, 0, lo:hi, :],
            v_half(vwt_sc, qi, 0, HALF), v_half(vwt_sc, qi, HALF, TILE), causal=True)
        m_prev, acc_prev = _triangle_tile(
            lambda lo, hi: kw_ref[0, prev, lo:hi, gsls[g]], q4, lambda lo, hi: twt_ref[g, lo:hi, :] + no_prev,
            v_half(vwt_sc, prev, 0, HALF), v_half(vwt_sc, prev, HALF, TILE), causal=False)
        m_win = jnp.maximum(m_own, m_prev)
        acc_w = acc_own * jnp.exp2(m_own - m_win) + acc_prev * jnp.exp2(m_prev - m_win)
        o_win = gate(g, 2) * _normalize(acc_w)

        kc = kc_ref[0, :, gsls[g]]
        vct = vct_ref[0, gsls[g], :]
        rows_per_tile = TILE // CMP_STRIDE
        c_rows = pl.ds(pl.multiple_of((nt - 1 - qi) * rows_per_tile, rows_per_tile), N_CMP_PAD)
        s = _dot(kc, q4) + _lane_cat([bct_ref[h, c_rows, :] for h in heads])
        e = jnp.exp2(s - jnp.max(s, axis=0, keepdims=True))
        p = jnp.where(_lane_cat([has_cmp] * NSA_HPG), e / jnp.sum(e, axis=0, keepdims=True), 0.0)
        psum = p[:, :TILE]
        for j in range(1, NSA_HPG):
            psum = psum + p[:, j * TILE:(j + 1) * TILE]
        og_sc[g] = gate(g, 0) * _dot(vct, p.astype(BF16)) + o_win

        p_hi, p_lo = _split_bf16(psum)
        imp = _dot(ovt_ref[...], p_hi) + _dot(ovt_ref[...], p_lo)
        forced = (blk == 0) | (blk == cur) | (blk == cur - 1)
        score = jnp.where(forced, FORCE_SCORE, jnp.where(blk <= cur, imp, NEG_INF))
        rank = _rank_before(score, n_sel)
        sel = jnp.where(rank < SEL_TOPN, jnp.where(score > NEG_INF / 2, 0.0, NEG_INF), NEG_INF)
        qa_sc[g, HEAD_DIM:HEAD_DIM + n_sel, :] = _lane_cat([sel.astype(BF16)] * NSA_HPG)

    yield [sel_stream(g) for g in range(NSA_GROUPS)]

    for g in range(NSA_GROUPS):
        o = og_sc[g] + gate(g, 1) * _normalize(acc_sc[g])
        for j, h in enumerate(group_heads[g]):
            o_ref[h * HEAD_DIM:(h + 1) * HEAD_DIM, :] = o[:, j * TILE:(j + 1) * TILE].astype(BF16)


def _moba_steps(c, qm_ref, km_ref, vm_ref, qx_ref, mk_ref, mv_ref, tt_ref, cfar_ref, om_ref, ox_ref,
                vmt_sc, mvt_sc, kmean_sc, qbd_sc, sel_sc, acc_sc, slot):
    nt = km_ref.shape[1]
    hsls = [slice(h * HEAD_DIM, (h + 1) * HEAD_DIM) for h in range(MOBA_HEADS)]

    @pl.when(c == 0)
    def _():
        kmean_sc[...] = jnp.zeros(kmean_sc.shape, F32)
        for n in range(nt):
            kmean_sc[n:n + 1, :] = jnp.sum(km_ref[0, n].astype(F32), axis=0, keepdims=True) * (1.0 / MOBA_BLOCK)
            vt = vm_ref[0, n].astype(F32).T
            for h in range(MOBA_HEADS):
                _store_v_aug(vmt_sc, (n, h), vt[hsls[h]])
        mvt = mv_ref[0].astype(F32).T
        for h in range(MOBA_HEADS):
            _store_v_aug(mvt_sc, h, mvt[hsls[h]])

    row_head = lax.broadcasted_iota(jnp.int32, (MOBA_W, TILE), 0) // HEAD_DIM

    def block_diag(q_ref):
        q_t = q_ref[...].astype(F32)
        return _lane_cat([jnp.where(row_head == h, q_t, 0.0) for h in range(MOBA_HEADS)]).astype(BF16)

    def per_head_pv(vts):
        return lambda pr: _lane_cat([_dot(vts(h), pr[:, h * TILE:(h + 1) * TILE]) for h in range(MOBA_HEADS)])

    def store_heads(o_t, out_ref):
        for h in range(MOBA_HEADS):
            out_ref[hsls[h], :] = o_t[:, h * TILE:(h + 1) * TILE].astype(BF16)

    qbd = block_diag(qm_ref)
    km_hi, km_lo = _split_bf16(kmean_sc[...])
    n_rows = -(-nt // SUBLANES) * SUBLANES
    gate = (_dot(km_hi, qbd) + _dot(km_lo, qbd))[:n_rows]
    blk = lax.broadcasted_iota(jnp.int32, gate.shape, 0)
    score = jnp.where(blk < c, gate, NEG_INF * Q_SCALE)
    rank = _rank_before(score, nt)
    sel_sc[0:n_rows, :] = jnp.where(rank < MOBA_TOPK, jnp.where(score > NEG_INF * Q_SCALE / 2, 1.0, 0.0), 0.0)

    qbd_sc[...] = qbd
    qk = lambda n: _dot(km_ref[0, n], qbd_sc[...])
    def own_pv(lo, hi):
        def pv(pr):
            width = pr.shape[1] // MOBA_HEADS
            return _lane_cat([_dot(vmt_sc[c, h, :, lo:hi], pr[:, h * width:(h + 1) * width])
                              for h in range(MOBA_HEADS)])
        return pv

    def own():
        return _triangle_tile(lambda lo, hi: km_ref[0, c, lo:hi, :], qbd_sc[...], lambda lo, hi: tt_ref[0, lo:hi, :],
                              own_pv(0, HALF), own_pv(HALF, TILE), causal=True)

    stream = dict(own=own, prev=lambda: qk(jnp.maximum(c - 1, 0)) + tt_ref[1], far=qk,
                  c_far=cfar_ref[...], w=lambda n: sel_sc[pl.ds(n, 1), :],
                  pv=lambda n: per_head_pv(lambda h: vmt_sc[n, h]))
    s = _dot(mk_ref[0], block_diag(qx_ref))
    store_heads(_normalize(_softmax_av([s], [per_head_pv(lambda h: mvt_sc[h])])), ox_ref)

    yield [stream]
    store_heads(_normalize(acc_sc[slot]), om_ref)


N_NSA_IN, N_MOBA_IN = 13, 8
N_NSA_SCRATCH, N_MOBA_SCRATCH = 4, 5
N_STREAMS = NSA_GROUPS + 1


def _attention_kernel(*refs):
    qi = pl.program_id(1)
    nsa_in, refs = refs[:N_NSA_IN], refs[N_NSA_IN:]
    moba_in, refs = refs[:N_MOBA_IN], refs[N_MOBA_IN:]
    (o_nsa, o_moba, o_mem), refs = refs[:3], refs[3:]
    nsa_sc, refs = refs[:N_NSA_SCRATCH], refs[N_NSA_SCRATCH:]
    moba_sc, refs = refs[:N_MOBA_SCRATCH], refs[N_MOBA_SCRATCH:]
    s_sc, m_sc, acc_sc = refs
    nsa = _nsa_steps(qi, *nsa_in, o_nsa, *nsa_sc, acc_sc)
    moba = _moba_steps(qi, *moba_in, o_moba, o_mem, *moba_sc, acc_sc, NSA_GROUPS)
    nsa_streams = next(nsa)
    for g, stream in enumerate(nsa_streams):
        _flash_start(g, stream, s_sc, m_sc, acc_sc)
    moba_streams = next(moba)
    _flash_start(NSA_GROUPS, moba_streams[0], s_sc, m_sc, acc_sc)
    _flash_pipelined(qi, nsa_streams + moba_streams, s_sc, m_sc, acc_sc)
    for steps in (nsa, moba):
        for _ in steps:
            pass


def _attention(b, s, qn, gn, kc, vct, ks, vs, kw, vw, bias_cmp, t_nsa, t_win, c_far_nsa, ovt,
               qm, km, vm, qx, mk, mv, t_moba, c_far_moba):
    nt = s // TILE
    mem_len = mk.shape[0] // b
    assert MOBA_TOPK <= nt - 1 and nt <= BF16_ROWS
    n_lanes = NSA_HPG * TILE
    assert MOBA_HEADS * TILE == n_lanes
    row_spec = lambda w: pl.BlockSpec((w, TILE), lambda i, j: (0, i * nt + j))
    seq_spec = lambda w: pl.BlockSpec((1, nt, TILE, w), lambda i, j: (i, 0, 0, 0))
    per_batch = lambda rows, w: pl.BlockSpec((1, rows, w), lambda i, j: (i, 0, 0))
    tiles = lambda a: a.reshape(b, nt, TILE, a.shape[-1])
    nsa_in = [(qn, row_spec(NSA_Q_W)), (gn, row_spec(GATE_PAD)),
              (kc, per_batch(N_CMP_PAD, NSA_KV_W)), (vct, per_batch(N_CMP_PAD, NSA_KV_W)),
              (tiles(ks), seq_spec(KS_AUG_W)), (tiles(vs), seq_spec(NSA_KV_W)),
              (tiles(kw), seq_spec(NSA_KV_W)), (tiles(vw), seq_spec(NSA_KV_W)),
              (bias_cmp, _const_spec(bias_cmp.shape)), (t_nsa, _const_spec(t_nsa.shape)),
              (t_win, _const_spec(t_win.shape)), (c_far_nsa, _const_spec(c_far_nsa.shape)),
              (ovt, _const_spec(ovt.shape))]
    moba_in = [(qm, row_spec(MOBA_W)), (tiles(km), seq_spec(MOBA_W)), (tiles(vm), seq_spec(MOBA_W)),
               (qx, row_spec(MEM_W)),
               (mk.reshape(b, mem_len, MEM_W), per_batch(mem_len, MEM_W)),
               (mv.reshape(b, mem_len, MEM_W), per_batch(mem_len, MEM_W)),
               (t_moba, _const_spec(t_moba.shape)), (c_far_moba, _const_spec(c_far_moba.shape))]
    assert len(nsa_in) == N_NSA_IN and len(moba_in) == N_MOBA_IN
    nsa_scratch = [pltpu.VMEM((nt, NSA_GROUPS, V_AUG, TILE), BF16),
                   pltpu.VMEM((nt, NSA_GROUPS, V_AUG, TILE), BF16),
                   pltpu.VMEM((NSA_GROUPS, 2 * HEAD_DIM, n_lanes), BF16),
                   pltpu.VMEM((NSA_GROUPS, HEAD_DIM, n_lanes), F32)]
    moba_scratch = [pltpu.VMEM((nt, MOBA_HEADS, V_AUG, TILE), BF16),
                    pltpu.VMEM((MEM_HEADS, V_AUG, mem_len), BF16),
                    pltpu.VMEM((BF16_ROWS, MOBA_W), F32),
                    pltpu.VMEM((MOBA_W, n_lanes), BF16),
                    pltpu.VMEM((BF16_ROWS, n_lanes), F32)]
    assert len(nsa_scratch) == N_NSA_SCRATCH and len(moba_scratch) == N_MOBA_SCRATCH
    shared_scratch = [pltpu.VMEM((N_STREAMS, TILE, n_lanes), F32),
                      pltpu.VMEM((N_STREAMS, 1, n_lanes), F32),
                      pltpu.VMEM((N_STREAMS, V_AUG, n_lanes), F32)]
    inputs = nsa_in + moba_in
    return pl.pallas_call(
        _attention_kernel,
        grid=(b, nt),
        in_specs=[spec for _, spec in inputs],
        out_specs=[row_spec(NSA_Q_W), row_spec(MOBA_W), row_spec(MEM_W)],
        out_shape=[jax.ShapeDtypeStruct((w, b * s), BF16) for w in (NSA_Q_W, MOBA_W, MEM_W)],
        scratch_shapes=nsa_scratch + moba_scratch + shared_scratch,
        compiler_params=_params(("arbitrary", "arbitrary")),
        name="attention",
    )(*[a for a, _ in inputs])


def _mix_kernel(x_ref, on_ref, om_ref, ox_ref, g_pre_ref, g_post_ref, wg_ref, wn_ref, wm_ref, wx_ref,
                wo_ref, o_ref):
    x = x_ref[...]
    h = _rms(x, g_pre_ref[...]).astype(BF16)
    rows = lambda o_ref: o_ref[...].astype(F32).T.astype(BF16)
    merged = jax.nn.sigmoid(_dot(h, wg_ref[:, :D_MODEL])) * _dot(rows(on_ref), wn_ref[...])
    merged = merged + jax.nn.sigmoid(_dot(h, wg_ref[:, D_MODEL:2 * D_MODEL])) * _dot(rows(om_ref), wm_ref[...])
    merged = merged + jax.nn.sigmoid(_dot(h, wg_ref[:, 2 * D_MODEL:])) * _dot(rows(ox_ref), wx_ref[...])
    y = _dot(merged.astype(BF16), wo_ref[...])
    o_ref[...] = x + _rms(y, g_post_ref[...])


def _mix(x2, o_nsa, o_moba, o_mem, g_pre, g_post, w_gates, w_nsa_o, w_moba_o, w_mem_o, w_mix_out, tm=512):
    m = x2.shape[0]
    row = lambda w: pl.BlockSpec((tm, w), lambda i: (i, 0))
    col = lambda w: pl.BlockSpec((w, tm), lambda i: (0, i))
    return pl.pallas_call(
        _mix_kernel,
        grid=(m // tm,),
        in_specs=[row(D_MODEL), col(NSA_Q_W), col(MOBA_W), col(MEM_W),
                  _const_spec((1, D_MODEL)), _const_spec((1, D_MODEL)),
                  _const_spec(w_gates.shape), _const_spec(w_nsa_o.shape), _const_spec(w_moba_o.shape),
                  _const_spec(w_mem_o.shape), _const_spec(w_mix_out.shape)],
        out_specs=row(D_MODEL),
        out_shape=jax.ShapeDtypeStruct((m, D_MODEL), F32),
        compiler_params=_params(("parallel",)),
        name="mix",
    )(x2, o_nsa, o_moba, o_mem, g_pre, g_post, w_gates, w_nsa_o, w_moba_o, w_mem_o, w_mix_out)


FFN_CHUNK = 256


def _ffn_kernel(x_ref, g_pre_ref, g_post_ref, wg_ref, wu_ref, wd_ref, o_ref, a_sc):
    x = x_ref[...]
    h = _rms(x, g_pre_ref[...]).astype(BF16)
    d_ff = wg_ref.shape[1]
    for j in range(d_ff // FFN_CHUNK):
        sl = slice(j * FFN_CHUNK, (j + 1) * FFN_CHUNK)
        a_sc[:, sl] = (jax.nn.silu(_dot(h, wg_ref[:, sl])) * _dot(h, wu_ref[:, sl])).astype(BF16)
    f = _dot(a_sc[...], wd_ref[...])
    o_ref[...] = x + _rms(f, g_post_ref[...])


def _ffn(x2, g_pre, g_post, wg, wu, wd, tm=512):
    m = x2.shape[0]
    d_ff = wg.shape[1]
    return pl.pallas_call(
        _ffn_kernel,
        grid=(m // tm,),
        in_specs=[pl.BlockSpec((tm, D_MODEL), lambda i: (i, 0)),
                  _const_spec((1, D_MODEL)), _const_spec((1, D_MODEL)),
                  _const_spec(wg.shape), _const_spec(wu.shape), _const_spec(wd.shape)],
        out_specs=pl.BlockSpec((tm, D_MODEL), lambda i: (i, 0)),
        out_shape=jax.ShapeDtypeStruct((m, D_MODEL), F32),
        scratch_shapes=[pltpu.VMEM((tm, d_ff), BF16)],
        compiler_params=_params(("parallel",)),
        name="ffn",
    )(x2, g_pre, g_post, wg, wu, wd)


def kernel(x, mem, rel_bias, pre_mix_g, mem_norm_g, post_mix_g, w_in, cmp_pos_k, cmp_w1_k, cmp_w2_k, cmp_pos_v, cmp_w1_v, cmp_w2_v, w_mem_kv, w_nsa_o, w_moba_o, w_mem_o, w_mix_out, pre_ffn_g, post_ffn_g, w_ffn_gate, w_ffn_up, w_ffn_down):
    b, s, d_model = x.shape
    depth = w_in.shape[0]
    assert d_model == D_MODEL and s % TILE == 0 and TILE == MOBA_BLOCK == WINDOW
    assert (s - CMP_LEN) // CMP_STRIDE + 1 < N_CMP_PAD and (s // SEL_BLOCK) % SUBLANES == 0 and s // SEL_BLOCK <= HEAD_DIM
    assert w_in.shape[2] == ATT_W + 3 * D_MODEL and rel_bias.shape == (REL_BUCKETS, N_BIAS_HEADS)

    tile_idx, win_idx, cmp_idx = _bucket_tables(s)
    rel_bias = rel_bias.astype(F32)
    t_nsa = _expand(tile_idx, rel_bias, 0, NSA_HEADS, NSA_HPG).reshape(NSA_GROUPS, 2, TILE, NSA_HPG * TILE)
    t_moba = _expand(tile_idx, rel_bias, NSA_HEADS, MOBA_HEADS, MOBA_HEADS).reshape(2, TILE, MOBA_HEADS * TILE)
    t_win = _expand(win_idx, rel_bias, 0, NSA_HEADS, NSA_HPG)
    b_cmp = _expand(cmp_idx, rel_bias, 0, NSA_HEADS)
    c_far = jnp.repeat(rel_bias[REL_BUCKETS - 1] * LOG2E, TILE)
    c_far_nsa = c_far[:NSA_HEADS * TILE].reshape(NSA_GROUPS, 1, NSA_HPG * TILE)
    c_far_moba = c_far[NSA_HEADS * TILE:].reshape(1, MOBA_HEADS * TILE)
    ovt = _overlap_table(s)
    sel_cols = np.zeros((s, HEAD_DIM), np.float32)
    sel_cols[np.arange(s), np.arange(s) // SEL_BLOCK] = 1.0
    sel_cols = jnp.asarray(sel_cols, BF16)
    gate_lo = NSA_Q_W + 6 * NSA_KV_W
    rows_per_chunk = CMP_STRIDE * NSA_KV_W

    x2 = x.reshape(b * s, D_MODEL)
    mem2 = mem.reshape(-1, D_MODEL)
    for l in range(depth):
        w_att = jnp.concatenate(
            [w_in[l, :, :gate_lo + NSA_GATE_W],
             jnp.zeros((D_MODEL, GATE_PAD - NSA_GATE_W), w_in.dtype),
             w_in[l, :, gate_lo + NSA_GATE_W:ATT_W]], axis=1).astype(BF16)
        w_gates = w_in[l, :, ATT_W:].astype(BF16)
        row = lambda v: v[l].reshape(1, D_MODEL)

        qn, kc_raw, vc_raw, ks, vs, kw, vw, gn, qm, km, vm, qx = _inproj(x2, row(pre_mix_g), w_att, sel_cols)

        pk, w1k = _compress_weights(cmp_pos_k[l], cmp_w1_k[l])
        pv, w1v = _compress_weights(cmp_pos_v[l], cmp_w1_v[l])
        kc, vct = _compress(kc_raw.reshape(b, s // CMP_STRIDE, rows_per_chunk),
                            vc_raw.reshape(b, s // CMP_STRIDE, rows_per_chunk),
                            pk, pv, w1k, w1v, cmp_w2_k[l].astype(BF16), cmp_w2_v[l].astype(BF16))

        mk, mv = _memkv(mem2, row(mem_norm_g), w_mem_kv[l].astype(BF16))

        o_nsa, o_moba, o_mem = _attention(b, s, qn, gn, kc, vct, ks, vs, kw, vw, b_cmp, t_nsa, t_win, c_far_nsa, ovt,
                                          qm, km, vm, qx, mk, mv, t_moba, c_far_moba)

        x2 = _mix(x2, o_nsa, o_moba, o_mem, row(pre_mix_g), row(post_mix_g), w_gates,
                  w_nsa_o[l].astype(BF16), w_moba_o[l].astype(BF16), w_mem_o[l].astype(BF16),
                  w_mix_out[l].astype(BF16))
        x2 = _ffn(x2, row(pre_ffn_g), row(post_ffn_g), w_ffn_gate[l].astype(BF16),
                  w_ffn_up[l].astype(BF16), w_ffn_down[l].astype(BF16))
    return x2.reshape(b, s, D_MODEL)
```

```python
import functools
import math

import numpy as np
import jax
import jax.numpy as jnp
from jax import lax
from jax.experimental import pallas as pl
from jax.experimental.pallas import tpu as pltpu

F32 = jnp.float32
BF16 = jnp.bfloat16

D_MODEL = 1024
HEAD_DIM = 64
SCALE = HEAD_DIM ** -0.5
LOG2E = math.log2(math.e)
Q_SCALE = SCALE * LOG2E
NSA_HEADS = 8
NSA_GROUPS = 2
NSA_HPG = NSA_HEADS // NSA_GROUPS
CMP_LEN = 32
CMP_STRIDE = 16
CMP_HIDDEN = 128
SEL_BLOCK = 64
SEL_TOPN = 8
WINDOW = 256
MOBA_HEADS = 4
MOBA_BLOCK = 256
MOBA_TOPK = 3
MEM_HEADS = 4
REL_BUCKETS = 32
REL_MAX_DIST = 128
N_BIAS_HEADS = NSA_HEADS + MOBA_HEADS
RMS_EPS = 1e-6
NEG_INF = -1e30
FORCE_SCORE = 1e4

NSA_Q_W = NSA_HEADS * HEAD_DIM
NSA_KV_W = NSA_GROUPS * HEAD_DIM
NSA_GATE_W = NSA_HEADS * 3
MOBA_W = MOBA_HEADS * HEAD_DIM
MEM_W = MEM_HEADS * HEAD_DIM
ATT_W = NSA_Q_W + 6 * NSA_KV_W + NSA_GATE_W + 3 * MOBA_W + MEM_W
LANES = 128
SUBLANES = 8
BF16_ROWS = 16
MXU_COLS = 256
GATE_PAD = LANES
TILE = 256
HALF = TILE // 2
N_CMP_PAD = 128
V_AUG = HEAD_DIM + BF16_ROWS
MASKED_BUCKET = REL_BUCKETS
VMEM_LIMIT = 56 * 1024 * 1024


def _dot(a, b):
    return jnp.dot(a, b, preferred_element_type=F32)


def _split_bf16(x):
    hi = x.astype(BF16)
    lo = (x - hi.astype(F32)).astype(BF16)
    return hi, lo


def _rms(x, g):
    return x * lax.rsqrt(jnp.mean(x * x, axis=-1, keepdims=True) + RMS_EPS) * g


def _params(sem):
    return pltpu.CompilerParams(dimension_semantics=sem, vmem_limit_bytes=VMEM_LIMIT)


def _const_spec(shape):
    nd = len(shape)
    return pl.BlockSpec(shape, lambda *_: (0,) * nd, pipeline_mode=pl.Buffered(1))


_INPROJ_OUTS = (
    ("qn", NSA_Q_W, BF16, True),
    ("kc", NSA_KV_W, F32, False), ("vc", NSA_KV_W, F32, False),
    ("ks", NSA_KV_W, BF16, False), ("vs", NSA_KV_W, BF16, False),
    ("kw", NSA_KV_W, BF16, False), ("vw", NSA_KV_W, BF16, False),
    ("gn", GATE_PAD, F32, False),
    ("qm", MOBA_W, BF16, True), ("km", MOBA_W, BF16, False), ("vm", MOBA_W, BF16, False),
    ("qx", MEM_W, BF16, True),
)
_INPROJ_W = sum(o[1] for o in _INPROJ_OUTS)
_INPROJ_CHUNKED = ("kc", "vc")
_INPROJ_TRANSPOSED = ("qn", "gn", "qm", "qx")
KS_AUG_W = NSA_GROUPS * 2 * HEAD_DIM


def _inproj_out_width(name, width):
    return KS_AUG_W if name == "ks" else width


def _inproj_kernel(x_ref, g_ref, w_ref, e_ref, *refs):
    out_refs, rows_sc = refs[:-1], refs[-1]
    h = _rms(x_ref[...], g_ref[...]).astype(BF16)
    runs, lo = [], 0
    for out in zip(_INPROJ_OUTS, out_refs):
        if runs and runs[-1][1] < MXU_COLS:
            runs[-1][0].append(out)
            runs[-1][1] += out[0][1]
        else:
            runs.append([[out], out[0][1], lo])
        lo += out[0][1]
    for outs, run_width, run_lo in runs:
        y_run = _dot(h, w_ref[:, run_lo:run_lo + run_width])
        lo = 0
        for (name, width, dtype, scaled), o_ref in outs:
            y = y_run[:, lo:lo + width]
            if scaled:
                y = y * Q_SCALE
            if name == "gn":
                y = jax.nn.sigmoid(y)
            if name in _INPROJ_TRANSPOSED:
                y = y.T
            y = y.astype(dtype)
            if name == "ks":
                e = e_ref[...]
                y = _lane_cat([y[:, :HEAD_DIM], e, y[:, HEAD_DIM:], e])
            if name in _INPROJ_CHUNKED:
                rows_sc[...] = y
                for j in range(CMP_STRIDE):
                    o_ref[:, j * width:(j + 1) * width] = rows_sc[pl.ds(j, o_ref.shape[0], stride=CMP_STRIDE), :]
            else:
                o_ref[...] = y
            lo += width


def _inproj(x2, g, w, e_cols, tm=512):
    m = x2.shape[0]
    tiles_per_seq = e_cols.shape[0] // tm
    out_specs, out_shape = [], []
    for name, width, dtype, _ in _INPROJ_OUTS:
        if name in _INPROJ_TRANSPOSED:
            out_specs.append(pl.BlockSpec((width, tm), lambda i: (0, i)))
            out_shape.append(jax.ShapeDtypeStruct((width, m), dtype))
            continue
        rows, width = (CMP_STRIDE, CMP_STRIDE * width) if name in _INPROJ_CHUNKED else (1, _inproj_out_width(name, width))
        out_specs.append(pl.BlockSpec((tm // rows, width), lambda i: (i, 0)))
        out_shape.append(jax.ShapeDtypeStruct((m // rows, width), dtype))
    return pl.pallas_call(
        _inproj_kernel,
        grid=(m // tm,),
        in_specs=[pl.BlockSpec((tm, D_MODEL), lambda i: (i, 0)),
                  _const_spec((1, D_MODEL)),
                  _const_spec((D_MODEL, _INPROJ_W)),
                  pl.BlockSpec((tm, HEAD_DIM), lambda i: (i % tiles_per_seq, 0))],
        out_specs=out_specs,
        out_shape=out_shape,
        scratch_shapes=[pltpu.VMEM((tm, NSA_KV_W), F32)],
        compiler_params=_params(("parallel",)),
        name="inproj",
    )(x2, g, w, e_cols)


def _compress_kernel(rk_ref, rv_ref, pk_ref, pv_ref, w1k_ref, w1v_ref, w2k_ref, w2v_ref, kc_ref, vc_ref):
    nb = rk_ref.shape[0]

    def one(r_ref, p_ref, w1_ref, w2_ref):
        r = r_ref[...].reshape(nb * N_CMP_PAD, r_ref.shape[2])
        top = _dot((r + p_ref[0:1, :]).astype(BF16), w1_ref[0])
        bot = _dot((r + p_ref[1:2, :]).astype(BF16), w1_ref[1])
        hid = top + pltpu.roll(bot, nb * N_CMP_PAD - 1, 0)
        act = jax.nn.gelu(hid).astype(BF16)
        return jnp.concatenate(
            [_dot(act[:, g * CMP_HIDDEN:(g + 1) * CMP_HIDDEN], w2_ref[...]) for g in range(NSA_GROUPS)], axis=1)

    k_out = one(rk_ref, pk_ref, w1k_ref, w2k_ref)
    v_out = one(rv_ref, pv_ref, w1v_ref, w2v_ref)
    for n in range(nb):
        rows = slice(n * N_CMP_PAD, (n + 1) * N_CMP_PAD)
        kc_ref[n] = k_out[rows].astype(BF16)
        vc_ref[n] = v_out[rows].T.astype(BF16)


def _compress(rk, rv, pk, pv, w1k, w1v, w2k, w2v):
    b = rk.shape[0]
    rw = rk.shape[2]
    nb = 2 if b % 2 == 0 else 1
    r_spec = pl.BlockSpec((nb, N_CMP_PAD, rw), lambda i: (i, 0, 0))
    o_spec = pl.BlockSpec((nb, N_CMP_PAD, NSA_KV_W), lambda i: (i, 0, 0))
    return pl.pallas_call(
        _compress_kernel,
        grid=(b // nb,),
        in_specs=[r_spec, r_spec, _const_spec(pk.shape), _const_spec(pv.shape),
                  _const_spec(w1k.shape), _const_spec(w1v.shape),
                  _const_spec(w2k.shape), _const_spec(w2v.shape)],
        out_specs=[o_spec, o_spec],
        out_shape=[jax.ShapeDtypeStruct((b, N_CMP_PAD, NSA_KV_W), BF16)] * 2,
        compiler_params=_params(("parallel",)),
        name="compress",
    )(rk, rv, pk, pv, w1k, w1v, w2k, w2v)


def _compress_weights(pos, w1):
    half = CMP_LEN // 2
    p = pos.reshape(2, half, 1, HEAD_DIM)
    p = jnp.broadcast_to(p, (2, half, NSA_GROUPS, HEAD_DIM)).reshape(2, half * NSA_KV_W)
    w = w1.reshape(2, half, HEAD_DIM, CMP_HIDDEN)
    eye = jnp.eye(NSA_GROUPS, dtype=w1.dtype)
    wbd = jnp.einsum("ajdm,gk->ajgdkm", w, eye).reshape(2, half * NSA_KV_W, NSA_GROUPS * CMP_HIDDEN)
    return p.astype(F32), wbd.astype(BF16)


def _memkv_kernel(m_ref, g_ref, w_ref, k_ref, v_ref):
    h = _rms(m_ref[...], g_ref[...]).astype(BF16)
    k_ref[...] = _dot(h, w_ref[:, :MEM_W]).astype(BF16)
    v_ref[...] = _dot(h, w_ref[:, MEM_W:]).astype(BF16)


def _memkv(mem2, g, w, tm=512):
    m = mem2.shape[0]
    tm = min(tm, m)
    o_spec = pl.BlockSpec((tm, MEM_W), lambda i: (i, 0))
    return pl.pallas_call(
        _memkv_kernel,
        grid=(m // tm,),
        in_specs=[pl.BlockSpec((tm, D_MODEL), lambda i: (i, 0)), _const_spec((1, D_MODEL)),
                  _const_spec((D_MODEL, 2 * MEM_W))],
        out_specs=[o_spec, o_spec],
        out_shape=[jax.ShapeDtypeStruct((m, MEM_W), BF16)] * 2,
        compiler_params=_params(("parallel",)),
        name="memkv",
    )(mem2, g, w)


def _expand_kernel(idx_ref, bias_ref, o_ref, *, head0, n_heads, heads_per_group):
    rows, cols = idx_ref.shape

    def body(i, carry):
        r = pl.multiple_of(i * SUBLANES, SUBLANES)
        for c0 in range(0, cols, TILE):
            idx = idx_ref[pl.ds(r, SUBLANES), c0:c0 + TILE]
            out = [jnp.full(idx.shape, NEG_INF, F32)] * n_heads
            for bkt in range(REL_BUCKETS):
                hit = idx == bkt
                out = [jnp.where(hit, bias_ref[bkt, head0 + h], out[h]) for h in range(n_heads)]
            for h in range(n_heads):
                col = (h % heads_per_group) * cols + c0
                o_ref[h // heads_per_group, pl.ds(r, SUBLANES), col:col + TILE] = out[h] * LOG2E
        return carry

    lax.fori_loop(0, rows // SUBLANES, body, 0)


def _expand(idx, rel_bias, head0, n_heads, heads_per_group=1):
    rows, cols = idx.shape
    return pl.pallas_call(
        functools.partial(_expand_kernel, head0=head0, n_heads=n_heads, heads_per_group=heads_per_group),
        in_specs=[pl.BlockSpec(memory_space=pltpu.VMEM), pl.BlockSpec(memory_space=pltpu.SMEM)],
        out_specs=pl.BlockSpec(memory_space=pltpu.VMEM),
        out_shape=jax.ShapeDtypeStruct((n_heads // heads_per_group, rows, heads_per_group * cols), F32),
        compiler_params=pltpu.CompilerParams(vmem_limit_bytes=VMEM_LIMIT),
        name="bias_expand",
    )(idx, rel_bias)


def _t5_bucket_np(dist):
    dist = np.maximum(dist, 0)
    max_exact = REL_BUCKETS // 2
    logd = np.log(np.maximum(dist, 1).astype(np.float32) / max_exact) / math.log(REL_MAX_DIST / max_exact)
    large = np.minimum(max_exact + (logd * (REL_BUCKETS - max_exact)).astype(np.int32), REL_BUCKETS - 1)
    return np.where(dist < max_exact, dist, large).astype(np.int32)


def _bucket_tables(s):
    j = np.arange(TILE)[:, None]
    i = np.arange(TILE)[None, :]
    assert TILE + 1 >= REL_MAX_DIST
    tiles = []
    for d in range(2):
        dist = d * TILE + i - j
        tiles.append(np.where(dist >= 0, _t5_bucket_np(dist), MASKED_BUCKET))
    dist1 = TILE + i - j
    win = np.where(dist1 < WINDOW, _t5_bucket_np(dist1), MASKED_BUCKET)
    n_cmp = (s - CMP_LEN) // CMP_STRIDE + 1
    assert n_cmp * CMP_STRIDE + CMP_LEN - 1 >= s or n_cmp == N_CMP_PAD
    rel = np.arange(N_CMP_PAD + (s - TILE) // CMP_STRIDE)[:, None] - (s - TILE) // CMP_STRIDE
    dist_c = i - (rel * CMP_STRIDE + CMP_LEN - 1)
    cmp_idx = np.where(dist_c >= 0, _t5_bucket_np(dist_c), MASKED_BUCKET)
    as_i32 = lambda a: jnp.asarray(a.astype(np.int32))
    return as_i32(np.concatenate(tiles, axis=0)), as_i32(win), as_i32(cmp_idx)


def _overlap_table(s):
    n_cmp = (s - CMP_LEN) // CMP_STRIDE + 1
    n_sel = s // SEL_BLOCK
    cs = np.arange(n_cmp) * CMP_STRIDE
    ss = np.arange(n_sel) * SEL_BLOCK
    ov = np.clip(np.minimum(cs[:, None] + CMP_LEN, ss[None, :] + SEL_BLOCK)
                 - np.maximum(cs[:, None], ss[None, :]), 0, None).astype(np.float32) / CMP_LEN
    ovt = np.zeros((n_sel, N_CMP_PAD), np.float32)
    ovt[:, :n_cmp] = ov.T
    return jnp.asarray(ovt, BF16)


def _store_v_aug(vt_sc, idx, vt):
    ones = jnp.ones((BF16_ROWS, vt.shape[1]), BF16)
    vt_sc[idx] = jnp.concatenate([vt.astype(BF16), ones], axis=0)


def _lane_cat(xs):
    return jnp.concatenate(xs, axis=1)


def _query_halves(x):
    n = x.shape[-1] // TILE
    first = _lane_cat([x[:, k * TILE:k * TILE + HALF] for k in range(n)])
    second = _lane_cat([x[:, k * TILE + HALF:(k + 1) * TILE] for k in range(n)])
    return first, second


def _join_query_halves(first, second):
    n = first.shape[-1] // HALF
    return _lane_cat([part for k in range(n)
                      for part in (first[:, k * HALF:(k + 1) * HALF], second[:, k * HALF:(k + 1) * HALF])])


def _triangle_tile(k, q, table, pv_lo, pv_hi, causal):
    return _triangle_softmax(_triangle_scores(k, q, table, causal), pv_lo, pv_hi, causal)


def _triangle_scores(k, q, table, causal):
    q_first, q_second = _query_halves(q)
    lo, hi = (0, HALF), (HALF, TILE)
    if causal:
        return _dot(k(*lo), q) + table(*lo), _dot(k(*hi), q_second) + _query_halves(table(*hi))[1]
    return _dot(k(*hi), q) + table(*hi), _dot(k(*lo), q_first) + _query_halves(table(*lo))[0]


def _triangle_softmax(scores, pv_lo, pv_hi, causal):
    s_wide, s_narrow = scores
    pv_wide, pv_narrow = (pv_lo, pv_hi) if causal else (pv_hi, pv_lo)
    mw_first, mw_second = _query_halves(jnp.max(s_wide, axis=0, keepdims=True))
    m_narrow = jnp.max(s_narrow, axis=0, keepdims=True)
    if causal:
        m_narrow = jnp.maximum(m_narrow, mw_second)
        m = _join_query_halves(mw_first, m_narrow)
    else:
        m_narrow = jnp.maximum(m_narrow, mw_first)
        m = _join_query_halves(m_narrow, mw_second)
    aw_first, aw_second = _query_halves(pv_wide(jnp.exp2(s_wide - m).astype(BF16)))
    a_narrow = pv_narrow(jnp.exp2(s_narrow - m_narrow).astype(BF16))
    if causal:
        return m, _join_query_halves(aw_first, aw_second + a_narrow)
    return m, _join_query_halves(aw_first + a_narrow, aw_second)


def _flash_pipelined(own, streams, s_sc, m_ref, acc_ref):
    has_prev = jnp.where(own > 0, 1.0, 0.0).astype(F32)
    prev = jnp.maximum(own - 1, 0)
    n_far = jnp.maximum(own - 1, 0)

    def absorb(g, s, kt, c_row, w_row):
        u = jnp.max(s, axis=0, keepdims=True) + c_row
        m_old = m_ref[g]
        m_new = jnp.maximum(m_old, jnp.where(w_row > 0.0, u, NEG_INF))
        alpha = jnp.exp2(m_old - m_new)
        shift = jnp.maximum(m_new, u) - c_row
        p = jnp.exp2(s - shift).astype(BF16)
        acc_ref[g] = alpha * acc_ref[g] + w_row * streams[g]["pv"](kt)(p)
        m_ref[g] = m_new

    def absorb_slot(g, i):
        is_prev = i == 0
        kt = jnp.where(is_prev, prev, i - 1)
        c_row = jnp.where(is_prev, 0.0, streams[g]["c_far"])
        w_row = streams[g]["w"](kt) * jnp.where(is_prev, has_prev, 1.0)
        absorb(g, s_sc[g], kt, c_row, w_row)

    def body(i, carry):
        for g in range(len(streams)):
            nxt = streams[g]["far"](i)
            absorb_slot(g, i)
            s_sc[g] = nxt
        return carry

    lax.fori_loop(0, n_far, body, 0)
    for g in range(len(streams)):
        absorb_slot(g, n_far)


def _run_ahead(jobs):
    pending = jobs[0][0]()
    for k, (_, consume) in enumerate(jobs):
        ahead = jobs[k + 1][0]() if k + 1 < len(jobs) else None
        consume(pending)
        pending = ahead


def _flash_start_jobs(g, stream, s_sc, m_ref, acc_ref):
    def init(scores):
        m_ref[g], acc_ref[g] = stream["own_softmax"](scores)

    def park(scores):
        s_sc[g] = scores

    return [(stream["own_scores"], init), (stream["prev"], park)]


def _softmax_av(s_list, pv_list):
    m = s_list[0].max(axis=0, keepdims=True)
    for s in s_list[1:]:
        m = jnp.maximum(m, s.max(axis=0, keepdims=True))
    acc = None
    for s, pv in zip(s_list, pv_list):
        part = pv(jnp.exp2(s - m).astype(BF16))
        acc = part if acc is None else acc + part
    return acc


def _normalize(acc):
    return acc[:HEAD_DIM] / acc[HEAD_DIM:HEAD_DIM + 1]


def _rank_before(score, n_cand):
    ranks = []
    for r0 in range(0, score.shape[0], SUBLANES):
        tile = score[r0:r0 + SUBLANES]
        blk = lax.broadcasted_iota(jnp.int32, tile.shape, 0) + r0
        rank = jnp.zeros(tile.shape, F32)
        for m in range(n_cand):
            row = score[m:m + 1, :]
            if m < r0:
                before = jnp.where(row >= tile, 1.0, 0.0)
            elif m >= r0 + SUBLANES:
                before = jnp.where(row > tile, 1.0, 0.0)
            else:
                before = jnp.where(blk > m, jnp.where(row >= tile, 1.0, 0.0), jnp.where(row > tile, 1.0, 0.0))
            rank = rank + before
        ranks.append(rank)
    return jnp.concatenate(ranks, axis=0)


def _nsa_steps(qi, q_ref, gn_ref, kc_ref, vct_ref, ks_ref, vs_ref, kw_ref, vw_ref,
               bct_ref, tt_ref, twt_ref, cfar_ref, ovt_ref, o_ref,
               vst_sc, vwt_sc, qa_sc, og_sc, acc_sc):
    nt = ks_ref.shape[1]
    n_sel = ovt_ref.shape[0]

    @pl.when(qi == 0)
    def _():
        for kt in range(nt):
            vs_t = vs_ref[0, kt].astype(F32).T
            vw_t = vw_ref[0, kt].astype(F32).T
            for g in range(NSA_GROUPS):
                _store_v_aug(vst_sc, (kt, g), vs_t[g * HEAD_DIM:(g + 1) * HEAD_DIM])
                _store_v_aug(vwt_sc, (kt, g), vw_t[g * HEAD_DIM:(g + 1) * HEAD_DIM])

    pos = lax.broadcasted_iota(jnp.int32, (1, TILE), 1) + qi * TILE
    cur = pos // SEL_BLOCK
    has_cmp = pos >= CMP_LEN - 1
    blk = lax.broadcasted_iota(jnp.int32, (n_sel, TILE), 0)
    prev = jnp.maximum(qi - 1, 0)
    no_prev = jnp.where(qi == 0, NEG_INF, 0.0).astype(F32)
    gates = gn_ref[...]

    gsls = [slice(g * HEAD_DIM, (g + 1) * HEAD_DIM) for g in range(NSA_GROUPS)]
    group_heads = [[g * NSA_HPG + j for j in range(NSA_HPG)] for g in range(NSA_GROUPS)]

    def gate(g, branch):
        return _lane_cat([gates[3 * h + branch:3 * h + branch + 1, :] for h in group_heads[g]])

    ones_row = jnp.ones((1, NSA_HPG * TILE), F32)

    def sel_stream(g):
        def qk(kt):
            return _dot(ks_ref[0, kt, :, g * 2 * HEAD_DIM:(g + 1) * 2 * HEAD_DIM], qa_sc[g])

        def own_scores():
            return _triangle_scores(
                lambda lo, hi: ks_ref[0, qi, lo:hi, g * 2 * HEAD_DIM:(g + 1) * 2 * HEAD_DIM], qa_sc[g],
                lambda lo, hi: tt_ref[g, 0, lo:hi, :], causal=True)

        def own_softmax(scores):
            return _triangle_softmax(
                scores, lambda pr: _dot(vst_sc[qi, g, :, 0:HALF], pr),
                lambda pr: _dot(vst_sc[qi, g, :, HALF:TILE], pr), causal=True)

        return dict(own_scores=own_scores, own_softmax=own_softmax, prev=lambda: qk(prev) + tt_ref[g, 1], far=qk,
                    c_far=cfar_ref[g], w=lambda kt: ones_row,
                    pv=lambda kt: (lambda pr: _dot(vst_sc[kt, g], pr)))

    def group_jobs(g):
        heads = group_heads[g]
        win = {}

        def v_half(kt, lo, hi):
            return lambda pr: _dot(vwt_sc[kt, g, :, lo:hi], pr)

        def window_own_scores():
            q4 = _lane_cat([q_ref[h * HEAD_DIM:(h + 1) * HEAD_DIM, :] for h in heads])
            qa_sc[g, 0:HEAD_DIM, :] = q4
            qa_sc[g, HEAD_DIM + n_sel:, :] = jnp.zeros((HEAD_DIM - n_sel, NSA_HPG * TILE), BF16)
            return _triangle_scores(lambda lo, hi: kw_ref[0, qi, lo:hi, gsls[g]], q4,
                                    lambda lo, hi: tt_ref[g, 0, lo:hi, :], causal=True)

        def window_own(scores):
            win["own"] = _triangle_softmax(scores, v_half(qi, 0, HALF), v_half(qi, HALF, TILE), causal=True)

        def window_prev_scores():
            return _triangle_scores(lambda lo, hi: kw_ref[0, prev, lo:hi, gsls[g]], qa_sc[g, 0:HEAD_DIM, :],
                                    lambda lo, hi: twt_ref[g, lo:hi, :] + no_prev, causal=False)

        def window_prev(scores):
            m_prev, acc_prev = _triangle_softmax(scores, v_half(prev, 0, HALF), v_half(prev, HALF, TILE), causal=False)
            m_own, acc_own = win["own"]
            m_win = jnp.maximum(m_own, m_prev)
            acc_w = acc_own * jnp.exp2(m_own - m_win) + acc_prev * jnp.exp2(m_prev - m_win)
            og_sc[g] = gate(g, 2) * _normalize(acc_w)

        def compressed_scores():
            rows_per_tile = TILE // CMP_STRIDE
            c_rows = pl.ds(pl.multiple_of((nt - 1 - qi) * rows_per_tile, rows_per_tile), N_CMP_PAD)
            return (_dot(kc_ref[0, :, gsls[g]], qa_sc[g, 0:HEAD_DIM, :])
                    + _lane_cat([bct_ref[h, c_rows, :] for h in heads]))

        def compressed(s):
            e = jnp.exp2(s - jnp.max(s, axis=0, keepdims=True))
            p = jnp.where(_lane_cat([has_cmp] * NSA_HPG), e / jnp.sum(e, axis=0, keepdims=True), 0.0)
            psum = p[:, :TILE]
            for j in range(1, NSA_HPG):
                psum = psum + p[:, j * TILE:(j + 1) * TILE]
            og_sc[g] = og_sc[g] + gate(g, 0) * _dot(vct_ref[0, gsls[g], :], p.astype(BF16))

            p_hi, p_lo = _split_bf16(psum)
            imp = _dot(ovt_ref[...], p_hi) + _dot(ovt_ref[...], p_lo)
            forced = (blk == 0) | (blk == cur) | (blk == cur - 1)
            score = jnp.where(forced, FORCE_SCORE, jnp.where(blk <= cur, imp, NEG_INF))
            rank = _rank_before(score, n_sel)
            sel = jnp.where(rank < SEL_TOPN, jnp.where(score > NEG_INF / 2, 0.0, NEG_INF), NEG_INF)
            qa_sc[g, HEAD_DIM:HEAD_DIM + n_sel, :] = _lane_cat([sel.astype(BF16)] * NSA_HPG)

        return [(window_own_scores, window_own), (window_prev_scores, window_prev), (compressed_scores, compressed)]

    yield [job for g in range(NSA_GROUPS) for job in group_jobs(g)]
    yield [sel_stream(g) for g in range(NSA_GROUPS)]

    for g in range(NSA_GROUPS):
        o = og_sc[g] + gate(g, 1) * _normalize(acc_sc[g])
        for j, h in enumerate(group_heads[g]):
            o_ref[h * HEAD_DIM:(h + 1) * HEAD_DIM, :] = o[:, j * TILE:(j + 1) * TILE].astype(BF16)


def _moba_steps(c, qm_ref, km_ref, vm_ref, qx_ref, mk_ref, mv_ref, tt_ref, cfar_ref, om_ref, ox_ref,
                vmt_sc, mvt_sc, kmean_sc, qbd_sc, sel_sc, acc_sc, slot):
    nt = km_ref.shape[1]
    hsls = [slice(h * HEAD_DIM, (h + 1) * HEAD_DIM) for h in range(MOBA_HEADS)]

    @pl.when(c == 0)
    def _():
        kmean_sc[...] = jnp.zeros(kmean_sc.shape, F32)
        for n in range(nt):
            kmean_sc[n:n + 1, :] = jnp.sum(km_ref[0, n].astype(F32), axis=0, keepdims=True) * (1.0 / MOBA_BLOCK)
            vt = vm_ref[0, n].astype(F32).T
            for h in range(MOBA_HEADS):
                _store_v_aug(vmt_sc, (n, h), vt[hsls[h]])
        mvt = mv_ref[0].astype(F32).T
        for h in range(MOBA_HEADS):
            _store_v_aug(mvt_sc, h, mvt[hsls[h]])

    row_head = lax.broadcasted_iota(jnp.int32, (MOBA_W, TILE), 0) // HEAD_DIM

    def block_diag(q_ref):
        q_t = q_ref[...].astype(F32)
        return _lane_cat([jnp.where(row_head == h, q_t, 0.0) for h in range(MOBA_HEADS)]).astype(BF16)

    def per_head_pv(vts):
        return lambda pr: _lane_cat([_dot(vts(h), pr[:, h * TILE:(h + 1) * TILE]) for h in range(MOBA_HEADS)])

    def store_heads(o_t, out_ref):
        for h in range(MOBA_HEADS):
            out_ref[hsls[h], :] = o_t[:, h * TILE:(h + 1) * TILE].astype(BF16)

    n_rows = -(-nt // SUBLANES) * SUBLANES

    def gate_scores():
        qbd = block_diag(qm_ref)
        qbd_sc[...] = qbd
        km_hi, km_lo = _split_bf16(kmean_sc[...])
        return (_dot(km_hi, qbd) + _dot(km_lo, qbd))[:n_rows]

    def select(gate):
        blk = lax.broadcasted_iota(jnp.int32, gate.shape, 0)
        score = jnp.where(blk < c, gate, NEG_INF * Q_SCALE)
        rank = _rank_before(score, nt)
        sel_sc[0:n_rows, :] = jnp.where(rank < MOBA_TOPK, jnp.where(score > NEG_INF * Q_SCALE / 2, 1.0, 0.0), 0.0)

    qk = lambda n: _dot(km_ref[0, n], qbd_sc[...])
    def own_pv(lo, hi):
        def pv(pr):
            width = pr.shape[1] // MOBA_HEADS
            return _lane_cat([_dot(vmt_sc[c, h, :, lo:hi], pr[:, h * width:(h + 1) * width])
                              for h in range(MOBA_HEADS)])
        return pv

    def own_scores():
        return _triangle_scores(lambda lo, hi: km_ref[0, c, lo:hi, :], qbd_sc[...],
                                lambda lo, hi: tt_ref[0, lo:hi, :], causal=True)

    def own_softmax(scores):
        return _triangle_softmax(scores, own_pv(0, HALF), own_pv(HALF, TILE), causal=True)

    stream = dict(own_scores=own_scores, own_softmax=own_softmax,
                  prev=lambda: qk(jnp.maximum(c - 1, 0)) + tt_ref[1], far=qk,
                  c_far=cfar_ref[...], w=lambda n: sel_sc[pl.ds(n, 1), :],
                  pv=lambda n: per_head_pv(lambda h: vmt_sc[n, h]))
    def memory(s):
        store_heads(_normalize(_softmax_av([s], [per_head_pv(lambda h: mvt_sc[h])])), ox_ref)

    yield [(gate_scores, select), (lambda: _dot(mk_ref[0], block_diag(qx_ref)), memory)]
    yield [stream]
    store_heads(_normalize(acc_sc[slot]), om_ref)


N_NSA_IN, N_MOBA_IN = 13, 8
N_NSA_SCRATCH, N_MOBA_SCRATCH = 4, 5
N_STREAMS = NSA_GROUPS + 1


def _attention_kernel(*refs):
    qi = pl.program_id(1)
    nsa_in, refs = refs[:N_NSA_IN], refs[N_NSA_IN:]
    moba_in, refs = refs[:N_MOBA_IN], refs[N_MOBA_IN:]
    (o_nsa, o_moba, o_mem), refs = refs[:3], refs[3:]
    nsa_sc, refs = refs[:N_NSA_SCRATCH], refs[N_NSA_SCRATCH:]
    moba_sc, refs = refs[:N_MOBA_SCRATCH], refs[N_MOBA_SCRATCH:]
    s_sc, m_sc, acc_sc = refs
    nsa = _nsa_steps(qi, *nsa_in, o_nsa, *nsa_sc, acc_sc)
    moba = _moba_steps(qi, *moba_in, o_moba, o_mem, *moba_sc, acc_sc, NSA_GROUPS)
    jobs = next(nsa) + next(moba)
    streams = next(nsa) + next(moba)
    jobs += [job for g, stream in enumerate(streams) for job in _flash_start_jobs(g, stream, s_sc, m_sc, acc_sc)]
    _run_ahead(jobs)
    _flash_pipelined(qi, streams, s_sc, m_sc, acc_sc)
    for steps in (nsa, moba):
        for _ in steps:
            pass


def _attention(b, s, qn, gn, kc, vct, ks, vs, kw, vw, bias_cmp, t_nsa, t_win, c_far_nsa, ovt,
               qm, km, vm, qx, mk, mv, t_moba, c_far_moba):
    nt = s // TILE
    mem_len = mk.shape[0] // b
    assert MOBA_TOPK <= nt - 1 and nt <= BF16_ROWS
    n_lanes = NSA_HPG * TILE
    assert MOBA_HEADS * TILE == n_lanes
    row_spec = lambda w: pl.BlockSpec((w, TILE), lambda i, j: (0, i * nt + j))
    seq_spec = lambda w: pl.BlockSpec((1, nt, TILE, w), lambda i, j: (i, 0, 0, 0))
    per_batch = lambda rows, w: pl.BlockSpec((1, rows, w), lambda i, j: (i, 0, 0))
    tiles = lambda a: a.reshape(b, nt, TILE, a.shape[-1])
    nsa_in = [(qn, row_spec(NSA_Q_W)), (gn, row_spec(GATE_PAD)),
              (kc, per_batch(N_CMP_PAD, NSA_KV_W)), (vct, per_batch(N_CMP_PAD, NSA_KV_W)),
              (tiles(ks), seq_spec(KS_AUG_W)), (tiles(vs), seq_spec(NSA_KV_W)),
              (tiles(kw), seq_spec(NSA_KV_W)), (tiles(vw), seq_spec(NSA_KV_W)),
              (bias_cmp, _const_spec(bias_cmp.shape)), (t_nsa, _const_spec(t_nsa.shape)),
              (t_win, _const_spec(t_win.shape)), (c_far_nsa, _const_spec(c_far_nsa.shape)),
              (ovt, _const_spec(ovt.shape))]
    moba_in = [(qm, row_spec(MOBA_W)), (tiles(km), seq_spec(MOBA_W)), (tiles(vm), seq_spec(MOBA_W)),
               (qx, row_spec(MEM_W)),
               (mk.reshape(b, mem_len, MEM_W), per_batch(mem_len, MEM_W)),
               (mv.reshape(b, mem_len, MEM_W), per_batch(mem_len, MEM_W)),
               (t_moba, _const_spec(t_moba.shape)), (c_far_moba, _const_spec(c_far_moba.shape))]
    assert len(nsa_in) == N_NSA_IN and len(moba_in) == N_MOBA_IN
    nsa_scratch = [pltpu.VMEM((nt, NSA_GROUPS, V_AUG, TILE), BF16),
                   pltpu.VMEM((nt, NSA_GROUPS, V_AUG, TILE), BF16),
                   pltpu.VMEM((NSA_GROUPS, 2 * HEAD_DIM, n_lanes), BF16),
                   pltpu.VMEM((NSA_GROUPS, HEAD_DIM, n_lanes), F32)]
    moba_scratch = [pltpu.VMEM((nt, MOBA_HEADS, V_AUG, TILE), BF16),
                    pltpu.VMEM((MEM_HEADS, V_AUG, mem_len), BF16),
                    pltpu.VMEM((BF16_ROWS, MOBA_W), F32),
                    pltpu.VMEM((MOBA_W, n_lanes), BF16),
                    pltpu.VMEM((BF16_ROWS, n_lanes), F32)]
    assert len(nsa_scratch) == N_NSA_SCRATCH and len(moba_scratch) == N_MOBA_SCRATCH
    shared_scratch = [pltpu.VMEM((N_STREAMS, TILE, n_lanes), F32),
                      pltpu.VMEM((N_STREAMS, 1, n_lanes), F32),
                      pltpu.VMEM((N_STREAMS, V_AUG, n_lanes), F32)]
    inputs = nsa_in + moba_in
    return pl.pallas_call(
        _attention_kernel,
        grid=(b, nt),
        in_specs=[spec for _, spec in inputs],
        out_specs=[row_spec(NSA_Q_W), row_spec(MOBA_W), row_spec(MEM_W)],
        out_shape=[jax.ShapeDtypeStruct((w, b * s), BF16) for w in (NSA_Q_W, MOBA_W, MEM_W)],
        scratch_shapes=nsa_scratch + moba_scratch + shared_scratch,
        compiler_params=_params(("arbitrary", "arbitrary")),
        name="attention",
    )(*[a for a, _ in inputs])


def _mix_kernel(x_ref, on_ref, om_ref, ox_ref, g_pre_ref, g_post_ref, wg_ref, wn_ref, wm_ref, wx_ref,
                wo_ref, o_ref):
    x = x_ref[...]
    h = _rms(x, g_pre_ref[...]).astype(BF16)
    rows = lambda o_ref: o_ref[...].astype(F32).T.astype(BF16)
    merged = jax.nn.sigmoid(_dot(h, wg_ref[:, :D_MODEL])) * _dot(rows(on_ref), wn_ref[...])
    merged = merged + jax.nn.sigmoid(_dot(h, wg_ref[:, D_MODEL:2 * D_MODEL])) * _dot(rows(om_ref), wm_ref[...])
    merged = merged + jax.nn.sigmoid(_dot(h, wg_ref[:, 2 * D_MODEL:])) * _dot(rows(ox_ref), wx_ref[...])
    y = _dot(merged.astype(BF16), wo_ref[...])
    o_ref[...] = x + _rms(y, g_post_ref[...])


def _mix(x2, o_nsa, o_moba, o_mem, g_pre, g_post, w_gates, w_nsa_o, w_moba_o, w_mem_o, w_mix_out, tm=512):
    m = x2.shape[0]
    row = lambda w: pl.BlockSpec((tm, w), lambda i: (i, 0))
    col = lambda w: pl.BlockSpec((w, tm), lambda i: (0, i))
    return pl.pallas_call(
        _mix_kernel,
        grid=(m // tm,),
        in_specs=[row(D_MODEL), col(NSA_Q_W), col(MOBA_W), col(MEM_W),
                  _const_spec((1, D_MODEL)), _const_spec((1, D_MODEL)),
                  _const_spec(w_gates.shape), _const_spec(w_nsa_o.shape), _const_spec(w_moba_o.shape),
                  _const_spec(w_mem_o.shape), _const_spec(w_mix_out.shape)],
        out_specs=row(D_MODEL),
        out_shape=jax.ShapeDtypeStruct((m, D_MODEL), F32),
        compiler_params=_params(("parallel",)),
        name="mix",
    )(x2, o_nsa, o_moba, o_mem, g_pre, g_post, w_gates, w_nsa_o, w_moba_o, w_mem_o, w_mix_out)


FFN_CHUNK = 256


def _ffn_kernel(x_ref, g_pre_ref, g_post_ref, wg_ref, wu_ref, wd_ref, o_ref, a_sc):
    x = x_ref[...]
    h = _rms(x, g_pre_ref[...]).astype(BF16)
    d_ff = wg_ref.shape[1]
    for j in range(d_ff // FFN_CHUNK):
        sl = slice(j * FFN_CHUNK, (j + 1) * FFN_CHUNK)
        a_sc[:, sl] = (jax.nn.silu(_dot(h, wg_ref[:, sl])) * _dot(h, wu_ref[:, sl])).astype(BF16)
    f = _dot(a_sc[...], wd_ref[...])
    o_ref[...] = x + _rms(f, g_post_ref[...])


def _ffn(x2, g_pre, g_post, wg, wu, wd, tm=512):
    m = x2.shape[0]
    d_ff = wg.shape[1]
    return pl.pallas_call(
        _ffn_kernel,
        grid=(m // tm,),
        in_specs=[pl.BlockSpec((tm, D_MODEL), lambda i: (i, 0)),
                  _const_spec((1, D_MODEL)), _const_spec((1, D_MODEL)),
                  _const_spec(wg.shape), _const_spec(wu.shape), _const_spec(wd.shape)],
        out_specs=pl.BlockSpec((tm, D_MODEL), lambda i: (i, 0)),
        out_shape=jax.ShapeDtypeStruct((m, D_MODEL), F32),
        scratch_shapes=[pltpu.VMEM((tm, d_ff), BF16)],
        compiler_params=_params(("parallel",)),
        name="ffn",
    )(x2, g_pre, g_post, wg, wu, wd)


def kernel(x, mem, rel_bias, pre_mix_g, mem_norm_g, post_mix_g, w_in, cmp_pos_k, cmp_w1_k, cmp_w2_k, cmp_pos_v, cmp_w1_v, cmp_w2_v, w_mem_kv, w_nsa_o, w_moba_o, w_mem_o, w_mix_out, pre_ffn_g, post_ffn_g, w_ffn_gate, w_ffn_up, w_ffn_down):
    b, s, d_model = x.shape
    depth = w_in.shape[0]
    assert d_model == D_MODEL and s % TILE == 0 and TILE == MOBA_BLOCK == WINDOW
    assert (s - CMP_LEN) // CMP_STRIDE + 1 < N_CMP_PAD and (s // SEL_BLOCK) % SUBLANES == 0 and s // SEL_BLOCK <= HEAD_DIM
    assert w_in.shape[2] == ATT_W + 3 * D_MODEL and rel_bias.shape == (REL_BUCKETS, N_BIAS_HEADS)

    tile_idx, win_idx, cmp_idx = _bucket_tables(s)
    rel_bias = rel_bias.astype(F32)
    t_nsa = _expand(tile_idx, rel_bias, 0, NSA_HEADS, NSA_HPG).reshape(NSA_GROUPS, 2, TILE, NSA_HPG * TILE)
    t_moba = _expand(tile_idx, rel_bias, NSA_HEADS, MOBA_HEADS, MOBA_HEADS).reshape(2, TILE, MOBA_HEADS * TILE)
    t_win = _expand(win_idx, rel_bias, 0, NSA_HEADS, NSA_HPG)
    b_cmp = _expand(cmp_idx, rel_bias, 0, NSA_HEADS)
    c_far = jnp.repeat(rel_bias[REL_BUCKETS - 1] * LOG2E, TILE)
    c_far_nsa = c_far[:NSA_HEADS * TILE].reshape(NSA_GROUPS, 1, NSA_HPG * TILE)
    c_far_moba = c_far[NSA_HEADS * TILE:].reshape(1, MOBA_HEADS * TILE)
    ovt = _overlap_table(s)
    sel_cols = np.zeros((s, HEAD_DIM), np.float32)
    sel_cols[np.arange(s), np.arange(s) // SEL_BLOCK] = 1.0
    sel_cols = jnp.asarray(sel_cols, BF16)
    gate_lo = NSA_Q_W + 6 * NSA_KV_W
    rows_per_chunk = CMP_STRIDE * NSA_KV_W

    x2 = x.reshape(b * s, D_MODEL)
    mem2 = mem.reshape(-1, D_MODEL)
    for l in range(depth):
        w_att = jnp.concatenate(
            [w_in[l, :, :gate_lo + NSA_GATE_W],
             jnp.zeros((D_MODEL, GATE_PAD - NSA_GATE_W), w_in.dtype),
             w_in[l, :, gate_lo + NSA_GATE_W:ATT_W]], axis=1).astype(BF16)
        w_gates = w_in[l, :, ATT_W:].astype(BF16)
        row = lambda v: v[l].reshape(1, D_MODEL)

        qn, kc_raw, vc_raw, ks, vs, kw, vw, gn, qm, km, vm, qx = _inproj(x2, row(pre_mix_g), w_att, sel_cols)

        pk, w1k = _compress_weights(cmp_pos_k[l], cmp_w1_k[l])
        pv, w1v = _compress_weights(cmp_pos_v[l], cmp_w1_v[l])
        kc, vct = _compress(kc_raw.reshape(b, s // CMP_STRIDE, rows_per_chunk),
                            vc_raw.reshape(b, s // CMP_STRIDE, rows_per_chunk),
                            pk, pv, w1k, w1v, cmp_w2_k[l].astype(BF16), cmp_w2_v[l].astype(BF16))

        mk, mv = _memkv(mem2, row(mem_norm_g), w_mem_kv[l].astype(BF16))

        o_nsa, o_moba, o_mem = _attention(b, s, qn, gn, kc, vct, ks, vs, kw, vw, b_cmp, t_nsa, t_win, c_far_nsa, ovt,
                                          qm, km, vm, qx, mk, mv, t_moba, c_far_moba)

        x2 = _mix(x2, o_nsa, o_moba, o_mem, row(pre_mix_g), row(post_mix_g), w_gates,
                  w_nsa_o[l].astype(BF16), w_moba_o[l].astype(BF16), w_mem_o[l].astype(BF16),
                  w_mix_out[l].astype(BF16))
        x2 = _ffn(x2, row(pre_ffn_g), row(post_ffn_g), w_ffn_gate[l].astype(BF16),
                  w_ffn_up[l].astype(BF16), w_ffn_down[l].astype(BF16))
    return x2.reshape(b, s, D_MODEL)
```

```python
import functools
import math

import numpy as np
import jax
import jax.numpy as jnp
from jax import lax
from jax.experimental import pallas as pl
from jax.experimental.pallas import tpu as pltpu

F32 = jnp.float32
BF16 = jnp.bfloat16

D_MODEL = 1024
HEAD_DIM = 64
SCALE = HEAD_DIM ** -0.5
LOG2E = math.log2(math.e)
Q_SCALE = SCALE * LOG2E
NSA_HEADS = 8
NSA_GROUPS = 2
NSA_HPG = NSA_HEADS // NSA_GROUPS
CMP_LEN = 32
CMP_STRIDE = 16
CMP_HIDDEN = 128
SEL_BLOCK = 64
SEL_TOPN = 8
WINDOW = 256
MOBA_HEADS = 4
MOBA_BLOCK = 256
MOBA_TOPK = 3
MEM_HEADS = 4
REL_BUCKETS = 32
REL_MAX_DIST = 128
N_BIAS_HEADS = NSA_HEADS + MOBA_HEADS
RMS_EPS = 1e-6
NEG_INF = -1e30
FORCE_SCORE = 1e4

NSA_Q_W = NSA_HEADS * HEAD_DIM
NSA_KV_W = NSA_GROUPS * HEAD_DIM
NSA_GATE_W = NSA_HEADS * 3
MOBA_W = MOBA_HEADS * HEAD_DIM
MEM_W = MEM_HEADS * HEAD_DIM
ATT_W = NSA_Q_W + 6 * NSA_KV_W + NSA_GATE_W + 3 * MOBA_W + MEM_W
LANES = 128
SUBLANES = 8
BF16_ROWS = 16
MXU_COLS = 256
GATE_PAD = LANES
TILE = 256
HALF = TILE // 2
RUN_AHEAD = 2
N_CMP_PAD = 128
V_AUG = HEAD_DIM + BF16_ROWS
MASKED_BUCKET = REL_BUCKETS
VMEM_LIMIT = 56 * 1024 * 1024


def _dot(a, b):
    return jnp.dot(a, b, preferred_element_type=F32)


def _split_bf16(x):
    hi = x.astype(BF16)
    lo = (x - hi.astype(F32)).astype(BF16)
    return hi, lo


def _rms(x, g):
    return x * lax.rsqrt(jnp.mean(x * x, axis=-1, keepdims=True) + RMS_EPS) * g


def _params(sem):
    return pltpu.CompilerParams(dimension_semantics=sem, vmem_limit_bytes=VMEM_LIMIT)


def _const_spec(shape):
    nd = len(shape)
    return pl.BlockSpec(shape, lambda *_: (0,) * nd, pipeline_mode=pl.Buffered(1))


_INPROJ_OUTS = (
    ("qn", NSA_Q_W, BF16, True),
    ("kc", NSA_KV_W, F32, False), ("vc", NSA_KV_W, F32, False),
    ("ks", NSA_KV_W, BF16, False), ("vs", NSA_KV_W, BF16, False),
    ("kw", NSA_KV_W, BF16, False), ("vw", NSA_KV_W, BF16, False),
    ("gn", GATE_PAD, F32, False),
    ("qm", MOBA_W, BF16, True), ("km", MOBA_W, BF16, False), ("vm", MOBA_W, BF16, False),
    ("qx", MEM_W, BF16, True),
)
_INPROJ_W = sum(o[1] for o in _INPROJ_OUTS)
_INPROJ_CHUNKED = ("kc", "vc")
_INPROJ_TRANSPOSED = ("qn", "gn", "qm", "qx")
KS_AUG_W = NSA_GROUPS * 2 * HEAD_DIM


def _inproj_out_width(name, width):
    return KS_AUG_W if name == "ks" else width


def _inproj_kernel(x_ref, g_ref, w_ref, e_ref, *refs):
    out_refs, rows_sc = refs[:-1], refs[-1]
    h = _rms(x_ref[...], g_ref[...]).astype(BF16)
    runs, lo = [], 0
    for out in zip(_INPROJ_OUTS, out_refs):
        if runs and runs[-1][1] < MXU_COLS:
            runs[-1][0].append(out)
            runs[-1][1] += out[0][1]
        else:
            runs.append([[out], out[0][1], lo])
        lo += out[0][1]
    for outs, run_width, run_lo in runs:
        y_run = _dot(h, w_ref[:, run_lo:run_lo + run_width])
        lo = 0
        for (name, width, dtype, scaled), o_ref in outs:
            y = y_run[:, lo:lo + width]
            if scaled:
                y = y * Q_SCALE
            if name == "gn":
                y = jax.nn.sigmoid(y)
            if name in _INPROJ_TRANSPOSED:
                y = y.T
            y = y.astype(dtype)
            if name == "ks":
                e = e_ref[...]
                y = _lane_cat([y[:, :HEAD_DIM], e, y[:, HEAD_DIM:], e])
            if name in _INPROJ_CHUNKED:
                rows_sc[...] = y
                for j in range(CMP_STRIDE):
                    o_ref[:, j * width:(j + 1) * width] = rows_sc[pl.ds(j, o_ref.shape[0], stride=CMP_STRIDE), :]
            else:
                o_ref[...] = y
            lo += width


def _inproj(x2, g, w, e_cols, tm=512):
    m = x2.shape[0]
    tiles_per_seq = e_cols.shape[0] // tm
    out_specs, out_shape = [], []
    for name, width, dtype, _ in _INPROJ_OUTS:
        if name in _INPROJ_TRANSPOSED:
            out_specs.append(pl.BlockSpec((width, tm), lambda i: (0, i)))
            out_shape.append(jax.ShapeDtypeStruct((width, m), dtype))
            continue
        rows, width = (CMP_STRIDE, CMP_STRIDE * width) if name in _INPROJ_CHUNKED else (1, _inproj_out_width(name, width))
        out_specs.append(pl.BlockSpec((tm // rows, width), lambda i: (i, 0)))
        out_shape.append(jax.ShapeDtypeStruct((m // rows, width), dtype))
    return pl.pallas_call(
        _inproj_kernel,
        grid=(m // tm,),
        in_specs=[pl.BlockSpec((tm, D_MODEL), lambda i: (i, 0)),
                  _const_spec((1, D_MODEL)),
                  _const_spec((D_MODEL, _INPROJ_W)),
                  pl.BlockSpec((tm, HEAD_DIM), lambda i: (i % tiles_per_seq, 0))],
        out_specs=out_specs,
        out_shape=out_shape,
        scratch_shapes=[pltpu.VMEM((tm, NSA_KV_W), F32)],
        compiler_params=_params(("parallel",)),
        name="inproj",
    )(x2, g, w, e_cols)


def _compress_kernel(rk_ref, rv_ref, pk_ref, pv_ref, w1k_ref, w1v_ref, w2k_ref, w2v_ref, kc_ref, vc_ref):
    nb = rk_ref.shape[0]

    def one(r_ref, p_ref, w1_ref, w2_ref):
        r = r_ref[...].reshape(nb * N_CMP_PAD, r_ref.shape[2])
        top = _dot((r + p_ref[0:1, :]).astype(BF16), w1_ref[0])
        bot = _dot((r + p_ref[1:2, :]).astype(BF16), w1_ref[1])
        hid = top + pltpu.roll(bot, nb * N_CMP_PAD - 1, 0)
        act = jax.nn.gelu(hid).astype(BF16)
        return jnp.concatenate(
            [_dot(act[:, g * CMP_HIDDEN:(g + 1) * CMP_HIDDEN], w2_ref[...]) for g in range(NSA_GROUPS)], axis=1)

    k_out = one(rk_ref, pk_ref, w1k_ref, w2k_ref)
    v_out = one(rv_ref, pv_ref, w1v_ref, w2v_ref)
    for n in range(nb):
        rows = slice(n * N_CMP_PAD, (n + 1) * N_CMP_PAD)
        kc_ref[n] = k_out[rows].astype(BF16)
        vc_ref[n] = v_out[rows].T.astype(BF16)


def _compress(rk, rv, pk, pv, w1k, w1v, w2k, w2v):
    b = rk.shape[0]
    rw = rk.shape[2]
    nb = 2 if b % 2 == 0 else 1
    r_spec = pl.BlockSpec((nb, N_CMP_PAD, rw), lambda i: (i, 0, 0))
    o_spec = pl.BlockSpec((nb, N_CMP_PAD, NSA_KV_W), lambda i: (i, 0, 0))
    return pl.pallas_call(
        _compress_kernel,
        grid=(b // nb,),
        in_specs=[r_spec, r_spec, _const_spec(pk.shape), _const_spec(pv.shape),
                  _const_spec(w1k.shape), _const_spec(w1v.shape),
                  _const_spec(w2k.shape), _const_spec(w2v.shape)],
        out_specs=[o_spec, o_spec],
        out_shape=[jax.ShapeDtypeStruct((b, N_CMP_PAD, NSA_KV_W), BF16)] * 2,
        compiler_params=_params(("parallel",)),
        name="compress",
    )(rk, rv, pk, pv, w1k, w1v, w2k, w2v)


def _compress_weights(pos, w1):
    half = CMP_LEN // 2
    p = pos.reshape(2, half, 1, HEAD_DIM)
    p = jnp.broadcast_to(p, (2, half, NSA_GROUPS, HEAD_DIM)).reshape(2, half * NSA_KV_W)
    w = w1.reshape(2, half, HEAD_DIM, CMP_HIDDEN)
    eye = jnp.eye(NSA_GROUPS, dtype=w1.dtype)
    wbd = jnp.einsum("ajdm,gk->ajgdkm", w, eye).reshape(2, half * NSA_KV_W, NSA_GROUPS * CMP_HIDDEN)
    return p.astype(F32), wbd.astype(BF16)


def _memkv_kernel(m_ref, g_ref, w_ref, k_ref, v_ref):
    h = _rms(m_ref[...], g_ref[...]).astype(BF16)
    k_ref[...] = _dot(h, w_ref[:, :MEM_W]).astype(BF16)
    v_ref[...] = _dot(h, w_ref[:, MEM_W:]).astype(BF16)


def _memkv(mem2, g, w, tm=512):
    m = mem2.shape[0]
    tm = min(tm, m)
    o_spec = pl.BlockSpec((tm, MEM_W), lambda i: (i, 0))
    return pl.pallas_call(
        _memkv_kernel,
        grid=(m // tm,),
        in_specs=[pl.BlockSpec((tm, D_MODEL), lambda i: (i, 0)), _const_spec((1, D_MODEL)),
                  _const_spec((D_MODEL, 2 * MEM_W))],
        out_specs=[o_spec, o_spec],
        out_shape=[jax.ShapeDtypeStruct((m, MEM_W), BF16)] * 2,
        compiler_params=_params(("parallel",)),
        name="memkv",
    )(mem2, g, w)


def _expand_kernel(idx_ref, bias_ref, o_ref, *, head0, n_heads, heads_per_group):
    rows, cols = idx_ref.shape

    def body(i, carry):
        r = pl.multiple_of(i * SUBLANES, SUBLANES)
        for c0 in range(0, cols, TILE):
            idx = idx_ref[pl.ds(r, SUBLANES), c0:c0 + TILE]
            out = [jnp.full(idx.shape, NEG_INF, F32)] * n_heads
            for bkt in range(REL_BUCKETS):
                hit = idx == bkt
                out = [jnp.where(hit, bias_ref[bkt, head0 + h], out[h]) for h in range(n_heads)]
            for h in range(n_heads):
                col = (h % heads_per_group) * cols + c0
                o_ref[h // heads_per_group, pl.ds(r, SUBLANES), col:col + TILE] = out[h] * LOG2E
        return carry

    lax.fori_loop(0, rows // SUBLANES, body, 0)


def _expand(idx, rel_bias, head0, n_heads, heads_per_group=1):
    rows, cols = idx.shape
    return pl.pallas_call(
        functools.partial(_expand_kernel, head0=head0, n_heads=n_heads, heads_per_group=heads_per_group),
        in_specs=[pl.BlockSpec(memory_space=pltpu.VMEM), pl.BlockSpec(memory_space=pltpu.SMEM)],
        out_specs=pl.BlockSpec(memory_space=pltpu.VMEM),
        out_shape=jax.ShapeDtypeStruct((n_heads // heads_per_group, rows, heads_per_group * cols), F32),
        compiler_params=pltpu.CompilerParams(vmem_limit_bytes=VMEM_LIMIT),
        name="bias_expand",
    )(idx, rel_bias)


def _t5_bucket_np(dist):
    dist = np.maximum(dist, 0)
    max_exact = REL_BUCKETS // 2
    logd = np.log(np.maximum(dist, 1).astype(np.float32) / max_exact) / math.log(REL_MAX_DIST / max_exact)
    large = np.minimum(max_exact + (logd * (REL_BUCKETS - max_exact)).astype(np.int32), REL_BUCKETS - 1)
    return np.where(dist < max_exact, dist, large).astype(np.int32)


def _bucket_tables(s):
    j = np.arange(TILE)[:, None]
    i = np.arange(TILE)[None, :]
    assert TILE + 1 >= REL_MAX_DIST
    tiles = []
    for d in range(2):
        dist = d * TILE + i - j
        tiles.append(np.where(dist >= 0, _t5_bucket_np(dist), MASKED_BUCKET))
    dist1 = TILE + i - j
    win = np.where(dist1 < WINDOW, _t5_bucket_np(dist1), MASKED_BUCKET)
    n_cmp = (s - CMP_LEN) // CMP_STRIDE + 1
    assert n_cmp * CMP_STRIDE + CMP_LEN - 1 >= s or n_cmp == N_CMP_PAD
    rel = np.arange(N_CMP_PAD + (s - TILE) // CMP_STRIDE)[:, None] - (s - TILE) // CMP_STRIDE
    dist_c = i - (rel * CMP_STRIDE + CMP_LEN - 1)
    cmp_idx = np.where(dist_c >= 0, _t5_bucket_np(dist_c), MASKED_BUCKET)
    as_i32 = lambda a: jnp.asarray(a.astype(np.int32))
    return as_i32(np.concatenate(tiles, axis=0)), as_i32(win), as_i32(cmp_idx)


def _overlap_table(s):
    n_cmp = (s - CMP_LEN) // CMP_STRIDE + 1
    n_sel = s // SEL_BLOCK
    cs = np.arange(n_cmp) * CMP_STRIDE
    ss = np.arange(n_sel) * SEL_BLOCK
    ov = np.clip(np.minimum(cs[:, None] + CMP_LEN, ss[None, :] + SEL_BLOCK)
                 - np.maximum(cs[:, None], ss[None, :]), 0, None).astype(np.float32) / CMP_LEN
    ovt = np.zeros((n_sel, N_CMP_PAD), np.float32)
    ovt[:, :n_cmp] = ov.T
    return jnp.asarray(ovt, BF16)


def _store_v_aug(vt_sc, idx, vt):
    ones = jnp.ones((BF16_ROWS, vt.shape[1]), BF16)
    vt_sc[idx] = jnp.concatenate([vt.astype(BF16), ones], axis=0)


def _lane_cat(xs):
    return jnp.concatenate(xs, axis=1)


def _query_halves(x):
    n = x.shape[-1] // TILE
    first = _lane_cat([x[:, k * TILE:k * TILE + HALF] for k in range(n)])
    second = _lane_cat([x[:, k * TILE + HALF:(k + 1) * TILE] for k in range(n)])
    return first, second


def _join_query_halves(first, second):
    n = first.shape[-1] // HALF
    return _lane_cat([part for k in range(n)
                      for part in (first[:, k * HALF:(k + 1) * HALF], second[:, k * HALF:(k + 1) * HALF])])


def _triangle_tile(k, q, table, pv_lo, pv_hi, causal):
    return _triangle_softmax(_triangle_scores(k, q, table, causal), pv_lo, pv_hi, causal)


def _triangle_scores(k, q, table, causal):
    q_first, q_second = _query_halves(q)
    lo, hi = (0, HALF), (HALF, TILE)
    if causal:
        return _dot(k(*lo), q) + table(*lo), _dot(k(*hi), q_second) + _query_halves(table(*hi))[1]
    return _dot(k(*hi), q) + table(*hi), _dot(k(*lo), q_first) + _query_halves(table(*lo))[0]


def _triangle_softmax(scores, pv_lo, pv_hi, causal):
    s_wide, s_narrow = scores
    pv_wide, pv_narrow = (pv_lo, pv_hi) if causal else (pv_hi, pv_lo)
    mw_first, mw_second = _query_halves(jnp.max(s_wide, axis=0, keepdims=True))
    m_narrow = jnp.max(s_narrow, axis=0, keepdims=True)
    if causal:
        m_narrow = jnp.maximum(m_narrow, mw_second)
        m = _join_query_halves(mw_first, m_narrow)
    else:
        m_narrow = jnp.maximum(m_narrow, mw_first)
        m = _join_query_halves(m_narrow, mw_second)
    aw_first, aw_second = _query_halves(pv_wide(jnp.exp2(s_wide - m).astype(BF16)))
    a_narrow = pv_narrow(jnp.exp2(s_narrow - m_narrow).astype(BF16))
    if causal:
        return m, _join_query_halves(aw_first, aw_second + a_narrow)
    return m, _join_query_halves(aw_first + a_narrow, aw_second)


def _flash_pipelined(own, streams, s_sc, m_ref, acc_ref):
    has_prev = jnp.where(own > 0, 1.0, 0.0).astype(F32)
    prev = jnp.maximum(own - 1, 0)
    n_far = jnp.maximum(own - 1, 0)

    def absorb(g, s, kt, c_row, w_row):
        u = jnp.max(s, axis=0, keepdims=True) + c_row
        m_old = m_ref[g]
        m_new = jnp.maximum(m_old, jnp.where(w_row > 0.0, u, NEG_INF))
        alpha = jnp.exp2(m_old - m_new)
        shift = jnp.maximum(m_new, u) - c_row
        p = jnp.exp2(s - shift).astype(BF16)
        acc_ref[g] = alpha * acc_ref[g] + w_row * streams[g]["pv"](kt)(p)
        m_ref[g] = m_new

    def absorb_slot(g, i):
        is_prev = i == 0
        kt = jnp.where(is_prev, prev, i - 1)
        c_row = jnp.where(is_prev, 0.0, streams[g]["c_far"])
        w_row = streams[g]["w"](kt) * jnp.where(is_prev, has_prev, 1.0)
        absorb(g, s_sc[g], kt, c_row, w_row)

    def body(i, carry):
        for g in range(len(streams)):
            nxt = streams[g]["far"](i)
            absorb_slot(g, i)
            s_sc[g] = nxt
        return carry

    lax.fori_loop(0, n_far, body, 0)
    for g in range(len(streams)):
        absorb_slot(g, n_far)


def _run_ahead(jobs):
    depth = RUN_AHEAD
    pending = [job[0]() for job in jobs[:depth]]
    for k, (_, consume) in enumerate(jobs):
        if k + depth < len(jobs):
            pending.append(jobs[k + depth][0]())
        consume(pending.pop(0))


def _flash_start_jobs(g, stream, s_sc, m_ref, acc_ref):
    def init(scores):
        m_ref[g], acc_ref[g] = stream["own_softmax"](scores)

    def park(scores):
        s_sc[g] = scores

    return [(stream["own_scores"], init), (stream["prev"], park)]


def _softmax_av(s_list, pv_list):
    m = s_list[0].max(axis=0, keepdims=True)
    for s in s_list[1:]:
        m = jnp.maximum(m, s.max(axis=0, keepdims=True))
    acc = None
    for s, pv in zip(s_list, pv_list):
        part = pv(jnp.exp2(s - m).astype(BF16))
        acc = part if acc is None else acc + part
    return acc


def _normalize(acc):
    return acc[:HEAD_DIM] * (1.0 / acc[HEAD_DIM:HEAD_DIM + 1])


def _rank_before(score, n_cand):
    ranks = []
    for r0 in range(0, score.shape[0], SUBLANES):
        tile = score[r0:r0 + SUBLANES]
        blk = lax.broadcasted_iota(jnp.int32, tile.shape, 0) + r0
        rank = jnp.zeros(tile.shape, F32)
        for m in range(n_cand):
            row = score[m:m + 1, :]
            if m < r0:
                before = jnp.where(row >= tile, 1.0, 0.0)
            elif m >= r0 + SUBLANES:
                before = jnp.where(row > tile, 1.0, 0.0)
            else:
                before = jnp.where(blk > m, jnp.where(row >= tile, 1.0, 0.0), jnp.where(row > tile, 1.0, 0.0))
            rank = rank + before
        ranks.append(rank)
    return jnp.concatenate(ranks, axis=0)


def _nsa_steps(qi, q_ref, gn_ref, kc_ref, vct_ref, ks_ref, vs_ref, kw_ref, vw_ref,
               bct_ref, tt_ref, twt_ref, cfar_ref, ovt_ref, o_ref,
               vst_sc, vwt_sc, qa_sc, og_sc, acc_sc):
    nt = ks_ref.shape[1]
    n_sel = ovt_ref.shape[0]

    @pl.when(qi == 0)
    def _():
        for kt in range(nt):
            vs_t = vs_ref[0, kt].astype(F32).T
            vw_t = vw_ref[0, kt].astype(F32).T
            for g in range(NSA_GROUPS):
                _store_v_aug(vst_sc, (kt, g), vs_t[g * HEAD_DIM:(g + 1) * HEAD_DIM])
                _store_v_aug(vwt_sc, (kt, g), vw_t[g * HEAD_DIM:(g + 1) * HEAD_DIM])

    pos = lax.broadcasted_iota(jnp.int32, (1, TILE), 1) + qi * TILE
    cur = pos // SEL_BLOCK
    has_cmp = pos >= CMP_LEN - 1
    blk = lax.broadcasted_iota(jnp.int32, (n_sel, TILE), 0)
    prev = jnp.maximum(qi - 1, 0)
    gates = gn_ref[...]

    gsls = [slice(g * HEAD_DIM, (g + 1) * HEAD_DIM) for g in range(NSA_GROUPS)]
    group_heads = [[g * NSA_HPG + j for j in range(NSA_HPG)] for g in range(NSA_GROUPS)]

    def gate(g, branch):
        return _lane_cat([gates[3 * h + branch:3 * h + branch + 1, :] for h in group_heads[g]])

    ones_row = jnp.ones((1, NSA_HPG * TILE), F32)

    def sel_stream(g):
        def qk(kt):
            return _dot(ks_ref[0, kt, :, g * 2 * HEAD_DIM:(g + 1) * 2 * HEAD_DIM], qa_sc[g])

        def own_scores():
            return _triangle_scores(
                lambda lo, hi: ks_ref[0, qi, lo:hi, g * 2 * HEAD_DIM:(g + 1) * 2 * HEAD_DIM], qa_sc[g],
                lambda lo, hi: tt_ref[g, 0, lo:hi, :], causal=True)

        def own_softmax(scores):
            return _triangle_softmax(
                scores, lambda pr: _dot(vst_sc[qi, g, :, 0:HALF], pr),
                lambda pr: _dot(vst_sc[qi, g, :, HALF:TILE], pr), causal=True)

        return dict(own_scores=own_scores, own_softmax=own_softmax, prev=lambda: qk(prev) + tt_ref[g, 1], far=qk,
                    c_far=cfar_ref[g], w=lambda kt: ones_row,
                    pv=lambda kt: (lambda pr: _dot(vst_sc[kt, g], pr)))

    def group_jobs(g):
        heads = group_heads[g]
        win = {}

        def v_half(kt, lo, hi):
            return lambda pr: _dot(vwt_sc[kt, g, :, lo:hi], pr)

        def window_own_scores():
            q4 = _lane_cat([q_ref[h * HEAD_DIM:(h + 1) * HEAD_DIM, :] for h in heads])
            qa_sc[g, 0:HEAD_DIM, :] = q4
            qa_sc[g, HEAD_DIM + n_sel:, :] = jnp.zeros((HEAD_DIM - n_sel, NSA_HPG * TILE), BF16)
            return _triangle_scores(lambda lo, hi: kw_ref[0, qi, lo:hi, gsls[g]], q4,
                                    lambda lo, hi: tt_ref[g, 0, lo:hi, :], causal=True)

        def window_own(scores):
            win["own"] = _triangle_softmax(scores, v_half(qi, 0, HALF), v_half(qi, HALF, TILE), causal=True)

        def window_prev_scores():
            return _triangle_scores(lambda lo, hi: kw_ref[0, prev, lo:hi, gsls[g]], qa_sc[g, 0:HEAD_DIM, :],
                                    lambda lo, hi: twt_ref[g, lo:hi, :], causal=False)

        def window_prev(scores):
            m_prev, acc_prev = _triangle_softmax(scores, v_half(prev, 0, HALF), v_half(prev, HALF, TILE), causal=False)
            m_prev = jnp.where(qi == 0, NEG_INF, m_prev)
            m_own, acc_own = win["own"]
            m_win = jnp.maximum(m_own, m_prev)
            acc_w = acc_own * jnp.exp2(m_own - m_win) + acc_prev * jnp.exp2(m_prev - m_win)
            og_sc[g] = gate(g, 2) * _normalize(acc_w)

        def compressed_scores():
            rows_per_tile = TILE // CMP_STRIDE
            c_rows = pl.ds(pl.multiple_of((nt - 1 - qi) * rows_per_tile, rows_per_tile), N_CMP_PAD)
            return (_dot(kc_ref[0, :, gsls[g]], qa_sc[g, 0:HEAD_DIM, :])
                    + _lane_cat([bct_ref[h, c_rows, :] for h in heads]))

        def compressed(s):
            e = jnp.exp2(s - jnp.max(s, axis=0, keepdims=True))
            p = e * jnp.where(_lane_cat([has_cmp] * NSA_HPG), 1.0 / jnp.sum(e, axis=0, keepdims=True), 0.0)
            psum = p[:, :TILE]
            for j in range(1, NSA_HPG):
                psum = psum + p[:, j * TILE:(j + 1) * TILE]
            og_sc[g] = og_sc[g] + gate(g, 0) * _dot(vct_ref[0, gsls[g], :], p.astype(BF16))

            p_hi, p_lo = _split_bf16(psum)
            imp = _dot(ovt_ref[...], p_hi) + _dot(ovt_ref[...], p_lo)
            forced = (blk == 0) | (blk == cur) | (blk == cur - 1)
            score = jnp.where(forced, FORCE_SCORE, jnp.where(blk <= cur, imp, NEG_INF))
            rank = _rank_before(score, n_sel)
            sel = jnp.where(rank < SEL_TOPN, jnp.where(score > NEG_INF / 2, 0.0, NEG_INF), NEG_INF)
            qa_sc[g, HEAD_DIM:HEAD_DIM + n_sel, :] = _lane_cat([sel.astype(BF16)] * NSA_HPG)

        return [(window_own_scores, window_own), (window_prev_scores, window_prev), (compressed_scores, compressed)]

    yield [job for g in range(NSA_GROUPS) for job in group_jobs(g)]
    yield [sel_stream(g) for g in range(NSA_GROUPS)]

    for g in range(NSA_GROUPS):
        o = og_sc[g] + gate(g, 1) * _normalize(acc_sc[g])
        for j, h in enumerate(group_heads[g]):
            o_ref[h * HEAD_DIM:(h + 1) * HEAD_DIM, :] = o[:, j * TILE:(j + 1) * TILE].astype(BF16)


def _moba_steps(c, qm_ref, km_ref, vm_ref, qx_ref, mk_ref, mv_ref, tt_ref, cfar_ref, om_ref, ox_ref,
                vmt_sc, mvt_sc, kmean_sc, qbd_sc, sel_sc, acc_sc, slot):
    nt = km_ref.shape[1]
    hsls = [slice(h * HEAD_DIM, (h + 1) * HEAD_DIM) for h in range(MOBA_HEADS)]

    @pl.when(c == 0)
    def _():
        kmean_sc[...] = jnp.zeros(kmean_sc.shape, F32)
        for n in range(nt):
            kmean_sc[n:n + 1, :] = jnp.sum(km_ref[0, n].astype(F32), axis=0, keepdims=True) * (1.0 / MOBA_BLOCK)
            vt = vm_ref[0, n].astype(F32).T
            for h in range(MOBA_HEADS):
                _store_v_aug(vmt_sc, (n, h), vt[hsls[h]])
        mvt = mv_ref[0].astype(F32).T
        for h in range(MOBA_HEADS):
            _store_v_aug(mvt_sc, h, mvt[hsls[h]])

    row_head = lax.broadcasted_iota(jnp.int32, (MOBA_W, TILE), 0) // HEAD_DIM

    def block_diag(q_ref):
        q_t = q_ref[...].astype(F32)
        return _lane_cat([jnp.where(row_head == h, q_t, 0.0) for h in range(MOBA_HEADS)]).astype(BF16)

    def per_head_pv(vts):
        return lambda pr: _lane_cat([_dot(vts(h), pr[:, h * TILE:(h + 1) * TILE]) for h in range(MOBA_HEADS)])

    def store_heads(o_t, out_ref):
        for h in range(MOBA_HEADS):
            out_ref[hsls[h], :] = o_t[:, h * TILE:(h + 1) * TILE].astype(BF16)

    n_rows = -(-nt // SUBLANES) * SUBLANES

    def gate_scores():
        qbd = block_diag(qm_ref)
        qbd_sc[...] = qbd
        km_hi, km_lo = _split_bf16(kmean_sc[...])
        return (_dot(km_hi, qbd) + _dot(km_lo, qbd))[:n_rows]

    def select(gate):
        blk = lax.broadcasted_iota(jnp.int32, gate.shape, 0)
        score = jnp.where(blk < c, gate, NEG_INF * Q_SCALE)
        rank = _rank_before(score, nt)
        sel_sc[0:n_rows, :] = jnp.where(rank < MOBA_TOPK, jnp.where(score > NEG_INF * Q_SCALE / 2, 1.0, 0.0), 0.0)

    qk = lambda n: _dot(km_ref[0, n], qbd_sc[...])
    def own_pv(lo, hi):
        def pv(pr):
            width = pr.shape[1] // MOBA_HEADS
            return _lane_cat([_dot(vmt_sc[c, h, :, lo:hi], pr[:, h * width:(h + 1) * width])
                              for h in range(MOBA_HEADS)])
        return pv

    def own_scores():
        return _triangle_scores(lambda lo, hi: km_ref[0, c, lo:hi, :], qbd_sc[...],
                                lambda lo, hi: tt_ref[0, lo:hi, :], causal=True)

    def own_softmax(scores):
        return _triangle_softmax(scores, own_pv(0, HALF), own_pv(HALF, TILE), causal=True)

    stream = dict(own_scores=own_scores, own_softmax=own_softmax,
                  prev=lambda: qk(jnp.maximum(c - 1, 0)) + tt_ref[1], far=qk,
                  c_far=cfar_ref[...], w=lambda n: sel_sc[pl.ds(n, 1), :],
                  pv=lambda n: per_head_pv(lambda h: vmt_sc[n, h]))
    def memory(s):
        store_heads(_normalize(_softmax_av([s], [per_head_pv(lambda h: mvt_sc[h])])), ox_ref)

    yield [(gate_scores, select), (lambda: _dot(mk_ref[0], block_diag(qx_ref)), memory)]
    yield [stream]
    store_heads(_normalize(acc_sc[slot]), om_ref)


N_NSA_IN, N_MOBA_IN = 13, 8
N_NSA_SCRATCH, N_MOBA_SCRATCH = 4, 5
N_STREAMS = NSA_GROUPS + 1


def _attention_kernel(*refs):
    qi = pl.program_id(1)
    nsa_in, refs = refs[:N_NSA_IN], refs[N_NSA_IN:]
    moba_in, refs = refs[:N_MOBA_IN], refs[N_MOBA_IN:]
    (o_nsa, o_moba, o_mem), refs = refs[:3], refs[3:]
    nsa_sc, refs = refs[:N_NSA_SCRATCH], refs[N_NSA_SCRATCH:]
    moba_sc, refs = refs[:N_MOBA_SCRATCH], refs[N_MOBA_SCRATCH:]
    s_sc, m_sc, acc_sc = refs
    nsa = _nsa_steps(qi, *nsa_in, o_nsa, *nsa_sc, acc_sc)
    moba = _moba_steps(qi, *moba_in, o_moba, o_mem, *moba_sc, acc_sc, NSA_GROUPS)
    jobs = next(nsa) + next(moba)
    streams = next(nsa) + next(moba)
    jobs += [job for g, stream in enumerate(streams) for job in _flash_start_jobs(g, stream, s_sc, m_sc, acc_sc)]
    _run_ahead(jobs)
    _flash_pipelined(qi, streams, s_sc, m_sc, acc_sc)
    for steps in (nsa, moba):
        for _ in steps:
            pass


def _attention(b, s, qn, gn, kc, vct, ks, vs, kw, vw, bias_cmp, t_nsa, t_win, c_far_nsa, ovt,
               qm, km, vm, qx, mk, mv, t_moba, c_far_moba):
    nt = s // TILE
    mem_len = mk.shape[0] // b
    assert MOBA_TOPK <= nt - 1 and nt <= BF16_ROWS
    n_lanes = NSA_HPG * TILE
    assert MOBA_HEADS * TILE == n_lanes
    row_spec = lambda w: pl.BlockSpec((w, TILE), lambda i, j: (0, i * nt + j))
    seq_spec = lambda w: pl.BlockSpec((1, nt, TILE, w), lambda i, j: (i, 0, 0, 0))
    per_batch = lambda rows, w: pl.BlockSpec((1, rows, w), lambda i, j: (i, 0, 0))
    tiles = lambda a: a.reshape(b, nt, TILE, a.shape[-1])
    nsa_in = [(qn, row_spec(NSA_Q_W)), (gn, row_spec(GATE_PAD)),
              (kc, per_batch(N_CMP_PAD, NSA_KV_W)), (vct, per_batch(N_CMP_PAD, NSA_KV_W)),
              (tiles(ks), seq_spec(KS_AUG_W)), (tiles(vs), seq_spec(NSA_KV_W)),
              (tiles(kw), seq_spec(NSA_KV_W)), (tiles(vw), seq_spec(NSA_KV_W)),
              (bias_cmp, _const_spec(bias_cmp.shape)), (t_nsa, _const_spec(t_nsa.shape)),
              (t_win, _const_spec(t_win.shape)), (c_far_nsa, _const_spec(c_far_nsa.shape)),
              (ovt, _const_spec(ovt.shape))]
    moba_in = [(qm, row_spec(MOBA_W)), (tiles(km), seq_spec(MOBA_W)), (tiles(vm), seq_spec(MOBA_W)),
               (qx, row_spec(MEM_W)),
               (mk.reshape(b, mem_len, MEM_W), per_batch(mem_len, MEM_W)),
               (mv.reshape(b, mem_len, MEM_W), per_batch(mem_len, MEM_W)),
               (t_moba, _const_spec(t_moba.shape)), (c_far_moba, _const_spec(c_far_moba.shape))]
    assert len(nsa_in) == N_NSA_IN and len(moba_in) == N_MOBA_IN
    nsa_scratch = [pltpu.VMEM((nt, NSA_GROUPS, V_AUG, TILE), BF16),
                   pltpu.VMEM((nt, NSA_GROUPS, V_AUG, TILE), BF16),
                   pltpu.VMEM((NSA_GROUPS, 2 * HEAD_DIM, n_lanes), BF16),
                   pltpu.VMEM((NSA_GROUPS, HEAD_DIM, n_lanes), F32)]
    moba_scratch = [pltpu.VMEM((nt, MOBA_HEADS, V_AUG, TILE), BF16),
                    pltpu.VMEM((MEM_HEADS, V_AUG, mem_len), BF16),
                    pltpu.VMEM((BF16_ROWS, MOBA_W), F32),
                    pltpu.VMEM((MOBA_W, n_lanes), BF16),
                    pltpu.VMEM((BF16_ROWS, n_lanes), F32)]
    assert len(nsa_scratch) == N_NSA_SCRATCH and len(moba_scratch) == N_MOBA_SCRATCH
    shared_scratch = [pltpu.VMEM((N_STREAMS, TILE, n_lanes), F32),
                      pltpu.VMEM((N_STREAMS, 1, n_lanes), F32),
                      pltpu.VMEM((N_STREAMS, V_AUG, n_lanes), F32)]
    inputs = nsa_in + moba_in
    return pl.pallas_call(
        _attention_kernel,
        grid=(b, nt),
        in_specs=[spec for _, spec in inputs],
        out_specs=[row_spec(NSA_Q_W), row_spec(MOBA_W), row_spec(MEM_W)],
        out_shape=[jax.ShapeDtypeStruct((w, b * s), BF16) for w in (NSA_Q_W, MOBA_W, MEM_W)],
        scratch_shapes=nsa_scratch + moba_scratch + shared_scratch,
        compiler_params=_params(("arbitrary", "arbitrary")),
        name="attention",
    )(*[a for a, _ in inputs])


def _mix_kernel(x_ref, on_ref, om_ref, ox_ref, g_pre_ref, g_post_ref, wg_ref, wn_ref, wm_ref, wx_ref,
                wo_ref, o_ref):
    x = x_ref[...]
    h = _rms(x, g_pre_ref[...]).astype(BF16)
    rows = lambda o_ref: o_ref[...].astype(F32).T.astype(BF16)
    merged = jax.nn.sigmoid(_dot(h, wg_ref[:, :D_MODEL])) * _dot(rows(on_ref), wn_ref[...])
    merged = merged + jax.nn.sigmoid(_dot(h, wg_ref[:, D_MODEL:2 * D_MODEL])) * _dot(rows(om_ref), wm_ref[...])
    merged = merged + jax.nn.sigmoid(_dot(h, wg_ref[:, 2 * D_MODEL:])) * _dot(rows(ox_ref), wx_ref[...])
    y = _dot(merged.astype(BF16), wo_ref[...])
    o_ref[...] = x + _rms(y, g_post_ref[...])


def _mix(x2, o_nsa, o_moba, o_mem, g_pre, g_post, w_gates, w_nsa_o, w_moba_o, w_mem_o, w_mix_out, tm=512):
    m = x2.shape[0]
    row = lambda w: pl.BlockSpec((tm, w), lambda i: (i, 0))
    col = lambda w: pl.BlockSpec((w, tm), lambda i: (0, i))
    return pl.pallas_call(
        _mix_kernel,
        grid=(m // tm,),
        in_specs=[row(D_MODEL), col(NSA_Q_W), col(MOBA_W), col(MEM_W),
                  _const_spec((1, D_MODEL)), _const_spec((1, D_MODEL)),
                  _const_spec(w_gates.shape), _const_spec(w_nsa_o.shape), _const_spec(w_moba_o.shape),
                  _const_spec(w_mem_o.shape), _const_spec(w_mix_out.shape)],
        out_specs=row(D_MODEL),
        out_shape=jax.ShapeDtypeStruct((m, D_MODEL), F32),
        compiler_params=_params(("parallel",)),
        name="mix",
    )(x2, o_nsa, o_moba, o_mem, g_pre, g_post, w_gates, w_nsa_o, w_moba_o, w_mem_o, w_mix_out)


FFN_CHUNK = 256


def _ffn_kernel(x_ref, g_pre_ref, g_post_ref, wg_ref, wu_ref, wd_ref, o_ref, a_sc):
    x = x_ref[...]
    h = _rms(x, g_pre_ref[...]).astype(BF16)
    d_ff = wg_ref.shape[1]
    for j in range(d_ff // FFN_CHUNK):
        sl = slice(j * FFN_CHUNK, (j + 1) * FFN_CHUNK)
        a_sc[:, sl] = (jax.nn.silu(_dot(h, wg_ref[:, sl])) * _dot(h, wu_ref[:, sl])).astype(BF16)
    f = _dot(a_sc[...], wd_ref[...])
    o_ref[...] = x + _rms(f, g_post_ref[...])


def _ffn(x2, g_pre, g_post, wg, wu, wd, tm=512):
    m = x2.shape[0]
    d_ff = wg.shape[1]
    return pl.pallas_call(
        _ffn_kernel,
        grid=(m // tm,),
        in_specs=[pl.BlockSpec((tm, D_MODEL), lambda i: (i, 0)),
                  _const_spec((1, D_MODEL)), _const_spec((1, D_MODEL)),
                  _const_spec(wg.shape), _const_spec(wu.shape), _const_spec(wd.shape)],
        out_specs=pl.BlockSpec((tm, D_MODEL), lambda i: (i, 0)),
        out_shape=jax.ShapeDtypeStruct((m, D_MODEL), F32),
        scratch_shapes=[pltpu.VMEM((tm, d_ff), BF16)],
        compiler_params=_params(("parallel",)),
        name="ffn",
    )(x2, g_pre, g_post, wg, wu, wd)


def kernel(x, mem, rel_bias, pre_mix_g, mem_norm_g, post_mix_g, w_in, cmp_pos_k, cmp_w1_k, cmp_w2_k, cmp_pos_v, cmp_w1_v, cmp_w2_v, w_mem_kv, w_nsa_o, w_moba_o, w_mem_o, w_mix_out, pre_ffn_g, post_ffn_g, w_ffn_gate, w_ffn_up, w_ffn_down):
    b, s, d_model = x.shape
    depth = w_in.shape[0]
    assert d_model == D_MODEL and s % TILE == 0 and TILE == MOBA_BLOCK == WINDOW
    assert (s - CMP_LEN) // CMP_STRIDE + 1 < N_CMP_PAD and (s // SEL_BLOCK) % SUBLANES == 0 and s // SEL_BLOCK <= HEAD_DIM
    assert w_in.shape[2] == ATT_W + 3 * D_MODEL and rel_bias.shape == (REL_BUCKETS, N_BIAS_HEADS)

    tile_idx, win_idx, cmp_idx = _bucket_tables(s)
    rel_bias = rel_bias.astype(F32)
    t_nsa = _expand(tile_idx, rel_bias, 0, NSA_HEADS, NSA_HPG).reshape(NSA_GROUPS, 2, TILE, NSA_HPG * TILE)
    t_moba = _expand(tile_idx, rel_bias, NSA_HEADS, MOBA_HEADS, MOBA_HEADS).reshape(2, TILE, MOBA_HEADS * TILE)
    t_win = _expand(win_idx, rel_bias, 0, NSA_HEADS, NSA_HPG)
    b_cmp = _expand(cmp_idx, rel_bias, 0, NSA_HEADS)
    c_far = jnp.repeat(rel_bias[REL_BUCKETS - 1] * LOG2E, TILE)
    c_far_nsa = c_far[:NSA_HEADS * TILE].reshape(NSA_GROUPS, 1, NSA_HPG * TILE)
    c_far_moba = c_far[NSA_HEADS * TILE:].reshape(1, MOBA_HEADS * TILE)
    ovt = _overlap_table(s)
    sel_cols = np.zeros((s, HEAD_DIM), np.float32)
    sel_cols[np.arange(s), np.arange(s) // SEL_BLOCK] = 1.0
    sel_cols = jnp.asarray(sel_cols, BF16)
    gate_lo = NSA_Q_W + 6 * NSA_KV_W
    rows_per_chunk = CMP_STRIDE * NSA_KV_W

    x2 = x.reshape(b * s, D_MODEL)
    mem2 = mem.reshape(-1, D_MODEL)
    for l in range(depth):
        w_att = jnp.concatenate(
            [w_in[l, :, :gate_lo + NSA_GATE_W],
             jnp.zeros((D_MODEL, GATE_PAD - NSA_GATE_W), w_in.dtype),
             w_in[l, :, gate_lo + NSA_GATE_W:ATT_W]], axis=1).astype(BF16)
        w_gates = w_in[l, :, ATT_W:].astype(BF16)
        row = lambda v: v[l].reshape(1, D_MODEL)

        qn, kc_raw, vc_raw, ks, vs, kw, vw, gn, qm, km, vm, qx = _inproj(x2, row(pre_mix_g), w_att, sel_cols)

        pk, w1k = _compress_weights(cmp_pos_k[l], cmp_w1_k[l])
        pv, w1v = _compress_weights(cmp_pos_v[l], cmp_w1_v[l])
        kc, vct = _compress(kc_raw.reshape(b, s // CMP_STRIDE, rows_per_chunk),
                            vc_raw.reshape(b, s // CMP_STRIDE, rows_per_chunk),
                            pk, pv, w1k, w1v, cmp_w2_k[l].astype(BF16), cmp_w2_v[l].astype(BF16))

        mk, mv = _memkv(mem2, row(mem_norm_g), w_mem_kv[l].astype(BF16))

        o_nsa, o_moba, o_mem = _attention(b, s, qn, gn, kc, vct, ks, vs, kw, vw, b_cmp, t_nsa, t_win, c_far_nsa, ovt,
                                          qm, km, vm, qx, mk, mv, t_moba, c_far_moba)

        x2 = _mix(x2, o_nsa, o_moba, o_mem, row(pre_mix_g), row(post_mix_g), w_gates,
                  w_nsa_o[l].astype(BF16), w_moba_o[l].astype(BF16), w_mem_o[l].astype(BF16),
                  w_mix_out[l].astype(BF16))
        x2 = _ffn(x2, row(pre_ffn_g), row(post_ffn_g), w_ffn_gate[l].astype(BF16),
                  w_ffn_up[l].astype(BF16), w_ffn_down[l].astype(BF16))
    return x2.reshape(b, s, D_MODEL)
```

```python
import functools
import math

import numpy as np
import jax
import jax.numpy as jnp
from jax import lax
from jax.experimental import pallas as pl
from jax.experimental.pallas import tpu as pltpu

F32 = jnp.float32
BF16 = jnp.bfloat16

D_MODEL = 1024
HEAD_DIM = 64
SCALE = HEAD_DIM ** -0.5
LOG2E = math.log2(math.e)
Q_SCALE = SCALE * LOG2E
NSA_HEADS = 8
NSA_GROUPS = 2
NSA_HPG = NSA_HEADS // NSA_GROUPS
CMP_LEN = 32
CMP_STRIDE = 16
CMP_HIDDEN = 128
SEL_BLOCK = 64
SEL_TOPN = 8
WINDOW = 256
MOBA_HEADS = 4
MOBA_BLOCK = 256
MOBA_TOPK = 3
MEM_HEADS = 4
REL_BUCKETS = 32
REL_MAX_DIST = 128
N_BIAS_HEADS = NSA_HEADS + MOBA_HEADS
RMS_EPS = 1e-6
NEG_INF = -1e30
FORCE_SCORE = 1e4

NSA_Q_W = NSA_HEADS * HEAD_DIM
NSA_KV_W = NSA_GROUPS * HEAD_DIM
NSA_GATE_W = NSA_HEADS * 3
MOBA_W = MOBA_HEADS * HEAD_DIM
MEM_W = MEM_HEADS * HEAD_DIM
ATT_W = NSA_Q_W + 6 * NSA_KV_W + NSA_GATE_W + 3 * MOBA_W + MEM_W
LANES = 128
SUBLANES = 8
BF16_ROWS = 16
MXU_COLS = 256
GATE_PAD = LANES
TILE = 256
HALF = TILE // 2
RUN_AHEAD = 2
N_CMP_PAD = 128
V_AUG = HEAD_DIM + BF16_ROWS
MASKED_BUCKET = REL_BUCKETS
VMEM_LIMIT = 56 * 1024 * 1024


def _dot(a, b):
    return jnp.dot(a, b, preferred_element_type=F32)


def _split_bf16(x):
    hi = x.astype(BF16)
    lo = (x - hi.astype(F32)).astype(BF16)
    return hi, lo


def _rms(x, g):
    return x * lax.rsqrt(jnp.mean(x * x, axis=-1, keepdims=True) + RMS_EPS) * g


def _params(sem):
    return pltpu.CompilerParams(dimension_semantics=sem, vmem_limit_bytes=VMEM_LIMIT)


def _const_spec(shape):
    nd = len(shape)
    return pl.BlockSpec(shape, lambda *_: (0,) * nd, pipeline_mode=pl.Buffered(1))


_INPROJ_OUTS = (
    ("qn", NSA_Q_W, BF16, True),
    ("kc", NSA_KV_W, F32, False), ("vc", NSA_KV_W, F32, False),
    ("ks", NSA_KV_W, BF16, False), ("vs", NSA_KV_W, BF16, False),
    ("kw", NSA_KV_W, BF16, False), ("vw", NSA_KV_W, BF16, False),
    ("gn", GATE_PAD, F32, False),
    ("qm", MOBA_W, BF16, True), ("km", MOBA_W, BF16, False), ("vm", MOBA_W, BF16, False),
    ("qx", MEM_W, BF16, True),
)
_INPROJ_W = sum(o[1] for o in _INPROJ_OUTS)
_INPROJ_CHUNKED = ("kc", "vc")
_INPROJ_TRANSPOSED = ("qn", "gn", "qm", "qx")
KS_AUG_W = NSA_GROUPS * 2 * HEAD_DIM


def _inproj_out_width(name, width):
    return KS_AUG_W if name == "ks" else width


def _inproj_kernel(x_ref, g_ref, w_ref, e_ref, *refs):
    out_refs, rows_sc = refs[:-1], refs[-1]
    h = _rms(x_ref[...], g_ref[...]).astype(BF16)
    runs, lo = [], 0
    for out in zip(_INPROJ_OUTS, out_refs):
        if runs and runs[-1][1] < MXU_COLS:
            runs[-1][0].append(out)
            runs[-1][1] += out[0][1]
        else:
            runs.append([[out], out[0][1], lo])
        lo += out[0][1]
    for outs, run_width, run_lo in runs:
        y_run = _dot(h, w_ref[:, run_lo:run_lo + run_width])
        lo = 0
        for (name, width, dtype, scaled), o_ref in outs:
            y = y_run[:, lo:lo + width]
            if scaled:
                y = y * Q_SCALE
            if name == "gn":
                y = jax.nn.sigmoid(y)
            if name in _INPROJ_TRANSPOSED:
                y = y.T
            y = y.astype(dtype)
            if name == "ks":
                e = e_ref[...]
                y = _lane_cat([y[:, :HEAD_DIM], e, y[:, HEAD_DIM:], e])
            if name in _INPROJ_CHUNKED:
                rows_sc[...] = y
                for j in range(CMP_STRIDE):
                    o_ref[:, j * width:(j + 1) * width] = rows_sc[pl.ds(j, o_ref.shape[0], stride=CMP_STRIDE), :]
            else:
                o_ref[...] = y
            lo += width


def _inproj(x2, g, w, e_cols, tm=512):
    m = x2.shape[0]
    tiles_per_seq = e_cols.shape[0] // tm
    out_specs, out_shape = [], []
    for name, width, dtype, _ in _INPROJ_OUTS:
        if name in _INPROJ_TRANSPOSED:
            out_specs.append(pl.BlockSpec((width, tm), lambda i: (0, i)))
            out_shape.append(jax.ShapeDtypeStruct((width, m), dtype))
            continue
        rows, width = (CMP_STRIDE, CMP_STRIDE * width) if name in _INPROJ_CHUNKED else (1, _inproj_out_width(name, width))
        out_specs.append(pl.BlockSpec((tm // rows, width), lambda i: (i, 0)))
        out_shape.append(jax.ShapeDtypeStruct((m // rows, width), dtype))
    return pl.pallas_call(
        _inproj_kernel,
        grid=(m // tm,),
        in_specs=[pl.BlockSpec((tm, D_MODEL), lambda i: (i, 0)),
                  _const_spec((1, D_MODEL)),
                  _const_spec((D_MODEL, _INPROJ_W)),
                  pl.BlockSpec((tm, HEAD_DIM), lambda i: (i % tiles_per_seq, 0))],
        out_specs=out_specs,
        out_shape=out_shape,
        scratch_shapes=[pltpu.VMEM((tm, NSA_KV_W), F32)],
        compiler_params=_params(("parallel",)),
        name="inproj",
    )(x2, g, w, e_cols)


def _compress_kernel(rk_ref, rv_ref, pk_ref, pv_ref, w1k_ref, w1v_ref, w2k_ref, w2v_ref, kc_ref, vc_ref):
    nb = rk_ref.shape[0]

    def one(r_ref, p_ref, w1_ref, w2_ref):
        r = r_ref[...].reshape(nb * N_CMP_PAD, r_ref.shape[2])
        top = _dot((r + p_ref[0:1, :]).astype(BF16), w1_ref[0])
        bot = _dot((r + p_ref[1:2, :]).astype(BF16), w1_ref[1])
        hid = top + pltpu.roll(bot, nb * N_CMP_PAD - 1, 0)
        act = jax.nn.gelu(hid).astype(BF16)
        return jnp.concatenate(
            [_dot(act[:, g * CMP_HIDDEN:(g + 1) * CMP_HIDDEN], w2_ref[...]) for g in range(NSA_GROUPS)], axis=1)

    k_out = one(rk_ref, pk_ref, w1k_ref, w2k_ref)
    v_out = one(rv_ref, pv_ref, w1v_ref, w2v_ref)
    for n in range(nb):
        rows = slice(n * N_CMP_PAD, (n + 1) * N_CMP_PAD)
        kc_ref[n] = k_out[rows].astype(BF16)
        vc_ref[n] = v_out[rows].T.astype(BF16)


def _compress(rk, rv, pk, pv, w1k, w1v, w2k, w2v):
    b = rk.shape[0]
    rw = rk.shape[2]
    nb = 2 if b % 2 == 0 else 1
    r_spec = pl.BlockSpec((nb, N_CMP_PAD, rw), lambda i: (i, 0, 0))
    o_spec = pl.BlockSpec((nb, N_CMP_PAD, NSA_KV_W), lambda i: (i, 0, 0))
    return pl.pallas_call(
        _compress_kernel,
        grid=(b // nb,),
        in_specs=[r_spec, r_spec, _const_spec(pk.shape), _const_spec(pv.shape),
                  _const_spec(w1k.shape), _const_spec(w1v.shape),
                  _const_spec(w2k.shape), _const_spec(w2v.shape)],
        out_specs=[o_spec, o_spec],
        out_shape=[jax.ShapeDtypeStruct((b, N_CMP_PAD, NSA_KV_W), BF16)] * 2,
        compiler_params=_params(("parallel",)),
        name="compress",
    )(rk, rv, pk, pv, w1k, w1v, w2k, w2v)


def _compress_weights(pos, w1):
    half = CMP_LEN // 2
    p = pos.reshape(2, half, 1, HEAD_DIM)
    p = jnp.broadcast_to(p, (2, half, NSA_GROUPS, HEAD_DIM)).reshape(2, half * NSA_KV_W)
    w = w1.reshape(2, half, HEAD_DIM, CMP_HIDDEN)
    eye = jnp.eye(NSA_GROUPS, dtype=w1.dtype)
    wbd = jnp.einsum("ajdm,gk->ajgdkm", w, eye).reshape(2, half * NSA_KV_W, NSA_GROUPS * CMP_HIDDEN)
    return p.astype(F32), wbd.astype(BF16)


def _memkv_kernel(m_ref, g_ref, w_ref, k_ref, v_ref):
    h = _rms(m_ref[...], g_ref[...]).astype(BF16)
    k_ref[...] = _dot(h, w_ref[:, :MEM_W]).astype(BF16)
    v_ref[...] = _dot(h, w_ref[:, MEM_W:]).astype(BF16)


def _memkv(mem2, g, w, tm=512):
    m = mem2.shape[0]
    tm = min(tm, m)
    o_spec = pl.BlockSpec((tm, MEM_W), lambda i: (i, 0))
    return pl.pallas_call(
        _memkv_kernel,
        grid=(m // tm,),
        in_specs=[pl.BlockSpec((tm, D_MODEL), lambda i: (i, 0)), _const_spec((1, D_MODEL)),
                  _const_spec((D_MODEL, 2 * MEM_W))],
        out_specs=[o_spec, o_spec],
        out_shape=[jax.ShapeDtypeStruct((m, MEM_W), BF16)] * 2,
        compiler_params=_params(("parallel",)),
        name="memkv",
    )(mem2, g, w)


def _expand_kernel(idx_ref, bias_ref, o_ref, *, head0, n_heads, heads_per_group):
    rows, cols = idx_ref.shape

    def body(i, carry):
        r = pl.multiple_of(i * SUBLANES, SUBLANES)
        for c0 in range(0, cols, TILE):
            idx = idx_ref[pl.ds(r, SUBLANES), c0:c0 + TILE]
            out = [jnp.full(idx.shape, NEG_INF, F32)] * n_heads
            for bkt in range(REL_BUCKETS):
                hit = idx == bkt
                out = [jnp.where(hit, bias_ref[bkt, head0 + h], out[h]) for h in range(n_heads)]
            for h in range(n_heads):
                col = (h % heads_per_group) * cols + c0
                o_ref[h // heads_per_group, pl.ds(r, SUBLANES), col:col + TILE] = out[h] * LOG2E
        return carry

    lax.fori_loop(0, rows // SUBLANES, body, 0)


def _expand(idx, rel_bias, head0, n_heads, heads_per_group=1):
    rows, cols = idx.shape
    return pl.pallas_call(
        functools.partial(_expand_kernel, head0=head0, n_heads=n_heads, heads_per_group=heads_per_group),
        in_specs=[pl.BlockSpec(memory_space=pltpu.VMEM), pl.BlockSpec(memory_space=pltpu.SMEM)],
        out_specs=pl.BlockSpec(memory_space=pltpu.VMEM),
        out_shape=jax.ShapeDtypeStruct((n_heads // heads_per_group, rows, heads_per_group * cols), F32),
        compiler_params=pltpu.CompilerParams(vmem_limit_bytes=VMEM_LIMIT),
        name="bias_expand",
    )(idx, rel_bias)


def _t5_bucket_np(dist):
    dist = np.maximum(dist, 0)
    max_exact = REL_BUCKETS // 2
    logd = np.log(np.maximum(dist, 1).astype(np.float32) / max_exact) / math.log(REL_MAX_DIST / max_exact)
    large = np.minimum(max_exact + (logd * (REL_BUCKETS - max_exact)).astype(np.int32), REL_BUCKETS - 1)
    return np.where(dist < max_exact, dist, large).astype(np.int32)


def _bucket_tables(s):
    j = np.arange(TILE)[:, None]
    i = np.arange(TILE)[None, :]
    assert TILE + 1 >= REL_MAX_DIST
    tiles = []
    for d in range(2):
        dist = d * TILE + i - j
        tiles.append(np.where(dist >= 0, _t5_bucket_np(dist), MASKED_BUCKET))
    dist1 = TILE + i - j
    win = np.where(dist1 < WINDOW, _t5_bucket_np(dist1), MASKED_BUCKET)
    n_cmp = (s - CMP_LEN) // CMP_STRIDE + 1
    assert n_cmp * CMP_STRIDE + CMP_LEN - 1 >= s or n_cmp == N_CMP_PAD
    rel = np.arange(N_CMP_PAD + (s - TILE) // CMP_STRIDE)[:, None] - (s - TILE) // CMP_STRIDE
    dist_c = i - (rel * CMP_STRIDE + CMP_LEN - 1)
    cmp_idx = np.where(dist_c >= 0, _t5_bucket_np(dist_c), MASKED_BUCKET)
    as_i32 = lambda a: jnp.asarray(a.astype(np.int32))
    return as_i32(np.concatenate(tiles, axis=0)), as_i32(win), as_i32(cmp_idx)


def _overlap_table(s):
    n_cmp = (s - CMP_LEN) // CMP_STRIDE + 1
    n_sel = s // SEL_BLOCK
    cs = np.arange(n_cmp) * CMP_STRIDE
    ss = np.arange(n_sel) * SEL_BLOCK
    ov = np.clip(np.minimum(cs[:, None] + CMP_LEN, ss[None, :] + SEL_BLOCK)
                 - np.maximum(cs[:, None], ss[None, :]), 0, None).astype(np.float32) / CMP_LEN
    ovt = np.zeros((n_sel, N_CMP_PAD), np.float32)
    ovt[:, :n_cmp] = ov.T
    return jnp.asarray(ovt, BF16)


def _store_v_aug(vt_sc, idx, vt):
    ones = jnp.ones((BF16_ROWS, vt.shape[1]), BF16)
    vt_sc[idx] = jnp.concatenate([vt.astype(BF16), ones], axis=0)


def _lane_cat(xs):
    return jnp.concatenate(xs, axis=1)


def _query_halves(x):
    n = x.shape[-1] // TILE
    first = _lane_cat([x[:, k * TILE:k * TILE + HALF] for k in range(n)])
    second = _lane_cat([x[:, k * TILE + HALF:(k + 1) * TILE] for k in range(n)])
    return first, second


def _join_query_halves(first, second):
    n = first.shape[-1] // HALF
    return _lane_cat([part for k in range(n)
                      for part in (first[:, k * HALF:(k + 1) * HALF], second[:, k * HALF:(k + 1) * HALF])])


def _triangle_scores(k, q, table, causal):
    q_first, q_second = _query_halves(q)
    lo, hi = (0, HALF), (HALF, TILE)
    if causal:
        return _dot(k(*lo), q) + table(*lo), _dot(k(*hi), q_second) + _query_halves(table(*hi))[1]
    return _dot(k(*hi), q) + table(*hi), _dot(k(*lo), q_first) + _query_halves(table(*lo))[0]


def _triangle_softmax(scores, pv_lo, pv_hi, causal):
    s_wide, s_narrow = scores
    pv_wide, pv_narrow = (pv_lo, pv_hi) if causal else (pv_hi, pv_lo)
    mw_first, mw_second = _query_halves(jnp.max(s_wide, axis=0, keepdims=True))
    m_narrow = jnp.max(s_narrow, axis=0, keepdims=True)
    if causal:
        m_narrow = jnp.maximum(m_narrow, mw_second)
        m = _join_query_halves(mw_first, m_narrow)
    else:
        m_narrow = jnp.maximum(m_narrow, mw_first)
        m = _join_query_halves(m_narrow, mw_second)
    aw_first, aw_second = _query_halves(pv_wide(jnp.exp2(s_wide - m).astype(BF16)))
    a_narrow = pv_narrow(jnp.exp2(s_narrow - m_narrow).astype(BF16))
    if causal:
        return m, _join_query_halves(aw_first, aw_second + a_narrow)
    return m, _join_query_halves(aw_first + a_narrow, aw_second)


def _flash_pipelined(own, streams, s_sc, m_ref, acc_ref):
    has_prev = jnp.where(own > 0, 1.0, 0.0).astype(F32)
    prev = jnp.maximum(own - 1, 0)
    n_far = jnp.maximum(own - 1, 0)

    def absorb(g, s, kt, c_row, w_row):
        u = jnp.max(s, axis=0, keepdims=True) + c_row
        m_old = m_ref[g]
        m_new = jnp.maximum(m_old, jnp.where(w_row > 0.0, u, NEG_INF))
        alpha = jnp.exp2(m_old - m_new)
        shift = jnp.maximum(m_new, u) - c_row
        p = jnp.exp2(s - shift).astype(BF16)
        acc_ref[g] = alpha * acc_ref[g] + w_row * streams[g]["pv"](kt)(p)
        m_ref[g] = m_new

    def absorb_slot(g, i):
        is_prev = i == 0
        kt = jnp.where(is_prev, prev, i - 1)
        c_row = jnp.where(is_prev, 0.0, streams[g]["c_far"])
        w_row = streams[g]["w"](kt) * jnp.where(is_prev, has_prev, 1.0)
        absorb(g, s_sc[g], kt, c_row, w_row)

    def body(i, carry):
        for g in reversed(range(len(streams))):
            nxt = streams[g]["far"](i)
            absorb_slot(g, i)
            s_sc[g] = nxt
        return carry

    lax.fori_loop(0, n_far, body, 0)
    for g in range(len(streams)):
        absorb_slot(g, n_far)


def _run_ahead(jobs):
    depth = RUN_AHEAD
    pending = [job[0]() for job in jobs[:depth]]
    for k, (_, consume) in enumerate(jobs):
        if k + depth < len(jobs):
            pending.append(jobs[k + depth][0]())
        consume(pending.pop(0))


def _flash_start_jobs(g, stream, s_sc, m_ref, acc_ref):
    def init(scores):
        m_ref[g], acc_ref[g] = stream["own_softmax"](scores)

    def park(scores):
        s_sc[g] = scores

    return [(stream["own_scores"], init), (stream["prev"], park)]


def _softmax_av(s_list, pv_list):
    m = s_list[0].max(axis=0, keepdims=True)
    for s in s_list[1:]:
        m = jnp.maximum(m, s.max(axis=0, keepdims=True))
    acc = None
    for s, pv in zip(s_list, pv_list):
        part = pv(jnp.exp2(s - m).astype(BF16))
        acc = part if acc is None else acc + part
    return acc


def _normalize(acc):
    return acc[:HEAD_DIM] * (1.0 / acc[HEAD_DIM:HEAD_DIM + 1])


def _rank_before(score, n_cand):
    ranks = []
    for r0 in range(0, score.shape[0], SUBLANES):
        tile = score[r0:r0 + SUBLANES]
        blk = lax.broadcasted_iota(jnp.int32, tile.shape, 0) + r0
        rank = jnp.zeros(tile.shape, F32)
        for m in range(n_cand):
            row = score[m:m + 1, :]
            if m < r0:
                before = jnp.where(row >= tile, 1.0, 0.0)
            elif m >= r0 + SUBLANES:
                before = jnp.where(row > tile, 1.0, 0.0)
            else:
                before = jnp.where(blk > m, jnp.where(row >= tile, 1.0, 0.0), jnp.where(row > tile, 1.0, 0.0))
            rank = rank + before
        ranks.append(rank)
    return jnp.concatenate(ranks, axis=0)


def _nsa_steps(qi, q_ref, gn_ref, kc_ref, vct_ref, ks_ref, vs_ref, kw_ref, vw_ref,
               bct_ref, tt_ref, twt_ref, cfar_ref, ovt_ref, o_ref,
               vst_sc, vwt_sc, qa_sc, og_sc, acc_sc):
    nt = ks_ref.shape[1]
    n_sel = ovt_ref.shape[0]

    @pl.when(qi == 0)
    def _():
        for kt in range(nt):
            vs_t = vs_ref[0, kt].astype(F32).T
            vw_t = vw_ref[0, kt].astype(F32).T
            for g in range(NSA_GROUPS):
                _store_v_aug(vst_sc, (kt, g), vs_t[g * HEAD_DIM:(g + 1) * HEAD_DIM])
                _store_v_aug(vwt_sc, (kt, g), vw_t[g * HEAD_DIM:(g + 1) * HEAD_DIM])

    pos = lax.broadcasted_iota(jnp.int32, (1, TILE), 1) + qi * TILE
    cur = pos // SEL_BLOCK
    has_cmp = pos >= CMP_LEN - 1
    blk = lax.broadcasted_iota(jnp.int32, (n_sel, TILE), 0)
    prev = jnp.maximum(qi - 1, 0)
    gates = gn_ref[...]

    gsls = [slice(g * HEAD_DIM, (g + 1) * HEAD_DIM) for g in range(NSA_GROUPS)]
    group_heads = [[g * NSA_HPG + j for j in range(NSA_HPG)] for g in range(NSA_GROUPS)]

    def gate(g, branch):
        return _lane_cat([gates[3 * h + branch:3 * h + branch + 1, :] for h in group_heads[g]])

    ones_row = jnp.ones((1, NSA_HPG * TILE), F32)

    def sel_stream(g):
        def qk(kt):
            return _dot(ks_ref[0, kt, :, g * 2 * HEAD_DIM:(g + 1) * 2 * HEAD_DIM], qa_sc[g])

        def own_scores():
            return _triangle_scores(
                lambda lo, hi: ks_ref[0, qi, lo:hi, g * 2 * HEAD_DIM:(g + 1) * 2 * HEAD_DIM], qa_sc[g],
                lambda lo, hi: tt_ref[g, 0, lo:hi, :], causal=True)

        def own_softmax(scores):
            return _triangle_softmax(
                scores, lambda pr: _dot(vst_sc[qi, g, :, 0:HALF], pr),
                lambda pr: _dot(vst_sc[qi, g, :, HALF:TILE], pr), causal=True)

        return dict(own_scores=own_scores, own_softmax=own_softmax, prev=lambda: qk(prev) + tt_ref[g, 1], far=qk,
                    c_far=cfar_ref[g], w=lambda kt: ones_row,
                    pv=lambda kt: (lambda pr: _dot(vst_sc[kt, g], pr)))

    def group_jobs(g):
        heads = group_heads[g]
        win = {}

        def v_half(kt, lo, hi):
            return lambda pr: _dot(vwt_sc[kt, g, :, lo:hi], pr)

        def window_own_scores():
            q4 = _lane_cat([q_ref[h * HEAD_DIM:(h + 1) * HEAD_DIM, :] for h in heads])
            qa_sc[g, 0:HEAD_DIM, :] = q4
            qa_sc[g, HEAD_DIM + n_sel:, :] = jnp.zeros((HEAD_DIM - n_sel, NSA_HPG * TILE), BF16)
            return _triangle_scores(lambda lo, hi: kw_ref[0, qi, lo:hi, gsls[g]], q4,
                                    lambda lo, hi: tt_ref[g, 0, lo:hi, :], causal=True)

        def window_own(scores):
            win["own"] = _triangle_softmax(scores, v_half(qi, 0, HALF), v_half(qi, HALF, TILE), causal=True)

        def window_prev_scores():
            return _triangle_scores(lambda lo, hi: kw_ref[0, prev, lo:hi, gsls[g]], qa_sc[g, 0:HEAD_DIM, :],
                                    lambda lo, hi: twt_ref[g, lo:hi, :], causal=False)

        def window_prev(scores):
            m_prev, acc_prev = _triangle_softmax(scores, v_half(prev, 0, HALF), v_half(prev, HALF, TILE), causal=False)
            m_prev = jnp.where(qi == 0, NEG_INF, m_prev)
            m_own, acc_own = win["own"]
            m_win = jnp.maximum(m_own, m_prev)
            acc_w = acc_own * jnp.exp2(m_own - m_win) + acc_prev * jnp.exp2(m_prev - m_win)
            og_sc[g] = gate(g, 2) * _normalize(acc_w)

        def compressed_scores():
            rows_per_tile = TILE // CMP_STRIDE
            c_rows = pl.ds(pl.multiple_of((nt - 1 - qi) * rows_per_tile, rows_per_tile), N_CMP_PAD)
            return (_dot(kc_ref[0, :, gsls[g]], qa_sc[g, 0:HEAD_DIM, :])
                    + _lane_cat([bct_ref[h, c_rows, :] for h in heads]))

        def compressed(s):
            e = jnp.exp2(s - jnp.max(s, axis=0, keepdims=True))
            p = e * jnp.where(_lane_cat([has_cmp] * NSA_HPG), 1.0 / jnp.sum(e, axis=0, keepdims=True), 0.0)
            psum = p[:, :TILE]
            for j in range(1, NSA_HPG):
                psum = psum + p[:, j * TILE:(j + 1) * TILE]
            og_sc[g] = og_sc[g] + gate(g, 0) * _dot(vct_ref[0, gsls[g], :], p.astype(BF16))

            p_hi, p_lo = _split_bf16(psum)
            imp = _dot(ovt_ref[...], p_hi) + _dot(ovt_ref[...], p_lo)
            forced = (blk == 0) | (blk == cur) | (blk == cur - 1)
            score = jnp.where(forced, FORCE_SCORE, jnp.where(blk <= cur, imp, NEG_INF))
            rank = _rank_before(score, n_sel)
            sel = jnp.where(rank < SEL_TOPN, jnp.where(score > NEG_INF / 2, 0.0, NEG_INF), NEG_INF)
            qa_sc[g, HEAD_DIM:HEAD_DIM + n_sel, :] = _lane_cat([sel.astype(BF16)] * NSA_HPG)

        return [(window_own_scores, window_own), (window_prev_scores, window_prev), (compressed_scores, compressed)]

    yield [job for g in range(NSA_GROUPS) for job in group_jobs(g)]
    yield [sel_stream(g) for g in range(NSA_GROUPS)]

    for g in range(NSA_GROUPS):
        o = og_sc[g] + gate(g, 1) * _normalize(acc_sc[g])
        for j, h in enumerate(group_heads[g]):
            o_ref[h * HEAD_DIM:(h + 1) * HEAD_DIM, :] = o[:, j * TILE:(j + 1) * TILE].astype(BF16)


def _moba_steps(c, qm_ref, km_ref, vm_ref, qx_ref, mk_ref, mv_ref, tt_ref, cfar_ref, om_ref, ox_ref,
                vmt_sc, mvt_sc, kmean_sc, qbd_sc, sel_sc, acc_sc, slot):
    nt = km_ref.shape[1]
    hsls = [slice(h * HEAD_DIM, (h + 1) * HEAD_DIM) for h in range(MOBA_HEADS)]

    @pl.when(c == 0)
    def _():
        kmean_sc[...] = jnp.zeros(kmean_sc.shape, F32)
        for n in range(nt):
            kmean_sc[n:n + 1, :] = jnp.sum(km_ref[0, n].astype(F32), axis=0, keepdims=True) * (1.0 / MOBA_BLOCK)
            vt = vm_ref[0, n].astype(F32).T
            for h in range(MOBA_HEADS):
                _store_v_aug(vmt_sc, (n, h), vt[hsls[h]])
        mvt = mv_ref[0].astype(F32).T
        for h in range(MOBA_HEADS):
            _store_v_aug(mvt_sc, h, mvt[hsls[h]])

    row_head = lax.broadcasted_iota(jnp.int32, (MOBA_W, TILE), 0) // HEAD_DIM

    def block_diag(q_ref):
        q_t = q_ref[...].astype(F32)
        return _lane_cat([jnp.where(row_head == h, q_t, 0.0) for h in range(MOBA_HEADS)]).astype(BF16)

    def per_head_pv(vts):
        return lambda pr: _lane_cat([_dot(vts(h), pr[:, h * TILE:(h + 1) * TILE]) for h in range(MOBA_HEADS)])

    def store_heads(o_t, out_ref):
        for h in range(MOBA_HEADS):
            out_ref[hsls[h], :] = o_t[:, h * TILE:(h + 1) * TILE].astype(BF16)

    n_rows = -(-nt // SUBLANES) * SUBLANES

    def gate_scores():
        qbd = block_diag(qm_ref)
        qbd_sc[...] = qbd
        km_hi, km_lo = _split_bf16(kmean_sc[...])
        return (_dot(km_hi, qbd) + _dot(km_lo, qbd))[:n_rows]

    def select(gate):
        blk = lax.broadcasted_iota(jnp.int32, gate.shape, 0)
        score = jnp.where(blk < c, gate, NEG_INF * Q_SCALE)
        rank = _rank_before(score, nt)
        sel_sc[0:n_rows, :] = jnp.where(rank < MOBA_TOPK, jnp.where(score > NEG_INF * Q_SCALE / 2, 1.0, 0.0), 0.0)

    qk = lambda n: _dot(km_ref[0, n], qbd_sc[...])
    def own_pv(lo, hi):
        def pv(pr):
            width = pr.shape[1] // MOBA_HEADS
            return _lane_cat([_dot(vmt_sc[c, h, :, lo:hi], pr[:, h * width:(h + 1) * width])
                              for h in range(MOBA_HEADS)])
        return pv

    def own_scores():
        return _triangle_scores(lambda lo, hi: km_ref[0, c, lo:hi, :], qbd_sc[...],
                                lambda lo, hi: tt_ref[0, lo:hi, :], causal=True)

    def own_softmax(scores):
        return _triangle_softmax(scores, own_pv(0, HALF), own_pv(HALF, TILE), causal=True)

    stream = dict(own_scores=own_scores, own_softmax=own_softmax,
                  prev=lambda: qk(jnp.maximum(c - 1, 0)) + tt_ref[1], far=qk,
                  c_far=cfar_ref[...], w=lambda n: sel_sc[pl.ds(n, 1), :],
                  pv=lambda n: per_head_pv(lambda h: vmt_sc[n, h]))
    def memory(s):
        store_heads(_normalize(_softmax_av([s], [per_head_pv(lambda h: mvt_sc[h])])), ox_ref)

    yield [(gate_scores, select), (lambda: _dot(mk_ref[0], block_diag(qx_ref)), memory)]
    yield [stream]
    store_heads(_normalize(acc_sc[slot]), om_ref)


N_STREAMS = NSA_GROUPS + 1


def _attention_kernel(*refs, n_nsa_in, n_moba_in, n_nsa_scratch, n_moba_scratch):
    qi = pl.program_id(1)
    nsa_in, refs = refs[:n_nsa_in], refs[n_nsa_in:]
    moba_in, refs = refs[:n_moba_in], refs[n_moba_in:]
    (o_nsa, o_moba, o_mem), refs = refs[:3], refs[3:]
    nsa_sc, refs = refs[:n_nsa_scratch], refs[n_nsa_scratch:]
    moba_sc, refs = refs[:n_moba_scratch], refs[n_moba_scratch:]
    s_sc, m_sc, acc_sc = refs
    nsa = _nsa_steps(qi, *nsa_in, o_nsa, *nsa_sc, acc_sc)
    moba = _moba_steps(qi, *moba_in, o_moba, o_mem, *moba_sc, acc_sc, NSA_GROUPS)
    jobs = next(nsa) + next(moba)
    streams = next(nsa) + next(moba)
    jobs += [job for g, stream in enumerate(streams) for job in _flash_start_jobs(g, stream, s_sc, m_sc, acc_sc)]
    _run_ahead(jobs)
    _flash_pipelined(qi, streams, s_sc, m_sc, acc_sc)
    for steps in (nsa, moba):
        for _ in steps:
            pass


def _attention(b, s, qn, gn, kc, vct, ks, vs, kw, vw, bias_cmp, t_nsa, t_win, c_far_nsa, ovt,
               qm, km, vm, qx, mk, mv, t_moba, c_far_moba):
    nt = s // TILE
    mem_len = mk.shape[0] // b
    assert MOBA_TOPK <= nt - 1 and nt <= BF16_ROWS
    n_lanes = NSA_HPG * TILE
    assert MOBA_HEADS * TILE == n_lanes
    row_spec = lambda w: pl.BlockSpec((w, TILE), lambda i, j: (0, i * nt + j))
    seq_spec = lambda w: pl.BlockSpec((1, nt, TILE, w), lambda i, j: (i, 0, 0, 0))
    per_batch = lambda rows, w: pl.BlockSpec((1, rows, w), lambda i, j: (i, 0, 0))
    tiles = lambda a: a.reshape(b, nt, TILE, a.shape[-1])
    nsa_in = [(qn, row_spec(NSA_Q_W)), (gn, row_spec(GATE_PAD)),
              (kc, per_batch(N_CMP_PAD, NSA_KV_W)), (vct, per_batch(N_CMP_PAD, NSA_KV_W)),
              (tiles(ks), seq_spec(KS_AUG_W)), (tiles(vs), seq_spec(NSA_KV_W)),
              (tiles(kw), seq_spec(NSA_KV_W)), (tiles(vw), seq_spec(NSA_KV_W)),
              (bias_cmp, _const_spec(bias_cmp.shape)), (t_nsa, _const_spec(t_nsa.shape)),
              (t_win, _const_spec(t_win.shape)), (c_far_nsa, _const_spec(c_far_nsa.shape)),
              (ovt, _const_spec(ovt.shape))]
    moba_in = [(qm, row_spec(MOBA_W)), (tiles(km), seq_spec(MOBA_W)), (tiles(vm), seq_spec(MOBA_W)),
               (qx, row_spec(MEM_W)),
               (mk.reshape(b, mem_len, MEM_W), per_batch(mem_len, MEM_W)),
               (mv.reshape(b, mem_len, MEM_W), per_batch(mem_len, MEM_W)),
               (t_moba, _const_spec(t_moba.shape)), (c_far_moba, _const_spec(c_far_moba.shape))]
    nsa_scratch =[pltpu.VMEM((nt, NSA_GROUPS, V_AUG, TILE), BF16),
                   pltpu.VMEM((nt, NSA_GROUPS, V_AUG, TILE), BF16),
                   pltpu.VMEM((NSA_GROUPS, 2 * HEAD_DIM, n_lanes), BF16),
                   pltpu.VMEM((NSA_GROUPS, HEAD_DIM, n_lanes), F32)]
    moba_scratch = [pltpu.VMEM((nt, MOBA_HEADS, V_AUG, TILE), BF16),
                    pltpu.VMEM((MEM_HEADS, V_AUG, mem_len), BF16),
                    pltpu.VMEM((BF16_ROWS, MOBA_W), F32),
                    pltpu.VMEM((MOBA_W, n_lanes), BF16),
                    pltpu.VMEM((BF16_ROWS, n_lanes), F32)]
    shared_scratch =[pltpu.VMEM((N_STREAMS, TILE, n_lanes), F32),
                      pltpu.VMEM((N_STREAMS, 1, n_lanes), F32),
                      pltpu.VMEM((N_STREAMS, V_AUG, n_lanes), F32)]
    inputs = nsa_in + moba_in
    return pl.pallas_call(
        functools.partial(_attention_kernel, n_nsa_in=len(nsa_in), n_moba_in=len(moba_in),
                          n_nsa_scratch=len(nsa_scratch), n_moba_scratch=len(moba_scratch)),
        grid=(b, nt),
        in_specs=[spec for _, spec in inputs],
        out_specs=[row_spec(NSA_Q_W), row_spec(MOBA_W), row_spec(MEM_W)],
        out_shape=[jax.ShapeDtypeStruct((w, b * s), BF16) for w in (NSA_Q_W, MOBA_W, MEM_W)],
        scratch_shapes=nsa_scratch + moba_scratch + shared_scratch,
        compiler_params=_params(("arbitrary", "arbitrary")),
        name="attention",
    )(*[a for a, _ in inputs])


def _mix_kernel(x_ref, on_ref, om_ref, ox_ref, g_pre_ref, g_post_ref, wg_ref, wn_ref, wm_ref, wx_ref,
                wo_ref, o_ref):
    x = x_ref[...]
    h = _rms(x, g_pre_ref[...]).astype(BF16)
    rows = lambda o_ref: o_ref[...].astype(F32).T.astype(BF16)
    merged = jax.nn.sigmoid(_dot(h, wg_ref[:, :D_MODEL])) * _dot(rows(on_ref), wn_ref[...])
    merged = merged + jax.nn.sigmoid(_dot(h, wg_ref[:, D_MODEL:2 * D_MODEL])) * _dot(rows(om_ref), wm_ref[...])
    merged = merged + jax.nn.sigmoid(_dot(h, wg_ref[:, 2 * D_MODEL:])) * _dot(rows(ox_ref), wx_ref[...])
    y = _dot(merged.astype(BF16), wo_ref[...])
    o_ref[...] = x + _rms(y, g_post_ref[...])


def _mix(x2, o_nsa, o_moba, o_mem, g_pre, g_post, w_gates, w_nsa_o, w_moba_o, w_mem_o, w_mix_out, tm=512):
    m = x2.shape[0]
    row = lambda w: pl.BlockSpec((tm, w), lambda i: (i, 0))
    col = lambda w: pl.BlockSpec((w, tm), lambda i: (0, i))
    return pl.pallas_call(
        _mix_kernel,
        grid=(m // tm,),
        in_specs=[row(D_MODEL), col(NSA_Q_W), col(MOBA_W), col(MEM_W),
                  _const_spec((1, D_MODEL)), _const_spec((1, D_MODEL)),
                  _const_spec(w_gates.shape), _const_spec(w_nsa_o.shape), _const_spec(w_moba_o.shape),
                  _const_spec(w_mem_o.shape), _const_spec(w_mix_out.shape)],
        out_specs=row(D_MODEL),
        out_shape=jax.ShapeDtypeStruct((m, D_MODEL), F32),
        compiler_params=_params(("parallel",)),
        name="mix",
    )(x2, o_nsa, o_moba, o_mem, g_pre, g_post, w_gates, w_nsa_o, w_moba_o, w_mem_o, w_mix_out)


FFN_CHUNK = 256


def _ffn_kernel(x_ref, g_pre_ref, g_post_ref, wg_ref, wu_ref, wd_ref, o_ref, a_sc):
    x = x_ref[...]
    h = _rms(x, g_pre_ref[...]).astype(BF16)
    d_ff = wg_ref.shape[1]
    for j in range(d_ff // FFN_CHUNK):
        sl = slice(j * FFN_CHUNK, (j + 1) * FFN_CHUNK)
        a_sc[:, sl] = (jax.nn.silu(_dot(h, wg_ref[:, sl])) * _dot(h, wu_ref[:, sl])).astype(BF16)
    f = _dot(a_sc[...], wd_ref[...])
    o_ref[...] = x + _rms(f, g_post_ref[...])


def _ffn(x2, g_pre, g_post, wg, wu, wd, tm=512):
    m = x2.shape[0]
    d_ff = wg.shape[1]
    return pl.pallas_call(
        _ffn_kernel,
        grid=(m // tm,),
        in_specs=[pl.BlockSpec((tm, D_MODEL), lambda i: (i, 0)),
                  _const_spec((1, D_MODEL)), _const_spec((1, D_MODEL)),
                  _const_spec(wg.shape), _const_spec(wu.shape), _const_spec(wd.shape)],
        out_specs=pl.BlockSpec((tm, D_MODEL), lambda i: (i, 0)),
        out_shape=jax.ShapeDtypeStruct((m, D_MODEL), F32),
        scratch_shapes=[pltpu.VMEM((tm, d_ff), BF16)],
        compiler_params=_params(("parallel",)),
        name="ffn",
    )(x2, g_pre, g_post, wg, wu, wd)


def kernel(x, mem, rel_bias, pre_mix_g, mem_norm_g, post_mix_g, w_in, cmp_pos_k, cmp_w1_k, cmp_w2_k, cmp_pos_v, cmp_w1_v, cmp_w2_v, w_mem_kv, w_nsa_o, w_moba_o, w_mem_o, w_mix_out, pre_ffn_g, post_ffn_g, w_ffn_gate, w_ffn_up, w_ffn_down):
    b, s, d_model = x.shape
    depth = w_in.shape[0]
    assert d_model == D_MODEL and s % TILE == 0 and TILE == MOBA_BLOCK == WINDOW
    assert (s - CMP_LEN) // CMP_STRIDE + 1 < N_CMP_PAD and (s // SEL_BLOCK) % SUBLANES == 0 and s // SEL_BLOCK <= HEAD_DIM
    assert w_in.shape[2] == ATT_W + 3 * D_MODEL and rel_bias.shape == (REL_BUCKETS, N_BIAS_HEADS)

    tile_idx, win_idx, cmp_idx = _bucket_tables(s)
    rel_bias = rel_bias.astype(F32)
    t_nsa = _expand(tile_idx, rel_bias, 0, NSA_HEADS, NSA_HPG).reshape(NSA_GROUPS, 2, TILE, NSA_HPG * TILE)
    t_moba = _expand(tile_idx, rel_bias, NSA_HEADS, MOBA_HEADS, MOBA_HEADS).reshape(2, TILE, MOBA_HEADS * TILE)
    t_win = _expand(win_idx, rel_bias, 0, NSA_HEADS, NSA_HPG)
    b_cmp = _expand(cmp_idx, rel_bias, 0, NSA_HEADS)
    c_far = jnp.repeat(rel_bias[REL_BUCKETS - 1] * LOG2E, TILE)
    c_far_nsa = c_far[:NSA_HEADS * TILE].reshape(NSA_GROUPS, 1, NSA_HPG * TILE)
    c_far_moba = c_far[NSA_HEADS * TILE:].reshape(1, MOBA_HEADS * TILE)
    ovt = _overlap_table(s)
    sel_cols = np.zeros((s, HEAD_DIM), np.float32)
    sel_cols[np.arange(s), np.arange(s) // SEL_BLOCK] = 1.0
    sel_cols = jnp.asarray(sel_cols, BF16)
    gate_lo = NSA_Q_W + 6 * NSA_KV_W
    rows_per_chunk = CMP_STRIDE * NSA_KV_W

    x2 = x.reshape(b * s, D_MODEL)
    mem2 = mem.reshape(-1, D_MODEL)
    for l in range(depth):
        w_att = jnp.concatenate(
            [w_in[l, :, :gate_lo + NSA_GATE_W],
             jnp.zeros((D_MODEL, GATE_PAD - NSA_GATE_W), w_in.dtype),
             w_in[l, :, gate_lo + NSA_GATE_W:ATT_W]], axis=1).astype(BF16)
        w_gates = w_in[l, :, ATT_W:].astype(BF16)
        row = lambda v: v[l].reshape(1, D_MODEL)

        qn, kc_raw, vc_raw, ks, vs, kw, vw, gn, qm, km, vm, qx = _inproj(x2, row(pre_mix_g), w_att, sel_cols)

        pk, w1k = _compress_weights(cmp_pos_k[l], cmp_w1_k[l])
        pv, w1v = _compress_weights(cmp_pos_v[l], cmp_w1_v[l])
        kc, vct = _compress(kc_raw.reshape(b, s // CMP_STRIDE, rows_per_chunk),
                            vc_raw.reshape(b, s // CMP_STRIDE, rows_per_chunk),
                            pk, pv, w1k, w1v, cmp_w2_k[l].astype(BF16), cmp_w2_v[l].astype(BF16))

        mk, mv = _memkv(mem2, row(mem_norm_g), w_mem_kv[l].astype(BF16))

        o_nsa, o_moba, o_mem = _attention(b, s, qn, gn, kc, vct, ks, vs, kw, vw, b_cmp, t_nsa, t_win, c_far_nsa, ovt,
                                          qm, km, vm, qx, mk, mv, t_moba, c_far_moba)

        x2 = _mix(x2, o_nsa, o_moba, o_mem, row(pre_mix_g), row(post_mix_g), w_gates,
                  w_nsa_o[l].astype(BF16), w_moba_o[l].astype(BF16), w_mem_o[l].astype(BF16),
                  w_mix_out[l].astype(BF16))
        x2 = _ffn(x2, row(pre_ffn_g), row(post_ffn_g), w_ffn_gate[l].astype(BF16),
                  w_ffn_up[l].astype(BF16), w_ffn_down[l].astype(BF16))
    return x2.reshape(b, s, D_MODEL)
```

```python
import functools
import math

import numpy as np
import jax
import jax.numpy as jnp
from jax import lax
from jax.experimental import pallas as pl
from jax.experimental.pallas import tpu as pltpu

F32 = jnp.float32
BF16 = jnp.bfloat16

D_MODEL = 1024
HEAD_DIM = 64
SCALE = HEAD_DIM ** -0.5
LOG2E = math.log2(math.e)
Q_SCALE = SCALE * LOG2E
NSA_HEADS = 8
NSA_GROUPS = 2
NSA_HPG = NSA_HEADS // NSA_GROUPS
CMP_LEN = 32
CMP_STRIDE = 16
CMP_HIDDEN = 128
SEL_BLOCK = 64
SEL_TOPN = 8
WINDOW = 256
MOBA_HEADS = 4
MOBA_BLOCK = 256
MOBA_TOPK = 3
MEM_HEADS = 4
REL_BUCKETS = 32
REL_MAX_DIST = 128
N_BIAS_HEADS = NSA_HEADS + MOBA_HEADS
RMS_EPS = 1e-6
NEG_INF = -1e30
FORCE_SCORE = 1e4

NSA_Q_W = NSA_HEADS * HEAD_DIM
NSA_KV_W = NSA_GROUPS * HEAD_DIM
NSA_GATE_W = NSA_HEADS * 3
MOBA_W = MOBA_HEADS * HEAD_DIM
MEM_W = MEM_HEADS * HEAD_DIM
ATT_W = NSA_Q_W + 6 * NSA_KV_W + NSA_GATE_W + 3 * MOBA_W + MEM_W
LANES = 128
SUBLANES = 8
BF16_ROWS = 16
MXU_COLS = 256
GATE_PAD = LANES
TILE = 256
HALF = TILE // 2
RUN_AHEAD = 2
N_CMP_PAD = 128
V_AUG = HEAD_DIM + BF16_ROWS
MASKED_BUCKET = REL_BUCKETS
VMEM_LIMIT = 56 * 1024 * 1024


def _dot(a, b):
    return jnp.dot(a, b, preferred_element_type=F32)


def _split_bf16(x):
    hi = x.astype(BF16)
    lo = (x - hi.astype(F32)).astype(BF16)
    return hi, lo


def _rms(x, g):
    return x * lax.rsqrt(jnp.mean(x * x, axis=-1, keepdims=True) + RMS_EPS) * g


def _params(sem):
    return pltpu.CompilerParams(dimension_semantics=sem, vmem_limit_bytes=VMEM_LIMIT)


def _const_spec(shape):
    nd = len(shape)
    return pl.BlockSpec(shape, lambda *_: (0,) * nd, pipeline_mode=pl.Buffered(1))


_INPROJ_OUTS = (
    ("qn", NSA_Q_W, BF16, True),
    ("kc", NSA_KV_W, F32, False), ("vc", NSA_KV_W, F32, False),
    ("ks", NSA_KV_W, BF16, False), ("vs", NSA_KV_W, BF16, False),
    ("kw", NSA_KV_W, BF16, False), ("vw", NSA_KV_W, BF16, False),
    ("gn", GATE_PAD, F32, False),
    ("qm", MOBA_W, BF16, True), ("km", MOBA_W, BF16, False), ("vm", MOBA_W, BF16, False),
    ("qx", MEM_W, BF16, True),
)
_INPROJ_W = sum(o[1] for o in _INPROJ_OUTS)
_INPROJ_CHUNKED = ("kc", "vc")
_INPROJ_TRANSPOSED = ("qn", "gn", "qm", "qx")
KS_AUG_W = NSA_GROUPS * 2 * HEAD_DIM


def _inproj_out_width(name, width):
    return KS_AUG_W if name == "ks" else width


def _inproj_kernel(x_ref, g_ref, w_ref, e_ref, *refs):
    out_refs, rows_sc = refs[:-1], refs[-1]
    h = _rms(x_ref[...], g_ref[...]).astype(BF16)
    runs, lo = [], 0
    for out in zip(_INPROJ_OUTS, out_refs):
        if runs and runs[-1][1] < MXU_COLS:
            runs[-1][0].append(out)
            runs[-1][1] += out[0][1]
        else:
            runs.append([[out], out[0][1], lo])
        lo += out[0][1]
    for outs, run_width, run_lo in runs:
        y_run = _dot(h, w_ref[:, run_lo:run_lo + run_width])
        lo = 0
        for (name, width, dtype, scaled), o_ref in outs:
            y = y_run[:, lo:lo + width]
            if scaled:
                y = y * Q_SCALE
            if name == "gn":
                y = jax.nn.sigmoid(y)
            if name in _INPROJ_TRANSPOSED:
                y = y.T
            y = y.astype(dtype)
            if name == "ks":
                e = e_ref[...]
                y = _lane_cat([y[:, :HEAD_DIM], e, y[:, HEAD_DIM:], e])
            if name in _INPROJ_CHUNKED:
                rows_sc[...] = y
                for j in range(CMP_STRIDE):
                    o_ref[:, j * width:(j + 1) * width] = rows_sc[pl.ds(j, o_ref.shape[0], stride=CMP_STRIDE), :]
            else:
                o_ref[...] = y
            lo += width


def _inproj(x2, g, w, e_cols, tm=512):
    m = x2.shape[0]
    tiles_per_seq = e_cols.shape[0] // tm
    out_specs, out_shape = [], []
    for name, width, dtype, _ in _INPROJ_OUTS:
        if name in _INPROJ_TRANSPOSED:
            out_specs.append(pl.BlockSpec((width, tm), lambda i: (0, i)))
            out_shape.append(jax.ShapeDtypeStruct((width, m), dtype))
            continue
        rows, width = (CMP_STRIDE, CMP_STRIDE * width) if name in _INPROJ_CHUNKED else (1, _inproj_out_width(name, width))
        out_specs.append(pl.BlockSpec((tm // rows, width), lambda i: (i, 0)))
        out_shape.append(jax.ShapeDtypeStruct((m // rows, width), dtype))
    return pl.pallas_call(
        _inproj_kernel,
        grid=(m // tm,),
        in_specs=[pl.BlockSpec((tm, D_MODEL), lambda i: (i, 0)),
                  _const_spec((1, D_MODEL)),
                  _const_spec((D_MODEL, _INPROJ_W)),
                  pl.BlockSpec((tm, HEAD_DIM), lambda i: (i % tiles_per_seq, 0))],
        out_specs=out_specs,
        out_shape=out_shape,
        scratch_shapes=[pltpu.VMEM((tm, NSA_KV_W), F32)],
        compiler_params=_params(("parallel",)),
        name="inproj",
    )(x2, g, w, e_cols)


def _compress_kernel(rk_ref, rv_ref, pk_ref, pv_ref, w1k_ref, w1v_ref, w2k_ref, w2v_ref, kc_ref, vc_ref):
    nb = rk_ref.shape[0]

    def one(r_ref, p_ref, w1_ref, w2_ref):
        r = r_ref[...].reshape(nb * N_CMP_PAD, r_ref.shape[2])
        top = _dot((r + p_ref[0:1, :]).astype(BF16), w1_ref[0])
        bot = _dot((r + p_ref[1:2, :]).astype(BF16), w1_ref[1])
        hid = top + pltpu.roll(bot, nb * N_CMP_PAD - 1, 0)
        act = jax.nn.gelu(hid).astype(BF16)
        return jnp.concatenate(
            [_dot(act[:, g * CMP_HIDDEN:(g + 1) * CMP_HIDDEN], w2_ref[...]) for g in range(NSA_GROUPS)], axis=1)

    k_out = one(rk_ref, pk_ref, w1k_ref, w2k_ref)
    v_out = one(rv_ref, pv_ref, w1v_ref, w2v_ref)
    for n in range(nb):
        rows = slice(n * N_CMP_PAD, (n + 1) * N_CMP_PAD)
        kc_ref[n] = k_out[rows].astype(BF16)
        vc_ref[n] = v_out[rows].T.astype(BF16)


def _compress(rk, rv, pk, pv, w1k, w1v, w2k, w2v):
    b = rk.shape[0]
    rw = rk.shape[2]
    nb = 2 if b % 2 == 0 else 1
    r_spec = pl.BlockSpec((nb, N_CMP_PAD, rw), lambda i: (i, 0, 0))
    o_spec = pl.BlockSpec((nb, N_CMP_PAD, NSA_KV_W), lambda i: (i, 0, 0))
    return pl.pallas_call(
        _compress_kernel,
        grid=(b // nb,),
        in_specs=[r_spec, r_spec, _const_spec(pk.shape), _const_spec(pv.shape),
                  _const_spec(w1k.shape), _const_spec(w1v.shape),
                  _const_spec(w2k.shape), _const_spec(w2v.shape)],
        out_specs=[o_spec, o_spec],
        out_shape=[jax.ShapeDtypeStruct((b, N_CMP_PAD, NSA_KV_W), BF16)] * 2,
        compiler_params=_params(("parallel",)),
        name="compress",
    )(rk, rv, pk, pv, w1k, w1v, w2k, w2v)


def _compress_weights(pos, w1):
    half = CMP_LEN // 2
    p = pos.reshape(2, half, 1, HEAD_DIM)
    p = jnp.broadcast_to(p, (2, half, NSA_GROUPS, HEAD_DIM)).reshape(2, half * NSA_KV_W)
    w = w1.reshape(2, half, HEAD_DIM, CMP_HIDDEN)
    eye = jnp.eye(NSA_GROUPS, dtype=w1.dtype)
    wbd = jnp.einsum("ajdm,gk->ajgdkm", w, eye).reshape(2, half * NSA_KV_W, NSA_GROUPS * CMP_HIDDEN)
    return p.astype(F32), wbd.astype(BF16)


def _memkv_kernel(m_ref, g_ref, w_ref, k_ref, v_ref):
    h = _rms(m_ref[...], g_ref[...]).astype(BF16)
    k_ref[...] = _dot(h, w_ref[:, :MEM_W]).astype(BF16)
    v_ref[...] = _dot(h, w_ref[:, MEM_W:]).astype(BF16)


def _memkv(mem2, g, w, tm=512):
    m = mem2.shape[0]
    tm = min(tm, m)
    o_spec = pl.BlockSpec((tm, MEM_W), lambda i: (i, 0))
    return pl.pallas_call(
        _memkv_kernel,
        grid=(m // tm,),
        in_specs=[pl.BlockSpec((tm, D_MODEL), lambda i: (i, 0)), _const_spec((1, D_MODEL)),
                  _const_spec((D_MODEL, 2 * MEM_W))],
        out_specs=[o_spec, o_spec],
        out_shape=[jax.ShapeDtypeStruct((m, MEM_W), BF16)] * 2,
        compiler_params=_params(("parallel",)),
        name="memkv",
    )(mem2, g, w)


def _expand_kernel(idx_ref, bias_ref, o_ref, *, head0, n_heads, heads_per_group):
    rows, cols = idx_ref.shape

    def body(i, carry):
        r = pl.multiple_of(i * SUBLANES, SUBLANES)
        for c0 in range(0, cols, TILE):
            idx = idx_ref[pl.ds(r, SUBLANES), c0:c0 + TILE]
            out = [jnp.full(idx.shape, NEG_INF, F32)] * n_heads
            for bkt in range(REL_BUCKETS):
                hit = idx == bkt
                out = [jnp.where(hit, bias_ref[bkt, head0 + h], out[h]) for h in range(n_heads)]
            for h in range(n_heads):
                col = (h % heads_per_group) * cols + c0
                o_ref[h // heads_per_group, pl.ds(r, SUBLANES), col:col + TILE] = out[h] * LOG2E
        return carry

    lax.fori_loop(0, rows // SUBLANES, body, 0)


def _expand(idx, rel_bias, head0, n_heads, heads_per_group=1):
    rows, cols = idx.shape
    return pl.pallas_call(
        functools.partial(_expand_kernel, head0=head0, n_heads=n_heads, heads_per_group=heads_per_group),
        in_specs=[pl.BlockSpec(memory_space=pltpu.VMEM), pl.BlockSpec(memory_space=pltpu.SMEM)],
        out_specs=pl.BlockSpec(memory_space=pltpu.VMEM),
        out_shape=jax.ShapeDtypeStruct((n_heads // heads_per_group, rows, heads_per_group * cols), F32),
        compiler_params=pltpu.CompilerParams(vmem_limit_bytes=VMEM_LIMIT),
        name="bias_expand",
    )(idx, rel_bias)


def _t5_bucket_np(dist):
    dist = np.maximum(dist, 0)
    max_exact = REL_BUCKETS // 2
    logd = np.log(np.maximum(dist, 1).astype(np.float32) / max_exact) / math.log(REL_MAX_DIST / max_exact)
    large = np.minimum(max_exact + (logd * (REL_BUCKETS - max_exact)).astype(np.int32), REL_BUCKETS - 1)
    return np.where(dist < max_exact, dist, large).astype(np.int32)


def _bucket_tables(s):
    j = np.arange(TILE)[:, None]
    i = np.arange(TILE)[None, :]
    assert TILE + 1 >= REL_MAX_DIST
    tiles = []
    for d in range(2):
        dist = d * TILE + i - j
        tiles.append(np.where(dist >= 0, _t5_bucket_np(dist), MASKED_BUCKET))
    dist1 = TILE + i - j
    win = np.where(dist1 < WINDOW, _t5_bucket_np(dist1), MASKED_BUCKET)
    n_cmp = (s - CMP_LEN) // CMP_STRIDE + 1
    assert n_cmp * CMP_STRIDE + CMP_LEN - 1 >= s or n_cmp == N_CMP_PAD
    rel = np.arange(N_CMP_PAD + (s - TILE) // CMP_STRIDE)[:, None] - (s - TILE) // CMP_STRIDE
    dist_c = i - (rel * CMP_STRIDE + CMP_LEN - 1)
    cmp_idx = np.where(dist_c >= 0, _t5_bucket_np(dist_c), MASKED_BUCKET)
    as_i32 = lambda a: jnp.asarray(a.astype(np.int32))
    return as_i32(np.concatenate(tiles, axis=0)), as_i32(win), as_i32(cmp_idx)


def _overlap_table(s):
    n_cmp = (s - CMP_LEN) // CMP_STRIDE + 1
    n_sel = s // SEL_BLOCK
    cs = np.arange(n_cmp) * CMP_STRIDE
    ss = np.arange(n_sel) * SEL_BLOCK
    ov = np.clip(np.minimum(cs[:, None] + CMP_LEN, ss[None, :] + SEL_BLOCK)
                 - np.maximum(cs[:, None], ss[None, :]), 0, None).astype(np.float32) / CMP_LEN
    ovt = np.zeros((n_sel, N_CMP_PAD), np.float32)
    ovt[:, :n_cmp] = ov.T
    return jnp.asarray(ovt, BF16)


def _store_v_aug(vt_sc, idx, vt):
    ones = jnp.ones((BF16_ROWS, vt.shape[1]), BF16)
    vt_sc[idx] = jnp.concatenate([vt.astype(BF16), ones], axis=0)


def _lane_cat(xs):
    return jnp.concatenate(xs, axis=1)


def _query_halves(x):
    n = x.shape[-1] // TILE
    first = _lane_cat([x[:, k * TILE:k * TILE + HALF] for k in range(n)])
    second = _lane_cat([x[:, k * TILE + HALF:(k + 1) * TILE] for k in range(n)])
    return first, second


def _join_query_halves(first, second):
    n = first.shape[-1] // HALF
    return _lane_cat([part for k in range(n)
                      for part in (first[:, k * HALF:(k + 1) * HALF], second[:, k * HALF:(k + 1) * HALF])])


def _triangle_scores(k, q, table, causal):
    q_first, q_second = _query_halves(q)
    lo, hi = (0, HALF), (HALF, TILE)
    if causal:
        return _dot(k(*lo), q) + table(*lo), _dot(k(*hi), q_second) + _query_halves(table(*hi))[1]
    return _dot(k(*hi), q) + table(*hi), _dot(k(*lo), q_first) + _query_halves(table(*lo))[0]


def _triangle_softmax(scores, pv_lo, pv_hi, causal):
    s_wide, s_narrow = scores
    pv_wide, pv_narrow = (pv_lo, pv_hi) if causal else (pv_hi, pv_lo)
    mw_first, mw_second = _query_halves(jnp.max(s_wide, axis=0, keepdims=True))
    m_narrow = jnp.max(s_narrow, axis=0, keepdims=True)
    if causal:
        m_narrow = jnp.maximum(m_narrow, mw_second)
        m = _join_query_halves(mw_first, m_narrow)
    else:
        m_narrow = jnp.maximum(m_narrow, mw_first)
        m = _join_query_halves(m_narrow, mw_second)
    aw_first, aw_second = _query_halves(pv_wide(jnp.exp2(s_wide - m).astype(BF16)))
    a_narrow = pv_narrow(jnp.exp2(s_narrow - m_narrow).astype(BF16))
    if causal:
        return m, _join_query_halves(aw_first, aw_second + a_narrow)
    return m, _join_query_halves(aw_first + a_narrow, aw_second)


def _flash_pipelined(own, streams, s_sc, m_ref, acc_ref):
    has_prev = jnp.where(own > 0, 1.0, 0.0).astype(F32)
    prev = jnp.maximum(own - 1, 0)
    n_far = jnp.maximum(own - 1, 0)

    def absorb(g, s, kt, c_row, w_row):
        u = jnp.max(s, axis=0, keepdims=True) + c_row
        m_old = m_ref[g]
        m_new = jnp.maximum(m_old, jnp.where(w_row > 0.0, u, NEG_INF))
        alpha = jnp.exp2(m_old - m_new)
        shift = jnp.maximum(m_new, u) - c_row
        p = jnp.exp2(s - shift).astype(BF16)
        acc_ref[g] = alpha * acc_ref[g] + w_row * streams[g]["pv"](kt)(p)
        m_ref[g] = m_new

    def absorb_slot(g, i):
        is_prev = i == 0
        kt = jnp.where(is_prev, prev, i - 1)
        c_row = jnp.where(is_prev, 0.0, streams[g]["c_far"])
        w_row = streams[g]["w"](kt) * jnp.where(is_prev, has_prev, 1.0)
        absorb(g, s_sc[g], kt, c_row, w_row)

    def body(i, carry):
        for g in reversed(range(len(streams))):
            nxt = streams[g]["far"](i)
            absorb_slot(g, i)
            s_sc[g] = nxt
        return carry

    lax.fori_loop(0, n_far, body, 0)
    for g in range(len(streams)):
        absorb_slot(g, n_far)


def _run_ahead(jobs):
    depth = RUN_AHEAD
    pending = [job[0]() for job in jobs[:depth]]
    for k, (_, consume) in enumerate(jobs):
        if k + depth < len(jobs):
            pending.append(jobs[k + depth][0]())
        consume(pending.pop(0))


def _flash_start_jobs(g, stream, s_sc, m_ref, acc_ref):
    def init(scores):
        m_ref[g], acc_ref[g] = stream["own_softmax"](scores)

    def park(scores):
        s_sc[g] = scores

    return [(stream["own_scores"], init), (stream["prev"], park)]


def _softmax_av(s_list, pv_list):
    m = s_list[0].max(axis=0, keepdims=True)
    for s in s_list[1:]:
        m = jnp.maximum(m, s.max(axis=0, keepdims=True))
    acc = None
    for s, pv in zip(s_list, pv_list):
        part = pv(jnp.exp2(s - m).astype(BF16))
        acc = part if acc is None else acc + part
    return acc


def _normalize(acc):
    return acc[:HEAD_DIM] * (1.0 / acc[HEAD_DIM:HEAD_DIM + 1])


def _rank_before(score, n_cand):
    ranks = []
    for r0 in range(0, score.shape[0], SUBLANES):
        tile = score[r0:r0 + SUBLANES]
        blk = lax.broadcasted_iota(jnp.int32, tile.shape, 0) + r0
        rank = jnp.zeros(tile.shape, F32)
        for m in range(n_cand):
            row = score[m:m + 1, :]
            if m < r0:
                before = jnp.where(row >= tile, 1.0, 0.0)
            elif m >= r0 + SUBLANES:
                before = jnp.where(row > tile, 1.0, 0.0)
            else:
                before = jnp.where(blk > m, jnp.where(row >= tile, 1.0, 0.0), jnp.where(row > tile, 1.0, 0.0))
            rank = rank + before
        ranks.append(rank)
    return jnp.concatenate(ranks, axis=0)


def _nsa_steps(qi, maybe_first, q_ref, gn_ref, kc_ref, vct_ref, ks_ref, vs_ref, kw_ref, vw_ref,
               bct_ref, tt_ref, twt_ref, cfar_ref, ovt_ref, o_ref,
               vst_sc, vwt_sc, qa_sc, og_sc, acc_sc):
    nt = ks_ref.shape[1]
    n_sel = ovt_ref.shape[0]

    def transpose_v():
        for kt in range(nt):
            vs_t = vs_ref[0, kt].astype(F32).T
            vw_t = vw_ref[0, kt].astype(F32).T
            for g in range(NSA_GROUPS):
                _store_v_aug(vst_sc, (kt, g), vs_t[g * HEAD_DIM:(g + 1) * HEAD_DIM])
                _store_v_aug(vwt_sc, (kt, g), vw_t[g * HEAD_DIM:(g + 1) * HEAD_DIM])

    if maybe_first:
        pl.when(qi == 0)(transpose_v)

    pos = lax.broadcasted_iota(jnp.int32, (1, TILE), 1) + qi * TILE
    cur = pos // SEL_BLOCK
    has_cmp = pos >= CMP_LEN - 1
    blk = lax.broadcasted_iota(jnp.int32, (n_sel, TILE), 0)
    prev = jnp.maximum(qi - 1, 0)
    gates = gn_ref[...]

    gsls = [slice(g * HEAD_DIM, (g + 1) * HEAD_DIM) for g in range(NSA_GROUPS)]
    group_heads = [[g * NSA_HPG + j for j in range(NSA_HPG)] for g in range(NSA_GROUPS)]

    def gate(g, branch):
        return _lane_cat([gates[3 * h + branch:3 * h + branch + 1, :] for h in group_heads[g]])

    ones_row = jnp.ones((1, NSA_HPG * TILE), F32)

    def sel_stream(g):
        def qk(kt):
            return _dot(ks_ref[0, kt, :, g * 2 * HEAD_DIM:(g + 1) * 2 * HEAD_DIM], qa_sc[g])

        def own_scores():
            return _triangle_scores(
                lambda lo, hi: ks_ref[0, qi, lo:hi, g * 2 * HEAD_DIM:(g + 1) * 2 * HEAD_DIM], qa_sc[g],
                lambda lo, hi: tt_ref[g, 0, lo:hi, :], causal=True)

        def own_softmax(scores):
            return _triangle_softmax(
                scores, lambda pr: _dot(vst_sc[qi, g, :, 0:HALF], pr),
                lambda pr: _dot(vst_sc[qi, g, :, HALF:TILE], pr), causal=True)

        return dict(own_scores=own_scores, own_softmax=own_softmax, prev=lambda: qk(prev) + tt_ref[g, 1], far=qk,
                    c_far=cfar_ref[g], w=lambda kt: ones_row,
                    pv=lambda kt: (lambda pr: _dot(vst_sc[kt, g], pr)))

    def group_jobs(g):
        heads = group_heads[g]
        win = {}

        def v_half(kt, lo, hi):
            return lambda pr: _dot(vwt_sc[kt, g, :, lo:hi], pr)

        def window_own_scores():
            q4 = _lane_cat([q_ref[h * HEAD_DIM:(h + 1) * HEAD_DIM, :] for h in heads])
            qa_sc[g, 0:HEAD_DIM, :] = q4
            qa_sc[g, HEAD_DIM + n_sel:, :] = jnp.zeros((HEAD_DIM - n_sel, NSA_HPG * TILE), BF16)
            return _triangle_scores(lambda lo, hi: kw_ref[0, qi, lo:hi, gsls[g]], q4,
                                    lambda lo, hi: tt_ref[g, 0, lo:hi, :], causal=True)

        def window_own(scores):
            win["own"] = _triangle_softmax(scores, v_half(qi, 0, HALF), v_half(qi, HALF, TILE), causal=True)

        def window_prev_scores():
            return _triangle_scores(lambda lo, hi: kw_ref[0, prev, lo:hi, gsls[g]], qa_sc[g, 0:HEAD_DIM, :],
                                    lambda lo, hi: twt_ref[g, lo:hi, :], causal=False)

        def window_prev(scores):
            m_prev, acc_prev = _triangle_softmax(scores, v_half(prev, 0, HALF), v_half(prev, HALF, TILE), causal=False)
            m_prev = jnp.where(qi == 0, NEG_INF, m_prev)
            m_own, acc_own = win["own"]
            m_win = jnp.maximum(m_own, m_prev)
            acc_w = acc_own * jnp.exp2(m_own - m_win) + acc_prev * jnp.exp2(m_prev - m_win)
            og_sc[g] = gate(g, 2) * _normalize(acc_w)

        def compressed_scores():
            rows_per_tile = TILE // CMP_STRIDE
            c_rows = pl.ds(pl.multiple_of((nt - 1 - qi) * rows_per_tile, rows_per_tile), N_CMP_PAD)
            return (_dot(kc_ref[0, :, gsls[g]], qa_sc[g, 0:HEAD_DIM, :])
                    + _lane_cat([bct_ref[h, c_rows, :] for h in heads]))

        def compressed(s):
            e = jnp.exp2(s - jnp.max(s, axis=0, keepdims=True))
            p = e * jnp.where(_lane_cat([has_cmp] * NSA_HPG), 1.0 / jnp.sum(e, axis=0, keepdims=True), 0.0)
            psum = p[:, :TILE]
            for j in range(1, NSA_HPG):
                psum = psum + p[:, j * TILE:(j + 1) * TILE]
            og_sc[g] = og_sc[g] + gate(g, 0) * _dot(vct_ref[0, gsls[g], :], p.astype(BF16))

            p_hi, p_lo = _split_bf16(psum)
            imp = _dot(ovt_ref[...], p_hi) + _dot(ovt_ref[...], p_lo)
            forced = (blk == 0) | (blk == cur) | (blk == cur - 1)
            score = jnp.where(forced, FORCE_SCORE, jnp.where(blk <= cur, imp, NEG_INF))
            rank = _rank_before(score, n_sel)
            sel = jnp.where(rank < SEL_TOPN, jnp.where(score > NEG_INF / 2, 0.0, NEG_INF), NEG_INF)
            qa_sc[g, HEAD_DIM:HEAD_DIM + n_sel, :] = _lane_cat([sel.astype(BF16)] * NSA_HPG)

        return [(window_own_scores, window_own), (window_prev_scores, window_prev), (compressed_scores, compressed)]

    yield [job for g in range(NSA_GROUPS) for job in group_jobs(g)]
    yield [sel_stream(g) for g in range(NSA_GROUPS)]

    for g in range(NSA_GROUPS):
        o = og_sc[g] + gate(g, 1) * _normalize(acc_sc[g])
        for j, h in enumerate(group_heads[g]):
            o_ref[h * HEAD_DIM:(h + 1) * HEAD_DIM, :] = o[:, j * TILE:(j + 1) * TILE].astype(BF16)


def _moba_steps(c, maybe_first, qm_ref, km_ref, vm_ref, qx_ref, mk_ref, mv_ref, tt_ref, cfar_ref, om_ref, ox_ref,
                vmt_sc, mvt_sc, kmean_sc, qbd_sc, sel_sc, acc_sc, slot):
    nt = km_ref.shape[1]
    hsls = [slice(h * HEAD_DIM, (h + 1) * HEAD_DIM) for h in range(MOBA_HEADS)]

    def per_sequence():
        kmean_sc[...] = jnp.zeros(kmean_sc.shape, F32)
        for n in range(nt):
            kmean_sc[n:n + 1, :] = jnp.sum(km_ref[0, n].astype(F32), axis=0, keepdims=True) * (1.0 / MOBA_BLOCK)
            vt = vm_ref[0, n].astype(F32).T
            for h in range(MOBA_HEADS):
                _store_v_aug(vmt_sc, (n, h), vt[hsls[h]])
        mvt = mv_ref[0].astype(F32).T
        for h in range(MOBA_HEADS):
            _store_v_aug(mvt_sc, h, mvt[hsls[h]])

    if maybe_first:
        pl.when(c == 0)(per_sequence)

    row_head = lax.broadcasted_iota(jnp.int32, (MOBA_W, TILE), 0) // HEAD_DIM

    def block_diag(q_ref):
        q_t = q_ref[...].astype(F32)
        return _lane_cat([jnp.where(row_head == h, q_t, 0.0) for h in range(MOBA_HEADS)]).astype(BF16)

    def per_head_pv(vts):
        return lambda pr: _lane_cat([_dot(vts(h), pr[:, h * TILE:(h + 1) * TILE]) for h in range(MOBA_HEADS)])

    def store_heads(o_t, out_ref):
        for h in range(MOBA_HEADS):
            out_ref[hsls[h], :] = o_t[:, h * TILE:(h + 1) * TILE].astype(BF16)

    n_rows = -(-nt // SUBLANES) * SUBLANES

    def gate_scores():
        qbd = block_diag(qm_ref)
        qbd_sc[...] = qbd
        km_hi, km_lo = _split_bf16(kmean_sc[...])
        return (_dot(km_hi, qbd) + _dot(km_lo, qbd))[:n_rows]

    def select(gate):
        blk = lax.broadcasted_iota(jnp.int32, gate.shape, 0)
        score = jnp.where(blk < c, gate, NEG_INF * Q_SCALE)
        rank = _rank_before(score, nt)
        sel_sc[0:n_rows, :] = jnp.where(rank < MOBA_TOPK, jnp.where(score > NEG_INF * Q_SCALE / 2, 1.0, 0.0), 0.0)

    qk = lambda n: _dot(km_ref[0, n], qbd_sc[...])
    def own_pv(lo, hi):
        def pv(pr):
            width = pr.shape[1] // MOBA_HEADS
            return _lane_cat([_dot(vmt_sc[c, h, :, lo:hi], pr[:, h * width:(h + 1) * width])
                              for h in range(MOBA_HEADS)])
        return pv

    def own_scores():
        return _triangle_scores(lambda lo, hi: km_ref[0, c, lo:hi, :], qbd_sc[...],
                                lambda lo, hi: tt_ref[0, lo:hi, :], causal=True)

    def own_softmax(scores):
        return _triangle_softmax(scores, own_pv(0, HALF), own_pv(HALF, TILE), causal=True)

    stream = dict(own_scores=own_scores, own_softmax=own_softmax,
                  prev=lambda: qk(jnp.maximum(c - 1, 0)) + tt_ref[1], far=qk,
                  c_far=cfar_ref[...], w=lambda n: sel_sc[pl.ds(n, 1), :],
                  pv=lambda n: per_head_pv(lambda h: vmt_sc[n, h]))
    def memory(s):
        store_heads(_normalize(_softmax_av([s], [per_head_pv(lambda h: mvt_sc[h])])), ox_ref)

    yield [(gate_scores, select), (lambda: _dot(mk_ref[0], block_diag(qx_ref)), memory)]
    yield [stream]
    store_heads(_normalize(acc_sc[slot]), om_ref)


N_STREAMS = NSA_GROUPS + 1
TILES_PER_STEP = 2
NSA_ROW_INPUTS = (0, 1)
MOBA_ROW_INPUTS = (0, 3)


def _attention_kernel(*refs, n_nsa_in, n_moba_in, n_nsa_scratch, n_moba_scratch):
    nsa_in, refs = refs[:n_nsa_in], refs[n_nsa_in:]
    moba_in, refs = refs[:n_moba_in], refs[n_moba_in:]
    outs, refs = refs[:3], refs[3:]
    nsa_sc, refs = refs[:n_nsa_scratch], refs[n_nsa_scratch:]
    moba_sc, refs = refs[:n_moba_scratch], refs[n_moba_scratch:]
    s_sc, m_sc, acc_sc = refs
    for t in range(TILES_PER_STEP):
        qi = pl.program_id(1) * TILES_PER_STEP + t
        cols = lambda ref: ref.at[:, t * TILE:(t + 1) * TILE]
        o_nsa, o_moba, o_mem = [cols(o) for o in outs]
        nsa_refs = [cols(r) if k in NSA_ROW_INPUTS else r for k, r in enumerate(nsa_in)]
        moba_refs = [cols(r) if k in MOBA_ROW_INPUTS else r for k, r in enumerate(moba_in)]
        nsa = _nsa_steps(qi, t == 0, *nsa_refs, o_nsa, *nsa_sc, acc_sc)
        moba = _moba_steps(qi, t == 0, *moba_refs, o_moba, o_mem, *moba_sc, acc_sc, NSA_GROUPS)
        jobs = next(nsa) + next(moba)
        streams = next(nsa) + next(moba)
        jobs += [job for g, stream in enumerate(streams) for job in _flash_start_jobs(g, stream, s_sc, m_sc, acc_sc)]
        _run_ahead(jobs)
        _flash_pipelined(qi, streams, s_sc, m_sc, acc_sc)
        for steps in (nsa, moba):
            for _ in steps:
                pass


def _attention(b, s, qn, gn, kc, vct, ks, vs, kw, vw, bias_cmp, t_nsa, t_win, c_far_nsa, ovt,
               qm, km, vm, qx, mk, mv, t_moba, c_far_moba):
    nt = s // TILE
    mem_len = mk.shape[0] // b
    assert MOBA_TOPK <= nt - 1 and nt <= BF16_ROWS
    n_lanes = NSA_HPG * TILE
    assert MOBA_HEADS * TILE == n_lanes
    assert nt % TILES_PER_STEP == 0
    steps = nt // TILES_PER_STEP
    row_spec = lambda w: pl.BlockSpec((w, TILES_PER_STEP * TILE), lambda i, j: (0, i * steps + j))
    seq_spec = lambda w: pl.BlockSpec((1, nt, TILE, w), lambda i, j: (i, 0, 0, 0))
    per_batch = lambda rows, w: pl.BlockSpec((1, rows, w), lambda i, j: (i, 0, 0))
    tiles = lambda a: a.reshape(b, nt, TILE, a.shape[-1])
    nsa_in = [(qn, row_spec(NSA_Q_W)), (gn, row_spec(GATE_PAD)),
              (kc, per_batch(N_CMP_PAD, NSA_KV_W)), (vct, per_batch(N_CMP_PAD, NSA_KV_W)),
              (tiles(ks), seq_spec(KS_AUG_W)), (tiles(vs), seq_spec(NSA_KV_W)),
              (tiles(kw), seq_spec(NSA_KV_W)), (tiles(vw), seq_spec(NSA_KV_W)),
              (bias_cmp, _const_spec(bias_cmp.shape)), (t_nsa, _const_spec(t_nsa.shape)),
              (t_win, _const_spec(t_win.shape)), (c_far_nsa, _const_spec(c_far_nsa.shape)),
              (ovt, _const_spec(ovt.shape))]
    moba_in = [(qm, row_spec(MOBA_W)), (tiles(km), seq_spec(MOBA_W)), (tiles(vm), seq_spec(MOBA_W)),
               (qx, row_spec(MEM_W)),
               (mk.reshape(b, mem_len, MEM_W), per_batch(mem_len, MEM_W)),
               (mv.reshape(b, mem_len, MEM_W), per_batch(mem_len, MEM_W)),
               (t_moba, _const_spec(t_moba.shape)), (c_far_moba, _const_spec(c_far_moba.shape))]
    nsa_scratch =[pltpu.VMEM((nt, NSA_GROUPS, V_AUG, TILE), BF16),
                   pltpu.VMEM((nt, NSA_GROUPS, V_AUG, TILE), BF16),
                   pltpu.VMEM((NSA_GROUPS, 2 * HEAD_DIM, n_lanes), BF16),
                   pltpu.VMEM((NSA_GROUPS, HEAD_DIM, n_lanes), F32)]
    moba_scratch = [pltpu.VMEM((nt, MOBA_HEADS, V_AUG, TILE), BF16),
                    pltpu.VMEM((MEM_HEADS, V_AUG, mem_len), BF16),
                    pltpu.VMEM((BF16_ROWS, MOBA_W), F32),
                    pltpu.VMEM((MOBA_W, n_lanes), BF16),
                    pltpu.VMEM((BF16_ROWS, n_lanes), F32)]
    shared_scratch =[pltpu.VMEM((N_STREAMS, TILE, n_lanes), F32),
                      pltpu.VMEM((N_STREAMS, 1, n_lanes), F32),
                      pltpu.VMEM((N_STREAMS, V_AUG, n_lanes), F32)]
    inputs = nsa_in + moba_in
    return pl.pallas_call(
        functools.partial(_attention_kernel, n_nsa_in=len(nsa_in), n_moba_in=len(moba_in),
                          n_nsa_scratch=len(nsa_scratch), n_moba_scratch=len(moba_scratch)),
        grid=(b, steps),
        in_specs=[spec for _, spec in inputs],
        out_specs=[row_spec(NSA_Q_W), row_spec(MOBA_W), row_spec(MEM_W)],
        out_shape=[jax.ShapeDtypeStruct((w, b * s), BF16) for w in (NSA_Q_W, MOBA_W, MEM_W)],
        scratch_shapes=nsa_scratch + moba_scratch + shared_scratch,
        compiler_params=_params(("arbitrary", "arbitrary")),
        name="attention",
    )(*[a for a, _ in inputs])


def _mix_kernel(x_ref, on_ref, om_ref, ox_ref, g_pre_ref, g_post_ref, wg_ref, wn_ref, wm_ref, wx_ref,
                wo_ref, o_ref):
    x = x_ref[...]
    h = _rms(x, g_pre_ref[...]).astype(BF16)
    rows = lambda o_ref: o_ref[...].astype(F32).T.astype(BF16)
    merged = jax.nn.sigmoid(_dot(h, wg_ref[:, :D_MODEL])) * _dot(rows(on_ref), wn_ref[...])
    merged = merged + jax.nn.sigmoid(_dot(h, wg_ref[:, D_MODEL:2 * D_MODEL])) * _dot(rows(om_ref), wm_ref[...])
    merged = merged + jax.nn.sigmoid(_dot(h, wg_ref[:, 2 * D_MODEL:])) * _dot(rows(ox_ref), wx_ref[...])
    y = _dot(merged.astype(BF16), wo_ref[...])
    o_ref[...] = x + _rms(y, g_post_ref[...])


def _mix(x2, o_nsa, o_moba, o_mem, g_pre, g_post, w_gates, w_nsa_o, w_moba_o, w_mem_o, w_mix_out, tm=512):
    m = x2.shape[0]
    row = lambda w: pl.BlockSpec((tm, w), lambda i: (i, 0))
    col = lambda w: pl.BlockSpec((w, tm), lambda i: (0, i))
    return pl.pallas_call(
        _mix_kernel,
        grid=(m // tm,),
        in_specs=[row(D_MODEL), col(NSA_Q_W), col(MOBA_W), col(MEM_W),
                  _const_spec((1, D_MODEL)), _const_spec((1, D_MODEL)),
                  _const_spec(w_gates.shape), _const_spec(w_nsa_o.shape), _const_spec(w_moba_o.shape),
                  _const_spec(w_mem_o.shape), _const_spec(w_mix_out.shape)],
        out_specs=row(D_MODEL),
        out_shape=jax.ShapeDtypeStruct((m, D_MODEL), F32),
        compiler_params=_params(("parallel",)),
        name="mix",
    )(x2, o_nsa, o_moba, o_mem, g_pre, g_post, w_gates, w_nsa_o, w_moba_o, w_mem_o, w_mix_out)


FFN_CHUNK = 256


def _ffn_kernel(x_ref, g_pre_ref, g_post_ref, wg_ref, wu_ref, wd_ref, o_ref, a_sc):
    x = x_ref[...]
    h = _rms(x, g_pre_ref[...]).astype(BF16)
    d_ff = wg_ref.shape[1]
    for j in range(d_ff // FFN_CHUNK):
        sl = slice(j * FFN_CHUNK, (j + 1) * FFN_CHUNK)
        a_sc[:, sl] = (jax.nn.silu(_dot(h, wg_ref[:, sl])) * _dot(h, wu_ref[:, sl])).astype(BF16)
    f = _dot(a_sc[...], wd_ref[...])
    o_ref[...] = x + _rms(f, g_post_ref[...])


def _ffn(x2, g_pre, g_post, wg, wu, wd, tm=512):
    m = x2.shape[0]
    d_ff = wg.shape[1]
    return pl.pallas_call(
        _ffn_kernel,
        grid=(m // tm,),
        in_specs=[pl.BlockSpec((tm, D_MODEL), lambda i: (i, 0)),
                  _const_spec((1, D_MODEL)), _const_spec((1, D_MODEL)),
                  _const_spec(wg.shape), _const_spec(wu.shape), _const_spec(wd.shape)],
        out_specs=pl.BlockSpec((tm, D_MODEL), lambda i: (i, 0)),
        out_shape=jax.ShapeDtypeStruct((m, D_MODEL), F32),
        scratch_shapes=[pltpu.VMEM((tm, d_ff), BF16)],
        compiler_params=_params(("parallel",)),
        name="ffn",
    )(x2, g_pre, g_post, wg, wu, wd)


def kernel(x, mem, rel_bias, pre_mix_g, mem_norm_g, post_mix_g, w_in, cmp_pos_k, cmp_w1_k, cmp_w2_k, cmp_pos_v, cmp_w1_v, cmp_w2_v, w_mem_kv, w_nsa_o, w_moba_o, w_mem_o, w_mix_out, pre_ffn_g, post_ffn_g, w_ffn_gate, w_ffn_up, w_ffn_down):
    b, s, d_model = x.shape
    depth = w_in.shape[0]
    assert d_model == D_MODEL and s % TILE == 0 and TILE == MOBA_BLOCK == WINDOW
    assert (s - CMP_LEN) // CMP_STRIDE + 1 < N_CMP_PAD and (s // SEL_BLOCK) % SUBLANES == 0 and s // SEL_BLOCK <= HEAD_DIM
    assert w_in.shape[2] == ATT_W + 3 * D_MODEL and rel_bias.shape == (REL_BUCKETS, N_BIAS_HEADS)

    tile_idx, win_idx, cmp_idx = _bucket_tables(s)
    rel_bias = rel_bias.astype(F32)
    t_nsa = _expand(tile_idx, rel_bias, 0, NSA_HEADS, NSA_HPG).reshape(NSA_GROUPS, 2, TILE, NSA_HPG * TILE)
    t_moba = _expand(tile_idx, rel_bias, NSA_HEADS, MOBA_HEADS, MOBA_HEADS).reshape(2, TILE, MOBA_HEADS * TILE)
    t_win = _expand(win_idx, rel_bias, 0, NSA_HEADS, NSA_HPG)
    b_cmp = _expand(cmp_idx, rel_bias, 0, NSA_HEADS)
    c_far = jnp.repeat(rel_bias[REL_BUCKETS - 1] * LOG2E, TILE)
    c_far_nsa = c_far[:NSA_HEADS * TILE].reshape(NSA_GROUPS, 1, NSA_HPG * TILE)
    c_far_moba = c_far[NSA_HEADS * TILE:].reshape(1, MOBA_HEADS * TILE)
    ovt = _overlap_table(s)
    sel_cols = np.zeros((s, HEAD_DIM), np.float32)
    sel_cols[np.arange(s), np.arange(s) // SEL_BLOCK] = 1.0
    sel_cols = jnp.asarray(sel_cols, BF16)
    gate_lo = NSA_Q_W + 6 * NSA_KV_W
    rows_per_chunk = CMP_STRIDE * NSA_KV_W

    x2 = x.reshape(b * s, D_MODEL)
    mem2 = mem.reshape(-1, D_MODEL)
    for l in range(depth):
        w_att = jnp.concatenate(
            [w_in[l, :, :gate_lo + NSA_GATE_W],
             jnp.zeros((D_MODEL, GATE_PAD - NSA_GATE_W), w_in.dtype),
             w_in[l, :, gate_lo + NSA_GATE_W:ATT_W]], axis=1).astype(BF16)
        w_gates = w_in[l, :, ATT_W:].astype(BF16)
        row = lambda v: v[l].reshape(1, D_MODEL)

        qn, kc_raw, vc_raw, ks, vs, kw, vw, gn, qm, km, vm, qx = _inproj(x2, row(pre_mix_g), w_att, sel_cols)

        pk, w1k = _compress_weights(cmp_pos_k[l], cmp_w1_k[l])
        pv, w1v = _compress_weights(cmp_pos_v[l], cmp_w1_v[l])
        kc, vct = _compress(kc_raw.reshape(b, s // CMP_STRIDE, rows_per_chunk),
                            vc_raw.reshape(b, s // CMP_STRIDE, rows_per_chunk),
                            pk, pv, w1k, w1v, cmp_w2_k[l].astype(BF16), cmp_w2_v[l].astype(BF16))

        mk, mv = _memkv(mem2, row(mem_norm_g), w_mem_kv[l].astype(BF16))

        o_nsa, o_moba, o_mem = _attention(b, s, qn, gn, kc, vct, ks, vs, kw, vw, b_cmp, t_nsa, t_win, c_far_nsa, ovt,
                                          qm, km, vm, qx, mk, mv, t_moba, c_far_moba)

        x2 = _mix(x2, o_nsa, o_moba, o_mem, row(pre_mix_g), row(post_mix_g), w_gates,
                  w_nsa_o[l].astype(BF16), w_moba_o[l].astype(BF16), w_mem_o[l].astype(BF16),
                  w_mix_out[l].astype(BF16))
        x2 = _ffn(x2, row(pre_ffn_g), row(post_ffn_g), w_ffn_gate[l].astype(BF16),
                  w_ffn_up[l].astype(BF16), w_ffn_down[l].astype(BF16))
    return x2.reshape(b, s, D_MODEL)
```

```python
import functools
import math

import numpy as np
import jax
import jax.numpy as jnp
from jax import lax
from jax.experimental import pallas as pl
from jax.experimental.pallas import tpu as pltpu

F32 = jnp.float32
BF16 = jnp.bfloat16

D_MODEL = 1024
HEAD_DIM = 64
SCALE = HEAD_DIM ** -0.5
LOG2E = math.log2(math.e)
Q_SCALE = SCALE * LOG2E
NSA_HEADS = 8
NSA_GROUPS = 2
NSA_HPG = NSA_HEADS // NSA_GROUPS
CMP_LEN = 32
CMP_STRIDE = 16
CMP_HIDDEN = 128
SEL_BLOCK = 64
SEL_TOPN = 8
WINDOW = 256
MOBA_HEADS = 4
MOBA_BLOCK = 256
MOBA_TOPK = 3
MEM_HEADS = 4
REL_BUCKETS = 32
REL_MAX_DIST = 128
N_BIAS_HEADS = NSA_HEADS + MOBA_HEADS
RMS_EPS = 1e-6
NEG_INF = -1e30
FORCE_SCORE = 1e4

NSA_Q_W = NSA_HEADS * HEAD_DIM
NSA_KV_W = NSA_GROUPS * HEAD_DIM
NSA_GATE_W = NSA_HEADS * 3
MOBA_W = MOBA_HEADS * HEAD_DIM
MEM_W = MEM_HEADS * HEAD_DIM
ATT_W = NSA_Q_W + 6 * NSA_KV_W + NSA_GATE_W + 3 * MOBA_W + MEM_W
LANES = 128
SUBLANES = 8
BF16_ROWS = 16
MXU_COLS = 256
GATE_PAD = LANES
TILE = 256
HALF = TILE // 2
RUN_AHEAD = 2
N_CMP_PAD = 128
V_AUG = HEAD_DIM + BF16_ROWS
MASKED_BUCKET = REL_BUCKETS
VMEM_LIMIT = 56 * 1024 * 1024


def _dot(a, b):
    return jnp.dot(a, b, preferred_element_type=F32)


def _split_bf16(x):
    hi = x.astype(BF16)
    lo = (x - hi.astype(F32)).astype(BF16)
    return hi, lo


def _rms(x, g):
    return x * lax.rsqrt(jnp.mean(x * x, axis=-1, keepdims=True) + RMS_EPS) * g


def _params(sem):
    return pltpu.CompilerParams(dimension_semantics=sem, vmem_limit_bytes=VMEM_LIMIT)


def _const_spec(shape):
    nd = len(shape)
    return pl.BlockSpec(shape, lambda *_: (0,) * nd, pipeline_mode=pl.Buffered(1))


_INPROJ_OUTS = (
    ("qn", NSA_Q_W, BF16, True),
    ("kc", NSA_KV_W, F32, False), ("vc", NSA_KV_W, F32, False),
    ("ks", NSA_KV_W, BF16, False), ("vs", NSA_KV_W, BF16, False),
    ("kw", NSA_KV_W, BF16, False), ("vw", NSA_KV_W, BF16, False),
    ("gn", GATE_PAD, F32, False),
    ("qm", MOBA_W, BF16, True), ("km", MOBA_W, BF16, False), ("vm", MOBA_W, BF16, False),
    ("qx", MEM_W, BF16, True),
)
_INPROJ_W = sum(o[1] for o in _INPROJ_OUTS)
_INPROJ_CHUNKED = ("kc", "vc")
_INPROJ_TRANSPOSED = ("qn", "gn", "qm", "qx")
KS_AUG_W = NSA_GROUPS * 2 * HEAD_DIM


def _inproj_out_width(name, width):
    return KS_AUG_W if name == "ks" else width


def _inproj_kernel(x_ref, g_ref, w_ref, e_ref, *refs):
    out_refs, rows_sc = refs[:-1], refs[-1]
    h = _rms(x_ref[...], g_ref[...]).astype(BF16)
    runs, lo = [], 0
    for out in zip(_INPROJ_OUTS, out_refs):
        if runs and runs[-1][1] < MXU_COLS:
            runs[-1][0].append(out)
            runs[-1][1] += out[0][1]
        else:
            runs.append([[out], out[0][1], lo])
        lo += out[0][1]
    for outs, run_width, run_lo in runs:
        y_run = _dot(h, w_ref[:, run_lo:run_lo + run_width])
        lo = 0
        for (name, width, dtype, scaled), o_ref in outs:
            y = y_run[:, lo:lo + width]
            if scaled:
                y = y * Q_SCALE
            if name == "gn":
                y = jax.nn.sigmoid(y)
            if name in _INPROJ_TRANSPOSED:
                y = y.T
            y = y.astype(dtype)
            if name == "ks":
                e = e_ref[...]
                y = _lane_cat([y[:, :HEAD_DIM], e, y[:, HEAD_DIM:], e])
            if name in _INPROJ_CHUNKED:
                rows_sc[...] = y
                for j in range(CMP_STRIDE):
                    o_ref[:, j * width:(j + 1) * width] = rows_sc[pl.ds(j, o_ref.shape[0], stride=CMP_STRIDE), :]
            else:
                o_ref[...] = y
            lo += width


def _inproj(x2, g, w, e_cols, tm=512):
    m = x2.shape[0]
    tiles_per_seq = e_cols.shape[0] // tm
    out_specs, out_shape = [], []
    for name, width, dtype, _ in _INPROJ_OUTS:
        if name in _INPROJ_TRANSPOSED:
            out_specs.append(pl.BlockSpec((width, tm), lambda i: (0, i)))
            out_shape.append(jax.ShapeDtypeStruct((width, m), dtype))
            continue
        rows, width = (CMP_STRIDE, CMP_STRIDE * width) if name in _INPROJ_CHUNKED else (1, _inproj_out_width(name, width))
        out_specs.append(pl.BlockSpec((tm // rows, width), lambda i: (i, 0)))
        out_shape.append(jax.ShapeDtypeStruct((m // rows, width), dtype))
    return pl.pallas_call(
        _inproj_kernel,
        grid=(m // tm,),
        in_specs=[pl.BlockSpec((tm, D_MODEL), lambda i: (i, 0)),
                  _const_spec((1, D_MODEL)),
                  _const_spec((D_MODEL, _INPROJ_W)),
                  pl.BlockSpec((tm, HEAD_DIM), lambda i: (i % tiles_per_seq, 0))],
        out_specs=out_specs,
        out_shape=out_shape,
        scratch_shapes=[pltpu.VMEM((tm, NSA_KV_W), F32)],
        compiler_params=_params(("parallel",)),
        name="inproj",
    )(x2, g, w, e_cols)


def _compress_kernel(rk_ref, rv_ref, pk_ref, pv_ref, w1k_ref, w1v_ref, w2k_ref, w2v_ref, kc_ref, vc_ref):
    nb = rk_ref.shape[0]

    def one(r_ref, p_ref, w1_ref, w2_ref):
        r = r_ref[...].reshape(nb * N_CMP_PAD, r_ref.shape[2])
        top = _dot((r + p_ref[0:1, :]).astype(BF16), w1_ref[0])
        bot = _dot((r + p_ref[1:2, :]).astype(BF16), w1_ref[1])
        hid = top + pltpu.roll(bot, nb * N_CMP_PAD - 1, 0)
        act = jax.nn.gelu(hid).astype(BF16)
        return jnp.concatenate(
            [_dot(act[:, g * CMP_HIDDEN:(g + 1) * CMP_HIDDEN], w2_ref[...]) for g in range(NSA_GROUPS)], axis=1)

    k_out = one(rk_ref, pk_ref, w1k_ref, w2k_ref)
    v_out = one(rv_ref, pv_ref, w1v_ref, w2v_ref)
    for n in range(nb):
        rows = slice(n * N_CMP_PAD, (n + 1) * N_CMP_PAD)
        kc_ref[n] = k_out[rows].astype(BF16)
        vc_ref[n] = v_out[rows].T.astype(BF16)


def _compress(rk, rv, pk, pv, w1k, w1v, w2k, w2v):
    b = rk.shape[0]
    rw = rk.shape[2]
    nb = 2 if b % 2 == 0 else 1
    r_spec = pl.BlockSpec((nb, N_CMP_PAD, rw), lambda i: (i, 0, 0))
    o_spec = pl.BlockSpec((nb, N_CMP_PAD, NSA_KV_W), lambda i: (i, 0, 0))
    return pl.pallas_call(
        _compress_kernel,
        grid=(b // nb,),
        in_specs=[r_spec, r_spec, _const_spec(pk.shape), _const_spec(pv.shape),
                  _const_spec(w1k.shape), _const_spec(w1v.shape),
                  _const_spec(w2k.shape), _const_spec(w2v.shape)],
        out_specs=[o_spec, o_spec],
        out_shape=[jax.ShapeDtypeStruct((b, N_CMP_PAD, NSA_KV_W), BF16)] * 2,
        compiler_params=_params(("parallel",)),
        name="compress",
    )(rk, rv, pk, pv, w1k, w1v, w2k, w2v)


def _compress_weights(pos, w1):
    half = CMP_LEN // 2
    p = pos.reshape(2, half, 1, HEAD_DIM)
    p = jnp.broadcast_to(p, (2, half, NSA_GROUPS, HEAD_DIM)).reshape(2, half * NSA_KV_W)
    w = w1.reshape(2, half, HEAD_DIM, CMP_HIDDEN)
    eye = jnp.eye(NSA_GROUPS, dtype=w1.dtype)
    wbd = jnp.einsum("ajdm,gk->ajgdkm", w, eye).reshape(2, half * NSA_KV_W, NSA_GROUPS * CMP_HIDDEN)
    return p.astype(F32), wbd.astype(BF16)


def _memkv_kernel(m_ref, g_ref, w_ref, k_ref, v_ref):
    h = _rms(m_ref[...], g_ref[...]).astype(BF16)
    k_ref[...] = _dot(h, w_ref[:, :MEM_W]).astype(BF16)
    v_ref[...] = _dot(h, w_ref[:, MEM_W:]).astype(BF16)


def _memkv(mem2, g, w, tm=512):
    m = mem2.shape[0]
    tm = min(tm, m)
    o_spec = pl.BlockSpec((tm, MEM_W), lambda i: (i, 0))
    return pl.pallas_call(
        _memkv_kernel,
        grid=(m // tm,),
        in_specs=[pl.BlockSpec((tm, D_MODEL), lambda i: (i, 0)), _const_spec((1, D_MODEL)),
                  _const_spec((D_MODEL, 2 * MEM_W))],
        out_specs=[o_spec, o_spec],
        out_shape=[jax.ShapeDtypeStruct((m, MEM_W), BF16)] * 2,
        compiler_params=_params(("parallel",)),
        name="memkv",
    )(mem2, g, w)


def _expand_kernel(idx_ref, bias_ref, o_ref, *, head0, n_heads, heads_per_group):
    rows, cols = idx_ref.shape

    def body(i, carry):
        r = pl.multiple_of(i * SUBLANES, SUBLANES)
        for c0 in range(0, cols, TILE):
            idx = idx_ref[pl.ds(r, SUBLANES), c0:c0 + TILE]
            out = [jnp.full(idx.shape, NEG_INF, F32)] * n_heads
            for bkt in range(REL_BUCKETS):
                hit = idx == bkt
                out = [jnp.where(hit, bias_ref[bkt, head0 + h], out[h]) for h in range(n_heads)]
            for h in range(n_heads):
                col = (h % heads_per_group) * cols + c0
                o_ref[h // heads_per_group, pl.ds(r, SUBLANES), col:col + TILE] = out[h] * LOG2E
        return carry

    lax.fori_loop(0, rows // SUBLANES, body, 0)


def _expand(idx, rel_bias, head0, n_heads, heads_per_group=1):
    rows, cols = idx.shape
    return pl.pallas_call(
        functools.partial(_expand_kernel, head0=head0, n_heads=n_heads, heads_per_group=heads_per_group),
        in_specs=[pl.BlockSpec(memory_space=pltpu.VMEM), pl.BlockSpec(memory_space=pltpu.SMEM)],
        out_specs=pl.BlockSpec(memory_space=pltpu.VMEM),
        out_shape=jax.ShapeDtypeStruct((n_heads // heads_per_group, rows, heads_per_group * cols), F32),
        compiler_params=pltpu.CompilerParams(vmem_limit_bytes=VMEM_LIMIT),
        name="bias_expand",
    )(idx, rel_bias)


def _t5_bucket_np(dist):
    dist = np.maximum(dist, 0)
    max_exact = REL_BUCKETS // 2
    logd = np.log(np.maximum(dist, 1).astype(np.float32) / max_exact) / math.log(REL_MAX_DIST / max_exact)
    large = np.minimum(max_exact + (logd * (REL_BUCKETS - max_exact)).astype(np.int32), REL_BUCKETS - 1)
    return np.where(dist < max_exact, dist, large).astype(np.int32)


def _bucket_tables(s):
    j = np.arange(TILE)[:, None]
    i = np.arange(TILE)[None, :]
    assert TILE + 1 >= REL_MAX_DIST
    tiles = []
    for d in range(2):
        dist = d * TILE + i - j
        tiles.append(np.where(dist >= 0, _t5_bucket_np(dist), MASKED_BUCKET))
    dist1 = TILE + i - j
    win = np.where(dist1 < WINDOW, _t5_bucket_np(dist1), MASKED_BUCKET)
    n_cmp = (s - CMP_LEN) // CMP_STRIDE + 1
    assert n_cmp * CMP_STRIDE + CMP_LEN - 1 >= s or n_cmp == N_CMP_PAD
    rel = np.arange(N_CMP_PAD + (s - TILE) // CMP_STRIDE)[:, None] - (s - TILE) // CMP_STRIDE
    dist_c = i - (rel * CMP_STRIDE + CMP_LEN - 1)
    cmp_idx = np.where(dist_c >= 0, _t5_bucket_np(dist_c), MASKED_BUCKET)
    as_i32 = lambda a: jnp.asarray(a.astype(np.int32))
    return as_i32(np.concatenate(tiles, axis=0)), as_i32(win), as_i32(cmp_idx)


def _overlap_table(s):
    n_cmp = (s - CMP_LEN) // CMP_STRIDE + 1
    n_sel = s // SEL_BLOCK
    cs = np.arange(n_cmp) * CMP_STRIDE
    ss = np.arange(n_sel) * SEL_BLOCK
    ov = np.clip(np.minimum(cs[:, None] + CMP_LEN, ss[None, :] + SEL_BLOCK)
                 - np.maximum(cs[:, None], ss[None, :]), 0, None).astype(np.float32) / CMP_LEN
    ovt = np.zeros((n_sel, N_CMP_PAD), np.float32)
    ovt[:, :n_cmp] = ov.T
    return jnp.asarray(ovt, BF16)


def _store_v_aug(vt_sc, idx, vt):
    ones = jnp.ones((BF16_ROWS, vt.shape[1]), BF16)
    vt_sc[idx] = jnp.concatenate([vt.astype(BF16), ones], axis=0)


def _lane_cat(xs):
    return jnp.concatenate(xs, axis=1)


def _query_halves(x):
    n = x.shape[-1] // TILE
    first = _lane_cat([x[:, k * TILE:k * TILE + HALF] for k in range(n)])
    second = _lane_cat([x[:, k * TILE + HALF:(k + 1) * TILE] for k in range(n)])
    return first, second


def _join_query_halves(first, second):
    n = first.shape[-1] // HALF
    return _lane_cat([part for k in range(n)
                      for part in (first[:, k * HALF:(k + 1) * HALF], second[:, k * HALF:(k + 1) * HALF])])


def _triangle_scores(k, q, table, causal):
    q_first, q_second = _query_halves(q)
    lo, hi = (0, HALF), (HALF, TILE)
    if causal:
        return _dot(k(*lo), q) + table(*lo), _dot(k(*hi), q_second) + _query_halves(table(*hi))[1]
    return _dot(k(*hi), q) + table(*hi), _dot(k(*lo), q_first) + _query_halves(table(*lo))[0]


def _triangle_softmax(scores, pv_lo, pv_hi, causal):
    s_wide, s_narrow = scores
    pv_wide, pv_narrow = (pv_lo, pv_hi) if causal else (pv_hi, pv_lo)
    mw_first, mw_second = _query_halves(jnp.max(s_wide, axis=0, keepdims=True))
    m_narrow = jnp.max(s_narrow, axis=0, keepdims=True)
    if causal:
        m_narrow = jnp.maximum(m_narrow, mw_second)
        m = _join_query_halves(mw_first, m_narrow)
    else:
        m_narrow = jnp.maximum(m_narrow, mw_first)
        m = _join_query_halves(m_narrow, mw_second)
    aw_first, aw_second = _query_halves(pv_wide(jnp.exp2(s_wide - m).astype(BF16)))
    a_narrow = pv_narrow(jnp.exp2(s_narrow - m_narrow).astype(BF16))
    if causal:
        return m, _join_query_halves(aw_first, aw_second + a_narrow)
    return m, _join_query_halves(aw_first + a_narrow, aw_second)


def _flash_pipelined(own, streams, s_sc, m_ref, acc_ref):
    has_prev = jnp.where(own > 0, 1.0, 0.0).astype(F32)
    prev = jnp.maximum(own - 1, 0)
    n_far = jnp.maximum(own - 1, 0)

    def absorb(g, s, kt, c_row, w_row):
        u = jnp.max(s, axis=0, keepdims=True) + c_row
        m_old = m_ref[g]
        m_new = jnp.maximum(m_old, jnp.where(w_row > 0.0, u, NEG_INF))
        alpha = jnp.exp2(m_old - m_new)
        shift = jnp.maximum(m_new, u) - c_row
        p = jnp.exp2(s - shift).astype(BF16)
        acc_ref[g] = alpha * acc_ref[g] + w_row * streams[g]["pv"](kt)(p)
        m_ref[g] = m_new

    def absorb_slot(g, i):
        is_prev = i == 0
        kt = jnp.where(is_prev, prev, i - 1)
        c_row = jnp.where(is_prev, 0.0, streams[g]["c_far"])
        w_row = streams[g]["w"](kt) * jnp.where(is_prev, has_prev, 1.0)
        absorb(g, s_sc[g], kt, c_row, w_row)

    def body(i, carry):
        for g in reversed(range(len(streams))):
            nxt = streams[g]["far"](i)
            absorb_slot(g, i)
            s_sc[g] = nxt
        return carry

    lax.fori_loop(0, n_far, body, 0)
    for g in range(len(streams)):
        absorb_slot(g, n_far)


def _run_ahead(jobs):
    depth = RUN_AHEAD
    pending = [job[0]() for job in jobs[:depth]]
    for k, (_, consume) in enumerate(jobs):
        if k + depth < len(jobs):
            pending.append(jobs[k + depth][0]())
        consume(pending.pop(0))


def _flash_start_jobs(g, stream, s_sc, m_ref, acc_ref):
    def init(scores):
        m_ref[g], acc_ref[g] = stream["own_softmax"](scores)

    def park(scores):
        s_sc[g] = scores

    return [(stream["own_scores"], init), (stream["prev"], park)]


def _softmax_av(s_list, pv_list):
    m = s_list[0].max(axis=0, keepdims=True)
    for s in s_list[1:]:
        m = jnp.maximum(m, s.max(axis=0, keepdims=True))
    acc = None
    for s, pv in zip(s_list, pv_list):
        part = pv(jnp.exp2(s - m).astype(BF16))
        acc = part if acc is None else acc + part
    return acc


def _normalize(acc):
    return acc[:HEAD_DIM] * (1.0 / acc[HEAD_DIM:HEAD_DIM + 1])


def _rank_before(score, n_cand):
    ranks = []
    for r0 in range(0, score.shape[0], SUBLANES):
        tile = score[r0:r0 + SUBLANES]
        blk = lax.broadcasted_iota(jnp.int32, tile.shape, 0) + r0
        rank = jnp.zeros(tile.shape, F32)
        for m in range(n_cand):
            row = score[m:m + 1, :]
            if m < r0:
                before = jnp.where(row >= tile, 1.0, 0.0)
            elif m >= r0 + SUBLANES:
                before = jnp.where(row > tile, 1.0, 0.0)
            else:
                before = jnp.where(blk > m, jnp.where(row >= tile, 1.0, 0.0), jnp.where(row > tile, 1.0, 0.0))
            rank = rank + before
        ranks.append(rank)
    return jnp.concatenate(ranks, axis=0)


def _nsa_steps(qi, maybe_first, q_ref, gn_ref, kc_ref, vct_ref, ks_ref, vs_ref, kw_ref, vw_ref,
               bct_ref, tt_ref, twt_ref, cfar_ref, ovt_ref, o_ref,
               vst_sc, vwt_sc, qa_sc, og_sc, acc_sc):
    nt = ks_ref.shape[1]
    n_sel = ovt_ref.shape[0]

    def transpose_v():
        for kt in range(nt):
            vs_t = vs_ref[0, kt].astype(F32).T
            vw_t = vw_ref[0, kt].astype(F32).T
            for g in range(NSA_GROUPS):
                _store_v_aug(vst_sc, (kt, g), vs_t[g * HEAD_DIM:(g + 1) * HEAD_DIM])
                _store_v_aug(vwt_sc, (kt, g), vw_t[g * HEAD_DIM:(g + 1) * HEAD_DIM])

    if maybe_first:
        pl.when(qi == 0)(transpose_v)

    pos = lax.broadcasted_iota(jnp.int32, (1, TILE), 1) + qi * TILE
    cur = pos // SEL_BLOCK
    has_cmp = pos >= CMP_LEN - 1
    blk = lax.broadcasted_iota(jnp.int32, (n_sel, TILE), 0)
    prev = jnp.maximum(qi - 1, 0)
    gates = gn_ref[...]

    gsls = [slice(g * HEAD_DIM, (g + 1) * HEAD_DIM) for g in range(NSA_GROUPS)]
    group_heads = [[g * NSA_HPG + j for j in range(NSA_HPG)] for g in range(NSA_GROUPS)]

    def gate(g, branch):
        return _lane_cat([gates[3 * h + branch:3 * h + branch + 1, :] for h in group_heads[g]])

    ones_row = jnp.ones((1, NSA_HPG * TILE), F32)

    def sel_stream(g):
        def qk(kt):
            return _dot(ks_ref[0, kt, :, g * 2 * HEAD_DIM:(g + 1) * 2 * HEAD_DIM], qa_sc[g])

        def own_scores():
            return _triangle_scores(
                lambda lo, hi: ks_ref[0, qi, lo:hi, g * 2 * HEAD_DIM:(g + 1) * 2 * HEAD_DIM], qa_sc[g],
                lambda lo, hi: tt_ref[g, 0, lo:hi, :], causal=True)

        def own_softmax(scores):
            return _triangle_softmax(
                scores, lambda pr: _dot(vst_sc[qi, g, :, 0:HALF], pr),
                lambda pr: _dot(vst_sc[qi, g, :, HALF:TILE], pr), causal=True)

        return dict(own_scores=own_scores, own_softmax=own_softmax, prev=lambda: qk(prev) + tt_ref[g, 1], far=qk,
                    c_far=cfar_ref[g], w=lambda kt: ones_row,
                    pv=lambda kt: (lambda pr: _dot(vst_sc[kt, g], pr)))

    def group_jobs(g):
        heads = group_heads[g]
        win = {}

        def v_half(kt, lo, hi):
            return lambda pr: _dot(vwt_sc[kt, g, :, lo:hi], pr)

        def window_own_scores():
            q4 = _lane_cat([q_ref[h * HEAD_DIM:(h + 1) * HEAD_DIM, :] for h in heads])
            qa_sc[g, 0:HEAD_DIM, :] = q4
            qa_sc[g, HEAD_DIM + n_sel:, :] = jnp.zeros((HEAD_DIM - n_sel, NSA_HPG * TILE), BF16)
            return _triangle_scores(lambda lo, hi: kw_ref[0, qi, lo:hi, gsls[g]], q4,
                                    lambda lo, hi: tt_ref[g, 0, lo:hi, :], causal=True)

        def window_own(scores):
            win["own"] = _triangle_softmax(scores, v_half(qi, 0, HALF), v_half(qi, HALF, TILE), causal=True)

        def window_prev_scores():
            return _triangle_scores(lambda lo, hi: kw_ref[0, prev, lo:hi, gsls[g]], qa_sc[g, 0:HEAD_DIM, :],
                                    lambda lo, hi: twt_ref[g, lo:hi, :], causal=False)

        def window_prev(scores):
            m_prev, acc_prev = _triangle_softmax(scores, v_half(prev, 0, HALF), v_half(prev, HALF, TILE), causal=False)
            m_prev = jnp.where(qi == 0, NEG_INF, m_prev)
            m_own, acc_own = win["own"]
            m_win = jnp.maximum(m_own, m_prev)
            acc_w = acc_own * jnp.exp2(m_own - m_win) + acc_prev * jnp.exp2(m_prev - m_win)
            og_sc[g] = gate(g, 2) * _normalize(acc_w)

        def compressed_scores():
            rows_per_tile = TILE // CMP_STRIDE
            c_rows = pl.ds(pl.multiple_of((nt - 1 - qi) * rows_per_tile, rows_per_tile), N_CMP_PAD)
            return (_dot(kc_ref[0, :, gsls[g]], qa_sc[g, 0:HEAD_DIM, :])
                    + _lane_cat([bct_ref[h, c_rows, :] for h in heads]))

        def compressed(s):
            e = jnp.exp2(s - jnp.max(s, axis=0, keepdims=True))
            p = e * jnp.where(_lane_cat([has_cmp] * NSA_HPG), 1.0 / jnp.sum(e, axis=0, keepdims=True), 0.0)
            psum = p[:, :TILE]
            for j in range(1, NSA_HPG):
                psum = psum + p[:, j * TILE:(j + 1) * TILE]
            og_sc[g] = og_sc[g] + gate(g, 0) * _dot(vct_ref[0, gsls[g], :], p.astype(BF16))

            p_hi, p_lo = _split_bf16(psum)
            imp = _dot(ovt_ref[...], p_hi) + _dot(ovt_ref[...], p_lo)
            forced = (blk == 0) | (blk == cur) | (blk == cur - 1)
            score = jnp.where(forced, FORCE_SCORE, jnp.where(blk <= cur, imp, NEG_INF))
            rank = _rank_before(score, n_sel)
            sel = jnp.where(rank < SEL_TOPN, jnp.where(score > NEG_INF / 2, 0.0, NEG_INF), NEG_INF)
            qa_sc[g, HEAD_DIM:HEAD_DIM + n_sel, :] = _lane_cat([sel.astype(BF16)] * NSA_HPG)

        return [(window_own_scores, window_own), (window_prev_scores, window_prev), (compressed_scores, compressed)]

    yield [job for g in range(NSA_GROUPS) for job in group_jobs(g)]
    yield [sel_stream(g) for g in range(NSA_GROUPS)]

    for g in range(NSA_GROUPS):
        o = og_sc[g] + gate(g, 1) * _normalize(acc_sc[g])
        for j, h in enumerate(group_heads[g]):
            o_ref[h * HEAD_DIM:(h + 1) * HEAD_DIM, :] = o[:, j * TILE:(j + 1) * TILE].astype(BF16)


def _moba_steps(c, maybe_first, qm_ref, km_ref, vm_ref, qx_ref, mk_ref, mv_ref, tt_ref, cfar_ref, om_ref, ox_ref,
                vmt_sc, mvt_sc, kmean_sc, qbd_sc, sel_sc, acc_sc, slot):
    nt = km_ref.shape[1]
    hsls = [slice(h * HEAD_DIM, (h + 1) * HEAD_DIM) for h in range(MOBA_HEADS)]

    def per_sequence():
        kmean_sc[...] = jnp.zeros(kmean_sc.shape, F32)
        for n in range(nt):
            kmean_sc[n:n + 1, :] = jnp.sum(km_ref[0, n].astype(F32), axis=0, keepdims=True) * (1.0 / MOBA_BLOCK)
            vt = vm_ref[0, n].astype(F32).T
            for h in range(MOBA_HEADS):
                _store_v_aug(vmt_sc, (n, h), vt[hsls[h]])
        mvt = mv_ref[0].astype(F32).T
        for h in range(MOBA_HEADS):
            _store_v_aug(mvt_sc, h, mvt[hsls[h]])

    if maybe_first:
        pl.when(c == 0)(per_sequence)

    row_head = lax.broadcasted_iota(jnp.int32, (MOBA_W, TILE), 0) // HEAD_DIM

    def block_diag(q_ref):
        q_t = q_ref[...].astype(F32)
        return _lane_cat([jnp.where(row_head == h, q_t, 0.0) for h in range(MOBA_HEADS)]).astype(BF16)

    def per_head_pv(vts):
        return lambda pr: _lane_cat([_dot(vts(h), pr[:, h * TILE:(h + 1) * TILE]) for h in range(MOBA_HEADS)])

    def store_heads(o_t, out_ref):
        for h in range(MOBA_HEADS):
            out_ref[hsls[h], :] = o_t[:, h * TILE:(h + 1) * TILE].astype(BF16)

    n_rows = -(-nt // SUBLANES) * SUBLANES

    def gate_scores():
        qbd = block_diag(qm_ref)
        qbd_sc[...] = qbd
        km_hi, km_lo = _split_bf16(kmean_sc[...])
        return (_dot(km_hi, qbd) + _dot(km_lo, qbd))[:n_rows]

    def select(gate):
        blk = lax.broadcasted_iota(jnp.int32, gate.shape, 0)
        score = jnp.where(blk < c, gate, NEG_INF * Q_SCALE)
        rank = _rank_before(score, nt)
        sel_sc[0:n_rows, :] = jnp.where(rank < MOBA_TOPK, jnp.where(score > NEG_INF * Q_SCALE / 2, 1.0, 0.0), 0.0)

    qk = lambda n: _dot(km_ref[0, n], qbd_sc[...])
    def own_pv(lo, hi):
        def pv(pr):
            width = pr.shape[1] // MOBA_HEADS
            return _lane_cat([_dot(vmt_sc[c, h, :, lo:hi], pr[:, h * width:(h + 1) * width])
                              for h in range(MOBA_HEADS)])
        return pv

    def own_scores():
        return _triangle_scores(lambda lo, hi: km_ref[0, c, lo:hi, :], qbd_sc[...],
                                lambda lo, hi: tt_ref[0, lo:hi, :], causal=True)

    def own_softmax(scores):
        return _triangle_softmax(scores, own_pv(0, HALF), own_pv(HALF, TILE), causal=True)

    stream = dict(own_scores=own_scores, own_softmax=own_softmax,
                  prev=lambda: qk(jnp.maximum(c - 1, 0)) + tt_ref[1], far=qk,
                  c_far=cfar_ref[...], w=lambda n: sel_sc[pl.ds(n, 1), :],
                  pv=lambda n: per_head_pv(lambda h: vmt_sc[n, h]))
    def memory(s):
        store_heads(_normalize(_softmax_av([s], [per_head_pv(lambda h: mvt_sc[h])])), ox_ref)

    yield [(gate_scores, select), (lambda: _dot(mk_ref[0], block_diag(qx_ref)), memory)]
    yield [stream]
    store_heads(_normalize(acc_sc[slot]), om_ref)


N_STREAMS = NSA_GROUPS + 1
TILES_PER_STEP = 4
NSA_ROW_INPUTS = (0, 1)
MOBA_ROW_INPUTS = (0, 3)


def _attention_kernel(*refs, n_nsa_in, n_moba_in, n_nsa_scratch, n_moba_scratch):
    nsa_in, refs = refs[:n_nsa_in], refs[n_nsa_in:]
    moba_in, refs = refs[:n_moba_in], refs[n_moba_in:]
    outs, refs = refs[:3], refs[3:]
    nsa_sc, refs = refs[:n_nsa_scratch], refs[n_nsa_scratch:]
    moba_sc, refs = refs[:n_moba_scratch], refs[n_moba_scratch:]
    s_sc, m_sc, acc_sc = refs
    for t in range(TILES_PER_STEP):
        qi = pl.program_id(1) * TILES_PER_STEP + t
        cols = lambda ref: ref.at[:, t * TILE:(t + 1) * TILE]
        o_nsa, o_moba, o_mem = [cols(o) for o in outs]
        nsa_refs = [cols(r) if k in NSA_ROW_INPUTS else r for k, r in enumerate(nsa_in)]
        moba_refs = [cols(r) if k in MOBA_ROW_INPUTS else r for k, r in enumerate(moba_in)]
        nsa = _nsa_steps(qi, t == 0, *nsa_refs, o_nsa, *nsa_sc, acc_sc)
        moba = _moba_steps(qi, t == 0, *moba_refs, o_moba, o_mem, *moba_sc, acc_sc, NSA_GROUPS)
        jobs = next(nsa) + next(moba)
        streams = next(nsa) + next(moba)
        jobs += [job for g, stream in enumerate(streams) for job in _flash_start_jobs(g, stream, s_sc, m_sc, acc_sc)]
        _run_ahead(jobs)
        _flash_pipelined(qi, streams, s_sc, m_sc, acc_sc)
        for steps in (nsa, moba):
            for _ in steps:
                pass


def _attention(b, s, qn, gn, kc, vct, ks, vs, kw, vw, bias_cmp, t_nsa, t_win, c_far_nsa, ovt,
               qm, km, vm, qx, mk, mv, t_moba, c_far_moba):
    nt = s // TILE
    mem_len = mk.shape[0] // b
    assert MOBA_TOPK <= nt - 1 and nt <= BF16_ROWS
    n_lanes = NSA_HPG * TILE
    assert MOBA_HEADS * TILE == n_lanes
    assert nt % TILES_PER_STEP == 0
    steps = nt // TILES_PER_STEP
    row_spec = lambda w: pl.BlockSpec((w, TILES_PER_STEP * TILE), lambda i, j: (0, i * steps + j))
    seq_spec = lambda w: pl.BlockSpec((1, nt, TILE, w), lambda i, j: (i, 0, 0, 0))
    per_batch = lambda rows, w: pl.BlockSpec((1, rows, w), lambda i, j: (i, 0, 0))
    tiles = lambda a: a.reshape(b, nt, TILE, a.shape[-1])
    nsa_in = [(qn, row_spec(NSA_Q_W)), (gn, row_spec(GATE_PAD)),
              (kc, per_batch(N_CMP_PAD, NSA_KV_W)), (vct, per_batch(N_CMP_PAD, NSA_KV_W)),
              (tiles(ks), seq_spec(KS_AUG_W)), (tiles(vs), seq_spec(NSA_KV_W)),
              (tiles(kw), seq_spec(NSA_KV_W)), (tiles(vw), seq_spec(NSA_KV_W)),
              (bias_cmp, _const_spec(bias_cmp.shape)), (t_nsa, _const_spec(t_nsa.shape)),
              (t_win, _const_spec(t_win.shape)), (c_far_nsa, _const_spec(c_far_nsa.shape)),
              (ovt, _const_spec(ovt.shape))]
    moba_in = [(qm, row_spec(MOBA_W)), (tiles(km), seq_spec(MOBA_W)), (tiles(vm), seq_spec(MOBA_W)),
               (qx, row_spec(MEM_W)),
               (mk.reshape(b, mem_len, MEM_W), per_batch(mem_len, MEM_W)),
               (mv.reshape(b, mem_len, MEM_W), per_batch(mem_len, MEM_W)),
               (t_moba, _const_spec(t_moba.shape)), (c_far_moba, _const_spec(c_far_moba.shape))]
    nsa_scratch =[pltpu.VMEM((nt, NSA_GROUPS, V_AUG, TILE), BF16),
                   pltpu.VMEM((nt, NSA_GROUPS, V_AUG, TILE), BF16),
                   pltpu.VMEM((NSA_GROUPS, 2 * HEAD_DIM, n_lanes), BF16),
                   pltpu.VMEM((NSA_GROUPS, HEAD_DIM, n_lanes), F32)]
    moba_scratch = [pltpu.VMEM((nt, MOBA_HEADS, V_AUG, TILE), BF16),
                    pltpu.VMEM((MEM_HEADS, V_AUG, mem_len), BF16),
                    pltpu.VMEM((BF16_ROWS, MOBA_W), F32),
                    pltpu.VMEM((MOBA_W, n_lanes), BF16),
                    pltpu.VMEM((BF16_ROWS, n_lanes), F32)]
    shared_scratch =[pltpu.VMEM((N_STREAMS, TILE, n_lanes), F32),
                      pltpu.VMEM((N_STREAMS, 1, n_lanes), F32),
                      pltpu.VMEM((N_STREAMS, V_AUG, n_lanes), F32)]
    inputs = nsa_in + moba_in
    return pl.pallas_call(
        functools.partial(_attention_kernel, n_nsa_in=len(nsa_in), n_moba_in=len(moba_in),
                          n_nsa_scratch=len(nsa_scratch), n_moba_scratch=len(moba_scratch)),
        grid=(b, steps),
        in_specs=[spec for _, spec in inputs],
        out_specs=[row_spec(NSA_Q_W), row_spec(MOBA_W), row_spec(MEM_W)],
        out_shape=[jax.ShapeDtypeStruct((w, b * s), BF16) for w in (NSA_Q_W, MOBA_W, MEM_W)],
        scratch_shapes=nsa_scratch + moba_scratch + shared_scratch,
        compiler_params=_params(("arbitrary", "arbitrary")),
        name="attention",
    )(*[a for a, _ in inputs])


def _mix_kernel(x_ref, on_ref, om_ref, ox_ref, g_pre_ref, g_post_ref, wg_ref, wn_ref, wm_ref, wx_ref,
                wo_ref, o_ref):
    x = x_ref[...]
    h = _rms(x, g_pre_ref[...]).astype(BF16)
    rows = lambda o_ref: o_ref[...].astype(F32).T.astype(BF16)
    merged = jax.nn.sigmoid(_dot(h, wg_ref[:, :D_MODEL])) * _dot(rows(on_ref), wn_ref[...])
    merged = merged + jax.nn.sigmoid(_dot(h, wg_ref[:, D_MODEL:2 * D_MODEL])) * _dot(rows(om_ref), wm_ref[...])
    merged = merged + jax.nn.sigmoid(_dot(h, wg_ref[:, 2 * D_MODEL:])) * _dot(rows(ox_ref), wx_ref[...])
    y = _dot(merged.astype(BF16), wo_ref[...])
    o_ref[...] = x + _rms(y, g_post_ref[...])


def _mix(x2, o_nsa, o_moba, o_mem, g_pre, g_post, w_gates, w_nsa_o, w_moba_o, w_mem_o, w_mix_out, tm=512):
    m = x2.shape[0]
    row = lambda w: pl.BlockSpec((tm, w), lambda i: (i, 0))
    col = lambda w: pl.BlockSpec((w, tm), lambda i: (0, i))
    return pl.pallas_call(
        _mix_kernel,
        grid=(m // tm,),
        in_specs=[row(D_MODEL), col(NSA_Q_W), col(MOBA_W), col(MEM_W),
                  _const_spec((1, D_MODEL)), _const_spec((1, D_MODEL)),
                  _const_spec(w_gates.shape), _const_spec(w_nsa_o.shape), _const_spec(w_moba_o.shape),
                  _const_spec(w_mem_o.shape), _const_spec(w_mix_out.shape)],
        out_specs=row(D_MODEL),
        out_shape=jax.ShapeDtypeStruct((m, D_MODEL), F32),
        compiler_params=_params(("parallel",)),
        name="mix",
    )(x2, o_nsa, o_moba, o_mem, g_pre, g_post, w_gates, w_nsa_o, w_moba_o, w_mem_o, w_mix_out)


FFN_CHUNK = 256


def _ffn_kernel(x_ref, g_pre_ref, g_post_ref, wg_ref, wu_ref, wd_ref, o_ref, a_sc):
    x = x_ref[...]
    h = _rms(x, g_pre_ref[...]).astype(BF16)
    d_ff = wg_ref.shape[1]
    for j in range(d_ff // FFN_CHUNK):
        sl = slice(j * FFN_CHUNK, (j + 1) * FFN_CHUNK)
        a_sc[:, sl] = (jax.nn.silu(_dot(h, wg_ref[:, sl])) * _dot(h, wu_ref[:, sl])).astype(BF16)
    f = _dot(a_sc[...], wd_ref[...])
    o_ref[...] = x + _rms(f, g_post_ref[...])


def _ffn(x2, g_pre, g_post, wg, wu, wd, tm=512):
    m = x2.shape[0]
    d_ff = wg.shape[1]
    return pl.pallas_call(
        _ffn_kernel,
        grid=(m // tm,),
        in_specs=[pl.BlockSpec((tm, D_MODEL), lambda i: (i, 0)),
                  _const_spec((1, D_MODEL)), _const_spec((1, D_MODEL)),
                  _const_spec(wg.shape), _const_spec(wu.shape), _const_spec(wd.shape)],
        out_specs=pl.BlockSpec((tm, D_MODEL), lambda i: (i, 0)),
        out_shape=jax.ShapeDtypeStruct((m, D_MODEL), F32),
        scratch_shapes=[pltpu.VMEM((tm, d_ff), BF16)],
        compiler_params=_params(("parallel",)),
        name="ffn",
    )(x2, g_pre, g_post, wg, wu, wd)


def kernel(x, mem, rel_bias, pre_mix_g, mem_norm_g, post_mix_g, w_in, cmp_pos_k, cmp_w1_k, cmp_w2_k, cmp_pos_v, cmp_w1_v, cmp_w2_v, w_mem_kv, w_nsa_o, w_moba_o, w_mem_o, w_mix_out, pre_ffn_g, post_ffn_g, w_ffn_gate, w_ffn_up, w_ffn_down):
    b, s, d_model = x.shape
    depth = w_in.shape[0]
    assert d_model == D_MODEL and s % TILE == 0 and TILE == MOBA_BLOCK == WINDOW
    assert (s - CMP_LEN) // CMP_STRIDE + 1 < N_CMP_PAD and (s // SEL_BLOCK) % SUBLANES == 0 and s // SEL_BLOCK <= HEAD_DIM
    assert w_in.shape[2] == ATT_W + 3 * D_MODEL and rel_bias.shape == (REL_BUCKETS, N_BIAS_HEADS)

    tile_idx, win_idx, cmp_idx = _bucket_tables(s)
    rel_bias = rel_bias.astype(F32)
    t_nsa = _expand(tile_idx, rel_bias, 0, NSA_HEADS, NSA_HPG).reshape(NSA_GROUPS, 2, TILE, NSA_HPG * TILE)
    t_moba = _expand(tile_idx, rel_bias, NSA_HEADS, MOBA_HEADS, MOBA_HEADS).reshape(2, TILE, MOBA_HEADS * TILE)
    t_win = _expand(win_idx, rel_bias, 0, NSA_HEADS, NSA_HPG)
    b_cmp = _expand(cmp_idx, rel_bias, 0, NSA_HEADS)
    c_far = jnp.repeat(rel_bias[REL_BUCKETS - 1] * LOG2E, TILE)
    c_far_nsa = c_far[:NSA_HEADS * TILE].reshape(NSA_GROUPS, 1, NSA_HPG * TILE)
    c_far_moba = c_far[NSA_HEADS * TILE:].reshape(1, MOBA_HEADS * TILE)
    ovt = _overlap_table(s)
    sel_cols = np.zeros((s, HEAD_DIM), np.float32)
    sel_cols[np.arange(s), np.arange(s) // SEL_BLOCK] = 1.0
    sel_cols = jnp.asarray(sel_cols, BF16)
    gate_lo = NSA_Q_W + 6 * NSA_KV_W
    rows_per_chunk = CMP_STRIDE * NSA_KV_W

    x2 = x.reshape(b * s, D_MODEL)
    mem2 = mem.reshape(-1, D_MODEL)
    for l in range(depth):
        w_att = jnp.concatenate(
            [w_in[l, :, :gate_lo + NSA_GATE_W],
             jnp.zeros((D_MODEL, GATE_PAD - NSA_GATE_W), w_in.dtype),
             w_in[l, :, gate_lo + NSA_GATE_W:ATT_W]], axis=1).astype(BF16)
        w_gates = w_in[l, :, ATT_W:].astype(BF16)
        row = lambda v: v[l].reshape(1, D_MODEL)

        qn, kc_raw, vc_raw, ks, vs, kw, vw, gn, qm, km, vm, qx = _inproj(x2, row(pre_mix_g), w_att, sel_cols)

        pk, w1k = _compress_weights(cmp_pos_k[l], cmp_w1_k[l])
        pv, w1v = _compress_weights(cmp_pos_v[l], cmp_w1_v[l])
        kc, vct = _compress(kc_raw.reshape(b, s // CMP_STRIDE, rows_per_chunk),
                            vc_raw.reshape(b, s // CMP_STRIDE, rows_per_chunk),
                            pk, pv, w1k, w1v, cmp_w2_k[l].astype(BF16), cmp_w2_v[l].astype(BF16))

        mk, mv = _memkv(mem2, row(mem_norm_g), w_mem_kv[l].astype(BF16))

        o_nsa, o_moba, o_mem = _attention(b, s, qn, gn, kc, vct, ks, vs, kw, vw, b_cmp, t_nsa, t_win, c_far_nsa, ovt,
                                          qm, km, vm, qx, mk, mv, t_moba, c_far_moba)

        x2 = _mix(x2, o_nsa, o_moba, o_mem, row(pre_mix_g), row(post_mix_g), w_gates,
                  w_nsa_o[l].astype(BF16), w_moba_o[l].astype(BF16), w_mem_o[l].astype(BF16),
                  w_mix_out[l].astype(BF16))
        x2 = _ffn(x2, row(pre_ffn_g), row(post_ffn_g), w_ffn_gate[l].astype(BF16),
                  w_ffn_up[l].astype(BF16), w_ffn_down[l].astype(BF16))
    return x2.reshape(b, s, D_MODEL)
```

```python
import functools
import math

import numpy as np
import jax
import jax.numpy as jnp
from jax import lax
from jax.experimental import pallas as pl
from jax.experimental.pallas import tpu as pltpu

F32 = jnp.float32
BF16 = jnp.bfloat16

D_MODEL = 1024
HEAD_DIM = 64
SCALE = HEAD_DIM ** -0.5
LOG2E = math.log2(math.e)
Q_SCALE = SCALE * LOG2E
NSA_HEADS = 8
NSA_GROUPS = 2
NSA_HPG = NSA_HEADS // NSA_GROUPS
CMP_LEN = 32
CMP_STRIDE = 16
CMP_HIDDEN = 128
SEL_BLOCK = 64
SEL_TOPN = 8
WINDOW = 256
MOBA_HEADS = 4
MOBA_BLOCK = 256
MOBA_TOPK = 3
MEM_HEADS = 4
REL_BUCKETS = 32
REL_MAX_DIST = 128
N_BIAS_HEADS = NSA_HEADS + MOBA_HEADS
RMS_EPS = 1e-6
NEG_INF = -1e30
FORCE_SCORE = 1e4

NSA_Q_W = NSA_HEADS * HEAD_DIM
NSA_KV_W = NSA_GROUPS * HEAD_DIM
NSA_GATE_W = NSA_HEADS * 3
MOBA_W = MOBA_HEADS * HEAD_DIM
MEM_W = MEM_HEADS * HEAD_DIM
ATT_W = NSA_Q_W + 6 * NSA_KV_W + NSA_GATE_W + 3 * MOBA_W + MEM_W
LANES = 128
SUBLANES = 8
BF16_ROWS = 16
MXU_COLS = 256
GATE_PAD = LANES
TILE = 256
HALF = TILE // 2
RUN_AHEAD = 2
N_CMP_PAD = 128
V_AUG = HEAD_DIM + BF16_ROWS
MASKED_BUCKET = REL_BUCKETS
VMEM_LIMIT = 56 * 1024 * 1024


def _dot(a, b):
    return jnp.dot(a, b, preferred_element_type=F32)


def _split_bf16(x):
    hi = x.astype(BF16)
    lo = (x - hi.astype(F32)).astype(BF16)
    return hi, lo


def _rms(x, g):
    return x * lax.rsqrt(jnp.mean(x * x, axis=-1, keepdims=True) + RMS_EPS) * g


def _params(sem):
    return pltpu.CompilerParams(dimension_semantics=sem, vmem_limit_bytes=VMEM_LIMIT)


def _const_spec(shape):
    nd = len(shape)
    return pl.BlockSpec(shape, lambda *_: (0,) * nd, pipeline_mode=pl.Buffered(1))


_INPROJ_OUTS = (
    ("qn", NSA_Q_W, BF16, True),
    ("kc", NSA_KV_W, F32, False), ("vc", NSA_KV_W, F32, False),
    ("ks", NSA_KV_W, BF16, False), ("vs", NSA_KV_W, BF16, False),
    ("kw", NSA_KV_W, BF16, False), ("vw", NSA_KV_W, BF16, False),
    ("gn", GATE_PAD, F32, False),
    ("qm", MOBA_W, BF16, True), ("km", MOBA_W, BF16, False), ("vm", MOBA_W, BF16, False),
    ("qx", MEM_W, BF16, True),
)
_INPROJ_W = sum(o[1] for o in _INPROJ_OUTS)
_INPROJ_CHUNKED = ("kc", "vc")
_INPROJ_TRANSPOSED = ("qn", "gn", "qm", "qx")
KS_AUG_W = NSA_GROUPS * 2 * HEAD_DIM


def _inproj_out_width(name, width):
    return KS_AUG_W if name == "ks" else width


def _inproj_kernel(x_ref, g_ref, w_ref, e_ref, *refs):
    out_refs, rows_sc = refs[:-1], refs[-1]
    h = _rms(x_ref[...], g_ref[...]).astype(BF16)
    runs, lo = [], 0
    for out in zip(_INPROJ_OUTS, out_refs):
        if runs and runs[-1][1] < MXU_COLS:
            runs[-1][0].append(out)
            runs[-1][1] += out[0][1]
        else:
            runs.append([[out], out[0][1], lo])
        lo += out[0][1]
    for outs, run_width, run_lo in runs:
        y_run = _dot(h, w_ref[:, run_lo:run_lo + run_width])
        lo = 0
        for (name, width, dtype, scaled), o_ref in outs:
            y = y_run[:, lo:lo + width]
            if scaled:
                y = y * Q_SCALE
            if name == "gn":
                y = jax.nn.sigmoid(y)
            if name in _INPROJ_TRANSPOSED:
                y = y.T
            y = y.astype(dtype)
            if name == "ks":
                e = e_ref[...]
                y = _lane_cat([y[:, :HEAD_DIM], e, y[:, HEAD_DIM:], e])
            if name in _INPROJ_CHUNKED:
                rows_sc[...] = y
                for j in range(CMP_STRIDE):
                    o_ref[:, j * width:(j + 1) * width] = rows_sc[pl.ds(j, o_ref.shape[0], stride=CMP_STRIDE), :]
            else:
                o_ref[...] = y
            lo += width


def _inproj(x2, g, w, e_cols, tm=512):
    m = x2.shape[0]
    tiles_per_seq = e_cols.shape[0] // tm
    out_specs, out_shape = [], []
    for name, width, dtype, _ in _INPROJ_OUTS:
        if name in _INPROJ_TRANSPOSED:
            out_specs.append(pl.BlockSpec((width, tm), lambda i: (0, i)))
            out_shape.append(jax.ShapeDtypeStruct((width, m), dtype))
            continue
        rows, width = (CMP_STRIDE, CMP_STRIDE * width) if name in _INPROJ_CHUNKED else (1, _inproj_out_width(name, width))
        out_specs.append(pl.BlockSpec((tm // rows, width), lambda i: (i, 0)))
        out_shape.append(jax.ShapeDtypeStruct((m // rows, width), dtype))
    return pl.pallas_call(
        _inproj_kernel,
        grid=(m // tm,),
        in_specs=[pl.BlockSpec((tm, D_MODEL), lambda i: (i, 0)),
                  _const_spec((1, D_MODEL)),
                  _const_spec((D_MODEL, _INPROJ_W)),
                  pl.BlockSpec((tm, HEAD_DIM), lambda i: (i % tiles_per_seq, 0))],
        out_specs=out_specs,
        out_shape=out_shape,
        scratch_shapes=[pltpu.VMEM((tm, NSA_KV_W), F32)],
        compiler_params=_params(("parallel",)),
        name="inproj",
    )(x2, g, w, e_cols)


def _compress_kernel(rk_ref, rv_ref, pk_ref, pv_ref, w1k_ref, w1v_ref, w2k_ref, w2v_ref, kc_ref, vc_ref):
    nb = rk_ref.shape[0]

    def one(r_ref, p_ref, w1_ref, w2_ref):
        r = r_ref[...].reshape(nb * N_CMP_PAD, r_ref.shape[2])
        top = _dot((r + p_ref[0:1, :]).astype(BF16), w1_ref[0])
        bot = _dot((r + p_ref[1:2, :]).astype(BF16), w1_ref[1])
        hid = top + pltpu.roll(bot, nb * N_CMP_PAD - 1, 0)
        act = jax.nn.gelu(hid).astype(BF16)
        return jnp.concatenate(
            [_dot(act[:, g * CMP_HIDDEN:(g + 1) * CMP_HIDDEN], w2_ref[...]) for g in range(NSA_GROUPS)], axis=1)

    k_out = one(rk_ref, pk_ref, w1k_ref, w2k_ref)
    v_out = one(rv_ref, pv_ref, w1v_ref, w2v_ref)
    for n in range(nb):
        rows = slice(n * N_CMP_PAD, (n + 1) * N_CMP_PAD)
        kc_ref[n] = k_out[rows].astype(BF16)
        vc_ref[n] = v_out[rows].T.astype(BF16)


def _compress(rk, rv, pk, pv, w1k, w1v, w2k, w2v):
    b = rk.shape[0]
    rw = rk.shape[2]
    nb = 2 if b % 2 == 0 else 1
    r_spec = pl.BlockSpec((nb, N_CMP_PAD, rw), lambda i: (i, 0, 0))
    o_spec = pl.BlockSpec((nb, N_CMP_PAD, NSA_KV_W), lambda i: (i, 0, 0))
    return pl.pallas_call(
        _compress_kernel,
        grid=(b // nb,),
        in_specs=[r_spec, r_spec, _const_spec(pk.shape), _const_spec(pv.shape),
                  _const_spec(w1k.shape), _const_spec(w1v.shape),
                  _const_spec(w2k.shape), _const_spec(w2v.shape)],
        out_specs=[o_spec, o_spec],
        out_shape=[jax.ShapeDtypeStruct((b, N_CMP_PAD, NSA_KV_W), BF16)] * 2,
        compiler_params=_params(("parallel",)),
        name="compress",
    )(rk, rv, pk, pv, w1k, w1v, w2k, w2v)


def _compress_weights(pos, w1):
    half = CMP_LEN // 2
    p = pos.reshape(2, half, 1, HEAD_DIM)
    p = jnp.broadcast_to(p, (2, half, NSA_GROUPS, HEAD_DIM)).reshape(2, half * NSA_KV_W)
    w = w1.reshape(2, half, HEAD_DIM, CMP_HIDDEN)
    eye = jnp.eye(NSA_GROUPS, dtype=w1.dtype)
    wbd = jnp.einsum("ajdm,gk->ajgdkm", w, eye).reshape(2, half * NSA_KV_W, NSA_GROUPS * CMP_HIDDEN)
    return p.astype(F32), wbd.astype(BF16)


def _memkv_kernel(m_ref, g_ref, w_ref, k_ref, v_ref):
    h = _rms(m_ref[...], g_ref[...]).astype(BF16)
    k_ref[...] = _dot(h, w_ref[:, :MEM_W]).astype(BF16)
    v_ref[...] = _dot(h, w_ref[:, MEM_W:]).astype(BF16)


def _memkv(mem2, g, w, tm=512):
    m = mem2.shape[0]
    tm = min(tm, m)
    o_spec = pl.BlockSpec((tm, MEM_W), lambda i: (i, 0))
    return pl.pallas_call(
        _memkv_kernel,
        grid=(m // tm,),
        in_specs=[pl.BlockSpec((tm, D_MODEL), lambda i: (i, 0)), _const_spec((1, D_MODEL)),
                  _const_spec((D_MODEL, 2 * MEM_W))],
        out_specs=[o_spec, o_spec],
        out_shape=[jax.ShapeDtypeStruct((m, MEM_W), BF16)] * 2,
        compiler_params=_params(("parallel",)),
        name="memkv",
    )(mem2, g, w)


def _expand_kernel(idx_ref, bias_ref, o_ref, *, head0, n_heads, heads_per_group):
    rows, cols = idx_ref.shape

    def body(i, carry):
        r = pl.multiple_of(i * SUBLANES, SUBLANES)
        for c0 in range(0, cols, TILE):
            idx = idx_ref[pl.ds(r, SUBLANES), c0:c0 + TILE]
            out = [jnp.full(idx.shape, NEG_INF, F32)] * n_heads
            for bkt in range(REL_BUCKETS):
                hit = idx == bkt
                out = [jnp.where(hit, bias_ref[bkt, head0 + h], out[h]) for h in range(n_heads)]
            for h in range(n_heads):
                col = (h % heads_per_group) * cols + c0
                o_ref[h // heads_per_group, pl.ds(r, SUBLANES), col:col + TILE] = out[h] * LOG2E
        return carry

    lax.fori_loop(0, rows // SUBLANES, body, 0)


def _expand(idx, rel_bias, head0, n_heads, heads_per_group=1):
    rows, cols = idx.shape
    return pl.pallas_call(
        functools.partial(_expand_kernel, head0=head0, n_heads=n_heads, heads_per_group=heads_per_group),
        in_specs=[pl.BlockSpec(memory_space=pltpu.VMEM), pl.BlockSpec(memory_space=pltpu.SMEM)],
        out_specs=pl.BlockSpec(memory_space=pltpu.VMEM),
        out_shape=jax.ShapeDtypeStruct((n_heads // heads_per_group, rows, heads_per_group * cols), F32),
        compiler_params=pltpu.CompilerParams(vmem_limit_bytes=VMEM_LIMIT),
        name="bias_expand",
    )(idx, rel_bias)


def _t5_bucket_np(dist):
    dist = np.maximum(dist, 0)
    max_exact = REL_BUCKETS // 2
    logd = np.log(np.maximum(dist, 1).astype(np.float32) / max_exact) / math.log(REL_MAX_DIST / max_exact)
    large = np.minimum(max_exact + (logd * (REL_BUCKETS - max_exact)).astype(np.int32), REL_BUCKETS - 1)
    return np.where(dist < max_exact, dist, large).astype(np.int32)


def _bucket_tables(s):
    j = np.arange(TILE)[:, None]
    i = np.arange(TILE)[None, :]
    assert TILE + 1 >= REL_MAX_DIST
    tiles = []
    for d in range(2):
        dist = d * TILE + i - j
        tiles.append(np.where(dist >= 0, _t5_bucket_np(dist), MASKED_BUCKET))
    dist1 = TILE + i - j
    win = np.where(dist1 < WINDOW, _t5_bucket_np(dist1), MASKED_BUCKET)
    n_cmp = (s - CMP_LEN) // CMP_STRIDE + 1
    assert n_cmp * CMP_STRIDE + CMP_LEN - 1 >= s or n_cmp == N_CMP_PAD
    rel = np.arange(N_CMP_PAD + (s - TILE) // CMP_STRIDE)[:, None] - (s - TILE) // CMP_STRIDE
    dist_c = i - (rel * CMP_STRIDE + CMP_LEN - 1)
    cmp_idx = np.where(dist_c >= 0, _t5_bucket_np(dist_c), MASKED_BUCKET)
    as_i32 = lambda a: jnp.asarray(a.astype(np.int32))
    return as_i32(np.concatenate(tiles, axis=0)), as_i32(win), as_i32(cmp_idx)


def _overlap_table(s):
    n_cmp = (s - CMP_LEN) // CMP_STRIDE + 1
    n_sel = s // SEL_BLOCK
    cs = np.arange(n_cmp) * CMP_STRIDE
    ss = np.arange(n_sel) * SEL_BLOCK
    ov = np.clip(np.minimum(cs[:, None] + CMP_LEN, ss[None, :] + SEL_BLOCK)
                 - np.maximum(cs[:, None], ss[None, :]), 0, None).astype(np.float32) / CMP_LEN
    ovt = np.zeros((n_sel, N_CMP_PAD), np.float32)
    ovt[:, :n_cmp] = ov.T
    return jnp.asarray(ovt, BF16)


def _store_v_aug(vt_sc, idx, vt):
    ones = jnp.ones((BF16_ROWS, vt.shape[1]), BF16)
    vt_sc[idx] = jnp.concatenate([vt.astype(BF16), ones], axis=0)


def _lane_cat(xs):
    return jnp.concatenate(xs, axis=1)


def _query_halves(x):
    n = x.shape[-1] // TILE
    first = _lane_cat([x[:, k * TILE:k * TILE + HALF] for k in range(n)])
    second = _lane_cat([x[:, k * TILE + HALF:(k + 1) * TILE] for k in range(n)])
    return first, second


def _join_query_halves(first, second):
    n = first.shape[-1] // HALF
    return _lane_cat([part for k in range(n)
                      for part in (first[:, k * HALF:(k + 1) * HALF], second[:, k * HALF:(k + 1) * HALF])])


def _triangle_scores(k, q, table, causal):
    q_first, q_second = _query_halves(q)
    lo, hi = (0, HALF), (HALF, TILE)
    if causal:
        return _dot(k(*lo), q) + table(*lo), _dot(k(*hi), q_second) + _query_halves(table(*hi))[1]
    return _dot(k(*hi), q) + table(*hi), _dot(k(*lo), q_first) + _query_halves(table(*lo))[0]


def _triangle_softmax(scores, pv_lo, pv_hi, causal):
    s_wide, s_narrow = scores
    pv_wide, pv_narrow = (pv_lo, pv_hi) if causal else (pv_hi, pv_lo)
    mw_first, mw_second = _query_halves(jnp.max(s_wide, axis=0, keepdims=True))
    m_narrow = jnp.max(s_narrow, axis=0, keepdims=True)
    if causal:
        m_narrow = jnp.maximum(m_narrow, mw_second)
        m = _join_query_halves(mw_first, m_narrow)
    else:
        m_narrow = jnp.maximum(m_narrow, mw_first)
        m = _join_query_halves(m_narrow, mw_second)
    aw_first, aw_second = _query_halves(pv_wide(jnp.exp2(s_wide - m).astype(BF16)))
    a_narrow = pv_narrow(jnp.exp2(s_narrow - m_narrow).astype(BF16))
    if causal:
        return m, _join_query_halves(aw_first, aw_second + a_narrow)
    return m, _join_query_halves(aw_first + a_narrow, aw_second)


def _flash_pipelined(own, streams, s_sc, m_ref, acc_ref):
    has_prev = jnp.where(own > 0, 1.0, 0.0).astype(F32)
    prev = jnp.maximum(own - 1, 0)
    n_far = jnp.maximum(own - 1, 0)

    def absorb(g, s, kt, c_row, w_row):
        u = jnp.max(s, axis=0, keepdims=True) + c_row
        m_old = m_ref[g]
        m_new = jnp.maximum(m_old, jnp.where(w_row > 0.0, u, NEG_INF))
        alpha = jnp.exp2(m_old - m_new)
        shift = jnp.maximum(m_new, u) - c_row
        p = jnp.exp2(s - shift).astype(BF16)
        acc_ref[g] = alpha * acc_ref[g] + w_row * streams[g]["pv"](kt)(p)
        m_ref[g] = m_new

    def absorb_slot(g, i):
        is_prev = i == 0
        kt = jnp.where(is_prev, prev, i - 1)
        c_row = jnp.where(is_prev, 0.0, streams[g]["c_far"])
        w_row = streams[g]["w"](kt) * jnp.where(is_prev, has_prev, 1.0)
        absorb(g, s_sc[g], kt, c_row, w_row)

    def body(i, carry):
        for g in reversed(range(len(streams))):
            nxt = streams[g]["far"](i)
            absorb_slot(g, i)
            s_sc[g] = nxt
        return carry

    lax.fori_loop(0, n_far, body, 0)
    for g in range(len(streams)):
        absorb_slot(g, n_far)


def _run_ahead(jobs, depth=RUN_AHEAD):
    pending = [job[0]() for job in jobs[:depth]]
    for k, (_, consume) in enumerate(jobs):
        if k + depth < len(jobs):
            pending.append(jobs[k + depth][0]())
        consume(pending.pop(0))


def _flash_start_jobs(g, stream, s_sc, m_ref, acc_ref):
    def init(scores):
        m_ref[g], acc_ref[g] = stream["own_softmax"](scores)

    def park(scores):
        s_sc[g] = scores

    return [(stream["own_scores"], init), (stream["prev"], park)]


def _softmax_av(s_list, pv_list):
    m = s_list[0].max(axis=0, keepdims=True)
    for s in s_list[1:]:
        m = jnp.maximum(m, s.max(axis=0, keepdims=True))
    acc = None
    for s, pv in zip(s_list, pv_list):
        part = pv(jnp.exp2(s - m).astype(BF16))
        acc = part if acc is None else acc + part
    return acc


def _normalize(acc):
    return acc[:HEAD_DIM] * (1.0 / acc[HEAD_DIM:HEAD_DIM + 1])


def _rank_before(score, n_cand):
    ranks = []
    for r0 in range(0, score.shape[0], SUBLANES):
        tile = score[r0:r0 + SUBLANES]
        blk = lax.broadcasted_iota(jnp.int32, tile.shape, 0) + r0
        rank = jnp.zeros(tile.shape, F32)
        for m in range(n_cand):
            row = score[m:m + 1, :]
            if m < r0:
                before = jnp.where(row >= tile, 1.0, 0.0)
            elif m >= r0 + SUBLANES:
                before = jnp.where(row > tile, 1.0, 0.0)
            else:
                before = jnp.where(blk > m, jnp.where(row >= tile, 1.0, 0.0), jnp.where(row > tile, 1.0, 0.0))
            rank = rank + before
        ranks.append(rank)
    return jnp.concatenate(ranks, axis=0)


def _nsa_steps(qi, maybe_first, q_ref, gn_ref, kc_ref, vct_ref, ks_ref, vs_ref, kw_ref, vw_ref,
               bct_ref, tt_ref, twt_ref, cfar_ref, ovt_ref, o_ref,
               vst_sc, vwt_sc, qa_sc, og_sc, acc_sc):
    nt = ks_ref.shape[1]
    n_sel = ovt_ref.shape[0]

    def transpose_v():
        for kt in range(nt):
            vs_t = vs_ref[0, kt].astype(F32).T
            vw_t = vw_ref[0, kt].astype(F32).T
            for g in range(NSA_GROUPS):
                _store_v_aug(vst_sc, (kt, g), vs_t[g * HEAD_DIM:(g + 1) * HEAD_DIM])
                _store_v_aug(vwt_sc, (kt, g), vw_t[g * HEAD_DIM:(g + 1) * HEAD_DIM])

    if maybe_first:
        pl.when(qi == 0)(transpose_v)

    pos = lax.broadcasted_iota(jnp.int32, (1, TILE), 1) + qi * TILE
    cur = pos // SEL_BLOCK
    has_cmp = pos >= CMP_LEN - 1
    blk = lax.broadcasted_iota(jnp.int32, (n_sel, TILE), 0)
    prev = jnp.maximum(qi - 1, 0)
    gates = gn_ref[...]

    gsls = [slice(g * HEAD_DIM, (g + 1) * HEAD_DIM) for g in range(NSA_GROUPS)]
    group_heads = [[g * NSA_HPG + j for j in range(NSA_HPG)] for g in range(NSA_GROUPS)]

    def gate(g, branch):
        return _lane_cat([gates[3 * h + branch:3 * h + branch + 1, :] for h in group_heads[g]])

    ones_row = jnp.ones((1, NSA_HPG * TILE), F32)

    def sel_stream(g):
        def qk(kt):
            return _dot(ks_ref[0, kt, :, g * 2 * HEAD_DIM:(g + 1) * 2 * HEAD_DIM], qa_sc[g])

        def own_scores():
            return _triangle_scores(
                lambda lo, hi: ks_ref[0, qi, lo:hi, g * 2 * HEAD_DIM:(g + 1) * 2 * HEAD_DIM], qa_sc[g],
                lambda lo, hi: tt_ref[g, 0, lo:hi, :], causal=True)

        def own_softmax(scores):
            return _triangle_softmax(
                scores, lambda pr: _dot(vst_sc[qi, g, :, 0:HALF], pr),
                lambda pr: _dot(vst_sc[qi, g, :, HALF:TILE], pr), causal=True)

        return dict(own_scores=own_scores, own_softmax=own_softmax, prev=lambda: qk(prev) + tt_ref[g, 1], far=qk,
                    c_far=cfar_ref[g], w=lambda kt: ones_row,
                    pv=lambda kt: (lambda pr: _dot(vst_sc[kt, g], pr)))

    def group_jobs(g):
        heads = group_heads[g]
        win = {}

        def v_half(kt, lo, hi):
            return lambda pr: _dot(vwt_sc[kt, g, :, lo:hi], pr)

        def window_own_scores():
            q4 = _lane_cat([q_ref[h * HEAD_DIM:(h + 1) * HEAD_DIM, :] for h in heads])
            qa_sc[g, 0:HEAD_DIM, :] = q4
            qa_sc[g, HEAD_DIM + n_sel:, :] = jnp.zeros((HEAD_DIM - n_sel, NSA_HPG * TILE), BF16)
            return _triangle_scores(lambda lo, hi: kw_ref[0, qi, lo:hi, gsls[g]], q4,
                                    lambda lo, hi: tt_ref[g, 0, lo:hi, :], causal=True)

        def window_own(scores):
            win["own"] = _triangle_softmax(scores, v_half(qi, 0, HALF), v_half(qi, HALF, TILE), causal=True)

        def window_prev_scores():
            return _triangle_scores(lambda lo, hi: kw_ref[0, prev, lo:hi, gsls[g]], qa_sc[g, 0:HEAD_DIM, :],
                                    lambda lo, hi: twt_ref[g, lo:hi, :], causal=False)

        def window_prev(scores):
            m_prev, acc_prev = _triangle_softmax(scores, v_half(prev, 0, HALF), v_half(prev, HALF, TILE), causal=False)
            m_prev = jnp.where(qi == 0, NEG_INF, m_prev)
            m_own, acc_own = win["own"]
            m_win = jnp.maximum(m_own, m_prev)
            acc_w = acc_own * jnp.exp2(m_own - m_win) + acc_prev * jnp.exp2(m_prev - m_win)
            og_sc[g] = gate(g, 2) * _normalize(acc_w)

        def compressed_scores():
            rows_per_tile = TILE // CMP_STRIDE
            c_rows = pl.ds(pl.multiple_of((nt - 1 - qi) * rows_per_tile, rows_per_tile), N_CMP_PAD)
            return (_dot(kc_ref[0, :, gsls[g]], qa_sc[g, 0:HEAD_DIM, :])
                    + _lane_cat([bct_ref[h, c_rows, :] for h in heads]))

        def compressed(s):
            e = jnp.exp2(s - jnp.max(s, axis=0, keepdims=True))
            p = e * jnp.where(_lane_cat([has_cmp] * NSA_HPG), 1.0 / jnp.sum(e, axis=0, keepdims=True), 0.0)
            psum = p[:, :TILE]
            for j in range(1, NSA_HPG):
                psum = psum + p[:, j * TILE:(j + 1) * TILE]
            og_sc[g] = og_sc[g] + gate(g, 0) * _dot(vct_ref[0, gsls[g], :], p.astype(BF16))

            p_hi, p_lo = _split_bf16(psum)
            imp = _dot(ovt_ref[...], p_hi) + _dot(ovt_ref[...], p_lo)
            forced = (blk == 0) | (blk == cur) | (blk == cur - 1)
            score = jnp.where(forced, FORCE_SCORE, jnp.where(blk <= cur, imp, NEG_INF))
            rank = _rank_before(score, n_sel)
            sel = jnp.where(rank < SEL_TOPN, jnp.where(score > NEG_INF / 2, 0.0, NEG_INF), NEG_INF)
            qa_sc[g, HEAD_DIM:HEAD_DIM + n_sel, :] = _lane_cat([sel.astype(BF16)] * NSA_HPG)

        return [(window_own_scores, window_own), (window_prev_scores, window_prev), (compressed_scores, compressed)]

    yield [job for g in range(NSA_GROUPS) for job in group_jobs(g)]
    yield [sel_stream(g) for g in range(NSA_GROUPS)]

    for g in range(NSA_GROUPS):
        o = og_sc[g] + gate(g, 1) * _normalize(acc_sc[g])
        for j, h in enumerate(group_heads[g]):
            o_ref[h * HEAD_DIM:(h + 1) * HEAD_DIM, :] = o[:, j * TILE:(j + 1) * TILE].astype(BF16)


def _moba_steps(c, maybe_first, qm_ref, km_ref, vm_ref, qx_ref, mk_ref, mv_ref, tt_ref, cfar_ref, om_ref, ox_ref,
                vmt_sc, mvt_sc, kmean_sc, qbd_sc, sel_sc, acc_sc, slot):
    nt = km_ref.shape[1]
    hsls = [slice(h * HEAD_DIM, (h + 1) * HEAD_DIM) for h in range(MOBA_HEADS)]

    def per_sequence():
        kmean_sc[...] = jnp.zeros(kmean_sc.shape, F32)
        for n in range(nt):
            kmean_sc[n:n + 1, :] = jnp.sum(km_ref[0, n].astype(F32), axis=0, keepdims=True) * (1.0 / MOBA_BLOCK)
            vt = vm_ref[0, n].astype(F32).T
            for h in range(MOBA_HEADS):
                _store_v_aug(vmt_sc, (n, h), vt[hsls[h]])
        mvt = mv_ref[0].astype(F32).T
        for h in range(MOBA_HEADS):
            _store_v_aug(mvt_sc, h, mvt[hsls[h]])

    if maybe_first:
        pl.when(c == 0)(per_sequence)

    row_head = lax.broadcasted_iota(jnp.int32, (MOBA_W, TILE), 0) // HEAD_DIM

    def block_diag(q_ref):
        q_t = q_ref[...].astype(F32)
        return _lane_cat([jnp.where(row_head == h, q_t, 0.0) for h in range(MOBA_HEADS)]).astype(BF16)

    def per_head_pv(vts):
        return lambda pr: _lane_cat([_dot(vts(h), pr[:, h * TILE:(h + 1) * TILE]) for h in range(MOBA_HEADS)])

    def store_heads(o_t, out_ref):
        for h in range(MOBA_HEADS):
            out_ref[hsls[h], :] = o_t[:, h * TILE:(h + 1) * TILE].astype(BF16)

    n_rows = -(-nt // SUBLANES) * SUBLANES

    def gate_scores():
        qbd = block_diag(qm_ref)
        qbd_sc[...] = qbd
        km_hi, km_lo = _split_bf16(kmean_sc[...])
        return (_dot(km_hi, qbd) + _dot(km_lo, qbd))[:n_rows]

    def select(gate):
        blk = lax.broadcasted_iota(jnp.int32, gate.shape, 0)
        score = jnp.where(blk < c, gate, NEG_INF * Q_SCALE)
        rank = _rank_before(score, nt)
        sel_sc[0:n_rows, :] = jnp.where(rank < MOBA_TOPK, jnp.where(score > NEG_INF * Q_SCALE / 2, 1.0, 0.0), 0.0)

    qk = lambda n: _dot(km_ref[0, n], qbd_sc[...])
    def own_pv(lo, hi):
        def pv(pr):
            width = pr.shape[1] // MOBA_HEADS
            return _lane_cat([_dot(vmt_sc[c, h, :, lo:hi], pr[:, h * width:(h + 1) * width])
                              for h in range(MOBA_HEADS)])
        return pv

    def own_scores():
        return _triangle_scores(lambda lo, hi: km_ref[0, c, lo:hi, :], qbd_sc[...],
                                lambda lo, hi: tt_ref[0, lo:hi, :], causal=True)

    def own_softmax(scores):
        return _triangle_softmax(scores, own_pv(0, HALF), own_pv(HALF, TILE), causal=True)

    stream = dict(own_scores=own_scores, own_softmax=own_softmax,
                  prev=lambda: qk(jnp.maximum(c - 1, 0)) + tt_ref[1], far=qk,
                  c_far=cfar_ref[...], w=lambda n: sel_sc[pl.ds(n, 1), :],
                  pv=lambda n: per_head_pv(lambda h: vmt_sc[n, h]))
    def memory(s):
        store_heads(_normalize(_softmax_av([s], [per_head_pv(lambda h: mvt_sc[h])])), ox_ref)

    yield [(gate_scores, select), (lambda: _dot(mk_ref[0], block_diag(qx_ref)), memory)]
    yield [stream]
    store_heads(_normalize(acc_sc[slot]), om_ref)


N_STREAMS = NSA_GROUPS + 1
TILES_PER_STEP = 2
NSA_ROW_INPUTS = (0, 1)
MOBA_ROW_INPUTS = (0, 3)


def _attention_kernel(*refs, n_nsa_in, n_moba_in, n_nsa_scratch, n_moba_scratch):
    nsa_in, refs = refs[:n_nsa_in], refs[n_nsa_in:]
    moba_in, refs = refs[:n_moba_in], refs[n_moba_in:]
    outs, refs = refs[:3], refs[3:]
    nsa_sc, refs = refs[:n_nsa_scratch], refs[n_nsa_scratch:]
    moba_sc, refs = refs[:n_moba_scratch], refs[n_moba_scratch:]
    s_sc, m_sc, acc_sc = refs
    for t in range(TILES_PER_STEP):
        qi = pl.program_id(1) * TILES_PER_STEP + t
        cols = lambda ref: ref.at[:, t * TILE:(t + 1) * TILE]
        o_nsa, o_moba, o_mem = [cols(o) for o in outs]
        nsa_refs = [cols(r) if k in NSA_ROW_INPUTS else r for k, r in enumerate(nsa_in)]
        moba_refs = [cols(r) if k in MOBA_ROW_INPUTS else r for k, r in enumerate(moba_in)]
        nsa = _nsa_steps(qi, t == 0, *nsa_refs, o_nsa, *nsa_sc, acc_sc)
        moba = _moba_steps(qi, t == 0, *moba_refs, o_moba, o_mem, *moba_sc, acc_sc, NSA_GROUPS)
        jobs = next(nsa) + next(moba)
        streams = next(nsa) + next(moba)
        jobs += [job for g, stream in enumerate(streams) for job in _flash_start_jobs(g, stream, s_sc, m_sc, acc_sc)]
        _run_ahead(jobs)
        _flash_pipelined(qi, streams, s_sc, m_sc, acc_sc)
        for steps in (nsa, moba):
            for _ in steps:
                pass


def _attention(b, s, qn, gn, kc, vct, ks, vs, kw, vw, bias_cmp, t_nsa, t_win, c_far_nsa, ovt,
               qm, km, vm, qx, mk, mv, t_moba, c_far_moba):
    nt = s // TILE
    mem_len = mk.shape[0] // b
    assert MOBA_TOPK <= nt - 1 and nt <= BF16_ROWS
    n_lanes = NSA_HPG * TILE
    assert MOBA_HEADS * TILE == n_lanes
    assert nt % TILES_PER_STEP == 0
    steps = nt // TILES_PER_STEP
    row_spec = lambda w: pl.BlockSpec((w, TILES_PER_STEP * TILE), lambda i, j: (0, i * steps + j))
    seq_spec = lambda w: pl.BlockSpec((1, nt, TILE, w), lambda i, j: (i, 0, 0, 0))
    per_batch = lambda rows, w: pl.BlockSpec((1, rows, w), lambda i, j: (i, 0, 0))
    tiles = lambda a: a.reshape(b, nt, TILE, a.shape[-1])
    nsa_in = [(qn, row_spec(NSA_Q_W)), (gn, row_spec(GATE_PAD)),
              (kc, per_batch(N_CMP_PAD, NSA_KV_W)), (vct, per_batch(N_CMP_PAD, NSA_KV_W)),
              (tiles(ks), seq_spec(KS_AUG_W)), (tiles(vs), seq_spec(NSA_KV_W)),
              (tiles(kw), seq_spec(NSA_KV_W)), (tiles(vw), seq_spec(NSA_KV_W)),
              (bias_cmp, _const_spec(bias_cmp.shape)), (t_nsa, _const_spec(t_nsa.shape)),
              (t_win, _const_spec(t_win.shape)), (c_far_nsa, _const_spec(c_far_nsa.shape)),
              (ovt, _const_spec(ovt.shape))]
    moba_in = [(qm, row_spec(MOBA_W)), (tiles(km), seq_spec(MOBA_W)), (tiles(vm), seq_spec(MOBA_W)),
               (qx, row_spec(MEM_W)),
               (mk.reshape(b, mem_len, MEM_W), per_batch(mem_len, MEM_W)),
               (mv.reshape(b, mem_len, MEM_W), per_batch(mem_len, MEM_W)),
               (t_moba, _const_spec(t_moba.shape)), (c_far_moba, _const_spec(c_far_moba.shape))]
    nsa_scratch =[pltpu.VMEM((nt, NSA_GROUPS, V_AUG, TILE), BF16),
                   pltpu.VMEM((nt, NSA_GROUPS, V_AUG, TILE), BF16),
                   pltpu.VMEM((NSA_GROUPS, 2 * HEAD_DIM, n_lanes), BF16),
                   pltpu.VMEM((NSA_GROUPS, HEAD_DIM, n_lanes), F32)]
    moba_scratch = [pltpu.VMEM((nt, MOBA_HEADS, V_AUG, TILE), BF16),
                    pltpu.VMEM((MEM_HEADS, V_AUG, mem_len), BF16),
                    pltpu.VMEM((BF16_ROWS, MOBA_W), F32),
                    pltpu.VMEM((MOBA_W, n_lanes), BF16),
                    pltpu.VMEM((BF16_ROWS, n_lanes), F32)]
    shared_scratch =[pltpu.VMEM((N_STREAMS, TILE, n_lanes), F32),
                      pltpu.VMEM((N_STREAMS, 1, n_lanes), F32),
                      pltpu.VMEM((N_STREAMS, V_AUG, n_lanes), F32)]
    inputs = nsa_in + moba_in
    return pl.pallas_call(
        functools.partial(_attention_kernel, n_nsa_in=len(nsa_in), n_moba_in=len(moba_in),
                          n_nsa_scratch=len(nsa_scratch), n_moba_scratch=len(moba_scratch)),
        grid=(b, steps),
        in_specs=[spec for _, spec in inputs],
        out_specs=[row_spec(NSA_Q_W), row_spec(MOBA_W), row_spec(MEM_W)],
        out_shape=[jax.ShapeDtypeStruct((w, b * s), BF16) for w in (NSA_Q_W, MOBA_W, MEM_W)],
        scratch_shapes=nsa_scratch + moba_scratch + shared_scratch,
        compiler_params=_params(("arbitrary", "arbitrary")),
        name="attention",
    )(*[a for a, _ in inputs])


MIX_CHUNK = MXU_COLS


def _mix_kernel(x_ref, on_ref, om_ref, ox_ref, g_pre_ref, g_post_ref, wg_ref, wn_ref, wm_ref, wx_ref,
                wo_ref, o_ref, merged_sc):
    x = x_ref[...]
    h = _rms(x, g_pre_ref[...]).astype(BF16)
    rows = lambda o_ref: o_ref[...].astype(F32).T.astype(BF16)
    branches = ((rows(on_ref), wn_ref), (rows(om_ref), wm_ref), (rows(ox_ref), wx_ref))

    def chunk_job(c0):
        cols = slice(c0, c0 + MIX_CHUNK)

        def produce():
            return [(_dot(h, wg_ref[:, i * D_MODEL + c0:i * D_MODEL + c0 + MIX_CHUNK]), _dot(o, w_ref[:, cols]))
                    for i, (o, w_ref) in enumerate(branches)]

        def consume(pairs):
            merged_sc[:, cols] = sum(jax.nn.sigmoid(g) * y for g, y in pairs).astype(BF16)

        return produce, consume

    _run_ahead([chunk_job(c0) for c0 in range(0, D_MODEL, MIX_CHUNK)], depth=1)
    y = _dot(merged_sc[...], wo_ref[...])
    o_ref[...] = x + _rms(y, g_post_ref[...])


def _mix(x2, o_nsa, o_moba, o_mem, g_pre, g_post, w_gates, w_nsa_o, w_moba_o, w_mem_o, w_mix_out, tm=512):
    m = x2.shape[0]
    row = lambda w: pl.BlockSpec((tm, w), lambda i: (i, 0))
    col = lambda w: pl.BlockSpec((w, tm), lambda i: (0, i))
    return pl.pallas_call(
        _mix_kernel,
        grid=(m // tm,),
        in_specs=[row(D_MODEL), col(NSA_Q_W), col(MOBA_W), col(MEM_W),
                  _const_spec((1, D_MODEL)), _const_spec((1, D_MODEL)),
                  _const_spec(w_gates.shape), _const_spec(w_nsa_o.shape), _const_spec(w_moba_o.shape),
                  _const_spec(w_mem_o.shape), _const_spec(w_mix_out.shape)],
        out_specs=row(D_MODEL),
        out_shape=jax.ShapeDtypeStruct((m, D_MODEL), F32),
        scratch_shapes=[pltpu.VMEM((tm, D_MODEL), BF16)],
        compiler_params=_params(("parallel",)),
        name="mix",
    )(x2, o_nsa, o_moba, o_mem, g_pre, g_post, w_gates, w_nsa_o, w_moba_o, w_mem_o, w_mix_out)


FFN_CHUNK = 256


def _ffn_kernel(x_ref, g_pre_ref, g_post_ref, wg_ref, wu_ref, wd_ref, o_ref, a_sc):
    x = x_ref[...]
    h = _rms(x, g_pre_ref[...]).astype(BF16)
    d_ff = wg_ref.shape[1]
    for j in range(d_ff // FFN_CHUNK):
        sl = slice(j * FFN_CHUNK, (j + 1) * FFN_CHUNK)
        a_sc[:, sl] = (jax.nn.silu(_dot(h, wg_ref[:, sl])) * _dot(h, wu_ref[:, sl])).astype(BF16)
    f = _dot(a_sc[...], wd_ref[...])
    o_ref[...] = x + _rms(f, g_post_ref[...])


def _ffn(x2, g_pre, g_post, wg, wu, wd, tm=512):
    m = x2.shape[0]
    d_ff = wg.shape[1]
    return pl.pallas_call(
        _ffn_kernel,
        grid=(m // tm,),
        in_specs=[pl.BlockSpec((tm, D_MODEL), lambda i: (i, 0)),
                  _const_spec((1, D_MODEL)), _const_spec((1, D_MODEL)),
                  _const_spec(wg.shape), _const_spec(wu.shape), _const_spec(wd.shape)],
        out_specs=pl.BlockSpec((tm, D_MODEL), lambda i: (i, 0)),
        out_shape=jax.ShapeDtypeStruct((m, D_MODEL), F32),
        scratch_shapes=[pltpu.VMEM((tm, d_ff), BF16)],
        compiler_params=_params(("parallel",)),
        name="ffn",
    )(x2, g_pre, g_post, wg, wu, wd)


def kernel(x, mem, rel_bias, pre_mix_g, mem_norm_g, post_mix_g, w_in, cmp_pos_k, cmp_w1_k, cmp_w2_k, cmp_pos_v, cmp_w1_v, cmp_w2_v, w_mem_kv, w_nsa_o, w_moba_o, w_mem_o, w_mix_out, pre_ffn_g, post_ffn_g, w_ffn_gate, w_ffn_up, w_ffn_down):
    b, s, d_model = x.shape
    depth = w_in.shape[0]
    assert d_model == D_MODEL and s % TILE == 0 and TILE == MOBA_BLOCK == WINDOW
    assert (s - CMP_LEN) // CMP_STRIDE + 1 < N_CMP_PAD and (s // SEL_BLOCK) % SUBLANES == 0 and s // SEL_BLOCK <= HEAD_DIM
    assert w_in.shape[2] == ATT_W + 3 * D_MODEL and rel_bias.shape == (REL_BUCKETS, N_BIAS_HEADS)

    tile_idx, win_idx, cmp_idx = _bucket_tables(s)
    rel_bias = rel_bias.astype(F32)
    t_nsa = _expand(tile_idx, rel_bias, 0, NSA_HEADS, NSA_HPG).reshape(NSA_GROUPS, 2, TILE, NSA_HPG * TILE)
    t_moba = _expand(tile_idx, rel_bias, NSA_HEADS, MOBA_HEADS, MOBA_HEADS).reshape(2, TILE, MOBA_HEADS * TILE)
    t_win = _expand(win_idx, rel_bias, 0, NSA_HEADS, NSA_HPG)
    b_cmp = _expand(cmp_idx, rel_bias, 0, NSA_HEADS)
    c_far = jnp.repeat(rel_bias[REL_BUCKETS - 1] * LOG2E, TILE)
    c_far_nsa = c_far[:NSA_HEADS * TILE].reshape(NSA_GROUPS, 1, NSA_HPG * TILE)
    c_far_moba = c_far[NSA_HEADS * TILE:].reshape(1, MOBA_HEADS * TILE)
    ovt = _overlap_table(s)
    sel_cols = np.zeros((s, HEAD_DIM), np.float32)
    sel_cols[np.arange(s), np.arange(s) // SEL_BLOCK] = 1.0
    sel_cols = jnp.asarray(sel_cols, BF16)
    gate_lo = NSA_Q_W + 6 * NSA_KV_W
    rows_per_chunk = CMP_STRIDE * NSA_KV_W

    x2 = x.reshape(b * s, D_MODEL)
    mem2 = mem.reshape(-1, D_MODEL)
    for l in range(depth):
        w_att = jnp.concatenate(
            [w_in[l, :, :gate_lo + NSA_GATE_W],
             jnp.zeros((D_MODEL, GATE_PAD - NSA_GATE_W), w_in.dtype),
             w_in[l, :, gate_lo + NSA_GATE_W:ATT_W]], axis=1).astype(BF16)
        w_gates = w_in[l, :, ATT_W:].astype(BF16)
        row = lambda v: v[l].reshape(1, D_MODEL)

        qn, kc_raw, vc_raw, ks, vs, kw, vw, gn, qm, km, vm, qx = _inproj(x2, row(pre_mix_g), w_att, sel_cols)

        pk, w1k = _compress_weights(cmp_pos_k[l], cmp_w1_k[l])
        pv, w1v = _compress_weights(cmp_pos_v[l], cmp_w1_v[l])
        kc, vct = _compress(kc_raw.reshape(b, s // CMP_STRIDE, rows_per_chunk),
                            vc_raw.reshape(b, s // CMP_STRIDE, rows_per_chunk),
                            pk, pv, w1k, w1v, cmp_w2_k[l].astype(BF16), cmp_w2_v[l].astype(BF16))

        mk, mv = _memkv(mem2, row(mem_norm_g), w_mem_kv[l].astype(BF16))

        o_nsa, o_moba, o_mem = _attention(b, s, qn, gn, kc, vct, ks, vs, kw, vw, b_cmp, t_nsa, t_win, c_far_nsa, ovt,
                                          qm, km, vm, qx, mk, mv, t_moba, c_far_moba)

        x2 = _mix(x2, o_nsa, o_moba, o_mem, row(pre_mix_g), row(post_mix_g), w_gates,
                  w_nsa_o[l].astype(BF16), w_moba_o[l].astype(BF16), w_mem_o[l].astype(BF16),
                  w_mix_out[l].astype(BF16))
        x2 = _ffn(x2, row(pre_ffn_g), row(post_ffn_g), w_ffn_gate[l].astype(BF16),
                  w_ffn_up[l].astype(BF16), w_ffn_down[l].astype(BF16))
    return x2.reshape(b, s, D_MODEL)
```

```python
import functools
import math

import numpy as np
import jax
import jax.numpy as jnp
from jax import lax
from jax.experimental import pallas as pl
from jax.experimental.pallas import tpu as pltpu

F32 = jnp.float32
BF16 = jnp.bfloat16

D_MODEL = 1024
HEAD_DIM = 64
SCALE = HEAD_DIM ** -0.5
LOG2E = math.log2(math.e)
Q_SCALE = SCALE * LOG2E
NSA_HEADS = 8
NSA_GROUPS = 2
NSA_HPG = NSA_HEADS // NSA_GROUPS
CMP_LEN = 32
CMP_STRIDE = 16
CMP_HIDDEN = 128
SEL_BLOCK = 64
SEL_TOPN = 8
WINDOW = 256
MOBA_HEADS = 4
MOBA_BLOCK = 256
MOBA_TOPK = 3
MEM_HEADS = 4
REL_BUCKETS = 32
REL_MAX_DIST = 128
N_BIAS_HEADS = NSA_HEADS + MOBA_HEADS
RMS_EPS = 1e-6
NEG_INF = -1e30
FORCE_SCORE = 1e4

NSA_Q_W = NSA_HEADS * HEAD_DIM
NSA_KV_W = NSA_GROUPS * HEAD_DIM
NSA_GATE_W = NSA_HEADS * 3
MOBA_W = MOBA_HEADS * HEAD_DIM
MEM_W = MEM_HEADS * HEAD_DIM
ATT_W = NSA_Q_W + 6 * NSA_KV_W + NSA_GATE_W + 3 * MOBA_W + MEM_W
LANES = 128
SUBLANES = 8
BF16_ROWS = 16
MXU_COLS = 256
GATE_PAD = LANES
TILE = 256
HALF = TILE // 2
RUN_AHEAD = 2
N_CMP_PAD = 128
V_AUG = HEAD_DIM + BF16_ROWS
MASKED_BUCKET = REL_BUCKETS
VMEM_LIMIT = 56 * 1024 * 1024


def _dot(a, b):
    return jnp.dot(a, b, preferred_element_type=F32)


def _split_bf16(x):
    hi = x.astype(BF16)
    lo = (x - hi.astype(F32)).astype(BF16)
    return hi, lo


def _rms(x, g):
    return x * lax.rsqrt(jnp.mean(x * x, axis=-1, keepdims=True) + RMS_EPS) * g


def _params(sem):
    return pltpu.CompilerParams(dimension_semantics=sem, vmem_limit_bytes=VMEM_LIMIT)


def _const_spec(shape):
    nd = len(shape)
    return pl.BlockSpec(shape, lambda *_: (0,) * nd, pipeline_mode=pl.Buffered(1))


_INPROJ_OUTS = (
    ("qn", NSA_Q_W, BF16, True),
    ("kc", NSA_KV_W, F32, False), ("vc", NSA_KV_W, F32, False),
    ("ks", NSA_KV_W, BF16, False), ("vs", NSA_KV_W, BF16, False),
    ("kw", NSA_KV_W, BF16, False), ("vw", NSA_KV_W, BF16, False),
    ("gn", GATE_PAD, F32, False),
    ("qm", MOBA_W, BF16, True), ("km", MOBA_W, BF16, False), ("vm", MOBA_W, BF16, False),
    ("qx", MEM_W, BF16, True),
)
_INPROJ_W = sum(o[1] for o in _INPROJ_OUTS)
_INPROJ_CHUNKED = ("kc", "vc")
_INPROJ_TRANSPOSED = ("qn", "gn", "qm", "qx")
KS_AUG_W = NSA_GROUPS * 2 * HEAD_DIM


def _inproj_out_width(name, width):
    return KS_AUG_W if name == "ks" else width


def _inproj_kernel(x_ref, g_ref, w_ref, e_ref, *refs):
    out_refs, rows_sc = refs[:-1], refs[-1]
    h = _rms(x_ref[...], g_ref[...]).astype(BF16)
    runs, lo = [], 0
    for out in zip(_INPROJ_OUTS, out_refs):
        if runs and runs[-1][1] < MXU_COLS:
            runs[-1][0].append(out)
            runs[-1][1] += out[0][1]
        else:
            runs.append([[out], out[0][1], lo])
        lo += out[0][1]
    for outs, run_width, run_lo in runs:
        y_run = _dot(h, w_ref[:, run_lo:run_lo + run_width])
        lo = 0
        for (name, width, dtype, scaled), o_ref in outs:
            y = y_run[:, lo:lo + width]
            if scaled:
                y = y * Q_SCALE
            if name == "gn":
                y = jax.nn.sigmoid(y)
            if name in _INPROJ_TRANSPOSED:
                y = y.T
            y = y.astype(dtype)
            if name == "ks":
                e = e_ref[...]
                y = _lane_cat([y[:, :HEAD_DIM], e, y[:, HEAD_DIM:], e])
            if name in _INPROJ_CHUNKED:
                rows_sc[...] = y
                for j in range(CMP_STRIDE):
                    o_ref[:, j * width:(j + 1) * width] = rows_sc[pl.ds(j, o_ref.shape[0], stride=CMP_STRIDE), :]
            else:
                o_ref[...] = y
            lo += width


def _inproj(x2, g, w, e_cols, tm=512):
    m = x2.shape[0]
    tiles_per_seq = e_cols.shape[0] // tm
    out_specs, out_shape = [], []
    for name, width, dtype, _ in _INPROJ_OUTS:
        if name in _INPROJ_TRANSPOSED:
            out_specs.append(pl.BlockSpec((width, tm), lambda i: (0, i)))
            out_shape.append(jax.ShapeDtypeStruct((width, m), dtype))
            continue
        rows, width = (CMP_STRIDE, CMP_STRIDE * width) if name in _INPROJ_CHUNKED else (1, _inproj_out_width(name, width))
        out_specs.append(pl.BlockSpec((tm // rows, width), lambda i: (i, 0)))
        out_shape.append(jax.ShapeDtypeStruct((m // rows, width), dtype))
    return pl.pallas_call(
        _inproj_kernel,
        grid=(m // tm,),
        in_specs=[pl.BlockSpec((tm, D_MODEL), lambda i: (i, 0)),
                  _const_spec((1, D_MODEL)),
                  _const_spec((D_MODEL, _INPROJ_W)),
                  pl.BlockSpec((tm, HEAD_DIM), lambda i: (i % tiles_per_seq, 0))],
        out_specs=out_specs,
        out_shape=out_shape,
        scratch_shapes=[pltpu.VMEM((tm, NSA_KV_W), F32)],
        compiler_params=_params(("parallel",)),
        name="inproj",
    )(x2, g, w, e_cols)


def _compress_kernel(rk_ref, rv_ref, pk_ref, pv_ref, w1k_ref, w1v_ref, w2k_ref, w2v_ref, kc_ref, vc_ref):
    nb = rk_ref.shape[0]

    def one(r_ref, p_ref, w1_ref, w2_ref):
        r = r_ref[...].reshape(nb * N_CMP_PAD, r_ref.shape[2])
        top = _dot((r + p_ref[0:1, :]).astype(BF16), w1_ref[0])
        bot = _dot((r + p_ref[1:2, :]).astype(BF16), w1_ref[1])
        hid = top + pltpu.roll(bot, nb * N_CMP_PAD - 1, 0)
        act = jax.nn.gelu(hid).astype(BF16)
        return jnp.concatenate(
            [_dot(act[:, g * CMP_HIDDEN:(g + 1) * CMP_HIDDEN], w2_ref[...]) for g in range(NSA_GROUPS)], axis=1)

    k_out = one(rk_ref, pk_ref, w1k_ref, w2k_ref)
    v_out = one(rv_ref, pv_ref, w1v_ref, w2v_ref)
    for n in range(nb):
        rows = slice(n * N_CMP_PAD, (n + 1) * N_CMP_PAD)
        kc_ref[n] = k_out[rows].astype(BF16)
        vc_ref[n] = v_out[rows].T.astype(BF16)


def _compress(rk, rv, pk, pv, w1k, w1v, w2k, w2v):
    b = rk.shape[0]
    rw = rk.shape[2]
    nb = 2 if b % 2 == 0 else 1
    r_spec = pl.BlockSpec((nb, N_CMP_PAD, rw), lambda i: (i, 0, 0))
    o_spec = pl.BlockSpec((nb, N_CMP_PAD, NSA_KV_W), lambda i: (i, 0, 0))
    return pl.pallas_call(
        _compress_kernel,
        grid=(b // nb,),
        in_specs=[r_spec, r_spec, _const_spec(pk.shape), _const_spec(pv.shape),
                  _const_spec(w1k.shape), _const_spec(w1v.shape),
                  _const_spec(w2k.shape), _const_spec(w2v.shape)],
        out_specs=[o_spec, o_spec],
        out_shape=[jax.ShapeDtypeStruct((b, N_CMP_PAD, NSA_KV_W), BF16)] * 2,
        compiler_params=_params(("parallel",)),
        name="compress",
    )(rk, rv, pk, pv, w1k, w1v, w2k, w2v)


def _compress_weights(pos, w1):
    half = CMP_LEN // 2
    p = pos.reshape(2, half, 1, HEAD_DIM)
    p = jnp.broadcast_to(p, (2, half, NSA_GROUPS, HEAD_DIM)).reshape(2, half * NSA_KV_W)
    w = w1.reshape(2, half, HEAD_DIM, CMP_HIDDEN)
    eye = jnp.eye(NSA_GROUPS, dtype=w1.dtype)
    wbd = jnp.einsum("ajdm,gk->ajgdkm", w, eye).reshape(2, half * NSA_KV_W, NSA_GROUPS * CMP_HIDDEN)
    return p.astype(F32), wbd.astype(BF16)


def _memkv_kernel(m_ref, g_ref, w_ref, k_ref, v_ref):
    h = _rms(m_ref[...], g_ref[...]).astype(BF16)
    k_ref[...] = _dot(h, w_ref[:, :MEM_W]).astype(BF16)
    v_ref[...] = _dot(h, w_ref[:, MEM_W:]).astype(BF16)


def _memkv(mem2, g, w, tm=512):
    m = mem2.shape[0]
    tm = min(tm, m)
    o_spec = pl.BlockSpec((tm, MEM_W), lambda i: (i, 0))
    return pl.pallas_call(
        _memkv_kernel,
        grid=(m // tm,),
        in_specs=[pl.BlockSpec((tm, D_MODEL), lambda i: (i, 0)), _const_spec((1, D_MODEL)),
                  _const_spec((D_MODEL, 2 * MEM_W))],
        out_specs=[o_spec, o_spec],
        out_shape=[jax.ShapeDtypeStruct((m, MEM_W), BF16)] * 2,
        compiler_params=_params(("parallel",)),
        name="memkv",
    )(mem2, g, w)


def _expand_kernel(idx_ref, bias_ref, o_ref, *, head0, n_heads, heads_per_group):
    rows, cols = idx_ref.shape

    def body(i, carry):
        r = pl.multiple_of(i * SUBLANES, SUBLANES)
        for c0 in range(0, cols, TILE):
            idx = idx_ref[pl.ds(r, SUBLANES), c0:c0 + TILE]
            out = [jnp.full(idx.shape, NEG_INF, F32)] * n_heads
            for bkt in range(REL_BUCKETS):
                hit = idx == bkt
                out = [jnp.where(hit, bias_ref[bkt, head0 + h], out[h]) for h in range(n_heads)]
            for h in range(n_heads):
                col = (h % heads_per_group) * cols + c0
                o_ref[h // heads_per_group, pl.ds(r, SUBLANES), col:col + TILE] = out[h] * LOG2E
        return carry

    lax.fori_loop(0, rows // SUBLANES, body, 0)


def _expand(idx, rel_bias, head0, n_heads, heads_per_group=1):
    rows, cols = idx.shape
    return pl.pallas_call(
        functools.partial(_expand_kernel, head0=head0, n_heads=n_heads, heads_per_group=heads_per_group),
        in_specs=[pl.BlockSpec(memory_space=pltpu.VMEM), pl.BlockSpec(memory_space=pltpu.SMEM)],
        out_specs=pl.BlockSpec(memory_space=pltpu.VMEM),
        out_shape=jax.ShapeDtypeStruct((n_heads // heads_per_group, rows, heads_per_group * cols), F32),
        compiler_params=pltpu.CompilerParams(vmem_limit_bytes=VMEM_LIMIT),
        name="bias_expand",
    )(idx, rel_bias)


def _t5_bucket_np(dist):
    dist = np.maximum(dist, 0)
    max_exact = REL_BUCKETS // 2
    logd = np.log(np.maximum(dist, 1).astype(np.float32) / max_exact) / math.log(REL_MAX_DIST / max_exact)
    large = np.minimum(max_exact + (logd * (REL_BUCKETS - max_exact)).astype(np.int32), REL_BUCKETS - 1)
    return np.where(dist < max_exact, dist, large).astype(np.int32)


def _bucket_tables(s):
    j = np.arange(TILE)[:, None]
    i = np.arange(TILE)[None, :]
    assert TILE + 1 >= REL_MAX_DIST
    tiles = []
    for d in range(2):
        dist = d * TILE + i - j
        tiles.append(np.where(dist >= 0, _t5_bucket_np(dist), MASKED_BUCKET))
    dist1 = TILE + i - j
    win = np.where(dist1 < WINDOW, _t5_bucket_np(dist1), MASKED_BUCKET)
    n_cmp = (s - CMP_LEN) // CMP_STRIDE + 1
    assert n_cmp * CMP_STRIDE + CMP_LEN - 1 >= s or n_cmp == N_CMP_PAD
    rel = np.arange(N_CMP_PAD + (s - TILE) // CMP_STRIDE)[:, None] - (s - TILE) // CMP_STRIDE
    dist_c = i - (rel * CMP_STRIDE + CMP_LEN - 1)
    cmp_idx = np.where(dist_c >= 0, _t5_bucket_np(dist_c), MASKED_BUCKET)
    as_i32 = lambda a: jnp.asarray(a.astype(np.int32))
    return as_i32(np.concatenate(tiles, axis=0)), as_i32(win), as_i32(cmp_idx)


def _overlap_table(s):
    n_cmp = (s - CMP_LEN) // CMP_STRIDE + 1
    n_sel = s // SEL_BLOCK
    cs = np.arange(n_cmp) * CMP_STRIDE
    ss = np.arange(n_sel) * SEL_BLOCK
    ov = np.clip(np.minimum(cs[:, None] + CMP_LEN, ss[None, :] + SEL_BLOCK)
                 - np.maximum(cs[:, None], ss[None, :]), 0, None).astype(np.float32) / CMP_LEN
    ovt = np.zeros((n_sel, N_CMP_PAD), np.float32)
    ovt[:, :n_cmp] = ov.T
    return jnp.asarray(ovt, BF16)


def _store_v_aug(vt_sc, idx, vt):
    ones = jnp.ones((BF16_ROWS, vt.shape[1]), BF16)
    vt_sc[idx] = jnp.concatenate([vt.astype(BF16), ones], axis=0)


def _lane_cat(xs):
    return jnp.concatenate(xs, axis=1)


def _query_halves(x):
    n = x.shape[-1] // TILE
    first = _lane_cat([x[:, k * TILE:k * TILE + HALF] for k in range(n)])
    second = _lane_cat([x[:, k * TILE + HALF:(k + 1) * TILE] for k in range(n)])
    return first, second


def _join_query_halves(first, second):
    n = first.shape[-1] // HALF
    return _lane_cat([part for k in range(n)
                      for part in (first[:, k * HALF:(k + 1) * HALF], second[:, k * HALF:(k + 1) * HALF])])


def _triangle_scores(k, q, table, causal):
    q_first, q_second = _query_halves(q)
    lo, hi = (0, HALF), (HALF, TILE)
    if causal:
        return _dot(k(*lo), q) + table(*lo), _dot(k(*hi), q_second) + _query_halves(table(*hi))[1]
    return _dot(k(*hi), q) + table(*hi), _dot(k(*lo), q_first) + _query_halves(table(*lo))[0]


def _triangle_softmax(scores, pv_lo, pv_hi, causal):
    s_wide, s_narrow = scores
    pv_wide, pv_narrow = (pv_lo, pv_hi) if causal else (pv_hi, pv_lo)
    mw_first, mw_second = _query_halves(jnp.max(s_wide, axis=0, keepdims=True))
    m_narrow = jnp.max(s_narrow, axis=0, keepdims=True)
    if causal:
        m_narrow = jnp.maximum(m_narrow, mw_second)
        m = _join_query_halves(mw_first, m_narrow)
    else:
        m_narrow = jnp.maximum(m_narrow, mw_first)
        m = _join_query_halves(m_narrow, mw_second)
    aw_first, aw_second = _query_halves(pv_wide(jnp.exp2(s_wide - m).astype(BF16)))
    a_narrow = pv_narrow(jnp.exp2(s_narrow - m_narrow).astype(BF16))
    if causal:
        return m, _join_query_halves(aw_first, aw_second + a_narrow)
    return m, _join_query_halves(aw_first + a_narrow, aw_second)


def _flash_pipelined(own, streams, s_sc, m_ref, acc_ref):
    has_prev = jnp.where(own > 0, 1.0, 0.0).astype(F32)
    prev = jnp.maximum(own - 1, 0)
    n_far = jnp.maximum(own - 1, 0)

    def absorb(g, s, kt, c_row, w_row):
        u = jnp.max(s, axis=0, keepdims=True) + c_row
        m_old = m_ref[g]
        m_new = jnp.maximum(m_old, jnp.where(w_row > 0.0, u, NEG_INF))
        alpha = jnp.exp2(m_old - m_new)
        shift = jnp.maximum(m_new, u) - c_row
        p = jnp.exp2(s - shift).astype(BF16)
        acc_ref[g] = alpha * acc_ref[g] + w_row * streams[g]["pv"](kt)(p)
        m_ref[g] = m_new

    def absorb_slot(g, i):
        is_prev = i == 0
        kt = jnp.where(is_prev, prev, i - 1)
        c_row = jnp.where(is_prev, 0.0, streams[g]["c_far"])
        w_row = streams[g]["w"](kt) * jnp.where(is_prev, has_prev, 1.0)
        absorb(g, s_sc[g], kt, c_row, w_row)

    def body(i, carry):
        for g in reversed(range(len(streams))):
            nxt = streams[g]["far"](i)
            absorb_slot(g, i)
            s_sc[g] = nxt
        return carry

    lax.fori_loop(0, n_far, body, 0)
    for g in range(len(streams)):
        absorb_slot(g, n_far)


def _run_ahead(jobs, depth=RUN_AHEAD):
    pending = [job[0]() for job in jobs[:depth]]
    for k, (_, consume) in enumerate(jobs):
        if k + depth < len(jobs):
            pending.append(jobs[k + depth][0]())
        consume(pending.pop(0))


def _flash_start_jobs(g, stream, s_sc, m_ref, acc_ref):
    def init(scores):
        m_ref[g], acc_ref[g] = stream["own_softmax"](scores)

    def park(scores):
        s_sc[g] = scores

    return [(stream["own_scores"], init), (stream["prev"], park)]


def _softmax_av(s_list, pv_list):
    m = s_list[0].max(axis=0, keepdims=True)
    for s in s_list[1:]:
        m = jnp.maximum(m, s.max(axis=0, keepdims=True))
    acc = None
    for s, pv in zip(s_list, pv_list):
        part = pv(jnp.exp2(s - m).astype(BF16))
        acc = part if acc is None else acc + part
    return acc


def _normalize(acc):
    return acc[:HEAD_DIM] * (1.0 / acc[HEAD_DIM:HEAD_DIM + 1])


def _rank_before(score, n_cand):
    ranks = []
    for r0 in range(0, score.shape[0], SUBLANES):
        tile = score[r0:r0 + SUBLANES]
        blk = lax.broadcasted_iota(jnp.int32, tile.shape, 0) + r0
        rank = jnp.zeros(tile.shape, F32)
        for m in range(n_cand):
            row = score[m:m + 1, :]
            if m < r0:
                before = jnp.where(row >= tile, 1.0, 0.0)
            elif m >= r0 + SUBLANES:
                before = jnp.where(row > tile, 1.0, 0.0)
            else:
                before = jnp.where(blk > m, jnp.where(row >= tile, 1.0, 0.0), jnp.where(row > tile, 1.0, 0.0))
            rank = rank + before
        ranks.append(rank)
    return jnp.concatenate(ranks, axis=0)


def _nsa_steps(qi, maybe_first, q_ref, gn_ref, kc_ref, vct_ref, ks_ref, vs_ref, kw_ref, vw_ref,
               bct_ref, tt_ref, twt_ref, cfar_ref, ovt_ref, o_ref,
               vst_sc, vwt_sc, qa_sc, og_sc, acc_sc):
    nt = ks_ref.shape[1]
    n_sel = ovt_ref.shape[0]

    def transpose_v():
        for kt in range(nt):
            vs_t = vs_ref[0, kt].astype(F32).T
            vw_t = vw_ref[0, kt].astype(F32).T
            for g in range(NSA_GROUPS):
                _store_v_aug(vst_sc, (kt, g), vs_t[g * HEAD_DIM:(g + 1) * HEAD_DIM])
                _store_v_aug(vwt_sc, (kt, g), vw_t[g * HEAD_DIM:(g + 1) * HEAD_DIM])

    if maybe_first:
        pl.when(qi == 0)(transpose_v)

    pos = lax.broadcasted_iota(jnp.int32, (1, TILE), 1) + qi * TILE
    cur = pos // SEL_BLOCK
    has_cmp = pos >= CMP_LEN - 1
    blk = lax.broadcasted_iota(jnp.int32, (n_sel, TILE), 0)
    prev = jnp.maximum(qi - 1, 0)
    gates = gn_ref[...]

    gsls = [slice(g * HEAD_DIM, (g + 1) * HEAD_DIM) for g in range(NSA_GROUPS)]
    group_heads = [[g * NSA_HPG + j for j in range(NSA_HPG)] for g in range(NSA_GROUPS)]

    def gate(g, branch):
        return _lane_cat([gates[3 * h + branch:3 * h + branch + 1, :] for h in group_heads[g]])

    ones_row = jnp.ones((1, NSA_HPG * TILE), F32)

    def sel_stream(g):
        def qk(kt):
            return _dot(ks_ref[0, kt, :, g * 2 * HEAD_DIM:(g + 1) * 2 * HEAD_DIM], qa_sc[g])

        def own_scores():
            return _triangle_scores(
                lambda lo, hi: ks_ref[0, qi, lo:hi, g * 2 * HEAD_DIM:(g + 1) * 2 * HEAD_DIM], qa_sc[g],
                lambda lo, hi: tt_ref[g, 0, lo:hi, :], causal=True)

        def own_softmax(scores):
            return _triangle_softmax(
                scores, lambda pr: _dot(vst_sc[qi, g, :, 0:HALF], pr),
                lambda pr: _dot(vst_sc[qi, g, :, HALF:TILE], pr), causal=True)

        return dict(own_scores=own_scores, own_softmax=own_softmax, prev=lambda: qk(prev) + tt_ref[g, 1], far=qk,
                    c_far=cfar_ref[g], w=lambda kt: ones_row,
                    pv=lambda kt: (lambda pr: _dot(vst_sc[kt, g], pr)))

    def group_jobs(g):
        heads = group_heads[g]
        win = {}

        def v_half(kt, lo, hi):
            return lambda pr: _dot(vwt_sc[kt, g, :, lo:hi], pr)

        def window_own_scores():
            q4 = _lane_cat([q_ref[h * HEAD_DIM:(h + 1) * HEAD_DIM, :] for h in heads])
            qa_sc[g, 0:HEAD_DIM, :] = q4
            qa_sc[g, HEAD_DIM + n_sel:, :] = jnp.zeros((HEAD_DIM - n_sel, NSA_HPG * TILE), BF16)
            return _triangle_scores(lambda lo, hi: kw_ref[0, qi, lo:hi, gsls[g]], q4,
                                    lambda lo, hi: tt_ref[g, 0, lo:hi, :], causal=True)

        def window_own(scores):
            win["own"] = _triangle_softmax(scores, v_half(qi, 0, HALF), v_half(qi, HALF, TILE), causal=True)

        def window_prev_scores():
            return _triangle_scores(lambda lo, hi: kw_ref[0, prev, lo:hi, gsls[g]], qa_sc[g, 0:HEAD_DIM, :],
                                    lambda lo, hi: twt_ref[g, lo:hi, :], causal=False)

        def window_prev(scores):
            m_prev, acc_prev = _triangle_softmax(scores, v_half(prev, 0, HALF), v_half(prev, HALF, TILE), causal=False)
            m_prev = jnp.where(qi == 0, NEG_INF, m_prev)
            m_own, acc_own = win["own"]
            m_win = jnp.maximum(m_own, m_prev)
            acc_w = acc_own * jnp.exp2(m_own - m_win) + acc_prev * jnp.exp2(m_prev - m_win)
            og_sc[g] = gate(g, 2) * _normalize(acc_w)

        def compressed_scores():
            rows_per_tile = TILE // CMP_STRIDE
            c_rows = pl.ds(pl.multiple_of((nt - 1 - qi) * rows_per_tile, rows_per_tile), N_CMP_PAD)
            return (_dot(kc_ref[0, :, gsls[g]], qa_sc[g, 0:HEAD_DIM, :])
                    + _lane_cat([bct_ref[h, c_rows, :] for h in heads]))

        def compressed(s):
            e = jnp.exp2(s - jnp.max(s, axis=0, keepdims=True))
            p = e * jnp.where(_lane_cat([has_cmp] * NSA_HPG), 1.0 / jnp.sum(e, axis=0, keepdims=True), 0.0)
            psum = p[:, :TILE]
            for j in range(1, NSA_HPG):
                psum = psum + p[:, j * TILE:(j + 1) * TILE]
            og_sc[g] = og_sc[g] + gate(g, 0) * _dot(vct_ref[0, gsls[g], :], p.astype(BF16))
            win["psum"] = psum

        def importance():
            p_hi, p_lo = _split_bf16(win["psum"])
            return _dot(ovt_ref[...], p_hi) + _dot(ovt_ref[...], p_lo)

        def select(imp):
            forced = (blk == 0) | (blk == cur) | (blk == cur - 1)
            score = jnp.where(forced, FORCE_SCORE, jnp.where(blk <= cur, imp, NEG_INF))
            rank = _rank_before(score, n_sel)
            sel = jnp.where(rank < SEL_TOPN, jnp.where(score > NEG_INF / 2, 0.0, NEG_INF), NEG_INF)
            qa_sc[g, HEAD_DIM:HEAD_DIM + n_sel, :] = _lane_cat([sel.astype(BF16)] * NSA_HPG)

        return [(window_own_scores, window_own), (window_prev_scores, window_prev), (compressed_scores, compressed),
                (importance, select)]

    per_group = [group_jobs(g) for g in range(NSA_GROUPS)]
    assert NSA_GROUPS == 2 and RUN_AHEAD <= 2
    (a0, b0, c0, i0), (a1, b1, c1, i1) = per_group
    yield [a0, b0, c0, a1, b1, i0, c1], i1
    yield [sel_stream(g) for g in range(NSA_GROUPS)]

    for g in range(NSA_GROUPS):
        o = og_sc[g] + gate(g, 1) * _normalize(acc_sc[g])
        for j, h in enumerate(group_heads[g]):
            o_ref[h * HEAD_DIM:(h + 1) * HEAD_DIM, :] = o[:, j * TILE:(j + 1) * TILE].astype(BF16)


def _moba_steps(c, maybe_first, qm_ref, km_ref, vm_ref, qx_ref, mk_ref, mv_ref, tt_ref, cfar_ref, om_ref, ox_ref,
                vmt_sc, mvt_sc, kmean_sc, qbd_sc, sel_sc, acc_sc, slot):
    nt = km_ref.shape[1]
    hsls = [slice(h * HEAD_DIM, (h + 1) * HEAD_DIM) for h in range(MOBA_HEADS)]

    def per_sequence():
        kmean_sc[...] = jnp.zeros(kmean_sc.shape, F32)
        for n in range(nt):
            kmean_sc[n:n + 1, :] = jnp.sum(km_ref[0, n].astype(F32), axis=0, keepdims=True) * (1.0 / MOBA_BLOCK)
            vt = vm_ref[0, n].astype(F32).T
            for h in range(MOBA_HEADS):
                _store_v_aug(vmt_sc, (n, h), vt[hsls[h]])
        mvt = mv_ref[0].astype(F32).T
        for h in range(MOBA_HEADS):
            _store_v_aug(mvt_sc, h, mvt[hsls[h]])

    if maybe_first:
        pl.when(c == 0)(per_sequence)

    row_head = lax.broadcasted_iota(jnp.int32, (MOBA_W, TILE), 0) // HEAD_DIM

    def block_diag(q_ref):
        q_t = q_ref[...].astype(F32)
        return _lane_cat([jnp.where(row_head == h, q_t, 0.0) for h in range(MOBA_HEADS)]).astype(BF16)

    def per_head_pv(vts):
        return lambda pr: _lane_cat([_dot(vts(h), pr[:, h * TILE:(h + 1) * TILE]) for h in range(MOBA_HEADS)])

    def store_heads(o_t, out_ref):
        for h in range(MOBA_HEADS):
            out_ref[hsls[h], :] = o_t[:, h * TILE:(h + 1) * TILE].astype(BF16)

    n_rows = -(-nt // SUBLANES) * SUBLANES

    def gate_scores():
        qbd = block_diag(qm_ref)
        qbd_sc[...] = qbd
        km_hi, km_lo = _split_bf16(kmean_sc[...])
        return (_dot(km_hi, qbd) + _dot(km_lo, qbd))[:n_rows]

    def select(gate):
        blk = lax.broadcasted_iota(jnp.int32, gate.shape, 0)
        score = jnp.where(blk < c, gate, NEG_INF * Q_SCALE)
        rank = _rank_before(score, nt)
        sel_sc[0:n_rows, :] = jnp.where(rank < MOBA_TOPK, jnp.where(score > NEG_INF * Q_SCALE / 2, 1.0, 0.0), 0.0)

    qk = lambda n: _dot(km_ref[0, n], qbd_sc[...])
    def own_pv(lo, hi):
        def pv(pr):
            width = pr.shape[1] // MOBA_HEADS
            return _lane_cat([_dot(vmt_sc[c, h, :, lo:hi], pr[:, h * width:(h + 1) * width])
                              for h in range(MOBA_HEADS)])
        return pv

    def own_scores():
        return _triangle_scores(lambda lo, hi: km_ref[0, c, lo:hi, :], qbd_sc[...],
                                lambda lo, hi: tt_ref[0, lo:hi, :], causal=True)

    def own_softmax(scores):
        return _triangle_softmax(scores, own_pv(0, HALF), own_pv(HALF, TILE), causal=True)

    stream = dict(own_scores=own_scores, own_softmax=own_softmax,
                  prev=lambda: qk(jnp.maximum(c - 1, 0)) + tt_ref[1], far=qk,
                  c_far=cfar_ref[...], w=lambda n: sel_sc[pl.ds(n, 1), :],
                  pv=lambda n: per_head_pv(lambda h: vmt_sc[n, h]))
    def memory(s):
        store_heads(_normalize(_softmax_av([s], [per_head_pv(lambda h: mvt_sc[h])])), ox_ref)

    yield [(gate_scores, select), (lambda: _dot(mk_ref[0], block_diag(qx_ref)), memory)]
    yield [stream]
    store_heads(_normalize(acc_sc[slot]), om_ref)


N_STREAMS = NSA_GROUPS + 1
TILES_PER_STEP = 2
NSA_ROW_INPUTS = (0, 1)
MOBA_ROW_INPUTS = (0, 3)


def _attention_kernel(*refs, n_nsa_in, n_moba_in, n_nsa_scratch, n_moba_scratch):
    nsa_in, refs = refs[:n_nsa_in], refs[n_nsa_in:]
    moba_in, refs = refs[:n_moba_in], refs[n_moba_in:]
    outs, refs = refs[:3], refs[3:]
    nsa_sc, refs = refs[:n_nsa_scratch], refs[n_nsa_scratch:]
    moba_sc, refs = refs[:n_moba_scratch], refs[n_moba_scratch:]
    s_sc, m_sc, acc_sc = refs
    for t in range(TILES_PER_STEP):
        qi = pl.program_id(1) * TILES_PER_STEP + t
        cols = lambda ref: ref.at[:, t * TILE:(t + 1) * TILE]
        o_nsa, o_moba, o_mem = [cols(o) for o in outs]
        nsa_refs = [cols(r) if k in NSA_ROW_INPUTS else r for k, r in enumerate(nsa_in)]
        moba_refs = [cols(r) if k in MOBA_ROW_INPUTS else r for k, r in enumerate(moba_in)]
        nsa = _nsa_steps(qi, t == 0, *nsa_refs, o_nsa, *nsa_sc, acc_sc)
        moba = _moba_steps(qi, t == 0, *moba_refs, o_moba, o_mem, *moba_sc, acc_sc, NSA_GROUPS)
        nsa_jobs, last_selection = next(nsa)
        jobs = nsa_jobs + next(moba) + [last_selection]
        streams = next(nsa) + next(moba)
        jobs += [job for g, stream in enumerate(streams) for job in _flash_start_jobs(g, stream, s_sc, m_sc, acc_sc)]
        _run_ahead(jobs)
        _flash_pipelined(qi, streams, s_sc, m_sc, acc_sc)
        for steps in (nsa, moba):
            for _ in steps:
                pass


def _attention(b, s, qn, gn, kc, vct, ks, vs, kw, vw, bias_cmp, t_nsa, t_win, c_far_nsa, ovt,
               qm, km, vm, qx, mk, mv, t_moba, c_far_moba):
    nt = s // TILE
    mem_len = mk.shape[0] // b
    assert MOBA_TOPK <= nt - 1 and nt <= BF16_ROWS
    n_lanes = NSA_HPG * TILE
    assert MOBA_HEADS * TILE == n_lanes
    assert nt % TILES_PER_STEP == 0
    steps = nt // TILES_PER_STEP
    row_spec = lambda w: pl.BlockSpec((w, TILES_PER_STEP * TILE), lambda i, j: (0, i * steps + j))
    seq_spec = lambda w: pl.BlockSpec((1, nt, TILE, w), lambda i, j: (i, 0, 0, 0))
    per_batch = lambda rows, w: pl.BlockSpec((1, rows, w), lambda i, j: (i, 0, 0))
    tiles = lambda a: a.reshape(b, nt, TILE, a.shape[-1])
    nsa_in = [(qn, row_spec(NSA_Q_W)), (gn, row_spec(GATE_PAD)),
              (kc, per_batch(N_CMP_PAD, NSA_KV_W)), (vct, per_batch(N_CMP_PAD, NSA_KV_W)),
              (tiles(ks), seq_spec(KS_AUG_W)), (tiles(vs), seq_spec(NSA_KV_W)),
              (tiles(kw), seq_spec(NSA_KV_W)), (tiles(vw), seq_spec(NSA_KV_W)),
              (bias_cmp, _const_spec(bias_cmp.shape)), (t_nsa, _const_spec(t_nsa.shape)),
              (t_win, _const_spec(t_win.shape)), (c_far_nsa, _const_spec(c_far_nsa.shape)),
              (ovt, _const_spec(ovt.shape))]
    moba_in = [(qm, row_spec(MOBA_W)), (tiles(km), seq_spec(MOBA_W)), (tiles(vm), seq_spec(MOBA_W)),
               (qx, row_spec(MEM_W)),
               (mk.reshape(b, mem_len, MEM_W), per_batch(mem_len, MEM_W)),
               (mv.reshape(b, mem_len, MEM_W), per_batch(mem_len, MEM_W)),
               (t_moba, _const_spec(t_moba.shape)), (c_far_moba, _const_spec(c_far_moba.shape))]
    nsa_scratch =[pltpu.VMEM((nt, NSA_GROUPS, V_AUG, TILE), BF16),
                   pltpu.VMEM((nt, NSA_GROUPS, V_AUG, TILE), BF16),
                   pltpu.VMEM((NSA_GROUPS, 2 * HEAD_DIM, n_lanes), BF16),
                   pltpu.VMEM((NSA_GROUPS, HEAD_DIM, n_lanes), F32)]
    moba_scratch = [pltpu.VMEM((nt, MOBA_HEADS, V_AUG, TILE), BF16),
                    pltpu.VMEM((MEM_HEADS, V_AUG, mem_len), BF16),
                    pltpu.VMEM((BF16_ROWS, MOBA_W), F32),
                    pltpu.VMEM((MOBA_W, n_lanes), BF16),
                    pltpu.VMEM((BF16_ROWS, n_lanes), F32)]
    shared_scratch =[pltpu.VMEM((N_STREAMS, TILE, n_lanes), F32),
                      pltpu.VMEM((N_STREAMS, 1, n_lanes), F32),
                      pltpu.VMEM((N_STREAMS, V_AUG, n_lanes), F32)]
    inputs = nsa_in + moba_in
    return pl.pallas_call(
        functools.partial(_attention_kernel, n_nsa_in=len(nsa_in), n_moba_in=len(moba_in),
                          n_nsa_scratch=len(nsa_scratch), n_moba_scratch=len(moba_scratch)),
        grid=(b, steps),
        in_specs=[spec for _, spec in inputs],
        out_specs=[row_spec(NSA_Q_W), row_spec(MOBA_W), row_spec(MEM_W)],
        out_shape=[jax.ShapeDtypeStruct((w, b * s), BF16) for w in (NSA_Q_W, MOBA_W, MEM_W)],
        scratch_shapes=nsa_scratch + moba_scratch + shared_scratch,
        compiler_params=_params(("arbitrary", "arbitrary")),
        name="attention",
    )(*[a for a, _ in inputs])


MIX_CHUNK = MXU_COLS


def _mix_kernel(x_ref, on_ref, om_ref, ox_ref, g_pre_ref, g_post_ref, wg_ref, wn_ref, wm_ref, wx_ref,
                wo_ref, o_ref, merged_sc):
    x = x_ref[...]
    h = _rms(x, g_pre_ref[...]).astype(BF16)
    rows = lambda o_ref: o_ref[...].astype(F32).T.astype(BF16)
    branches = ((rows(on_ref), wn_ref), (rows(om_ref), wm_ref), (rows(ox_ref), wx_ref))

    def chunk_job(c0):
        cols = slice(c0, c0 + MIX_CHUNK)

        def produce():
            gates = [_dot(h, wg_ref[:, i * D_MODEL + c0:i * D_MODEL + c0 + MIX_CHUNK]) for i in range(len(branches))]
            return list(zip(gates, [_dot(o, w_ref[:, cols]) for o, w_ref in branches]))

        def consume(pairs):
            merged_sc[:, cols] = sum(jax.nn.sigmoid(g) * y for g, y in pairs).astype(BF16)

        return produce, consume

    _run_ahead([chunk_job(c0) for c0 in range(0, D_MODEL, MIX_CHUNK)], depth=1)
    y = _dot(merged_sc[...], wo_ref[...])
    o_ref[...] = x + _rms(y, g_post_ref[...])


def _mix(x2, o_nsa, o_moba, o_mem, g_pre, g_post, w_gates, w_nsa_o, w_moba_o, w_mem_o, w_mix_out, tm=512):
    m = x2.shape[0]
    row = lambda w: pl.BlockSpec((tm, w), lambda i: (i, 0))
    col = lambda w: pl.BlockSpec((w, tm), lambda i: (0, i))
    return pl.pallas_call(
        _mix_kernel,
        grid=(m // tm,),
        in_specs=[row(D_MODEL), col(NSA_Q_W), col(MOBA_W), col(MEM_W),
                  _const_spec((1, D_MODEL)), _const_spec((1, D_MODEL)),
                  _const_spec(w_gates.shape), _const_spec(w_nsa_o.shape), _const_spec(w_moba_o.shape),
                  _const_spec(w_mem_o.shape), _const_spec(w_mix_out.shape)],
        out_specs=row(D_MODEL),
        out_shape=jax.ShapeDtypeStruct((m, D_MODEL), F32),
        scratch_shapes=[pltpu.VMEM((tm, D_MODEL), BF16)],
        compiler_params=_params(("parallel",)),
        name="mix",
    )(x2, o_nsa, o_moba, o_mem, g_pre, g_post, w_gates, w_nsa_o, w_moba_o, w_mem_o, w_mix_out)


FFN_CHUNK = 256


def _ffn_kernel(x_ref, g_pre_ref, g_post_ref, wg_ref, wu_ref, wd_ref, o_ref, a_sc):
    x = x_ref[...]
    h = _rms(x, g_pre_ref[...]).astype(BF16)
    d_ff = wg_ref.shape[1]
    for j in range(d_ff // FFN_CHUNK):
        sl = slice(j * FFN_CHUNK, (j + 1) * FFN_CHUNK)
        a_sc[:, sl] = (jax.nn.silu(_dot(h, wg_ref[:, sl])) * _dot(h, wu_ref[:, sl])).astype(BF16)
    f = _dot(a_sc[...], wd_ref[...])
    o_ref[...] = x + _rms(f, g_post_ref[...])


def _ffn(x2, g_pre, g_post, wg, wu, wd, tm=512):
    m = x2.shape[0]
    d_ff = wg.shape[1]
    return pl.pallas_call(
        _ffn_kernel,
        grid=(m // tm,),
        in_specs=[pl.BlockSpec((tm, D_MODEL), lambda i: (i, 0)),
                  _const_spec((1, D_MODEL)), _const_spec((1, D_MODEL)),
                  _const_spec(wg.shape), _const_spec(wu.shape), _const_spec(wd.shape)],
        out_specs=pl.BlockSpec((tm, D_MODEL), lambda i: (i, 0)),
        out_shape=jax.ShapeDtypeStruct((m, D_MODEL), F32),
        scratch_shapes=[pltpu.VMEM((tm, d_ff), BF16)],
        compiler_params=_params(("parallel",)),
        name="ffn",
    )(x2, g_pre, g_post, wg, wu, wd)


def kernel(x, mem, rel_bias, pre_mix_g, mem_norm_g, post_mix_g, w_in, cmp_pos_k, cmp_w1_k, cmp_w2_k, cmp_pos_v, cmp_w1_v, cmp_w2_v, w_mem_kv, w_nsa_o, w_moba_o, w_mem_o, w_mix_out, pre_ffn_g, post_ffn_g, w_ffn_gate, w_ffn_up, w_ffn_down):
    b, s, d_model = x.shape
    depth = w_in.shape[0]
    assert d_model == D_MODEL and s % TILE == 0 and TILE == MOBA_BLOCK == WINDOW
    assert (s - CMP_LEN) // CMP_STRIDE + 1 < N_CMP_PAD and (s // SEL_BLOCK) % SUBLANES == 0 and s // SEL_BLOCK <= HEAD_DIM
    assert w_in.shape[2] == ATT_W + 3 * D_MODEL and rel_bias.shape == (REL_BUCKETS, N_BIAS_HEADS)

    tile_idx, win_idx, cmp_idx = _bucket_tables(s)
    rel_bias = rel_bias.astype(F32)
    t_nsa = _expand(tile_idx, rel_bias, 0, NSA_HEADS, NSA_HPG).reshape(NSA_GROUPS, 2, TILE, NSA_HPG * TILE)
    t_moba = _expand(tile_idx, rel_bias, NSA_HEADS, MOBA_HEADS, MOBA_HEADS).reshape(2, TILE, MOBA_HEADS * TILE)
    t_win = _expand(win_idx, rel_bias, 0, NSA_HEADS, NSA_HPG)
    b_cmp = _expand(cmp_idx, rel_bias, 0, NSA_HEADS)
    c_far = jnp.repeat(rel_bias[REL_BUCKETS - 1] * LOG2E, TILE)
    c_far_nsa = c_far[:NSA_HEADS * TILE].reshape(NSA_GROUPS, 1, NSA_HPG * TILE)
    c_far_moba = c_far[NSA_HEADS * TILE:].reshape(1, MOBA_HEADS * TILE)
    ovt = _overlap_table(s)
    sel_cols = np.zeros((s, HEAD_DIM), np.float32)
    sel_cols[np.arange(s), np.arange(s) // SEL_BLOCK] = 1.0
    sel_cols = jnp.asarray(sel_cols, BF16)
    gate_lo = NSA_Q_W + 6 * NSA_KV_W
    rows_per_chunk = CMP_STRIDE * NSA_KV_W

    x2 = x.reshape(b * s, D_MODEL)
    mem2 = mem.reshape(-1, D_MODEL)
    for l in range(depth):
        w_att = jnp.concatenate(
            [w_in[l, :, :gate_lo + NSA_GATE_W],
             jnp.zeros((D_MODEL, GATE_PAD - NSA_GATE_W), w_in.dtype),
             w_in[l, :, gate_lo + NSA_GATE_W:ATT_W]], axis=1).astype(BF16)
        w_gates = w_in[l, :, ATT_W:].astype(BF16)
        row = lambda v: v[l].reshape(1, D_MODEL)

        qn, kc_raw, vc_raw, ks, vs, kw, vw, gn, qm, km, vm, qx = _inproj(x2, row(pre_mix_g), w_att, sel_cols)

        pk, w1k = _compress_weights(cmp_pos_k[l], cmp_w1_k[l])
        pv, w1v = _compress_weights(cmp_pos_v[l], cmp_w1_v[l])
        kc, vct = _compress(kc_raw.reshape(b, s // CMP_STRIDE, rows_per_chunk),
                            vc_raw.reshape(b, s // CMP_STRIDE, rows_per_chunk),
                            pk, pv, w1k, w1v, cmp_w2_k[l].astype(BF16), cmp_w2_v[l].astype(BF16))

        mk, mv = _memkv(mem2, row(mem_norm_g), w_mem_kv[l].astype(BF16))

        o_nsa, o_moba, o_mem = _attention(b, s, qn, gn, kc, vct, ks, vs, kw, vw, b_cmp, t_nsa, t_win, c_far_nsa, ovt,
                                          qm, km, vm, qx, mk, mv, t_moba, c_far_moba)

        x2 = _mix(x2, o_nsa, o_moba, o_mem, row(pre_mix_g), row(post_mix_g), w_gates,
                  w_nsa_o[l].astype(BF16), w_moba_o[l].astype(BF16), w_mem_o[l].astype(BF16),
                  w_mix_out[l].astype(BF16))
        x2 = _ffn(x2, row(pre_ffn_g), row(post_ffn_g), w_ffn_gate[l].astype(BF16),
                  w_ffn_up[l].astype(BF16), w_ffn_down[l].astype(BF16))
    return x2.reshape(b, s, D_MODEL)
```

```python
import functools
import math

import numpy as np
import jax
import jax.numpy as jnp
from jax import lax
from jax.experimental import pallas as pl
from jax.experimental.pallas import tpu as pltpu

F32 = jnp.float32
BF16 = jnp.bfloat16

D_MODEL = 1024
HEAD_DIM = 64
SCALE = HEAD_DIM ** -0.5
LOG2E = math.log2(math.e)
Q_SCALE = SCALE * LOG2E
NSA_HEADS = 8
NSA_GROUPS = 2
NSA_HPG = NSA_HEADS // NSA_GROUPS
CMP_LEN = 32
CMP_STRIDE = 16
CMP_HIDDEN = 128
SEL_BLOCK = 64
SEL_TOPN = 8
WINDOW = 256
MOBA_HEADS = 4
MOBA_BLOCK = 256
MOBA_TOPK = 3
MEM_HEADS = 4
REL_BUCKETS = 32
REL_MAX_DIST = 128
N_BIAS_HEADS = NSA_HEADS + MOBA_HEADS
RMS_EPS = 1e-6
NEG_INF = -1e30
FORCE_SCORE = 1e4

NSA_Q_W = NSA_HEADS * HEAD_DIM
NSA_KV_W = NSA_GROUPS * HEAD_DIM
NSA_GATE_W = NSA_HEADS * 3
MOBA_W = MOBA_HEADS * HEAD_DIM
MEM_W = MEM_HEADS * HEAD_DIM
ATT_W = NSA_Q_W + 6 * NSA_KV_W + NSA_GATE_W + 3 * MOBA_W + MEM_W
LANES = 128
SUBLANES = 8
BF16_ROWS = 16
MXU_COLS = 256
GATE_PAD = LANES
TILE = 256
HALF = TILE // 2
RUN_AHEAD = 2
N_CMP_PAD = 128
V_AUG = HEAD_DIM + BF16_ROWS
MASKED_BUCKET = REL_BUCKETS
VMEM_LIMIT = 56 * 1024 * 1024


def _dot(a, b):
    return jnp.dot(a, b, preferred_element_type=F32)


def _split_bf16(x):
    hi = x.astype(BF16)
    lo = (x - hi.astype(F32)).astype(BF16)
    return hi, lo


def _rms(x, g):
    return x * lax.rsqrt(jnp.mean(x * x, axis=-1, keepdims=True) + RMS_EPS) * g


def _params(sem):
    return pltpu.CompilerParams(dimension_semantics=sem, vmem_limit_bytes=VMEM_LIMIT)


def _const_spec(shape):
    nd = len(shape)
    return pl.BlockSpec(shape, lambda *_: (0,) * nd, pipeline_mode=pl.Buffered(1))


_INPROJ_OUTS = (
    ("qn", NSA_Q_W, BF16, True),
    ("kc", NSA_KV_W, F32, False), ("vc", NSA_KV_W, F32, False),
    ("ks", NSA_KV_W, BF16, False), ("vs", NSA_KV_W, BF16, False),
    ("kw", NSA_KV_W, BF16, False), ("vw", NSA_KV_W, BF16, False),
    ("gn", GATE_PAD, F32, False),
    ("qm", MOBA_W, BF16, True), ("km", MOBA_W, BF16, False), ("vm", MOBA_W, BF16, False),
    ("qx", MEM_W, BF16, True),
)
_INPROJ_W = sum(o[1] for o in _INPROJ_OUTS)
_INPROJ_CHUNKED = ("kc", "vc")
_INPROJ_TRANSPOSED = ("qn", "gn", "qm", "qx")
KS_AUG_W = NSA_GROUPS * 2 * HEAD_DIM


def _inproj_out_width(name, width):
    return KS_AUG_W if name == "ks" else width


def _inproj_kernel(x_ref, g_ref, w_ref, e_ref, *refs):
    out_refs, rows_sc = refs[:-1], refs[-1]
    runs, lo = [], 0
    for out in zip(_INPROJ_OUTS, out_refs):
        if runs and runs[-1][1] < MXU_COLS:
            runs[-1][0].append(out)
            runs[-1][1] += out[0][1]
        else:
            runs.append([[out], out[0][1], lo])
        lo += out[0][1]
    tm = rows_sc.shape[1]
    for t in range(rows_sc.shape[0]):
        tile = slice(t * tm, (t + 1) * tm)
        h = _rms(x_ref[tile, :], g_ref[...]).astype(BF16)
        for outs, run_width, run_lo in runs:
            y_run = _dot(h, w_ref[:, run_lo:run_lo + run_width])
            lo = 0
            for (name, width, dtype, scaled), o_ref in outs:
                y = y_run[:, lo:lo + width]
                if scaled:
                    y = y * Q_SCALE
                if name == "gn":
                    y = jax.nn.sigmoid(y)
                if name in _INPROJ_TRANSPOSED:
                    o_ref[:, tile] = y.T.astype(dtype)
                elif name in _INPROJ_CHUNKED:
                    rows_sc[t] = y
                    n_rows = tm // CMP_STRIDE
                    for j in range(CMP_STRIDE):
                        o_ref[t * n_rows:(t + 1) * n_rows, j * width:(j + 1) * width] = (
                            rows_sc[t, pl.ds(j, n_rows, stride=CMP_STRIDE), :])
                else:
                    y = y.astype(dtype)
                    if name == "ks":
                        e = e_ref[tile, :]
                        y = _lane_cat([y[:, :HEAD_DIM], e, y[:, HEAD_DIM:], e])
                    o_ref[tile, :] = y
                lo += width


def _inproj(x2, g, w, e_cols, tm=512, tiles_per_step=2):
    m = x2.shape[0]
    step = tm * tiles_per_step
    steps_per_seq = e_cols.shape[0] // step
    out_specs, out_shape = [], []
    for name, width, dtype, _ in _INPROJ_OUTS:
        if name in _INPROJ_TRANSPOSED:
            out_specs.append(pl.BlockSpec((width, step), lambda i: (0, i)))
            out_shape.append(jax.ShapeDtypeStruct((width, m), dtype))
            continue
        rows, width = (CMP_STRIDE, CMP_STRIDE * width) if name in _INPROJ_CHUNKED else (1, _inproj_out_width(name, width))
        out_specs.append(pl.BlockSpec((step // rows, width), lambda i: (i, 0)))
        out_shape.append(jax.ShapeDtypeStruct((m // rows, width), dtype))
    return pl.pallas_call(
        _inproj_kernel,
        grid=(m // step,),
        in_specs=[pl.BlockSpec((step, D_MODEL), lambda i: (i, 0)),
                  _const_spec((1, D_MODEL)),
                  _const_spec((D_MODEL, _INPROJ_W)),
                  pl.BlockSpec((step, HEAD_DIM), lambda i: (i % steps_per_seq, 0))],
        out_specs=out_specs,
        out_shape=out_shape,
        scratch_shapes=[pltpu.VMEM((tiles_per_step, tm, NSA_KV_W), F32)],
        compiler_params=_params(("parallel",)),
        name="inproj",
    )(x2, g, w, e_cols)


def _compress_kernel(rk_ref, rv_ref, pk_ref, pv_ref, w1k_ref, w1v_ref, w2k_ref, w2v_ref, kc_ref, vc_ref):
    nb = rk_ref.shape[0]

    def one(r_ref, p_ref, w1_ref, w2_ref):
        r = r_ref[...].reshape(nb * N_CMP_PAD, r_ref.shape[2])
        top = _dot((r + p_ref[0:1, :]).astype(BF16), w1_ref[0])
        bot = _dot((r + p_ref[1:2, :]).astype(BF16), w1_ref[1])
        hid = top + pltpu.roll(bot, nb * N_CMP_PAD - 1, 0)
        act = jax.nn.gelu(hid).astype(BF16)
        return jnp.concatenate(
            [_dot(act[:, g * CMP_HIDDEN:(g + 1) * CMP_HIDDEN], w2_ref[...]) for g in range(NSA_GROUPS)], axis=1)

    k_out = one(rk_ref, pk_ref, w1k_ref, w2k_ref)
    v_out = one(rv_ref, pv_ref, w1v_ref, w2v_ref)
    for n in range(nb):
        rows = slice(n * N_CMP_PAD, (n + 1) * N_CMP_PAD)
        kc_ref[n] = k_out[rows].astype(BF16)
        vc_ref[n] = v_out[rows].T.astype(BF16)


def _compress(rk, rv, pk, pv, w1k, w1v, w2k, w2v):
    b = rk.shape[0]
    rw = rk.shape[2]
    nb = 2 if b % 2 == 0 else 1
    r_spec = pl.BlockSpec((nb, N_CMP_PAD, rw), lambda i: (i, 0, 0))
    o_spec = pl.BlockSpec((nb, N_CMP_PAD, NSA_KV_W), lambda i: (i, 0, 0))
    return pl.pallas_call(
        _compress_kernel,
        grid=(b // nb,),
        in_specs=[r_spec, r_spec, _const_spec(pk.shape), _const_spec(pv.shape),
                  _const_spec(w1k.shape), _const_spec(w1v.shape),
                  _const_spec(w2k.shape), _const_spec(w2v.shape)],
        out_specs=[o_spec, o_spec],
        out_shape=[jax.ShapeDtypeStruct((b, N_CMP_PAD, NSA_KV_W), BF16)] * 2,
        compiler_params=_params(("parallel",)),
        name="compress",
    )(rk, rv, pk, pv, w1k, w1v, w2k, w2v)


def _compress_weights(pos, w1):
    half = CMP_LEN // 2
    p = pos.reshape(2, half, 1, HEAD_DIM)
    p = jnp.broadcast_to(p, (2, half, NSA_GROUPS, HEAD_DIM)).reshape(2, half * NSA_KV_W)
    w = w1.reshape(2, half, HEAD_DIM, CMP_HIDDEN)
    eye = jnp.eye(NSA_GROUPS, dtype=w1.dtype)
    wbd = jnp.einsum("ajdm,gk->ajgdkm", w, eye).reshape(2, half * NSA_KV_W, NSA_GROUPS * CMP_HIDDEN)
    return p.astype(F32), wbd.astype(BF16)


def _memkv_kernel(m_ref, g_ref, w_ref, k_ref, v_ref):
    h = _rms(m_ref[...], g_ref[...]).astype(BF16)
    k_ref[...] = _dot(h, w_ref[:, :MEM_W]).astype(BF16)
    v_ref[...] = _dot(h, w_ref[:, MEM_W:]).astype(BF16)


def _memkv(mem2, g, w, tm=512):
    m = mem2.shape[0]
    tm = min(tm, m)
    o_spec = pl.BlockSpec((tm, MEM_W), lambda i: (i, 0))
    return pl.pallas_call(
        _memkv_kernel,
        grid=(m // tm,),
        in_specs=[pl.BlockSpec((tm, D_MODEL), lambda i: (i, 0)), _const_spec((1, D_MODEL)),
                  _const_spec((D_MODEL, 2 * MEM_W))],
        out_specs=[o_spec, o_spec],
        out_shape=[jax.ShapeDtypeStruct((m, MEM_W), BF16)] * 2,
        compiler_params=_params(("parallel",)),
        name="memkv",
    )(mem2, g, w)


def _expand_kernel(idx_ref, bias_ref, o_ref, *, head0, n_heads, heads_per_group):
    rows, cols = idx_ref.shape

    def body(i, carry):
        r = pl.multiple_of(i * SUBLANES, SUBLANES)
        for c0 in range(0, cols, TILE):
            idx = idx_ref[pl.ds(r, SUBLANES), c0:c0 + TILE]
            out = [jnp.full(idx.shape, NEG_INF, F32)] * n_heads
            for bkt in range(REL_BUCKETS):
                hit = idx == bkt
                out = [jnp.where(hit, bias_ref[bkt, head0 + h], out[h]) for h in range(n_heads)]
            for h in range(n_heads):
                col = (h % heads_per_group) * cols + c0
                o_ref[h // heads_per_group, pl.ds(r, SUBLANES), col:col + TILE] = out[h] * LOG2E
        return carry

    lax.fori_loop(0, rows // SUBLANES, body, 0)


def _expand(idx, rel_bias, head0, n_heads, heads_per_group=1):
    rows, cols = idx.shape
    return pl.pallas_call(
        functools.partial(_expand_kernel, head0=head0, n_heads=n_heads, heads_per_group=heads_per_group),
        in_specs=[pl.BlockSpec(memory_space=pltpu.VMEM), pl.BlockSpec(memory_space=pltpu.SMEM)],
        out_specs=pl.BlockSpec(memory_space=pltpu.VMEM),
        out_shape=jax.ShapeDtypeStruct((n_heads // heads_per_group, rows, heads_per_group * cols), F32),
        compiler_params=pltpu.CompilerParams(vmem_limit_bytes=VMEM_LIMIT),
        name="bias_expand",
    )(idx, rel_bias)


def _t5_bucket_np(dist):
    dist = np.maximum(dist, 0)
    max_exact = REL_BUCKETS // 2
    logd = np.log(np.maximum(dist, 1).astype(np.float32) / max_exact) / math.log(REL_MAX_DIST / max_exact)
    large = np.minimum(max_exact + (logd * (REL_BUCKETS - max_exact)).astype(np.int32), REL_BUCKETS - 1)
    return np.where(dist < max_exact, dist, large).astype(np.int32)


def _bucket_tables(s):
    j = np.arange(TILE)[:, None]
    i = np.arange(TILE)[None, :]
    assert TILE + 1 >= REL_MAX_DIST
    tiles = []
    for d in range(2):
        dist = d * TILE + i - j
        tiles.append(np.where(dist >= 0, _t5_bucket_np(dist), MASKED_BUCKET))
    dist1 = TILE + i - j
    win = np.where(dist1 < WINDOW, _t5_bucket_np(dist1), MASKED_BUCKET)
    n_cmp = (s - CMP_LEN) // CMP_STRIDE + 1
    assert n_cmp * CMP_STRIDE + CMP_LEN - 1 >= s or n_cmp == N_CMP_PAD
    rel = np.arange(N_CMP_PAD + (s - TILE) // CMP_STRIDE)[:, None] - (s - TILE) // CMP_STRIDE
    dist_c = i - (rel * CMP_STRIDE + CMP_LEN - 1)
    cmp_idx = np.where(dist_c >= 0, _t5_bucket_np(dist_c), MASKED_BUCKET)
    as_i32 = lambda a: jnp.asarray(a.astype(np.int32))
    return as_i32(np.concatenate(tiles, axis=0)), as_i32(win), as_i32(cmp_idx)


def _overlap_table(s):
    n_cmp = (s - CMP_LEN) // CMP_STRIDE + 1
    n_sel = s // SEL_BLOCK
    cs = np.arange(n_cmp) * CMP_STRIDE
    ss = np.arange(n_sel) * SEL_BLOCK
    ov = np.clip(np.minimum(cs[:, None] + CMP_LEN, ss[None, :] + SEL_BLOCK)
                 - np.maximum(cs[:, None], ss[None, :]), 0, None).astype(np.float32) / CMP_LEN
    ovt = np.zeros((n_sel, N_CMP_PAD), np.float32)
    ovt[:, :n_cmp] = ov.T
    return jnp.asarray(ovt, BF16)


def _store_v_aug(vt_sc, idx, vt):
    ones = jnp.ones((BF16_ROWS, vt.shape[1]), BF16)
    vt_sc[idx] = jnp.concatenate([vt.astype(BF16), ones], axis=0)


def _lane_cat(xs):
    return jnp.concatenate(xs, axis=1)


def _query_halves(x):
    n = x.shape[-1] // TILE
    first = _lane_cat([x[:, k * TILE:k * TILE + HALF] for k in range(n)])
    second = _lane_cat([x[:, k * TILE + HALF:(k + 1) * TILE] for k in range(n)])
    return first, second


def _join_query_halves(first, second):
    n = first.shape[-1] // HALF
    return _lane_cat([part for k in range(n)
                      for part in (first[:, k * HALF:(k + 1) * HALF], second[:, k * HALF:(k + 1) * HALF])])


def _triangle_scores(k, q, table, causal):
    q_first, q_second = _query_halves(q)
    lo, hi = (0, HALF), (HALF, TILE)
    if causal:
        return _dot(k(*lo), q) + table(*lo), _dot(k(*hi), q_second) + _query_halves(table(*hi))[1]
    return _dot(k(*hi), q) + table(*hi), _dot(k(*lo), q_first) + _query_halves(table(*lo))[0]


def _triangle_softmax(scores, pv_lo, pv_hi, causal):
    s_wide, s_narrow = scores
    pv_wide, pv_narrow = (pv_lo, pv_hi) if causal else (pv_hi, pv_lo)
    mw_first, mw_second = _query_halves(jnp.max(s_wide, axis=0, keepdims=True))
    m_narrow = jnp.max(s_narrow, axis=0, keepdims=True)
    if causal:
        m_narrow = jnp.maximum(m_narrow, mw_second)
        m = _join_query_halves(mw_first, m_narrow)
    else:
        m_narrow = jnp.maximum(m_narrow, mw_first)
        m = _join_query_halves(m_narrow, mw_second)
    aw_first, aw_second = _query_halves(pv_wide(jnp.exp2(s_wide - m).astype(BF16)))
    a_narrow = pv_narrow(jnp.exp2(s_narrow - m_narrow).astype(BF16))
    if causal:
        return m, _join_query_halves(aw_first, aw_second + a_narrow)
    return m, _join_query_halves(aw_first + a_narrow, aw_second)


def _flash_pipelined(own, streams, s_sc, m_ref, acc_ref):
    has_prev = jnp.where(own > 0, 1.0, 0.0).astype(F32)
    prev = jnp.maximum(own - 1, 0)
    n_far = jnp.maximum(own - 1, 0)

    def absorb(g, s, kt, c_row, w_row):
        u = jnp.max(s, axis=0, keepdims=True) + c_row
        m_old = m_ref[g]
        m_new = jnp.maximum(m_old, jnp.where(w_row > 0.0, u, NEG_INF))
        alpha = jnp.exp2(m_old - m_new)
        shift = jnp.maximum(m_new, u) - c_row
        p = jnp.exp2(s - shift).astype(BF16)
        acc_ref[g] = alpha * acc_ref[g] + w_row * streams[g]["pv"](kt)(p)
        m_ref[g] = m_new

    def absorb_slot(g, i):
        is_prev = i == 0
        kt = jnp.where(is_prev, prev, i - 1)
        c_row = jnp.where(is_prev, 0.0, streams[g]["c_far"])
        w_row = streams[g]["w"](kt) * jnp.where(is_prev, has_prev, 1.0)
        absorb(g, s_sc[g], kt, c_row, w_row)

    def body(i, carry):
        for g in reversed(range(len(streams))):
            nxt = streams[g]["far"](i)
            absorb_slot(g, i)
            s_sc[g] = nxt
        return carry

    lax.fori_loop(0, n_far, body, 0)
    for g in range(len(streams)):
        absorb_slot(g, n_far)


def _run_ahead(jobs, depth=RUN_AHEAD):
    pending = [job[0]() for job in jobs[:depth]]
    for k, (_, consume) in enumerate(jobs):
        if k + depth < len(jobs):
            pending.append(jobs[k + depth][0]())
        consume(pending.pop(0))


def _flash_start_jobs(g, stream, s_sc, m_ref, acc_ref):
    def init(scores):
        m_ref[g], acc_ref[g] = stream["own_softmax"](scores)

    def park(scores):
        s_sc[g] = scores

    return [(stream["own_scores"], init), (stream["prev"], park)]


def _softmax_av(s_list, pv_list):
    m = s_list[0].max(axis=0, keepdims=True)
    for s in s_list[1:]:
        m = jnp.maximum(m, s.max(axis=0, keepdims=True))
    acc = None
    for s, pv in zip(s_list, pv_list):
        part = pv(jnp.exp2(s - m).astype(BF16))
        acc = part if acc is None else acc + part
    return acc


def _normalize(acc):
    return acc[:HEAD_DIM] * (1.0 / acc[HEAD_DIM:HEAD_DIM + 1])


def _rank_before(score, n_cand):
    ranks = []
    for r0 in range(0, score.shape[0], SUBLANES):
        tile = score[r0:r0 + SUBLANES]
        blk = lax.broadcasted_iota(jnp.int32, tile.shape, 0) + r0
        rank = jnp.zeros(tile.shape, F32)
        for m in range(n_cand):
            row = score[m:m + 1, :]
            if m < r0:
                before = jnp.where(row >= tile, 1.0, 0.0)
            elif m >= r0 + SUBLANES:
                before = jnp.where(row > tile, 1.0, 0.0)
            else:
                before = jnp.where(blk > m, jnp.where(row >= tile, 1.0, 0.0), jnp.where(row > tile, 1.0, 0.0))
            rank = rank + before
        ranks.append(rank)
    return jnp.concatenate(ranks, axis=0)


def _nsa_steps(qi, maybe_first, q_ref, gn_ref, kc_ref, vct_ref, ks_ref, vs_ref, kw_ref, vw_ref,
               bct_ref, tt_ref, twt_ref, cfar_ref, ovt_ref, o_ref,
               vst_sc, vwt_sc, qa_sc, og_sc, acc_sc):
    nt = ks_ref.shape[1]
    n_sel = ovt_ref.shape[0]

    def transpose_v():
        for kt in range(nt):
            vs_t = vs_ref[0, kt].astype(F32).T
            vw_t = vw_ref[0, kt].astype(F32).T
            for g in range(NSA_GROUPS):
                _store_v_aug(vst_sc, (kt, g), vs_t[g * HEAD_DIM:(g + 1) * HEAD_DIM])
                _store_v_aug(vwt_sc, (kt, g), vw_t[g * HEAD_DIM:(g + 1) * HEAD_DIM])

    if maybe_first:
        pl.when(qi == 0)(transpose_v)

    pos = lax.broadcasted_iota(jnp.int32, (1, TILE), 1) + qi * TILE
    cur = pos // SEL_BLOCK
    has_cmp = pos >= CMP_LEN - 1
    blk = lax.broadcasted_iota(jnp.int32, (n_sel, TILE), 0)
    prev = jnp.maximum(qi - 1, 0)
    gates = gn_ref[...]

    gsls = [slice(g * HEAD_DIM, (g + 1) * HEAD_DIM) for g in range(NSA_GROUPS)]
    group_heads = [[g * NSA_HPG + j for j in range(NSA_HPG)] for g in range(NSA_GROUPS)]

    def gate(g, branch):
        return _lane_cat([gates[3 * h + branch:3 * h + branch + 1, :] for h in group_heads[g]])

    ones_row = jnp.ones((1, NSA_HPG * TILE), F32)

    def sel_stream(g):
        def qk(kt):
            return _dot(ks_ref[0, kt, :, g * 2 * HEAD_DIM:(g + 1) * 2 * HEAD_DIM], qa_sc[g])

        def own_scores():
            return _triangle_scores(
                lambda lo, hi: ks_ref[0, qi, lo:hi, g * 2 * HEAD_DIM:(g + 1) * 2 * HEAD_DIM], qa_sc[g],
                lambda lo, hi: tt_ref[g, 0, lo:hi, :], causal=True)

        def own_softmax(scores):
            return _triangle_softmax(
                scores, lambda pr: _dot(vst_sc[qi, g, :, 0:HALF], pr),
                lambda pr: _dot(vst_sc[qi, g, :, HALF:TILE], pr), causal=True)

        return dict(own_scores=own_scores, own_softmax=own_softmax, prev=lambda: qk(prev) + tt_ref[g, 1], far=qk,
                    c_far=cfar_ref[g], w=lambda kt: ones_row,
                    pv=lambda kt: (lambda pr: _dot(vst_sc[kt, g], pr)))

    def group_jobs(g):
        heads = group_heads[g]
        win = {}

        def v_half(kt, lo, hi):
            return lambda pr: _dot(vwt_sc[kt, g, :, lo:hi], pr)

        def window_own_scores():
            q4 = _lane_cat([q_ref[h * HEAD_DIM:(h + 1) * HEAD_DIM, :] for h in heads])
            qa_sc[g, 0:HEAD_DIM, :] = q4
            qa_sc[g, HEAD_DIM + n_sel:, :] = jnp.zeros((HEAD_DIM - n_sel, NSA_HPG * TILE), BF16)
            return _triangle_scores(lambda lo, hi: kw_ref[0, qi, lo:hi, gsls[g]], q4,
                                    lambda lo, hi: tt_ref[g, 0, lo:hi, :], causal=True)

        def window_own(scores):
            win["own"] = _triangle_softmax(scores, v_half(qi, 0, HALF), v_half(qi, HALF, TILE), causal=True)

        def window_prev_scores():
            return _triangle_scores(lambda lo, hi: kw_ref[0, prev, lo:hi, gsls[g]], qa_sc[g, 0:HEAD_DIM, :],
                                    lambda lo, hi: twt_ref[g, lo:hi, :], causal=False)

        def window_prev(scores):
            m_prev, acc_prev = _triangle_softmax(scores, v_half(prev, 0, HALF), v_half(prev, HALF, TILE), causal=False)
            m_prev = jnp.where(qi == 0, NEG_INF, m_prev)
            m_own, acc_own = win["own"]
            m_win = jnp.maximum(m_own, m_prev)
            acc_w = acc_own * jnp.exp2(m_own - m_win) + acc_prev * jnp.exp2(m_prev - m_win)
            og_sc[g] = gate(g, 2) * _normalize(acc_w)

        def compressed_scores():
            rows_per_tile = TILE // CMP_STRIDE
            c_rows = pl.ds(pl.multiple_of((nt - 1 - qi) * rows_per_tile, rows_per_tile), N_CMP_PAD)
            return (_dot(kc_ref[0, :, gsls[g]], qa_sc[g, 0:HEAD_DIM, :])
                    + _lane_cat([bct_ref[h, c_rows, :] for h in heads]))

        def compressed(s):
            e = jnp.exp2(s - jnp.max(s, axis=0, keepdims=True))
            p = e * jnp.where(_lane_cat([has_cmp] * NSA_HPG), 1.0 / jnp.sum(e, axis=0, keepdims=True), 0.0)
            psum = p[:, :TILE]
            for j in range(1, NSA_HPG):
                psum = psum + p[:, j * TILE:(j + 1) * TILE]
            og_sc[g] = og_sc[g] + gate(g, 0) * _dot(vct_ref[0, gsls[g], :], p.astype(BF16))
            win["psum"] = psum

        def importance():
            p_hi, p_lo = _split_bf16(win["psum"])
            return _dot(ovt_ref[...], p_hi) + _dot(ovt_ref[...], p_lo)

        def select(imp):
            forced = (blk == 0) | (blk == cur) | (blk == cur - 1)
            score = jnp.where(forced, FORCE_SCORE, jnp.where(blk <= cur, imp, NEG_INF))
            rank = _rank_before(score, n_sel)
            sel = jnp.where(rank < SEL_TOPN, jnp.where(score > NEG_INF / 2, 0.0, NEG_INF), NEG_INF)
            qa_sc[g, HEAD_DIM:HEAD_DIM + n_sel, :] = _lane_cat([sel.astype(BF16)] * NSA_HPG)

        return [(window_own_scores, window_own), (window_prev_scores, window_prev), (compressed_scores, compressed),
                (importance, select)]

    per_group = [group_jobs(g) for g in range(NSA_GROUPS)]
    assert NSA_GROUPS == 2 and RUN_AHEAD <= 2
    (a0, b0, c0, i0), (a1, b1, c1, i1) = per_group
    yield [a0, b0, c0, a1, b1, i0, c1], i1
    yield [sel_stream(g) for g in range(NSA_GROUPS)]

    for g in range(NSA_GROUPS):
        o = og_sc[g] + gate(g, 1) * _normalize(acc_sc[g])
        for j, h in enumerate(group_heads[g]):
            o_ref[h * HEAD_DIM:(h + 1) * HEAD_DIM, :] = o[:, j * TILE:(j + 1) * TILE].astype(BF16)


def _moba_steps(c, maybe_first, qm_ref, km_ref, vm_ref, qx_ref, mk_ref, mv_ref, tt_ref, cfar_ref, om_ref, ox_ref,
                vmt_sc, mvt_sc, kmean_sc, qbd_sc, sel_sc, acc_sc, slot):
    nt = km_ref.shape[1]
    hsls = [slice(h * HEAD_DIM, (h + 1) * HEAD_DIM) for h in range(MOBA_HEADS)]

    def per_sequence():
        kmean_sc[...] = jnp.zeros(kmean_sc.shape, F32)
        for n in range(nt):
            kmean_sc[n:n + 1, :] = jnp.sum(km_ref[0, n].astype(F32), axis=0, keepdims=True) * (1.0 / MOBA_BLOCK)
            vt = vm_ref[0, n].astype(F32).T
            for h in range(MOBA_HEADS):
                _store_v_aug(vmt_sc, (n, h), vt[hsls[h]])
        mvt = mv_ref[0].astype(F32).T
        for h in range(MOBA_HEADS):
            _store_v_aug(mvt_sc, h, mvt[hsls[h]])

    if maybe_first:
        pl.when(c == 0)(per_sequence)

    row_head = lax.broadcasted_iota(jnp.int32, (MOBA_W, TILE), 0) // HEAD_DIM

    def block_diag(q_ref):
        q_t = q_ref[...].astype(F32)
        return _lane_cat([jnp.where(row_head == h, q_t, 0.0) for h in range(MOBA_HEADS)]).astype(BF16)

    def per_head_pv(vts):
        return lambda pr: _lane_cat([_dot(vts(h), pr[:, h * TILE:(h + 1) * TILE]) for h in range(MOBA_HEADS)])

    def store_heads(o_t, out_ref):
        for h in range(MOBA_HEADS):
            out_ref[hsls[h], :] = o_t[:, h * TILE:(h + 1) * TILE].astype(BF16)

    n_rows = -(-nt // SUBLANES) * SUBLANES

    def gate_scores():
        qbd = block_diag(qm_ref)
        qbd_sc[...] = qbd
        km_hi, km_lo = _split_bf16(kmean_sc[...])
        return (_dot(km_hi, qbd) + _dot(km_lo, qbd))[:n_rows]

    def select(gate):
        blk = lax.broadcasted_iota(jnp.int32, gate.shape, 0)
        score = jnp.where(blk < c, gate, NEG_INF * Q_SCALE)
        rank = _rank_before(score, nt)
        sel_sc[0:n_rows, :] = jnp.where(rank < MOBA_TOPK, jnp.where(score > NEG_INF * Q_SCALE / 2, 1.0, 0.0), 0.0)

    qk = lambda n: _dot(km_ref[0, n], qbd_sc[...])
    def own_pv(lo, hi):
        def pv(pr):
            width = pr.shape[1] // MOBA_HEADS
            return _lane_cat([_dot(vmt_sc[c, h, :, lo:hi], pr[:, h * width:(h + 1) * width])
                              for h in range(MOBA_HEADS)])
        return pv

    def own_scores():
        return _triangle_scores(lambda lo, hi: km_ref[0, c, lo:hi, :], qbd_sc[...],
                                lambda lo, hi: tt_ref[0, lo:hi, :], causal=True)

    def own_softmax(scores):
        return _triangle_softmax(scores, own_pv(0, HALF), own_pv(HALF, TILE), causal=True)

    stream = dict(own_scores=own_scores, own_softmax=own_softmax,
                  prev=lambda: qk(jnp.maximum(c - 1, 0)) + tt_ref[1], far=qk,
                  c_far=cfar_ref[...], w=lambda n: sel_sc[pl.ds(n, 1), :],
                  pv=lambda n: per_head_pv(lambda h: vmt_sc[n, h]))
    def memory(s):
        store_heads(_normalize(_softmax_av([s], [per_head_pv(lambda h: mvt_sc[h])])), ox_ref)

    yield [(gate_scores, select), (lambda: _dot(mk_ref[0], block_diag(qx_ref)), memory)]
    yield [stream]
    store_heads(_normalize(acc_sc[slot]), om_ref)


N_STREAMS = NSA_GROUPS + 1
TILES_PER_STEP = 2
NSA_ROW_INPUTS = (0, 1)
MOBA_ROW_INPUTS = (0, 3)


def _attention_kernel(*refs, n_nsa_in, n_moba_in, n_nsa_scratch, n_moba_scratch):
    nsa_in, refs = refs[:n_nsa_in], refs[n_nsa_in:]
    moba_in, refs = refs[:n_moba_in], refs[n_moba_in:]
    outs, refs = refs[:3], refs[3:]
    nsa_sc, refs = refs[:n_nsa_scratch], refs[n_nsa_scratch:]
    moba_sc, refs = refs[:n_moba_scratch], refs[n_moba_scratch:]
    s_sc, m_sc, acc_sc = refs
    for t in range(TILES_PER_STEP):
        qi = pl.program_id(1) * TILES_PER_STEP + t
        cols = lambda ref: ref.at[:, t * TILE:(t + 1) * TILE]
        o_nsa, o_moba, o_mem = [cols(o) for o in outs]
        nsa_refs = [cols(r) if k in NSA_ROW_INPUTS else r for k, r in enumerate(nsa_in)]
        moba_refs = [cols(r) if k in MOBA_ROW_INPUTS else r for k, r in enumerate(moba_in)]
        nsa = _nsa_steps(qi, t == 0, *nsa_refs, o_nsa, *nsa_sc, acc_sc)
        moba = _moba_steps(qi, t == 0, *moba_refs, o_moba, o_mem, *moba_sc, acc_sc, NSA_GROUPS)
        nsa_jobs, last_selection = next(nsa)
        jobs = nsa_jobs + next(moba) + [last_selection]
        streams = next(nsa) + next(moba)
        jobs += [job for g, stream in enumerate(streams) for job in _flash_start_jobs(g, stream, s_sc, m_sc, acc_sc)]
        _run_ahead(jobs)
        _flash_pipelined(qi, streams, s_sc, m_sc, acc_sc)
        for steps in (nsa, moba):
            for _ in steps:
                pass


def _attention(b, s, qn, gn, kc, vct, ks, vs, kw, vw, bias_cmp, t_nsa, t_win, c_far_nsa, ovt,
               qm, km, vm, qx, mk, mv, t_moba, c_far_moba):
    nt = s // TILE
    mem_len = mk.shape[0] // b
    assert MOBA_TOPK <= nt - 1 and nt <= BF16_ROWS
    n_lanes = NSA_HPG * TILE
    assert MOBA_HEADS * TILE == n_lanes
    assert nt % TILES_PER_STEP == 0
    steps = nt // TILES_PER_STEP
    row_spec = lambda w: pl.BlockSpec((w, TILES_PER_STEP * TILE), lambda i, j: (0, i * steps + j))
    seq_spec = lambda w: pl.BlockSpec((1, nt, TILE, w), lambda i, j: (i, 0, 0, 0))
    per_batch = lambda rows, w: pl.BlockSpec((1, rows, w), lambda i, j: (i, 0, 0))
    tiles = lambda a: a.reshape(b, nt, TILE, a.shape[-1])
    nsa_in = [(qn, row_spec(NSA_Q_W)), (gn, row_spec(GATE_PAD)),
              (kc, per_batch(N_CMP_PAD, NSA_KV_W)), (vct, per_batch(N_CMP_PAD, NSA_KV_W)),
              (tiles(ks), seq_spec(KS_AUG_W)), (tiles(vs), seq_spec(NSA_KV_W)),
              (tiles(kw), seq_spec(NSA_KV_W)), (tiles(vw), seq_spec(NSA_KV_W)),
              (bias_cmp, _const_spec(bias_cmp.shape)), (t_nsa, _const_spec(t_nsa.shape)),
              (t_win, _const_spec(t_win.shape)), (c_far_nsa, _const_spec(c_far_nsa.shape)),
              (ovt, _const_spec(ovt.shape))]
    moba_in = [(qm, row_spec(MOBA_W)), (tiles(km), seq_spec(MOBA_W)), (tiles(vm), seq_spec(MOBA_W)),
               (qx, row_spec(MEM_W)),
               (mk.reshape(b, mem_len, MEM_W), per_batch(mem_len, MEM_W)),
               (mv.reshape(b, mem_len, MEM_W), per_batch(mem_len, MEM_W)),
               (t_moba, _const_spec(t_moba.shape)), (c_far_moba, _const_spec(c_far_moba.shape))]
    nsa_scratch =[pltpu.VMEM((nt, NSA_GROUPS, V_AUG, TILE), BF16),
                   pltpu.VMEM((nt, NSA_GROUPS, V_AUG, TILE), BF16),
                   pltpu.VMEM((NSA_GROUPS, 2 * HEAD_DIM, n_lanes), BF16),
                   pltpu.VMEM((NSA_GROUPS, HEAD_DIM, n_lanes), F32)]
    moba_scratch = [pltpu.VMEM((nt, MOBA_HEADS, V_AUG, TILE), BF16),
                    pltpu.VMEM((MEM_HEADS, V_AUG, mem_len), BF16),
                    pltpu.VMEM((BF16_ROWS, MOBA_W), F32),
                    pltpu.VMEM((MOBA_W, n_lanes), BF16),
                    pltpu.VMEM((BF16_ROWS, n_lanes), F32)]
    shared_scratch =[pltpu.VMEM((N_STREAMS, TILE, n_lanes), F32),
                      pltpu.VMEM((N_STREAMS, 1, n_lanes), F32),
                      pltpu.VMEM((N_STREAMS, V_AUG, n_lanes), F32)]
    inputs = nsa_in + moba_in
    return pl.pallas_call(
        functools.partial(_attention_kernel, n_nsa_in=len(nsa_in), n_moba_in=len(moba_in),
                          n_nsa_scratch=len(nsa_scratch), n_moba_scratch=len(moba_scratch)),
        grid=(b, steps),
        in_specs=[spec for _, spec in inputs],
        out_specs=[row_spec(NSA_Q_W), row_spec(MOBA_W), row_spec(MEM_W)],
        out_shape=[jax.ShapeDtypeStruct((w, b * s), BF16) for w in (NSA_Q_W, MOBA_W, MEM_W)],
        scratch_shapes=nsa_scratch + moba_scratch + shared_scratch,
        compiler_params=_params(("arbitrary", "arbitrary")),
        name="attention",
    )(*[a for a, _ in inputs])


MIX_CHUNK = MXU_COLS


def _mix_kernel(x_ref, on_ref, om_ref, ox_ref, g_pre_ref, g_post_ref, wg_ref, wn_ref, wm_ref, wx_ref,
                wo_ref, o_ref, merged_sc):
    tm = merged_sc.shape[1]
    for t in range(merged_sc.shape[0]):
        tile = slice(t * tm, (t + 1) * tm)
        x = x_ref[tile, :]
        h = _rms(x, g_pre_ref[...]).astype(BF16)
        rows = lambda o_ref: o_ref[:, tile].astype(F32).T.astype(BF16)
        branches = ((rows(on_ref), wn_ref), (rows(om_ref), wm_ref), (rows(ox_ref), wx_ref))

        def chunk_job(c0):
            cols = slice(c0, c0 + MIX_CHUNK)

            def produce():
                gates = [_dot(h, wg_ref[:, i * D_MODEL + c0:i * D_MODEL + c0 + MIX_CHUNK])
                         for i in range(len(branches))]
                return list(zip(gates, [_dot(o, w_ref[:, cols]) for o, w_ref in branches]))

            def consume(pairs):
                merged_sc[t, :, cols] = sum(jax.nn.sigmoid(g) * y for g, y in pairs).astype(BF16)

            return produce, consume

        _run_ahead([chunk_job(c0) for c0 in range(0, D_MODEL, MIX_CHUNK)], depth=1)
        y = _dot(merged_sc[t], wo_ref[...])
        o_ref[tile, :] = x + _rms(y, g_post_ref[...])


def _mix(x2, o_nsa, o_moba, o_mem, g_pre, g_post, w_gates, w_nsa_o, w_moba_o, w_mem_o, w_mix_out, tm=512,
         tiles_per_step=2):
    m = x2.shape[0]
    rows_per_step = tm * tiles_per_step
    row = lambda w: pl.BlockSpec((rows_per_step, w), lambda i: (i, 0))
    col = lambda w: pl.BlockSpec((w, rows_per_step), lambda i: (0, i))
    return pl.pallas_call(
        _mix_kernel,
        grid=(m // rows_per_step,),
        in_specs=[row(D_MODEL), col(NSA_Q_W), col(MOBA_W), col(MEM_W),
                  _const_spec((1, D_MODEL)), _const_spec((1, D_MODEL)),
                  _const_spec(w_gates.shape), _const_spec(w_nsa_o.shape), _const_spec(w_moba_o.shape),
                  _const_spec(w_mem_o.shape), _const_spec(w_mix_out.shape)],
        out_specs=row(D_MODEL),
        out_shape=jax.ShapeDtypeStruct((m, D_MODEL), F32),
        scratch_shapes=[pltpu.VMEM((tiles_per_step, tm, D_MODEL), BF16)],
        compiler_params=_params(("parallel",)),
        name="mix",
    )(x2, o_nsa, o_moba, o_mem, g_pre, g_post, w_gates, w_nsa_o, w_moba_o, w_mem_o, w_mix_out)


FFN_CHUNK = 256


def _ffn_kernel(x_ref, g_pre_ref, g_post_ref, wg_ref, wu_ref, wd_ref, o_ref, a_sc):
    tm = a_sc.shape[1]
    d_ff = wg_ref.shape[1]
    for t in range(a_sc.shape[0]):
        rows = slice(t * tm, (t + 1) * tm)
        x = x_ref[rows, :]
        h = _rms(x, g_pre_ref[...]).astype(BF16)
        for j in range(d_ff // FFN_CHUNK):
            sl = slice(j * FFN_CHUNK, (j + 1) * FFN_CHUNK)
            a_sc[t, :, sl] = (jax.nn.silu(_dot(h, wg_ref[:, sl])) * _dot(h, wu_ref[:, sl])).astype(BF16)
        f = _dot(a_sc[t], wd_ref[...])
        o_ref[rows, :] = x + _rms(f, g_post_ref[...])


def _ffn(x2, g_pre, g_post, wg, wu, wd, tm=512, tiles_per_step=2):
    m = x2.shape[0]
    d_ff = wg.shape[1]
    rows = tm * tiles_per_step
    return pl.pallas_call(
        _ffn_kernel,
        grid=(m // rows,),
        in_specs=[pl.BlockSpec((rows, D_MODEL), lambda i: (i, 0)),
                  _const_spec((1, D_MODEL)), _const_spec((1, D_MODEL)),
                  _const_spec(wg.shape), _const_spec(wu.shape), _const_spec(wd.shape)],
        out_specs=pl.BlockSpec((rows, D_MODEL), lambda i: (i, 0)),
        out_shape=jax.ShapeDtypeStruct((m, D_MODEL), F32),
        scratch_shapes=[pltpu.VMEM((tiles_per_step, tm, d_ff), BF16)],
        compiler_params=_params(("parallel",)),
        name="ffn",
    )(x2, g_pre, g_post, wg, wu, wd)


def kernel(x, mem, rel_bias, pre_mix_g, mem_norm_g, post_mix_g, w_in, cmp_pos_k, cmp_w1_k, cmp_w2_k, cmp_pos_v, cmp_w1_v, cmp_w2_v, w_mem_kv, w_nsa_o, w_moba_o, w_mem_o, w_mix_out, pre_ffn_g, post_ffn_g, w_ffn_gate, w_ffn_up, w_ffn_down):
    b, s, d_model = x.shape
    depth = w_in.shape[0]
    assert d_model == D_MODEL and s % TILE == 0 and TILE == MOBA_BLOCK == WINDOW
    assert (s - CMP_LEN) // CMP_STRIDE + 1 < N_CMP_PAD and (s // SEL_BLOCK) % SUBLANES == 0 and s // SEL_BLOCK <= HEAD_DIM
    assert w_in.shape[2] == ATT_W + 3 * D_MODEL and rel_bias.shape == (REL_BUCKETS, N_BIAS_HEADS)

    tile_idx, win_idx, cmp_idx = _bucket_tables(s)
    rel_bias = rel_bias.astype(F32)
    t_nsa = _expand(tile_idx, rel_bias, 0, NSA_HEADS, NSA_HPG).reshape(NSA_GROUPS, 2, TILE, NSA_HPG * TILE)
    t_moba = _expand(tile_idx, rel_bias, NSA_HEADS, MOBA_HEADS, MOBA_HEADS).reshape(2, TILE, MOBA_HEADS * TILE)
    t_win = _expand(win_idx, rel_bias, 0, NSA_HEADS, NSA_HPG)
    b_cmp = _expand(cmp_idx, rel_bias, 0, NSA_HEADS)
    c_far = jnp.repeat(rel_bias[REL_BUCKETS - 1] * LOG2E, TILE)
    c_far_nsa = c_far[:NSA_HEADS * TILE].reshape(NSA_GROUPS, 1, NSA_HPG * TILE)
    c_far_moba = c_far[NSA_HEADS * TILE:].reshape(1, MOBA_HEADS * TILE)
    ovt = _overlap_table(s)
    sel_cols = np.zeros((s, HEAD_DIM), np.float32)
    sel_cols[np.arange(s), np.arange(s) // SEL_BLOCK] = 1.0
    sel_cols = jnp.asarray(sel_cols, BF16)
    gate_lo = NSA_Q_W + 6 * NSA_KV_W
    rows_per_chunk = CMP_STRIDE * NSA_KV_W

    x2 = x.reshape(b * s, D_MODEL)
    mem2 = mem.reshape(-1, D_MODEL)
    for l in range(depth):
        w_att = jnp.concatenate(
            [w_in[l, :, :gate_lo + NSA_GATE_W],
             jnp.zeros((D_MODEL, GATE_PAD - NSA_GATE_W), w_in.dtype),
             w_in[l, :, gate_lo + NSA_GATE_W:ATT_W]], axis=1).astype(BF16)
        w_gates = w_in[l, :, ATT_W:].astype(BF16)
        row = lambda v: v[l].reshape(1, D_MODEL)

        qn, kc_raw, vc_raw, ks, vs, kw, vw, gn, qm, km, vm, qx = _inproj(x2, row(pre_mix_g), w_att, sel_cols)

        pk, w1k = _compress_weights(cmp_pos_k[l], cmp_w1_k[l])
        pv, w1v = _compress_weights(cmp_pos_v[l], cmp_w1_v[l])
        kc, vct = _compress(kc_raw.reshape(b, s // CMP_STRIDE, rows_per_chunk),
                            vc_raw.reshape(b, s // CMP_STRIDE, rows_per_chunk),
                            pk, pv, w1k, w1v, cmp_w2_k[l].astype(BF16), cmp_w2_v[l].astype(BF16))

        mk, mv = _memkv(mem2, row(mem_norm_g), w_mem_kv[l].astype(BF16))

        o_nsa, o_moba, o_mem = _attention(b, s, qn, gn, kc, vct, ks, vs, kw, vw, b_cmp, t_nsa, t_win, c_far_nsa, ovt,
                                          qm, km, vm, qx, mk, mv, t_moba, c_far_moba)

        x2 = _mix(x2, o_nsa, o_moba, o_mem, row(pre_mix_g), row(post_mix_g), w_gates,
                  w_nsa_o[l].astype(BF16), w_moba_o[l].astype(BF16), w_mem_o[l].astype(BF16),
                  w_mix_out[l].astype(BF16))
        x2 = _ffn(x2, row(pre_ffn_g), row(post_ffn_g), w_ffn_gate[l].astype(BF16),
                  w_ffn_up[l].astype(BF16), w_ffn_down[l].astype(BF16))
    return x2.reshape(b, s, D_MODEL)
```

```python
import functools
import math

import numpy as np
import jax
import jax.numpy as jnp
from jax import lax
from jax.experimental import pallas as pl
from jax.experimental.pallas import tpu as pltpu

F32 = jnp.float32
BF16 = jnp.bfloat16

D_MODEL = 1024
HEAD_DIM = 64
SCALE = HEAD_DIM ** -0.5
LOG2E = math.log2(math.e)
Q_SCALE = SCALE * LOG2E
NSA_HEADS = 8
NSA_GROUPS = 2
NSA_HPG = NSA_HEADS // NSA_GROUPS
CMP_LEN = 32
CMP_STRIDE = 16
CMP_HIDDEN = 128
SEL_BLOCK = 64
SEL_TOPN = 8
WINDOW = 256
MOBA_HEADS = 4
MOBA_BLOCK = 256
MOBA_TOPK = 3
MEM_HEADS = 4
REL_BUCKETS = 32
REL_MAX_DIST = 128
N_BIAS_HEADS = NSA_HEADS + MOBA_HEADS
RMS_EPS = 1e-6
NEG_INF = -1e30
FORCE_SCORE = 1e4

NSA_Q_W = NSA_HEADS * HEAD_DIM
NSA_KV_W = NSA_GROUPS * HEAD_DIM
NSA_GATE_W = NSA_HEADS * 3
MOBA_W = MOBA_HEADS * HEAD_DIM
MEM_W = MEM_HEADS * HEAD_DIM
ATT_W = NSA_Q_W + 6 * NSA_KV_W + NSA_GATE_W + 3 * MOBA_W + MEM_W
LANES = 128
SUBLANES = 8
BF16_ROWS = 16
MXU_COLS = 256
GATE_PAD = LANES
TILE = 256
HALF = TILE // 2
RUN_AHEAD = 2
N_CMP_PAD = 128
V_AUG = HEAD_DIM + BF16_ROWS
MASKED_BUCKET = REL_BUCKETS
VMEM_LIMIT = 56 * 1024 * 1024


def _dot(a, b):
    return jnp.dot(a, b, preferred_element_type=F32)


def _split_bf16(x):
    hi = x.astype(BF16)
    lo = (x - hi.astype(F32)).astype(BF16)
    return hi, lo


def _rms(x, g):
    return x * lax.rsqrt(jnp.mean(x * x, axis=-1, keepdims=True) + RMS_EPS) * g


def _params(sem):
    return pltpu.CompilerParams(dimension_semantics=sem, vmem_limit_bytes=VMEM_LIMIT)


def _const_spec(shape):
    nd = len(shape)
    return pl.BlockSpec(shape, lambda *_: (0,) * nd, pipeline_mode=pl.Buffered(1))


_INPROJ_OUTS = (
    ("qn", NSA_Q_W, BF16, True),
    ("kc", NSA_KV_W, F32, False), ("vc", NSA_KV_W, F32, False),
    ("ks", NSA_KV_W, BF16, False), ("vs", NSA_KV_W, BF16, False),
    ("kw", NSA_KV_W, BF16, False), ("vw", NSA_KV_W, BF16, False),
    ("gn", GATE_PAD, F32, False),
    ("qm", MOBA_W, BF16, True), ("km", MOBA_W, BF16, False), ("vm", MOBA_W, BF16, False),
    ("qx", MEM_W, BF16, True),
)
_INPROJ_W = sum(o[1] for o in _INPROJ_OUTS)
_INPROJ_CHUNKED = ("kc", "vc")
_INPROJ_TRANSPOSED = ("qn", "gn", "qm", "qx")
KS_AUG_W = NSA_GROUPS * 2 * HEAD_DIM


def _inproj_out_width(name, width):
    return KS_AUG_W if name == "ks" else width


def _inproj_kernel(x_ref, g_ref, w_ref, e_ref, *refs):
    out_refs, rows_sc = refs[:-1], refs[-1]
    runs, lo = [], 0
    for out in zip(_INPROJ_OUTS, out_refs):
        if runs and runs[-1][1] < MXU_COLS:
            runs[-1][0].append(out)
            runs[-1][1] += out[0][1]
        else:
            runs.append([[out], out[0][1], lo])
        lo += out[0][1]
    tm = rows_sc.shape[1]
    for t in range(rows_sc.shape[0]):
        tile = slice(t * tm, (t + 1) * tm)
        h = _rms(x_ref[tile, :], g_ref[...]).astype(BF16)
        for outs, run_width, run_lo in runs:
            y_run = _dot(h, w_ref[:, run_lo:run_lo + run_width])
            lo = 0
            for (name, width, dtype, scaled), o_ref in outs:
                y = y_run[:, lo:lo + width]
                if scaled:
                    y = y * Q_SCALE
                if name == "gn":
                    y = jax.nn.sigmoid(y)
                if name in _INPROJ_TRANSPOSED:
                    o_ref[:, tile] = y.T.astype(dtype)
                elif name in _INPROJ_CHUNKED:
                    rows_sc[t] = y
                    n_rows = tm // CMP_STRIDE
                    for j in range(CMP_STRIDE):
                        o_ref[t * n_rows:(t + 1) * n_rows, j * width:(j + 1) * width] = (
                            rows_sc[t, pl.ds(j, n_rows, stride=CMP_STRIDE), :])
                else:
                    y = y.astype(dtype)
                    if name == "ks":
                        e = e_ref[tile, :]
                        y = _lane_cat([y[:, :HEAD_DIM], e, y[:, HEAD_DIM:], e])
                    o_ref[tile, :] = y
                lo += width


def _inproj(x2, g, w, e_cols, tm=512, tiles_per_step=2):
    m = x2.shape[0]
    step = tm * tiles_per_step
    assert m % step == 0 and e_cols.shape[0] % step == 0
    steps_per_seq = e_cols.shape[0] // step
    out_specs, out_shape = [], []
    for name, width, dtype, _ in _INPROJ_OUTS:
        if name in _INPROJ_TRANSPOSED:
            out_specs.append(pl.BlockSpec((width, step), lambda i: (0, i)))
            out_shape.append(jax.ShapeDtypeStruct((width, m), dtype))
            continue
        rows, width = (CMP_STRIDE, CMP_STRIDE * width) if name in _INPROJ_CHUNKED else (1, _inproj_out_width(name, width))
        out_specs.append(pl.BlockSpec((step // rows, width), lambda i: (i, 0)))
        out_shape.append(jax.ShapeDtypeStruct((m // rows, width), dtype))
    return pl.pallas_call(
        _inproj_kernel,
        grid=(m // step,),
        in_specs=[pl.BlockSpec((step, D_MODEL), lambda i: (i, 0)),
                  _const_spec((1, D_MODEL)),
                  _const_spec((D_MODEL, _INPROJ_W)),
                  pl.BlockSpec((step, HEAD_DIM), lambda i: (i % steps_per_seq, 0))],
        out_specs=out_specs,
        out_shape=out_shape,
        scratch_shapes=[pltpu.VMEM((tiles_per_step, tm, NSA_KV_W), F32)],
        compiler_params=_params(("parallel",)),
        name="inproj",
    )(x2, g, w, e_cols)


def _compress_kernel(rk_ref, rv_ref, pk_ref, pv_ref, w1k_ref, w1v_ref, w2k_ref, w2v_ref, kc_ref, vc_ref):
    nb = rk_ref.shape[0]

    def one(r_ref, p_ref, w1_ref, w2_ref):
        r = r_ref[...].reshape(nb * N_CMP_PAD, r_ref.shape[2])
        top = _dot((r + p_ref[0:1, :]).astype(BF16), w1_ref[0])
        bot = _dot((r + p_ref[1:2, :]).astype(BF16), w1_ref[1])
        hid = top + pltpu.roll(bot, nb * N_CMP_PAD - 1, 0)
        act = jax.nn.gelu(hid).astype(BF16)
        return jnp.concatenate(
            [_dot(act[:, g * CMP_HIDDEN:(g + 1) * CMP_HIDDEN], w2_ref[...]) for g in range(NSA_GROUPS)], axis=1)

    k_out = one(rk_ref, pk_ref, w1k_ref, w2k_ref)
    v_out = one(rv_ref, pv_ref, w1v_ref, w2v_ref)
    for n in range(nb):
        rows = slice(n * N_CMP_PAD, (n + 1) * N_CMP_PAD)
        kc_ref[n] = k_out[rows].astype(BF16)
        vc_ref[n] = v_out[rows].T.astype(BF16)


def _compress(rk, rv, pk, pv, w1k, w1v, w2k, w2v):
    b = rk.shape[0]
    rw = rk.shape[2]
    nb = 2 if b % 2 == 0 else 1
    r_spec = pl.BlockSpec((nb, N_CMP_PAD, rw), lambda i: (i, 0, 0))
    o_spec = pl.BlockSpec((nb, N_CMP_PAD, NSA_KV_W), lambda i: (i, 0, 0))
    return pl.pallas_call(
        _compress_kernel,
        grid=(b // nb,),
        in_specs=[r_spec, r_spec, _const_spec(pk.shape), _const_spec(pv.shape),
                  _const_spec(w1k.shape), _const_spec(w1v.shape),
                  _const_spec(w2k.shape), _const_spec(w2v.shape)],
        out_specs=[o_spec, o_spec],
        out_shape=[jax.ShapeDtypeStruct((b, N_CMP_PAD, NSA_KV_W), BF16)] * 2,
        compiler_params=_params(("parallel",)),
        name="compress",
    )(rk, rv, pk, pv, w1k, w1v, w2k, w2v)


def _compress_weights(pos, w1):
    half = CMP_LEN // 2
    p = pos.reshape(2, half, 1, HEAD_DIM)
    p = jnp.broadcast_to(p, (2, half, NSA_GROUPS, HEAD_DIM)).reshape(2, half * NSA_KV_W)
    w = w1.reshape(2, half, HEAD_DIM, CMP_HIDDEN)
    eye = jnp.eye(NSA_GROUPS, dtype=w1.dtype)
    wbd = jnp.einsum("ajdm,gk->ajgdkm", w, eye).reshape(2, half * NSA_KV_W, NSA_GROUPS * CMP_HIDDEN)
    return p.astype(F32), wbd.astype(BF16)


def _memkv_kernel(m_ref, g_ref, w_ref, k_ref, v_ref):
    h = _rms(m_ref[...], g_ref[...]).astype(BF16)
    k_ref[...] = _dot(h, w_ref[:, :MEM_W]).astype(BF16)
    v_ref[...] = _dot(h, w_ref[:, MEM_W:]).astype(BF16)


def _memkv(mem2, g, w, tm=512):
    m = mem2.shape[0]
    tm = min(tm, m)
    o_spec = pl.BlockSpec((tm, MEM_W), lambda i: (i, 0))
    return pl.pallas_call(
        _memkv_kernel,
        grid=(m // tm,),
        in_specs=[pl.BlockSpec((tm, D_MODEL), lambda i: (i, 0)), _const_spec((1, D_MODEL)),
                  _const_spec((D_MODEL, 2 * MEM_W))],
        out_specs=[o_spec, o_spec],
        out_shape=[jax.ShapeDtypeStruct((m, MEM_W), BF16)] * 2,
        compiler_params=_params(("parallel",)),
        name="memkv",
    )(mem2, g, w)


def _expand_kernel(idx_ref, bias_ref, o_ref, *, head0, n_heads, heads_per_group):
    rows, cols = idx_ref.shape

    def body(i, carry):
        r = pl.multiple_of(i * SUBLANES, SUBLANES)
        for c0 in range(0, cols, TILE):
            idx = idx_ref[pl.ds(r, SUBLANES), c0:c0 + TILE]
            out = [jnp.full(idx.shape, NEG_INF, F32)] * n_heads
            for bkt in range(REL_BUCKETS):
                hit = idx == bkt
                out = [jnp.where(hit, bias_ref[bkt, head0 + h], out[h]) for h in range(n_heads)]
            for h in range(n_heads):
                col = (h % heads_per_group) * cols + c0
                o_ref[h // heads_per_group, pl.ds(r, SUBLANES), col:col + TILE] = out[h] * LOG2E
        return carry

    lax.fori_loop(0, rows // SUBLANES, body, 0)


def _expand(idx, rel_bias, head0, n_heads, heads_per_group=1):
    rows, cols = idx.shape
    return pl.pallas_call(
        functools.partial(_expand_kernel, head0=head0, n_heads=n_heads, heads_per_group=heads_per_group),
        in_specs=[pl.BlockSpec(memory_space=pltpu.VMEM), pl.BlockSpec(memory_space=pltpu.SMEM)],
        out_specs=pl.BlockSpec(memory_space=pltpu.VMEM),
        out_shape=jax.ShapeDtypeStruct((n_heads // heads_per_group, rows, heads_per_group * cols), F32),
        compiler_params=pltpu.CompilerParams(vmem_limit_bytes=VMEM_LIMIT),
        name="bias_expand",
    )(idx, rel_bias)


def _t5_bucket_np(dist):
    dist = np.maximum(dist, 0)
    max_exact = REL_BUCKETS // 2
    logd = np.log(np.maximum(dist, 1).astype(np.float32) / max_exact) / math.log(REL_MAX_DIST / max_exact)
    large = np.minimum(max_exact + (logd * (REL_BUCKETS - max_exact)).astype(np.int32), REL_BUCKETS - 1)
    return np.where(dist < max_exact, dist, large).astype(np.int32)


def _bucket_tables(s):
    j = np.arange(TILE)[:, None]
    i = np.arange(TILE)[None, :]
    assert TILE + 1 >= REL_MAX_DIST
    tiles = []
    for d in range(2):
        dist = d * TILE + i - j
        tiles.append(np.where(dist >= 0, _t5_bucket_np(dist), MASKED_BUCKET))
    dist1 = TILE + i - j
    win = np.where(dist1 < WINDOW, _t5_bucket_np(dist1), MASKED_BUCKET)
    n_cmp = (s - CMP_LEN) // CMP_STRIDE + 1
    assert n_cmp * CMP_STRIDE + CMP_LEN - 1 >= s or n_cmp == N_CMP_PAD
    rel = np.arange(N_CMP_PAD + (s - TILE) // CMP_STRIDE)[:, None] - (s - TILE) // CMP_STRIDE
    dist_c = i - (rel * CMP_STRIDE + CMP_LEN - 1)
    cmp_idx = np.where(dist_c >= 0, _t5_bucket_np(dist_c), MASKED_BUCKET)
    as_i32 = lambda a: jnp.asarray(a.astype(np.int32))
    return as_i32(np.concatenate(tiles, axis=0)), as_i32(win), as_i32(cmp_idx)


def _overlap_table(s):
    n_cmp = (s - CMP_LEN) // CMP_STRIDE + 1
    n_sel = s // SEL_BLOCK
    cs = np.arange(n_cmp) * CMP_STRIDE
    ss = np.arange(n_sel) * SEL_BLOCK
    ov = np.clip(np.minimum(cs[:, None] + CMP_LEN, ss[None, :] + SEL_BLOCK)
                 - np.maximum(cs[:, None], ss[None, :]), 0, None).astype(np.float32) / CMP_LEN
    ovt = np.zeros((n_sel, N_CMP_PAD), np.float32)
    ovt[:, :n_cmp] = ov.T
    return jnp.asarray(ovt, BF16)


def _store_v_aug(vt_sc, idx, vt):
    ones = jnp.ones((BF16_ROWS, vt.shape[1]), BF16)
    vt_sc[idx] = jnp.concatenate([vt.astype(BF16), ones], axis=0)


def _lane_cat(xs):
    return jnp.concatenate(xs, axis=1)


def _query_halves(x):
    n = x.shape[-1] // TILE
    first = _lane_cat([x[:, k * TILE:k * TILE + HALF] for k in range(n)])
    second = _lane_cat([x[:, k * TILE + HALF:(k + 1) * TILE] for k in range(n)])
    return first, second


def _join_query_halves(first, second):
    n = first.shape[-1] // HALF
    return _lane_cat([part for k in range(n)
                      for part in (first[:, k * HALF:(k + 1) * HALF], second[:, k * HALF:(k + 1) * HALF])])


def _triangle_scores(k, q, table, causal):
    q_first, q_second = _query_halves(q)
    lo, hi = (0, HALF), (HALF, TILE)
    if causal:
        return _dot(k(*lo), q) + table(*lo), _dot(k(*hi), q_second) + _query_halves(table(*hi))[1]
    return _dot(k(*hi), q) + table(*hi), _dot(k(*lo), q_first) + _query_halves(table(*lo))[0]


def _triangle_softmax(scores, pv_lo, pv_hi, causal):
    s_wide, s_narrow = scores
    pv_wide, pv_narrow = (pv_lo, pv_hi) if causal else (pv_hi, pv_lo)
    mw_first, mw_second = _query_halves(jnp.max(s_wide, axis=0, keepdims=True))
    m_narrow = jnp.max(s_narrow, axis=0, keepdims=True)
    if causal:
        m_narrow = jnp.maximum(m_narrow, mw_second)
        m = _join_query_halves(mw_first, m_narrow)
    else:
        m_narrow = jnp.maximum(m_narrow, mw_first)
        m = _join_query_halves(m_narrow, mw_second)
    aw_first, aw_second = _query_halves(pv_wide(jnp.exp2(s_wide - m).astype(BF16)))
    a_narrow = pv_narrow(jnp.exp2(s_narrow - m_narrow).astype(BF16))
    if causal:
        return m, _join_query_halves(aw_first, aw_second + a_narrow)
    return m, _join_query_halves(aw_first + a_narrow, aw_second)


def _flash_pipelined(own, streams, s_sc, m_ref, acc_ref):
    has_prev = jnp.where(own > 0, 1.0, 0.0).astype(F32)
    prev = jnp.maximum(own - 1, 0)
    n_far = jnp.maximum(own - 1, 0)

    def absorb(g, s, kt, c_row, w_row):
        u = jnp.max(s, axis=0, keepdims=True) + c_row
        m_old = m_ref[g]
        m_new = jnp.maximum(m_old, jnp.where(w_row > 0.0, u, NEG_INF))
        alpha = jnp.exp2(m_old - m_new)
        shift = jnp.maximum(m_new, u) - c_row
        p = jnp.exp2(s - shift).astype(BF16)
        acc_ref[g] = alpha * acc_ref[g] + w_row * streams[g]["pv"](kt)(p)
        m_ref[g] = m_new

    def absorb_slot(g, i):
        is_prev = i == 0
        kt = jnp.where(is_prev, prev, i - 1)
        c_row = jnp.where(is_prev, 0.0, streams[g]["c_far"])
        w_row = streams[g]["w"](kt) * jnp.where(is_prev, has_prev, 1.0)
        absorb(g, s_sc[g], kt, c_row, w_row)

    def body(i, carry):
        for g in reversed(range(len(streams))):
            nxt = streams[g]["far"](i)
            absorb_slot(g, i)
            s_sc[g] = nxt
        return carry

    lax.fori_loop(0, n_far, body, 0)
    for g in range(len(streams)):
        absorb_slot(g, n_far)


def _run_ahead(jobs, depth=RUN_AHEAD):
    pending = [job[0]() for job in jobs[:depth]]
    for k, (_, consume) in enumerate(jobs):
        if k + depth < len(jobs):
            pending.append(jobs[k + depth][0]())
        consume(pending.pop(0))


def _flash_start_jobs(g, stream, s_sc, m_ref, acc_ref):
    def init(scores):
        m_ref[g], acc_ref[g] = stream["own_softmax"](scores)

    def park(scores):
        s_sc[g] = scores

    return [(stream["own_scores"], init), (stream["prev"], park)]


def _softmax_av(s_list, pv_list):
    m = s_list[0].max(axis=0, keepdims=True)
    for s in s_list[1:]:
        m = jnp.maximum(m, s.max(axis=0, keepdims=True))
    acc = None
    for s, pv in zip(s_list, pv_list):
        part = pv(jnp.exp2(s - m).astype(BF16))
        acc = part if acc is None else acc + part
    return acc


def _normalize(acc):
    return acc[:HEAD_DIM] * (1.0 / acc[HEAD_DIM:HEAD_DIM + 1])


def _rank_before(score, n_cand):
    ranks = []
    for r0 in range(0, score.shape[0], SUBLANES):
        tile = score[r0:r0 + SUBLANES]
        blk = lax.broadcasted_iota(jnp.int32, tile.shape, 0) + r0
        rank = jnp.zeros(tile.shape, F32)
        for m in range(n_cand):
            row = score[m:m + 1, :]
            if m < r0:
                before = jnp.where(row >= tile, 1.0, 0.0)
            elif m >= r0 + SUBLANES:
                before = jnp.where(row > tile, 1.0, 0.0)
            else:
                before = jnp.where(blk > m, jnp.where(row >= tile, 1.0, 0.0), jnp.where(row > tile, 1.0, 0.0))
            rank = rank + before
        ranks.append(rank)
    return jnp.concatenate(ranks, axis=0)


def _nsa_steps(qi, maybe_first, q_ref, gn_ref, kc_ref, vct_ref, ks_ref, vs_ref, kw_ref, vw_ref,
               bct_ref, tt_ref, twt_ref, cfar_ref, ovt_ref, o_ref,
               vst_sc, vwt_sc, qa_sc, og_sc, acc_sc):
    nt = ks_ref.shape[1]
    n_sel = ovt_ref.shape[0]

    def transpose_v():
        for kt in range(nt):
            vs_t = vs_ref[0, kt].astype(F32).T
            vw_t = vw_ref[0, kt].astype(F32).T
            for g in range(NSA_GROUPS):
                _store_v_aug(vst_sc, (kt, g), vs_t[g * HEAD_DIM:(g + 1) * HEAD_DIM])
                _store_v_aug(vwt_sc, (kt, g), vw_t[g * HEAD_DIM:(g + 1) * HEAD_DIM])

    if maybe_first:
        pl.when(qi == 0)(transpose_v)

    pos = lax.broadcasted_iota(jnp.int32, (1, TILE), 1) + qi * TILE
    cur = pos // SEL_BLOCK
    has_cmp = pos >= CMP_LEN - 1
    blk = lax.broadcasted_iota(jnp.int32, (n_sel, TILE), 0)
    prev = jnp.maximum(qi - 1, 0)
    gates = gn_ref[...]

    gsls = [slice(g * HEAD_DIM, (g + 1) * HEAD_DIM) for g in range(NSA_GROUPS)]
    group_heads = [[g * NSA_HPG + j for j in range(NSA_HPG)] for g in range(NSA_GROUPS)]

    def gate(g, branch):
        return _lane_cat([gates[3 * h + branch:3 * h + branch + 1, :] for h in group_heads[g]])

    ones_row = jnp.ones((1, NSA_HPG * TILE), F32)

    def sel_stream(g):
        def qk(kt):
            return _dot(ks_ref[0, kt, :, g * 2 * HEAD_DIM:(g + 1) * 2 * HEAD_DIM], qa_sc[g])

        def own_scores():
            return _triangle_scores(
                lambda lo, hi: ks_ref[0, qi, lo:hi, g * 2 * HEAD_DIM:(g + 1) * 2 * HEAD_DIM], qa_sc[g],
                lambda lo, hi: tt_ref[g, 0, lo:hi, :], causal=True)

        def own_softmax(scores):
            return _triangle_softmax(
                scores, lambda pr: _dot(vst_sc[qi, g, :, 0:HALF], pr),
                lambda pr: _dot(vst_sc[qi, g, :, HALF:TILE], pr), causal=True)

        return dict(own_scores=own_scores, own_softmax=own_softmax, prev=lambda: qk(prev) + tt_ref[g, 1], far=qk,
                    c_far=cfar_ref[g], w=lambda kt: ones_row,
                    pv=lambda kt: (lambda pr: _dot(vst_sc[kt, g], pr)))

    def group_jobs(g):
        heads = group_heads[g]
        win = {}

        def v_half(kt, lo, hi):
            return lambda pr: _dot(vwt_sc[kt, g, :, lo:hi], pr)

        def window_own_scores():
            q4 = _lane_cat([q_ref[h * HEAD_DIM:(h + 1) * HEAD_DIM, :] for h in heads])
            qa_sc[g, 0:HEAD_DIM, :] = q4
            qa_sc[g, HEAD_DIM + n_sel:, :] = jnp.zeros((HEAD_DIM - n_sel, NSA_HPG * TILE), BF16)
            return _triangle_scores(lambda lo, hi: kw_ref[0, qi, lo:hi, gsls[g]], q4,
                                    lambda lo, hi: tt_ref[g, 0, lo:hi, :], causal=True)

        def window_own(scores):
            win["own"] = _triangle_softmax(scores, v_half(qi, 0, HALF), v_half(qi, HALF, TILE), causal=True)

        def window_prev_scores():
            return _triangle_scores(lambda lo, hi: kw_ref[0, prev, lo:hi, gsls[g]], qa_sc[g, 0:HEAD_DIM, :],
                                    lambda lo, hi: twt_ref[g, lo:hi, :], causal=False)

        def window_prev(scores):
            m_prev, acc_prev = _triangle_softmax(scores, v_half(prev, 0, HALF), v_half(prev, HALF, TILE), causal=False)
            m_prev = jnp.where(qi == 0, NEG_INF, m_prev)
            m_own, acc_own = win["own"]
            m_win = jnp.maximum(m_own, m_prev)
            acc_w = acc_own * jnp.exp2(m_own - m_win) + acc_prev * jnp.exp2(m_prev - m_win)
            og_sc[g] = gate(g, 2) * _normalize(acc_w)

        def compressed_scores():
            rows_per_tile = TILE // CMP_STRIDE
            c_rows = pl.ds(pl.multiple_of((nt - 1 - qi) * rows_per_tile, rows_per_tile), N_CMP_PAD)
            return (_dot(kc_ref[0, :, gsls[g]], qa_sc[g, 0:HEAD_DIM, :])
                    + _lane_cat([bct_ref[h, c_rows, :] for h in heads]))

        def compressed(s):
            e = jnp.exp2(s - jnp.max(s, axis=0, keepdims=True))
            p = e * jnp.where(_lane_cat([has_cmp] * NSA_HPG), 1.0 / jnp.sum(e, axis=0, keepdims=True), 0.0)
            psum = p[:, :TILE]
            for j in range(1, NSA_HPG):
                psum = psum + p[:, j * TILE:(j + 1) * TILE]
            og_sc[g] = og_sc[g] + gate(g, 0) * _dot(vct_ref[0, gsls[g], :], p.astype(BF16))
            win["psum"] = psum

        def importance():
            p_hi, p_lo = _split_bf16(win["psum"])
            return _dot(ovt_ref[...], p_hi) + _dot(ovt_ref[...], p_lo)

        def select(imp):
            forced = (blk == 0) | (blk == cur) | (blk == cur - 1)
            score = jnp.where(forced, FORCE_SCORE, jnp.where(blk <= cur, imp, NEG_INF))
            rank = _rank_before(score, n_sel)
            sel = jnp.where(rank < SEL_TOPN, jnp.where(score > NEG_INF / 2, 0.0, NEG_INF), NEG_INF)
            qa_sc[g, HEAD_DIM:HEAD_DIM + n_sel, :] = _lane_cat([sel.astype(BF16)] * NSA_HPG)

        return [(window_own_scores, window_own), (window_prev_scores, window_prev), (compressed_scores, compressed),
                (importance, select)]

    per_group = [group_jobs(g) for g in range(NSA_GROUPS)]
    assert NSA_GROUPS == 2 and RUN_AHEAD <= 2
    (a0, b0, c0, i0), (a1, b1, c1, i1) = per_group
    yield [a0, b0, c0, a1, b1, i0, c1], i1
    yield [sel_stream(g) for g in range(NSA_GROUPS)]

    for g in range(NSA_GROUPS):
        o = og_sc[g] + gate(g, 1) * _normalize(acc_sc[g])
        for j, h in enumerate(group_heads[g]):
            o_ref[h * HEAD_DIM:(h + 1) * HEAD_DIM, :] = o[:, j * TILE:(j + 1) * TILE].astype(BF16)


def _moba_steps(c, maybe_first, qm_ref, km_ref, vm_ref, qx_ref, mk_ref, mv_ref, tt_ref, cfar_ref, om_ref, ox_ref,
                vmt_sc, mvt_sc, kmean_sc, qbd_sc, sel_sc, acc_sc, slot):
    nt = km_ref.shape[1]
    hsls = [slice(h * HEAD_DIM, (h + 1) * HEAD_DIM) for h in range(MOBA_HEADS)]

    def per_sequence():
        kmean_sc[...] = jnp.zeros(kmean_sc.shape, F32)
        for n in range(nt):
            kmean_sc[n:n + 1, :] = jnp.sum(km_ref[0, n].astype(F32), axis=0, keepdims=True) * (1.0 / MOBA_BLOCK)
            vt = vm_ref[0, n].astype(F32).T
            for h in range(MOBA_HEADS):
                _store_v_aug(vmt_sc, (n, h), vt[hsls[h]])
        mvt = mv_ref[0].astype(F32).T
        for h in range(MOBA_HEADS):
            _store_v_aug(mvt_sc, h, mvt[hsls[h]])

    if maybe_first:
        pl.when(c == 0)(per_sequence)

    row_head = lax.broadcasted_iota(jnp.int32, (MOBA_W, TILE), 0) // HEAD_DIM

    def block_diag(q_ref):
        q_t = q_ref[...].astype(F32)
        return _lane_cat([jnp.where(row_head == h, q_t, 0.0) for h in range(MOBA_HEADS)]).astype(BF16)

    def per_head_pv(vts):
        return lambda pr: _lane_cat([_dot(vts(h), pr[:, h * TILE:(h + 1) * TILE]) for h in range(MOBA_HEADS)])

    def store_heads(o_t, out_ref):
        for h in range(MOBA_HEADS):
            out_ref[hsls[h], :] = o_t[:, h * TILE:(h + 1) * TILE].astype(BF16)

    n_rows = -(-nt // SUBLANES) * SUBLANES

    def gate_scores():
        qbd = block_diag(qm_ref)
        qbd_sc[...] = qbd
        km_hi, km_lo = _split_bf16(kmean_sc[...])
        return (_dot(km_hi, qbd) + _dot(km_lo, qbd))[:n_rows]

    def select(gate):
        blk = lax.broadcasted_iota(jnp.int32, gate.shape, 0)
        score = jnp.where(blk < c, gate, NEG_INF * Q_SCALE)
        rank = _rank_before(score, nt)
        sel_sc[0:n_rows, :] = jnp.where(rank < MOBA_TOPK, jnp.where(score > NEG_INF * Q_SCALE / 2, 1.0, 0.0), 0.0)

    qk = lambda n: _dot(km_ref[0, n], qbd_sc[...])
    def own_pv(lo, hi):
        def pv(pr):
            width = pr.shape[1] // MOBA_HEADS
            return _lane_cat([_dot(vmt_sc[c, h, :, lo:hi], pr[:, h * width:(h + 1) * width])
                              for h in range(MOBA_HEADS)])
        return pv

    def own_scores():
        return _triangle_scores(lambda lo, hi: km_ref[0, c, lo:hi, :], qbd_sc[...],
                                lambda lo, hi: tt_ref[0, lo:hi, :], causal=True)

    def own_softmax(scores):
        return _triangle_softmax(scores, own_pv(0, HALF), own_pv(HALF, TILE), causal=True)

    stream = dict(own_scores=own_scores, own_softmax=own_softmax,
                  prev=lambda: qk(jnp.maximum(c - 1, 0)) + tt_ref[1], far=qk,
                  c_far=cfar_ref[...], w=lambda n: sel_sc[pl.ds(n, 1), :],
                  pv=lambda n: per_head_pv(lambda h: vmt_sc[n, h]))
    def memory(s):
        store_heads(_normalize(_softmax_av([s], [per_head_pv(lambda h: mvt_sc[h])])), ox_ref)

    yield [(gate_scores, select), (lambda: _dot(mk_ref[0], block_diag(qx_ref)), memory)]
    yield [stream]
    store_heads(_normalize(acc_sc[slot]), om_ref)


N_STREAMS = NSA_GROUPS + 1
TILES_PER_STEP = 2
NSA_ROW_INPUTS = (0, 1)
MOBA_ROW_INPUTS = (0, 3)


def _attention_kernel(*refs, n_nsa_in, n_moba_in, n_nsa_scratch, n_moba_scratch):
    nsa_in, refs = refs[:n_nsa_in], refs[n_nsa_in:]
    moba_in, refs = refs[:n_moba_in], refs[n_moba_in:]
    outs, refs = refs[:3], refs[3:]
    nsa_sc, refs = refs[:n_nsa_scratch], refs[n_nsa_scratch:]
    moba_sc, refs = refs[:n_moba_scratch], refs[n_moba_scratch:]
    s_sc, m_sc, acc_sc = refs
    for t in range(TILES_PER_STEP):
        qi = pl.program_id(1) * TILES_PER_STEP + t
        cols = lambda ref: ref.at[:, t * TILE:(t + 1) * TILE]
        o_nsa, o_moba, o_mem = [cols(o) for o in outs]
        nsa_refs = [cols(r) if k in NSA_ROW_INPUTS else r for k, r in enumerate(nsa_in)]
        moba_refs = [cols(r) if k in MOBA_ROW_INPUTS else r for k, r in enumerate(moba_in)]
        nsa = _nsa_steps(qi, t == 0, *nsa_refs, o_nsa, *nsa_sc, acc_sc)
        moba = _moba_steps(qi, t == 0, *moba_refs, o_moba, o_mem, *moba_sc, acc_sc, NSA_GROUPS)
        nsa_jobs, last_selection = next(nsa)
        jobs = nsa_jobs + next(moba) + [last_selection]
        streams = next(nsa) + next(moba)
        jobs += [job for g, stream in enumerate(streams) for job in _flash_start_jobs(g, stream, s_sc, m_sc, acc_sc)]
        _run_ahead(jobs)
        _flash_pipelined(qi, streams, s_sc, m_sc, acc_sc)
        for steps in (nsa, moba):
            for _ in steps:
                pass


def _attention(b, s, qn, gn, kc, vct, ks, vs, kw, vw, bias_cmp, t_nsa, t_win, c_far_nsa, ovt,
               qm, km, vm, qx, mk, mv, t_moba, c_far_moba):
    nt = s // TILE
    mem_len = mk.shape[0] // b
    assert MOBA_TOPK <= nt - 1 and nt <= BF16_ROWS
    n_lanes = NSA_HPG * TILE
    assert MOBA_HEADS * TILE == n_lanes
    assert nt % TILES_PER_STEP == 0
    steps = nt // TILES_PER_STEP
    row_spec = lambda w: pl.BlockSpec((w, TILES_PER_STEP * TILE), lambda i, j: (0, i * steps + j))
    seq_spec = lambda w: pl.BlockSpec((1, nt, TILE, w), lambda i, j: (i, 0, 0, 0))
    per_batch = lambda rows, w: pl.BlockSpec((1, rows, w), lambda i, j: (i, 0, 0))
    tiles = lambda a: a.reshape(b, nt, TILE, a.shape[-1])
    nsa_in = [(qn, row_spec(NSA_Q_W)), (gn, row_spec(GATE_PAD)),
              (kc, per_batch(N_CMP_PAD, NSA_KV_W)), (vct, per_batch(N_CMP_PAD, NSA_KV_W)),
              (tiles(ks), seq_spec(KS_AUG_W)), (tiles(vs), seq_spec(NSA_KV_W)),
              (tiles(kw), seq_spec(NSA_KV_W)), (tiles(vw), seq_spec(NSA_KV_W)),
              (bias_cmp, _const_spec(bias_cmp.shape)), (t_nsa, _const_spec(t_nsa.shape)),
              (t_win, _const_spec(t_win.shape)), (c_far_nsa, _const_spec(c_far_nsa.shape)),
              (ovt, _const_spec(ovt.shape))]
    moba_in = [(qm, row_spec(MOBA_W)), (tiles(km), seq_spec(MOBA_W)), (tiles(vm), seq_spec(MOBA_W)),
               (qx, row_spec(MEM_W)),
               (mk.reshape(b, mem_len, MEM_W), per_batch(mem_len, MEM_W)),
               (mv.reshape(b, mem_len, MEM_W), per_batch(mem_len, MEM_W)),
               (t_moba, _const_spec(t_moba.shape)), (c_far_moba, _const_spec(c_far_moba.shape))]
    nsa_scratch =[pltpu.VMEM((nt, NSA_GROUPS, V_AUG, TILE), BF16),
                   pltpu.VMEM((nt, NSA_GROUPS, V_AUG, TILE), BF16),
                   pltpu.VMEM((NSA_GROUPS, 2 * HEAD_DIM, n_lanes), BF16),
                   pltpu.VMEM((NSA_GROUPS, HEAD_DIM, n_lanes), F32)]
    moba_scratch = [pltpu.VMEM((nt, MOBA_HEADS, V_AUG, TILE), BF16),
                    pltpu.VMEM((MEM_HEADS, V_AUG, mem_len), BF16),
                    pltpu.VMEM((BF16_ROWS, MOBA_W), F32),
                    pltpu.VMEM((MOBA_W, n_lanes), BF16),
                    pltpu.VMEM((BF16_ROWS, n_lanes), F32)]
    shared_scratch =[pltpu.VMEM((N_STREAMS, TILE, n_lanes), F32),
                      pltpu.VMEM((N_STREAMS, 1, n_lanes), F32),
                      pltpu.VMEM((N_STREAMS, V_AUG, n_lanes), F32)]
    inputs = nsa_in + moba_in
    return pl.pallas_call(
        functools.partial(_attention_kernel, n_nsa_in=len(nsa_in), n_moba_in=len(moba_in),
                          n_nsa_scratch=len(nsa_scratch), n_moba_scratch=len(moba_scratch)),
        grid=(b, steps),
        in_specs=[spec for _, spec in inputs],
        out_specs=[row_spec(NSA_Q_W), row_spec(MOBA_W), row_spec(MEM_W)],
        out_shape=[jax.ShapeDtypeStruct((w, b * s), BF16) for w in (NSA_Q_W, MOBA_W, MEM_W)],
        scratch_shapes=nsa_scratch + moba_scratch + shared_scratch,
        compiler_params=_params(("arbitrary", "arbitrary")),
        name="attention",
    )(*[a for a, _ in inputs])


MIX_CHUNK = MXU_COLS


def _mix_kernel(x_ref, on_ref, om_ref, ox_ref, g_pre_ref, g_post_ref, wg_ref, wn_ref, wm_ref, wx_ref,
                wo_ref, o_ref, merged_sc):
    tm = merged_sc.shape[1]
    for t in range(merged_sc.shape[0]):
        tile = slice(t * tm, (t + 1) * tm)
        x = x_ref[tile, :]
        h = _rms(x, g_pre_ref[...]).astype(BF16)
        rows = lambda o_ref: o_ref[:, tile].astype(F32).T.astype(BF16)
        branches = ((rows(on_ref), wn_ref), (rows(om_ref), wm_ref), (rows(ox_ref), wx_ref))

        def chunk_job(c0):
            cols = slice(c0, c0 + MIX_CHUNK)

            def produce():
                gates = [_dot(h, wg_ref[:, i * D_MODEL + c0:i * D_MODEL + c0 + MIX_CHUNK])
                         for i in range(len(branches))]
                return list(zip(gates, [_dot(o, w_ref[:, cols]) for o, w_ref in branches]))

            def consume(pairs):
                merged_sc[t, :, cols] = sum(jax.nn.sigmoid(g) * y for g, y in pairs).astype(BF16)

            return produce, consume

        _run_ahead([chunk_job(c0) for c0 in range(0, D_MODEL, MIX_CHUNK)], depth=1)
        y = _dot(merged_sc[t], wo_ref[...])
        o_ref[tile, :] = x + _rms(y, g_post_ref[...])


def _mix(x2, o_nsa, o_moba, o_mem, g_pre, g_post, w_gates, w_nsa_o, w_moba_o, w_mem_o, w_mix_out, tm=512,
         tiles_per_step=2):
    m = x2.shape[0]
    rows_per_step = tm * tiles_per_step
    assert m % rows_per_step == 0
    row = lambda w: pl.BlockSpec((rows_per_step, w), lambda i: (i, 0))
    col = lambda w: pl.BlockSpec((w, rows_per_step), lambda i: (0, i))
    return pl.pallas_call(
        _mix_kernel,
        grid=(m // rows_per_step,),
        in_specs=[row(D_MODEL), col(NSA_Q_W), col(MOBA_W), col(MEM_W),
                  _const_spec((1, D_MODEL)), _const_spec((1, D_MODEL)),
                  _const_spec(w_gates.shape), _const_spec(w_nsa_o.shape), _const_spec(w_moba_o.shape),
                  _const_spec(w_mem_o.shape), _const_spec(w_mix_out.shape)],
        out_specs=row(D_MODEL),
        out_shape=jax.ShapeDtypeStruct((m, D_MODEL), F32),
        scratch_shapes=[pltpu.VMEM((tiles_per_step, tm, D_MODEL), BF16)],
        compiler_params=_params(("parallel",)),
        name="mix",
    )(x2, o_nsa, o_moba, o_mem, g_pre, g_post, w_gates, w_nsa_o, w_moba_o, w_mem_o, w_mix_out)


FFN_CHUNK = 256


def _ffn_kernel(x_ref, g_pre_ref, g_post_ref, wg_ref, wu_ref, wd_ref, o_ref, a_sc):
    tm = a_sc.shape[1]
    d_ff = wg_ref.shape[1]
    for t in range(a_sc.shape[0]):
        rows = slice(t * tm, (t + 1) * tm)
        x = x_ref[rows, :]
        h = _rms(x, g_pre_ref[...]).astype(BF16)
        for j in range(d_ff // FFN_CHUNK):
            sl = slice(j * FFN_CHUNK, (j + 1) * FFN_CHUNK)
            a_sc[t, :, sl] = (jax.nn.silu(_dot(h, wg_ref[:, sl])) * _dot(h, wu_ref[:, sl])).astype(BF16)
        f = _dot(a_sc[t], wd_ref[...])
        o_ref[rows, :] = x + _rms(f, g_post_ref[...])


def _ffn(x2, g_pre, g_post, wg, wu, wd, tm=512, tiles_per_step=2):
    m = x2.shape[0]
    d_ff = wg.shape[1]
    rows = tm * tiles_per_step
    assert m % rows == 0
    return pl.pallas_call(
        _ffn_kernel,
        grid=(m // rows,),
        in_specs=[pl.BlockSpec((rows, D_MODEL), lambda i: (i, 0)),
                  _const_spec((1, D_MODEL)), _const_spec((1, D_MODEL)),
                  _const_spec(wg.shape), _const_spec(wu.shape), _const_spec(wd.shape)],
        out_specs=pl.BlockSpec((rows, D_MODEL), lambda i: (i, 0)),
        out_shape=jax.ShapeDtypeStruct((m, D_MODEL), F32),
        scratch_shapes=[pltpu.VMEM((tiles_per_step, tm, d_ff), BF16)],
        compiler_params=_params(("parallel",)),
        name="ffn",
    )(x2, g_pre, g_post, wg, wu, wd)


def kernel(x, mem, rel_bias, pre_mix_g, mem_norm_g, post_mix_g, w_in, cmp_pos_k, cmp_w1_k, cmp_w2_k, cmp_pos_v, cmp_w1_v, cmp_w2_v, w_mem_kv, w_nsa_o, w_moba_o, w_mem_o, w_mix_out, pre_ffn_g, post_ffn_g, w_ffn_gate, w_ffn_up, w_ffn_down):
    b, s, d_model = x.shape
    depth = w_in.shape[0]
    assert d_model == D_MODEL and s % TILE == 0 and TILE == MOBA_BLOCK == WINDOW
    assert (s - CMP_LEN) // CMP_STRIDE + 1 < N_CMP_PAD and (s // SEL_BLOCK) % SUBLANES == 0 and s // SEL_BLOCK <= HEAD_DIM
    assert w_in.shape[2] == ATT_W + 3 * D_MODEL and rel_bias.shape == (REL_BUCKETS, N_BIAS_HEADS)

    tile_idx, win_idx, cmp_idx = _bucket_tables(s)
    rel_bias = rel_bias.astype(F32)
    t_nsa = _expand(tile_idx, rel_bias, 0, NSA_HEADS, NSA_HPG).reshape(NSA_GROUPS, 2, TILE, NSA_HPG * TILE)
    t_moba = _expand(tile_idx, rel_bias, NSA_HEADS, MOBA_HEADS, MOBA_HEADS).reshape(2, TILE, MOBA_HEADS * TILE)
    t_win = _expand(win_idx, rel_bias, 0, NSA_HEADS, NSA_HPG)
    b_cmp = _expand(cmp_idx, rel_bias, 0, NSA_HEADS)
    c_far = jnp.repeat(rel_bias[REL_BUCKETS - 1] * LOG2E, TILE)
    c_far_nsa = c_far[:NSA_HEADS * TILE].reshape(NSA_GROUPS, 1, NSA_HPG * TILE)
    c_far_moba = c_far[NSA_HEADS * TILE:].reshape(1, MOBA_HEADS * TILE)
    ovt = _overlap_table(s)
    sel_cols = np.zeros((s, HEAD_DIM), np.float32)
    sel_cols[np.arange(s), np.arange(s) // SEL_BLOCK] = 1.0
    sel_cols = jnp.asarray(sel_cols, BF16)
    gate_lo = NSA_Q_W + 6 * NSA_KV_W
    rows_per_chunk = CMP_STRIDE * NSA_KV_W

    x2 = x.reshape(b * s, D_MODEL)
    mem2 = mem.reshape(-1, D_MODEL)
    for l in range(depth):
        w_att = jnp.concatenate(
            [w_in[l, :, :gate_lo + NSA_GATE_W],
             jnp.zeros((D_MODEL, GATE_PAD - NSA_GATE_W), w_in.dtype),
             w_in[l, :, gate_lo + NSA_GATE_W:ATT_W]], axis=1).astype(BF16)
        w_gates = w_in[l, :, ATT_W:].astype(BF16)
        row = lambda v: v[l].reshape(1, D_MODEL)

        qn, kc_raw, vc_raw, ks, vs, kw, vw, gn, qm, km, vm, qx = _inproj(x2, row(pre_mix_g), w_att, sel_cols)

        pk, w1k = _compress_weights(cmp_pos_k[l], cmp_w1_k[l])
        pv, w1v = _compress_weights(cmp_pos_v[l], cmp_w1_v[l])
        kc, vct = _compress(kc_raw.reshape(b, s // CMP_STRIDE, rows_per_chunk),
                            vc_raw.reshape(b, s // CMP_STRIDE, rows_per_chunk),
                            pk, pv, w1k, w1v, cmp_w2_k[l].astype(BF16), cmp_w2_v[l].astype(BF16))

        mk, mv = _memkv(mem2, row(mem_norm_g), w_mem_kv[l].astype(BF16))

        o_nsa, o_moba, o_mem = _attention(b, s, qn, gn, kc, vct, ks, vs, kw, vw, b_cmp, t_nsa, t_win, c_far_nsa, ovt,
                                          qm, km, vm, qx, mk, mv, t_moba, c_far_moba)

        x2 = _mix(x2, o_nsa, o_moba, o_mem, row(pre_mix_g), row(post_mix_g), w_gates,
                  w_nsa_o[l].astype(BF16), w_moba_o[l].astype(BF16), w_mem_o[l].astype(BF16),
                  w_mix_out[l].astype(BF16))
        x2 = _ffn(x2, row(pre_ffn_g), row(post_ffn_g), w_ffn_gate[l].astype(BF16),
                  w_ffn_up[l].astype(BF16), w_ffn_down[l].astype(BF16))
    return x2.reshape(b, s, D_MODEL)
```

```python
import functools
import math

import numpy as np
import jax
import jax.numpy as jnp
from jax import lax
from jax.experimental import pallas as pl
from jax.experimental.pallas import tpu as pltpu

F32 = jnp.float32
BF16 = jnp.bfloat16

D_MODEL = 1024
HEAD_DIM = 64
SCALE = HEAD_DIM ** -0.5
LOG2E = math.log2(math.e)
Q_SCALE = SCALE * LOG2E
NSA_HEADS = 8
NSA_GROUPS = 2
NSA_HPG = NSA_HEADS // NSA_GROUPS
CMP_LEN = 32
CMP_STRIDE = 16
CMP_HIDDEN = 128
SEL_BLOCK = 64
SEL_TOPN = 8
WINDOW = 256
MOBA_HEADS = 4
MOBA_BLOCK = 256
MOBA_TOPK = 3
MEM_HEADS = 4
REL_BUCKETS = 32
REL_MAX_DIST = 128
N_BIAS_HEADS = NSA_HEADS + MOBA_HEADS
RMS_EPS = 1e-6
NEG_INF = -1e30
FORCE_SCORE = 1e4

NSA_Q_W = NSA_HEADS * HEAD_DIM
NSA_KV_W = NSA_GROUPS * HEAD_DIM
NSA_GATE_W = NSA_HEADS * 3
MOBA_W = MOBA_HEADS * HEAD_DIM
MEM_W = MEM_HEADS * HEAD_DIM
ATT_W = NSA_Q_W + 6 * NSA_KV_W + NSA_GATE_W + 3 * MOBA_W + MEM_W
LANES = 128
SUBLANES = 8
BF16_ROWS = 16
MXU_COLS = 256
GATE_PAD = LANES
TILE = 256
HALF = TILE // 2
RUN_AHEAD = 2
N_CMP_PAD = 128
V_AUG = HEAD_DIM + BF16_ROWS
MASKED_BUCKET = REL_BUCKETS
VMEM_LIMIT = 56 * 1024 * 1024


def _dot(a, b):
    return jnp.dot(a, b, preferred_element_type=F32)


def _split_bf16(x):
    hi = x.astype(BF16)
    lo = (x - hi.astype(F32)).astype(BF16)
    return hi, lo


def _rms(x, g):
    return x * lax.rsqrt(jnp.mean(x * x, axis=-1, keepdims=True) + RMS_EPS) * g


def _params(sem):
    return pltpu.CompilerParams(dimension_semantics=sem, vmem_limit_bytes=VMEM_LIMIT)


def _const_spec(shape):
    nd = len(shape)
    return pl.BlockSpec(shape, lambda *_: (0,) * nd, pipeline_mode=pl.Buffered(1))


_INPROJ_OUTS = (
    ("qn", NSA_Q_W, BF16, True),
    ("kc", NSA_KV_W, F32, False), ("vc", NSA_KV_W, F32, False),
    ("ks", NSA_KV_W, BF16, False), ("vs", NSA_KV_W, BF16, False),
    ("kw", NSA_KV_W, BF16, False), ("vw", NSA_KV_W, BF16, False),
    ("gn", GATE_PAD, F32, False),
    ("qm", MOBA_W, BF16, True), ("km", MOBA_W, BF16, False), ("vm", MOBA_W, BF16, False),
    ("qx", MEM_W, BF16, True),
)
_INPROJ_W = sum(o[1] for o in _INPROJ_OUTS)
_INPROJ_CHUNKED = ("kc", "vc")
_INPROJ_TRANSPOSED = ("qn", "gn", "qm", "qx")
KS_AUG_W = NSA_GROUPS * 2 * HEAD_DIM


def _inproj_out_width(name, width):
    return KS_AUG_W if name == "ks" else width


def _inproj_kernel(x_ref, g_ref, w_ref, e_ref, *refs):
    out_refs, rows_sc = refs[:-1], refs[-1]
    runs, lo = [], 0
    for out in zip(_INPROJ_OUTS, out_refs):
        if runs and runs[-1][1] < MXU_COLS:
            runs[-1][0].append(out)
            runs[-1][1] += out[0][1]
        else:
            runs.append([[out], out[0][1], lo])
        lo += out[0][1]
    tm = rows_sc.shape[1]
    for t in range(rows_sc.shape[0]):
        tile = slice(t * tm, (t + 1) * tm)
        h = _rms(x_ref[tile, :], g_ref[...]).astype(BF16)
        for outs, run_width, run_lo in runs:
            y_run = _dot(h, w_ref[:, run_lo:run_lo + run_width])
            lo = 0
            for (name, width, dtype, scaled), o_ref in outs:
                y = y_run[:, lo:lo + width]
                if scaled:
                    y = y * Q_SCALE
                if name == "gn":
                    y = jax.nn.sigmoid(y)
                if name in _INPROJ_TRANSPOSED:
                    o_ref[:, tile] = y.T.astype(dtype)
                elif name in _INPROJ_CHUNKED:
                    rows_sc[t] = y
                    n_rows = tm // CMP_STRIDE
                    for j in range(CMP_STRIDE):
                        o_ref[t * n_rows:(t + 1) * n_rows, j * width:(j + 1) * width] = (
                            rows_sc[t, pl.ds(j, n_rows, stride=CMP_STRIDE), :])
                else:
                    y = y.astype(dtype)
                    if name == "ks":
                        e = e_ref[tile, :]
                        y = _lane_cat([y[:, :HEAD_DIM], e, y[:, HEAD_DIM:], e])
                    o_ref[tile, :] = y
                lo += width


def _inproj(x2, g, w, e_cols, tm=512, tiles_per_step=2):
    m = x2.shape[0]
    step = tm * tiles_per_step
    assert m % step == 0 and e_cols.shape[0] % step == 0
    steps_per_seq = e_cols.shape[0] // step
    out_specs, out_shape = [], []
    for name, width, dtype, _ in _INPROJ_OUTS:
        if name in _INPROJ_TRANSPOSED:
            out_specs.append(pl.BlockSpec((width, step), lambda i: (0, i)))
            out_shape.append(jax.ShapeDtypeStruct((width, m), dtype))
            continue
        rows, width = (CMP_STRIDE, CMP_STRIDE * width) if name in _INPROJ_CHUNKED else (1, _inproj_out_width(name, width))
        out_specs.append(pl.BlockSpec((step // rows, width), lambda i: (i, 0)))
        out_shape.append(jax.ShapeDtypeStruct((m // rows, width), dtype))
    return pl.pallas_call(
        _inproj_kernel,
        grid=(m // step,),
        in_specs=[pl.BlockSpec((step, D_MODEL), lambda i: (i, 0)),
                  _const_spec((1, D_MODEL)),
                  _const_spec((D_MODEL, _INPROJ_W)),
                  pl.BlockSpec((step, HEAD_DIM), lambda i: (i % steps_per_seq, 0))],
        out_specs=out_specs,
        out_shape=out_shape,
        scratch_shapes=[pltpu.VMEM((tiles_per_step, tm, NSA_KV_W), F32)],
        compiler_params=_params(("parallel",)),
        name="inproj",
    )(x2, g, w, e_cols)


def _compress_kernel(rk_ref, rv_ref, pk_ref, pv_ref, w1k_ref, w1v_ref, w2k_ref, w2v_ref, kc_ref, vc_ref):
    nb = rk_ref.shape[0]

    def one(r_ref, p_ref, w1_ref, w2_ref):
        r = r_ref[...].reshape(nb * N_CMP_PAD, r_ref.shape[2])
        top = _dot((r + p_ref[0:1, :]).astype(BF16), w1_ref[0])
        bot = _dot((r + p_ref[1:2, :]).astype(BF16), w1_ref[1])
        hid = top + pltpu.roll(bot, nb * N_CMP_PAD - 1, 0)
        act = jax.nn.gelu(hid).astype(BF16)
        return jnp.concatenate(
            [_dot(act[:, g * CMP_HIDDEN:(g + 1) * CMP_HIDDEN], w2_ref[...]) for g in range(NSA_GROUPS)], axis=1)

    k_out = one(rk_ref, pk_ref, w1k_ref, w2k_ref)
    v_out = one(rv_ref, pv_ref, w1v_ref, w2v_ref)
    for n in range(nb):
        rows = slice(n * N_CMP_PAD, (n + 1) * N_CMP_PAD)
        kc_ref[n] = k_out[rows].astype(BF16)
        vc_ref[n] = v_out[rows].T.astype(BF16)


def _compress(rk, rv, pk, pv, w1k, w1v, w2k, w2v):
    b = rk.shape[0]
    rw = rk.shape[2]
    nb = 2 if b % 2 == 0 else 1
    r_spec = pl.BlockSpec((nb, N_CMP_PAD, rw), lambda i: (i, 0, 0))
    o_spec = pl.BlockSpec((nb, N_CMP_PAD, NSA_KV_W), lambda i: (i, 0, 0))
    return pl.pallas_call(
        _compress_kernel,
        grid=(b // nb,),
        in_specs=[r_spec, r_spec, _const_spec(pk.shape), _const_spec(pv.shape),
                  _const_spec(w1k.shape), _const_spec(w1v.shape),
                  _const_spec(w2k.shape), _const_spec(w2v.shape)],
        out_specs=[o_spec, o_spec],
        out_shape=[jax.ShapeDtypeStruct((b, N_CMP_PAD, NSA_KV_W), BF16)] * 2,
        compiler_params=_params(("parallel",)),
        name="compress",
    )(rk, rv, pk, pv, w1k, w1v, w2k, w2v)


def _compress_weights(pos, w1):
    half = CMP_LEN // 2
    p = pos.reshape(2, half, 1, HEAD_DIM)
    p = jnp.broadcast_to(p, (2, half, NSA_GROUPS, HEAD_DIM)).reshape(2, half * NSA_KV_W)
    w = w1.reshape(2, half, HEAD_DIM, CMP_HIDDEN)
    eye = jnp.eye(NSA_GROUPS, dtype=w1.dtype)
    wbd = jnp.einsum("ajdm,gk->ajgdkm", w, eye).reshape(2, half * NSA_KV_W, NSA_GROUPS * CMP_HIDDEN)
    return p.astype(F32), wbd.astype(BF16)


def _memkv_kernel(m_ref, g_ref, w_ref, k_ref, v_ref):
    h = _rms(m_ref[...], g_ref[...]).astype(BF16)
    k_ref[...] = _dot(h, w_ref[:, :MEM_W]).astype(BF16)
    v_ref[...] = _dot(h, w_ref[:, MEM_W:]).astype(BF16)


def _memkv(mem2, g, w, tm=512):
    m = mem2.shape[0]
    tm = min(tm, m)
    o_spec = pl.BlockSpec((tm, MEM_W), lambda i: (i, 0))
    return pl.pallas_call(
        _memkv_kernel,
        grid=(m // tm,),
        in_specs=[pl.BlockSpec((tm, D_MODEL), lambda i: (i, 0)), _const_spec((1, D_MODEL)),
                  _const_spec((D_MODEL, 2 * MEM_W))],
        out_specs=[o_spec, o_spec],
        out_shape=[jax.ShapeDtypeStruct((m, MEM_W), BF16)] * 2,
        compiler_params=_params(("parallel",)),
        name="memkv",
    )(mem2, g, w)


def _expand_kernel(idx_ref, bias_ref, o_ref, *, head0, n_heads, heads_per_group):
    rows, cols = idx_ref.shape

    def body(i, carry):
        r = pl.multiple_of(i * SUBLANES, SUBLANES)
        for c0 in range(0, cols, TILE):
            idx = idx_ref[pl.ds(r, SUBLANES), c0:c0 + TILE]
            out = [jnp.full(idx.shape, NEG_INF, F32)] * n_heads
            for bkt in range(REL_BUCKETS):
                hit = idx == bkt
                out = [jnp.where(hit, bias_ref[bkt, head0 + h], out[h]) for h in range(n_heads)]
            for h in range(n_heads):
                col = (h % heads_per_group) * cols + c0
                o_ref[h // heads_per_group, pl.ds(r, SUBLANES), col:col + TILE] = out[h] * LOG2E
        return carry

    lax.fori_loop(0, rows // SUBLANES, body, 0)


def _expand(idx, rel_bias, head0, n_heads, heads_per_group=1):
    rows, cols = idx.shape
    return pl.pallas_call(
        functools.partial(_expand_kernel, head0=head0, n_heads=n_heads, heads_per_group=heads_per_group),
        in_specs=[pl.BlockSpec(memory_space=pltpu.VMEM), pl.BlockSpec(memory_space=pltpu.SMEM)],
        out_specs=pl.BlockSpec(memory_space=pltpu.VMEM),
        out_shape=jax.ShapeDtypeStruct((n_heads // heads_per_group, rows, heads_per_group * cols), F32),
        compiler_params=pltpu.CompilerParams(vmem_limit_bytes=VMEM_LIMIT),
        name="bias_expand",
    )(idx, rel_bias)


def _t5_bucket_np(dist):
    dist = np.maximum(dist, 0)
    max_exact = REL_BUCKETS // 2
    logd = np.log(np.maximum(dist, 1).astype(np.float32) / max_exact) / math.log(REL_MAX_DIST / max_exact)
    large = np.minimum(max_exact + (logd * (REL_BUCKETS - max_exact)).astype(np.int32), REL_BUCKETS - 1)
    return np.where(dist < max_exact, dist, large).astype(np.int32)


def _bucket_tables(s):
    j = np.arange(TILE)[:, None]
    i = np.arange(TILE)[None, :]
    assert TILE + 1 >= REL_MAX_DIST
    tiles = []
    for d in range(2):
        dist = d * TILE + i - j
        tiles.append(np.where(dist >= 0, _t5_bucket_np(dist), MASKED_BUCKET))
    dist1 = TILE + i - j
    win = np.where(dist1 < WINDOW, _t5_bucket_np(dist1), MASKED_BUCKET)
    n_cmp = (s - CMP_LEN) // CMP_STRIDE + 1
    assert n_cmp * CMP_STRIDE + CMP_LEN - 1 >= s or n_cmp == N_CMP_PAD
    rel = np.arange(N_CMP_PAD + (s - TILE) // CMP_STRIDE)[:, None] - (s - TILE) // CMP_STRIDE
    dist_c = i - (rel * CMP_STRIDE + CMP_LEN - 1)
    cmp_idx = np.where(dist_c >= 0, _t5_bucket_np(dist_c), MASKED_BUCKET)
    as_i32 = lambda a: jnp.asarray(a.astype(np.int32))
    return as_i32(np.concatenate(tiles, axis=0)), as_i32(win), as_i32(cmp_idx)


def _overlap_table(s):
    n_cmp = (s - CMP_LEN) // CMP_STRIDE + 1
    n_sel = s // SEL_BLOCK
    cs = np.arange(n_cmp) * CMP_STRIDE
    ss = np.arange(n_sel) * SEL_BLOCK
    ov = np.clip(np.minimum(cs[:, None] + CMP_LEN, ss[None, :] + SEL_BLOCK)
                 - np.maximum(cs[:, None], ss[None, :]), 0, None).astype(np.float32) / CMP_LEN
    ovt = np.zeros((n_sel, N_CMP_PAD), np.float32)
    ovt[:, :n_cmp] = ov.T
    return jnp.asarray(ovt, BF16)


def _store_v_aug(vt_sc, idx, vt):
    ones = jnp.ones((BF16_ROWS, vt.shape[1]), BF16)
    vt_sc[idx] = jnp.concatenate([vt.astype(BF16), ones], axis=0)


def _lane_cat(xs):
    return jnp.concatenate(xs, axis=1)


def _query_halves(x):
    n = x.shape[-1] // TILE
    first = _lane_cat([x[:, k * TILE:k * TILE + HALF] for k in range(n)])
    second = _lane_cat([x[:, k * TILE + HALF:(k + 1) * TILE] for k in range(n)])
    return first, second


def _join_query_halves(first, second):
    n = first.shape[-1] // HALF
    return _lane_cat([part for k in range(n)
                      for part in (first[:, k * HALF:(k + 1) * HALF], second[:, k * HALF:(k + 1) * HALF])])


def _triangle_scores(k, q, table, causal):
    q_first, q_second = _query_halves(q)
    lo, hi = (0, HALF), (HALF, TILE)
    if causal:
        return _dot(k(*lo), q) + table(*lo), _dot(k(*hi), q_second) + _query_halves(table(*hi))[1]
    return _dot(k(*hi), q) + table(*hi), _dot(k(*lo), q_first) + _query_halves(table(*lo))[0]


def _triangle_softmax(scores, pv_lo, pv_hi, causal):
    s_wide, s_narrow = scores
    pv_wide, pv_narrow = (pv_lo, pv_hi) if causal else (pv_hi, pv_lo)
    mw_first, mw_second = _query_halves(jnp.max(s_wide, axis=0, keepdims=True))
    m_narrow = jnp.max(s_narrow, axis=0, keepdims=True)
    if causal:
        m_narrow = jnp.maximum(m_narrow, mw_second)
        m = _join_query_halves(mw_first, m_narrow)
    else:
        m_narrow = jnp.maximum(m_narrow, mw_first)
        m = _join_query_halves(m_narrow, mw_second)
    aw_first, aw_second = _query_halves(pv_wide(jnp.exp2(s_wide - m).astype(BF16)))
    a_narrow = pv_narrow(jnp.exp2(s_narrow - m_narrow).astype(BF16))
    if causal:
        return m, _join_query_halves(aw_first, aw_second + a_narrow)
    return m, _join_query_halves(aw_first + a_narrow, aw_second)


def _flash_pipelined(own, streams, s_sc, m_ref, acc_ref):
    has_prev = jnp.where(own > 0, 1.0, 0.0).astype(F32)
    prev = jnp.maximum(own - 1, 0)
    n_far = jnp.maximum(own - 1, 0)

    def absorb(g, s, kt, c_row, w_row):
        u = jnp.max(s, axis=0, keepdims=True) + c_row
        m_old = m_ref[g]
        m_new = jnp.maximum(m_old, jnp.where(w_row > 0.0, u, NEG_INF))
        alpha = jnp.exp2(m_old - m_new)
        shift = jnp.maximum(m_new, u) - c_row
        p = jnp.exp2(s - shift).astype(BF16)
        acc_ref[g] = alpha * acc_ref[g] + w_row * streams[g]["pv"](kt)(p)
        m_ref[g] = m_new

    def absorb_slot(g, i):
        is_prev = i == 0
        kt = jnp.where(is_prev, prev, i - 1)
        c_row = jnp.where(is_prev, 0.0, streams[g]["c_far"])
        w_row = streams[g]["w"](kt) * jnp.where(is_prev, has_prev, 1.0)
        absorb(g, s_sc[g], kt, c_row, w_row)

    def body(i, carry):
        for g in reversed(range(len(streams))):
            nxt = streams[g]["far"](i)
            absorb_slot(g, i)
            s_sc[g] = nxt
        return carry

    lax.fori_loop(0, n_far, body, 0)
    for g in range(len(streams)):
        absorb_slot(g, n_far)


def _run_ahead(jobs, depth=RUN_AHEAD):
    pending = [job[0]() for job in jobs[:depth]]
    for k, (_, consume) in enumerate(jobs):
        if k + depth < len(jobs):
            pending.append(jobs[k + depth][0]())
        consume(pending.pop(0))


def _flash_start_jobs(g, stream, s_sc, m_ref, acc_ref):
    def init(scores):
        m_ref[g], acc_ref[g] = stream["own_softmax"](scores)

    def park(scores):
        s_sc[g] = scores

    return [(stream["own_scores"], init), (stream["prev"], park)]


def _softmax_av(s_list, pv_list):
    m = s_list[0].max(axis=0, keepdims=True)
    for s in s_list[1:]:
        m = jnp.maximum(m, s.max(axis=0, keepdims=True))
    acc = None
    for s, pv in zip(s_list, pv_list):
        part = pv(jnp.exp2(s - m).astype(BF16))
        acc = part if acc is None else acc + part
    return acc


def _normalize(acc):
    return acc[:HEAD_DIM] * (1.0 / acc[HEAD_DIM:HEAD_DIM + 1])


def _rank_before(score, n_cand):
    ranks = []
    for r0 in range(0, score.shape[0], SUBLANES):
        tile = score[r0:r0 + SUBLANES]
        blk = lax.broadcasted_iota(jnp.int32, tile.shape, 0) + r0
        rank = jnp.zeros(tile.shape, F32)
        for m in range(n_cand):
            row = score[m:m + 1, :]
            if m < r0:
                before = jnp.where(row >= tile, 1.0, 0.0)
            elif m >= r0 + SUBLANES:
                before = jnp.where(row > tile, 1.0, 0.0)
            else:
                before = jnp.where(blk > m, jnp.where(row >= tile, 1.0, 0.0), jnp.where(row > tile, 1.0, 0.0))
            rank = rank + before
        ranks.append(rank)
    return jnp.concatenate(ranks, axis=0)


def _nsa_steps(qi, maybe_first, q_ref, gn_ref, kc_ref, vct_ref, ks_ref, vs_ref, kw_ref, vw_ref,
               bct_ref, tt_ref, twt_ref, cfar_ref, ovt_ref, o_ref,
               vst_sc, vwt_sc, qa_sc, og_sc, acc_sc):
    nt = ks_ref.shape[1]
    n_sel = ovt_ref.shape[0]

    def transpose_v():
        for kt in range(nt):
            vs_t = vs_ref[0, kt].astype(F32).T
            vw_t = vw_ref[0, kt].astype(F32).T
            for g in range(NSA_GROUPS):
                _store_v_aug(vst_sc, (kt, g), vs_t[g * HEAD_DIM:(g + 1) * HEAD_DIM])
                _store_v_aug(vwt_sc, (kt, g), vw_t[g * HEAD_DIM:(g + 1) * HEAD_DIM])

    if maybe_first:
        pl.when(qi == 0)(transpose_v)

    pos = lax.broadcasted_iota(jnp.int32, (1, TILE), 1) + qi * TILE
    cur = pos // SEL_BLOCK
    has_cmp = pos >= CMP_LEN - 1
    blk = lax.broadcasted_iota(jnp.int32, (n_sel, TILE), 0)
    prev = jnp.maximum(qi - 1, 0)
    gates = gn_ref[...]

    gsls = [slice(g * HEAD_DIM, (g + 1) * HEAD_DIM) for g in range(NSA_GROUPS)]
    group_heads = [[g * NSA_HPG + j for j in range(NSA_HPG)] for g in range(NSA_GROUPS)]

    def gate(g, branch):
        return _lane_cat([gates[3 * h + branch:3 * h + branch + 1, :] for h in group_heads[g]])

    ones_row = jnp.ones((1, NSA_HPG * TILE), F32)

    def sel_stream(g):
        def qk(kt):
            return _dot(ks_ref[0, kt, :, g * 2 * HEAD_DIM:(g + 1) * 2 * HEAD_DIM], qa_sc[g])

        def own_scores():
            return _triangle_scores(
                lambda lo, hi: ks_ref[0, qi, lo:hi, g * 2 * HEAD_DIM:(g + 1) * 2 * HEAD_DIM], qa_sc[g],
                lambda lo, hi: tt_ref[g, 0, lo:hi, :], causal=True)

        def own_softmax(scores):
            return _triangle_softmax(
                scores, lambda pr: _dot(vst_sc[qi, g, :, 0:HALF], pr),
                lambda pr: _dot(vst_sc[qi, g, :, HALF:TILE], pr), causal=True)

        return dict(own_scores=own_scores, own_softmax=own_softmax, prev=lambda: qk(prev) + tt_ref[g, 1], far=qk,
                    c_far=cfar_ref[g], w=lambda kt: ones_row,
                    pv=lambda kt: (lambda pr: _dot(vst_sc[kt, g], pr)))

    def group_jobs(g):
        heads = group_heads[g]
        win = {}

        def v_half(kt, lo, hi):
            return lambda pr: _dot(vwt_sc[kt, g, :, lo:hi], pr)

        def window_own_scores():
            q4 = _lane_cat([q_ref[h * HEAD_DIM:(h + 1) * HEAD_DIM, :] for h in heads])
            qa_sc[g, 0:HEAD_DIM, :] = q4
            qa_sc[g, HEAD_DIM + n_sel:, :] = jnp.zeros((HEAD_DIM - n_sel, NSA_HPG * TILE), BF16)
            return _triangle_scores(lambda lo, hi: kw_ref[0, qi, lo:hi, gsls[g]], q4,
                                    lambda lo, hi: tt_ref[g, 0, lo:hi, :], causal=True)

        def window_own(scores):
            win["own"] = _triangle_softmax(scores, v_half(qi, 0, HALF), v_half(qi, HALF, TILE), causal=True)

        def window_prev_scores():
            return _triangle_scores(lambda lo, hi: kw_ref[0, prev, lo:hi, gsls[g]], qa_sc[g, 0:HEAD_DIM, :],
                                    lambda lo, hi: twt_ref[g, lo:hi, :], causal=False)

        def window_prev(scores):
            m_prev, acc_prev = _triangle_softmax(scores, v_half(prev, 0, HALF), v_half(prev, HALF, TILE), causal=False)
            m_prev = jnp.where(qi == 0, NEG_INF, m_prev)
            m_own, acc_own = win["own"]
            m_win = jnp.maximum(m_own, m_prev)
            acc_w = acc_own * jnp.exp2(m_own - m_win) + acc_prev * jnp.exp2(m_prev - m_win)
            og_sc[g] = gate(g, 2) * _normalize(acc_w)

        def compressed_scores():
            rows_per_tile = TILE // CMP_STRIDE
            c_rows = pl.ds(pl.multiple_of((nt - 1 - qi) * rows_per_tile, rows_per_tile), N_CMP_PAD)
            return (_dot(kc_ref[0, :, gsls[g]], qa_sc[g, 0:HEAD_DIM, :])
                    + _lane_cat([bct_ref[h, c_rows, :] for h in heads]))

        def compressed(s):
            e = jnp.exp2(s - jnp.max(s, axis=0, keepdims=True))
            p = e * jnp.where(_lane_cat([has_cmp] * NSA_HPG), 1.0 / jnp.sum(e, axis=0, keepdims=True), 0.0)
            psum = p[:, :TILE]
            for j in range(1, NSA_HPG):
                psum = psum + p[:, j * TILE:(j + 1) * TILE]
            og_sc[g] = og_sc[g] + gate(g, 0) * _dot(vct_ref[0, gsls[g], :], p.astype(BF16))
            win["psum"] = psum

        def importance():
            p_hi, p_lo = _split_bf16(win["psum"])
            return _dot(ovt_ref[...], p_hi) + _dot(ovt_ref[...], p_lo)

        def select(imp):
            forced = (blk == 0) | (blk == cur) | (blk == cur - 1)
            score = jnp.where(forced, FORCE_SCORE, jnp.where(blk <= cur, imp, NEG_INF))
            rank = _rank_before(score, n_sel)
            sel = jnp.where(rank < SEL_TOPN, jnp.where(score > NEG_INF / 2, 0.0, NEG_INF), NEG_INF)
            qa_sc[g, HEAD_DIM:HEAD_DIM + n_sel, :] = _lane_cat([sel.astype(BF16)] * NSA_HPG)

        return [(window_own_scores, window_own), (window_prev_scores, window_prev), (compressed_scores, compressed),
                (importance, select)]

    per_group = [group_jobs(g) for g in range(NSA_GROUPS)]
    assert NSA_GROUPS == 2 and RUN_AHEAD <= 2
    (a0, b0, c0, i0), (a1, b1, c1, i1) = per_group
    yield [a0, b0, c0, a1, b1, i0, c1], i1
    yield [sel_stream(g) for g in range(NSA_GROUPS)]

    for g in range(NSA_GROUPS):
        o = og_sc[g] + gate(g, 1) * _normalize(acc_sc[g])
        for j, h in enumerate(group_heads[g]):
            o_ref[h * HEAD_DIM:(h + 1) * HEAD_DIM, :] = o[:, j * TILE:(j + 1) * TILE].astype(BF16)


def _moba_steps(c, maybe_first, qm_ref, km_ref, vm_ref, qx_ref, mk_ref, mv_ref, tt_ref, cfar_ref, om_ref, ox_ref,
                vmt_sc, mvt_sc, kmean_sc, qbd_sc, sel_sc, acc_sc, slot):
    nt = km_ref.shape[1]
    hsls = [slice(h * HEAD_DIM, (h + 1) * HEAD_DIM) for h in range(MOBA_HEADS)]

    def per_sequence():
        kmean_sc[...] = jnp.zeros(kmean_sc.shape, F32)
        for n in range(nt):
            kmean_sc[n:n + 1, :] = jnp.sum(km_ref[0, n].astype(F32), axis=0, keepdims=True) * (1.0 / MOBA_BLOCK)
            vt = vm_ref[0, n].astype(F32).T
            for h in range(MOBA_HEADS):
                _store_v_aug(vmt_sc, (n, h), vt[hsls[h]])
        mvt = mv_ref[0].astype(F32).T
        for h in range(MOBA_HEADS):
            _store_v_aug(mvt_sc, h, mvt[hsls[h]])

    if maybe_first:
        pl.when(c == 0)(per_sequence)

    row_head = lax.broadcasted_iota(jnp.int32, (MOBA_W, TILE), 0) // HEAD_DIM

    def block_diag(q_ref):
        q_t = q_ref[...].astype(F32)
        return _lane_cat([jnp.where(row_head == h, q_t, 0.0) for h in range(MOBA_HEADS)]).astype(BF16)

    def per_head_pv(vts):
        return lambda pr: _lane_cat([_dot(vts(h), pr[:, h * TILE:(h + 1) * TILE]) for h in range(MOBA_HEADS)])

    def store_heads(o_t, out_ref):
        for h in range(MOBA_HEADS):
            out_ref[hsls[h], :] = o_t[:, h * TILE:(h + 1) * TILE].astype(BF16)

    n_rows = -(-nt // SUBLANES) * SUBLANES

    def gate_scores():
        qbd = block_diag(qm_ref)
        qbd_sc[...] = qbd
        km_hi, km_lo = _split_bf16(kmean_sc[...])
        return (_dot(km_hi, qbd) + _dot(km_lo, qbd))[:n_rows]

    def select(gate):
        blk = lax.broadcasted_iota(jnp.int32, gate.shape, 0)
        score = jnp.where(blk < c, gate, NEG_INF * Q_SCALE)
        rank = _rank_before(score, nt)
        sel_sc[0:n_rows, :] = jnp.where(rank < MOBA_TOPK, jnp.where(score > NEG_INF * Q_SCALE / 2, 1.0, 0.0), 0.0)

    qk = lambda n: _dot(km_ref[0, n], qbd_sc[...])
    def own_pv(lo, hi):
        def pv(pr):
            width = pr.shape[1] // MOBA_HEADS
            return _lane_cat([_dot(vmt_sc[c, h, :, lo:hi], pr[:, h * width:(h + 1) * width])
                              for h in range(MOBA_HEADS)])
        return pv

    def own_scores():
        return _triangle_scores(lambda lo, hi: km_ref[0, c, lo:hi, :], qbd_sc[...],
                                lambda lo, hi: tt_ref[0, lo:hi, :], causal=True)

    def own_softmax(scores):
        return _triangle_softmax(scores, own_pv(0, HALF), own_pv(HALF, TILE), causal=True)

    stream = dict(own_scores=own_scores, own_softmax=own_softmax,
                  prev=lambda: qk(jnp.maximum(c - 1, 0)) + tt_ref[1], far=qk,
                  c_far=cfar_ref[...], w=lambda n: sel_sc[pl.ds(n, 1), :],
                  pv=lambda n: per_head_pv(lambda h: vmt_sc[n, h]))
    def memory(s):
        store_heads(_normalize(_softmax_av([s], [per_head_pv(lambda h: mvt_sc[h])])), ox_ref)

    yield [(gate_scores, select), (lambda: _dot(mk_ref[0], block_diag(qx_ref)), memory)]
    yield [stream]
    store_heads(_normalize(acc_sc[slot]), om_ref)


N_STREAMS = NSA_GROUPS + 1
TILES_PER_STEP = 2
NSA_ROW_INPUTS = (0, 1)
MOBA_ROW_INPUTS = (0, 3)


def _attention_kernel(*refs, n_nsa_in, n_moba_in, n_nsa_scratch, n_moba_scratch):
    nsa_in, refs = refs[:n_nsa_in], refs[n_nsa_in:]
    moba_in, refs = refs[:n_moba_in], refs[n_moba_in:]
    outs, refs = refs[:3], refs[3:]
    nsa_sc, refs = refs[:n_nsa_scratch], refs[n_nsa_scratch:]
    moba_sc, refs = refs[:n_moba_scratch], refs[n_moba_scratch:]
    s_sc, m_sc, acc_sc = refs
    for t in range(TILES_PER_STEP):
        qi = pl.program_id(1) * TILES_PER_STEP + t
        cols = lambda ref: ref.at[:, t * TILE:(t + 1) * TILE]
        o_nsa, o_moba, o_mem = [cols(o) for o in outs]
        nsa_refs = [cols(r) if k in NSA_ROW_INPUTS else r for k, r in enumerate(nsa_in)]
        moba_refs = [cols(r) if k in MOBA_ROW_INPUTS else r for k, r in enumerate(moba_in)]
        nsa = _nsa_steps(qi, t == 0, *nsa_refs, o_nsa, *nsa_sc, acc_sc)
        moba = _moba_steps(qi, t == 0, *moba_refs, o_moba, o_mem, *moba_sc, acc_sc, NSA_GROUPS)
        nsa_jobs, last_selection = next(nsa)
        moba_jobs = next(moba)
        streams = next(nsa) + next(moba)
        starts = [_flash_start_jobs(g, stream, s_sc, m_sc, acc_sc) for g, stream in enumerate(streams)]
        jobs = nsa_jobs + moba_jobs + [last_selection] + starts[NSA_GROUPS] + starts[1] + starts[0]
        _run_ahead(jobs)
        _flash_pipelined(qi, streams, s_sc, m_sc, acc_sc)
        for steps in (nsa, moba):
            for _ in steps:
                pass


def _attention(b, s, qn, gn, kc, vct, ks, vs, kw, vw, bias_cmp, t_nsa, t_win, c_far_nsa, ovt,
               qm, km, vm, qx, mk, mv, t_moba, c_far_moba):
    nt = s // TILE
    mem_len = mk.shape[0] // b
    assert MOBA_TOPK <= nt - 1 and nt <= BF16_ROWS
    n_lanes = NSA_HPG * TILE
    assert MOBA_HEADS * TILE == n_lanes
    assert nt % TILES_PER_STEP == 0
    steps = nt // TILES_PER_STEP
    row_spec = lambda w: pl.BlockSpec((w, TILES_PER_STEP * TILE), lambda i, j: (0, i * steps + j))
    seq_spec = lambda w: pl.BlockSpec((1, nt, TILE, w), lambda i, j: (i, 0, 0, 0))
    per_batch = lambda rows, w: pl.BlockSpec((1, rows, w), lambda i, j: (i, 0, 0))
    tiles = lambda a: a.reshape(b, nt, TILE, a.shape[-1])
    nsa_in = [(qn, row_spec(NSA_Q_W)), (gn, row_spec(GATE_PAD)),
              (kc, per_batch(N_CMP_PAD, NSA_KV_W)), (vct, per_batch(N_CMP_PAD, NSA_KV_W)),
              (tiles(ks), seq_spec(KS_AUG_W)), (tiles(vs), seq_spec(NSA_KV_W)),
              (tiles(kw), seq_spec(NSA_KV_W)), (tiles(vw), seq_spec(NSA_KV_W)),
              (bias_cmp, _const_spec(bias_cmp.shape)), (t_nsa, _const_spec(t_nsa.shape)),
              (t_win, _const_spec(t_win.shape)), (c_far_nsa, _const_spec(c_far_nsa.shape)),
              (ovt, _const_spec(ovt.shape))]
    moba_in = [(qm, row_spec(MOBA_W)), (tiles(km), seq_spec(MOBA_W)), (tiles(vm), seq_spec(MOBA_W)),
               (qx, row_spec(MEM_W)),
               (mk.reshape(b, mem_len, MEM_W), per_batch(mem_len, MEM_W)),
               (mv.reshape(b, mem_len, MEM_W), per_batch(mem_len, MEM_W)),
               (t_moba, _const_spec(t_moba.shape)), (c_far_moba, _const_spec(c_far_moba.shape))]
    nsa_scratch =[pltpu.VMEM((nt, NSA_GROUPS, V_AUG, TILE), BF16),
                   pltpu.VMEM((nt, NSA_GROUPS, V_AUG, TILE), BF16),
                   pltpu.VMEM((NSA_GROUPS, 2 * HEAD_DIM, n_lanes), BF16),
                   pltpu.VMEM((NSA_GROUPS, HEAD_DIM, n_lanes), F32)]
    moba_scratch = [pltpu.VMEM((nt, MOBA_HEADS, V_AUG, TILE), BF16),
                    pltpu.VMEM((MEM_HEADS, V_AUG, mem_len), BF16),
                    pltpu.VMEM((BF16_ROWS, MOBA_W), F32),
                    pltpu.VMEM((MOBA_W, n_lanes), BF16),
                    pltpu.VMEM((BF16_ROWS, n_lanes), F32)]
    shared_scratch =[pltpu.VMEM((N_STREAMS, TILE, n_lanes), F32),
                      pltpu.VMEM((N_STREAMS, 1, n_lanes), F32),
                      pltpu.VMEM((N_STREAMS, V_AUG, n_lanes), F32)]
    inputs = nsa_in + moba_in
    return pl.pallas_call(
        functools.partial(_attention_kernel, n_nsa_in=len(nsa_in), n_moba_in=len(moba_in),
                          n_nsa_scratch=len(nsa_scratch), n_moba_scratch=len(moba_scratch)),
        grid=(b, steps),
        in_specs=[spec for _, spec in inputs],
        out_specs=[row_spec(NSA_Q_W), row_spec(MOBA_W), row_spec(MEM_W)],
        out_shape=[jax.ShapeDtypeStruct((w, b * s), BF16) for w in (NSA_Q_W, MOBA_W, MEM_W)],
        scratch_shapes=nsa_scratch + moba_scratch + shared_scratch,
        compiler_params=_params(("arbitrary", "arbitrary")),
        name="attention",
    )(*[a for a, _ in inputs])


MIX_CHUNK = MXU_COLS


def _mix_kernel(x_ref, on_ref, om_ref, ox_ref, g_pre_ref, g_post_ref, wg_ref, wn_ref, wm_ref, wx_ref,
                wo_ref, o_ref, merged_sc):
    tm = merged_sc.shape[1]
    for t in range(merged_sc.shape[0]):
        tile = slice(t * tm, (t + 1) * tm)
        x = x_ref[tile, :]
        h = _rms(x, g_pre_ref[...]).astype(BF16)
        rows = lambda o_ref: o_ref[:, tile].astype(F32).T.astype(BF16)
        branches = ((rows(on_ref), wn_ref), (rows(om_ref), wm_ref), (rows(ox_ref), wx_ref))

        def chunk_job(c0):
            cols = slice(c0, c0 + MIX_CHUNK)

            def produce():
                gates = [_dot(h, wg_ref[:, i * D_MODEL + c0:i * D_MODEL + c0 + MIX_CHUNK])
                         for i in range(len(branches))]
                return list(zip(gates, [_dot(o, w_ref[:, cols]) for o, w_ref in branches]))

            def consume(pairs):
                merged_sc[t, :, cols] = sum(jax.nn.sigmoid(g) * y for g, y in pairs).astype(BF16)

            return produce, consume

        _run_ahead([chunk_job(c0) for c0 in range(0, D_MODEL, MIX_CHUNK)], depth=1)
        y = _dot(merged_sc[t], wo_ref[...])
        o_ref[tile, :] = x + _rms(y, g_post_ref[...])


def _mix(x2, o_nsa, o_moba, o_mem, g_pre, g_post, w_gates, w_nsa_o, w_moba_o, w_mem_o, w_mix_out, tm=512,
         tiles_per_step=2):
    m = x2.shape[0]
    rows_per_step = tm * tiles_per_step
    assert m % rows_per_step == 0
    row = lambda w: pl.BlockSpec((rows_per_step, w), lambda i: (i, 0))
    col = lambda w: pl.BlockSpec((w, rows_per_step), lambda i: (0, i))
    return pl.pallas_call(
        _mix_kernel,
        grid=(m // rows_per_step,),
        in_specs=[row(D_MODEL), col(NSA_Q_W), col(MOBA_W), col(MEM_W),
                  _const_spec((1, D_MODEL)), _const_spec((1, D_MODEL)),
                  _const_spec(w_gates.shape), _const_spec(w_nsa_o.shape), _const_spec(w_moba_o.shape),
                  _const_spec(w_mem_o.shape), _const_spec(w_mix_out.shape)],
        out_specs=row(D_MODEL),
        out_shape=jax.ShapeDtypeStruct((m, D_MODEL), F32),
        scratch_shapes=[pltpu.VMEM((tiles_per_step, tm, D_MODEL), BF16)],
        compiler_params=_params(("parallel",)),
        name="mix",
    )(x2, o_nsa, o_moba, o_mem, g_pre, g_post, w_gates, w_nsa_o, w_moba_o, w_mem_o, w_mix_out)


FFN_CHUNK = 256


def _ffn_kernel(x_ref, g_pre_ref, g_post_ref, wg_ref, wu_ref, wd_ref, o_ref, a_sc):
    tm = a_sc.shape[1]
    d_ff = wg_ref.shape[1]
    for t in range(a_sc.shape[0]):
        rows = slice(t * tm, (t + 1) * tm)
        x = x_ref[rows, :]
        h = _rms(x, g_pre_ref[...]).astype(BF16)
        for j in range(d_ff // FFN_CHUNK):
            sl = slice(j * FFN_CHUNK, (j + 1) * FFN_CHUNK)
            a_sc[t, :, sl] = (jax.nn.silu(_dot(h, wg_ref[:, sl])) * _dot(h, wu_ref[:, sl])).astype(BF16)
        f = _dot(a_sc[t], wd_ref[...])
        o_ref[rows, :] = x + _rms(f, g_post_ref[...])


def _ffn(x2, g_pre, g_post, wg, wu, wd, tm=512, tiles_per_step=2):
    m = x2.shape[0]
    d_ff = wg.shape[1]
    rows = tm * tiles_per_step
    assert m % rows == 0
    return pl.pallas_call(
        _ffn_kernel,
        grid=(m // rows,),
        in_specs=[pl.BlockSpec((rows, D_MODEL), lambda i: (i, 0)),
                  _const_spec((1, D_MODEL)), _const_spec((1, D_MODEL)),
                  _const_spec(wg.shape), _const_spec(wu.shape), _const_spec(wd.shape)],
        out_specs=pl.BlockSpec((rows, D_MODEL), lambda i: (i, 0)),
        out_shape=jax.ShapeDtypeStruct((m, D_MODEL), F32),
        scratch_shapes=[pltpu.VMEM((tiles_per_step, tm, d_ff), BF16)],
        compiler_params=_params(("parallel",)),
        name="ffn",
    )(x2, g_pre, g_post, wg, wu, wd)


def kernel(x, mem, rel_bias, pre_mix_g, mem_norm_g, post_mix_g, w_in, cmp_pos_k, cmp_w1_k, cmp_w2_k, cmp_pos_v, cmp_w1_v, cmp_w2_v, w_mem_kv, w_nsa_o, w_moba_o, w_mem_o, w_mix_out, pre_ffn_g, post_ffn_g, w_ffn_gate, w_ffn_up, w_ffn_down):
    b, s, d_model = x.shape
    depth = w_in.shape[0]
    assert d_model == D_MODEL and s % TILE == 0 and TILE == MOBA_BLOCK == WINDOW
    assert (s - CMP_LEN) // CMP_STRIDE + 1 < N_CMP_PAD and (s // SEL_BLOCK) % SUBLANES == 0 and s // SEL_BLOCK <= HEAD_DIM
    assert w_in.shape[2] == ATT_W + 3 * D_MODEL and rel_bias.shape == (REL_BUCKETS, N_BIAS_HEADS)

    tile_idx, win_idx, cmp_idx = _bucket_tables(s)
    rel_bias = rel_bias.astype(F32)
    t_nsa = _expand(tile_idx, rel_bias, 0, NSA_HEADS, NSA_HPG).reshape(NSA_GROUPS, 2, TILE, NSA_HPG * TILE)
    t_moba = _expand(tile_idx, rel_bias, NSA_HEADS, MOBA_HEADS, MOBA_HEADS).reshape(2, TILE, MOBA_HEADS * TILE)
    t_win = _expand(win_idx, rel_bias, 0, NSA_HEADS, NSA_HPG)
    b_cmp = _expand(cmp_idx, rel_bias, 0, NSA_HEADS)
    c_far = jnp.repeat(rel_bias[REL_BUCKETS - 1] * LOG2E, TILE)
    c_far_nsa = c_far[:NSA_HEADS * TILE].reshape(NSA_GROUPS, 1, NSA_HPG * TILE)
    c_far_moba = c_far[NSA_HEADS * TILE:].reshape(1, MOBA_HEADS * TILE)
    ovt = _overlap_table(s)
    sel_cols = np.zeros((s, HEAD_DIM), np.float32)
    sel_cols[np.arange(s), np.arange(s) // SEL_BLOCK] = 1.0
    sel_cols = jnp.asarray(sel_cols, BF16)
    gate_lo = NSA_Q_W + 6 * NSA_KV_W
    rows_per_chunk = CMP_STRIDE * NSA_KV_W

    x2 = x.reshape(b * s, D_MODEL)
    mem2 = mem.reshape(-1, D_MODEL)
    for l in range(depth):
        w_att = jnp.concatenate(
            [w_in[l, :, :gate_lo + NSA_GATE_W],
             jnp.zeros((D_MODEL, GATE_PAD - NSA_GATE_W), w_in.dtype),
             w_in[l, :, gate_lo + NSA_GATE_W:ATT_W]], axis=1).astype(BF16)
        w_gates = w_in[l, :, ATT_W:].astype(BF16)
        row = lambda v: v[l].reshape(1, D_MODEL)

        qn, kc_raw, vc_raw, ks, vs, kw, vw, gn, qm, km, vm, qx = _inproj(x2, row(pre_mix_g), w_att, sel_cols)

        pk, w1k = _compress_weights(cmp_pos_k[l], cmp_w1_k[l])
        pv, w1v = _compress_weights(cmp_pos_v[l], cmp_w1_v[l])
        kc, vct = _compress(kc_raw.reshape(b, s // CMP_STRIDE, rows_per_chunk),
                            vc_raw.reshape(b, s // CMP_STRIDE, rows_per_chunk),
                            pk, pv, w1k, w1v, cmp_w2_k[l].astype(BF16), cmp_w2_v[l].astype(BF16))

        mk, mv = _memkv(mem2, row(mem_norm_g), w_mem_kv[l].astype(BF16))

        o_nsa, o_moba, o_mem = _attention(b, s, qn, gn, kc, vct, ks, vs, kw, vw, b_cmp, t_nsa, t_win, c_far_nsa, ovt,
                                          qm, km, vm, qx, mk, mv, t_moba, c_far_moba)

        x2 = _mix(x2, o_nsa, o_moba, o_mem, row(pre_mix_g), row(post_mix_g), w_gates,
                  w_nsa_o[l].astype(BF16), w_moba_o[l].astype(BF16), w_mem_o[l].astype(BF16),
                  w_mix_out[l].astype(BF16))
        x2 = _ffn(x2, row(pre_ffn_g), row(post_ffn_g), w_ffn_gate[l].astype(BF16),
                  w_ffn_up[l].astype(BF16), w_ffn_down[l].astype(BF16))
    return x2.reshape(b, s, D_MODEL)
```

```python
import functools
import math

import numpy as np
import jax
import jax.numpy as jnp
from jax import lax
from jax.experimental import pallas as pl
from jax.experimental.pallas import tpu as pltpu

F32 = jnp.float32
BF16 = jnp.bfloat16

D_MODEL = 1024
HEAD_DIM = 64
SCALE = HEAD_DIM ** -0.5
LOG2E = math.log2(math.e)
Q_SCALE = SCALE * LOG2E
NSA_HEADS = 8
NSA_GROUPS = 2
NSA_HPG = NSA_HEADS // NSA_GROUPS
CMP_LEN = 32
CMP_STRIDE = 16
CMP_HIDDEN = 128
SEL_BLOCK = 64
SEL_TOPN = 8
WINDOW = 256
MOBA_HEADS = 4
MOBA_BLOCK = 256
MOBA_TOPK = 3
MEM_HEADS = 4
REL_BUCKETS = 32
REL_MAX_DIST = 128
N_BIAS_HEADS = NSA_HEADS + MOBA_HEADS
RMS_EPS = 1e-6
NEG_INF = -1e30
FORCE_SCORE = 1e4

NSA_Q_W = NSA_HEADS * HEAD_DIM
NSA_KV_W = NSA_GROUPS * HEAD_DIM
NSA_GATE_W = NSA_HEADS * 3
MOBA_W = MOBA_HEADS * HEAD_DIM
MEM_W = MEM_HEADS * HEAD_DIM
ATT_W = NSA_Q_W + 6 * NSA_KV_W + NSA_GATE_W + 3 * MOBA_W + MEM_W
LANES = 128
SUBLANES = 8
BF16_ROWS = 16
MXU_COLS = 256
GATE_PAD = LANES
TILE = 256
HALF = TILE // 2
RUN_AHEAD = 2
N_CMP_PAD = 128
V_AUG = HEAD_DIM + BF16_ROWS
MASKED_BUCKET = REL_BUCKETS
VMEM_LIMIT = 56 * 1024 * 1024


def _dot(a, b):
    return jnp.dot(a, b, preferred_element_type=F32)


def _split_bf16(x):
    hi = x.astype(BF16)
    lo = (x - hi.astype(F32)).astype(BF16)
    return hi, lo


def _rms(x, g):
    return x * lax.rsqrt(jnp.mean(x * x, axis=-1, keepdims=True) + RMS_EPS) * g


def _params(sem):
    return pltpu.CompilerParams(dimension_semantics=sem, vmem_limit_bytes=VMEM_LIMIT)


def _const_spec(shape):
    nd = len(shape)
    return pl.BlockSpec(shape, lambda *_: (0,) * nd, pipeline_mode=pl.Buffered(1))


_INPROJ_OUTS = (
    ("qn", NSA_Q_W, BF16, True),
    ("kc", NSA_KV_W, F32, False), ("vc", NSA_KV_W, F32, False),
    ("ks", NSA_KV_W, BF16, False), ("vs", NSA_KV_W, BF16, False),
    ("kw", NSA_KV_W, BF16, False), ("vw", NSA_KV_W, BF16, False),
    ("gn", GATE_PAD, F32, False),
    ("qm", MOBA_W, BF16, True), ("km", MOBA_W, BF16, False), ("vm", MOBA_W, BF16, False),
    ("qx", MEM_W, BF16, True),
)
_INPROJ_W = sum(o[1] for o in _INPROJ_OUTS)
_INPROJ_CHUNKED = ("kc", "vc")
_INPROJ_TRANSPOSED = ("qn", "gn", "qm", "qx")
KS_AUG_W = NSA_GROUPS * 2 * HEAD_DIM


def _inproj_out_width(name, width):
    return KS_AUG_W if name == "ks" else width


def _inproj_kernel(x_ref, g_ref, w_ref, e_ref, *refs):
    out_refs, rows_sc = refs[:-1], refs[-1]
    runs, lo = [], 0
    for out in zip(_INPROJ_OUTS, out_refs):
        if runs and runs[-1][1] < MXU_COLS:
            runs[-1][0].append(out)
            runs[-1][1] += out[0][1]
        else:
            runs.append([[out], out[0][1], lo])
        lo += out[0][1]
    tm = rows_sc.shape[1]
    for t in range(rows_sc.shape[0]):
        tile = slice(t * tm, (t + 1) * tm)
        h = _rms(x_ref[tile, :], g_ref[...]).astype(BF16)
        for outs, run_width, run_lo in runs:
            y_run = _dot(h, w_ref[:, run_lo:run_lo + run_width])
            lo = 0
            for (name, width, dtype, scaled), o_ref in outs:
                y = y_run[:, lo:lo + width]
                if scaled:
                    y = y * Q_SCALE
                if name == "gn":
                    y = jax.nn.sigmoid(y)
                if name in _INPROJ_TRANSPOSED:
                    o_ref[:, tile] = y.T.astype(dtype)
                elif name in _INPROJ_CHUNKED:
                    rows_sc[t] = y
                    n_rows = tm // CMP_STRIDE
                    for j in range(CMP_STRIDE):
                        o_ref[t * n_rows:(t + 1) * n_rows, j * width:(j + 1) * width] = (
                            rows_sc[t, pl.ds(j, n_rows, stride=CMP_STRIDE), :])
                else:
                    y = y.astype(dtype)
                    if name == "ks":
                        e = e_ref[tile, :]
                        y = _lane_cat([y[:, :HEAD_DIM], e, y[:, HEAD_DIM:], e])
                    o_ref[tile, :] = y
                lo += width


def _inproj(x2, g, w, e_cols, tm=512, tiles_per_step=2):
    m = x2.shape[0]
    step = tm * tiles_per_step
    assert m % step == 0 and e_cols.shape[0] % step == 0
    steps_per_seq = e_cols.shape[0] // step
    out_specs, out_shape = [], []
    for name, width, dtype, _ in _INPROJ_OUTS:
        if name in _INPROJ_TRANSPOSED:
            out_specs.append(pl.BlockSpec((width, step), lambda i: (0, i)))
            out_shape.append(jax.ShapeDtypeStruct((width, m), dtype))
            continue
        rows, width = (CMP_STRIDE, CMP_STRIDE * width) if name in _INPROJ_CHUNKED else (1, _inproj_out_width(name, width))
        out_specs.append(pl.BlockSpec((step // rows, width), lambda i: (i, 0)))
        out_shape.append(jax.ShapeDtypeStruct((m // rows, width), dtype))
    return pl.pallas_call(
        _inproj_kernel,
        grid=(m // step,),
        in_specs=[pl.BlockSpec((step, D_MODEL), lambda i: (i, 0)),
                  _const_spec((1, D_MODEL)),
                  _const_spec((D_MODEL, _INPROJ_W)),
                  pl.BlockSpec((step, HEAD_DIM), lambda i: (i % steps_per_seq, 0))],
        out_specs=out_specs,
        out_shape=out_shape,
        scratch_shapes=[pltpu.VMEM((tiles_per_step, tm, NSA_KV_W), F32)],
        compiler_params=_params(("parallel",)),
        name="inproj",
    )(x2, g, w, e_cols)


def _compress_kernel(rk_ref, rv_ref, pk_ref, pv_ref, w1k_ref, w1v_ref, w2k_ref, w2v_ref, kc_ref, vc_ref):
    nb = rk_ref.shape[0]

    def one(r_ref, p_ref, w1_ref, w2_ref):
        r = r_ref[...].reshape(nb * N_CMP_PAD, r_ref.shape[2])
        top = _dot((r + p_ref[0:1, :]).astype(BF16), w1_ref[0])
        bot = _dot((r + p_ref[1:2, :]).astype(BF16), w1_ref[1])
        hid = top + pltpu.roll(bot, nb * N_CMP_PAD - 1, 0)
        act = jax.nn.gelu(hid).astype(BF16)
        return jnp.concatenate(
            [_dot(act[:, g * CMP_HIDDEN:(g + 1) * CMP_HIDDEN], w2_ref[...]) for g in range(NSA_GROUPS)], axis=1)

    k_out = one(rk_ref, pk_ref, w1k_ref, w2k_ref)
    v_out = one(rv_ref, pv_ref, w1v_ref, w2v_ref)
    for n in range(nb):
        rows = slice(n * N_CMP_PAD, (n + 1) * N_CMP_PAD)
        kc_ref[n] = k_out[rows].astype(BF16)
        vc_ref[n] = v_out[rows].T.astype(BF16)


def _compress(rk, rv, pk, pv, w1k, w1v, w2k, w2v):
    b = rk.shape[0]
    rw = rk.shape[2]
    nb = 2 if b % 2 == 0 else 1
    r_spec = pl.BlockSpec((nb, N_CMP_PAD, rw), lambda i: (i, 0, 0))
    o_spec = pl.BlockSpec((nb, N_CMP_PAD, NSA_KV_W), lambda i: (i, 0, 0))
    return pl.pallas_call(
        _compress_kernel,
        grid=(b // nb,),
        in_specs=[r_spec, r_spec, _const_spec(pk.shape), _const_spec(pv.shape),
                  _const_spec(w1k.shape), _const_spec(w1v.shape),
                  _const_spec(w2k.shape), _const_spec(w2v.shape)],
        out_specs=[o_spec, o_spec],
        out_shape=[jax.ShapeDtypeStruct((b, N_CMP_PAD, NSA_KV_W), BF16)] * 2,
        compiler_params=_params(("parallel",)),
        name="compress",
    )(rk, rv, pk, pv, w1k, w1v, w2k, w2v)


def _compress_weights(pos, w1):
    half = CMP_LEN // 2
    p = pos.reshape(2, half, 1, HEAD_DIM)
    p = jnp.broadcast_to(p, (2, half, NSA_GROUPS, HEAD_DIM)).reshape(2, half * NSA_KV_W)
    w = w1.reshape(2, half, HEAD_DIM, CMP_HIDDEN)
    eye = jnp.eye(NSA_GROUPS, dtype=w1.dtype)
    wbd = jnp.einsum("ajdm,gk->ajgdkm", w, eye).reshape(2, half * NSA_KV_W, NSA_GROUPS * CMP_HIDDEN)
    return p.astype(F32), wbd.astype(BF16)


def _memkv_kernel(m_ref, g_ref, w_ref, k_ref, v_ref):
    h = _rms(m_ref[...], g_ref[...]).astype(BF16)
    k_ref[...] = _dot(h, w_ref[:, :MEM_W]).astype(BF16)
    v_ref[...] = _dot(h, w_ref[:, MEM_W:]).astype(BF16)


def _memkv(mem2, g, w, tm=512):
    m = mem2.shape[0]
    tm = min(tm, m)
    o_spec = pl.BlockSpec((tm, MEM_W), lambda i: (i, 0))
    return pl.pallas_call(
        _memkv_kernel,
        grid=(m // tm,),
        in_specs=[pl.BlockSpec((tm, D_MODEL), lambda i: (i, 0)), _const_spec((1, D_MODEL)),
                  _const_spec((D_MODEL, 2 * MEM_W))],
        out_specs=[o_spec, o_spec],
        out_shape=[jax.ShapeDtypeStruct((m, MEM_W), BF16)] * 2,
        compiler_params=_params(("parallel",)),
        name="memkv",
    )(mem2, g, w)


def _expand_kernel(idx_ref, bias_ref, o_ref, *, head0, n_heads, heads_per_group):
    rows, cols = idx_ref.shape

    def body(i, carry):
        r = pl.multiple_of(i * SUBLANES, SUBLANES)
        for c0 in range(0, cols, TILE):
            idx = idx_ref[pl.ds(r, SUBLANES), c0:c0 + TILE]
            out = [jnp.full(idx.shape, NEG_INF, F32)] * n_heads
            for bkt in range(REL_BUCKETS):
                hit = idx == bkt
                out = [jnp.where(hit, bias_ref[bkt, head0 + h], out[h]) for h in range(n_heads)]
            for h in range(n_heads):
                col = (h % heads_per_group) * cols + c0
                o_ref[h // heads_per_group, pl.ds(r, SUBLANES), col:col + TILE] = out[h] * LOG2E
        return carry

    lax.fori_loop(0, rows // SUBLANES, body, 0)


def _expand(idx, rel_bias, head0, n_heads, heads_per_group=1):
    rows, cols = idx.shape
    return pl.pallas_call(
        functools.partial(_expand_kernel, head0=head0, n_heads=n_heads, heads_per_group=heads_per_group),
        in_specs=[pl.BlockSpec(memory_space=pltpu.VMEM), pl.BlockSpec(memory_space=pltpu.SMEM)],
        out_specs=pl.BlockSpec(memory_space=pltpu.VMEM),
        out_shape=jax.ShapeDtypeStruct((n_heads // heads_per_group, rows, heads_per_group * cols), F32),
        compiler_params=pltpu.CompilerParams(vmem_limit_bytes=VMEM_LIMIT),
        name="bias_expand",
    )(idx, rel_bias)


def _t5_bucket_np(dist):
    dist = np.maximum(dist, 0)
    max_exact = REL_BUCKETS // 2
    logd = np.log(np.maximum(dist, 1).astype(np.float32) / max_exact) / math.log(REL_MAX_DIST / max_exact)
    large = np.minimum(max_exact + (logd * (REL_BUCKETS - max_exact)).astype(np.int32), REL_BUCKETS - 1)
    return np.where(dist < max_exact, dist, large).astype(np.int32)


def _bucket_tables(s):
    j = np.arange(TILE)[:, None]
    i = np.arange(TILE)[None, :]
    assert TILE + 1 >= REL_MAX_DIST
    tiles = []
    for d in range(2):
        dist = d * TILE + i - j
        tiles.append(np.where(dist >= 0, _t5_bucket_np(dist), MASKED_BUCKET))
    dist1 = TILE + i - j
    win = np.where(dist1 < WINDOW, _t5_bucket_np(dist1), MASKED_BUCKET)
    n_cmp = (s - CMP_LEN) // CMP_STRIDE + 1
    assert n_cmp * CMP_STRIDE + CMP_LEN - 1 >= s or n_cmp == N_CMP_PAD
    rel = np.arange(N_CMP_PAD + (s - TILE) // CMP_STRIDE)[:, None] - (s - TILE) // CMP_STRIDE
    dist_c = i - (rel * CMP_STRIDE + CMP_LEN - 1)
    cmp_idx = np.where(dist_c >= 0, _t5_bucket_np(dist_c), MASKED_BUCKET)
    as_i32 = lambda a: jnp.asarray(a.astype(np.int32))
    return as_i32(np.concatenate(tiles, axis=0)), as_i32(win), as_i32(cmp_idx)


def _overlap_table(s):
    n_cmp = (s - CMP_LEN) // CMP_STRIDE + 1
    n_sel = s // SEL_BLOCK
    cs = np.arange(n_cmp) * CMP_STRIDE
    ss = np.arange(n_sel) * SEL_BLOCK
    ov = np.clip(np.minimum(cs[:, None] + CMP_LEN, ss[None, :] + SEL_BLOCK)
                 - np.maximum(cs[:, None], ss[None, :]), 0, None).astype(np.float32) / CMP_LEN
    ovt = np.zeros((n_sel, N_CMP_PAD), np.float32)
    ovt[:, :n_cmp] = ov.T
    return jnp.asarray(ovt, BF16)


def _store_v_aug(vt_sc, idx, vt):
    ones = jnp.ones((BF16_ROWS, vt.shape[1]), BF16)
    vt_sc[idx] = jnp.concatenate([vt.astype(BF16), ones], axis=0)


def _lane_cat(xs):
    return jnp.concatenate(xs, axis=1)


def _query_halves(x):
    n = x.shape[-1] // TILE
    first = _lane_cat([x[:, k * TILE:k * TILE + HALF] for k in range(n)])
    second = _lane_cat([x[:, k * TILE + HALF:(k + 1) * TILE] for k in range(n)])
    return first, second


def _join_query_halves(first, second):
    n = first.shape[-1] // HALF
    return _lane_cat([part for k in range(n)
                      for part in (first[:, k * HALF:(k + 1) * HALF], second[:, k * HALF:(k + 1) * HALF])])


def _triangle_scores(k, q, table, causal):
    q_first, q_second = _query_halves(q)
    lo, hi = (0, HALF), (HALF, TILE)
    if causal:
        return _dot(k(*lo), q) + table(*lo), _dot(k(*hi), q_second) + _query_halves(table(*hi))[1]
    return _dot(k(*hi), q) + table(*hi), _dot(k(*lo), q_first) + _query_halves(table(*lo))[0]


def _triangle_softmax(scores, pv_lo, pv_hi, causal):
    s_wide, s_narrow = scores
    pv_wide, pv_narrow = (pv_lo, pv_hi) if causal else (pv_hi, pv_lo)
    mw_first, mw_second = _query_halves(jnp.max(s_wide, axis=0, keepdims=True))
    m_narrow = jnp.max(s_narrow, axis=0, keepdims=True)
    if causal:
        m_narrow = jnp.maximum(m_narrow, mw_second)
        m = _join_query_halves(mw_first, m_narrow)
    else:
        m_narrow = jnp.maximum(m_narrow, mw_first)
        m = _join_query_halves(m_narrow, mw_second)
    aw_first, aw_second = _query_halves(pv_wide(jnp.exp2(s_wide - m).astype(BF16)))
    a_narrow = pv_narrow(jnp.exp2(s_narrow - m_narrow).astype(BF16))
    if causal:
        return m, _join_query_halves(aw_first, aw_second + a_narrow)
    return m, _join_query_halves(aw_first + a_narrow, aw_second)


def _flash_pipelined(own, streams, s_sc, m_ref, acc_ref, before_last=lambda: None):
    has_prev = jnp.where(own > 0, 1.0, 0.0).astype(F32)
    prev = jnp.maximum(own - 1, 0)
    n_far = jnp.maximum(own - 1, 0)

    def absorb(g, s, kt, c_row, w_row):
        u = jnp.max(s, axis=0, keepdims=True) + c_row
        m_old = m_ref[g]
        m_new = jnp.maximum(m_old, jnp.where(w_row > 0.0, u, NEG_INF))
        alpha = jnp.exp2(m_old - m_new)
        shift = jnp.maximum(m_new, u) - c_row
        p = jnp.exp2(s - shift).astype(BF16)
        acc_ref[g] = alpha * acc_ref[g] + w_row * streams[g]["pv"](kt)(p)
        m_ref[g] = m_new

    def absorb_slot(g, i):
        is_prev = i == 0
        kt = jnp.where(is_prev, prev, i - 1)
        c_row = jnp.where(is_prev, 0.0, streams[g]["c_far"])
        w_row = streams[g]["w"](kt) * jnp.where(is_prev, has_prev, 1.0)
        absorb(g, s_sc[g], kt, c_row, w_row)

    def body(i, carry):
        for g in reversed(range(len(streams))):
            nxt = streams[g]["far"](i)
            absorb_slot(g, i)
            s_sc[g] = nxt
        return carry

    lax.fori_loop(0, n_far, body, 0)
    before_last()
    for g in range(len(streams)):
        absorb_slot(g, n_far)


def _run_ahead_prime(jobs, depth=RUN_AHEAD):
    return [job[0]() for job in jobs[:depth]]


def _run_ahead(jobs, depth=RUN_AHEAD, pending=None):
    pending = _run_ahead_prime(jobs, depth) if pending is None else pending
    for k, (_, consume) in enumerate(jobs):
        if k + depth < len(jobs):
            pending.append(jobs[k + depth][0]())
        consume(pending.pop(0))


def _flash_start_jobs(g, stream, s_sc, m_ref, acc_ref):
    def init(scores):
        m_ref[g], acc_ref[g] = stream["own_softmax"](scores)

    def park(scores):
        s_sc[g] = scores

    return [(stream["own_scores"], init), (stream["prev"], park)]


def _softmax_av(s_list, pv_list):
    m = s_list[0].max(axis=0, keepdims=True)
    for s in s_list[1:]:
        m = jnp.maximum(m, s.max(axis=0, keepdims=True))
    acc = None
    for s, pv in zip(s_list, pv_list):
        part = pv(jnp.exp2(s - m).astype(BF16))
        acc = part if acc is None else acc + part
    return acc


def _normalize(acc):
    return acc[:HEAD_DIM] * (1.0 / acc[HEAD_DIM:HEAD_DIM + 1])


def _rank_before(score, n_cand):
    ranks = []
    for r0 in range(0, score.shape[0], SUBLANES):
        tile = score[r0:r0 + SUBLANES]
        blk = lax.broadcasted_iota(jnp.int32, tile.shape, 0) + r0
        rank = jnp.zeros(tile.shape, F32)
        for m in range(n_cand):
            row = score[m:m + 1, :]
            if m < r0:
                before = jnp.where(row >= tile, 1.0, 0.0)
            elif m >= r0 + SUBLANES:
                before = jnp.where(row > tile, 1.0, 0.0)
            else:
                before = jnp.where(blk > m, jnp.where(row >= tile, 1.0, 0.0), jnp.where(row > tile, 1.0, 0.0))
            rank = rank + before
        ranks.append(rank)
    return jnp.concatenate(ranks, axis=0)


def _nsa_steps(qi, maybe_first, q_ref, gn_ref, kc_ref, vct_ref, ks_ref, vs_ref, kw_ref, vw_ref,
               bct_ref, tt_ref, twt_ref, cfar_ref, ovt_ref, o_ref,
               vst_sc, vwt_sc, qa_sc, og_sc, acc_sc):
    nt = ks_ref.shape[1]
    n_sel = ovt_ref.shape[0]

    def transpose_v():
        for kt in range(nt):
            vs_t = vs_ref[0, kt].astype(F32).T
            vw_t = vw_ref[0, kt].astype(F32).T
            for g in range(NSA_GROUPS):
                _store_v_aug(vst_sc, (kt, g), vs_t[g * HEAD_DIM:(g + 1) * HEAD_DIM])
                _store_v_aug(vwt_sc, (kt, g), vw_t[g * HEAD_DIM:(g + 1) * HEAD_DIM])

    if maybe_first:
        pl.when(qi == 0)(transpose_v)

    pos = lax.broadcasted_iota(jnp.int32, (1, TILE), 1) + qi * TILE
    cur = pos // SEL_BLOCK
    has_cmp = pos >= CMP_LEN - 1
    blk = lax.broadcasted_iota(jnp.int32, (n_sel, TILE), 0)
    prev = jnp.maximum(qi - 1, 0)
    gates = gn_ref[...]

    gsls = [slice(g * HEAD_DIM, (g + 1) * HEAD_DIM) for g in range(NSA_GROUPS)]
    group_heads = [[g * NSA_HPG + j for j in range(NSA_HPG)] for g in range(NSA_GROUPS)]

    def gate(g, branch):
        return _lane_cat([gates[3 * h + branch:3 * h + branch + 1, :] for h in group_heads[g]])

    ones_row = jnp.ones((1, NSA_HPG * TILE), F32)

    def sel_stream(g):
        def qk(kt):
            return _dot(ks_ref[0, kt, :, g * 2 * HEAD_DIM:(g + 1) * 2 * HEAD_DIM], qa_sc[g])

        def own_scores():
            return _triangle_scores(
                lambda lo, hi: ks_ref[0, qi, lo:hi, g * 2 * HEAD_DIM:(g + 1) * 2 * HEAD_DIM], qa_sc[g],
                lambda lo, hi: tt_ref[g, 0, lo:hi, :], causal=True)

        def own_softmax(scores):
            return _triangle_softmax(
                scores, lambda pr: _dot(vst_sc[qi, g, :, 0:HALF], pr),
                lambda pr: _dot(vst_sc[qi, g, :, HALF:TILE], pr), causal=True)

        return dict(own_scores=own_scores, own_softmax=own_softmax, prev=lambda: qk(prev) + tt_ref[g, 1], far=qk,
                    c_far=cfar_ref[g], w=lambda kt: ones_row,
                    pv=lambda kt: (lambda pr: _dot(vst_sc[kt, g], pr)))

    def group_jobs(g):
        heads = group_heads[g]
        win = {}

        def v_half(kt, lo, hi):
            return lambda pr: _dot(vwt_sc[kt, g, :, lo:hi], pr)

        def window_own_scores():
            q4 = _lane_cat([q_ref[h * HEAD_DIM:(h + 1) * HEAD_DIM, :] for h in heads])
            qa_sc[g, 0:HEAD_DIM, :] = q4
            qa_sc[g, HEAD_DIM + n_sel:, :] = jnp.zeros((HEAD_DIM - n_sel, NSA_HPG * TILE), BF16)
            return _triangle_scores(lambda lo, hi: kw_ref[0, qi, lo:hi, gsls[g]], q4,
                                    lambda lo, hi: tt_ref[g, 0, lo:hi, :], causal=True)

        def window_own(scores):
            win["own"] = _triangle_softmax(scores, v_half(qi, 0, HALF), v_half(qi, HALF, TILE), causal=True)

        def window_prev_scores():
            return _triangle_scores(lambda lo, hi: kw_ref[0, prev, lo:hi, gsls[g]], qa_sc[g, 0:HEAD_DIM, :],
                                    lambda lo, hi: twt_ref[g, lo:hi, :], causal=False)

        def window_prev(scores):
            m_prev, acc_prev = _triangle_softmax(scores, v_half(prev, 0, HALF), v_half(prev, HALF, TILE), causal=False)
            m_prev = jnp.where(qi == 0, NEG_INF, m_prev)
            m_own, acc_own = win["own"]
            m_win = jnp.maximum(m_own, m_prev)
            acc_w = acc_own * jnp.exp2(m_own - m_win) + acc_prev * jnp.exp2(m_prev - m_win)
            og_sc[g] = gate(g, 2) * _normalize(acc_w)

        def compressed_scores():
            rows_per_tile = TILE // CMP_STRIDE
            c_rows = pl.ds(pl.multiple_of((nt - 1 - qi) * rows_per_tile, rows_per_tile), N_CMP_PAD)
            return (_dot(kc_ref[0, :, gsls[g]], qa_sc[g, 0:HEAD_DIM, :])
                    + _lane_cat([bct_ref[h, c_rows, :] for h in heads]))

        def compressed(s):
            e = jnp.exp2(s - jnp.max(s, axis=0, keepdims=True))
            p = e * jnp.where(_lane_cat([has_cmp] * NSA_HPG), 1.0 / jnp.sum(e, axis=0, keepdims=True), 0.0)
            psum = p[:, :TILE]
            for j in range(1, NSA_HPG):
                psum = psum + p[:, j * TILE:(j + 1) * TILE]
            og_sc[g] = og_sc[g] + gate(g, 0) * _dot(vct_ref[0, gsls[g], :], p.astype(BF16))
            win["psum"] = psum

        def importance():
            p_hi, p_lo = _split_bf16(win["psum"])
            return _dot(ovt_ref[...], p_hi) + _dot(ovt_ref[...], p_lo)

        def select(imp):
            forced = (blk == 0) | (blk == cur) | (blk == cur - 1)
            score = jnp.where(forced, FORCE_SCORE, jnp.where(blk <= cur, imp, NEG_INF))
            rank = _rank_before(score, n_sel)
            sel = jnp.where(rank < SEL_TOPN, jnp.where(score > NEG_INF / 2, 0.0, NEG_INF), NEG_INF)
            qa_sc[g, HEAD_DIM:HEAD_DIM + n_sel, :] = _lane_cat([sel.astype(BF16)] * NSA_HPG)

        return [(window_own_scores, window_own), (window_prev_scores, window_prev), (compressed_scores, compressed),
                (importance, select)]

    per_group = [group_jobs(g) for g in range(NSA_GROUPS)]
    assert NSA_GROUPS == 2 and RUN_AHEAD <= 2
    (a0, b0, c0, i0), (a1, b1, c1, i1) = per_group
    yield [a0, b0, c0, a1, b1, i0, c1], i1
    yield [sel_stream(g) for g in range(NSA_GROUPS)]

    for g in range(NSA_GROUPS):
        o = og_sc[g] + gate(g, 1) * _normalize(acc_sc[g])
        for j, h in enumerate(group_heads[g]):
            o_ref[h * HEAD_DIM:(h + 1) * HEAD_DIM, :] = o[:, j * TILE:(j + 1) * TILE].astype(BF16)


def _moba_steps(c, maybe_first, qm_ref, km_ref, vm_ref, qx_ref, mk_ref, mv_ref, tt_ref, cfar_ref, om_ref, ox_ref,
                vmt_sc, mvt_sc, kmean_sc, qbd_sc, sel_sc, acc_sc, slot):
    nt = km_ref.shape[1]
    hsls = [slice(h * HEAD_DIM, (h + 1) * HEAD_DIM) for h in range(MOBA_HEADS)]

    def per_sequence():
        kmean_sc[...] = jnp.zeros(kmean_sc.shape, F32)
        for n in range(nt):
            kmean_sc[n:n + 1, :] = jnp.sum(km_ref[0, n].astype(F32), axis=0, keepdims=True) * (1.0 / MOBA_BLOCK)
            vt = vm_ref[0, n].astype(F32).T
            for h in range(MOBA_HEADS):
                _store_v_aug(vmt_sc, (n, h), vt[hsls[h]])
        mvt = mv_ref[0].astype(F32).T
        for h in range(MOBA_HEADS):
            _store_v_aug(mvt_sc, h, mvt[hsls[h]])

    if maybe_first:
        pl.when(c == 0)(per_sequence)

    row_head = lax.broadcasted_iota(jnp.int32, (MOBA_W, TILE), 0) // HEAD_DIM

    def block_diag(q_ref):
        q_t = q_ref[...].astype(F32)
        return _lane_cat([jnp.where(row_head == h, q_t, 0.0) for h in range(MOBA_HEADS)]).astype(BF16)

    def per_head_pv(vts):
        return lambda pr: _lane_cat([_dot(vts(h), pr[:, h * TILE:(h + 1) * TILE]) for h in range(MOBA_HEADS)])

    def store_heads(o_t, out_ref):
        for h in range(MOBA_HEADS):
            out_ref[hsls[h], :] = o_t[:, h * TILE:(h + 1) * TILE].astype(BF16)

    n_rows = -(-nt // SUBLANES) * SUBLANES

    def gate_scores():
        qbd = block_diag(qm_ref)
        qbd_sc[...] = qbd
        km_hi, km_lo = _split_bf16(kmean_sc[...])
        return (_dot(km_hi, qbd) + _dot(km_lo, qbd))[:n_rows]

    def select(gate):
        blk = lax.broadcasted_iota(jnp.int32, gate.shape, 0)
        score = jnp.where(blk < c, gate, NEG_INF * Q_SCALE)
        rank = _rank_before(score, nt)
        sel_sc[0:n_rows, :] = jnp.where(rank < MOBA_TOPK, jnp.where(score > NEG_INF * Q_SCALE / 2, 1.0, 0.0), 0.0)

    qk = lambda n: _dot(km_ref[0, n], qbd_sc[...])
    def own_pv(lo, hi):
        def pv(pr):
            width = pr.shape[1] // MOBA_HEADS
            return _lane_cat([_dot(vmt_sc[c, h, :, lo:hi], pr[:, h * width:(h + 1) * width])
                              for h in range(MOBA_HEADS)])
        return pv

    def own_scores():
        return _triangle_scores(lambda lo, hi: km_ref[0, c, lo:hi, :], qbd_sc[...],
                                lambda lo, hi: tt_ref[0, lo:hi, :], causal=True)

    def own_softmax(scores):
        return _triangle_softmax(scores, own_pv(0, HALF), own_pv(HALF, TILE), causal=True)

    stream = dict(own_scores=own_scores, own_softmax=own_softmax,
                  prev=lambda: qk(jnp.maximum(c - 1, 0)) + tt_ref[1], far=qk,
                  c_far=cfar_ref[...], w=lambda n: sel_sc[pl.ds(n, 1), :],
                  pv=lambda n: per_head_pv(lambda h: vmt_sc[n, h]))
    def memory(s):
        store_heads(_normalize(_softmax_av([s], [per_head_pv(lambda h: mvt_sc[h])])), ox_ref)

    yield [(gate_scores, select), (lambda: _dot(mk_ref[0], block_diag(qx_ref)), memory)]
    yield [stream]
    store_heads(_normalize(acc_sc[slot]), om_ref)


N_STREAMS = NSA_GROUPS + 1
TILES_PER_STEP = 2
NSA_ROW_INPUTS = (0, 1)
MOBA_ROW_INPUTS = (0, 3)


def _attention_kernel(*refs, n_nsa_in, n_moba_in, n_nsa_scratch, n_moba_scratch):
    nsa_in, refs = refs[:n_nsa_in], refs[n_nsa_in:]
    moba_in, refs = refs[:n_moba_in], refs[n_moba_in:]
    outs, refs = refs[:3], refs[3:]
    nsa_sc, refs = refs[:n_nsa_scratch], refs[n_nsa_scratch:]
    moba_sc, refs = refs[:n_moba_scratch], refs[n_moba_scratch:]
    s_sc, m_sc, acc_sc = refs
    def open_tile(t):
        qi = pl.program_id(1) * TILES_PER_STEP + t
        cols = lambda ref: ref.at[:, t * TILE:(t + 1) * TILE]
        o_nsa, o_moba, o_mem = [cols(o) for o in outs]
        nsa_refs = [cols(r) if k in NSA_ROW_INPUTS else r for k, r in enumerate(nsa_in)]
        moba_refs = [cols(r) if k in MOBA_ROW_INPUTS else r for k, r in enumerate(moba_in)]
        nsa = _nsa_steps(qi, t == 0, *nsa_refs, o_nsa, *nsa_sc, acc_sc)
        moba = _moba_steps(qi, t == 0, *moba_refs, o_moba, o_mem, *moba_sc, acc_sc, NSA_GROUPS)
        nsa_jobs, last_selection = next(nsa)
        moba_jobs = next(moba)
        streams = next(nsa) + next(moba)
        starts = [_flash_start_jobs(g, stream, s_sc, m_sc, acc_sc) for g, stream in enumerate(streams)]
        jobs = nsa_jobs + moba_jobs + [last_selection] + starts[NSA_GROUPS] + starts[1] + starts[0]
        return dict(qi=qi, jobs=jobs, streams=streams, steps=(nsa, moba), pending=_run_ahead_prime(jobs))

    tiles = {0: open_tile(0)}
    for t in range(TILES_PER_STEP):
        tile = tiles.pop(t)
        _run_ahead(tile["jobs"], pending=tile["pending"])

        def open_next(t=t):
            if t + 1 < TILES_PER_STEP:
                tiles[t + 1] = open_tile(t + 1)

        _flash_pipelined(tile["qi"], tile["streams"], s_sc, m_sc, acc_sc, before_last=open_next)
        for steps in tile["steps"]:
            for _ in steps:
                pass


def _attention(b, s, qn, gn, kc, vct, ks, vs, kw, vw, bias_cmp, t_nsa, t_win, c_far_nsa, ovt,
               qm, km, vm, qx, mk, mv, t_moba, c_far_moba):
    nt = s // TILE
    mem_len = mk.shape[0] // b
    assert MOBA_TOPK <= nt - 1 and nt <= BF16_ROWS
    n_lanes = NSA_HPG * TILE
    assert MOBA_HEADS * TILE == n_lanes
    assert nt % TILES_PER_STEP == 0
    steps = nt // TILES_PER_STEP
    row_spec = lambda w: pl.BlockSpec((w, TILES_PER_STEP * TILE), lambda i, j: (0, i * steps + j))
    seq_spec = lambda w: pl.BlockSpec((1, nt, TILE, w), lambda i, j: (i, 0, 0, 0))
    per_batch = lambda rows, w: pl.BlockSpec((1, rows, w), lambda i, j: (i, 0, 0))
    tiles = lambda a: a.reshape(b, nt, TILE, a.shape[-1])
    nsa_in = [(qn, row_spec(NSA_Q_W)), (gn, row_spec(GATE_PAD)),
              (kc, per_batch(N_CMP_PAD, NSA_KV_W)), (vct, per_batch(N_CMP_PAD, NSA_KV_W)),
              (tiles(ks), seq_spec(KS_AUG_W)), (tiles(vs), seq_spec(NSA_KV_W)),
              (tiles(kw), seq_spec(NSA_KV_W)), (tiles(vw), seq_spec(NSA_KV_W)),
              (bias_cmp, _const_spec(bias_cmp.shape)), (t_nsa, _const_spec(t_nsa.shape)),
              (t_win, _const_spec(t_win.shape)), (c_far_nsa, _const_spec(c_far_nsa.shape)),
              (ovt, _const_spec(ovt.shape))]
    moba_in = [(qm, row_spec(MOBA_W)), (tiles(km), seq_spec(MOBA_W)), (tiles(vm), seq_spec(MOBA_W)),
               (qx, row_spec(MEM_W)),
               (mk.reshape(b, mem_len, MEM_W), per_batch(mem_len, MEM_W)),
               (mv.reshape(b, mem_len, MEM_W), per_batch(mem_len, MEM_W)),
               (t_moba, _const_spec(t_moba.shape)), (c_far_moba, _const_spec(c_far_moba.shape))]
    nsa_scratch =[pltpu.VMEM((nt, NSA_GROUPS, V_AUG, TILE), BF16),
                   pltpu.VMEM((nt, NSA_GROUPS, V_AUG, TILE), BF16),
                   pltpu.VMEM((NSA_GROUPS, 2 * HEAD_DIM, n_lanes), BF16),
                   pltpu.VMEM((NSA_GROUPS, HEAD_DIM, n_lanes), F32)]
    moba_scratch = [pltpu.VMEM((nt, MOBA_HEADS, V_AUG, TILE), BF16),
                    pltpu.VMEM((MEM_HEADS, V_AUG, mem_len), BF16),
                    pltpu.VMEM((BF16_ROWS, MOBA_W), F32),
                    pltpu.VMEM((MOBA_W, n_lanes), BF16),
                    pltpu.VMEM((BF16_ROWS, n_lanes), F32)]
    shared_scratch =[pltpu.VMEM((N_STREAMS, TILE, n_lanes), F32),
                      pltpu.VMEM((N_STREAMS, 1, n_lanes), F32),
                      pltpu.VMEM((N_STREAMS, V_AUG, n_lanes), F32)]
    inputs = nsa_in + moba_in
    return pl.pallas_call(
        functools.partial(_attention_kernel, n_nsa_in=len(nsa_in), n_moba_in=len(moba_in),
                          n_nsa_scratch=len(nsa_scratch), n_moba_scratch=len(moba_scratch)),
        grid=(b, steps),
        in_specs=[spec for _, spec in inputs],
        out_specs=[row_spec(NSA_Q_W), row_spec(MOBA_W), row_spec(MEM_W)],
        out_shape=[jax.ShapeDtypeStruct((w, b * s), BF16) for w in (NSA_Q_W, MOBA_W, MEM_W)],
        scratch_shapes=nsa_scratch + moba_scratch + shared_scratch,
        compiler_params=_params(("arbitrary", "arbitrary")),
        name="attention",
    )(*[a for a, _ in inputs])


MIX_CHUNK = MXU_COLS


def _mix_kernel(x_ref, on_ref, om_ref, ox_ref, g_pre_ref, g_post_ref, wg_ref, wn_ref, wm_ref, wx_ref,
                wo_ref, o_ref, merged_sc):
    tm = merged_sc.shape[1]
    for t in range(merged_sc.shape[0]):
        tile = slice(t * tm, (t + 1) * tm)
        x = x_ref[tile, :]
        h = _rms(x, g_pre_ref[...]).astype(BF16)
        rows = lambda o_ref: o_ref[:, tile].astype(F32).T.astype(BF16)
        branches = ((rows(on_ref), wn_ref), (rows(om_ref), wm_ref), (rows(ox_ref), wx_ref))

        def chunk_job(c0):
            cols = slice(c0, c0 + MIX_CHUNK)

            def produce():
                gates = [_dot(h, wg_ref[:, i * D_MODEL + c0:i * D_MODEL + c0 + MIX_CHUNK])
                         for i in range(len(branches))]
                return list(zip(gates, [_dot(o, w_ref[:, cols]) for o, w_ref in branches]))

            def consume(pairs):
                merged_sc[t, :, cols] = sum(jax.nn.sigmoid(g) * y for g, y in pairs).astype(BF16)

            return produce, consume

        _run_ahead([chunk_job(c0) for c0 in range(0, D_MODEL, MIX_CHUNK)], depth=1)
        y = _dot(merged_sc[t], wo_ref[...])
        o_ref[tile, :] = x + _rms(y, g_post_ref[...])


def _mix(x2, o_nsa, o_moba, o_mem, g_pre, g_post, w_gates, w_nsa_o, w_moba_o, w_mem_o, w_mix_out, tm=512,
         tiles_per_step=2):
    m = x2.shape[0]
    rows_per_step = tm * tiles_per_step
    assert m % rows_per_step == 0
    row = lambda w: pl.BlockSpec((rows_per_step, w), lambda i: (i, 0))
    col = lambda w: pl.BlockSpec((w, rows_per_step), lambda i: (0, i))
    return pl.pallas_call(
        _mix_kernel,
        grid=(m // rows_per_step,),
        in_specs=[row(D_MODEL), col(NSA_Q_W), col(MOBA_W), col(MEM_W),
                  _const_spec((1, D_MODEL)), _const_spec((1, D_MODEL)),
                  _const_spec(w_gates.shape), _const_spec(w_nsa_o.shape), _const_spec(w_moba_o.shape),
                  _const_spec(w_mem_o.shape), _const_spec(w_mix_out.shape)],
        out_specs=row(D_MODEL),
        out_shape=jax.ShapeDtypeStruct((m, D_MODEL), F32),
        scratch_shapes=[pltpu.VMEM((tiles_per_step, tm, D_MODEL), BF16)],
        compiler_params=_params(("parallel",)),
        name="mix",
    )(x2, o_nsa, o_moba, o_mem, g_pre, g_post, w_gates, w_nsa_o, w_moba_o, w_mem_o, w_mix_out)


FFN_CHUNK = 256


def _ffn_kernel(x_ref, g_pre_ref, g_post_ref, wg_ref, wu_ref, wd_ref, o_ref, a_sc):
    tm = a_sc.shape[1]
    d_ff = wg_ref.shape[1]
    for t in range(a_sc.shape[0]):
        rows = slice(t * tm, (t + 1) * tm)
        x = x_ref[rows, :]
        h = _rms(x, g_pre_ref[...]).astype(BF16)
        for j in range(d_ff // FFN_CHUNK):
            sl = slice(j * FFN_CHUNK, (j + 1) * FFN_CHUNK)
            a_sc[t, :, sl] = (jax.nn.silu(_dot(h, wg_ref[:, sl])) * _dot(h, wu_ref[:, sl])).astype(BF16)
        f = _dot(a_sc[t], wd_ref[...])
        o_ref[rows, :] = x + _rms(f, g_post_ref[...])


def _ffn(x2, g_pre, g_post, wg, wu, wd, tm=512, tiles_per_step=2):
    m = x2.shape[0]
    d_ff = wg.shape[1]
    rows = tm * tiles_per_step
    assert m % rows == 0
    return pl.pallas_call(
        _ffn_kernel,
        grid=(m // rows,),
        in_specs=[pl.BlockSpec((rows, D_MODEL), lambda i: (i, 0)),
                  _const_spec((1, D_MODEL)), _const_spec((1, D_MODEL)),
                  _const_spec(wg.shape), _const_spec(wu.shape), _const_spec(wd.shape)],
        out_specs=pl.BlockSpec((rows, D_MODEL), lambda i: (i, 0)),
        out_shape=jax.ShapeDtypeStruct((m, D_MODEL), F32),
        scratch_shapes=[pltpu.VMEM((tiles_per_step, tm, d_ff), BF16)],
        compiler_params=_params(("parallel",)),
        name="ffn",
    )(x2, g_pre, g_post, wg, wu, wd)


def kernel(x, mem, rel_bias, pre_mix_g, mem_norm_g, post_mix_g, w_in, cmp_pos_k, cmp_w1_k, cmp_w2_k, cmp_pos_v, cmp_w1_v, cmp_w2_v, w_mem_kv, w_nsa_o, w_moba_o, w_mem_o, w_mix_out, pre_ffn_g, post_ffn_g, w_ffn_gate, w_ffn_up, w_ffn_down):
    b, s, d_model = x.shape
    depth = w_in.shape[0]
    assert d_model == D_MODEL and s % TILE == 0 and TILE == MOBA_BLOCK == WINDOW
    assert (s - CMP_LEN) // CMP_STRIDE + 1 < N_CMP_PAD and (s // SEL_BLOCK) % SUBLANES == 0 and s // SEL_BLOCK <= HEAD_DIM
    assert w_in.shape[2] == ATT_W + 3 * D_MODEL and rel_bias.shape == (REL_BUCKETS, N_BIAS_HEADS)

    tile_idx, win_idx, cmp_idx = _bucket_tables(s)
    rel_bias = rel_bias.astype(F32)
    t_nsa = _expand(tile_idx, rel_bias, 0, NSA_HEADS, NSA_HPG).reshape(NSA_GROUPS, 2, TILE, NSA_HPG * TILE)
    t_moba = _expand(tile_idx, rel_bias, NSA_HEADS, MOBA_HEADS, MOBA_HEADS).reshape(2, TILE, MOBA_HEADS * TILE)
    t_win = _expand(win_idx, rel_bias, 0, NSA_HEADS, NSA_HPG)
    b_cmp = _expand(cmp_idx, rel_bias, 0, NSA_HEADS)
    c_far = jnp.repeat(rel_bias[REL_BUCKETS - 1] * LOG2E, TILE)
    c_far_nsa = c_far[:NSA_HEADS * TILE].reshape(NSA_GROUPS, 1, NSA_HPG * TILE)
    c_far_moba = c_far[NSA_HEADS * TILE:].reshape(1, MOBA_HEADS * TILE)
    ovt = _overlap_table(s)
    sel_cols = np.zeros((s, HEAD_DIM), np.float32)
    sel_cols[np.arange(s), np.arange(s) // SEL_BLOCK] = 1.0
    sel_cols = jnp.asarray(sel_cols, BF16)
    gate_lo = NSA_Q_W + 6 * NSA_KV_W
    rows_per_chunk = CMP_STRIDE * NSA_KV_W

    x2 = x.reshape(b * s, D_MODEL)
    mem2 = mem.reshape(-1, D_MODEL)
    for l in range(depth):
        w_att = jnp.concatenate(
            [w_in[l, :, :gate_lo + NSA_GATE_W],
             jnp.zeros((D_MODEL, GATE_PAD - NSA_GATE_W), w_in.dtype),
             w_in[l, :, gate_lo + NSA_GATE_W:ATT_W]], axis=1).astype(BF16)
        w_gates = w_in[l, :, ATT_W:].astype(BF16)
        row = lambda v: v[l].reshape(1, D_MODEL)

        qn, kc_raw, vc_raw, ks, vs, kw, vw, gn, qm, km, vm, qx = _inproj(x2, row(pre_mix_g), w_att, sel_cols)

        pk, w1k = _compress_weights(cmp_pos_k[l], cmp_w1_k[l])
        pv, w1v = _compress_weights(cmp_pos_v[l], cmp_w1_v[l])
        kc, vct = _compress(kc_raw.reshape(b, s // CMP_STRIDE, rows_per_chunk),
                            vc_raw.reshape(b, s // CMP_STRIDE, rows_per_chunk),
                            pk, pv, w1k, w1v, cmp_w2_k[l].astype(BF16), cmp_w2_v[l].astype(BF16))

        mk, mv = _memkv(mem2, row(mem_norm_g), w_mem_kv[l].astype(BF16))

        o_nsa, o_moba, o_mem = _attention(b, s, qn, gn, kc, vct, ks, vs, kw, vw, b_cmp, t_nsa, t_win, c_far_nsa, ovt,
                                          qm, km, vm, qx, mk, mv, t_moba, c_far_moba)

        x2 = _mix(x2, o_nsa, o_moba, o_mem, row(pre_mix_g), row(post_mix_g), w_gates,
                  w_nsa_o[l].astype(BF16), w_moba_o[l].astype(BF16), w_mem_o[l].astype(BF16),
                  w_mix_out[l].astype(BF16))
        x2 = _ffn(x2, row(pre_ffn_g), row(post_ffn_g), w_ffn_gate[l].astype(BF16),
                  w_ffn_up[l].astype(BF16), w_ffn_down[l].astype(BF16))
    return x2.reshape(b, s, D_MODEL)
```
